```python
import jax, jax.numpy as jnp
from jax import lax
import numpy as np

D_MODEL = 2048
BATCH = 8
SEQ = 8192
DEPTH = 1

GDN_HEADS = 8
GDN_DK = 128
GDN_DV = 128
GDN_CONV = 4
GDN_CHUNK = 64
FOX_HEADS = 8
FOX_DH = 128
FOX_BLOCK = 128
MEM_LEN = 256
MEM_HEADS = 4
MEM_DH = 256
D_FF = 4 * D_MODEL
N_BRANCH = 3
EPS = 1e-6

GDN_QK = GDN_HEADS * GDN_DK
GDN_V = GDN_HEADS * GDN_DV
GDN_QKV = 2 * GDN_QK + GDN_V
FOX_W = FOX_HEADS * FOX_DH
MEM_W = MEM_HEADS * MEM_DH
IN_SPLITS = (GDN_QKV, GDN_V, GDN_HEADS, GDN_HEADS, FOX_W, FOX_W, FOX_W, FOX_HEADS, MEM_W, N_BRANCH * D_MODEL)
D_IN = 2 * GDN_QK + 2 * GDN_V + 2 * GDN_HEADS + 3 * FOX_W + FOX_HEADS + MEM_W + N_BRANCH * D_MODEL

kernel_name = "hybrid_gdn_fox_memory_block"


def rms_norm(x, g):
    xf = x.astype(jnp.float32)
    y = xf * lax.rsqrt(jnp.mean(xf * xf, axis=-1, keepdims=True) + EPS)
    return (y * g.astype(jnp.float32)).astype(x.dtype)


def l2_norm(x):
    return x * lax.rsqrt(jnp.sum(x * x, axis=-1, keepdims=True) + EPS)


def to_heads(t, n_heads):
    b, s, _ = t.shape
    return t.reshape(b, s, n_heads, -1).transpose(0, 2, 1, 3)


def causal_conv_silu(x, w):
    k_w = w.shape[0]
    s = x.shape[1]
    xp = jnp.pad(x, ((0, 0), (k_w - 1, 0), (0, 0)))
    y = xp[:, 0:s] * w[0]
    for i in range(1, k_w):
        y = y + xp[:, i:i + s] * w[i]
    return jax.nn.silu(y)


def gated_delta_rule(q, k, v, g, beta):
    b, h, s, dk = q.shape
    dv = v.shape[-1]
    c = GDN_CHUNK
    n = s // c
    q = q.reshape(b, h, n, c, dk)
    k = k.reshape(b, h, n, c, dk)
    v = v.reshape(b, h, n, c, dv)
    beta = beta.reshape(b, h, n, c)
    gam = jnp.cumsum(g.reshape(b, h, n, c), axis=-1)
    idx = jnp.arange(c)
    strict = idx[:, None] > idx[None, :]
    incl = idx[:, None] >= idx[None, :]
    diff = gam[..., :, None] - gam[..., None, :]
    dec_strict = jnp.where(strict, jnp.exp(jnp.where(strict, diff, 0.0)), 0.0)
    dec_incl = jnp.where(incl, jnp.exp(jnp.where(incl, diff, 0.0)), 0.0)
    m_low = beta[..., :, None] * jnp.einsum('bhnid,bhnjd->bhnij', k, k) * dec_strict
    a_mat = jnp.eye(c, dtype=jnp.float32) + m_low
    rhs = jnp.concatenate([(beta * jnp.exp(gam))[..., None] * k, beta[..., None] * v], axis=-1)
    sol = lax.linalg.triangular_solve(a_mat, rhs, left_side=True, lower=True, unit_diagonal=True)
    w_c, u_c = sol[..., :dk], sol[..., dk:]
    qk = jnp.einsum('bhnid,bhnjd->bhnij', q, k) * dec_incl
    q_dec = q * jnp.exp(gam)[..., None]
    k_dec = k * jnp.exp(gam[..., -1:] - gam)[..., None]
    chunk_dec = jnp.exp(gam[..., -1])

    def step(state, xs):
        w_i, u_i, qk_i, qd_i, kd_i, cd_i = xs
        u = u_i - jnp.einsum('bhid,bhde->bhie', w_i, state)
        o = jnp.einsum('bhid,bhde->bhie', qd_i, state) + jnp.einsum('bhij,bhje->bhie', qk_i, u)
        state = cd_i[..., None, None] * state + jnp.einsum('bhid,bhie->bhde', kd_i, u)
        return state, o

    xs = tuple(jnp.moveaxis(t, 2, 0) for t in (w_c, u_c, qk, q_dec, k_dec, chunk_dec))
    _, o = lax.scan(step, jnp.zeros((b, h, dk, dv), jnp.float32), xs)
    return jnp.moveaxis(o, 0, 2).reshape(b, h, s, dv)


def forgetting_attention(q, k, v, log_f):
    _, _, s, d = q.shape
    cum = jnp.cumsum(log_f, axis=-1)
    scale = d ** -0.5
    outs = []
    for start in range(0, s, FOX_BLOCK):
        end = start + FOX_BLOCK
        logits = jnp.einsum('bhqd,bhkd->bhqk', q[:, :, start:end], k[:, :, :end]).astype(jnp.float32) * scale
        logits = logits + cum[:, :, start:end, None] - cum[:, :, None, :end]
        mask = (start + jnp.arange(FOX_BLOCK))[:, None] >= jnp.arange(end)[None, :]
        p = jax.nn.softmax(jnp.where(mask, logits, -jnp.inf), axis=-1)
        outs.append(jnp.einsum('bhqk,bhkd->bhqd', p.astype(v.dtype), v[:, :, :end]))
    return jnp.concatenate(outs, axis=2)


def _fwd_setup_inputs(seed: int = 0) -> dict:
    key = jax.random.key(seed)
    ks = jax.random.split(key, 24)
    L, D = DEPTH, D_MODEL
    nrm = lambda k, shape, fan_in: jax.random.normal(k, shape, jnp.float32) * (fan_in ** -0.5)
    gain = lambda k, shape: 1.0 + 0.02 * jax.random.normal(k, shape, jnp.float32)
    a_log = jnp.log(jax.random.uniform(ks[5], (L, GDN_HEADS), jnp.float32, 1.0, 16.0))
    dt = jnp.exp(jax.random.uniform(ks[6], (L, GDN_HEADS), jnp.float32, np.log(1e-3), np.log(1e-1)))
    dt_bias = dt + jnp.log(-jnp.expm1(-dt))
    return {
        "x": jax.random.normal(ks[0], (BATCH, SEQ, D), jnp.float32),
        "mem": jax.random.normal(ks[1], (BATCH, MEM_LEN, D), jnp.float32),
        "g_mix": gain(ks[2], (L, D)),
        "w_in": nrm(ks[3], (L, D, D_IN), D),
        "conv_w": nrm(ks[4], (L, GDN_CONV, GDN_QKV), GDN_CONV),
        "a_log": a_log,
        "dt_bias": dt_bias,
        "gdn_norm_g": gain(ks[7], (L, GDN_DV)),
        "fox_b_f": jax.random.uniform(ks[8], (L, FOX_HEADS), jnp.float32, 1.0, 4.0),
        "fox_q_norm": gain(ks[9], (L, FOX_DH)),
        "fox_k_norm": gain(ks[10], (L, FOX_DH)),
        "g_mem": gain(ks[11], (L, D)),
        "w_mem_kv": nrm(ks[12], (L, D, 2 * MEM_W), D),
        "mem_q_norm": gain(ks[13], (L, MEM_DH)),
        "mem_k_norm": gain(ks[14], (L, MEM_DH)),
        "w_up_gdn": nrm(ks[15], (L, GDN_V, D), GDN_V),
        "w_up_fox": nrm(ks[16], (L, FOX_W, D), FOX_W),
        "w_up_mem": nrm(ks[17], (L, MEM_W, D), MEM_W),
        "w_out": nrm(ks[18], (L, D, D), D),
        "g_mlp": gain(ks[19], (L, D)),
        "w_ff1": nrm(ks[20], (L, D, D_FF), D),
        "w_ff2": nrm(ks[21], (L, D_FF, D), D_FF),
    }


def _fwd_reference(x, mem, g_mix, w_in, conv_w, a_log, dt_bias, gdn_norm_g, fox_b_f, fox_q_norm, fox_k_norm,
              g_mem, w_mem_kv, mem_q_norm, mem_k_norm, w_up_gdn, w_up_fox, w_up_mem, w_out, g_mlp, w_ff1, w_ff2):
    b, s, _ = x.shape
    splits = np.cumsum(IN_SPLITS)[:-1].tolist()
    f32 = jnp.float32
    for l in range(DEPTH):
        h = rms_norm(x, g_mix[l])
        proj = h @ w_in[l]
        qkv_a, z_a, b_a, a_a, q_b, k_b, v_b, f_b, q_m, gates = jnp.split(proj, splits, axis=-1)

        qkv_a = causal_conv_silu(qkv_a, conv_w[l])
        q_a, k_a, v_a = jnp.split(qkv_a, [GDN_QK, 2 * GDN_QK], axis=-1)
        q_a = l2_norm(to_heads(q_a, GDN_HEADS).astype(f32)) * (GDN_DK ** -0.5)
        k_a = l2_norm(to_heads(k_a, GDN_HEADS).astype(f32))
        v_a = to_heads(v_a, GDN_HEADS).astype(f32)
        beta = jax.nn.sigmoid(b_a.astype(f32)).transpose(0, 2, 1)
        g_dec = (-jnp.exp(a_log[l].astype(f32)) * jax.nn.softplus(a_a.astype(f32) + dt_bias[l].astype(f32))).transpose(0, 2, 1)
        o_a = gated_delta_rule(q_a, k_a, v_a, g_dec, beta).transpose(0, 2, 1, 3)
        o_a = rms_norm(o_a, gdn_norm_g[l]) * jax.nn.silu(z_a.astype(f32).reshape(b, s, GDN_HEADS, GDN_DV))
        o_a = o_a.reshape(b, s, GDN_V).astype(x.dtype)

        q_bh = rms_norm(to_heads(q_b, FOX_HEADS), fox_q_norm[l])
        k_bh = rms_norm(to_heads(k_b, FOX_HEADS), fox_k_norm[l])
        v_bh = to_heads(v_b, FOX_HEADS)
        log_f = jax.nn.log_sigmoid(f_b.astype(f32) + fox_b_f[l].astype(f32)).transpose(0, 2, 1)
        o_b = forgetting_attention(q_bh, k_bh, v_bh, log_f)
        o_b = o_b.transpose(0, 2, 1, 3).reshape(b, s, FOX_W)

        kv_m = rms_norm(mem, g_mem[l]) @ w_mem_kv[l]
        k_m, v_m = jnp.split(kv_m, 2, axis=-1)
        q_mh = rms_norm(to_heads(q_m, MEM_HEADS), mem_q_norm[l])
        k_mh = rms_norm(to_heads(k_m, MEM_HEADS), mem_k_norm[l])
        v_mh = to_heads(v_m, MEM_HEADS)
        logits_m = jnp.einsum('bhqd,bhkd->bhqk', q_mh, k_mh).astype(f32) * (MEM_DH ** -0.5)
        p_m = jax.nn.softmax(logits_m, axis=-1).astype(v_mh.dtype)
        o_m = jnp.einsum('bhqk,bhkd->bhqd', p_m, v_mh).transpose(0, 2, 1, 3).reshape(b, s, MEM_W)

        gate_a, gate_b, gate_m = jnp.split(jax.nn.sigmoid(gates), N_BRANCH, axis=-1)
        y = gate_a * (o_a @ w_up_gdn[l]) + gate_b * (o_b @ w_up_fox[l]) + gate_m * (o_m @ w_up_mem[l])
        x = x + y @ w_out[l]

        h2 = rms_norm(x, g_mlp[l])
        x = x + jnp.square(jax.nn.relu(h2 @ w_ff1[l])) @ w_ff2[l]
    return x


import jax as _jax
import jax.numpy as _jnp

TWIN_FORMAT = 'train_step'
FWD_PARAMS = ['x', 'mem', 'g_mix', 'w_in', 'conv_w', 'a_log', 'dt_bias', 'gdn_norm_g', 'fox_b_f', 'fox_q_norm', 'fox_k_norm', 'g_mem', 'w_mem_kv', 'mem_q_norm', 'mem_k_norm', 'w_up_gdn', 'w_up_fox', 'w_up_mem', 'w_out', 'g_mlp', 'w_ff1', 'w_ff2']
TWIN_WEIGHTS = ['g_mix', 'w_in', 'conv_w', 'a_log', 'dt_bias', 'gdn_norm_g', 'fox_b_f', 'fox_q_norm', 'fox_k_norm', 'g_mem', 'w_mem_kv', 'mem_q_norm', 'mem_k_norm', 'w_up_gdn', 'w_up_fox', 'w_up_mem', 'w_out', 'g_mlp', 'w_ff1', 'w_ff2']
TWIN_DIFF_INPUT = 'x'
TWIN_INPUTS = ['x', 'mem', 'g_mix', 'w_in', 'conv_w', 'a_log', 'dt_bias', 'gdn_norm_g', 'fox_b_f', 'fox_q_norm', 'fox_k_norm', 'g_mem', 'w_mem_kv', 'mem_q_norm', 'mem_k_norm', 'w_up_gdn', 'w_up_fox', 'w_up_mem', 'w_out', 'g_mlp', 'w_ff1', 'w_ff2', 'loss_target', 'm_g_mix', 'm_w_in', 'm_conv_w', 'm_a_log', 'm_dt_bias', 'm_gdn_norm_g', 'm_fox_b_f', 'm_fox_q_norm', 'm_fox_k_norm', 'm_g_mem', 'm_w_mem_kv', 'm_mem_q_norm', 'm_mem_k_norm', 'm_w_up_gdn', 'm_w_up_fox', 'm_w_up_mem', 'm_w_out', 'm_g_mlp', 'm_w_ff1', 'm_w_ff2', 'v_g_mix', 'v_w_in', 'v_conv_w', 'v_a_log', 'v_dt_bias', 'v_gdn_norm_g', 'v_fox_b_f', 'v_fox_q_norm', 'v_fox_k_norm', 'v_g_mem', 'v_w_mem_kv', 'v_mem_q_norm', 'v_mem_k_norm', 'v_w_up_gdn', 'v_w_up_fox', 'v_w_up_mem', 'v_w_out', 'v_g_mlp', 'v_w_ff1', 'v_w_ff2']
TWIN_OUTPUTS = ['loss', 'grad_x', 'grad_g_mix', 'grad_w_in', 'grad_conv_w', 'grad_a_log', 'grad_dt_bias', 'grad_gdn_norm_g', 'grad_fox_b_f', 'grad_fox_q_norm', 'grad_fox_k_norm', 'grad_g_mem', 'grad_w_mem_kv', 'grad_mem_q_norm', 'grad_mem_k_norm', 'grad_w_up_gdn', 'grad_w_up_fox', 'grad_w_up_mem', 'grad_w_out', 'grad_g_mlp', 'grad_w_ff1', 'grad_w_ff2', 'delta_g_mix', 'delta_w_in', 'delta_conv_w', 'delta_a_log', 'delta_dt_bias', 'delta_gdn_norm_g', 'delta_fox_b_f', 'delta_fox_q_norm', 'delta_fox_k_norm', 'delta_g_mem', 'delta_w_mem_kv', 'delta_mem_q_norm', 'delta_mem_k_norm', 'delta_w_up_gdn', 'delta_w_up_fox', 'delta_w_up_mem', 'delta_w_out', 'delta_g_mlp', 'delta_w_ff1', 'delta_w_ff2', 'new_m_g_mix', 'new_m_w_in', 'new_m_conv_w', 'new_m_a_log', 'new_m_dt_bias', 'new_m_gdn_norm_g', 'new_m_fox_b_f', 'new_m_fox_q_norm', 'new_m_fox_k_norm', 'new_m_g_mem', 'new_m_w_mem_kv', 'new_m_mem_q_norm', 'new_m_mem_k_norm', 'new_m_w_up_gdn', 'new_m_w_up_fox', 'new_m_w_up_mem', 'new_m_w_out', 'new_m_g_mlp', 'new_m_w_ff1', 'new_m_w_ff2', 'new_v_g_mix', 'new_v_w_in', 'new_v_conv_w', 'new_v_a_log', 'new_v_dt_bias', 'new_v_gdn_norm_g', 'new_v_fox_b_f', 'new_v_fox_q_norm', 'new_v_fox_k_norm', 'new_v_g_mem', 'new_v_w_mem_kv', 'new_v_mem_q_norm', 'new_v_mem_k_norm', 'new_v_w_up_gdn', 'new_v_w_up_fox', 'new_v_w_up_mem', 'new_v_w_out', 'new_v_g_mlp', 'new_v_w_ff1', 'new_v_w_ff2']
TWIN_LEAF_KINDS = {'loss': 'loss', 'grad_x': 'grad_x', 'grad_g_mix': 'grad_w', 'grad_w_in': 'grad_w', 'grad_conv_w': 'grad_w', 'grad_a_log': 'grad_w', 'grad_dt_bias': 'grad_w', 'grad_gdn_norm_g': 'grad_w', 'grad_fox_b_f': 'grad_w', 'grad_fox_q_norm': 'grad_w', 'grad_fox_k_norm': 'grad_w', 'grad_g_mem': 'grad_w', 'grad_w_mem_kv': 'grad_w', 'grad_mem_q_norm': 'grad_w', 'grad_mem_k_norm': 'grad_w', 'grad_w_up_gdn': 'grad_w', 'grad_w_up_fox': 'grad_w', 'grad_w_up_mem': 'grad_w', 'grad_w_out': 'grad_w', 'grad_g_mlp': 'grad_w', 'grad_w_ff1': 'grad_w', 'grad_w_ff2': 'grad_w', 'delta_g_mix': 'delta_w', 'delta_w_in': 'delta_w', 'delta_conv_w': 'delta_w', 'delta_a_log': 'delta_w', 'delta_dt_bias': 'delta_w', 'delta_gdn_norm_g': 'delta_w', 'delta_fox_b_f': 'delta_w', 'delta_fox_q_norm': 'delta_w', 'delta_fox_k_norm': 'delta_w', 'delta_g_mem': 'delta_w', 'delta_w_mem_kv': 'delta_w', 'delta_mem_q_norm': 'delta_w', 'delta_mem_k_norm': 'delta_w', 'delta_w_up_gdn': 'delta_w', 'delta_w_up_fox': 'delta_w', 'delta_w_up_mem': 'delta_w', 'delta_w_out': 'delta_w', 'delta_g_mlp': 'delta_w', 'delta_w_ff1': 'delta_w', 'delta_w_ff2': 'delta_w', 'new_m_g_mix': 'new_m', 'new_m_w_in': 'new_m', 'new_m_conv_w': 'new_m', 'new_m_a_log': 'new_m', 'new_m_dt_bias': 'new_m', 'new_m_gdn_norm_g': 'new_m', 'new_m_fox_b_f': 'new_m', 'new_m_fox_q_norm': 'new_m', 'new_m_fox_k_norm': 'new_m', 'new_m_g_mem': 'new_m', 'new_m_w_mem_kv': 'new_m', 'new_m_mem_q_norm': 'new_m', 'new_m_mem_k_norm': 'new_m', 'new_m_w_up_gdn': 'new_m', 'new_m_w_up_fox': 'new_m', 'new_m_w_up_mem': 'new_m', 'new_m_w_out': 'new_m', 'new_m_g_mlp': 'new_m', 'new_m_w_ff1': 'new_m', 'new_m_w_ff2': 'new_m', 'new_v_g_mix': 'new_v', 'new_v_w_in': 'new_v', 'new_v_conv_w': 'new_v', 'new_v_a_log': 'new_v', 'new_v_dt_bias': 'new_v', 'new_v_gdn_norm_g': 'new_v', 'new_v_fox_b_f': 'new_v', 'new_v_fox_q_norm': 'new_v', 'new_v_fox_k_norm': 'new_v', 'new_v_g_mem': 'new_v', 'new_v_w_mem_kv': 'new_v', 'new_v_mem_q_norm': 'new_v', 'new_v_mem_k_norm': 'new_v', 'new_v_w_up_gdn': 'new_v', 'new_v_w_up_fox': 'new_v', 'new_v_w_up_mem': 'new_v', 'new_v_w_out': 'new_v', 'new_v_g_mlp': 'new_v', 'new_v_w_ff1': 'new_v', 'new_v_w_ff2': 'new_v'}


def _forward(args):
    return _fwd_reference(*[args[k] for k in FWD_PARAMS])


def _output_shape():
    def fwd():
        inp = _fwd_setup_inputs(0)
        return _fwd_reference(*[inp[k] for k in FWD_PARAMS])
    out = _jax.eval_shape(fwd)
    return out.shape, out.dtype

N_MICROBATCH = 1
ADAM_LR = 0.001
ADAM_B1 = 0.9
ADAM_B2 = 0.999
ADAM_EPS = 1e-08
ADAM_WD = 0.01
ADAM_STEP = 10
PER_EXAMPLE_BATCH_AXIS = {'x': 0, 'mem': 0, 'loss_target': 0}
SHARED_INPUTS = []
_WEIGHT_DTYPES = {'g_mix': _jnp.float32, 'w_in': _jnp.float32, 'conv_w': _jnp.float32, 'a_log': _jnp.float32, 'dt_bias': _jnp.float32, 'gdn_norm_g': _jnp.float32, 'fox_b_f': _jnp.float32, 'fox_q_norm': _jnp.float32, 'fox_k_norm': _jnp.float32, 'g_mem': _jnp.float32, 'w_mem_kv': _jnp.float32, 'mem_q_norm': _jnp.float32, 'mem_k_norm': _jnp.float32, 'w_up_gdn': _jnp.float32, 'w_up_fox': _jnp.float32, 'w_up_mem': _jnp.float32, 'w_out': _jnp.float32, 'g_mlp': _jnp.float32, 'w_ff1': _jnp.float32, 'w_ff2': _jnp.float32}
MOMENT_SCALE = {'g_mix': 6.038664e+00, 'w_in': 1.563160e-01, 'conv_w': 8.575218e-01, 'a_log': 2.771845e+01, 'dt_bias': 2.669139e+01, 'gdn_norm_g': 5.217310e+01, 'fox_b_f': 1.247767e+02, 'fox_q_norm': 8.413900e+00, 'fox_k_norm': 8.421101e+00, 'g_mem': 2.969507e-01, 'w_mem_kv': 3.064365e-01, 'mem_q_norm': 6.863759e-01, 'mem_k_norm': 6.873951e-01, 'w_up_gdn': 1.677763e+00, 'w_up_fox': 1.621562e-01, 'w_up_mem': 3.215203e-01, 'w_out': 1.604944e+00, 'g_mlp': 9.578339e+01, 'w_ff1': 7.626193e-01, 'w_ff2': 7.991685e+00}


def _to_microbatches(a, axis):
    t = _jnp.moveaxis(a, axis, 0)
    t = t.reshape((N_MICROBATCH, t.shape[0] // N_MICROBATCH) + t.shape[1:])
    return _jnp.moveaxis(t, 1, axis + 1)


def setup_inputs(seed: int = 0) -> dict:
    inp = _fwd_setup_inputs(seed)
    key = _jax.random.fold_in(_jax.random.key(seed), 7919)
    shape, _ = _output_shape()
    out = dict(inp)
    out["loss_target"] = _jax.random.normal(_jax.random.fold_in(key, 0), shape, _jnp.float32)
    for i, name in enumerate(TWIN_WEIGHTS):
        w = inp[name].astype(_jnp.float32)
        if MOMENT_SCALE is None:
            s = _jnp.sqrt(_jnp.mean(_jnp.square(w)) + 1e-30)
        else:
            s = MOMENT_SCALE[name]
        km, kv = _jax.random.split(_jax.random.fold_in(key, i + 1))
        out[name] = w
        out["m_" + name] = s * _jax.random.normal(km, w.shape, _jnp.float32)
        out["v_" + name] = (s * s) * _jax.random.uniform(kv, w.shape, _jnp.float32, 0.5, 1.5)
    if N_MICROBATCH > 1:
        for name, axis in PER_EXAMPLE_BATCH_AXIS.items():
            out[name] = _to_microbatches(out[name], axis)
    return {'x': out['x'], 'mem': out['mem'], 'g_mix': out['g_mix'], 'w_in': out['w_in'], 'conv_w': out['conv_w'], 'a_log': out['a_log'], 'dt_bias': out['dt_bias'], 'gdn_norm_g': out['gdn_norm_g'], 'fox_b_f': out['fox_b_f'], 'fox_q_norm': out['fox_q_norm'], 'fox_k_norm': out['fox_k_norm'], 'g_mem': out['g_mem'], 'w_mem_kv': out['w_mem_kv'], 'mem_q_norm': out['mem_q_norm'], 'mem_k_norm': out['mem_k_norm'], 'w_up_gdn': out['w_up_gdn'], 'w_up_fox': out['w_up_fox'], 'w_up_mem': out['w_up_mem'], 'w_out': out['w_out'], 'g_mlp': out['g_mlp'], 'w_ff1': out['w_ff1'], 'w_ff2': out['w_ff2'], 'loss_target': out['loss_target'], 'm_g_mix': out['m_g_mix'], 'm_w_in': out['m_w_in'], 'm_conv_w': out['m_conv_w'], 'm_a_log': out['m_a_log'], 'm_dt_bias': out['m_dt_bias'], 'm_gdn_norm_g': out['m_gdn_norm_g'], 'm_fox_b_f': out['m_fox_b_f'], 'm_fox_q_norm': out['m_fox_q_norm'], 'm_fox_k_norm': out['m_fox_k_norm'], 'm_g_mem': out['m_g_mem'], 'm_w_mem_kv': out['m_w_mem_kv'], 'm_mem_q_norm': out['m_mem_q_norm'], 'm_mem_k_norm': out['m_mem_k_norm'], 'm_w_up_gdn': out['m_w_up_gdn'], 'm_w_up_fox': out['m_w_up_fox'], 'm_w_up_mem': out['m_w_up_mem'], 'm_w_out': out['m_w_out'], 'm_g_mlp': out['m_g_mlp'], 'm_w_ff1': out['m_w_ff1'], 'm_w_ff2': out['m_w_ff2'], 'v_g_mix': out['v_g_mix'], 'v_w_in': out['v_w_in'], 'v_conv_w': out['v_conv_w'], 'v_a_log': out['v_a_log'], 'v_dt_bias': out['v_dt_bias'], 'v_gdn_norm_g': out['v_gdn_norm_g'], 'v_fox_b_f': out['v_fox_b_f'], 'v_fox_q_norm': out['v_fox_q_norm'], 'v_fox_k_norm': out['v_fox_k_norm'], 'v_g_mem': out['v_g_mem'], 'v_w_mem_kv': out['v_w_mem_kv'], 'v_mem_q_norm': out['v_mem_q_norm'], 'v_mem_k_norm': out['v_mem_k_norm'], 'v_w_up_gdn': out['v_w_up_gdn'], 'v_w_up_fox': out['v_w_up_fox'], 'v_w_up_mem': out['v_w_up_mem'], 'v_w_out': out['v_w_out'], 'v_g_mlp': out['v_g_mlp'], 'v_w_ff1': out['v_w_ff1'], 'v_w_ff2': out['v_w_ff2']}


def _loss(weights, diff, rest, loss_target):
    with _jax.named_scope("forward"):
        args = {**rest, TWIN_DIFF_INPUT: diff, **{k: w.astype(_WEIGHT_DTYPES[k]) for k, w in weights.items()}}
        y = _forward(args)
    with _jax.named_scope("loss_head"):
        err = _jnp.square(y.astype(_jnp.float32) - loss_target)
        return 0.5 * _jnp.sum(_jnp.mean(err, axis=-1)) if err.ndim else 0.5 * err


def _adamw(w, g, m, v):
    m = ADAM_B1 * m + (1.0 - ADAM_B1) * g
    v = ADAM_B2 * v + (1.0 - ADAM_B2) * _jnp.square(g)
    m_hat = m / (1.0 - ADAM_B1 ** ADAM_STEP)
    v_hat = v / (1.0 - ADAM_B2 ** ADAM_STEP)
    delta = -ADAM_LR * (m_hat / (_jnp.sqrt(v_hat) + ADAM_EPS) + ADAM_WD * w)
    return delta, m, v


def reference(x, mem, g_mix, w_in, conv_w, a_log, dt_bias, gdn_norm_g, fox_b_f, fox_q_norm, fox_k_norm, g_mem, w_mem_kv, mem_q_norm, mem_k_norm, w_up_gdn, w_up_fox, w_up_mem, w_out, g_mlp, w_ff1, w_ff2, loss_target, m_g_mix, m_w_in, m_conv_w, m_a_log, m_dt_bias, m_gdn_norm_g, m_fox_b_f, m_fox_q_norm, m_fox_k_norm, m_g_mem, m_w_mem_kv, m_mem_q_norm, m_mem_k_norm, m_w_up_gdn, m_w_up_fox, m_w_up_mem, m_w_out, m_g_mlp, m_w_ff1, m_w_ff2, v_g_mix, v_w_in, v_conv_w, v_a_log, v_dt_bias, v_gdn_norm_g, v_fox_b_f, v_fox_q_norm, v_fox_k_norm, v_g_mem, v_w_mem_kv, v_mem_q_norm, v_mem_k_norm, v_w_up_gdn, v_w_up_fox, v_w_up_mem, v_w_out, v_g_mlp, v_w_ff1, v_w_ff2):
    given = dict(x=x, mem=mem, g_mix=g_mix, w_in=w_in, conv_w=conv_w, a_log=a_log, dt_bias=dt_bias, gdn_norm_g=gdn_norm_g, fox_b_f=fox_b_f, fox_q_norm=fox_q_norm, fox_k_norm=fox_k_norm, g_mem=g_mem, w_mem_kv=w_mem_kv, mem_q_norm=mem_q_norm, mem_k_norm=mem_k_norm, w_up_gdn=w_up_gdn, w_up_fox=w_up_fox, w_up_mem=w_up_mem, w_out=w_out, g_mlp=g_mlp, w_ff1=w_ff1, w_ff2=w_ff2, loss_target=loss_target, m_g_mix=m_g_mix, m_w_in=m_w_in, m_conv_w=m_conv_w, m_a_log=m_a_log, m_dt_bias=m_dt_bias, m_gdn_norm_g=m_gdn_norm_g, m_fox_b_f=m_fox_b_f, m_fox_q_norm=m_fox_q_norm, m_fox_k_norm=m_fox_k_norm, m_g_mem=m_g_mem, m_w_mem_kv=m_w_mem_kv, m_mem_q_norm=m_mem_q_norm, m_mem_k_norm=m_mem_k_norm, m_w_up_gdn=m_w_up_gdn, m_w_up_fox=m_w_up_fox, m_w_up_mem=m_w_up_mem, m_w_out=m_w_out, m_g_mlp=m_g_mlp, m_w_ff1=m_w_ff1, m_w_ff2=m_w_ff2, v_g_mix=v_g_mix, v_w_in=v_w_in, v_conv_w=v_conv_w, v_a_log=v_a_log, v_dt_bias=v_dt_bias, v_gdn_norm_g=v_gdn_norm_g, v_fox_b_f=v_fox_b_f, v_fox_q_norm=v_fox_q_norm, v_fox_k_norm=v_fox_k_norm, v_g_mem=v_g_mem, v_w_mem_kv=v_w_mem_kv, v_mem_q_norm=v_mem_q_norm, v_mem_k_norm=v_mem_k_norm, v_w_up_gdn=v_w_up_gdn, v_w_up_fox=v_w_up_fox, v_w_up_mem=v_w_up_mem, v_w_out=v_w_out, v_g_mlp=v_g_mlp, v_w_ff1=v_w_ff1, v_w_ff2=v_w_ff2)
    weights = {n: given[n] for n in TWIN_WEIGHTS}
    shared = {n: given[n] for n in SHARED_INPUTS}
    per_example = {n: given[n] for n in ['x', 'mem']}
    grad_fn = _jax.value_and_grad(_loss, argnums=(0, 1))

    def one_microbatch(ex, loss_target):
        ex = dict(ex)
        diff = ex.pop(TWIN_DIFF_INPUT)
        return grad_fn(weights, diff, {**shared, **ex}, loss_target)

    if N_MICROBATCH == 1:
        loss, (grad_w, grad_x) = one_microbatch(per_example, given["loss_target"])
    else:
        def body(carry, xs):
            loss_sum, grad_sum = carry
            l_k, (gw_k, gx_k) = one_microbatch(xs[0], xs[1])
            with _jax.named_scope("update"):
                return (loss_sum + l_k, _jax.tree.map(_jnp.add, grad_sum, gw_k)), gx_k

        init = (_jnp.zeros((), _jnp.float32), _jax.tree.map(_jnp.zeros_like, weights))
        (loss, grad_w), grad_x = _jax.lax.scan(body, init, (per_example, given["loss_target"]))
    with _jax.named_scope("update"):
        delta_w, new_m, new_v = {}, {}, {}
        for n in TWIN_WEIGHTS:
            delta_w[n], new_m[n], new_v[n] = _adamw(weights[n], grad_w[n], given["m_" + n], given["v_" + n])
    return (loss, grad_x, *[grad_w[n] for n in TWIN_WEIGHTS], *[delta_w[n] for n in TWIN_WEIGHTS],
            *[new_m[n] for n in TWIN_WEIGHTS], *[new_v[n] for n in TWIN_WEIGHTS])
```

```python
import functools

import jax
import jax.numpy as jnp
import numpy as np
from jax import lax
from jax.experimental import pallas as pl
from jax.experimental.pallas import tpu as pltpu

F32 = jnp.float32
BF16 = jnp.bfloat16
HI = lax.Precision.HIGHEST

EPS = 1e-6
GDN_CHUNK = 64
GDN_CONV = 4
HEAD = 128
LANES = 128
HALO = 16
N_DEV = 8
MESH = pl.DeviceIdType.MESH
VMEM_LIMIT_V7X = 56 * 1024 * 1024

ADAM_LR, ADAM_B1, ADAM_B2, ADAM_EPS, ADAM_WD, ADAM_STEP = 0.001, 0.9, 0.999, 1e-08, 0.01, 10

LANE_B, LANE_A, LANE_F = 0, 8, 16


def _params(sem):
    return pltpu.CompilerParams(dimension_semantics=sem, vmem_limit_bytes=VMEM_LIMIT_V7X)


def _dg(a, b, ca, cb, prec):
    nb = a.ndim - 2
    batch = tuple(range(nb))
    return lax.dot_general(a, b, (((ca + nb,), (cb + nb,)), (batch, batch)), precision=prec,
                           preferred_element_type=F32)


def _make_mm(prec, cast):
    def c(x):
        return x.astype(BF16) if cast else x

    @jax.custom_vjp
    def nn(a, b):
        return _dg(c(a), c(b), 1, 0, prec)

    @jax.custom_vjp
    def nt(a, b):
        return _dg(c(a), c(b), 1, 1, prec)

    @jax.custom_vjp
    def tn(a, b):
        return _dg(c(a), c(b), 0, 0, prec)

    nn.defvjp(lambda a, b: (nn(a, b), (a, b)), lambda r, g: (nt(g, r[1]), tn(r[0], g)))
    nt.defvjp(lambda a, b: (nt(a, b), (a, b)), lambda r, g: (nn(g, r[1]), tn(g, r[0])))
    tn.defvjp(lambda a, b: (tn(a, b), (a, b)), lambda r, g: (nt(r[1], g), nn(r[0], g)))
    return nn, nt, tn


NN, NT, TN = _make_mm(None, True)
NNH, NTH, TNH = _make_mm(HI, False)


def _sigmoid(x):
    return 1.0 / (1.0 + jnp.exp(-x))


def _silu(x):
    return x * _sigmoid(x)


def _softplus(x):
    return jnp.maximum(x, 0.0) + jnp.log(1.0 + jnp.exp(-jnp.abs(x)))


def _log_sigmoid(x):
    return -_softplus(-x)


def _rms(x, g):
    return x * lax.rsqrt(jnp.mean(x * x, axis=-1, keepdims=True) + EPS) * g


def _tile(n, target):
    t = target
    while t >= LANES:
        if n % t == 0:
            return t
        t //= 2
    return n


def matmul(a, b, *, mode, name, out_dtypes=(BF16,), epi=None, extras=(), tm=1024, tn=1024, tk=512):
    if mode == "nn":
        (M, K), (K2, N) = a.shape, b.shape
    elif mode == "nt":
        (M, K), (N, K2) = a.shape, b.shape
    else:
        (K, M), (K2, N) = a.shape, b.shape
    assert K == K2, (name, a.shape, b.shape)
    tm, tn, tk = _tile(M, tm), _tile(N, tn), _tile(K, tk)
    nk = K // tk
    a_spec = (pl.BlockSpec((tk, tm), lambda i, j, k: (k, i)) if mode == "tn"
              else pl.BlockSpec((tm, tk), lambda i, j, k: (i, k)))
    b_spec = (pl.BlockSpec((tn, tk), lambda i, j, k: (j, k)) if mode == "nt"
              else pl.BlockSpec((tk, tn), lambda i, j, k: (k, j)))
    o_spec = pl.BlockSpec((tm, tn), lambda i, j, k: (i, j))
    dims = {"nn": ((1,), (0,)), "nt": ((1,), (1,)), "tn": ((0,), (0,))}[mode]
    ne, no = len(extras), len(out_dtypes)

    def body(a_ref, b_ref, *rest):
        ex, outs, acc = rest[:ne], rest[ne:ne + no], rest[-1]
        k = pl.program_id(2)

        @pl.when(k == 0)
        def _():
            acc[...] = jnp.zeros_like(acc)

        acc[...] += lax.dot_general(a_ref[...].astype(BF16), b_ref[...].astype(BF16), (dims, ((), ())),
                                    preferred_element_type=F32)

        @pl.when(k == nk - 1)
        def _():
            r = acc[...]
            vals = epi(r, *[e[...] for e in ex]) if epi is not None else (r,)
            for o, v in zip(outs, vals):
                o[...] = v.astype(o.dtype)

    out = pl.pallas_call(
        body, name=name, grid=(M // tm, N // tn, nk),
        in_specs=[a_spec, b_spec] + [o_spec] * ne, out_specs=[o_spec] * no,
        out_shape=[jax.ShapeDtypeStruct((M, N), d) for d in out_dtypes],
        scratch_shapes=[pltpu.VMEM((tm, tn), F32)],
        compiler_params=_params(("parallel", "parallel", "arbitrary")))(a, b, *extras)
    return out[0] if no == 1 else out


def _row_spec(tr, cb, off, moves):
    return pl.BlockSpec((tr, cb), lambda i, j: (i, off + moves * j))


def _whole_spec(p):
    return pl.BlockSpec(p.shape, lambda i, j: (0,) * p.ndim)


def rowwise(fn, rows, params, outs, *, name, ncb=1, tr=512):
    T = rows[0][0].shape[0]
    tr = min(tr, T)
    assert T % tr == 0
    nr, npar = len(rows), len(params)

    def body(*refs):
        r, p, o = refs[:nr], refs[nr:nr + npar], refs[nr + npar:]
        vals = fn(*[x[...].astype(F32) for x in r], *[x[...] for x in p])
        for oref, v in zip(o, vals):
            oref[...] = v.astype(oref.dtype)

    res = pl.pallas_call(
        body, name=name, grid=(T // tr, ncb),
        in_specs=[_row_spec(tr, cb, off, mv) for (_, cb, off, mv) in rows] + [_whole_spec(p) for p in params],
        out_specs=[_row_spec(tr, cb, 0, 1) for (_, cb, _) in outs],
        out_shape=[jax.ShapeDtypeStruct((T, cols), d) for (cols, _, d) in outs],
        compiler_params=_params(("parallel", "parallel")))(*[r[0] for r in rows], *params)
    return res


def rowwise_bwd(fn, rows, params, cots, drows, dparams, *, name, ncb=1, tr=512, adds=()):
    T = rows[0][0].shape[0]
    tr = min(tr, T)
    assert T % tr == 0
    nr, npar, nc, ndr, na = len(rows), len(params), len(cots), len(drows), len(adds)

    def body(*refs):
        r, p, c = refs[:nr], refs[nr:nr + npar], refs[nr + npar:nr + npar + nc]
        base = nr + npar + nc + na
        ad, o_r, o_p = refs[base - na:base], refs[base:base + ndr], refs[base + ndr:]
        prim = [x[...].astype(F32) for x in r] + [x[...] for x in p]
        _, vjp = jax.vjp(lambda *a: tuple(fn(*a)), *prim)
        g = vjp(tuple(x[...].astype(F32) for x in c))
        for k, (oref, (idx, _)) in enumerate(zip(o_r, drows)):
            val = g[idx] + ad[k][...].astype(F32) if k < na else g[idx]
            oref[...] = val.astype(oref.dtype)
        first = jnp.logical_and(pl.program_id(0) == 0, pl.program_id(1) == 0)

        @pl.when(first)
        def _():
            for oref in o_p:
                oref[...] = jnp.zeros_like(oref)

        for oref, idx in zip(o_p, dparams):
            oref[...] += g[nr + idx]

    res = pl.pallas_call(
        body, name=name, grid=(T // tr, ncb),
        in_specs=([_row_spec(tr, cb, off, mv) for (_, cb, off, mv) in rows] + [_whole_spec(p) for p in params]
                  + [_row_spec(tr, cb, off, mv) for (_, cb, off, mv) in tuple(cots) + tuple(adds)]),
        out_specs=([_row_spec(tr, rows[idx][1], 0, 1) for (idx, _) in drows]
                   + [_whole_spec(params[idx]) for idx in dparams]),
        out_shape=([jax.ShapeDtypeStruct((T, ncb * rows[idx][1] if rows[idx][3] else rows[idx][1]), d)
                    for (idx, d) in drows]
                   + [jax.ShapeDtypeStruct(params[idx].shape, F32) for idx in dparams]),
        compiler_params=_params(("arbitrary", "arbitrary")))(
            *[r[0] for r in rows], *params, *[c[0] for c in cots], *[a[0] for a in adds])
    return res


def full(a):
    return (a, a.shape[1], 0, 0)


def cumsum_tokens(x, *, reverse, name, tb=256):
    T = x.shape[0]
    tb = min(tb, T)
    nb = T // tb
    idx = (lambda i: (nb - 1 - i, 0)) if reverse else (lambda i: (i, 0))

    def body(x_ref, o_ref, carry):
        @pl.when(pl.program_id(0) == 0)
        def _():
            carry[...] = jnp.zeros_like(carry)

        ii = lax.broadcasted_iota(jnp.int32, (tb, tb), 0)
        jj = lax.broadcasted_iota(jnp.int32, (tb, tb), 1)
        tri = ((ii <= jj) if reverse else (ii >= jj)).astype(F32)
        c = lax.dot_general(tri, x_ref[...], (((1,), (0,)), ((), ())), precision=HI,
                            preferred_element_type=F32) + carry[0:1, :]
        o_ref[...] = c
        carry[0:1, :] = c[0:1, :] if reverse else c[tb - 1:tb, :]

    return pl.pallas_call(
        body, name=name, grid=(nb,), in_specs=[pl.BlockSpec((tb, LANES), idx)],
        out_specs=pl.BlockSpec((tb, LANES), idx), out_shape=jax.ShapeDtypeStruct((T, LANES), F32),
        scratch_shapes=[pltpu.VMEM((8, LANES), F32)], compiler_params=_params(("arbitrary",)))(x)


def _conv_post(y, kind, dk):
    c = _silu(y)
    r = lax.rsqrt(jnp.sum(c * c, axis=-1, keepdims=True) + EPS)
    return jnp.where(kind == 0, c * r * (dk ** -0.5), jnp.where(kind == 1, c * r, c))


def _conv_taps(cur, prev, w, tr):
    ext = jnp.concatenate([prev, cur], axis=0)
    y = w[3:4, :] * cur
    for d in (1, 2, 3):
        y = y + w[3 - d:4 - d, :] * pltpu.roll(ext, d, 0)[HALO:HALO + tr]
    return y


def conv_fwd(big, w, n_qk_heads, *, name, tr=512):
    T, W = big.shape[0], w.shape[1]
    tr = min(tr, T)
    nh = W // HEAD

    def body(cur_ref, prev_ref, w_ref, o_ref):
        i, j = pl.program_id(0), pl.program_id(1)
        prev = jnp.where(i > 0, prev_ref[...].astype(F32), 0.0)
        y = _conv_taps(cur_ref[...].astype(F32), prev, w_ref[...], tr)
        kind = jnp.where(j < n_qk_heads, 0, jnp.where(j < 2 * n_qk_heads, 1, 2))
        o_ref[...] = _conv_post(y, kind, HEAD)

    return pl.pallas_call(
        body, name=name, grid=(T // tr, nh),
        in_specs=[pl.BlockSpec((tr, HEAD), lambda i, j: (i, j)),
                  pl.BlockSpec((HALO, HEAD), lambda i, j: (jnp.maximum(i * (tr // HALO) - 1, 0), j)),
                  pl.BlockSpec((GDN_CONV, HEAD), lambda i, j: (0, j))],
        out_specs=pl.BlockSpec((tr, HEAD), lambda i, j: (i, j)),
        out_shape=jax.ShapeDtypeStruct((T, W), F32), compiler_params=_params(("parallel", "parallel")))(big, big, w)


def conv_bwd_taps(big, w, dcn, n_qk_heads, *, name, tr=512):
    T, W = big.shape[0], w.shape[1]
    tr = min(tr, T)
    nh = W // HEAD

    def body(cur_ref, prev_ref, w_ref, g_ref, dy_ref, dw_ref):
        j, i = pl.program_id(0), pl.program_id(1)
        cur = cur_ref[...].astype(F32)
        prev = jnp.where(i > 0, prev_ref[...].astype(F32), 0.0)
        y = _conv_taps(cur, prev, w_ref[...], tr)
        kind = jnp.where(j < n_qk_heads, 0, jnp.where(j < 2 * n_qk_heads, 1, 2))
        _, vjp = jax.vjp(lambda t: _conv_post(t, kind, HEAD), y)
        dy, = vjp(g_ref[...])
        dy_ref[...] = dy
        ext = jnp.concatenate([prev, cur], axis=0)
        rows = [jnp.sum(dy * (cur if d == 0 else pltpu.roll(ext, d, 0)[HALO:HALO + tr]), axis=0, keepdims=True)
                for d in (3, 2, 1, 0)]

        @pl.when(i == 0)
        def _():
            dw_ref[...] = jnp.zeros_like(dw_ref)

        dw_ref[...] += jnp.concatenate(rows, axis=0)

    return pl.pallas_call(
        body, name=name, grid=(nh, T // tr),
        in_specs=[pl.BlockSpec((tr, HEAD), lambda j, i: (i, j)),
                  pl.BlockSpec((HALO, HEAD), lambda j, i: (jnp.maximum(i * (tr // HALO) - 1, 0), j)),
                  pl.BlockSpec((GDN_CONV, HEAD), lambda j, i: (0, j)),
                  pl.BlockSpec((tr, HEAD), lambda j, i: (i, j))],
        out_specs=[pl.BlockSpec((tr, HEAD), lambda j, i: (i, j)), pl.BlockSpec((GDN_CONV, HEAD), lambda j, i: (0, j))],
        out_shape=[jax.ShapeDtypeStruct((T, W), F32), jax.ShapeDtypeStruct((GDN_CONV, W), F32)],
        compiler_params=_params(("parallel", "arbitrary")))(big, big, w, dcn)


def conv_bwd_input(dy, w, *, name, tr=512):
    T, W = dy.shape
    tr = min(tr, T)
    nrow = T // tr

    def body(cur_ref, nxt_ref, w_ref, o_ref):
        i = pl.program_id(0)
        cur = cur_ref[...]
        nxt = jnp.where(i < nrow - 1, nxt_ref[...], 0.0)
        ext = jnp.concatenate([cur, nxt], axis=0)
        w = w_ref[...]
        dx = w[3:4, :] * cur
        for d in (1, 2, 3):
            dx = dx + w[3 - d:4 - d, :] * pltpu.roll(ext, tr + HALO - d, 0)[0:tr]
        o_ref[...] = dx.astype(o_ref.dtype)

    return pl.pallas_call(
        body, name=name, grid=(nrow, W // HEAD),
        in_specs=[pl.BlockSpec((tr, HEAD), lambda i, j: (i, j)),
                  pl.BlockSpec((HALO, HEAD), lambda i, j: (jnp.minimum((i + 1) * (tr // HALO), T // HALO - 1), j)),
                  pl.BlockSpec((GDN_CONV, HEAD), lambda i, j: (0, j))],
        out_specs=pl.BlockSpec((tr, HEAD), lambda i, j: (i, j)),
        out_shape=jax.ShapeDtypeStruct((T, W), BF16), compiler_params=_params(("parallel", "parallel")))(dy, dy, w)


def _gdn_chunk(q, k, v, bg, S):
    H, C = q.shape[0], q.shape[1]
    ii = lax.broadcasted_iota(jnp.int32, (C, C), 0)
    jj = lax.broadcasted_iota(jnp.int32, (C, C), 1)
    lincl = (ii >= jj).astype(F32)
    strict, incl, eye = (ii > jj)[None], (ii >= jj)[None], (ii == jj)[None]
    gam2d = lax.dot_general(lincl, bg, (((1,), (0,)), ((), ())), precision=HI, preferred_element_type=F32)
    lane = lax.broadcasted_iota(jnp.int32, (H, 1, LANES), 2)
    hh = lax.broadcasted_iota(jnp.int32, (H, 1, LANES), 0)
    beta = jnp.sum(bg[None] * (lane == hh + LANE_B).astype(F32), axis=2, keepdims=True)
    gam = jnp.sum(gam2d[None] * (lane == hh + LANE_A).astype(F32), axis=2, keepdims=True)
    last = (lax.broadcasted_iota(jnp.int32, (1, C, 1), 1) == C - 1).astype(F32)
    gam_last = jnp.sum(gam * last, axis=1, keepdims=True)
    gam_row = NNH(jnp.ones((H, C, C), F32), jnp.where(eye, gam, 0.0))
    diff = gam - gam_row
    dec_s = jnp.where(strict, jnp.exp(jnp.where(strict, diff, 0.0)), 0.0)
    dec_i = jnp.where(incl, jnp.exp(jnp.where(incl, diff, 0.0)), 0.0)
    p = -(beta * NT(k, k) * dec_s)
    t = jnp.where(eye, 1.0, 0.0) + p
    for _ in range(5):
        p = NNH(p, p)
        t = t + NNH(t, p)
    eg = jnp.exp(gam)
    w = NNH(t, beta * eg * k)
    u0 = NNH(t, beta * v)
    qk = NT(q, k) * dec_i
    u = u0 - NN(w, S)
    o = NN(q * eg, S) + NN(qk, u)
    S2 = jnp.exp(gam_last) * S + TN(k * jnp.exp(gam_last - gam), u)
    return o, S2


def _heads(x, base, H):
    return jnp.stack([x[:, base + h * HEAD:base + (h + 1) * HEAD] for h in range(H)])


def _unheads(x):
    return jnp.concatenate([x[h] for h in range(x.shape[0])], axis=1)


def gdn_fwd(cn, bgf, H, *, name):
    T, C, W = cn.shape[0], GDN_CHUNK, H * HEAD
    N = T // C

    def body(cn_ref, bg_ref, o_ref, ss_ref, s_scr):
        @pl.when(pl.program_id(0) == 0)
        def _():
            s_scr[...] = jnp.zeros_like(s_scr)

        x, S = cn_ref[...], s_scr[...]
        ss_ref[0] = S
        o, S2 = _gdn_chunk(_heads(x, 0, H), _heads(x, W, H), _heads(x, 2 * W, H), bg_ref[...], S)
        o_ref[...] = _unheads(o)
        s_scr[...] = S2

    return pl.pallas_call(
        body, name=name, grid=(N,),
        in_specs=[pl.BlockSpec((C, 3 * W), lambda n: (n, 0)), pl.BlockSpec((C, LANES), lambda n: (n, 0))],
        out_specs=[pl.BlockSpec((C, W), lambda n: (n, 0)), pl.BlockSpec((1, H, HEAD, HEAD), lambda n: (n, 0, 0, 0))],
        out_shape=[jax.ShapeDtypeStruct((T, W), F32), jax.ShapeDtypeStruct((N, H, HEAD, HEAD), F32)],
        scratch_shapes=[pltpu.VMEM((H, HEAD, HEAD), F32)], compiler_params=_params(("arbitrary",)))(cn, bgf)


def gdn_bwd(cn, bgf, ss, do, H, *, name):
    T, C, W = cn.shape[0], GDN_CHUNK, H * HEAD
    N = T // C

    def body(cn_ref, bg_ref, ss_ref, do_ref, dcn_ref, dbg_ref, ds_scr):
        @pl.when(pl.program_id(0) == 0)
        def _():
            ds_scr[...] = jnp.zeros_like(ds_scr)

        x = cn_ref[...]
        _, vjp = jax.vjp(_gdn_chunk, _heads(x, 0, H), _heads(x, W, H), _heads(x, 2 * W, H), bg_ref[...], ss_ref[0])
        dq, dk, dv, dbg, dS = vjp((_heads(do_ref[...], 0, H), ds_scr[...]))
        dcn_ref[...] = jnp.concatenate([_unheads(dq), _unheads(dk), _unheads(dv)], axis=1)
        dbg_ref[...] = dbg
        ds_scr[...] = dS

    rev = lambda n: (N - 1 - n, 0)
    return pl.pallas_call(
        body, name=name, grid=(N,),
        in_specs=[pl.BlockSpec((C, 3 * W), rev), pl.BlockSpec((C, LANES), rev),
                  pl.BlockSpec((1, H, HEAD, HEAD), lambda n: (N - 1 - n, 0, 0, 0)), pl.BlockSpec((C, W), rev)],
        out_specs=[pl.BlockSpec((C, 3 * W), rev), pl.BlockSpec((C, LANES), rev)],
        out_shape=[jax.ShapeDtypeStruct((T, 3 * W), F32), jax.ShapeDtypeStruct((T, LANES), F32)],
        scratch_shapes=[pltpu.VMEM((H, HEAD, HEAD), F32)], compiler_params=_params(("arbitrary",)))(cn, bgf, ss, do)


def _fox_scores(q, k, cq, ck, i, j, tq, tk):
    s = lax.dot_general(q, k, (((1,), (1,)), ((), ())), preferred_element_type=F32) * (HEAD ** -0.5) + cq - ck
    rows = i * tq + lax.broadcasted_iota(jnp.int32, (tq, tk), 0)
    cols = j * tk + lax.broadcasted_iota(jnp.int32, (tq, tk), 1)
    return s, rows >= cols


def fox_fwd(q, k, v, cum, cum_t, H, *, name, tq=512, tk=512):
    T, W = q.shape
    tq, tk = min(tq, T), min(tk, T)
    nq, nk = T // tq, T // tk
    assert tq == tk

    def body(q_ref, k_ref, v_ref, cq_ref, ck_ref, o_ref, lse_ref, acc, m_scr, l_scr):
        i, j = pl.program_id(0), pl.program_id(1)

        @pl.when(j == 0)
        def _():
            acc[...] = jnp.zeros_like(acc)
            m_scr[...] = jnp.full_like(m_scr, -jnp.inf)
            l_scr[...] = jnp.zeros_like(l_scr)

        @pl.when(j <= i)
        def _():
            lane = lax.broadcasted_iota(jnp.int32, (tq, LANES), 1)
            m_all, l_all = m_scr[...], l_scr[...]
            new_m, new_l = m_all, l_all
            for h in range(H):
                hs = slice(h * HEAD, (h + 1) * HEAD)
                s, mask = _fox_scores(q_ref[:, hs], k_ref[:, hs], cq_ref[:, LANE_F + h:LANE_F + h + 1],
                                      ck_ref[h:h + 1, :], i, j, tq, tk)
                s = jnp.where(mask, s, -jnp.inf)
                m_prev, l_prev = m_all[:, h:h + 1], l_all[:, h:h + 1]
                m_new = jnp.maximum(m_prev, jnp.max(s, axis=1, keepdims=True))
                p = jnp.exp(s - m_new)
                alpha = jnp.exp(m_prev - m_new)
                p_hi = p.astype(BF16)
                p_lo = (p - p_hi.astype(F32)).astype(BF16)
                pv = lambda t: lax.dot_general(t, v_ref[:, hs], (((1,), (0,)), ((), ())), preferred_element_type=F32)
                acc[:, hs] = alpha * acc[:, hs] + (pv(p_hi) + pv(p_lo))
                new_m = jnp.where(lane == h, m_new, new_m)
                new_l = jnp.where(lane == h, alpha * l_prev + jnp.sum(p, axis=1, keepdims=True), new_l)
            m_scr[...] = new_m
            l_scr[...] = new_l

        @pl.when(j == nk - 1)
        def _():
            l_all = l_scr[...]
            for h in range(H):
                hs = slice(h * HEAD, (h + 1) * HEAD)
                o_ref[:, hs] = (acc[:, hs] / l_all[:, h:h + 1]).astype(o_ref.dtype)
            lse_ref[...] = m_scr[...] + jnp.log(l_all)

    kv_idx = lambda i, j: (jnp.minimum(j, i), 0)
    return pl.pallas_call(
        body, name=name, grid=(nq, nk),
        in_specs=[pl.BlockSpec((tq, W), lambda i, j: (i, 0)), pl.BlockSpec((tk, W), kv_idx), pl.BlockSpec((tk, W), kv_idx),
                  pl.BlockSpec((tq, LANES), lambda i, j: (i, 0)), pl.BlockSpec((8, tk), lambda i, j: (0, jnp.minimum(j, i)))],
        out_specs=[pl.BlockSpec((tq, W), lambda i, j: (i, 0)), pl.BlockSpec((tq, LANES), lambda i, j: (i, 0))],
        out_shape=[jax.ShapeDtypeStruct((T, W), F32), jax.ShapeDtypeStruct((T, LANES), F32)],
        scratch_shapes=[pltpu.VMEM((tq, W), F32), pltpu.VMEM((tq, LANES), F32), pltpu.VMEM((tq, LANES), F32)],
        compiler_params=_params(("parallel", "arbitrary")))(q, k, v, cum, cum_t)


def _fox_ds(q_ref, k_ref, v_ref, do_ref, o_ref, cq_ref, ck_ref, lse_ref, h, i, j, tq, tk):
    hs = slice(h * HEAD, (h + 1) * HEAD)
    s, mask = _fox_scores(q_ref[:, hs], k_ref[:, hs], cq_ref[:, LANE_F + h:LANE_F + h + 1], ck_ref[h:h + 1, :],
                          i, j, tq, tk)
    p = jnp.where(mask, jnp.exp(s - lse_ref[:, h:h + 1]), 0.0)
    do = do_ref[:, hs]
    dp = lax.dot_general(do, v_ref[:, hs], (((1,), (1,)), ((), ())), preferred_element_type=F32)
    delta = jnp.sum(do.astype(F32) * o_ref[:, hs].astype(F32), axis=1, keepdims=True)
    return p, p * (dp - delta)


def fox_bwd_kv(q, k, v, do, o, cum, cum_t, lse, H, *, name, tq=512, tk=512):
    T, W = q.shape
    tq, tk = min(tq, T), min(tk, T)
    nq, nk = T // tq, T // tk

    def body(q_ref, k_ref, v_ref, do_ref, o_ref, cq_ref, ck_ref, lse_ref, dk_ref, dv_ref, dc_ref, dk_acc, dv_acc, dc_acc):
        j, i = pl.program_id(0), pl.program_id(1)

        @pl.when(i == 0)
        def _():
            dk_acc[...] = jnp.zeros_like(dk_acc)
            dv_acc[...] = jnp.zeros_like(dv_acc)
            dc_acc[...] = jnp.zeros_like(dc_acc)

        @pl.when(i >= j)
        def _():
            row = lax.broadcasted_iota(jnp.int32, (8, tk), 0)
            dc = dc_acc[...]
            for h in range(H):
                hs = slice(h * HEAD, (h + 1) * HEAD)
                p, ds = _fox_ds(q_ref, k_ref, v_ref, do_ref, o_ref, cq_ref, ck_ref, lse_ref, h, i, j, tq, tk)
                dv_acc[:, hs] += lax.dot_general(p.astype(BF16), do_ref[:, hs], (((0,), (0,)), ((), ())),
                                                 preferred_element_type=F32)
                dk_acc[:, hs] += lax.dot_general(ds.astype(BF16), q_ref[:, hs], (((0,), (0,)), ((), ())),
                                                 preferred_element_type=F32) * (HEAD ** -0.5)
                dc = jnp.where(row == h, dc - jnp.sum(ds, axis=0, keepdims=True), dc)
            dc_acc[...] = dc

        @pl.when(i == nq - 1)
        def _():
            dk_ref[...] = dk_acc[...].astype(dk_ref.dtype)
            dv_ref[...] = dv_acc[...].astype(dv_ref.dtype)
            dc_ref[...] = dc_acc[...]

    q_idx = lambda j, i: (jnp.maximum(i, j), 0)
    kv_idx = lambda j, i: (j, 0)
    return pl.pallas_call(
        body, name=name, grid=(nk, nq),
        in_specs=[pl.BlockSpec((tq, W), q_idx), pl.BlockSpec((tk, W), kv_idx), pl.BlockSpec((tk, W), kv_idx),
                  pl.BlockSpec((tq, W), q_idx), pl.BlockSpec((tq, W), q_idx), pl.BlockSpec((tq, LANES), q_idx),
                  pl.BlockSpec((8, tk), lambda j, i: (0, j)), pl.BlockSpec((tq, LANES), q_idx)],
        out_specs=[pl.BlockSpec((tk, W), kv_idx), pl.BlockSpec((tk, W), kv_idx), pl.BlockSpec((8, tk), lambda j, i: (0, j))],
        out_shape=[jax.ShapeDtypeStruct((T, W), BF16), jax.ShapeDtypeStruct((T, W), BF16), jax.ShapeDtypeStruct((8, T), F32)],
        scratch_shapes=[pltpu.VMEM((tk, W), F32), pltpu.VMEM((tk, W), F32), pltpu.VMEM((8, tk), F32)],
        compiler_params=_params(("parallel", "arbitrary")))(q, k, v, do, o, cum, cum_t, lse)


def fox_bwd_q(q, k, v, do, o, cum, cum_t, lse, H, *, name, tq=512, tk=512):
    T, W = q.shape
    tq, tk = min(tq, T), min(tk, T)
    nq, nk = T // tq, T // tk

    def body(q_ref, k_ref, v_ref, do_ref, o_ref, cq_ref, ck_ref, lse_ref, dq_ref, dq_acc):
        i, j = pl.program_id(0), pl.program_id(1)

        @pl.when(j == 0)
        def _():
            dq_acc[...] = jnp.zeros_like(dq_acc)

        @pl.when(j <= i)
        def _():
            for h in range(H):
                hs = slice(h * HEAD, (h + 1) * HEAD)
                _, ds = _fox_ds(q_ref, k_ref, v_ref, do_ref, o_ref, cq_ref, ck_ref, lse_ref, h, i, j, tq, tk)
                dq_acc[:, hs] += lax.dot_general(ds.astype(BF16), k_ref[:, hs], (((1,), (0,)), ((), ())),
                                                 preferred_element_type=F32) * (HEAD ** -0.5)

        @pl.when(j == nk - 1)
        def _():
            dq_ref[...] = dq_acc[...].astype(dq_ref.dtype)

    q_idx = lambda i, j: (i, 0)
    kv_idx = lambda i, j: (jnp.minimum(j, i), 0)
    return pl.pallas_call(
        body, name=name, grid=(nq, nk),
        in_specs=[pl.BlockSpec((tq, W), q_idx), pl.BlockSpec((tk, W), kv_idx), pl.BlockSpec((tk, W), kv_idx),
                  pl.BlockSpec((tq, W), q_idx), pl.BlockSpec((tq, W), q_idx), pl.BlockSpec((tq, LANES), q_idx),
                  pl.BlockSpec((8, tk), lambda i, j: (0, jnp.minimum(j, i))), pl.BlockSpec((tq, LANES), q_idx)],
        out_specs=pl.BlockSpec((tq, W), q_idx), out_shape=jax.ShapeDtypeStruct((T, W), BF16),
        scratch_shapes=[pltpu.VMEM((tq, W), F32)],
        compiler_params=_params(("parallel", "arbitrary")))(q, k, v, do, o, cum, cum_t, lse)


def _gates_fn(small, a_log_l, dt_bias_l, b_f_l):
    lane = lax.broadcasted_iota(jnp.int32, small.shape, 1)
    beta = _sigmoid(small)
    g = -jnp.exp(a_log_l) * _softplus(small + dt_bias_l)
    lf = _log_sigmoid(small + b_f_l)
    return (jnp.where(lane < LANE_A, beta, jnp.where(lane < LANE_F, g, jnp.where(lane < LANE_F + 8, lf, 0.0))),)


def _gated_norm_fn(o, z, g):
    return (_rms(o, g) * _silu(z),)


def _merge_fn(ya, yb, ym, ga, gb, gm):
    return (_sigmoid(ga) * ya + _sigmoid(gb) * yb + _sigmoid(gm) * ym,)


def _mem_attn_fn(nh, dh, mq, kn, v, gq):
    outs = []
    for h in range(nh):
        hs = slice(h * dh, (h + 1) * dh)
        qn = _rms(mq[:, hs], gq)
        s = NT(qn, kn[:, hs]) * (dh ** -0.5)
        e = jnp.exp(s - jnp.max(s, axis=1, keepdims=True))
        p = e / jnp.sum(e, axis=1, keepdims=True)
        outs.append(NN(p, v[:, hs]))
    return (jnp.concatenate(outs, axis=1),)


def sum_squares(x, *, name, tr=512):
    T, D = x.shape
    tr = min(tr, T)

    def body(x_ref, o_ref):
        @pl.when(pl.program_id(0) == 0)
        def _():
            o_ref[...] = jnp.zeros_like(o_ref)

        v = x_ref[...]
        o_ref[...] += jnp.sum(jnp.sum(v * v, axis=1, keepdims=True), axis=0, keepdims=True)

    return pl.pallas_call(
        body, name=name, grid=(T // tr,), in_specs=[pl.BlockSpec((tr, D), lambda i: (i, 0))],
        out_specs=pl.BlockSpec((1, LANES), lambda i: (0, 0)), out_shape=jax.ShapeDtypeStruct((1, LANES), F32),
        compiler_params=_params(("arbitrary",)))(x)


def exchange(arrays, scatter, *, name):
    n = len(arrays)
    out_shape = [jax.ShapeDtypeStruct(a.shape if sc else (N_DEV,) + a.shape, a.dtype) for a, sc in zip(arrays, scatter)]

    def body(*refs):
        ins, outs = refs[:n], refs[n:2 * n]
        send_sems, recv_sems, local_sems = refs[2 * n:]
        x, y, c = lax.axis_index("x"), lax.axis_index("y"), lax.axis_index("c")
        me = 4 * x + 2 * y + c
        peers = []
        for r in range(1, N_DEV):
            px = 1 - x if r & 4 else x
            py = 1 - y if r & 2 else y
            pc = 1 - c if r & 1 else c
            peers.append((r, (px, py, pc), 4 * px + 2 * py + pc))
        copies = []
        for a in range(n):
            src_me = ins[a].at[me] if scatter[a] else ins[a]
            local = pltpu.make_async_copy(src_me, outs[a].at[me], local_sems.at[a])
            local.start()
            copies.append(local)
        sends = []
        for a in range(n):
            for r, dev, lin in peers:
                cp = pltpu.make_async_remote_copy(
                    src_ref=ins[a].at[lin] if scatter[a] else ins[a], dst_ref=outs[a].at[me],
                    send_sem=send_sems.at[a, r - 1], recv_sem=recv_sems.at[a, r - 1], device_id=dev, device_id_type=MESH)
                cp.start()
                sends.append(cp)
        for a in range(n):
            for r, dev, lin in peers:
                pltpu.make_async_remote_copy(
                    src_ref=ins[a].at[lin] if scatter[a] else ins[a], dst_ref=outs[a].at[lin],
                    send_sem=send_sems.at[a, r - 1], recv_sem=recv_sems.at[a, r - 1], device_id=dev,
                    device_id_type=MESH).wait_recv()
        for cp in sends:
            cp.wait_send()
        for cp in copies:
            cp.wait()

    any_spec = pl.BlockSpec(memory_space=pl.ANY)
    return pl.pallas_call(
        body, name=name, in_specs=[any_spec] * n, out_specs=[any_spec] * n, out_shape=out_shape,
        scratch_shapes=[pltpu.SemaphoreType.DMA((n, N_DEV - 1)), pltpu.SemaphoreType.DMA((n, N_DEV - 1)),
                        pltpu.SemaphoreType.DMA((n,))])(*arrays)


def adamw(parts, w, m, v, *, name, tr=128):
    R, Cc = w.shape
    tr = min(tr, R)
    assert R % tr == 0

    def body(p_ref, w_ref, m_ref, v_ref, g_ref, d_ref, nm_ref, nv_ref):
        g = p_ref[0]
        for s in range(1, N_DEV):
            g = g + p_ref[s]
        nm = ADAM_B1 * m_ref[...] + (1.0 - ADAM_B1) * g
        nv = ADAM_B2 * v_ref[...] + (1.0 - ADAM_B2) * (g * g)
        m_hat = nm / (1.0 - ADAM_B1 ** ADAM_STEP)
        v_hat = nv / (1.0 - ADAM_B2 ** ADAM_STEP)
        g_ref[...] = g
        d_ref[...] = -ADAM_LR * (m_hat / (jnp.sqrt(v_hat) + ADAM_EPS) + ADAM_WD * w_ref[...])
        nm_ref[...] = nm
        nv_ref[...] = nv

    spec = pl.BlockSpec((tr, Cc), lambda i: (i, 0))
    return pl.pallas_call(
        body, name=name, grid=(R // tr,),
        in_specs=[pl.BlockSpec((N_DEV, tr, Cc), lambda i: (0, i, 0)), spec, spec, spec], out_specs=[spec] * 4,
        out_shape=[jax.ShapeDtypeStruct((R, Cc), F32)] * 4, compiler_params=_params(("parallel",)))(parts, w, m, v)


def _lanes(vec, base):
    return jnp.zeros((1, LANES), F32).at[0, base:base + vec.shape[0]].set(vec)


def _col_shards(full_w):
    R, Ct = full_w.shape
    return jnp.transpose(full_w.reshape(R, N_DEV, Ct // N_DEV), (1, 0, 2))


def _from_col_shards(g):
    return jnp.transpose(g, (1, 0, 2)).reshape(g.shape[1], -1)


def kernel(x, mem, g_mix, w_in, conv_w, a_log, dt_bias, gdn_norm_g, fox_b_f, fox_q_norm, fox_k_norm, g_mem, w_mem_kv, mem_q_norm, mem_k_norm, w_up_gdn, w_up_fox, w_up_mem, w_out, g_mlp, w_ff1, w_ff2, loss_target, m_g_mix, m_w_in, m_conv_w, m_a_log, m_dt_bias, m_gdn_norm_g, m_fox_b_f, m_fox_q_norm, m_fox_k_norm, m_g_mem, m_w_mem_kv, m_mem_q_norm, m_mem_k_norm, m_w_up_gdn, m_w_up_fox, m_w_up_mem, m_w_out, m_g_mlp, m_w_ff1, m_w_ff2, v_g_mix, v_w_in, v_conv_w, v_a_log, v_dt_bias, v_gdn_norm_g, v_fox_b_f, v_fox_q_norm, v_fox_k_norm, v_g_mem, v_w_mem_kv, v_mem_q_norm, v_mem_k_norm, v_w_up_gdn, v_w_up_fox, v_w_up_mem, v_w_out, v_g_mlp, v_w_ff1, v_w_ff2):
    loc = dict(locals())
    big_names = ["w_in", "conv_w", "w_mem_kv", "w_up_gdn", "w_up_fox", "w_up_mem", "w_out", "w_ff1", "w_ff2"]
    col_sharded = {"w_in", "conv_w", "w_up_gdn", "w_up_fox", "w_up_mem", "w_ff1"}
    small_names = ["g_mix", "a_log", "dt_bias", "gdn_norm_g", "fox_b_f", "fox_q_norm", "fox_k_norm", "g_mem",
                   "mem_q_norm", "mem_k_norm", "g_mlp"]

    xs, tgt, mems = x[0], loss_target[0], mem[0]
    T, D = xs.shape
    HG = a_log.shape[1]
    HF = fox_b_f.shape[1]
    DM = mem_q_norm.shape[1]
    GQK, GV = HG * HEAD, HG * HEAD
    GQKV = 2 * GQK + GV
    FW = HF * HEAD
    MW = w_mem_kv.shape[2] // 2
    HM = MW // DM
    assert HG <= 8 and HF <= 8

    shards = [loc[n][0].astype(BF16) for n in big_names]
    gathered = exchange(shards, [False] * len(shards), name="gather_weights")
    W = {}
    for n, g in zip(big_names, gathered):
        W[n] = _from_col_shards(g) if n in col_sharded else g.reshape(-1, g.shape[2])
    widths = [GQKV, GV, HG, HG, FW, FW, FW, HF, MW, 3 * D]
    offs = np.concatenate([[0], np.cumsum(widths)]).tolist()
    seg = [W["w_in"][:, offs[i]:offs[i + 1]] for i in range(len(widths))]
    w_big = jnp.concatenate([seg[0], seg[1], seg[4], seg[5], seg[6], seg[8], seg[9]], axis=1)
    pad8 = lambda s: jnp.pad(s, ((0, 0), (0, 8 - s.shape[1])))
    w_small = jnp.concatenate([pad8(seg[2]), pad8(seg[3]), pad8(seg[7]), jnp.zeros((D, LANES - 24), BF16)], axis=1)
    o_z, o_fq, o_fk, o_fv = GQKV, GQKV + GV, GQKV + GV + FW, GQKV + GV + 2 * FW
    o_mq = o_fv + FW
    o_gt = o_mq + MW
    WB = o_gt + 3 * D
    conv_full = W["conv_w"].astype(F32)

    a_log_l, dt_bias_l, b_f_l = _lanes(a_log[0], LANE_A), _lanes(dt_bias[0], LANE_A), _lanes(fox_b_f[0], LANE_F)
    rms_fn = lambda t, g: (_rms(t, g),)

    h, = rowwise(rms_fn, [full(xs)], [g_mix], [(D, D, BF16)], name="rms_mix")
    big = matmul(h, w_big, mode="nn", name="proj_big")
    small = matmul(h, w_small, mode="nn", name="proj_small", out_dtypes=(F32,))
    bgf, = rowwise(_gates_fn, [full(small)], [a_log_l, dt_bias_l, b_f_l], [(LANES, LANES, F32)], name="gates")

    cn = conv_fwd(big, conv_full, HG, name="conv")
    o_gdn, ss = gdn_fwd(cn, bgf, HG, name="gdn_fwd")
    oa, = rowwise(_gated_norm_fn, [(o_gdn, HEAD, 0, 1), (big, HEAD, o_z // HEAD, 1)], [gdn_norm_g],
                  [(GV, HEAD, BF16)], name="gated_norm", ncb=HG)

    fqn, = rowwise(rms_fn, [(big, HEAD, o_fq // HEAD, 1)], [fox_q_norm], [(FW, HEAD, BF16)], name="fox_qnorm", ncb=HF)
    fkn, = rowwise(rms_fn, [(big, HEAD, o_fk // HEAD, 1)], [fox_k_norm], [(FW, HEAD, BF16)], name="fox_knorm", ncb=HF)
    fv = big[:, o_fv:o_fv + FW]
    cum = cumsum_tokens(bgf, reverse=False, name="cumsum")
    cum_t = jnp.zeros((8, T), F32).at[:HF].set(cum[:, LANE_F:LANE_F + HF].T)
    ob, lse = fox_fwd(fqn, fkn, fv, cum, cum_t, HF, name="fox_fwd")

    memn, = rowwise(rms_fn, [full(mems)], [g_mem], [(D, D, BF16)], name="rms_mem")
    kv_m = matmul(memn, W["w_mem_kv"], mode="nn", name="mem_kv", out_dtypes=(F32,))
    kmn, = rowwise(rms_fn, [(kv_m, DM, 0, 1)], [mem_k_norm], [(MW, DM, F32)], name="mem_knorm", ncb=HM)
    vm = kv_m[:, MW:]
    mem_fn = functools.partial(_mem_attn_fn, HM, DM)
    om, = rowwise(mem_fn, [(big, MW, o_mq // MW, 0)], [kmn, vm, mem_q_norm], [(MW, MW, BF16)], name="mem_attn")

    ya = matmul(oa, W["w_up_gdn"], mode="nn", name="up_gdn")
    yb = matmul(ob, W["w_up_fox"], mode="nn", name="up_fox")
    ym = matmul(om, W["w_up_mem"], mode="nn", name="up_mem")
    cbm = min(512, D)
    gate_rows = [(big, cbm, (o_gt + b * D) // cbm, 1) for b in range(3)]
    merge_rows = [(ya, cbm, 0, 1), (yb, cbm, 0, 1), (ym, cbm, 0, 1)] + gate_rows
    y, = rowwise(_merge_fn, merge_rows, [], [(D, cbm, BF16)], name="merge", ncb=D // cbm)
    x1 = matmul(y, W["w_out"], mode="nn", name="out_proj", out_dtypes=(F32,), extras=(xs,),
                epi=lambda r, res: (r + res,))

    h2, = rowwise(rms_fn, [full(x1)], [g_mlp], [(D, D, BF16)], name="rms_mlp")
    u_ff, a_ff = matmul(h2, W["w_ff1"], mode="nn", name="ff1", out_dtypes=(BF16, BF16),
                        epi=lambda r: (r, jnp.square(jnp.maximum(r, 0.0))))
    d_out = matmul(a_ff, W["w_ff2"], mode="nn", name="ff2_loss", out_dtypes=(F32,), extras=(x1, tgt),
                   epi=lambda r, res, t: ((r + res - t) * (1.0 / D),))
    loss_local = 0.5 * D * sum_squares(d_out, name="loss_sum")[0, 0]
    loss = lax.psum(loss_local, ("x", "y", "c"))

    G = {}
    d_u = matmul(d_out, W["w_ff2"], mode="nt", name="d_ff2_in", extras=(u_ff,),
                 epi=lambda r, u: (r * 2.0 * jnp.maximum(u.astype(F32), 0.0),))
    G["w_ff2"] = matmul(a_ff, d_out, mode="tn", name="d_w_ff2", out_dtypes=(F32,))
    d_h2 = matmul(d_u, W["w_ff1"], mode="nt", name="d_ff1_in")
    G["w_ff1"] = matmul(h2, d_u, mode="tn", name="d_w_ff1", out_dtypes=(F32,))
    d_x1, G["g_mlp"] = rowwise_bwd(rms_fn, [full(x1)], [g_mlp], [full(d_h2)], [(0, F32)], [0], name="d_rms_mlp",
                                   adds=[full(d_out)])
    d_y = matmul(d_x1, W["w_out"], mode="nt", name="d_out_proj_in")
    G["w_out"] = matmul(y, d_x1, mode="tn", name="d_w_out", out_dtypes=(F32,))
    d_ya, d_yb, d_ym, d_ga, d_gb, d_gm = rowwise_bwd(
        _merge_fn, merge_rows, [], [(d_y, cbm, 0, 1)], [(k, BF16) for k in range(6)], [], name="d_merge", ncb=D // cbm)
    d_oa = matmul(d_ya, W["w_up_gdn"], mode="nt", name="d_up_gdn_in")
    d_ob = matmul(d_yb, W["w_up_fox"], mode="nt", name="d_up_fox_in")
    d_om = matmul(d_ym, W["w_up_mem"], mode="nt", name="d_up_mem_in")
    G["w_up_gdn"] = matmul(oa, d_ya, mode="tn", name="d_w_up_gdn", out_dtypes=(F32,))
    G["w_up_fox"] = matmul(ob, d_yb, mode="tn", name="d_w_up_fox", out_dtypes=(F32,))
    G["w_up_mem"] = matmul(om, d_ym, mode="tn", name="d_w_up_mem", out_dtypes=(F32,))

    d_mq, d_kmn, d_vm, G["mem_q_norm"] = rowwise_bwd(
        mem_fn, [(big, MW, o_mq // MW, 0)], [kmn, vm, mem_q_norm], [full(d_om)], [(0, BF16)], [0, 1, 2], name="d_mem_attn")
    d_km, G["mem_k_norm"] = rowwise_bwd(rms_fn, [(kv_m, DM, 0, 1)], [mem_k_norm], [(d_kmn, DM, 0, 1)], [(0, F32)], [0],
                                         name="d_mem_knorm", ncb=HM)
    d_kv_m = jnp.concatenate([d_km, d_vm], axis=1)
    G["w_mem_kv"] = matmul(memn, d_kv_m, mode="tn", name="d_w_mem_kv", out_dtypes=(F32,))
    d_memn = matmul(d_kv_m, W["w_mem_kv"], mode="nt", name="d_mem_kv_in")
    _, G["g_mem"] = rowwise_bwd(rms_fn, [full(mems)], [g_mem], [full(d_memn)], [(0, BF16)], [0], name="d_rms_mem")

    d_fkn, d_fv, d_cum_t = fox_bwd_kv(fqn, fkn, fv, d_ob, ob, cum, cum_t, lse, HF, name="fox_bwd_kv")
    d_fqn = fox_bwd_q(fqn, fkn, fv, d_ob, ob, cum, cum_t, lse, HF, name="fox_bwd_q")
    d_fq, G["fox_q_norm"] = rowwise_bwd(rms_fn, [(big, HEAD, o_fq // HEAD, 1)], [fox_q_norm], [(d_fqn, HEAD, 0, 1)],
                                         [(0, BF16)], [0], name="d_fox_qnorm", ncb=HF)
    d_fk, G["fox_k_norm"] = rowwise_bwd(rms_fn, [(big, HEAD, o_fk // HEAD, 1)], [fox_k_norm], [(d_fkn, HEAD, 0, 1)],
                                         [(0, BF16)], [0], name="d_fox_knorm", ncb=HF)
    d_cum = jnp.zeros((T, LANES), F32).at[:, LANE_F:LANE_F + HF].set(d_cum_t[:HF].T)
    d_logf = cumsum_tokens(d_cum, reverse=True, name="cumsum_rev")

    d_o_gdn, d_z, G["gdn_norm_g"] = rowwise_bwd(
        _gated_norm_fn, [(o_gdn, HEAD, 0, 1), (big, HEAD, o_z // HEAD, 1)], [gdn_norm_g], [(d_oa, HEAD, 0, 1)],
        [(0, F32), (1, BF16)], [0], name="d_gated_norm", ncb=HG)
    d_cn, d_bg = gdn_bwd(cn, bgf, ss, d_o_gdn, HG, name="gdn_bwd")
    d_conv_y, G["conv_w"] = conv_bwd_taps(big, conv_full, d_cn, HG, name="d_conv_taps")
    d_qkv = conv_bwd_input(d_conv_y, conv_full, name="d_conv_in")
    d_small, d_al, d_dt, d_bf = rowwise_bwd(_gates_fn, [full(small)], [a_log_l, dt_bias_l, b_f_l], [full(d_bg + d_logf)],
                                            [(0, F32)], [0, 1, 2], name="d_gates")
    G["a_log"], G["dt_bias"], G["fox_b_f"] = (d_al[:, LANE_A:LANE_A + HG], d_dt[:, LANE_A:LANE_A + HG],
                                               d_bf[:, LANE_F:LANE_F + HF])

    d_big = jnp.concatenate([d_qkv, d_z, d_fq, d_fk, d_fv, d_mq, d_ga, d_gb, d_gm], axis=1)
    d_h_s = matmul(d_small, w_small, mode="nt", name="d_proj_small_in", out_dtypes=(F32,))
    d_h = matmul(d_big, w_big, mode="nt", name="d_proj_big_in", extras=(d_h_s,), epi=lambda r, e: (r + e,))
    g_big = matmul(h, d_big, mode="tn", name="d_w_big", out_dtypes=(F32,))
    g_small = matmul(h, d_small, mode="tn", name="d_w_small", out_dtypes=(F32,))
    grad_x, G["g_mix"] = rowwise_bwd(rms_fn, [full(xs)], [g_mix], [full(d_h)], [(0, F32)], [0], name="d_rms_mix",
                                     adds=[full(d_x1)])
    grad_x = grad_x[None]
    cols = lambda a, o, wd: a[:, o:o + wd]
    G["w_in"] = jnp.concatenate([
        cols(g_big, 0, GQKV), cols(g_big, o_z, GV), cols(g_small, LANE_B, HG), cols(g_small, LANE_A, HG),
        cols(g_big, o_fq, FW), cols(g_big, o_fk, FW), cols(g_big, o_fv, FW), cols(g_small, LANE_F, HF),
        cols(g_big, o_mq, MW), cols(g_big, o_gt, 3 * D)], axis=1)

    parts = [_col_shards(G[n]) if n in col_sharded else G[n].reshape(N_DEV, -1, G[n].shape[1]) for n in big_names]
    small_sizes = [loc[n].shape[1] for n in small_names]
    pack = lambda d: jnp.concatenate([d[n].reshape(1, -1) for n in small_names], axis=1)
    npad = -sum(small_sizes) % LANES
    padp = lambda a: jnp.pad(a, ((0, 0), (0, npad)))
    g_small_packed = padp(pack(G))
    recv = exchange(parts + [g_small_packed], [True] * len(parts) + [False], name="exchange_grads")

    res = {}
    for n, p in zip(big_names, recv[:-1]):
        res[n] = [t[None] for t in adamw(p, loc[n][0], loc["m_" + n][0], loc["v_" + n][0], name="adamw_" + n)]
    sm = adamw(recv[-1], padp(pack({n: loc[n] for n in small_names})), padp(pack({n: loc["m_" + n] for n in small_names})),
               padp(pack({n: loc["v_" + n] for n in small_names})), name="adamw_small")
    so = np.concatenate([[0], np.cumsum(small_sizes)]).tolist()
    for i, n in enumerate(small_names):
        res[n] = [t[:, so[i]:so[i + 1]] for t in sm]

    order = ["g_mix", "w_in", "conv_w", "a_log", "dt_bias", "gdn_norm_g", "fox_b_f", "fox_q_norm", "fox_k_norm", "g_mem",
             "w_mem_kv", "mem_q_norm", "mem_k_norm", "w_up_gdn", "w_up_fox", "w_up_mem", "w_out", "g_mlp", "w_ff1", "w_ff2"]
    return (loss, grad_x, *[res[n][0] for n in order], *[res[n][1] for n in order],
            *[res[n][2] for n in order], *[res[n][3] for n in order])
```

```python
import functools

import jax
import jax.numpy as jnp
import numpy as np
from jax import lax
from jax.experimental import pallas as pl
from jax.experimental.pallas import tpu as pltpu

F32 = jnp.float32
BF16 = jnp.bfloat16
HI = lax.Precision.HIGHEST

EPS = 1e-6
GDN_CHUNK = 64
GDN_CONV = 4
HEAD = 128
LANES = 128
HALO = 16
N_DEV = 8
MESH = pl.DeviceIdType.MESH
VMEM_LIMIT_V7X = 56 * 1024 * 1024

ADAM_LR, ADAM_B1, ADAM_B2, ADAM_EPS, ADAM_WD, ADAM_STEP = 0.001, 0.9, 0.999, 1e-08, 0.01, 10

LANE_B, LANE_A, LANE_F = 0, 8, 16


def _params(sem):
    return pltpu.CompilerParams(dimension_semantics=sem, vmem_limit_bytes=VMEM_LIMIT_V7X)


def _dg(a, b, ca, cb, prec):
    nb = a.ndim - 2
    batch = tuple(range(nb))
    return lax.dot_general(a, b, (((ca + nb,), (cb + nb,)), (batch, batch)), precision=prec,
                           preferred_element_type=F32)


def _make_mm(prec, cast):
    def c(x):
        return x.astype(BF16) if cast else x

    @jax.custom_vjp
    def nn(a, b):
        return _dg(c(a), c(b), 1, 0, prec)

    @jax.custom_vjp
    def nt(a, b):
        return _dg(c(a), c(b), 1, 1, prec)

    @jax.custom_vjp
    def tn(a, b):
        return _dg(c(a), c(b), 0, 0, prec)

    nn.defvjp(lambda a, b: (nn(a, b), (a, b)), lambda r, g: (nt(g, r[1]), tn(r[0], g)))
    nt.defvjp(lambda a, b: (nt(a, b), (a, b)), lambda r, g: (nn(g, r[1]), tn(g, r[0])))
    tn.defvjp(lambda a, b: (tn(a, b), (a, b)), lambda r, g: (nt(r[1], g), nn(r[0], g)))
    return nn, nt, tn


NN, NT, TN = _make_mm(None, True)
NNH, NTH, TNH = _make_mm(lax.Precision.HIGH, False)
NNX, _, _ = _make_mm(HI, False)


def _sigmoid(x):
    return 1.0 / (1.0 + jnp.exp(-x))


def _silu(x):
    return x * _sigmoid(x)


def _softplus(x):
    return jnp.maximum(x, 0.0) + jnp.log(1.0 + jnp.exp(-jnp.abs(x)))


def _log_sigmoid(x):
    return -_softplus(-x)


def _rms(x, g):
    return x * lax.rsqrt(jnp.mean(x * x, axis=-1, keepdims=True) + EPS) * g


def _tile(n, target):
    t = target
    while t >= LANES:
        if n % t == 0:
            return t
        t //= 2
    return n


def matmul(a, b, *, mode, name, out_dtypes=(BF16,), epi=None, extras=(), tm=1024, tn=1024, tk=512):
    if mode == "nn":
        (M, K), (K2, N) = a.shape, b.shape
    elif mode == "nt":
        (M, K), (N, K2) = a.shape, b.shape
    else:
        (K, M), (K2, N) = a.shape, b.shape
    assert K == K2, (name, a.shape, b.shape)
    tm, tn, tk = _tile(M, tm), _tile(N, tn), _tile(K, tk)
    nk = K // tk
    a_spec = (pl.BlockSpec((tk, tm), lambda i, j, k: (k, i)) if mode == "tn"
              else pl.BlockSpec((tm, tk), lambda i, j, k: (i, k)))
    b_spec = (pl.BlockSpec((tn, tk), lambda i, j, k: (j, k)) if mode == "nt"
              else pl.BlockSpec((tk, tn), lambda i, j, k: (k, j)))
    o_spec = pl.BlockSpec((tm, tn), lambda i, j, k: (i, j))
    dims = {"nn": ((1,), (0,)), "nt": ((1,), (1,)), "tn": ((0,), (0,))}[mode]
    ne, no = len(extras), len(out_dtypes)

    def body(a_ref, b_ref, *rest):
        ex, outs, acc = rest[:ne], rest[ne:ne + no], rest[-1]
        k = pl.program_id(2)

        @pl.when(k == 0)
        def _():
            acc[...] = jnp.zeros_like(acc)

        acc[...] += lax.dot_general(a_ref[...].astype(BF16), b_ref[...].astype(BF16), (dims, ((), ())),
                                    preferred_element_type=F32)

        @pl.when(k == nk - 1)
        def _():
            r = acc[...]
            vals = epi(r, *[e[...] for e in ex]) if epi is not None else (r,)
            for o, v in zip(outs, vals):
                o[...] = v.astype(o.dtype)

    out = pl.pallas_call(
        body, name=name, grid=(M // tm, N // tn, nk),
        in_specs=[a_spec, b_spec] + [o_spec] * ne, out_specs=[o_spec] * no,
        out_shape=[jax.ShapeDtypeStruct((M, N), d) for d in out_dtypes],
        scratch_shapes=[pltpu.VMEM((tm, tn), F32)],
        compiler_params=_params(("parallel", "parallel", "arbitrary")))(a, b, *extras)
    return out[0] if no == 1 else out


def _row_spec(tr, cb, off, moves):
    return pl.BlockSpec((tr, cb), lambda i, j: (i, off + moves * j))


def _whole_spec(p):
    return pl.BlockSpec(p.shape, lambda i, j: (0,) * p.ndim)


def rowwise(fn, rows, params, outs, *, name, ncb=1, tr=512):
    T = rows[0][0].shape[0]
    tr = min(tr, T)
    assert T % tr == 0
    nr, npar = len(rows), len(params)

    def body(*refs):
        r, p, o = refs[:nr], refs[nr:nr + npar], refs[nr + npar:]
        vals = fn(*[x[...].astype(F32) for x in r], *[x[...] for x in p])
        for oref, v in zip(o, vals):
            oref[...] = v.astype(oref.dtype)

    res = pl.pallas_call(
        body, name=name, grid=(T // tr, ncb),
        in_specs=[_row_spec(tr, cb, off, mv) for (_, cb, off, mv) in rows] + [_whole_spec(p) for p in params],
        out_specs=[_row_spec(tr, cb, 0, 1) for (_, cb, _) in outs],
        out_shape=[jax.ShapeDtypeStruct((T, cols), d) for (cols, _, d) in outs],
        compiler_params=_params(("parallel", "parallel")))(*[r[0] for r in rows], *params)
    return res


def rowwise_bwd(fn, rows, params, cots, drows, dparams, *, name, ncb=1, tr=512, adds=()):
    T = rows[0][0].shape[0]
    tr = min(tr, T)
    assert T % tr == 0
    nr, npar, nc, ndr, na = len(rows), len(params), len(cots), len(drows), len(adds)

    def body(*refs):
        r, p, c = refs[:nr], refs[nr:nr + npar], refs[nr + npar:nr + npar + nc]
        base = nr + npar + nc + na
        ad, o_r, o_p = refs[base - na:base], refs[base:base + ndr], refs[base + ndr:]
        prim = [x[...].astype(F32) for x in r] + [x[...] for x in p]
        _, vjp = jax.vjp(lambda *a: tuple(fn(*a)), *prim)
        g = vjp(tuple(x[...].astype(F32) for x in c))
        for k, (oref, (idx, _)) in enumerate(zip(o_r, drows)):
            val = g[idx] + ad[k][...].astype(F32) if k < na else g[idx]
            oref[...] = val.astype(oref.dtype)
        first = jnp.logical_and(pl.program_id(0) == 0, pl.program_id(1) == 0)

        @pl.when(first)
        def _():
            for oref in o_p:
                oref[...] = jnp.zeros_like(oref)

        for oref, idx in zip(o_p, dparams):
            oref[...] += g[nr + idx]

    res = pl.pallas_call(
        body, name=name, grid=(T // tr, ncb),
        in_specs=([_row_spec(tr, cb, off, mv) for (_, cb, off, mv) in rows] + [_whole_spec(p) for p in params]
                  + [_row_spec(tr, cb, off, mv) for (_, cb, off, mv) in tuple(cots) + tuple(adds)]),
        out_specs=([_row_spec(tr, rows[idx][1], 0, 1) for (idx, _) in drows]
                   + [_whole_spec(params[idx]) for idx in dparams]),
        out_shape=([jax.ShapeDtypeStruct((T, ncb * rows[idx][1] if rows[idx][3] else rows[idx][1]), d)
                    for (idx, d) in drows]
                   + [jax.ShapeDtypeStruct(params[idx].shape, F32) for idx in dparams]),
        compiler_params=_params(("arbitrary", "arbitrary")))(
            *[r[0] for r in rows], *params, *[c[0] for c in cots], *[a[0] for a in adds])
    return res


def full(a):
    return (a, a.shape[1], 0, 0)


def cumsum_tokens(x, *, reverse, name, tb=256):
    T = x.shape[0]
    tb = min(tb, T)
    nb = T // tb
    idx = (lambda i: (nb - 1 - i, 0)) if reverse else (lambda i: (i, 0))

    def body(x_ref, o_ref, carry):
        @pl.when(pl.program_id(0) == 0)
        def _():
            carry[...] = jnp.zeros_like(carry)

        ii = lax.broadcasted_iota(jnp.int32, (tb, tb), 0)
        jj = lax.broadcasted_iota(jnp.int32, (tb, tb), 1)
        tri = ((ii <= jj) if reverse else (ii >= jj)).astype(F32)
        c = lax.dot_general(tri, x_ref[...], (((1,), (0,)), ((), ())), precision=HI,
                            preferred_element_type=F32) + carry[0:1, :]
        o_ref[...] = c
        carry[0:1, :] = c[0:1, :] if reverse else c[tb - 1:tb, :]

    return pl.pallas_call(
        body, name=name, grid=(nb,), in_specs=[pl.BlockSpec((tb, LANES), idx)],
        out_specs=pl.BlockSpec((tb, LANES), idx), out_shape=jax.ShapeDtypeStruct((T, LANES), F32),
        scratch_shapes=[pltpu.VMEM((8, LANES), F32)], compiler_params=_params(("arbitrary",)))(x)


def _conv_post(y, kind, dk):
    c = _silu(y)
    r = lax.rsqrt(jnp.sum(c * c, axis=-1, keepdims=True) + EPS)
    return jnp.where(kind == 0, c * r * (dk ** -0.5), jnp.where(kind == 1, c * r, c))


def _conv_taps(cur, prev, w, tr):
    ext = jnp.concatenate([prev, cur], axis=0)
    y = w[3:4, :] * cur
    for d in (1, 2, 3):
        y = y + w[3 - d:4 - d, :] * pltpu.roll(ext, d, 0)[HALO:HALO + tr]
    return y


def conv_fwd(big, w, n_qk_heads, *, name, tr=512):
    T, W = big.shape[0], w.shape[1]
    tr = min(tr, T)
    nh = W // HEAD

    def body(cur_ref, prev_ref, w_ref, o_ref):
        i, j = pl.program_id(0), pl.program_id(1)
        prev = jnp.where(i > 0, prev_ref[...].astype(F32), 0.0)
        y = _conv_taps(cur_ref[...].astype(F32), prev, w_ref[...], tr)
        kind = jnp.where(j < n_qk_heads, 0, jnp.where(j < 2 * n_qk_heads, 1, 2))
        o_ref[...] = _conv_post(y, kind, HEAD)

    return pl.pallas_call(
        body, name=name, grid=(T // tr, nh),
        in_specs=[pl.BlockSpec((tr, HEAD), lambda i, j: (i, j)),
                  pl.BlockSpec((HALO, HEAD), lambda i, j: (jnp.maximum(i * (tr // HALO) - 1, 0), j)),
                  pl.BlockSpec((GDN_CONV, HEAD), lambda i, j: (0, j))],
        out_specs=pl.BlockSpec((tr, HEAD), lambda i, j: (i, j)),
        out_shape=jax.ShapeDtypeStruct((T, W), F32), compiler_params=_params(("parallel", "parallel")))(big, big, w)


def conv_bwd_taps(big, w, dcn, n_qk_heads, *, name, tr=512):
    T, W = big.shape[0], w.shape[1]
    tr = min(tr, T)
    nh = W // HEAD

    def body(cur_ref, prev_ref, w_ref, g_ref, dy_ref, dw_ref):
        j, i = pl.program_id(0), pl.program_id(1)
        cur = cur_ref[...].astype(F32)
        prev = jnp.where(i > 0, prev_ref[...].astype(F32), 0.0)
        y = _conv_taps(cur, prev, w_ref[...], tr)
        kind = jnp.where(j < n_qk_heads, 0, jnp.where(j < 2 * n_qk_heads, 1, 2))
        _, vjp = jax.vjp(lambda t: _conv_post(t, kind, HEAD), y)
        dy, = vjp(g_ref[...])
        dy_ref[...] = dy
        ext = jnp.concatenate([prev, cur], axis=0)
        rows = [jnp.sum(dy * (cur if d == 0 else pltpu.roll(ext, d, 0)[HALO:HALO + tr]), axis=0, keepdims=True)
                for d in (3, 2, 1, 0)]

        @pl.when(i == 0)
        def _():
            dw_ref[...] = jnp.zeros_like(dw_ref)

        dw_ref[...] += jnp.concatenate(rows, axis=0)

    return pl.pallas_call(
        body, name=name, grid=(nh, T // tr),
        in_specs=[pl.BlockSpec((tr, HEAD), lambda j, i: (i, j)),
                  pl.BlockSpec((HALO, HEAD), lambda j, i: (jnp.maximum(i * (tr // HALO) - 1, 0), j)),
                  pl.BlockSpec((GDN_CONV, HEAD), lambda j, i: (0, j)),
                  pl.BlockSpec((tr, HEAD), lambda j, i: (i, j))],
        out_specs=[pl.BlockSpec((tr, HEAD), lambda j, i: (i, j)), pl.BlockSpec((GDN_CONV, HEAD), lambda j, i: (0, j))],
        out_shape=[jax.ShapeDtypeStruct((T, W), F32), jax.ShapeDtypeStruct((GDN_CONV, W), F32)],
        compiler_params=_params(("parallel", "arbitrary")))(big, big, w, dcn)


def conv_bwd_input(dy, w, *, name, tr=512):
    T, W = dy.shape
    tr = min(tr, T)
    nrow = T // tr

    def body(cur_ref, nxt_ref, w_ref, o_ref):
        i = pl.program_id(0)
        cur = cur_ref[...]
        nxt = jnp.where(i < nrow - 1, nxt_ref[...], 0.0)
        ext = jnp.concatenate([cur, nxt], axis=0)
        w = w_ref[...]
        dx = w[3:4, :] * cur
        for d in (1, 2, 3):
            dx = dx + w[3 - d:4 - d, :] * pltpu.roll(ext, tr + HALO - d, 0)[0:tr]
        o_ref[...] = dx.astype(o_ref.dtype)

    return pl.pallas_call(
        body, name=name, grid=(nrow, W // HEAD),
        in_specs=[pl.BlockSpec((tr, HEAD), lambda i, j: (i, j)),
                  pl.BlockSpec((HALO, HEAD), lambda i, j: (jnp.minimum((i + 1) * (tr // HALO), T // HALO - 1), j)),
                  pl.BlockSpec((GDN_CONV, HEAD), lambda i, j: (0, j))],
        out_specs=pl.BlockSpec((tr, HEAD), lambda i, j: (i, j)),
        out_shape=jax.ShapeDtypeStruct((T, W), BF16), compiler_params=_params(("parallel", "parallel")))(dy, dy, w)


def _gdn_chunk(q, k, v, bg, S):
    H, C = q.shape[0], q.shape[1]
    ii = lax.broadcasted_iota(jnp.int32, (C, C), 0)
    jj = lax.broadcasted_iota(jnp.int32, (C, C), 1)
    lincl = (ii >= jj).astype(F32)
    strict, incl, eye = (ii > jj)[None], (ii >= jj)[None], (ii == jj)[None]
    gam2d = lax.dot_general(lincl, bg, (((1,), (0,)), ((), ())), precision=HI, preferred_element_type=F32)
    lane = lax.broadcasted_iota(jnp.int32, (H, 1, LANES), 2)
    hh = lax.broadcasted_iota(jnp.int32, (H, 1, LANES), 0)
    beta = jnp.sum(bg[None] * (lane == hh + LANE_B).astype(F32), axis=2, keepdims=True)
    gam = jnp.sum(gam2d[None] * (lane == hh + LANE_A).astype(F32), axis=2, keepdims=True)
    last = (lax.broadcasted_iota(jnp.int32, (1, C, 1), 1) == C - 1).astype(F32)
    gam_last = jnp.sum(gam * last, axis=1, keepdims=True)
    gam_row = NNX(jnp.ones((H, C, C), F32), jnp.where(eye, gam, 0.0))
    diff = gam - gam_row
    dec_s = jnp.where(strict, jnp.exp(jnp.where(strict, diff, 0.0)), 0.0)
    dec_i = jnp.where(incl, jnp.exp(jnp.where(incl, diff, 0.0)), 0.0)
    p = -(beta * NT(k, k) * dec_s)
    t = jnp.where(eye, 1.0, 0.0) + p
    for _ in range(5):
        p = NNH(p, p)
        t = t + NNH(t, p)
    eg = jnp.exp(gam)
    w = NNH(t, beta * eg * k)
    u0 = NNH(t, beta * v)
    qk = NT(q, k) * dec_i
    u = u0 - NN(w, S)
    o = NN(q * eg, S) + NN(qk, u)
    S2 = jnp.exp(gam_last) * S + TN(k * jnp.exp(gam_last - gam), u)
    return o, S2


def _heads(x, base, H):
    return jnp.stack([x[:, base + h * HEAD:base + (h + 1) * HEAD] for h in range(H)])


def _unheads(x):
    return jnp.concatenate([x[h] for h in range(x.shape[0])], axis=1)


def gdn_fwd(cn, bgf, H, *, name):
    T, C, W = cn.shape[0], GDN_CHUNK, H * HEAD
    N = T // C

    def body(cn_ref, bg_ref, o_ref, ss_ref, s_scr):
        @pl.when(pl.program_id(0) == 0)
        def _():
            s_scr[...] = jnp.zeros_like(s_scr)

        x, S = cn_ref[...], s_scr[...]
        ss_ref[0] = S
        o, S2 = _gdn_chunk(_heads(x, 0, H), _heads(x, W, H), _heads(x, 2 * W, H), bg_ref[...], S)
        o_ref[...] = _unheads(o)
        s_scr[...] = S2

    return pl.pallas_call(
        body, name=name, grid=(N,),
        in_specs=[pl.BlockSpec((C, 3 * W), lambda n: (n, 0)), pl.BlockSpec((C, LANES), lambda n: (n, 0))],
        out_specs=[pl.BlockSpec((C, W), lambda n: (n, 0)), pl.BlockSpec((1, H, HEAD, HEAD), lambda n: (n, 0, 0, 0))],
        out_shape=[jax.ShapeDtypeStruct((T, W), F32), jax.ShapeDtypeStruct((N, H, HEAD, HEAD), F32)],
        scratch_shapes=[pltpu.VMEM((H, HEAD, HEAD), F32)], compiler_params=_params(("arbitrary",)))(cn, bgf)


def gdn_bwd(cn, bgf, ss, do, H, *, name):
    T, C, W = cn.shape[0], GDN_CHUNK, H * HEAD
    N = T // C

    def body(cn_ref, bg_ref, ss_ref, do_ref, dcn_ref, dbg_ref, ds_scr):
        @pl.when(pl.program_id(0) == 0)
        def _():
            ds_scr[...] = jnp.zeros_like(ds_scr)

        x = cn_ref[...]
        _, vjp = jax.vjp(_gdn_chunk, _heads(x, 0, H), _heads(x, W, H), _heads(x, 2 * W, H), bg_ref[...], ss_ref[0])
        dq, dk, dv, dbg, dS = vjp((_heads(do_ref[...], 0, H), ds_scr[...]))
        dcn_ref[...] = jnp.concatenate([_unheads(dq), _unheads(dk), _unheads(dv)], axis=1)
        dbg_ref[...] = dbg
        ds_scr[...] = dS

    rev = lambda n: (N - 1 - n, 0)
    return pl.pallas_call(
        body, name=name, grid=(N,),
        in_specs=[pl.BlockSpec((C, 3 * W), rev), pl.BlockSpec((C, LANES), rev),
                  pl.BlockSpec((1, H, HEAD, HEAD), lambda n: (N - 1 - n, 0, 0, 0)), pl.BlockSpec((C, W), rev)],
        out_specs=[pl.BlockSpec((C, 3 * W), rev), pl.BlockSpec((C, LANES), rev)],
        out_shape=[jax.ShapeDtypeStruct((T, 3 * W), F32), jax.ShapeDtypeStruct((T, LANES), F32)],
        scratch_shapes=[pltpu.VMEM((H, HEAD, HEAD), F32)], compiler_params=_params(("arbitrary",)))(cn, bgf, ss, do)


def _fox_scores(q, k, cq, ck, i, j, tq, tk):
    s = lax.dot_general(q, k, (((1,), (1,)), ((), ())), preferred_element_type=F32) * (HEAD ** -0.5) + cq - ck
    rows = i * tq + lax.broadcasted_iota(jnp.int32, (tq, tk), 0)
    cols = j * tk + lax.broadcasted_iota(jnp.int32, (tq, tk), 1)
    return s, rows >= cols


def fox_fwd(q, k, v, cum, cum_t, H, *, name, tq=512, tk=512):
    T, W = q.shape
    tq, tk = min(tq, T), min(tk, T)
    nq, nk = T // tq, T // tk
    assert tq == tk

    def body(q_ref, k_ref, v_ref, cq_ref, ck_ref, o_ref, lse_ref, acc, m_scr, l_scr):
        i, j = pl.program_id(0), pl.program_id(1)

        @pl.when(j == 0)
        def _():
            acc[...] = jnp.zeros_like(acc)
            m_scr[...] = jnp.full_like(m_scr, -jnp.inf)
            l_scr[...] = jnp.zeros_like(l_scr)

        @pl.when(j <= i)
        def _():
            lane = lax.broadcasted_iota(jnp.int32, (tq, LANES), 1)
            m_all, l_all = m_scr[...], l_scr[...]
            new_m, new_l = m_all, l_all
            for h in range(H):
                hs = slice(h * HEAD, (h + 1) * HEAD)
                s, mask = _fox_scores(q_ref[:, hs], k_ref[:, hs], cq_ref[:, LANE_F + h:LANE_F + h + 1],
                                      ck_ref[h:h + 1, :], i, j, tq, tk)
                s = jnp.where(mask, s, -jnp.inf)
                m_prev, l_prev = m_all[:, h:h + 1], l_all[:, h:h + 1]
                m_new = jnp.maximum(m_prev, jnp.max(s, axis=1, keepdims=True))
                p = jnp.exp(s - m_new)
                alpha = jnp.exp(m_prev - m_new)
                p_hi = p.astype(BF16)
                p_lo = (p - p_hi.astype(F32)).astype(BF16)
                pv = lambda t: lax.dot_general(t, v_ref[:, hs], (((1,), (0,)), ((), ())), preferred_element_type=F32)
                acc[:, hs] = alpha * acc[:, hs] + (pv(p_hi) + pv(p_lo))
                new_m = jnp.where(lane == h, m_new, new_m)
                new_l = jnp.where(lane == h, alpha * l_prev + jnp.sum(p, axis=1, keepdims=True), new_l)
            m_scr[...] = new_m
            l_scr[...] = new_l

        @pl.when(j == nk - 1)
        def _():
            l_all = l_scr[...]
            for h in range(H):
                hs = slice(h * HEAD, (h + 1) * HEAD)
                o_ref[:, hs] = (acc[:, hs] / l_all[:, h:h + 1]).astype(o_ref.dtype)
            lse_ref[...] = m_scr[...] + jnp.log(l_all)

    kv_idx = lambda i, j: (jnp.minimum(j, i), 0)
    return pl.pallas_call(
        body, name=name, grid=(nq, nk),
        in_specs=[pl.BlockSpec((tq, W), lambda i, j: (i, 0)), pl.BlockSpec((tk, W), kv_idx), pl.BlockSpec((tk, W), kv_idx),
                  pl.BlockSpec((tq, LANES), lambda i, j: (i, 0)), pl.BlockSpec((8, tk), lambda i, j: (0, jnp.minimum(j, i)))],
        out_specs=[pl.BlockSpec((tq, W), lambda i, j: (i, 0)), pl.BlockSpec((tq, LANES), lambda i, j: (i, 0))],
        out_shape=[jax.ShapeDtypeStruct((T, W), F32), jax.ShapeDtypeStruct((T, LANES), F32)],
        scratch_shapes=[pltpu.VMEM((tq, W), F32), pltpu.VMEM((tq, LANES), F32), pltpu.VMEM((tq, LANES), F32)],
        compiler_params=_params(("parallel", "arbitrary")))(q, k, v, cum, cum_t)


def _fox_ds(q_ref, k_ref, v_ref, do_ref, o_ref, cq_ref, ck_ref, lse_ref, h, i, j, tq, tk):
    hs = slice(h * HEAD, (h + 1) * HEAD)
    s, mask = _fox_scores(q_ref[:, hs], k_ref[:, hs], cq_ref[:, LANE_F + h:LANE_F + h + 1], ck_ref[h:h + 1, :],
                          i, j, tq, tk)
    p = jnp.where(mask, jnp.exp(s - lse_ref[:, h:h + 1]), 0.0)
    do = do_ref[:, hs]
    dp = lax.dot_general(do, v_ref[:, hs], (((1,), (1,)), ((), ())), preferred_element_type=F32)
    delta = jnp.sum(do.astype(F32) * o_ref[:, hs].astype(F32), axis=1, keepdims=True)
    return p, p * (dp - delta)


def fox_bwd_kv(q, k, v, do, o, cum, cum_t, lse, H, *, name, tq=512, tk=512):
    T, W = q.shape
    tq, tk = min(tq, T), min(tk, T)
    nq, nk = T // tq, T // tk

    def body(q_ref, k_ref, v_ref, do_ref, o_ref, cq_ref, ck_ref, lse_ref, dk_ref, dv_ref, dc_ref, dk_acc, dv_acc, dc_acc):
        j, i = pl.program_id(0), pl.program_id(1)

        @pl.when(i == 0)
        def _():
            dk_acc[...] = jnp.zeros_like(dk_acc)
            dv_acc[...] = jnp.zeros_like(dv_acc)
            dc_acc[...] = jnp.zeros_like(dc_acc)

        @pl.when(i >= j)
        def _():
            row = lax.broadcasted_iota(jnp.int32, (8, tk), 0)
            dc = dc_acc[...]
            for h in range(H):
                hs = slice(h * HEAD, (h + 1) * HEAD)
                p, ds = _fox_ds(q_ref, k_ref, v_ref, do_ref, o_ref, cq_ref, ck_ref, lse_ref, h, i, j, tq, tk)
                dv_acc[:, hs] += lax.dot_general(p.astype(BF16), do_ref[:, hs], (((0,), (0,)), ((), ())),
                                                 preferred_element_type=F32)
                dk_acc[:, hs] += lax.dot_general(ds.astype(BF16), q_ref[:, hs], (((0,), (0,)), ((), ())),
                                                 preferred_element_type=F32) * (HEAD ** -0.5)
                dc = jnp.where(row == h, dc - jnp.sum(ds, axis=0, keepdims=True), dc)
            dc_acc[...] = dc

        @pl.when(i == nq - 1)
        def _():
            dk_ref[...] = dk_acc[...].astype(dk_ref.dtype)
            dv_ref[...] = dv_acc[...].astype(dv_ref.dtype)
            dc_ref[...] = dc_acc[...]

    q_idx = lambda j, i: (jnp.maximum(i, j), 0)
    kv_idx = lambda j, i: (j, 0)
    return pl.pallas_call(
        body, name=name, grid=(nk, nq),
        in_specs=[pl.BlockSpec((tq, W), q_idx), pl.BlockSpec((tk, W), kv_idx), pl.BlockSpec((tk, W), kv_idx),
                  pl.BlockSpec((tq, W), q_idx), pl.BlockSpec((tq, W), q_idx), pl.BlockSpec((tq, LANES), q_idx),
                  pl.BlockSpec((8, tk), lambda j, i: (0, j)), pl.BlockSpec((tq, LANES), q_idx)],
        out_specs=[pl.BlockSpec((tk, W), kv_idx), pl.BlockSpec((tk, W), kv_idx), pl.BlockSpec((8, tk), lambda j, i: (0, j))],
        out_shape=[jax.ShapeDtypeStruct((T, W), BF16), jax.ShapeDtypeStruct((T, W), BF16), jax.ShapeDtypeStruct((8, T), F32)],
        scratch_shapes=[pltpu.VMEM((tk, W), F32), pltpu.VMEM((tk, W), F32), pltpu.VMEM((8, tk), F32)],
        compiler_params=_params(("parallel", "arbitrary")))(q, k, v, do, o, cum, cum_t, lse)


def fox_bwd_q(q, k, v, do, o, cum, cum_t, lse, H, *, name, tq=512, tk=512):
    T, W = q.shape
    tq, tk = min(tq, T), min(tk, T)
    nq, nk = T // tq, T // tk

    def body(q_ref, k_ref, v_ref, do_ref, o_ref, cq_ref, ck_ref, lse_ref, dq_ref, dq_acc):
        i, j = pl.program_id(0), pl.program_id(1)

        @pl.when(j == 0)
        def _():
            dq_acc[...] = jnp.zeros_like(dq_acc)

        @pl.when(j <= i)
        def _():
            for h in range(H):
                hs = slice(h * HEAD, (h + 1) * HEAD)
                _, ds = _fox_ds(q_ref, k_ref, v_ref, do_ref, o_ref, cq_ref, ck_ref, lse_ref, h, i, j, tq, tk)
                dq_acc[:, hs] += lax.dot_general(ds.astype(BF16), k_ref[:, hs], (((1,), (0,)), ((), ())),
                                                 preferred_element_type=F32) * (HEAD ** -0.5)

        @pl.when(j == nk - 1)
        def _():
            dq_ref[...] = dq_acc[...].astype(dq_ref.dtype)

    q_idx = lambda i, j: (i, 0)
    kv_idx = lambda i, j: (jnp.minimum(j, i), 0)
    return pl.pallas_call(
        body, name=name, grid=(nq, nk),
        in_specs=[pl.BlockSpec((tq, W), q_idx), pl.BlockSpec((tk, W), kv_idx), pl.BlockSpec((tk, W), kv_idx),
                  pl.BlockSpec((tq, W), q_idx), pl.BlockSpec((tq, W), q_idx), pl.BlockSpec((tq, LANES), q_idx),
                  pl.BlockSpec((8, tk), lambda i, j: (0, jnp.minimum(j, i))), pl.BlockSpec((tq, LANES), q_idx)],
        out_specs=pl.BlockSpec((tq, W), q_idx), out_shape=jax.ShapeDtypeStruct((T, W), BF16),
        scratch_shapes=[pltpu.VMEM((tq, W), F32)],
        compiler_params=_params(("parallel", "arbitrary")))(q, k, v, do, o, cum, cum_t, lse)


def _gates_fn(small, a_log_l, dt_bias_l, b_f_l):
    lane = lax.broadcasted_iota(jnp.int32, small.shape, 1)
    beta = _sigmoid(small)
    g = -jnp.exp(a_log_l) * _softplus(small + dt_bias_l)
    lf = _log_sigmoid(small + b_f_l)
    return (jnp.where(lane < LANE_A, beta, jnp.where(lane < LANE_F, g, jnp.where(lane < LANE_F + 8, lf, 0.0))),)


def _gated_norm_fn(o, z, g):
    return (_rms(o, g) * _silu(z),)


def _merge_fn(ya, yb, ym, ga, gb, gm):
    return (_sigmoid(ga) * ya + _sigmoid(gb) * yb + _sigmoid(gm) * ym,)


def _mem_attn_fn(nh, dh, mq, kn, v, gq):
    outs = []
    for h in range(nh):
        hs = slice(h * dh, (h + 1) * dh)
        qn = _rms(mq[:, hs], gq)
        s = NT(qn, kn[:, hs]) * (dh ** -0.5)
        e = jnp.exp(s - jnp.max(s, axis=1, keepdims=True))
        p = e / jnp.sum(e, axis=1, keepdims=True)
        outs.append(NN(p, v[:, hs]))
    return (jnp.concatenate(outs, axis=1),)


def sum_squares(x, *, name, tr=512):
    T, D = x.shape
    tr = min(tr, T)

    def body(x_ref, o_ref):
        @pl.when(pl.program_id(0) == 0)
        def _():
            o_ref[...] = jnp.zeros_like(o_ref)

        v = x_ref[...]
        o_ref[...] += jnp.sum(jnp.sum(v * v, axis=1, keepdims=True), axis=0, keepdims=True)

    return pl.pallas_call(
        body, name=name, grid=(T // tr,), in_specs=[pl.BlockSpec((tr, D), lambda i: (i, 0))],
        out_specs=pl.BlockSpec((1, LANES), lambda i: (0, 0)), out_shape=jax.ShapeDtypeStruct((1, LANES), F32),
        compiler_params=_params(("arbitrary",)))(x)


def exchange(arrays, scatter, *, name):
    n = len(arrays)
    out_shape = [jax.ShapeDtypeStruct(a.shape if sc else (N_DEV,) + a.shape, a.dtype) for a, sc in zip(arrays, scatter)]

    def body(*refs):
        ins, outs = refs[:n], refs[n:2 * n]
        send_sems, recv_sems, local_sems = refs[2 * n:]
        x, y, c = lax.axis_index("x"), lax.axis_index("y"), lax.axis_index("c")
        me = 4 * x + 2 * y + c
        peers = []
        for r in range(1, N_DEV):
            px = 1 - x if r & 4 else x
            py = 1 - y if r & 2 else y
            pc = 1 - c if r & 1 else c
            peers.append((r, (px, py, pc), 4 * px + 2 * py + pc))
        copies = []
        for a in range(n):
            src_me = ins[a].at[me] if scatter[a] else ins[a]
            local = pltpu.make_async_copy(src_me, outs[a].at[me], local_sems.at[a])
            local.start()
            copies.append(local)
        sends = []
        for a in range(n):
            for r, dev, lin in peers:
                cp = pltpu.make_async_remote_copy(
                    src_ref=ins[a].at[lin] if scatter[a] else ins[a], dst_ref=outs[a].at[me],
                    send_sem=send_sems.at[a, r - 1], recv_sem=recv_sems.at[a, r - 1], device_id=dev, device_id_type=MESH)
                cp.start()
                sends.append(cp)
        for a in range(n):
            for r, dev, lin in peers:
                pltpu.make_async_remote_copy(
                    src_ref=ins[a].at[lin] if scatter[a] else ins[a], dst_ref=outs[a].at[lin],
                    send_sem=send_sems.at[a, r - 1], recv_sem=recv_sems.at[a, r - 1], device_id=dev,
                    device_id_type=MESH).wait_recv()
        for cp in sends:
            cp.wait_send()
        for cp in copies:
            cp.wait()

    any_spec = pl.BlockSpec(memory_space=pl.ANY)
    return pl.pallas_call(
        body, name=name, in_specs=[any_spec] * n, out_specs=[any_spec] * n, out_shape=out_shape,
        scratch_shapes=[pltpu.SemaphoreType.DMA((n, N_DEV - 1)), pltpu.SemaphoreType.DMA((n, N_DEV - 1)),
                        pltpu.SemaphoreType.DMA((n,))])(*arrays)


def _place():
    x, y, c = lax.axis_index("x"), lax.axis_index("y"), lax.axis_index("c")
    chips = [(1 - x, y), (x, 1 - y), (1 - x, 1 - y)]
    return x, y, c, chips


def gather_two_level(arrays, *, name):
    n = len(arrays)

    def body(*refs):
        ins, outs = refs[:n], refs[n:2 * n]
        send_sems, recv_sems, local_sems = refs[2 * n:]
        x, y, c, chips = _place()
        lin = lambda px, py, pc: 4 * px + 2 * py + pc
        me, sibling = (x, y, c), (x, y, 1 - c)

        def copy(a, k, block, to, src=None):
            slot = outs[a].at[lin(*block)]
            return pltpu.make_async_remote_copy(src_ref=slot if src is None else src, dst_ref=slot,
                                                send_sem=send_sems.at[a, k], recv_sem=recv_sems.at[a, k],
                                                device_id=to, device_id_type=MESH)

        started = []
        for a in range(n):
            mine = pltpu.make_async_copy(ins[a], outs[a].at[lin(*me)], local_sems.at[a])
            mine.start()
            started.append(mine)
        sends = []
        for a in range(n):
            for j, chip in enumerate(chips):
                sends.append(copy(a, 1 + j, me, (*chip, c), src=ins[a]))
            sends.append(copy(a, 0, me, sibling, src=ins[a]))
        for cp in sends:
            cp.start()
        for a in range(n):
            for j, chip in enumerate(chips):
                copy(a, 1 + j, (*chip, c), me).wait_recv()
                fwd = copy(a, 4 + j, (*chip, c), sibling)
                fwd.start()
                sends.append(fwd)
        for a in range(n):
            copy(a, 0, sibling, me).wait_recv()
            for j, chip in enumerate(chips):
                copy(a, 4 + j, (*chip, 1 - c), me).wait_recv()
        for cp in sends:
            cp.wait_send()
        for cp in started:
            cp.wait()

    any_spec = pl.BlockSpec(memory_space=pl.ANY)
    return pl.pallas_call(
        body, name=name, in_specs=[any_spec] * n, out_specs=[any_spec] * n,
        out_shape=[jax.ShapeDtypeStruct((N_DEV,) + a.shape, a.dtype) for a in arrays],
        scratch_shapes=[pltpu.SemaphoreType.DMA((n, 7)), pltpu.SemaphoreType.DMA((n, 7)),
                        pltpu.SemaphoreType.DMA((n,))])(*arrays)


def sibling_exchange(parts, *, name):
    n = len(parts)

    def body(*refs):
        ins, outs = refs[:n], refs[n:2 * n]
        send_sems, recv_sems = refs[2 * n:]
        x, y, c, _ = _place()
        sends = []
        for a in range(n):
            for k in range(4):
                cp = pltpu.make_async_remote_copy(
                    src_ref=ins[a].at[2 * k + (1 - c)], dst_ref=outs[a].at[k], send_sem=send_sems.at[a, k],
                    recv_sem=recv_sems.at[a, k], device_id=(x, y, 1 - c), device_id_type=MESH)
                cp.start()
                sends.append(cp)
        for cp in sends:
            cp.wait()

    any_spec = pl.BlockSpec(memory_space=pl.ANY)
    return pl.pallas_call(
        body, name=name, in_specs=[any_spec] * n, out_specs=[any_spec] * n,
        out_shape=[jax.ShapeDtypeStruct((4,) + p.shape[1:], p.dtype) for p in parts],
        scratch_shapes=[pltpu.SemaphoreType.DMA((n, 4)), pltpu.SemaphoreType.DMA((n, 4))])(*parts)


def pair_sum(parts, sib, *, name, tr=128):
    _, R, Cc = parts.shape
    tr = min(tr, R)
    assert R % tr == 0
    core = lax.axis_index("c").astype(jnp.int32).reshape(1)

    def body(c_ref, p_ref, s_ref, o_ref):
        o_ref[0] = (p_ref[0, 0] + s_ref[0]).astype(o_ref.dtype)

    return pl.pallas_call(
        body, name=name,
        grid_spec=pltpu.PrefetchScalarGridSpec(
            num_scalar_prefetch=1, grid=(4, R // tr),
            in_specs=[pl.BlockSpec((1, 1, tr, Cc), lambda k, i, c: (k, c[0], i, 0)),
                      pl.BlockSpec((1, tr, Cc), lambda k, i, c: (k, i, 0))],
            out_specs=pl.BlockSpec((1, tr, Cc), lambda k, i, c: (k, i, 0))),
        out_shape=jax.ShapeDtypeStruct((4, R, Cc), BF16),
        compiler_params=_params(("parallel", "parallel")))(core, parts.reshape(4, 2, R, Cc), sib)


def chip_exchange(sums, *, name):
    n = len(sums)

    def body(*refs):
        ins, outs = refs[:n], refs[n:2 * n]
        send_sems, recv_sems, local_sems = refs[2 * n:]
        x, y, c, chips = _place()
        km = 2 * x + y
        started, sends = [], []
        for a in range(n):
            mine = pltpu.make_async_copy(ins[a].at[km], outs[a].at[km], local_sems.at[a])
            mine.start()
            started.append(mine)
        for a in range(n):
            for j, (px, py) in enumerate(chips):
                cp = pltpu.make_async_remote_copy(
                    src_ref=ins[a].at[2 * px + py], dst_ref=outs[a].at[km], send_sem=send_sems.at[a, j],
                    recv_sem=recv_sems.at[a, j], device_id=(px, py, c), device_id_type=MESH)
                cp.start()
                sends.append(cp)
        for a in range(n):
            for j, (px, py) in enumerate(chips):
                pltpu.make_async_remote_copy(
                    src_ref=ins[a].at[km], dst_ref=outs[a].at[2 * px + py], send_sem=send_sems.at[a, j],
                    recv_sem=recv_sems.at[a, j], device_id=(px, py, c), device_id_type=MESH).wait_recv()
        for cp in sends:
            cp.wait_send()
        for cp in started:
            cp.wait()

    any_spec = pl.BlockSpec(memory_space=pl.ANY)
    return pl.pallas_call(
        body, name=name, in_specs=[any_spec] * n, out_specs=[any_spec] * n,
        out_shape=[jax.ShapeDtypeStruct(s.shape, s.dtype) for s in sums],
        scratch_shapes=[pltpu.SemaphoreType.DMA((n, 3)), pltpu.SemaphoreType.DMA((n, 3)),
                        pltpu.SemaphoreType.DMA((n,))])(*sums)


def adamw(parts, w, m, v, *, name, tr=128):
    R, Cc = w.shape
    tr = min(tr, R)
    assert R % tr == 0
    n_parts = parts.shape[0]

    def body(p_ref, w_ref, m_ref, v_ref, g_ref, d_ref, nm_ref, nv_ref):
        g = p_ref[0].astype(F32)
        for s in range(1, n_parts):
            g = g + p_ref[s].astype(F32)
        nm = ADAM_B1 * m_ref[...] + (1.0 - ADAM_B1) * g
        nv = ADAM_B2 * v_ref[...] + (1.0 - ADAM_B2) * (g * g)
        m_hat = nm / (1.0 - ADAM_B1 ** ADAM_STEP)
        v_hat = nv / (1.0 - ADAM_B2 ** ADAM_STEP)
        g_ref[...] = g
        d_ref[...] = -ADAM_LR * (m_hat / (jnp.sqrt(v_hat) + ADAM_EPS) + ADAM_WD * w_ref[...])
        nm_ref[...] = nm
        nv_ref[...] = nv

    spec = pl.BlockSpec((tr, Cc), lambda i: (i, 0))
    return pl.pallas_call(
        body, name=name, grid=(R // tr,),
        in_specs=[pl.BlockSpec((n_parts, tr, Cc), lambda i: (0, i, 0)), spec, spec, spec], out_specs=[spec] * 4,
        out_shape=[jax.ShapeDtypeStruct((R, Cc), F32)] * 4, compiler_params=_params(("parallel",)))(parts, w, m, v)


def _lanes(vec, base):
    return jnp.pad(vec[None].astype(F32), ((0, 0), (base, LANES - base - vec.shape[0])))


def _col_shards(full_w):
    R, Ct = full_w.shape
    return jnp.transpose(full_w.reshape(R, N_DEV, Ct // N_DEV), (1, 0, 2))


def _from_col_shards(g):
    return jnp.transpose(g, (1, 0, 2)).reshape(g.shape[1], -1)


def kernel(x, mem, g_mix, w_in, conv_w, a_log, dt_bias, gdn_norm_g, fox_b_f, fox_q_norm, fox_k_norm, g_mem, w_mem_kv, mem_q_norm, mem_k_norm, w_up_gdn, w_up_fox, w_up_mem, w_out, g_mlp, w_ff1, w_ff2, loss_target, m_g_mix, m_w_in, m_conv_w, m_a_log, m_dt_bias, m_gdn_norm_g, m_fox_b_f, m_fox_q_norm, m_fox_k_norm, m_g_mem, m_w_mem_kv, m_mem_q_norm, m_mem_k_norm, m_w_up_gdn, m_w_up_fox, m_w_up_mem, m_w_out, m_g_mlp, m_w_ff1, m_w_ff2, v_g_mix, v_w_in, v_conv_w, v_a_log, v_dt_bias, v_gdn_norm_g, v_fox_b_f, v_fox_q_norm, v_fox_k_norm, v_g_mem, v_w_mem_kv, v_mem_q_norm, v_mem_k_norm, v_w_up_gdn, v_w_up_fox, v_w_up_mem, v_w_out, v_g_mlp, v_w_ff1, v_w_ff2):
    loc = dict(locals())
    big_names = ["w_in", "conv_w", "w_mem_kv", "w_up_gdn", "w_up_fox", "w_up_mem", "w_out", "w_ff1", "w_ff2"]
    col_sharded = {"w_in", "conv_w", "w_up_gdn", "w_up_fox", "w_up_mem", "w_ff1"}
    small_names = ["g_mix", "a_log", "dt_bias", "gdn_norm_g", "fox_b_f", "fox_q_norm", "fox_k_norm", "g_mem",
                   "mem_q_norm", "mem_k_norm", "g_mlp"]

    xs, tgt, mems = x[0], loss_target[0], mem[0]
    T, D = xs.shape
    HG = a_log.shape[1]
    HF = fox_b_f.shape[1]
    DM = mem_q_norm.shape[1]
    GQK, GV = HG * HEAD, HG * HEAD
    GQKV = 2 * GQK + GV
    FW = HF * HEAD
    MW = w_mem_kv.shape[2] // 2
    HM = MW // DM
    assert HG <= 8 and HF <= 8

    shards = [loc[n][0].astype(BF16) for n in big_names]
    gathered = gather_two_level(shards, name="gather_weights")
    W = {}
    for n, g in zip(big_names, gathered):
        W[n] = _from_col_shards(g) if n in col_sharded else g.reshape(-1, g.shape[2])
    widths = [GQKV, GV, HG, HG, FW, FW, FW, HF, MW, 3 * D]
    offs = np.concatenate([[0], np.cumsum(widths)]).tolist()
    seg = [W["w_in"][:, offs[i]:offs[i + 1]] for i in range(len(widths))]
    w_big = jnp.concatenate([seg[0], seg[1], seg[4], seg[5], seg[6], seg[8], seg[9]], axis=1)
    pad8 = lambda s: jnp.pad(s, ((0, 0), (0, 8 - s.shape[1])))
    w_small = jnp.concatenate([pad8(seg[2]), pad8(seg[3]), pad8(seg[7]), jnp.zeros((D, LANES - 24), BF16)], axis=1)
    o_z, o_fq, o_fk, o_fv = GQKV, GQKV + GV, GQKV + GV + FW, GQKV + GV + 2 * FW
    o_mq = o_fv + FW
    o_gt = o_mq + MW
    WB = o_gt + 3 * D
    conv_full = W["conv_w"].astype(F32)

    a_log_l, dt_bias_l, b_f_l = _lanes(a_log[0], LANE_A), _lanes(dt_bias[0], LANE_A), _lanes(fox_b_f[0], LANE_F)
    rms_fn = lambda t, g: (_rms(t, g),)

    h, = rowwise(rms_fn, [full(xs)], [g_mix], [(D, D, BF16)], name="rms_mix")
    big = matmul(h, w_big, mode="nn", name="proj_big")
    small = matmul(h, w_small, mode="nn", name="proj_small", out_dtypes=(F32,))
    bgf, = rowwise(_gates_fn, [full(small)], [a_log_l, dt_bias_l, b_f_l], [(LANES, LANES, F32)], name="gates")

    cn = conv_fwd(big, conv_full, HG, name="conv")
    o_gdn, ss = gdn_fwd(cn, bgf, HG, name="gdn_fwd")
    oa, = rowwise(_gated_norm_fn, [(o_gdn, HEAD, 0, 1), (big, HEAD, o_z // HEAD, 1)], [gdn_norm_g],
                  [(GV, HEAD, BF16)], name="gated_norm", ncb=HG)

    fqn, = rowwise(rms_fn, [(big, HEAD, o_fq // HEAD, 1)], [fox_q_norm], [(FW, HEAD, BF16)], name="fox_qnorm", ncb=HF)
    fkn, = rowwise(rms_fn, [(big, HEAD, o_fk // HEAD, 1)], [fox_k_norm], [(FW, HEAD, BF16)], name="fox_knorm", ncb=HF)
    fv = big[:, o_fv:o_fv + FW]
    cum = cumsum_tokens(bgf, reverse=False, name="cumsum")
    cum_t = jnp.pad(cum[:, LANE_F:LANE_F + HF].T, ((0, 8 - HF), (0, 0)))
    ob, lse = fox_fwd(fqn, fkn, fv, cum, cum_t, HF, name="fox_fwd")

    memn, = rowwise(rms_fn, [full(mems)], [g_mem], [(D, D, BF16)], name="rms_mem")
    kv_m = matmul(memn, W["w_mem_kv"], mode="nn", name="mem_kv", out_dtypes=(F32,))
    kmn, = rowwise(rms_fn, [(kv_m, DM, 0, 1)], [mem_k_norm], [(MW, DM, F32)], name="mem_knorm", ncb=HM)
    vm = kv_m[:, MW:]
    mem_fn = functools.partial(_mem_attn_fn, HM, DM)
    om, = rowwise(mem_fn, [(big, MW, o_mq // MW, 0)], [kmn, vm, mem_q_norm], [(MW, MW, BF16)], name="mem_attn")

    ya = matmul(oa, W["w_up_gdn"], mode="nn", name="up_gdn")
    yb = matmul(ob, W["w_up_fox"], mode="nn", name="up_fox")
    ym = matmul(om, W["w_up_mem"], mode="nn", name="up_mem")
    cbm = min(512, D)
    gate_rows = [(big, cbm, (o_gt + b * D) // cbm, 1) for b in range(3)]
    merge_rows = [(ya, cbm, 0, 1), (yb, cbm, 0, 1), (ym, cbm, 0, 1)] + gate_rows
    y, = rowwise(_merge_fn, merge_rows, [], [(D, cbm, BF16)], name="merge", ncb=D // cbm)
    x1 = matmul(y, W["w_out"], mode="nn", name="out_proj", out_dtypes=(F32,), extras=(xs,),
                epi=lambda r, res: (r + res,))

    h2, = rowwise(rms_fn, [full(x1)], [g_mlp], [(D, D, BF16)], name="rms_mlp")
    u_ff, a_ff = matmul(h2, W["w_ff1"], mode="nn", name="ff1", out_dtypes=(BF16, BF16),
                        epi=lambda r: (r, jnp.square(jnp.maximum(r, 0.0))))
    d_out = matmul(a_ff, W["w_ff2"], mode="nn", name="ff2_loss", out_dtypes=(F32,), extras=(x1, tgt),
                   epi=lambda r, res, t: ((r + res - t) * (1.0 / D),))
    loss_local = 0.5 * D * sum_squares(d_out, name="loss_sum")[0, 0]
    loss = lax.psum(loss_local, ("x", "y", "c"))

    G = {}
    d_u = matmul(d_out, W["w_ff2"], mode="nt", name="d_ff2_in", extras=(u_ff,),
                 epi=lambda r, u: (r * 2.0 * jnp.maximum(u.astype(F32), 0.0),))
    G["w_ff2"] = matmul(a_ff, d_out, mode="tn", name="d_w_ff2", out_dtypes=(F32,))
    d_h2 = matmul(d_u, W["w_ff1"], mode="nt", name="d_ff1_in")
    G["w_ff1"] = matmul(h2, d_u, mode="tn", name="d_w_ff1", out_dtypes=(F32,))
    d_x1, G["g_mlp"] = rowwise_bwd(rms_fn, [full(x1)], [g_mlp], [full(d_h2)], [(0, F32)], [0], name="d_rms_mlp",
                                   adds=[full(d_out)])
    d_y = matmul(d_x1, W["w_out"], mode="nt", name="d_out_proj_in")
    G["w_out"] = matmul(y, d_x1, mode="tn", name="d_w_out", out_dtypes=(F32,))
    d_ya, d_yb, d_ym, d_ga, d_gb, d_gm = rowwise_bwd(
        _merge_fn, merge_rows, [], [(d_y, cbm, 0, 1)], [(k, BF16) for k in range(6)], [], name="d_merge", ncb=D // cbm)
    d_oa = matmul(d_ya, W["w_up_gdn"], mode="nt", name="d_up_gdn_in")
    d_ob = matmul(d_yb, W["w_up_fox"], mode="nt", name="d_up_fox_in")
    d_om = matmul(d_ym, W["w_up_mem"], mode="nt", name="d_up_mem_in")
    G["w_up_gdn"] = matmul(oa, d_ya, mode="tn", name="d_w_up_gdn", out_dtypes=(F32,))
    G["w_up_fox"] = matmul(ob, d_yb, mode="tn", name="d_w_up_fox", out_dtypes=(F32,))
    G["w_up_mem"] = matmul(om, d_ym, mode="tn", name="d_w_up_mem", out_dtypes=(F32,))

    d_mq, d_kmn, d_vm, G["mem_q_norm"] = rowwise_bwd(
        mem_fn, [(big, MW, o_mq // MW, 0)], [kmn, vm, mem_q_norm], [full(d_om)], [(0, BF16)], [0, 1, 2], name="d_mem_attn")
    d_km, G["mem_k_norm"] = rowwise_bwd(rms_fn, [(kv_m, DM, 0, 1)], [mem_k_norm], [(d_kmn, DM, 0, 1)], [(0, F32)], [0],
                                         name="d_mem_knorm", ncb=HM)
    d_kv_m = jnp.concatenate([d_km, d_vm], axis=1)
    G["w_mem_kv"] = matmul(memn, d_kv_m, mode="tn", name="d_w_mem_kv", out_dtypes=(F32,))
    d_memn = matmul(d_kv_m, W["w_mem_kv"], mode="nt", name="d_mem_kv_in")
    _, G["g_mem"] = rowwise_bwd(rms_fn, [full(mems)], [g_mem], [full(d_memn)], [(0, BF16)], [0], name="d_rms_mem")

    d_fkn, d_fv, d_cum_t = fox_bwd_kv(fqn, fkn, fv, d_ob, ob, cum, cum_t, lse, HF, name="fox_bwd_kv")
    d_fqn = fox_bwd_q(fqn, fkn, fv, d_ob, ob, cum, cum_t, lse, HF, name="fox_bwd_q")
    d_fq, G["fox_q_norm"] = rowwise_bwd(rms_fn, [(big, HEAD, o_fq // HEAD, 1)], [fox_q_norm], [(d_fqn, HEAD, 0, 1)],
                                         [(0, BF16)], [0], name="d_fox_qnorm", ncb=HF)
    d_fk, G["fox_k_norm"] = rowwise_bwd(rms_fn, [(big, HEAD, o_fk // HEAD, 1)], [fox_k_norm], [(d_fkn, HEAD, 0, 1)],
                                         [(0, BF16)], [0], name="d_fox_knorm", ncb=HF)
    d_cum = jnp.pad(d_cum_t[:HF].T, ((0, 0), (LANE_F, LANES - LANE_F - HF)))
    d_logf = cumsum_tokens(d_cum, reverse=True, name="cumsum_rev")

    d_o_gdn, d_z, G["gdn_norm_g"] = rowwise_bwd(
        _gated_norm_fn, [(o_gdn, HEAD, 0, 1), (big, HEAD, o_z // HEAD, 1)], [gdn_norm_g], [(d_oa, HEAD, 0, 1)],
        [(0, F32), (1, BF16)], [0], name="d_gated_norm", ncb=HG)
    d_cn, d_bg = gdn_bwd(cn, bgf, ss, d_o_gdn, HG, name="gdn_bwd")
    d_conv_y, G["conv_w"] = conv_bwd_taps(big, conv_full, d_cn, HG, name="d_conv_taps")
    d_qkv = conv_bwd_input(d_conv_y, conv_full, name="d_conv_in")
    d_small, d_al, d_dt, d_bf = rowwise_bwd(_gates_fn, [full(small)], [a_log_l, dt_bias_l, b_f_l], [full(d_bg + d_logf)],
                                            [(0, F32)], [0, 1, 2], name="d_gates")
    G["a_log"], G["dt_bias"], G["fox_b_f"] = (d_al[:, LANE_A:LANE_A + HG], d_dt[:, LANE_A:LANE_A + HG],
                                               d_bf[:, LANE_F:LANE_F + HF])

    d_big = jnp.concatenate([d_qkv, d_z, d_fq, d_fk, d_fv, d_mq, d_ga, d_gb, d_gm], axis=1)
    d_h_s = matmul(d_small, w_small, mode="nt", name="d_proj_small_in", out_dtypes=(F32,))
    d_h = matmul(d_big, w_big, mode="nt", name="d_proj_big_in", extras=(d_h_s,), epi=lambda r, e: (r + e,))
    g_big = matmul(h, d_big, mode="tn", name="d_w_big", out_dtypes=(F32,))
    g_small = matmul(h, d_small, mode="tn", name="d_w_small", out_dtypes=(F32,))
    grad_x, G["g_mix"] = rowwise_bwd(rms_fn, [full(xs)], [g_mix], [full(d_h)], [(0, F32)], [0], name="d_rms_mix",
                                     adds=[full(d_x1)])
    grad_x = grad_x[None]
    cols = lambda a, o, wd: a[:, o:o + wd]
    G["w_in"] = jnp.concatenate([
        cols(g_big, 0, GQKV), cols(g_big, o_z, GV), cols(g_small, LANE_B, HG), cols(g_small, LANE_A, HG),
        cols(g_big, o_fq, FW), cols(g_big, o_fk, FW), cols(g_big, o_fv, FW), cols(g_small, LANE_F, HF),
        cols(g_big, o_mq, MW), cols(g_big, o_gt, 3 * D)], axis=1)

    parts = [_col_shards(G[n]) if n in col_sharded else G[n].reshape(N_DEV, -1, G[n].shape[1]) for n in big_names]
    small_sizes = [loc[n].shape[1] for n in small_names]
    pack = lambda d: jnp.concatenate([d[n].reshape(1, -1) for n in small_names], axis=1)
    npad = -sum(small_sizes) % LANES
    padp = lambda a: jnp.pad(a, ((0, 0), (0, npad)))
    g_small_packed = padp(pack(G))
    from_sibling = sibling_exchange(parts, name="grads_to_sibling")
    chip_sums = [pair_sum(p, s, name="pair_sum_" + n) for n, p, s in zip(big_names, parts, from_sibling)]
    recv = chip_exchange(chip_sums, name="grads_to_owner")
    recv_small, = exchange([g_small_packed], [False], name="gather_small_grads")

    res = {}
    for n, p in zip(big_names, recv):
        res[n] = [t[None] for t in adamw(p, loc[n][0], loc["m_" + n][0], loc["v_" + n][0], name="adamw_" + n)]
    sm = adamw(recv_small, padp(pack({n: loc[n] for n in small_names})), padp(pack({n: loc["m_" + n] for n in small_names})),
               padp(pack({n: loc["v_" + n] for n in small_names})), name="adamw_small")
    so = np.concatenate([[0], np.cumsum(small_sizes)]).tolist()
    for i, n in enumerate(small_names):
        res[n] = [t[:, so[i]:so[i + 1]] for t in sm]

    order = ["g_mix", "w_in", "conv_w", "a_log", "dt_bias", "gdn_norm_g", "fox_b_f", "fox_q_norm", "fox_k_norm", "g_mem",
             "w_mem_kv", "mem_q_norm", "mem_k_norm", "w_up_gdn", "w_up_fox", "w_up_mem", "w_out", "g_mlp", "w_ff1", "w_ff2"]
    return (loss, grad_x, *[res[n][0] for n in order], *[res[n][1] for n in order],
            *[res[n][2] for n in order], *[res[n][3] for n in order])
```

```python
import functools

import jax
import jax.numpy as jnp
import numpy as np
from jax import lax
from jax.experimental import pallas as pl
from jax.experimental.pallas import tpu as pltpu

F32 = jnp.float32
BF16 = jnp.bfloat16
HI = lax.Precision.HIGHEST

EPS = 1e-6
GDN_CHUNK = 64
GDN_CONV = 4
HEAD = 128
LANES = 128
HALO = 16
N_DEV = 8
MESH = pl.DeviceIdType.MESH
VMEM_LIMIT_V7X = 56 * 1024 * 1024

ADAM_LR, ADAM_B1, ADAM_B2, ADAM_EPS, ADAM_WD, ADAM_STEP = 0.001, 0.9, 0.999, 1e-08, 0.01, 10

LANE_B, LANE_A, LANE_F = 0, 8, 16


def _params(sem):
    return pltpu.CompilerParams(dimension_semantics=sem, vmem_limit_bytes=VMEM_LIMIT_V7X)


def _dg(a, b, ca, cb, prec):
    nb = a.ndim - 2
    batch = tuple(range(nb))
    return lax.dot_general(a, b, (((ca + nb,), (cb + nb,)), (batch, batch)), precision=prec,
                           preferred_element_type=F32)


def _make_mm(prec, cast):
    def c(x):
        return x.astype(BF16) if cast else x

    @jax.custom_vjp
    def nn(a, b):
        return _dg(c(a), c(b), 1, 0, prec)

    @jax.custom_vjp
    def nt(a, b):
        return _dg(c(a), c(b), 1, 1, prec)

    @jax.custom_vjp
    def tn(a, b):
        return _dg(c(a), c(b), 0, 0, prec)

    nn.defvjp(lambda a, b: (nn(a, b), (a, b)), lambda r, g: (nt(g, r[1]), tn(r[0], g)))
    nt.defvjp(lambda a, b: (nt(a, b), (a, b)), lambda r, g: (nn(g, r[1]), tn(g, r[0])))
    tn.defvjp(lambda a, b: (tn(a, b), (a, b)), lambda r, g: (nt(r[1], g), nn(r[0], g)))
    return nn, nt, tn


NN, NT, TN = _make_mm(None, True)
NNH, NTH, TNH = _make_mm(lax.Precision.HIGH, False)
NNX, _, _ = _make_mm(HI, False)


def _sigmoid(x):
    return 1.0 / (1.0 + jnp.exp(-x))


def _silu(x):
    return x * _sigmoid(x)


def _softplus(x):
    return jnp.maximum(x, 0.0) + jnp.log(1.0 + jnp.exp(-jnp.abs(x)))


def _log_sigmoid(x):
    return -_softplus(-x)


def _rms(x, g):
    return x * lax.rsqrt(jnp.mean(x * x, axis=-1, keepdims=True) + EPS) * g


def _tile(n, target):
    t = target
    while t >= LANES:
        if n % t == 0:
            return t
        t //= 2
    return n


def matmul(a, b, *, mode, name, out_dtypes=(BF16,), epi=None, extras=(), tm=1024, tn=1024, tk=2048, comm=None):
    if mode == "nn":
        (M, K), (K2, N) = a.shape, b.shape
    elif mode == "nt":
        (M, K), (N, K2) = a.shape, b.shape
    else:
        (K, M), (K2, N) = a.shape, b.shape
    assert K == K2, (name, a.shape, b.shape)
    tm, tn, tk = _tile(M, tm), _tile(N, tn), _tile(K, tk)
    ni, nj, nk = M // tm, N // tn, K // tk
    a_spec = (pl.BlockSpec((tk, tm), lambda i, j, k: (k, i)) if mode == "tn"
              else pl.BlockSpec((tm, tk), lambda i, j, k: (i, k)))
    b_spec = (pl.BlockSpec((tn, tk), lambda i, j, k: (j, k)) if mode == "nt"
              else pl.BlockSpec((tk, tn), lambda i, j, k: (k, j)))
    o_spec = pl.BlockSpec((tm, tn), lambda i, j, k: (i, j))
    dims = {"nn": ((1,), (0,)), "nt": ((1,), (1,)), "tn": ((0,), (0,))}[mode]
    ne, no = len(extras), len(out_dtypes)
    nc = len(comm.arrays) if comm is not None else 0
    n_steps = ni * nj * nk

    def body(a_ref, b_ref, *rest):
        ex, c_in = rest[:ne], rest[ne:ne + nc]
        outs, c_out = rest[ne + nc:ne + nc + no], rest[ne + nc + no:ne + nc + no + nc]
        acc = rest[ne + nc + no + nc]
        c_sems = rest[ne + nc + no + nc + 1:]
        k = pl.program_id(2)
        step = (pl.program_id(0) * nj + pl.program_id(1)) * nk + k

        if comm is not None:
            pl.when(step == 0)(lambda: comm.start(c_in, c_out, c_sems))
            if comm.forward is not None:
                pl.when(step == n_steps // 2)(lambda: comm.forward(c_in, c_out, c_sems))

        @pl.when(k == 0)
        def _():
            acc[...] = jnp.zeros_like(acc)

        acc[...] += lax.dot_general(a_ref[...].astype(BF16), b_ref[...].astype(BF16), (dims, ((), ())),
                                    preferred_element_type=F32)

        @pl.when(k == nk - 1)
        def _():
            r = acc[...]
            vals = epi(r, *[e[...] for e in ex]) if epi is not None else (r,)
            for o, v in zip(outs, vals):
                o[...] = v.astype(o.dtype)

        if comm is not None:
            pl.when(step == n_steps - 1)(lambda: comm.finish(c_in, c_out, c_sems))

    any_spec = pl.BlockSpec(memory_space=pl.ANY)
    sem = ("arbitrary",) * 3 if comm is not None else ("parallel", "parallel", "arbitrary")
    assert comm is None or comm.forward is None or n_steps >= 3
    out = pl.pallas_call(
        body, name=name, grid=(ni, nj, nk),
        in_specs=[a_spec, b_spec] + [o_spec] * ne + [any_spec] * nc, out_specs=[o_spec] * no + [any_spec] * nc,
        out_shape=[jax.ShapeDtypeStruct((M, N), d) for d in out_dtypes] + (comm.out_shape if comm is not None else []),
        scratch_shapes=[pltpu.VMEM((tm, tn), F32)] + (comm.sems if comm is not None else []),
        compiler_params=_params(sem))(a, b, *extras, *(comm.arrays if comm is not None else []))
    return out[0] if len(out) == 1 else out


def _row_spec(tr, cb, off, moves):
    return pl.BlockSpec((tr, cb), lambda i, j: (i, off + moves * j))


def _whole_spec(p):
    return pl.BlockSpec(p.shape, lambda i, j: (0,) * p.ndim)


def rowwise(fn, rows, params, outs, *, name, ncb=1, tr=512):
    T = rows[0][0].shape[0]
    tr = min(tr, T)
    assert T % tr == 0
    nr, npar = len(rows), len(params)

    def body(*refs):
        r, p, o = refs[:nr], refs[nr:nr + npar], refs[nr + npar:]
        vals = fn(*[x[...].astype(F32) for x in r], *[x[...] for x in p])
        for oref, v in zip(o, vals):
            oref[...] = v.astype(oref.dtype)

    res = pl.pallas_call(
        body, name=name, grid=(T // tr, ncb),
        in_specs=[_row_spec(tr, cb, off, mv) for (_, cb, off, mv) in rows] + [_whole_spec(p) for p in params],
        out_specs=[_row_spec(tr, cb, 0, 1) for (_, cb, _) in outs],
        out_shape=[jax.ShapeDtypeStruct((T, cols), d) for (cols, _, d) in outs],
        compiler_params=_params(("parallel", "parallel")))(*[r[0] for r in rows], *params)
    return res


def rowwise_bwd(fn, rows, params, cots, drows, dparams, *, name, ncb=1, tr=512, adds=()):
    T = rows[0][0].shape[0]
    tr = min(tr, T)
    assert T % tr == 0
    nr, npar, nc, ndr, na = len(rows), len(params), len(cots), len(drows), len(adds)

    def body(*refs):
        r, p, c = refs[:nr], refs[nr:nr + npar], refs[nr + npar:nr + npar + nc]
        base = nr + npar + nc + na
        ad, o_r, o_p = refs[base - na:base], refs[base:base + ndr], refs[base + ndr:]
        prim = [x[...].astype(F32) for x in r] + [x[...] for x in p]
        _, vjp = jax.vjp(lambda *a: tuple(fn(*a)), *prim)
        g = vjp(tuple(x[...].astype(F32) for x in c))
        for k, (oref, (idx, _)) in enumerate(zip(o_r, drows)):
            val = g[idx] + ad[k][...].astype(F32) if k < na else g[idx]
            oref[...] = val.astype(oref.dtype)
        first = jnp.logical_and(pl.program_id(0) == 0, pl.program_id(1) == 0)

        @pl.when(first)
        def _():
            for oref in o_p:
                oref[...] = jnp.zeros_like(oref)

        for oref, idx in zip(o_p, dparams):
            oref[...] += g[nr + idx]

    res = pl.pallas_call(
        body, name=name, grid=(T // tr, ncb),
        in_specs=([_row_spec(tr, cb, off, mv) for (_, cb, off, mv) in rows] + [_whole_spec(p) for p in params]
                  + [_row_spec(tr, cb, off, mv) for (_, cb, off, mv) in tuple(cots) + tuple(adds)]),
        out_specs=([_row_spec(tr, rows[idx][1], 0, 1) for (idx, _) in drows]
                   + [_whole_spec(params[idx]) for idx in dparams]),
        out_shape=([jax.ShapeDtypeStruct((T, ncb * rows[idx][1] if rows[idx][3] else rows[idx][1]), d)
                    for (idx, d) in drows]
                   + [jax.ShapeDtypeStruct(params[idx].shape, F32) for idx in dparams]),
        compiler_params=_params(("arbitrary", "arbitrary")))(
            *[r[0] for r in rows], *params, *[c[0] for c in cots], *[a[0] for a in adds])
    return res


def full(a):
    return (a, a.shape[1], 0, 0)


def cumsum_tokens(x, *, reverse, name, tb=256):
    T = x.shape[0]
    tb = min(tb, T)
    nb = T // tb
    idx = (lambda i: (nb - 1 - i, 0)) if reverse else (lambda i: (i, 0))

    def body(x_ref, o_ref, carry):
        @pl.when(pl.program_id(0) == 0)
        def _():
            carry[...] = jnp.zeros_like(carry)

        ii = lax.broadcasted_iota(jnp.int32, (tb, tb), 0)
        jj = lax.broadcasted_iota(jnp.int32, (tb, tb), 1)
        tri = ((ii <= jj) if reverse else (ii >= jj)).astype(F32)
        c = lax.dot_general(tri, x_ref[...], (((1,), (0,)), ((), ())), precision=HI,
                            preferred_element_type=F32) + carry[0:1, :]
        o_ref[...] = c
        carry[0:1, :] = c[0:1, :] if reverse else c[tb - 1:tb, :]

    return pl.pallas_call(
        body, name=name, grid=(nb,), in_specs=[pl.BlockSpec((tb, LANES), idx)],
        out_specs=pl.BlockSpec((tb, LANES), idx), out_shape=jax.ShapeDtypeStruct((T, LANES), F32),
        scratch_shapes=[pltpu.VMEM((8, LANES), F32)], compiler_params=_params(("arbitrary",)))(x)


def _conv_post(y, kind, dk):
    c = _silu(y)
    r = lax.rsqrt(jnp.sum(c * c, axis=-1, keepdims=True) + EPS)
    return jnp.where(kind == 0, c * r * (dk ** -0.5), jnp.where(kind == 1, c * r, c))


def _conv_taps(cur, prev, w, tr):
    ext = jnp.concatenate([prev, cur], axis=0)
    y = w[3:4, :] * cur
    for d in (1, 2, 3):
        y = y + w[3 - d:4 - d, :] * pltpu.roll(ext, d, 0)[HALO:HALO + tr]
    return y


def conv_fwd(big, w, n_qk_heads, *, name, tr=512):
    T, W = big.shape[0], w.shape[1]
    tr = min(tr, T)
    nh = W // HEAD

    def body(cur_ref, prev_ref, w_ref, o_ref):
        i, j = pl.program_id(0), pl.program_id(1)
        prev = jnp.where(i > 0, prev_ref[...].astype(F32), 0.0)
        y = _conv_taps(cur_ref[...].astype(F32), prev, w_ref[...], tr)
        kind = jnp.where(j < n_qk_heads, 0, jnp.where(j < 2 * n_qk_heads, 1, 2))
        o_ref[...] = _conv_post(y, kind, HEAD)

    return pl.pallas_call(
        body, name=name, grid=(T // tr, nh),
        in_specs=[pl.BlockSpec((tr, HEAD), lambda i, j: (i, j)),
                  pl.BlockSpec((HALO, HEAD), lambda i, j: (jnp.maximum(i * (tr // HALO) - 1, 0), j)),
                  pl.BlockSpec((GDN_CONV, HEAD), lambda i, j: (0, j))],
        out_specs=pl.BlockSpec((tr, HEAD), lambda i, j: (i, j)),
        out_shape=jax.ShapeDtypeStruct((T, W), F32), compiler_params=_params(("parallel", "parallel")))(big, big, w)


def conv_bwd_taps(big, w, dcn, n_qk_heads, *, name, tr=512):
    T, W = big.shape[0], w.shape[1]
    tr = min(tr, T)
    nh = W // HEAD

    def body(cur_ref, prev_ref, w_ref, g_ref, dy_ref, dw_ref):
        j, i = pl.program_id(0), pl.program_id(1)
        cur = cur_ref[...].astype(F32)
        prev = jnp.where(i > 0, prev_ref[...].astype(F32), 0.0)
        y = _conv_taps(cur, prev, w_ref[...], tr)
        kind = jnp.where(j < n_qk_heads, 0, jnp.where(j < 2 * n_qk_heads, 1, 2))
        _, vjp = jax.vjp(lambda t: _conv_post(t, kind, HEAD), y)
        dy, = vjp(g_ref[...])
        dy_ref[...] = dy
        ext = jnp.concatenate([prev, cur], axis=0)
        rows = [jnp.sum(dy * (cur if d == 0 else pltpu.roll(ext, d, 0)[HALO:HALO + tr]), axis=0, keepdims=True)
                for d in (3, 2, 1, 0)]

        @pl.when(i == 0)
        def _():
            dw_ref[...] = jnp.zeros_like(dw_ref)

        dw_ref[...] += jnp.concatenate(rows, axis=0)

    return pl.pallas_call(
        body, name=name, grid=(nh, T // tr),
        in_specs=[pl.BlockSpec((tr, HEAD), lambda j, i: (i, j)),
                  pl.BlockSpec((HALO, HEAD), lambda j, i: (jnp.maximum(i * (tr // HALO) - 1, 0), j)),
                  pl.BlockSpec((GDN_CONV, HEAD), lambda j, i: (0, j)),
                  pl.BlockSpec((tr, HEAD), lambda j, i: (i, j))],
        out_specs=[pl.BlockSpec((tr, HEAD), lambda j, i: (i, j)), pl.BlockSpec((GDN_CONV, HEAD), lambda j, i: (0, j))],
        out_shape=[jax.ShapeDtypeStruct((T, W), F32), jax.ShapeDtypeStruct((GDN_CONV, W), F32)],
        compiler_params=_params(("parallel", "arbitrary")))(big, big, w, dcn)


def conv_bwd_input(dy, w, *, name, tr=512):
    T, W = dy.shape
    tr = min(tr, T)
    nrow = T // tr

    def body(cur_ref, nxt_ref, w_ref, o_ref):
        i = pl.program_id(0)
        cur = cur_ref[...]
        nxt = jnp.where(i < nrow - 1, nxt_ref[...], 0.0)
        ext = jnp.concatenate([cur, nxt], axis=0)
        w = w_ref[...]
        dx = w[3:4, :] * cur
        for d in (1, 2, 3):
            dx = dx + w[3 - d:4 - d, :] * pltpu.roll(ext, tr + HALO - d, 0)[0:tr]
        o_ref[...] = dx.astype(o_ref.dtype)

    return pl.pallas_call(
        body, name=name, grid=(nrow, W // HEAD),
        in_specs=[pl.BlockSpec((tr, HEAD), lambda i, j: (i, j)),
                  pl.BlockSpec((HALO, HEAD), lambda i, j: (jnp.minimum((i + 1) * (tr // HALO), T // HALO - 1), j)),
                  pl.BlockSpec((GDN_CONV, HEAD), lambda i, j: (0, j))],
        out_specs=pl.BlockSpec((tr, HEAD), lambda i, j: (i, j)),
        out_shape=jax.ShapeDtypeStruct((T, W), BF16), compiler_params=_params(("parallel", "parallel")))(dy, dy, w)


def _gdn_chunk(q, k, v, bg, S):
    H, C = q.shape[0], q.shape[1]
    ii = lax.broadcasted_iota(jnp.int32, (C, C), 0)
    jj = lax.broadcasted_iota(jnp.int32, (C, C), 1)
    lincl = (ii >= jj).astype(F32)
    strict, incl, eye = (ii > jj)[None], (ii >= jj)[None], (ii == jj)[None]
    gam2d = lax.dot_general(lincl, bg, (((1,), (0,)), ((), ())), precision=HI, preferred_element_type=F32)
    lane = lax.broadcasted_iota(jnp.int32, (H, 1, LANES), 2)
    hh = lax.broadcasted_iota(jnp.int32, (H, 1, LANES), 0)
    beta = jnp.sum(bg[None] * (lane == hh + LANE_B).astype(F32), axis=2, keepdims=True)
    gam = jnp.sum(gam2d[None] * (lane == hh + LANE_A).astype(F32), axis=2, keepdims=True)
    last = (lax.broadcasted_iota(jnp.int32, (1, C, 1), 1) == C - 1).astype(F32)
    gam_last = jnp.sum(gam * last, axis=1, keepdims=True)
    gam_row = NNX(jnp.ones((H, C, C), F32), jnp.where(eye, gam, 0.0))
    diff = gam - gam_row
    dec_s = jnp.where(strict, jnp.exp(jnp.where(strict, diff, 0.0)), 0.0)
    dec_i = jnp.where(incl, jnp.exp(jnp.where(incl, diff, 0.0)), 0.0)
    p = -(beta * NT(k, k) * dec_s)
    t = jnp.where(eye, 1.0, 0.0) + p
    for _ in range(5):
        p = NNH(p, p)
        t = t + NNH(t, p)
    eg = jnp.exp(gam)
    w = NNH(t, beta * eg * k)
    u0 = NNH(t, beta * v)
    qk = NT(q, k) * dec_i
    u = u0 - NN(w, S)
    o = NN(q * eg, S) + NN(qk, u)
    S2 = jnp.exp(gam_last) * S + TN(k * jnp.exp(gam_last - gam), u)
    return o, S2


def _heads(x, base, H):
    return jnp.stack([x[:, base + h * HEAD:base + (h + 1) * HEAD] for h in range(H)])


def _unheads(x):
    return jnp.concatenate([x[h] for h in range(x.shape[0])], axis=1)


def gdn_fwd(cn, bgf, H, *, name):
    T, C, W = cn.shape[0], GDN_CHUNK, H * HEAD
    N = T // C

    def body(cn_ref, bg_ref, o_ref, ss_ref, s_scr):
        @pl.when(pl.program_id(0) == 0)
        def _():
            s_scr[...] = jnp.zeros_like(s_scr)

        x, S = cn_ref[...], s_scr[...]
        ss_ref[0] = S
        o, S2 = _gdn_chunk(_heads(x, 0, H), _heads(x, W, H), _heads(x, 2 * W, H), bg_ref[...], S)
        o_ref[...] = _unheads(o)
        s_scr[...] = S2

    return pl.pallas_call(
        body, name=name, grid=(N,),
        in_specs=[pl.BlockSpec((C, 3 * W), lambda n: (n, 0)), pl.BlockSpec((C, LANES), lambda n: (n, 0))],
        out_specs=[pl.BlockSpec((C, W), lambda n: (n, 0)), pl.BlockSpec((1, H, HEAD, HEAD), lambda n: (n, 0, 0, 0))],
        out_shape=[jax.ShapeDtypeStruct((T, W), F32), jax.ShapeDtypeStruct((N, H, HEAD, HEAD), F32)],
        scratch_shapes=[pltpu.VMEM((H, HEAD, HEAD), F32)], compiler_params=_params(("arbitrary",)))(cn, bgf)


def gdn_bwd(cn, bgf, ss, do, H, *, name):
    T, C, W = cn.shape[0], GDN_CHUNK, H * HEAD
    N = T // C

    def body(cn_ref, bg_ref, ss_ref, do_ref, dcn_ref, dbg_ref, ds_scr):
        @pl.when(pl.program_id(0) == 0)
        def _():
            ds_scr[...] = jnp.zeros_like(ds_scr)

        x = cn_ref[...]
        _, vjp = jax.vjp(_gdn_chunk, _heads(x, 0, H), _heads(x, W, H), _heads(x, 2 * W, H), bg_ref[...], ss_ref[0])
        dq, dk, dv, dbg, dS = vjp((_heads(do_ref[...], 0, H), ds_scr[...]))
        dcn_ref[...] = jnp.concatenate([_unheads(dq), _unheads(dk), _unheads(dv)], axis=1)
        dbg_ref[...] = dbg
        ds_scr[...] = dS

    rev = lambda n: (N - 1 - n, 0)
    return pl.pallas_call(
        body, name=name, grid=(N,),
        in_specs=[pl.BlockSpec((C, 3 * W), rev), pl.BlockSpec((C, LANES), rev),
                  pl.BlockSpec((1, H, HEAD, HEAD), lambda n: (N - 1 - n, 0, 0, 0)), pl.BlockSpec((C, W), rev)],
        out_specs=[pl.BlockSpec((C, 3 * W), rev), pl.BlockSpec((C, LANES), rev)],
        out_shape=[jax.ShapeDtypeStruct((T, 3 * W), F32), jax.ShapeDtypeStruct((T, LANES), F32)],
        scratch_shapes=[pltpu.VMEM((H, HEAD, HEAD), F32)], compiler_params=_params(("arbitrary",)))(cn, bgf, ss, do)


def _fox_scores(q, k, cq, ck, i, j, tq, tk):
    s = lax.dot_general(q, k, (((1,), (1,)), ((), ())), preferred_element_type=F32) * (HEAD ** -0.5) + cq - ck
    rows = i * tq + lax.broadcasted_iota(jnp.int32, (tq, tk), 0)
    cols = j * tk + lax.broadcasted_iota(jnp.int32, (tq, tk), 1)
    return s, rows >= cols


def fox_fwd(q, k, v, cum, cum_t, H, *, name, tq=512, tk=512):
    T, W = q.shape
    tq, tk = min(tq, T), min(tk, T)
    nq, nk = T // tq, T // tk
    assert tq == tk
    scale = HEAD ** -0.5

    def body(q_ref, k_ref, v_ref, cq_ref, ck_ref, o_ref, lse_ref, acc, m_scr, l_scr):
        i, sweep, j = pl.program_id(0), pl.program_id(1), pl.program_id(2)

        @pl.when(jnp.logical_and(sweep == 0, j == 0))
        def _():
            m_scr[...] = jnp.full_like(m_scr, -jnp.inf)

        @pl.when(jnp.logical_and(sweep == 1, j == 0))
        def _():
            acc[...] = jnp.zeros_like(acc)
            l_scr[...] = jnp.zeros_like(l_scr)

        def scores(h, diagonal):
            hs = slice(h * HEAD, (h + 1) * HEAD)
            s = lax.dot_general(q_ref[:, hs], k_ref[:, hs], (((1,), (1,)), ((), ())), preferred_element_type=F32)
            s = s * scale + (cq_ref[:, LANE_F + h:LANE_F + h + 1] - ck_ref[h:h + 1, :])
            if diagonal:
                keep = lax.broadcasted_iota(jnp.int32, (tq, tk), 0) >= lax.broadcasted_iota(jnp.int32, (tq, tk), 1)
                s = jnp.where(keep, s, -jnp.inf)
            return s

        def find_max(diagonal):
            lane = lax.broadcasted_iota(jnp.int32, (tq, LANES), 1)
            m_all = m_scr[...]
            new_m = m_all
            for h in range(H):
                m_h = jnp.maximum(m_all[:, h:h + 1], jnp.max(scores(h, diagonal), axis=1, keepdims=True))
                new_m = jnp.where(lane == h, m_h, new_m)
            m_scr[...] = new_m

        def accumulate(diagonal):
            lane = lax.broadcasted_iota(jnp.int32, (tq, LANES), 1)
            m_all, l_all = m_scr[...], l_scr[...]
            new_l = l_all
            for h in range(H):
                hs = slice(h * HEAD, (h + 1) * HEAD)
                p = jnp.exp(scores(h, diagonal) - m_all[:, h:h + 1])
                p_hi = p.astype(BF16)
                p_lo = (p - p_hi.astype(F32)).astype(BF16)
                pv = lambda t: lax.dot_general(t, v_ref[:, hs], (((1,), (0,)), ((), ())), preferred_element_type=F32)
                acc[:, hs] += pv(p_hi) + pv(p_lo)
                new_l = jnp.where(lane == h, l_all[:, h:h + 1] + jnp.sum(p, axis=1, keepdims=True), new_l)
            l_scr[...] = new_l

        for sw, fn in ((0, find_max), (1, accumulate)):
            pl.when(jnp.logical_and(sweep == sw, j < i))(functools.partial(fn, False))
            pl.when(jnp.logical_and(sweep == sw, j == i))(functools.partial(fn, True))

        @pl.when(jnp.logical_and(sweep == 1, j == nk - 1))
        def _():
            l_all = l_scr[...]
            for h in range(H):
                hs = slice(h * HEAD, (h + 1) * HEAD)
                o_ref[:, hs] = (acc[:, hs] / l_all[:, h:h + 1]).astype(o_ref.dtype)
            lse_ref[...] = m_scr[...] + jnp.log(l_all)

    kv_idx = lambda i, s, j: (jnp.minimum(j, i), 0)
    q_idx = lambda i, s, j: (i, 0)
    return pl.pallas_call(
        body, name=name, grid=(nq, 2, nk),
        in_specs=[pl.BlockSpec((tq, W), q_idx), pl.BlockSpec((tk, W), kv_idx),
                  pl.BlockSpec((tk, W), lambda i, s, j: (s * jnp.minimum(j, i), 0)),
                  pl.BlockSpec((tq, LANES), q_idx), pl.BlockSpec((8, tk), lambda i, s, j: (0, jnp.minimum(j, i)))],
        out_specs=[pl.BlockSpec((tq, W), q_idx), pl.BlockSpec((tq, LANES), q_idx)],
        out_shape=[jax.ShapeDtypeStruct((T, W), F32), jax.ShapeDtypeStruct((T, LANES), F32)],
        scratch_shapes=[pltpu.VMEM((tq, W), F32), pltpu.VMEM((tq, LANES), F32), pltpu.VMEM((tq, LANES), F32)],
        compiler_params=_params(("parallel", "arbitrary", "arbitrary")))(q, k, v, cum, cum_t)


def _fox_ds(q_ref, k_ref, v_ref, do_ref, o_ref, cq_ref, ck_ref, lse_ref, h, i, j, tq, tk):
    hs = slice(h * HEAD, (h + 1) * HEAD)
    s, mask = _fox_scores(q_ref[:, hs], k_ref[:, hs], cq_ref[:, LANE_F + h:LANE_F + h + 1], ck_ref[h:h + 1, :],
                          i, j, tq, tk)
    p = jnp.where(mask, jnp.exp(s - lse_ref[:, h:h + 1]), 0.0)
    do = do_ref[:, hs]
    dp = lax.dot_general(do, v_ref[:, hs], (((1,), (1,)), ((), ())), preferred_element_type=F32)
    delta = jnp.sum(do.astype(F32) * o_ref[:, hs].astype(F32), axis=1, keepdims=True)
    return p, p * (dp - delta)


def fox_bwd_kv(q, k, v, do, o, cum, cum_t, lse, H, *, name, tq=512, tk=512):
    T, W = q.shape
    tq, tk = min(tq, T), min(tk, T)
    nq, nk = T // tq, T // tk

    def body(q_ref, k_ref, v_ref, do_ref, o_ref, cq_ref, ck_ref, lse_ref, dk_ref, dv_ref, dc_ref, dk_acc, dv_acc, dc_acc):
        j, i = pl.program_id(0), pl.program_id(1)

        @pl.when(i == 0)
        def _():
            dk_acc[...] = jnp.zeros_like(dk_acc)
            dv_acc[...] = jnp.zeros_like(dv_acc)
            dc_acc[...] = jnp.zeros_like(dc_acc)

        @pl.when(i >= j)
        def _():
            row = lax.broadcasted_iota(jnp.int32, (8, tk), 0)
            dc = dc_acc[...]
            for h in range(H):
                hs = slice(h * HEAD, (h + 1) * HEAD)
                p, ds = _fox_ds(q_ref, k_ref, v_ref, do_ref, o_ref, cq_ref, ck_ref, lse_ref, h, i, j, tq, tk)
                dv_acc[:, hs] += lax.dot_general(p.astype(BF16), do_ref[:, hs], (((0,), (0,)), ((), ())),
                                                 preferred_element_type=F32)
                dk_acc[:, hs] += lax.dot_general(ds.astype(BF16), q_ref[:, hs], (((0,), (0,)), ((), ())),
                                                 preferred_element_type=F32) * (HEAD ** -0.5)
                dc = jnp.where(row == h, dc - jnp.sum(ds, axis=0, keepdims=True), dc)
            dc_acc[...] = dc

        @pl.when(i == nq - 1)
        def _():
            dk_ref[...] = dk_acc[...].astype(dk_ref.dtype)
            dv_ref[...] = dv_acc[...].astype(dv_ref.dtype)
            dc_ref[...] = dc_acc[...]

    q_idx = lambda j, i: (jnp.maximum(i, j), 0)
    kv_idx = lambda j, i: (j, 0)
    return pl.pallas_call(
        body, name=name, grid=(nk, nq),
        in_specs=[pl.BlockSpec((tq, W), q_idx), pl.BlockSpec((tk, W), kv_idx), pl.BlockSpec((tk, W), kv_idx),
                  pl.BlockSpec((tq, W), q_idx), pl.BlockSpec((tq, W), q_idx), pl.BlockSpec((tq, LANES), q_idx),
                  pl.BlockSpec((8, tk), lambda j, i: (0, j)), pl.BlockSpec((tq, LANES), q_idx)],
        out_specs=[pl.BlockSpec((tk, W), kv_idx), pl.BlockSpec((tk, W), kv_idx), pl.BlockSpec((8, tk), lambda j, i: (0, j))],
        out_shape=[jax.ShapeDtypeStruct((T, W), BF16), jax.ShapeDtypeStruct((T, W), BF16), jax.ShapeDtypeStruct((8, T), F32)],
        scratch_shapes=[pltpu.VMEM((tk, W), F32), pltpu.VMEM((tk, W), F32), pltpu.VMEM((8, tk), F32)],
        compiler_params=_params(("parallel", "arbitrary")))(q, k, v, do, o, cum, cum_t, lse)


def fox_bwd_q(q, k, v, do, o, cum, cum_t, lse, H, *, name, tq=512, tk=512):
    T, W = q.shape
    tq, tk = min(tq, T), min(tk, T)
    nq, nk = T // tq, T // tk

    def body(q_ref, k_ref, v_ref, do_ref, o_ref, cq_ref, ck_ref, lse_ref, dq_ref, dq_acc):
        i, j = pl.program_id(0), pl.program_id(1)

        @pl.when(j == 0)
        def _():
            dq_acc[...] = jnp.zeros_like(dq_acc)

        @pl.when(j <= i)
        def _():
            for h in range(H):
                hs = slice(h * HEAD, (h + 1) * HEAD)
                _, ds = _fox_ds(q_ref, k_ref, v_ref, do_ref, o_ref, cq_ref, ck_ref, lse_ref, h, i, j, tq, tk)
                dq_acc[:, hs] += lax.dot_general(ds.astype(BF16), k_ref[:, hs], (((1,), (0,)), ((), ())),
                                                 preferred_element_type=F32) * (HEAD ** -0.5)

        @pl.when(j == nk - 1)
        def _():
            dq_ref[...] = dq_acc[...].astype(dq_ref.dtype)

    q_idx = lambda i, j: (i, 0)
    kv_idx = lambda i, j: (jnp.minimum(j, i), 0)
    return pl.pallas_call(
        body, name=name, grid=(nq, nk),
        in_specs=[pl.BlockSpec((tq, W), q_idx), pl.BlockSpec((tk, W), kv_idx), pl.BlockSpec((tk, W), kv_idx),
                  pl.BlockSpec((tq, W), q_idx), pl.BlockSpec((tq, W), q_idx), pl.BlockSpec((tq, LANES), q_idx),
                  pl.BlockSpec((8, tk), lambda i, j: (0, jnp.minimum(j, i))), pl.BlockSpec((tq, LANES), q_idx)],
        out_specs=pl.BlockSpec((tq, W), q_idx), out_shape=jax.ShapeDtypeStruct((T, W), BF16),
        scratch_shapes=[pltpu.VMEM((tq, W), F32)],
        compiler_params=_params(("parallel", "arbitrary")))(q, k, v, do, o, cum, cum_t, lse)


def _gates_fn(small, a_log_l, dt_bias_l, b_f_l):
    lane = lax.broadcasted_iota(jnp.int32, small.shape, 1)
    beta = _sigmoid(small)
    g = -jnp.exp(a_log_l) * _softplus(small + dt_bias_l)
    lf = _log_sigmoid(small + b_f_l)
    return (jnp.where(lane < LANE_A, beta, jnp.where(lane < LANE_F, g, jnp.where(lane < LANE_F + 8, lf, 0.0))),)


def _gated_norm_fn(o, z, g):
    return (_rms(o, g) * _silu(z),)


def _merge_fn(ya, yb, ym, ga, gb, gm):
    return (_sigmoid(ga) * ya + _sigmoid(gb) * yb + _sigmoid(gm) * ym,)


def _mem_attn_fn(nh, dh, mq, kn, v, gq):
    outs = []
    for h in range(nh):
        hs = slice(h * dh, (h + 1) * dh)
        qn = _rms(mq[:, hs], gq)
        s = NT(qn, kn[:, hs]) * (dh ** -0.5)
        e = jnp.exp(s - jnp.max(s, axis=1, keepdims=True))
        p = e / jnp.sum(e, axis=1, keepdims=True)
        outs.append(NN(p, v[:, hs]))
    return (jnp.concatenate(outs, axis=1),)


def sum_squares(x, *, name, tr=512):
    T, D = x.shape
    tr = min(tr, T)

    def body(x_ref, o_ref):
        @pl.when(pl.program_id(0) == 0)
        def _():
            o_ref[...] = jnp.zeros_like(o_ref)

        v = x_ref[...]
        o_ref[...] += jnp.sum(jnp.sum(v * v, axis=1, keepdims=True), axis=0, keepdims=True)

    return pl.pallas_call(
        body, name=name, grid=(T // tr,), in_specs=[pl.BlockSpec((tr, D), lambda i: (i, 0))],
        out_specs=pl.BlockSpec((1, LANES), lambda i: (0, 0)), out_shape=jax.ShapeDtypeStruct((1, LANES), F32),
        compiler_params=_params(("arbitrary",)))(x)


class Comm:
    def __init__(self, arrays, out_shape, sems, start, forward, finish):
        self.arrays, self.out_shape, self.sems = list(arrays), list(out_shape), list(sems)
        self.start, self.forward, self.finish = start, forward, finish


def _place():
    x, y, c = lax.axis_index("x"), lax.axis_index("y"), lax.axis_index("c")
    chips = [(1 - x, y), (x, 1 - y), (1 - x, 1 - y)]
    return x, y, c, chips


def comm_gather(arrays):
    n = len(arrays)
    lin = lambda px, py, pc: 4 * px + 2 * py + pc

    def copy(ins, outs, sems, a, k, block, to, src=None):
        slot = outs[a].at[lin(*block)]
        return pltpu.make_async_remote_copy(src_ref=slot if src is None else src, dst_ref=slot, send_sem=sems[0].at[a, k],
                                            recv_sem=sems[1].at[a, k], device_id=to, device_id_type=MESH)

    def local(ins, outs, sems, a):
        x, y, c, _ = _place()
        return pltpu.make_async_copy(ins[a], outs[a].at[lin(x, y, c)], sems[2].at[a])

    def start(ins, outs, sems):
        x, y, c, chips = _place()
        for a in range(n):
            local(ins, outs, sems, a).start()
        for a in range(n):
            for j, chip in enumerate(chips):
                copy(ins, outs, sems, a, 1 + j, (x, y, c), (*chip, c), src=ins[a]).start()
            copy(ins, outs, sems, a, 0, (x, y, c), (x, y, 1 - c), src=ins[a]).start()

    def forward(ins, outs, sems):
        x, y, c, chips = _place()
        for a in range(n):
            for j, chip in enumerate(chips):
                copy(ins, outs, sems, a, 1 + j, (*chip, c), (x, y, c)).wait_recv()
                copy(ins, outs, sems, a, 4 + j, (*chip, c), (x, y, 1 - c)).start()

    def finish(ins, outs, sems):
        x, y, c, chips = _place()
        for a in range(n):
            copy(ins, outs, sems, a, 0, (x, y, 1 - c), (x, y, c)).wait_recv()
            for j, chip in enumerate(chips):
                copy(ins, outs, sems, a, 4 + j, (*chip, 1 - c), (x, y, c)).wait_recv()
        for a in range(n):
            for j, chip in enumerate(chips):
                copy(ins, outs, sems, a, 1 + j, (x, y, c), (*chip, c), src=ins[a]).wait_send()
                copy(ins, outs, sems, a, 4 + j, (*chip, c), (x, y, 1 - c)).wait_send()
            copy(ins, outs, sems, a, 0, (x, y, c), (x, y, 1 - c), src=ins[a]).wait_send()
            local(ins, outs, sems, a).wait()

    return Comm(arrays, [jax.ShapeDtypeStruct((N_DEV,) + a.shape, a.dtype) for a in arrays],
                [pltpu.SemaphoreType.DMA((n, 7)), pltpu.SemaphoreType.DMA((n, 7)), pltpu.SemaphoreType.DMA((n,))],
                start, forward, finish)


def comm_direct(arrays, scatter):
    n = len(arrays)

    def peers():
        x, y, c = lax.axis_index("x"), lax.axis_index("y"), lax.axis_index("c")
        out = []
        for r in range(1, N_DEV):
            px, py, pc = (1 - x if r & 4 else x), (1 - y if r & 2 else y), (1 - c if r & 1 else c)
            out.append((r, (px, py, pc), 4 * px + 2 * py + pc))
        return 4 * x + 2 * y + c, out

    def remote(ins, outs, sems, a, r, dev, src_slot, dst_slot):
        return pltpu.make_async_remote_copy(
            src_ref=ins[a].at[src_slot] if scatter[a] else ins[a], dst_ref=outs[a].at[dst_slot],
            send_sem=sems[0].at[a, r - 1], recv_sem=sems[1].at[a, r - 1], device_id=dev, device_id_type=MESH)

    def local(ins, outs, sems, a, me):
        return pltpu.make_async_copy(ins[a].at[me] if scatter[a] else ins[a], outs[a].at[me], sems[2].at[a])

    def start(ins, outs, sems):
        me, ps = peers()
        for a in range(n):
            local(ins, outs, sems, a, me).start()
        for a in range(n):
            for r, dev, lin in ps:
                remote(ins, outs, sems, a, r, dev, lin, me).start()

    def finish(ins, outs, sems):
        me, ps = peers()
        for a in range(n):
            for r, dev, lin in ps:
                remote(ins, outs, sems, a, r, dev, lin, lin).wait_recv()
        for a in range(n):
            for r, dev, lin in ps:
                remote(ins, outs, sems, a, r, dev, lin, me).wait_send()
            local(ins, outs, sems, a, me).wait()

    return Comm(arrays, [jax.ShapeDtypeStruct(a.shape if sc else (N_DEV,) + a.shape, a.dtype) for a, sc in zip(arrays, scatter)],
                [pltpu.SemaphoreType.DMA((n, N_DEV - 1)), pltpu.SemaphoreType.DMA((n, N_DEV - 1)),
                 pltpu.SemaphoreType.DMA((n,))], start, None, finish)


def run_comm(comm, *, name):
    n = len(comm.arrays)

    def body(*refs):
        ins, outs, sems = refs[:n], refs[n:2 * n], refs[2 * n:]
        comm.start(ins, outs, sems)
        if comm.forward is not None:
            comm.forward(ins, outs, sems)
        comm.finish(ins, outs, sems)

    any_spec = pl.BlockSpec(memory_space=pl.ANY)
    return pl.pallas_call(body, name=name, in_specs=[any_spec] * n, out_specs=[any_spec] * n, out_shape=comm.out_shape,
                          scratch_shapes=comm.sems)(*comm.arrays)


def adamw(parts, w, m, v, *, name, tr=128):
    R, Cc = w.shape
    tr = min(tr, R)
    assert R % tr == 0
    n_parts = parts.shape[0]

    def body(p_ref, w_ref, m_ref, v_ref, g_ref, d_ref, nm_ref, nv_ref):
        g = p_ref[0].astype(F32)
        for s in range(1, n_parts):
            g = g + p_ref[s].astype(F32)
        nm = ADAM_B1 * m_ref[...] + (1.0 - ADAM_B1) * g
        nv = ADAM_B2 * v_ref[...] + (1.0 - ADAM_B2) * (g * g)
        m_hat = nm / (1.0 - ADAM_B1 ** ADAM_STEP)
        v_hat = nv / (1.0 - ADAM_B2 ** ADAM_STEP)
        g_ref[...] = g
        d_ref[...] = -ADAM_LR * (m_hat / (jnp.sqrt(v_hat) + ADAM_EPS) + ADAM_WD * w_ref[...])
        nm_ref[...] = nm
        nv_ref[...] = nv

    spec = pl.BlockSpec((tr, Cc), lambda i: (i, 0))
    return pl.pallas_call(
        body, name=name, grid=(R // tr,),
        in_specs=[pl.BlockSpec((n_parts, tr, Cc), lambda i: (0, i, 0)), spec, spec, spec], out_specs=[spec] * 4,
        out_shape=[jax.ShapeDtypeStruct((R, Cc), F32)] * 4, compiler_params=_params(("parallel",)))(parts, w, m, v)


def _lanes(vec, base):
    return jnp.pad(vec[None].astype(F32), ((0, 0), (base, LANES - base - vec.shape[0])))


def _col_shards(full_w):
    R, Ct = full_w.shape
    return jnp.transpose(full_w.reshape(R, N_DEV, Ct // N_DEV), (1, 0, 2))


def _from_col_shards(g):
    return jnp.transpose(g, (1, 0, 2)).reshape(g.shape[1], -1)


def kernel(x, mem, g_mix, w_in, conv_w, a_log, dt_bias, gdn_norm_g, fox_b_f, fox_q_norm, fox_k_norm, g_mem, w_mem_kv, mem_q_norm, mem_k_norm, w_up_gdn, w_up_fox, w_up_mem, w_out, g_mlp, w_ff1, w_ff2, loss_target, m_g_mix, m_w_in, m_conv_w, m_a_log, m_dt_bias, m_gdn_norm_g, m_fox_b_f, m_fox_q_norm, m_fox_k_norm, m_g_mem, m_w_mem_kv, m_mem_q_norm, m_mem_k_norm, m_w_up_gdn, m_w_up_fox, m_w_up_mem, m_w_out, m_g_mlp, m_w_ff1, m_w_ff2, v_g_mix, v_w_in, v_conv_w, v_a_log, v_dt_bias, v_gdn_norm_g, v_fox_b_f, v_fox_q_norm, v_fox_k_norm, v_g_mem, v_w_mem_kv, v_mem_q_norm, v_mem_k_norm, v_w_up_gdn, v_w_up_fox, v_w_up_mem, v_w_out, v_g_mlp, v_w_ff1, v_w_ff2):
    loc = dict(locals())
    big_names = ["w_in", "conv_w", "w_mem_kv", "w_up_gdn", "w_up_fox", "w_up_mem", "w_out", "w_ff1", "w_ff2"]
    col_sharded = {"w_in", "conv_w", "w_up_gdn", "w_up_fox", "w_up_mem", "w_ff1"}
    small_names = ["g_mix", "a_log", "dt_bias", "gdn_norm_g", "fox_b_f", "fox_q_norm", "fox_k_norm", "g_mem",
                   "mem_q_norm", "mem_k_norm", "g_mlp"]

    xs, tgt, mems = x[0], loss_target[0], mem[0]
    T, D = xs.shape
    HG = a_log.shape[1]
    HF = fox_b_f.shape[1]
    DM = mem_q_norm.shape[1]
    GQK, GV = HG * HEAD, HG * HEAD
    GQKV = 2 * GQK + GV
    FW = HF * HEAD
    MW = w_mem_kv.shape[2] // 2
    HM = MW // DM
    assert HG <= 8 and HF <= 8

    shard = {n: loc[n][0].astype(BF16) for n in big_names}
    first, rest = big_names[:2], big_names[2:]
    W = {}

    def take(names, gathered):
        for n, g in zip(names, gathered):
            W[n] = _from_col_shards(g) if n in col_sharded else g.reshape(-1, g.shape[2])

    take(first, run_comm(comm_gather([shard[n] for n in first]), name="gather_in"))
    widths = [GQKV, GV, HG, HG, FW, FW, FW, HF, MW, 3 * D]
    offs = np.concatenate([[0], np.cumsum(widths)]).tolist()
    seg = [W["w_in"][:, offs[i]:offs[i + 1]] for i in range(len(widths))]
    w_big = jnp.concatenate([seg[0], seg[1], seg[4], seg[5], seg[6], seg[8], seg[9]], axis=1)
    pad8 = lambda s: jnp.pad(s, ((0, 0), (0, 8 - s.shape[1])))
    w_small = jnp.concatenate([pad8(seg[2]), pad8(seg[3]), pad8(seg[7]), jnp.zeros((D, LANES - 24), BF16)], axis=1)
    o_z, o_fq, o_fk, o_fv = GQKV, GQKV + GV, GQKV + GV + FW, GQKV + GV + 2 * FW
    o_mq = o_fv + FW
    o_gt = o_mq + MW
    WB = o_gt + 3 * D
    conv_full = W["conv_w"].astype(F32)

    a_log_l, dt_bias_l, b_f_l = _lanes(a_log[0], LANE_A), _lanes(dt_bias[0], LANE_A), _lanes(fox_b_f[0], LANE_F)
    rms_fn = lambda t, g: (_rms(t, g),)

    h, = rowwise(rms_fn, [full(xs)], [g_mix], [(D, D, BF16)], name="rms_mix")
    big, *gathered = matmul(h, w_big, mode="nn", name="proj_big", comm=comm_gather([shard[n] for n in rest]))
    take(rest, gathered)
    small = matmul(h, w_small, mode="nn", name="proj_small", out_dtypes=(F32,))
    bgf, = rowwise(_gates_fn, [full(small)], [a_log_l, dt_bias_l, b_f_l], [(LANES, LANES, F32)], name="gates")

    cn = conv_fwd(big, conv_full, HG, name="conv")
    o_gdn, ss = gdn_fwd(cn, bgf, HG, name="gdn_fwd")
    oa, = rowwise(_gated_norm_fn, [(o_gdn, HEAD, 0, 1), (big, HEAD, o_z // HEAD, 1)], [gdn_norm_g],
                  [(GV, HEAD, BF16)], name="gated_norm", ncb=HG)

    fqn, = rowwise(rms_fn, [(big, HEAD, o_fq // HEAD, 1)], [fox_q_norm], [(FW, HEAD, BF16)], name="fox_qnorm", ncb=HF)
    fkn, = rowwise(rms_fn, [(big, HEAD, o_fk // HEAD, 1)], [fox_k_norm], [(FW, HEAD, BF16)], name="fox_knorm", ncb=HF)
    fv = big[:, o_fv:o_fv + FW]
    cum = cumsum_tokens(bgf, reverse=False, name="cumsum")
    cum_t = jnp.pad(cum[:, LANE_F:LANE_F + HF].T, ((0, 8 - HF), (0, 0)))
    ob, lse = fox_fwd(fqn, fkn, fv, cum, cum_t, HF, name="fox_fwd")

    memn, = rowwise(rms_fn, [full(mems)], [g_mem], [(D, D, BF16)], name="rms_mem")
    kv_m = matmul(memn, W["w_mem_kv"], mode="nn", name="mem_kv", out_dtypes=(F32,))
    kmn, = rowwise(rms_fn, [(kv_m, DM, 0, 1)], [mem_k_norm], [(MW, DM, F32)], name="mem_knorm", ncb=HM)
    vm = kv_m[:, MW:]
    mem_fn = functools.partial(_mem_attn_fn, HM, DM)
    om, = rowwise(mem_fn, [(big, MW, o_mq // MW, 0)], [kmn, vm, mem_q_norm], [(MW, MW, BF16)], name="mem_attn")

    ya = matmul(oa, W["w_up_gdn"], mode="nn", name="up_gdn")
    yb = matmul(ob, W["w_up_fox"], mode="nn", name="up_fox")
    ym = matmul(om, W["w_up_mem"], mode="nn", name="up_mem")
    cbm = min(512, D)
    gate_rows = [(big, cbm, (o_gt + b * D) // cbm, 1) for b in range(3)]
    merge_rows = [(ya, cbm, 0, 1), (yb, cbm, 0, 1), (ym, cbm, 0, 1)] + gate_rows
    y, = rowwise(_merge_fn, merge_rows, [], [(D, cbm, BF16)], name="merge", ncb=D // cbm)
    x1 = matmul(y, W["w_out"], mode="nn", name="out_proj", out_dtypes=(F32,), extras=(xs,),
                epi=lambda r, res: (r + res,))

    h2, = rowwise(rms_fn, [full(x1)], [g_mlp], [(D, D, BF16)], name="rms_mlp")
    u_ff, a_ff = matmul(h2, W["w_ff1"], mode="nn", name="ff1", out_dtypes=(BF16, BF16),
                        epi=lambda r: (r, jnp.square(jnp.maximum(r, 0.0))))
    d_out = matmul(a_ff, W["w_ff2"], mode="nn", name="ff2_loss", out_dtypes=(F32,), extras=(x1, tgt),
                   epi=lambda r, res, t: ((r + res - t) * (1.0 / D),))
    loss_local = 0.5 * D * sum_squares(d_out, name="loss_sum")[0, 0]
    loss = lax.psum(loss_local, ("x", "y", "c"))

    G = {}
    d_u = matmul(d_out, W["w_ff2"], mode="nt", name="d_ff2_in", extras=(u_ff,),
                 epi=lambda r, u: (r * 2.0 * jnp.maximum(u.astype(F32), 0.0),))
    d_h2 = matmul(d_u, W["w_ff1"], mode="nt", name="d_ff1_in")
    d_x1, G["g_mlp"] = rowwise_bwd(rms_fn, [full(x1)], [g_mlp], [full(d_h2)], [(0, F32)], [0], name="d_rms_mlp",
                                   adds=[full(d_out)])
    d_y = matmul(d_x1, W["w_out"], mode="nt", name="d_out_proj_in")
    G["w_out"] = matmul(y, d_x1, mode="tn", name="d_w_out")
    d_ya, d_yb, d_ym, d_ga, d_gb, d_gm = rowwise_bwd(
        _merge_fn, merge_rows, [], [(d_y, cbm, 0, 1)], [(k, BF16) for k in range(6)], [], name="d_merge", ncb=D // cbm)
    d_oa = matmul(d_ya, W["w_up_gdn"], mode="nt", name="d_up_gdn_in")
    d_ob = matmul(d_yb, W["w_up_fox"], mode="nt", name="d_up_fox_in")
    d_om = matmul(d_ym, W["w_up_mem"], mode="nt", name="d_up_mem_in")
    G["w_up_gdn"] = matmul(oa, d_ya, mode="tn", name="d_w_up_gdn")
    G["w_up_fox"] = matmul(ob, d_yb, mode="tn", name="d_w_up_fox")
    G["w_up_mem"] = matmul(om, d_ym, mode="tn", name="d_w_up_mem")

    d_mq, d_kmn, d_vm, G["mem_q_norm"] = rowwise_bwd(
        mem_fn, [(big, MW, o_mq // MW, 0)], [kmn, vm, mem_q_norm], [full(d_om)], [(0, BF16)], [0, 1, 2], name="d_mem_attn")
    d_km, G["mem_k_norm"] = rowwise_bwd(rms_fn, [(kv_m, DM, 0, 1)], [mem_k_norm], [(d_kmn, DM, 0, 1)], [(0, F32)], [0],
                                         name="d_mem_knorm", ncb=HM)
    d_kv_m = jnp.concatenate([d_km, d_vm], axis=1)
    G["w_mem_kv"] = matmul(memn, d_kv_m, mode="tn", name="d_w_mem_kv")
    d_memn = matmul(d_kv_m, W["w_mem_kv"], mode="nt", name="d_mem_kv_in")
    _, G["g_mem"] = rowwise_bwd(rms_fn, [full(mems)], [g_mem], [full(d_memn)], [(0, BF16)], [0], name="d_rms_mem")

    d_fkn, d_fv, d_cum_t = fox_bwd_kv(fqn, fkn, fv, d_ob, ob, cum, cum_t, lse, HF, name="fox_bwd_kv")
    d_fqn = fox_bwd_q(fqn, fkn, fv, d_ob, ob, cum, cum_t, lse, HF, name="fox_bwd_q")
    d_fq, G["fox_q_norm"] = rowwise_bwd(rms_fn, [(big, HEAD, o_fq // HEAD, 1)], [fox_q_norm], [(d_fqn, HEAD, 0, 1)],
                                         [(0, BF16)], [0], name="d_fox_qnorm", ncb=HF)
    d_fk, G["fox_k_norm"] = rowwise_bwd(rms_fn, [(big, HEAD, o_fk // HEAD, 1)], [fox_k_norm], [(d_fkn, HEAD, 0, 1)],
                                         [(0, BF16)], [0], name="d_fox_knorm", ncb=HF)
    d_cum = jnp.pad(d_cum_t[:HF].T, ((0, 0), (LANE_F, LANES - LANE_F - HF)))
    d_logf = cumsum_tokens(d_cum, reverse=True, name="cumsum_rev")

    d_o_gdn, d_z, G["gdn_norm_g"] = rowwise_bwd(
        _gated_norm_fn, [(o_gdn, HEAD, 0, 1), (big, HEAD, o_z // HEAD, 1)], [gdn_norm_g], [(d_oa, HEAD, 0, 1)],
        [(0, F32), (1, BF16)], [0], name="d_gated_norm", ncb=HG)
    d_cn, d_bg = gdn_bwd(cn, bgf, ss, d_o_gdn, HG, name="gdn_bwd")
    d_conv_y, G["conv_w"] = conv_bwd_taps(big, conv_full, d_cn, HG, name="d_conv_taps")
    d_qkv = conv_bwd_input(d_conv_y, conv_full, name="d_conv_in")
    d_small, d_al, d_dt, d_bf = rowwise_bwd(_gates_fn, [full(small)], [a_log_l, dt_bias_l, b_f_l], [full(d_bg + d_logf)],
                                            [(0, F32)], [0, 1, 2], name="d_gates")
    G["a_log"], G["dt_bias"], G["fox_b_f"] = (d_al[:, LANE_A:LANE_A + HG], d_dt[:, LANE_A:LANE_A + HG],
                                               d_bf[:, LANE_F:LANE_F + HF])

    def parts(n):
        g = G[n].astype(BF16)
        return _col_shards(g) if n in col_sharded else g.reshape(N_DEV, -1, g.shape[1])

    recv = {}

    def carried(names, out):
        for n, r in zip(names, out):
            recv[n] = r

    d_big = jnp.concatenate([d_qkv, d_z, d_fq, d_fk, d_fv, d_mq, d_ga, d_gb, d_gm], axis=1)
    group = ["conv_w", "w_mem_kv", "w_up_gdn", "w_up_fox", "w_up_mem", "w_out"]
    G["w_ff2"], *out = matmul(a_ff, d_out, mode="tn", name="d_w_ff2",
                              comm=comm_direct([parts(n) for n in group], [True] * len(group)))
    carried(group, out)
    G["w_ff1"], *out = matmul(h2, d_u, mode="tn", name="d_w_ff1", comm=comm_direct([parts("w_ff2")], [True]))
    carried(["w_ff2"], out)
    g_big, *out = matmul(h, d_big, mode="tn", name="d_w_big", comm=comm_direct([parts("w_ff1")], [True]))
    carried(["w_ff1"], out)
    g_small = matmul(h, d_small, mode="tn", name="d_w_small", out_dtypes=(F32,))
    cols = lambda a, o, wd: a[:, o:o + wd]
    G["w_in"] = jnp.concatenate([
        cols(g_big, 0, GQKV), cols(g_big, o_z, GV), cols(g_small, LANE_B, HG), cols(g_small, LANE_A, HG),
        cols(g_big, o_fq, FW), cols(g_big, o_fk, FW), cols(g_big, o_fv, FW), cols(g_small, LANE_F, HF),
        cols(g_big, o_mq, MW), cols(g_big, o_gt, 3 * D)], axis=1)
    d_h_s = matmul(d_small, w_small, mode="nt", name="d_proj_small_in", out_dtypes=(F32,))
    d_h, *out = matmul(d_big, w_big, mode="nt", name="d_proj_big_in", extras=(d_h_s,), epi=lambda r, e: (r + e,),
                       comm=comm_direct([parts("w_in")], [True]))
    carried(["w_in"], out)
    grad_x, G["g_mix"] = rowwise_bwd(rms_fn, [full(xs)], [g_mix], [full(d_h)], [(0, F32)], [0], name="d_rms_mix",
                                     adds=[full(d_x1)])
    grad_x = grad_x[None]

    small_sizes = [loc[n].shape[1] for n in small_names]
    pack = lambda d: jnp.concatenate([d[n].reshape(1, -1) for n in small_names], axis=1)
    npad = -sum(small_sizes) % LANES
    padp = lambda a: jnp.pad(a, ((0, 0), (0, npad)))
    recv_small, = run_comm(comm_direct([padp(pack(G))], [False]), name="gather_small_grads")

    res = {}
    for n in big_names:
        res[n] = [t[None] for t in adamw(recv[n], loc[n][0], loc["m_" + n][0], loc["v_" + n][0], name="adamw_" + n)]
    sm = adamw(recv_small, padp(pack({n: loc[n] for n in small_names})), padp(pack({n: loc["m_" + n] for n in small_names})),
               padp(pack({n: loc["v_" + n] for n in small_names})), name="adamw_small")
    so = np.concatenate([[0], np.cumsum(small_sizes)]).tolist()
    for i, n in enumerate(small_names):
        res[n] = [t[:, so[i]:so[i + 1]] for t in sm]

    order = ["g_mix", "w_in", "conv_w", "a_log", "dt_bias", "gdn_norm_g", "fox_b_f", "fox_q_norm", "fox_k_norm", "g_mem",
             "w_mem_kv", "mem_q_norm", "mem_k_norm", "w_up_gdn", "w_up_fox", "w_up_mem", "w_out", "g_mlp", "w_ff1", "w_ff2"]
    return (loss, grad_x, *[res[n][0] for n in order], *[res[n][1] for n in order],
            *[res[n][2] for n in order], *[res[n][3] for n in order])
```

```python
import functools

import jax
import jax.numpy as jnp
import numpy as np
from jax import lax
from jax.experimental import pallas as pl
from jax.experimental.pallas import tpu as pltpu

F32 = jnp.float32
BF16 = jnp.bfloat16
HI = lax.Precision.HIGHEST

EPS = 1e-6
GDN_CHUNK = 64
GDN_CONV = 4
HEAD = 128
LANES = 128
HALO = 16
N_DEV = 8
MESH = pl.DeviceIdType.MESH
VMEM_LIMIT_V7X = 56 * 1024 * 1024

ADAM_LR, ADAM_B1, ADAM_B2, ADAM_EPS, ADAM_WD, ADAM_STEP = 0.001, 0.9, 0.999, 1e-08, 0.01, 10

LANE_B, LANE_A, LANE_F = 0, 8, 16


def _params(sem):
    return pltpu.CompilerParams(dimension_semantics=sem, vmem_limit_bytes=VMEM_LIMIT_V7X)


def _dg(a, b, ca, cb, prec):
    nb = a.ndim - 2
    batch = tuple(range(nb))
    return lax.dot_general(a, b, (((ca + nb,), (cb + nb,)), (batch, batch)), precision=prec,
                           preferred_element_type=F32)


def _make_mm(prec, cast):
    def c(x):
        return x.astype(BF16) if cast else x

    @jax.custom_vjp
    def nn(a, b):
        return _dg(c(a), c(b), 1, 0, prec)

    @jax.custom_vjp
    def nt(a, b):
        return _dg(c(a), c(b), 1, 1, prec)

    @jax.custom_vjp
    def tn(a, b):
        return _dg(c(a), c(b), 0, 0, prec)

    nn.defvjp(lambda a, b: (nn(a, b), (a, b)), lambda r, g: (nt(g, r[1]), tn(r[0], g)))
    nt.defvjp(lambda a, b: (nt(a, b), (a, b)), lambda r, g: (nn(g, r[1]), tn(g, r[0])))
    tn.defvjp(lambda a, b: (tn(a, b), (a, b)), lambda r, g: (nt(r[1], g), nn(r[0], g)))
    return nn, nt, tn


NN, NT, TN = _make_mm(None, True)
NNH, NTH, TNH = _make_mm(lax.Precision.HIGH, False)
NNX, _, _ = _make_mm(HI, False)


def _sigmoid(x):
    return 1.0 / (1.0 + jnp.exp(-x))


def _silu(x):
    return x * _sigmoid(x)


def _softplus(x):
    return jnp.maximum(x, 0.0) + jnp.log(1.0 + jnp.exp(-jnp.abs(x)))


def _log_sigmoid(x):
    return -_softplus(-x)


def _rms(x, g):
    return x * lax.rsqrt(jnp.mean(x * x, axis=-1, keepdims=True) + EPS) * g


def _tile(n, target):
    t = target
    while t >= LANES:
        if n % t == 0:
            return t
        t //= 2
    return n


def matmul(a, b, *, mode, name, out_dtypes=(BF16,), epi=None, extras=(), tm=1024, tn=1024, tk=2048, comm=None):
    if mode == "nn":
        (M, K), (K2, N) = a.shape, b.shape
    elif mode == "nt":
        (M, K), (N, K2) = a.shape, b.shape
    else:
        (K, M), (K2, N) = a.shape, b.shape
    assert K == K2, (name, a.shape, b.shape)
    tm, tn, tk = _tile(M, tm), _tile(N, tn), _tile(K, tk)
    ni, nj, nk = M // tm, N // tn, K // tk
    a_spec = (pl.BlockSpec((tk, tm), lambda i, j, k: (k, i)) if mode == "tn"
              else pl.BlockSpec((tm, tk), lambda i, j, k: (i, k)))
    b_spec = (pl.BlockSpec((tn, tk), lambda i, j, k: (j, k)) if mode == "nt"
              else pl.BlockSpec((tk, tn), lambda i, j, k: (k, j)))
    o_spec = pl.BlockSpec((tm, tn), lambda i, j, k: (i, j))
    dims = {"nn": ((1,), (0,)), "nt": ((1,), (1,)), "tn": ((0,), (0,))}[mode]
    ne, no = len(extras), len(out_dtypes)
    nc = len(comm.arrays) if comm is not None else 0
    n_steps = ni * nj * nk

    def body(a_ref, b_ref, *rest):
        ex, c_in = rest[:ne], rest[ne:ne + nc]
        outs, c_out = rest[ne + nc:ne + nc + no], rest[ne + nc + no:ne + nc + no + nc]
        acc = rest[ne + nc + no + nc]
        c_sems = rest[ne + nc + no + nc + 1:]
        k = pl.program_id(2)
        step = (pl.program_id(0) * nj + pl.program_id(1)) * nk + k

        if comm is not None:
            pl.when(step == 0)(lambda: comm.start(c_in, c_out, c_sems))
            if comm.forward is not None:
                pl.when(step == n_steps // 2)(lambda: comm.forward(c_in, c_out, c_sems))

        @pl.when(k == 0)
        def _():
            acc[...] = jnp.zeros_like(acc)

        acc[...] += lax.dot_general(a_ref[...].astype(BF16), b_ref[...].astype(BF16), (dims, ((), ())),
                                    preferred_element_type=F32)

        @pl.when(k == nk - 1)
        def _():
            r = acc[...]
            vals = epi(r, *[e[...] for e in ex]) if epi is not None else (r,)
            for o, v in zip(outs, vals):
                o[...] = v.astype(o.dtype)

        if comm is not None:
            pl.when(step == n_steps - 1)(lambda: comm.finish(c_in, c_out, c_sems))

    any_spec = pl.BlockSpec(memory_space=pl.ANY)
    sem = ("arbitrary",) * 3 if comm is not None else ("parallel", "parallel", "arbitrary")
    assert comm is None or comm.forward is None or n_steps >= 3
    out = pl.pallas_call(
        body, name=name, grid=(ni, nj, nk),
        in_specs=[a_spec, b_spec] + [o_spec] * ne + [any_spec] * nc, out_specs=[o_spec] * no + [any_spec] * nc,
        out_shape=[jax.ShapeDtypeStruct((M, N), d) for d in out_dtypes] + (comm.out_shape if comm is not None else []),
        scratch_shapes=[pltpu.VMEM((tm, tn), F32)] + (comm.sems if comm is not None else []),
        compiler_params=_params(sem))(a, b, *extras, *(comm.arrays if comm is not None else []))
    return out[0] if len(out) == 1 else out


def _row_spec(tr, cb, off, moves):
    return pl.BlockSpec((tr, cb), lambda i, j: (i, off + moves * j))


def _whole_spec(p):
    return pl.BlockSpec(p.shape, lambda i, j: (0,) * p.ndim)


def _row_tile(T, rows):
    widest = max(cb for (_, cb, _, _) in rows)
    return min(T, max(512, (1 << 19) // widest))


def rowwise(fn, rows, params, outs, *, name, ncb=1, tr=None):
    T = rows[0][0].shape[0]
    tr = min(tr, T) if tr else _row_tile(T, rows)
    assert T % tr == 0
    nr, npar = len(rows), len(params)

    def body(*refs):
        r, p, o = refs[:nr], refs[nr:nr + npar], refs[nr + npar:]
        vals = fn(*[x[...].astype(F32) for x in r], *[x[...] for x in p])
        for oref, v in zip(o, vals):
            oref[...] = v.astype(oref.dtype)

    res = pl.pallas_call(
        body, name=name, grid=(T // tr, ncb),
        in_specs=[_row_spec(tr, cb, off, mv) for (_, cb, off, mv) in rows] + [_whole_spec(p) for p in params],
        out_specs=[_row_spec(tr, cb, 0, 1) for (_, cb, _) in outs],
        out_shape=[jax.ShapeDtypeStruct((T, cols), d) for (cols, _, d) in outs],
        compiler_params=_params(("parallel", "parallel")))(*[r[0] for r in rows], *params)
    return res


def rowwise_bwd(fn, rows, params, cots, drows, dparams, *, name, ncb=1, tr=None, adds=()):
    T = rows[0][0].shape[0]
    tr = min(tr, T) if tr else _row_tile(T, rows)
    assert T % tr == 0
    nr, npar, nc, ndr, na = len(rows), len(params), len(cots), len(drows), len(adds)

    def body(*refs):
        r, p, c = refs[:nr], refs[nr:nr + npar], refs[nr + npar:nr + npar + nc]
        base = nr + npar + nc + na
        ad, o_r, o_p = refs[base - na:base], refs[base:base + ndr], refs[base + ndr:]
        prim = [x[...].astype(F32) for x in r] + [x[...] for x in p]
        _, vjp = jax.vjp(lambda *a: tuple(fn(*a)), *prim)
        g = vjp(tuple(x[...].astype(F32) for x in c))
        for k, (oref, (idx, _)) in enumerate(zip(o_r, drows)):
            val = g[idx] + ad[k][...].astype(F32) if k < na else g[idx]
            oref[...] = val.astype(oref.dtype)
        first = jnp.logical_and(pl.program_id(0) == 0, pl.program_id(1) == 0)

        @pl.when(first)
        def _():
            for oref in o_p:
                oref[...] = jnp.zeros_like(oref)

        for oref, idx in zip(o_p, dparams):
            oref[...] += g[nr + idx]

    res = pl.pallas_call(
        body, name=name, grid=(T // tr, ncb),
        in_specs=([_row_spec(tr, cb, off, mv) for (_, cb, off, mv) in rows] + [_whole_spec(p) for p in params]
                  + [_row_spec(tr, cb, off, mv) for (_, cb, off, mv) in tuple(cots) + tuple(adds)]),
        out_specs=([_row_spec(tr, rows[idx][1], 0, 1) for (idx, _) in drows]
                   + [_whole_spec(params[idx]) for idx in dparams]),
        out_shape=([jax.ShapeDtypeStruct((T, ncb * rows[idx][1] if rows[idx][3] else rows[idx][1]), d)
                    for (idx, d) in drows]
                   + [jax.ShapeDtypeStruct(params[idx].shape, F32) for idx in dparams]),
        compiler_params=_params(("arbitrary", "arbitrary")))(
            *[r[0] for r in rows], *params, *[c[0] for c in cots], *[a[0] for a in adds])
    return res


def full(a):
    return (a, a.shape[1], 0, 0)


def cumsum_tokens(x, *, reverse, name, tb=256):
    T = x.shape[0]
    tb = min(tb, T)
    nb = T // tb
    idx = (lambda i: (nb - 1 - i, 0)) if reverse else (lambda i: (i, 0))

    def body(x_ref, o_ref, carry):
        @pl.when(pl.program_id(0) == 0)
        def _():
            carry[...] = jnp.zeros_like(carry)

        ii = lax.broadcasted_iota(jnp.int32, (tb, tb), 0)
        jj = lax.broadcasted_iota(jnp.int32, (tb, tb), 1)
        tri = ((ii <= jj) if reverse else (ii >= jj)).astype(F32)
        c = lax.dot_general(tri, x_ref[...], (((1,), (0,)), ((), ())), precision=HI,
                            preferred_element_type=F32) + carry[0:1, :]
        o_ref[...] = c
        carry[0:1, :] = c[0:1, :] if reverse else c[tb - 1:tb, :]

    return pl.pallas_call(
        body, name=name, grid=(nb,), in_specs=[pl.BlockSpec((tb, LANES), idx)],
        out_specs=pl.BlockSpec((tb, LANES), idx), out_shape=jax.ShapeDtypeStruct((T, LANES), F32),
        scratch_shapes=[pltpu.VMEM((8, LANES), F32)], compiler_params=_params(("arbitrary",)))(x)


def _conv_post(y, kind, dk):
    c = _silu(y)
    r = lax.rsqrt(jnp.sum(c * c, axis=-1, keepdims=True) + EPS)
    return jnp.where(kind == 0, c * r * (dk ** -0.5), jnp.where(kind == 1, c * r, c))


def _conv_taps(cur, prev, w, tr):
    ext = jnp.concatenate([prev, cur], axis=0)
    y = w[3:4, :] * cur
    for d in (1, 2, 3):
        y = y + w[3 - d:4 - d, :] * pltpu.roll(ext, d, 0)[HALO:HALO + tr]
    return y


def conv_fwd(big, w, n_qk_heads, *, name, tr=2048):
    T, W = big.shape[0], w.shape[1]
    tr = min(tr, T)
    nh = W // HEAD

    def body(cur_ref, prev_ref, w_ref, o_ref):
        i, j = pl.program_id(0), pl.program_id(1)
        prev = jnp.where(i > 0, prev_ref[...].astype(F32), 0.0)
        y = _conv_taps(cur_ref[...].astype(F32), prev, w_ref[...], tr)
        kind = jnp.where(j < n_qk_heads, 0, jnp.where(j < 2 * n_qk_heads, 1, 2))
        o_ref[...] = _conv_post(y, kind, HEAD)

    return pl.pallas_call(
        body, name=name, grid=(T // tr, nh),
        in_specs=[pl.BlockSpec((tr, HEAD), lambda i, j: (i, j)),
                  pl.BlockSpec((HALO, HEAD), lambda i, j: (jnp.maximum(i * (tr // HALO) - 1, 0), j)),
                  pl.BlockSpec((GDN_CONV, HEAD), lambda i, j: (0, j))],
        out_specs=pl.BlockSpec((tr, HEAD), lambda i, j: (i, j)),
        out_shape=jax.ShapeDtypeStruct((T, W), F32), compiler_params=_params(("parallel", "parallel")))(big, big, w)


def conv_bwd_taps(big, w, dcn, n_qk_heads, *, name, tr=2048):
    T, W = big.shape[0], w.shape[1]
    tr = min(tr, T)
    nh = W // HEAD

    def body(cur_ref, prev_ref, w_ref, g_ref, dy_ref, dw_ref):
        j, i = pl.program_id(0), pl.program_id(1)
        cur = cur_ref[...].astype(F32)
        prev = jnp.where(i > 0, prev_ref[...].astype(F32), 0.0)
        y = _conv_taps(cur, prev, w_ref[...], tr)
        kind = jnp.where(j < n_qk_heads, 0, jnp.where(j < 2 * n_qk_heads, 1, 2))
        _, vjp = jax.vjp(lambda t: _conv_post(t, kind, HEAD), y)
        dy, = vjp(g_ref[...])
        dy_ref[...] = dy
        ext = jnp.concatenate([prev, cur], axis=0)
        rows = [jnp.sum(dy * (cur if d == 0 else pltpu.roll(ext, d, 0)[HALO:HALO + tr]), axis=0, keepdims=True)
                for d in (3, 2, 1, 0)]

        @pl.when(i == 0)
        def _():
            dw_ref[...] = jnp.zeros_like(dw_ref)

        dw_ref[...] += jnp.concatenate(rows, axis=0)

    return pl.pallas_call(
        body, name=name, grid=(nh, T // tr),
        in_specs=[pl.BlockSpec((tr, HEAD), lambda j, i: (i, j)),
                  pl.BlockSpec((HALO, HEAD), lambda j, i: (jnp.maximum(i * (tr // HALO) - 1, 0), j)),
                  pl.BlockSpec((GDN_CONV, HEAD), lambda j, i: (0, j)),
                  pl.BlockSpec((tr, HEAD), lambda j, i: (i, j))],
        out_specs=[pl.BlockSpec((tr, HEAD), lambda j, i: (i, j)), pl.BlockSpec((GDN_CONV, HEAD), lambda j, i: (0, j))],
        out_shape=[jax.ShapeDtypeStruct((T, W), F32), jax.ShapeDtypeStruct((GDN_CONV, W), F32)],
        compiler_params=_params(("parallel", "arbitrary")))(big, big, w, dcn)


def conv_bwd_input(dy, w, *, name, tr=2048):
    T, W = dy.shape
    tr = min(tr, T)
    nrow = T // tr

    def body(cur_ref, nxt_ref, w_ref, o_ref):
        i = pl.program_id(0)
        cur = cur_ref[...]
        nxt = jnp.where(i < nrow - 1, nxt_ref[...], 0.0)
        ext = jnp.concatenate([cur, nxt], axis=0)
        w = w_ref[...]
        dx = w[3:4, :] * cur
        for d in (1, 2, 3):
            dx = dx + w[3 - d:4 - d, :] * pltpu.roll(ext, tr + HALO - d, 0)[0:tr]
        o_ref[...] = dx.astype(o_ref.dtype)

    return pl.pallas_call(
        body, name=name, grid=(nrow, W // HEAD),
        in_specs=[pl.BlockSpec((tr, HEAD), lambda i, j: (i, j)),
                  pl.BlockSpec((HALO, HEAD), lambda i, j: (jnp.minimum((i + 1) * (tr // HALO), T // HALO - 1), j)),
                  pl.BlockSpec((GDN_CONV, HEAD), lambda i, j: (0, j))],
        out_specs=pl.BlockSpec((tr, HEAD), lambda i, j: (i, j)),
        out_shape=jax.ShapeDtypeStruct((T, W), BF16), compiler_params=_params(("parallel", "parallel")))(dy, dy, w)


def _gdn_chunk(q, k, v, bg, S):
    H, C = q.shape[0], q.shape[1]
    ii = lax.broadcasted_iota(jnp.int32, (C, C), 0)
    jj = lax.broadcasted_iota(jnp.int32, (C, C), 1)
    lincl = (ii >= jj).astype(F32)
    strict, incl, eye = (ii > jj)[None], (ii >= jj)[None], (ii == jj)[None]
    gam2d = lax.dot_general(lincl, bg, (((1,), (0,)), ((), ())), precision=HI, preferred_element_type=F32)
    lane = lax.broadcasted_iota(jnp.int32, (H, 1, LANES), 2)
    hh = lax.broadcasted_iota(jnp.int32, (H, 1, LANES), 0)
    beta = jnp.sum(bg[None] * (lane == hh + LANE_B).astype(F32), axis=2, keepdims=True)
    gam = jnp.sum(gam2d[None] * (lane == hh + LANE_A).astype(F32), axis=2, keepdims=True)
    last = (lax.broadcasted_iota(jnp.int32, (1, C, 1), 1) == C - 1).astype(F32)
    gam_last = jnp.sum(gam * last, axis=1, keepdims=True)
    gam_row = NNX(jnp.ones((H, C, C), F32), jnp.where(eye, gam, 0.0))
    diff = gam - gam_row
    dec_s = jnp.where(strict, jnp.exp(jnp.where(strict, diff, 0.0)), 0.0)
    dec_i = jnp.where(incl, jnp.exp(jnp.where(incl, diff, 0.0)), 0.0)
    p = -(beta * NT(k, k) * dec_s)
    t = jnp.where(eye, 1.0, 0.0) + p
    for _ in range(5):
        p = NNH(p, p)
        t = t + NNH(t, p)
    eg = jnp.exp(gam)
    w = NNH(t, beta * eg * k)
    u0 = NNH(t, beta * v)
    qk = NT(q, k) * dec_i
    u = u0 - NN(w, S)
    o = NN(q * eg, S) + NN(qk, u)
    S2 = jnp.exp(gam_last) * S + TN(k * jnp.exp(gam_last - gam), u)
    return o, S2


def _heads(x, base, H):
    return jnp.stack([x[:, base + h * HEAD:base + (h + 1) * HEAD] for h in range(H)])


def _unheads(x):
    return jnp.concatenate([x[h] for h in range(x.shape[0])], axis=1)


def gdn_fwd(cn, bgf, H, *, name):
    T, C, W = cn.shape[0], GDN_CHUNK, H * HEAD
    N = T // C

    def body(cn_ref, bg_ref, o_ref, ss_ref, s_scr):
        @pl.when(pl.program_id(0) == 0)
        def _():
            s_scr[...] = jnp.zeros_like(s_scr)

        x, S = cn_ref[...], s_scr[...]
        ss_ref[0] = S
        o, S2 = _gdn_chunk(_heads(x, 0, H), _heads(x, W, H), _heads(x, 2 * W, H), bg_ref[...], S)
        o_ref[...] = _unheads(o)
        s_scr[...] = S2

    return pl.pallas_call(
        body, name=name, grid=(N,),
        in_specs=[pl.BlockSpec((C, 3 * W), lambda n: (n, 0)), pl.BlockSpec((C, LANES), lambda n: (n, 0))],
        out_specs=[pl.BlockSpec((C, W), lambda n: (n, 0)), pl.BlockSpec((1, H, HEAD, HEAD), lambda n: (n, 0, 0, 0))],
        out_shape=[jax.ShapeDtypeStruct((T, W), F32), jax.ShapeDtypeStruct((N, H, HEAD, HEAD), F32)],
        scratch_shapes=[pltpu.VMEM((H, HEAD, HEAD), F32)], compiler_params=_params(("arbitrary",)))(cn, bgf)


def gdn_bwd(cn, bgf, ss, do, H, *, name):
    T, C, W = cn.shape[0], GDN_CHUNK, H * HEAD
    N = T // C

    def body(cn_ref, bg_ref, ss_ref, do_ref, dcn_ref, dbg_ref, ds_scr):
        @pl.when(pl.program_id(0) == 0)
        def _():
            ds_scr[...] = jnp.zeros_like(ds_scr)

        x = cn_ref[...]
        _, vjp = jax.vjp(_gdn_chunk, _heads(x, 0, H), _heads(x, W, H), _heads(x, 2 * W, H), bg_ref[...], ss_ref[0])
        dq, dk, dv, dbg, dS = vjp((_heads(do_ref[...], 0, H), ds_scr[...]))
        dcn_ref[...] = jnp.concatenate([_unheads(dq), _unheads(dk), _unheads(dv)], axis=1)
        dbg_ref[...] = dbg
        ds_scr[...] = dS

    rev = lambda n: (N - 1 - n, 0)
    return pl.pallas_call(
        body, name=name, grid=(N,),
        in_specs=[pl.BlockSpec((C, 3 * W), rev), pl.BlockSpec((C, LANES), rev),
                  pl.BlockSpec((1, H, HEAD, HEAD), lambda n: (N - 1 - n, 0, 0, 0)), pl.BlockSpec((C, W), rev)],
        out_specs=[pl.BlockSpec((C, 3 * W), rev), pl.BlockSpec((C, LANES), rev)],
        out_shape=[jax.ShapeDtypeStruct((T, 3 * W), F32), jax.ShapeDtypeStruct((T, LANES), F32)],
        scratch_shapes=[pltpu.VMEM((H, HEAD, HEAD), F32)], compiler_params=_params(("arbitrary",)))(cn, bgf, ss, do)


AUG = 2 * HEAD


def fox_prep(big, col_off, gain, cum, kind, H, *, name, tr=2048):
    T = big.shape[0]
    tr = min(tr, T)

    def body(x_ref, g_ref, c_ref, o_ref):
        h = pl.program_id(1)
        x = x_ref[...].astype(F32)
        lane = lax.broadcasted_iota(jnp.int32, (tr, HEAD), 1)
        if kind == 2:
            main, aug = x, jnp.ones((tr, HEAD), F32)
        else:
            main = _rms(x, g_ref[...]) * ((HEAD ** -0.5) if kind == 0 else 1.0)
            c = jnp.sum(jnp.where(lane == LANE_F + h, c_ref[...], 0.0), axis=1, keepdims=True)
            hi = c.astype(BF16).astype(F32)
            mid = (c - hi).astype(BF16).astype(F32)
            lo = c - hi - mid
            if kind == 0:
                aug = jnp.where(lane == 0, hi, jnp.where(lane == 1, mid, jnp.where(lane == 2, lo,
                                                                                   jnp.where(lane < 6, 1.0, 0.0))))
            else:
                aug = jnp.where(lane < 3, 1.0, jnp.where(lane == 3, -hi, jnp.where(lane == 4, -mid,
                                                                                   jnp.where(lane == 5, -lo, 0.0))))
        o_ref[...] = jnp.concatenate([main, aug], axis=1).astype(o_ref.dtype)

    return pl.pallas_call(
        body, name=name, grid=(T // tr, H),
        in_specs=[pl.BlockSpec((tr, HEAD), lambda i, h: (i, col_off // HEAD + h)),
                  pl.BlockSpec((1, HEAD), lambda i, h: (0, 0)), pl.BlockSpec((tr, LANES), lambda i, h: (i, 0))],
        out_specs=pl.BlockSpec((tr, AUG), lambda i, h: (i, h)),
        out_shape=jax.ShapeDtypeStruct((T, H * AUG), BF16), compiler_params=_params(("parallel", "parallel")))(big, gain, cum)


def _fox_logits(q_ref, k_ref, h, tq, tk, diagonal):
    ha = slice(h * AUG, (h + 1) * AUG)
    s = lax.dot_general(q_ref[:, ha], k_ref[:, ha], (((1,), (1,)), ((), ())), preferred_element_type=F32)
    keep = None
    if diagonal:
        keep = lax.broadcasted_iota(jnp.int32, (tq, tk), 0) >= lax.broadcasted_iota(jnp.int32, (tq, tk), 1)
    return s, keep


def _both(cond_off, cond_diag, fn):
    pl.when(cond_off)(functools.partial(fn, False))
    pl.when(cond_diag)(functools.partial(fn, True))


def fox_fwd(qa, ka, va, H, *, name, tq=512, tk=512):
    T = qa.shape[0]
    tq, tk = min(tq, T), min(tk, T)
    nq, nk = T // tq, T // tk
    assert tq == tk

    def body(q_ref, k_ref, v_ref, o_ref, lse_ref, acc, m_scr):
        i, sweep, j = pl.program_id(0), pl.program_id(1), pl.program_id(2)

        @pl.when(jnp.logical_and(sweep == 0, j == 0))
        def _():
            m_scr[...] = jnp.full_like(m_scr, -jnp.inf)

        @pl.when(jnp.logical_and(sweep == 1, j == 0))
        def _():
            acc[...] = jnp.zeros_like(acc)

        def find_max(diagonal):
            lane = lax.broadcasted_iota(jnp.int32, (tq, LANES), 1)
            m_all = m_scr[...]
            new_m = m_all
            for h in range(H):
                s, keep = _fox_logits(q_ref, k_ref, h, tq, tk, diagonal)
                if diagonal:
                    s = jnp.where(keep, s, -jnp.inf)
                new_m = jnp.where(lane == h, jnp.maximum(m_all[:, h:h + 1], jnp.max(s, axis=1, keepdims=True)), new_m)
            m_scr[...] = new_m

        def accumulate(diagonal):
            m_all = m_scr[...]
            for h in range(H):
                ha = slice(h * AUG, (h + 1) * AUG)
                s, keep = _fox_logits(q_ref, k_ref, h, tq, tk, diagonal)
                p = jnp.exp(s - m_all[:, h:h + 1])
                if diagonal:
                    p = jnp.where(keep, p, 0.0)
                p_hi = p.astype(BF16)
                p_lo = (p - p_hi.astype(F32)).astype(BF16)
                pv = lambda t: lax.dot_general(t, v_ref[:, ha], (((1,), (0,)), ((), ())), preferred_element_type=F32)
                acc[:, ha] += pv(p_hi) + pv(p_lo)

        _both(jnp.logical_and(sweep == 0, j < i), jnp.logical_and(sweep == 0, j == i), find_max)
        _both(jnp.logical_and(sweep == 1, j < i), jnp.logical_and(sweep == 1, j == i), accumulate)

        @pl.when(jnp.logical_and(sweep == 1, j == nk - 1))
        def _():
            lane = lax.broadcasted_iota(jnp.int32, (tq, LANES), 1)
            m_all = m_scr[...]
            lse = jnp.zeros((tq, LANES), F32)
            for h in range(H):
                den = acc[:, h * AUG + HEAD:(h + 1) * AUG]
                o_ref[:, h * HEAD:(h + 1) * HEAD] = acc[:, h * AUG:h * AUG + HEAD] / den
                lse = jnp.where(lane == h, m_all[:, h:h + 1] + jnp.log(den[:, 0:1]), lse)
            lse_ref[...] = lse

    kv_idx = lambda i, s, j: (jnp.minimum(j, i), 0)
    q_idx = lambda i, s, j: (i, 0)
    return pl.pallas_call(
        body, name=name, grid=(nq, 2, nk),
        in_specs=[pl.BlockSpec((tq, H * AUG), q_idx), pl.BlockSpec((tk, H * AUG), kv_idx),
                  pl.BlockSpec((tk, H * AUG), lambda i, s, j: (s * jnp.minimum(j, i), 0))],
        out_specs=[pl.BlockSpec((tq, H * HEAD), q_idx), pl.BlockSpec((tq, LANES), q_idx)],
        out_shape=[jax.ShapeDtypeStruct((T, H * HEAD), F32), jax.ShapeDtypeStruct((T, LANES), F32)],
        scratch_shapes=[pltpu.VMEM((tq, H * AUG), F32), pltpu.VMEM((tq, LANES), F32)],
        compiler_params=_params(("parallel", "arbitrary", "arbitrary")))(qa, ka, va)


def _fox_ds(q_ref, k_ref, v_ref, do_ref, o_ref, lse_ref, h, tq, tk, diagonal):
    hs = slice(h * HEAD, (h + 1) * HEAD)
    s, keep = _fox_logits(q_ref, k_ref, h, tq, tk, diagonal)
    p = jnp.exp(s - lse_ref[:, h:h + 1])
    if diagonal:
        p = jnp.where(keep, p, 0.0)
    do = do_ref[:, hs]
    dp = lax.dot_general(do, v_ref[:, h * AUG:h * AUG + HEAD], (((1,), (1,)), ((), ())), preferred_element_type=F32)
    delta = jnp.sum(do.astype(F32) * o_ref[:, hs], axis=1, keepdims=True)
    return p, p * (dp - delta)


def fox_bwd_kv(qa, ka, va, do, o, lse, H, *, name, tq=512, tk=512):
    T, W = do.shape
    tq, tk = min(tq, T), min(tk, T)
    nq, nk = T // tq, T // tk

    def body(q_ref, k_ref, v_ref, do_ref, o_ref, lse_ref, dk_ref, dv_ref, dc_ref, dk_acc, dv_acc, dc_acc):
        j, i = pl.program_id(0), pl.program_id(1)

        @pl.when(i == 0)
        def _():
            dk_acc[...] = jnp.zeros_like(dk_acc)
            dv_acc[...] = jnp.zeros_like(dv_acc)
            dc_acc[...] = jnp.zeros_like(dc_acc)

        def tile(diagonal):
            row = lax.broadcasted_iota(jnp.int32, (8, tk), 0)
            dc = dc_acc[...]
            for h in range(H):
                hs = slice(h * HEAD, (h + 1) * HEAD)
                p, ds = _fox_ds(q_ref, k_ref, v_ref, do_ref, o_ref, lse_ref, h, tq, tk, diagonal)
                dv_acc[:, hs] += lax.dot_general(p.astype(BF16), do_ref[:, hs], (((0,), (0,)), ((), ())),
                                                 preferred_element_type=F32)
                dk_acc[:, hs] += lax.dot_general(ds.astype(BF16), q_ref[:, h * AUG:h * AUG + HEAD],
                                                 (((0,), (0,)), ((), ())), preferred_element_type=F32)
                dc = jnp.where(row == h, dc - jnp.sum(ds, axis=0, keepdims=True), dc)
            dc_acc[...] = dc

        _both(i > j, i == j, tile)

        @pl.when(i == nq - 1)
        def _():
            dk_ref[...] = dk_acc[...].astype(dk_ref.dtype)
            dv_ref[...] = dv_acc[...].astype(dv_ref.dtype)
            dc_ref[...] = dc_acc[...]

    q_idx = lambda j, i: (jnp.maximum(i, j), 0)
    kv_idx = lambda j, i: (j, 0)
    return pl.pallas_call(
        body, name=name, grid=(nk, nq),
        in_specs=[pl.BlockSpec((tq, H * AUG), q_idx), pl.BlockSpec((tk, H * AUG), kv_idx), pl.BlockSpec((tk, H * AUG), kv_idx),
                  pl.BlockSpec((tq, W), q_idx), pl.BlockSpec((tq, W), q_idx), pl.BlockSpec((tq, LANES), q_idx)],
        out_specs=[pl.BlockSpec((tk, W), kv_idx), pl.BlockSpec((tk, W), kv_idx), pl.BlockSpec((8, tk), lambda j, i: (0, j))],
        out_shape=[jax.ShapeDtypeStruct((T, W), BF16), jax.ShapeDtypeStruct((T, W), BF16), jax.ShapeDtypeStruct((8, T), F32)],
        scratch_shapes=[pltpu.VMEM((tk, W), F32), pltpu.VMEM((tk, W), F32), pltpu.VMEM((8, tk), F32)],
        compiler_params=_params(("parallel", "arbitrary")))(qa, ka, va, do, o, lse)


def fox_bwd_q(qa, ka, va, do, o, lse, H, *, name, tq=512, tk=512):
    T, W = do.shape
    tq, tk = min(tq, T), min(tk, T)
    nq, nk = T // tq, T // tk

    def body(q_ref, k_ref, v_ref, do_ref, o_ref, lse_ref, dq_ref, dq_acc):
        i, j = pl.program_id(0), pl.program_id(1)

        @pl.when(j == 0)
        def _():
            dq_acc[...] = jnp.zeros_like(dq_acc)

        def tile(diagonal):
            for h in range(H):
                hs = slice(h * HEAD, (h + 1) * HEAD)
                _, ds = _fox_ds(q_ref, k_ref, v_ref, do_ref, o_ref, lse_ref, h, tq, tk, diagonal)
                dq_acc[:, hs] += lax.dot_general(ds.astype(BF16), k_ref[:, h * AUG:h * AUG + HEAD], (((1,), (0,)), ((), ())),
                                                 preferred_element_type=F32)

        _both(j < i, j == i, tile)

        @pl.when(j == nk - 1)
        def _():
            dq_ref[...] = (dq_acc[...] * (HEAD ** -0.5)).astype(dq_ref.dtype)

    q_idx = lambda i, j: (i, 0)
    kv_idx = lambda i, j: (jnp.minimum(j, i), 0)
    return pl.pallas_call(
        body, name=name, grid=(nq, nk),
        in_specs=[pl.BlockSpec((tq, H * AUG), q_idx), pl.BlockSpec((tk, H * AUG), kv_idx), pl.BlockSpec((tk, H * AUG), kv_idx),
                  pl.BlockSpec((tq, W), q_idx), pl.BlockSpec((tq, W), q_idx), pl.BlockSpec((tq, LANES), q_idx)],
        out_specs=pl.BlockSpec((tq, W), q_idx), out_shape=jax.ShapeDtypeStruct((T, W), BF16),
        scratch_shapes=[pltpu.VMEM((tq, W), F32)],
        compiler_params=_params(("parallel", "arbitrary")))(qa, ka, va, do, o, lse)


def _gates_fn(small, a_log_l, dt_bias_l, b_f_l):
    lane = lax.broadcasted_iota(jnp.int32, small.shape, 1)
    beta = _sigmoid(small)
    g = -jnp.exp(a_log_l) * _softplus(small + dt_bias_l)
    lf = _log_sigmoid(small + b_f_l)
    return (jnp.where(lane < LANE_A, beta, jnp.where(lane < LANE_F, g, jnp.where(lane < LANE_F + 8, lf, 0.0))),)


def _gated_norm_fn(o, z, g):
    return (_rms(o, g) * _silu(z),)


def _merge_fn(ya, yb, ym, ga, gb, gm):
    return (_sigmoid(ga) * ya + _sigmoid(gb) * yb + _sigmoid(gm) * ym,)


def _mem_attn_fn(nh, dh, mq, kn, v, gq):
    outs = []
    for h in range(nh):
        hs = slice(h * dh, (h + 1) * dh)
        qn = _rms(mq[:, hs], gq)
        s = NT(qn, kn[:, hs]) * (dh ** -0.5)
        e = jnp.exp(s - jnp.max(s, axis=1, keepdims=True))
        p = e / jnp.sum(e, axis=1, keepdims=True)
        outs.append(NN(p, v[:, hs]))
    return (jnp.concatenate(outs, axis=1),)


def sum_squares(x, *, name, tr=512):
    T, D = x.shape
    tr = min(tr, T)

    def body(x_ref, o_ref):
        @pl.when(pl.program_id(0) == 0)
        def _():
            o_ref[...] = jnp.zeros_like(o_ref)

        v = x_ref[...]
        o_ref[...] += jnp.sum(jnp.sum(v * v, axis=1, keepdims=True), axis=0, keepdims=True)

    return pl.pallas_call(
        body, name=name, grid=(T // tr,), in_specs=[pl.BlockSpec((tr, D), lambda i: (i, 0))],
        out_specs=pl.BlockSpec((1, LANES), lambda i: (0, 0)), out_shape=jax.ShapeDtypeStruct((1, LANES), F32),
        compiler_params=_params(("arbitrary",)))(x)


class Comm:
    def __init__(self, arrays, out_shape, sems, start, forward, finish):
        self.arrays, self.out_shape, self.sems = list(arrays), list(out_shape), list(sems)
        self.start, self.forward, self.finish = start, forward, finish


def _place():
    x, y, c = lax.axis_index("x"), lax.axis_index("y"), lax.axis_index("c")
    chips = [(1 - x, y), (x, 1 - y), (1 - x, 1 - y)]
    return x, y, c, chips


def comm_gather(arrays):
    n = len(arrays)
    lin = lambda px, py, pc: 4 * px + 2 * py + pc

    def copy(ins, outs, sems, a, k, block, to, src=None):
        slot = outs[a].at[lin(*block)]
        return pltpu.make_async_remote_copy(src_ref=slot if src is None else src, dst_ref=slot, send_sem=sems[0].at[a, k],
                                            recv_sem=sems[1].at[a, k], device_id=to, device_id_type=MESH)

    def local(ins, outs, sems, a):
        x, y, c, _ = _place()
        return pltpu.make_async_copy(ins[a], outs[a].at[lin(x, y, c)], sems[2].at[a])

    def start(ins, outs, sems):
        x, y, c, chips = _place()
        for a in range(n):
            local(ins, outs, sems, a).start()
        for a in range(n):
            for j, chip in enumerate(chips):
                copy(ins, outs, sems, a, 1 + j, (x, y, c), (*chip, c), src=ins[a]).start()
            copy(ins, outs, sems, a, 0, (x, y, c), (x, y, 1 - c), src=ins[a]).start()

    def forward(ins, outs, sems):
        x, y, c, chips = _place()
        for a in range(n):
            for j, chip in enumerate(chips):
                copy(ins, outs, sems, a, 1 + j, (*chip, c), (x, y, c)).wait_recv()
                copy(ins, outs, sems, a, 4 + j, (*chip, c), (x, y, 1 - c)).start()

    def finish(ins, outs, sems):
        x, y, c, chips = _place()
        for a in range(n):
            copy(ins, outs, sems, a, 0, (x, y, 1 - c), (x, y, c)).wait_recv()
            for j, chip in enumerate(chips):
                copy(ins, outs, sems, a, 4 + j, (*chip, 1 - c), (x, y, c)).wait_recv()
        for a in range(n):
            for j, chip in enumerate(chips):
                copy(ins, outs, sems, a, 1 + j, (x, y, c), (*chip, c), src=ins[a]).wait_send()
                copy(ins, outs, sems, a, 4 + j, (*chip, c), (x, y, 1 - c)).wait_send()
            copy(ins, outs, sems, a, 0, (x, y, c), (x, y, 1 - c), src=ins[a]).wait_send()
            local(ins, outs, sems, a).wait()

    return Comm(arrays, [jax.ShapeDtypeStruct((N_DEV,) + a.shape, a.dtype) for a in arrays],
                [pltpu.SemaphoreType.DMA((n, 7)), pltpu.SemaphoreType.DMA((n, 7)), pltpu.SemaphoreType.DMA((n,))],
                start, forward, finish)


def comm_direct(arrays, scatter):
    n = len(arrays)

    def peers():
        x, y, c = lax.axis_index("x"), lax.axis_index("y"), lax.axis_index("c")
        out = []
        for r in range(1, N_DEV):
            px, py, pc = (1 - x if r & 4 else x), (1 - y if r & 2 else y), (1 - c if r & 1 else c)
            out.append((r, (px, py, pc), 4 * px + 2 * py + pc))
        return 4 * x + 2 * y + c, out

    def remote(ins, outs, sems, a, r, dev, src_slot, dst_slot):
        return pltpu.make_async_remote_copy(
            src_ref=ins[a].at[src_slot] if scatter[a] else ins[a], dst_ref=outs[a].at[dst_slot],
            send_sem=sems[0].at[a, r - 1], recv_sem=sems[1].at[a, r - 1], device_id=dev, device_id_type=MESH)

    def local(ins, outs, sems, a, me):
        return pltpu.make_async_copy(ins[a].at[me] if scatter[a] else ins[a], outs[a].at[me], sems[2].at[a])

    def start(ins, outs, sems):
        me, ps = peers()
        for a in range(n):
            local(ins, outs, sems, a, me).start()
        for a in range(n):
            for r, dev, lin in ps:
                remote(ins, outs, sems, a, r, dev, lin, me).start()

    def finish(ins, outs, sems):
        me, ps = peers()
        for a in range(n):
            for r, dev, lin in ps:
                remote(ins, outs, sems, a, r, dev, lin, lin).wait_recv()
        for a in range(n):
            for r, dev, lin in ps:
                remote(ins, outs, sems, a, r, dev, lin, me).wait_send()
            local(ins, outs, sems, a, me).wait()

    return Comm(arrays, [jax.ShapeDtypeStruct(a.shape if sc else (N_DEV,) + a.shape, a.dtype) for a, sc in zip(arrays, scatter)],
                [pltpu.SemaphoreType.DMA((n, N_DEV - 1)), pltpu.SemaphoreType.DMA((n, N_DEV - 1)),
                 pltpu.SemaphoreType.DMA((n,))], start, None, finish)


def run_comm(comm, *, name):
    n = len(comm.arrays)

    def body(*refs):
        ins, outs, sems = refs[:n], refs[n:2 * n], refs[2 * n:]
        comm.start(ins, outs, sems)
        if comm.forward is not None:
            comm.forward(ins, outs, sems)
        comm.finish(ins, outs, sems)

    any_spec = pl.BlockSpec(memory_space=pl.ANY)
    return pl.pallas_call(body, name=name, in_specs=[any_spec] * n, out_specs=[any_spec] * n, out_shape=comm.out_shape,
                          scratch_shapes=comm.sems)(*comm.arrays)


def adamw(parts, w, m, v, *, name, tr=128):
    R, Cc = w.shape
    tr = min(tr, R)
    assert R % tr == 0
    n_parts = parts.shape[0]

    def body(p_ref, w_ref, m_ref, v_ref, g_ref, d_ref, nm_ref, nv_ref):
        g = p_ref[0].astype(F32)
        for s in range(1, n_parts):
            g = g + p_ref[s].astype(F32)
        nm = ADAM_B1 * m_ref[...] + (1.0 - ADAM_B1) * g
        nv = ADAM_B2 * v_ref[...] + (1.0 - ADAM_B2) * (g * g)
        m_hat = nm / (1.0 - ADAM_B1 ** ADAM_STEP)
        v_hat = nv / (1.0 - ADAM_B2 ** ADAM_STEP)
        g_ref[...] = g
        d_ref[...] = -ADAM_LR * (m_hat / (jnp.sqrt(v_hat) + ADAM_EPS) + ADAM_WD * w_ref[...])
        nm_ref[...] = nm
        nv_ref[...] = nv

    spec = pl.BlockSpec((tr, Cc), lambda i: (i, 0))
    return pl.pallas_call(
        body, name=name, grid=(R // tr,),
        in_specs=[pl.BlockSpec((n_parts, tr, Cc), lambda i: (0, i, 0)), spec, spec, spec], out_specs=[spec] * 4,
        out_shape=[jax.ShapeDtypeStruct((R, Cc), F32)] * 4, compiler_params=_params(("parallel",)))(parts, w, m, v)


def _lanes(vec, base):
    return jnp.pad(vec[None].astype(F32), ((0, 0), (base, LANES - base - vec.shape[0])))


def _col_shards(full_w):
    R, Ct = full_w.shape
    return jnp.transpose(full_w.reshape(R, N_DEV, Ct // N_DEV), (1, 0, 2))


def _from_col_shards(g):
    return jnp.transpose(g, (1, 0, 2)).reshape(g.shape[1], -1)


def kernel(x, mem, g_mix, w_in, conv_w, a_log, dt_bias, gdn_norm_g, fox_b_f, fox_q_norm, fox_k_norm, g_mem, w_mem_kv, mem_q_norm, mem_k_norm, w_up_gdn, w_up_fox, w_up_mem, w_out, g_mlp, w_ff1, w_ff2, loss_target, m_g_mix, m_w_in, m_conv_w, m_a_log, m_dt_bias, m_gdn_norm_g, m_fox_b_f, m_fox_q_norm, m_fox_k_norm, m_g_mem, m_w_mem_kv, m_mem_q_norm, m_mem_k_norm, m_w_up_gdn, m_w_up_fox, m_w_up_mem, m_w_out, m_g_mlp, m_w_ff1, m_w_ff2, v_g_mix, v_w_in, v_conv_w, v_a_log, v_dt_bias, v_gdn_norm_g, v_fox_b_f, v_fox_q_norm, v_fox_k_norm, v_g_mem, v_w_mem_kv, v_mem_q_norm, v_mem_k_norm, v_w_up_gdn, v_w_up_fox, v_w_up_mem, v_w_out, v_g_mlp, v_w_ff1, v_w_ff2):
    loc = dict(locals())
    big_names = ["w_in", "conv_w", "w_mem_kv", "w_up_gdn", "w_up_fox", "w_up_mem", "w_out", "w_ff1", "w_ff2"]
    col_sharded = {"w_in", "conv_w", "w_up_gdn", "w_up_fox", "w_up_mem", "w_ff1"}
    small_names = ["g_mix", "a_log", "dt_bias", "gdn_norm_g", "fox_b_f", "fox_q_norm", "fox_k_norm", "g_mem",
                   "mem_q_norm", "mem_k_norm", "g_mlp"]

    xs, tgt, mems = x[0], loss_target[0], mem[0]
    T, D = xs.shape
    HG = a_log.shape[1]
    HF = fox_b_f.shape[1]
    DM = mem_q_norm.shape[1]
    GQK, GV = HG * HEAD, HG * HEAD
    GQKV = 2 * GQK + GV
    FW = HF * HEAD
    MW = w_mem_kv.shape[2] // 2
    HM = MW // DM
    assert HG <= 8 and HF <= 8

    shard = {n: loc[n][0].astype(BF16) for n in big_names}
    first, rest = big_names[:2], big_names[2:]
    W = {}

    def take(names, gathered):
        for n, g in zip(names, gathered):
            W[n] = _from_col_shards(g) if n in col_sharded else g.reshape(-1, g.shape[2])

    take(first, run_comm(comm_gather([shard[n] for n in first]), name="gather_in"))
    widths = [GQKV, GV, HG, HG, FW, FW, FW, HF, MW, 3 * D]
    offs = np.concatenate([[0], np.cumsum(widths)]).tolist()
    seg = [W["w_in"][:, offs[i]:offs[i + 1]] for i in range(len(widths))]
    w_big = jnp.concatenate([seg[0], seg[1], seg[4], seg[5], seg[6], seg[8], seg[9]], axis=1)
    pad8 = lambda s: jnp.pad(s, ((0, 0), (0, 8 - s.shape[1])))
    w_small = jnp.concatenate([pad8(seg[2]), pad8(seg[3]), pad8(seg[7]), jnp.zeros((D, LANES - 24), BF16)], axis=1)
    o_z, o_fq, o_fk, o_fv = GQKV, GQKV + GV, GQKV + GV + FW, GQKV + GV + 2 * FW
    o_mq = o_fv + FW
    o_gt = o_mq + MW
    WB = o_gt + 3 * D
    conv_full = W["conv_w"].astype(F32)

    a_log_l, dt_bias_l, b_f_l = _lanes(a_log[0], LANE_A), _lanes(dt_bias[0], LANE_A), _lanes(fox_b_f[0], LANE_F)
    rms_fn = lambda t, g: (_rms(t, g),)

    h, = rowwise(rms_fn, [full(xs)], [g_mix], [(D, D, BF16)], name="rms_mix")
    big, *gathered = matmul(h, w_big, mode="nn", name="proj_big", comm=comm_gather([shard[n] for n in rest]))
    take(rest, gathered)
    small = matmul(h, w_small, mode="nn", name="proj_small", out_dtypes=(F32,))
    bgf, = rowwise(_gates_fn, [full(small)], [a_log_l, dt_bias_l, b_f_l], [(LANES, LANES, F32)], name="gates")

    cn = conv_fwd(big, conv_full, HG, name="conv")
    o_gdn, ss = gdn_fwd(cn, bgf, HG, name="gdn_fwd")
    oa, = rowwise(_gated_norm_fn, [(o_gdn, HEAD, 0, 1), (big, HEAD, o_z // HEAD, 1)], [gdn_norm_g],
                  [(GV, HEAD, BF16)], name="gated_norm", ncb=HG)

    cum = cumsum_tokens(bgf, reverse=False, name="cumsum")
    fqa = fox_prep(big, o_fq, fox_q_norm, cum, 0, HF, name="fox_prep_q")
    fka = fox_prep(big, o_fk, fox_k_norm, cum, 1, HF, name="fox_prep_k")
    fva = fox_prep(big, o_fv, fox_k_norm, cum, 2, HF, name="fox_prep_v")
    ob, lse = fox_fwd(fqa, fka, fva, HF, name="fox_fwd")

    memn, = rowwise(rms_fn, [full(mems)], [g_mem], [(D, D, BF16)], name="rms_mem")
    kv_m = matmul(memn, W["w_mem_kv"], mode="nn", name="mem_kv", out_dtypes=(F32,))
    kmn, = rowwise(rms_fn, [(kv_m, DM, 0, 1)], [mem_k_norm], [(MW, DM, F32)], name="mem_knorm", ncb=HM)
    vm = kv_m[:, MW:]
    mem_fn = functools.partial(_mem_attn_fn, HM, DM)
    om, = rowwise(mem_fn, [(big, MW, o_mq // MW, 0)], [kmn, vm, mem_q_norm], [(MW, MW, BF16)], name="mem_attn")

    ya = matmul(oa, W["w_up_gdn"], mode="nn", name="up_gdn")
    yb = matmul(ob, W["w_up_fox"], mode="nn", name="up_fox")
    ym = matmul(om, W["w_up_mem"], mode="nn", name="up_mem")
    cbm = min(512, D)
    gate_rows = [(big, cbm, (o_gt + b * D) // cbm, 1) for b in range(3)]
    merge_rows = [(ya, cbm, 0, 1), (yb, cbm, 0, 1), (ym, cbm, 0, 1)] + gate_rows
    y, = rowwise(_merge_fn, merge_rows, [], [(D, cbm, BF16)], name="merge", ncb=D // cbm)
    x1 = matmul(y, W["w_out"], mode="nn", name="out_proj", out_dtypes=(F32,), extras=(xs,),
                epi=lambda r, res: (r + res,))

    h2, = rowwise(rms_fn, [full(x1)], [g_mlp], [(D, D, BF16)], name="rms_mlp")
    u_ff, a_ff = matmul(h2, W["w_ff1"], mode="nn", name="ff1", out_dtypes=(BF16, BF16),
                        epi=lambda r: (r, jnp.square(jnp.maximum(r, 0.0))))
    d_out = matmul(a_ff, W["w_ff2"], mode="nn", name="ff2_loss", out_dtypes=(F32,), extras=(x1, tgt),
                   epi=lambda r, res, t: ((r + res - t) * (1.0 / D),))
    loss_local = 0.5 * D * sum_squares(d_out, name="loss_sum")[0, 0]
    loss = lax.psum(loss_local, ("x", "y", "c"))

    G = {}
    d_u = matmul(d_out, W["w_ff2"], mode="nt", name="d_ff2_in", extras=(u_ff,),
                 epi=lambda r, u: (r * 2.0 * jnp.maximum(u.astype(F32), 0.0),))
    d_h2 = matmul(d_u, W["w_ff1"], mode="nt", name="d_ff1_in")
    d_x1, G["g_mlp"] = rowwise_bwd(rms_fn, [full(x1)], [g_mlp], [full(d_h2)], [(0, F32)], [0], name="d_rms_mlp",
                                   adds=[full(d_out)])
    d_y = matmul(d_x1, W["w_out"], mode="nt", name="d_out_proj_in")
    G["w_out"] = matmul(y, d_x1, mode="tn", name="d_w_out")
    d_ya, d_yb, d_ym, d_ga, d_gb, d_gm = rowwise_bwd(
        _merge_fn, merge_rows, [], [(d_y, cbm, 0, 1)], [(k, BF16) for k in range(6)], [], name="d_merge", ncb=D // cbm)
    d_oa = matmul(d_ya, W["w_up_gdn"], mode="nt", name="d_up_gdn_in")
    d_ob = matmul(d_yb, W["w_up_fox"], mode="nt", name="d_up_fox_in")
    d_om = matmul(d_ym, W["w_up_mem"], mode="nt", name="d_up_mem_in")
    G["w_up_gdn"] = matmul(oa, d_ya, mode="tn", name="d_w_up_gdn")
    G["w_up_fox"] = matmul(ob, d_yb, mode="tn", name="d_w_up_fox")
    G["w_up_mem"] = matmul(om, d_ym, mode="tn", name="d_w_up_mem")

    d_mq, d_kmn, d_vm, G["mem_q_norm"] = rowwise_bwd(
        mem_fn, [(big, MW, o_mq // MW, 0)], [kmn, vm, mem_q_norm], [full(d_om)], [(0, BF16)], [0, 1, 2], name="d_mem_attn")
    d_km, G["mem_k_norm"] = rowwise_bwd(rms_fn, [(kv_m, DM, 0, 1)], [mem_k_norm], [(d_kmn, DM, 0, 1)], [(0, F32)], [0],
                                         name="d_mem_knorm", ncb=HM)
    d_kv_m = jnp.concatenate([d_km, d_vm], axis=1)
    G["w_mem_kv"] = matmul(memn, d_kv_m, mode="tn", name="d_w_mem_kv")
    d_memn = matmul(d_kv_m, W["w_mem_kv"], mode="nt", name="d_mem_kv_in")
    _, G["g_mem"] = rowwise_bwd(rms_fn, [full(mems)], [g_mem], [full(d_memn)], [(0, BF16)], [0], name="d_rms_mem")

    d_fkn, d_fv, d_cum_t = fox_bwd_kv(fqa, fka, fva, d_ob, ob, lse, HF, name="fox_bwd_kv")
    d_fqn = fox_bwd_q(fqa, fka, fva, d_ob, ob, lse, HF, name="fox_bwd_q")
    d_fq, G["fox_q_norm"] = rowwise_bwd(rms_fn, [(big, HEAD, o_fq // HEAD, 1)], [fox_q_norm], [(d_fqn, HEAD, 0, 1)],
                                         [(0, BF16)], [0], name="d_fox_qnorm", ncb=HF)
    d_fk, G["fox_k_norm"] = rowwise_bwd(rms_fn, [(big, HEAD, o_fk // HEAD, 1)], [fox_k_norm], [(d_fkn, HEAD, 0, 1)],
                                         [(0, BF16)], [0], name="d_fox_knorm", ncb=HF)
    d_cum = jnp.pad(d_cum_t[:HF].T, ((0, 0), (LANE_F, LANES - LANE_F - HF)))
    d_logf = cumsum_tokens(d_cum, reverse=True, name="cumsum_rev")

    d_o_gdn, d_z, G["gdn_norm_g"] = rowwise_bwd(
        _gated_norm_fn, [(o_gdn, HEAD, 0, 1), (big, HEAD, o_z // HEAD, 1)], [gdn_norm_g], [(d_oa, HEAD, 0, 1)],
        [(0, F32), (1, BF16)], [0], name="d_gated_norm", ncb=HG)
    d_cn, d_bg = gdn_bwd(cn, bgf, ss, d_o_gdn, HG, name="gdn_bwd")
    d_conv_y, G["conv_w"] = conv_bwd_taps(big, conv_full, d_cn, HG, name="d_conv_taps")
    d_qkv = conv_bwd_input(d_conv_y, conv_full, name="d_conv_in")
    d_small, d_al, d_dt, d_bf = rowwise_bwd(_gates_fn, [full(small)], [a_log_l, dt_bias_l, b_f_l], [full(d_bg + d_logf)],
                                            [(0, F32)], [0, 1, 2], name="d_gates")
    G["a_log"], G["dt_bias"], G["fox_b_f"] = (d_al[:, LANE_A:LANE_A + HG], d_dt[:, LANE_A:LANE_A + HG],
                                               d_bf[:, LANE_F:LANE_F + HF])

    def parts(n):
        g = G[n].astype(BF16)
        return _col_shards(g) if n in col_sharded else g.reshape(N_DEV, -1, g.shape[1])

    recv = {}

    def carried(names, out):
        for n, r in zip(names, out):
            recv[n] = r

    d_big = jnp.concatenate([d_qkv, d_z, d_fq, d_fk, d_fv, d_mq, d_ga, d_gb, d_gm], axis=1)
    group = ["conv_w", "w_mem_kv", "w_up_gdn", "w_up_fox", "w_up_mem", "w_out"]
    G["w_ff2"], *out = matmul(a_ff, d_out, mode="tn", name="d_w_ff2",
                              comm=comm_direct([parts(n) for n in group], [True] * len(group)))
    carried(group, out)
    G["w_ff1"], *out = matmul(h2, d_u, mode="tn", name="d_w_ff1", comm=comm_direct([parts("w_ff2")], [True]))
    carried(["w_ff2"], out)
    g_big, *out = matmul(h, d_big, mode="tn", name="d_w_big", comm=comm_direct([parts("w_ff1")], [True]))
    carried(["w_ff1"], out)
    g_small = matmul(h, d_small, mode="tn", name="d_w_small", out_dtypes=(F32,))
    cols = lambda a, o, wd: a[:, o:o + wd]
    G["w_in"] = jnp.concatenate([
        cols(g_big, 0, GQKV), cols(g_big, o_z, GV), cols(g_small, LANE_B, HG), cols(g_small, LANE_A, HG),
        cols(g_big, o_fq, FW), cols(g_big, o_fk, FW), cols(g_big, o_fv, FW), cols(g_small, LANE_F, HF),
        cols(g_big, o_mq, MW), cols(g_big, o_gt, 3 * D)], axis=1)
    d_h_s = matmul(d_small, w_small, mode="nt", name="d_proj_small_in", out_dtypes=(F32,))
    d_h, *out = matmul(d_big, w_big, mode="nt", name="d_proj_big_in", extras=(d_h_s,), epi=lambda r, e: (r + e,),
                       comm=comm_direct([parts("w_in")], [True]))
    carried(["w_in"], out)
    grad_x, G["g_mix"] = rowwise_bwd(rms_fn, [full(xs)], [g_mix], [full(d_h)], [(0, F32)], [0], name="d_rms_mix",
                                     adds=[full(d_x1)])
    grad_x = grad_x[None]

    small_sizes = [loc[n].shape[1] for n in small_names]
    pack = lambda d: jnp.concatenate([d[n].reshape(1, -1) for n in small_names], axis=1)
    npad = -sum(small_sizes) % LANES
    padp = lambda a: jnp.pad(a, ((0, 0), (0, npad)))
    recv_small, = run_comm(comm_direct([padp(pack(G))], [False]), name="gather_small_grads")

    res = {}
    for n in big_names:
        res[n] = [t[None] for t in adamw(recv[n], loc[n][0], loc["m_" + n][0], loc["v_" + n][0], name="adamw_" + n)]
    sm = adamw(recv_small, padp(pack({n: loc[n] for n in small_names})), padp(pack({n: loc["m_" + n] for n in small_names})),
               padp(pack({n: loc["v_" + n] for n in small_names})), name="adamw_small")
    so = np.concatenate([[0], np.cumsum(small_sizes)]).tolist()
    for i, n in enumerate(small_names):
        res[n] = [t[:, so[i]:so[i + 1]] for t in sm]

    order = ["g_mix", "w_in", "conv_w", "a_log", "dt_bias", "gdn_norm_g", "fox_b_f", "fox_q_norm", "fox_k_norm", "g_mem",
             "w_mem_kv", "mem_q_norm", "mem_k_norm", "w_up_gdn", "w_up_fox", "w_up_mem", "w_out", "g_mlp", "w_ff1", "w_ff2"]
    return (loss, grad_x, *[res[n][0] for n in order], *[res[n][1] for n in order],
            *[res[n][2] for n in order], *[res[n][3] for n in order])
```

```python
import functools

import jax
import jax.numpy as jnp
import numpy as np
from jax import lax
from jax.experimental import pallas as pl
from jax.experimental.pallas import tpu as pltpu

F32 = jnp.float32
BF16 = jnp.bfloat16
HI = lax.Precision.HIGHEST

EPS = 1e-6
GDN_CHUNK = 64
GDN_CONV = 4
HEAD = 128
LANES = 128
HALO = 16
N_DEV = 8
MESH = pl.DeviceIdType.MESH
VMEM_LIMIT_V7X = 56 * 1024 * 1024

ADAM_LR, ADAM_B1, ADAM_B2, ADAM_EPS, ADAM_WD, ADAM_STEP = 0.001, 0.9, 0.999, 1e-08, 0.01, 10

LANE_B, LANE_A, LANE_F = 0, 8, 16


def _params(sem):
    return pltpu.CompilerParams(dimension_semantics=sem, vmem_limit_bytes=VMEM_LIMIT_V7X)


def _dg(a, b, ca, cb, prec):
    nb = a.ndim - 2
    batch = tuple(range(nb))
    return lax.dot_general(a, b, (((ca + nb,), (cb + nb,)), (batch, batch)), precision=prec,
                           preferred_element_type=F32)


def _make_mm(prec, cast):
    def c(x):
        return x.astype(BF16) if cast else x

    @jax.custom_vjp
    def nn(a, b):
        return _dg(c(a), c(b), 1, 0, prec)

    @jax.custom_vjp
    def nt(a, b):
        return _dg(c(a), c(b), 1, 1, prec)

    @jax.custom_vjp
    def tn(a, b):
        return _dg(c(a), c(b), 0, 0, prec)

    nn.defvjp(lambda a, b: (nn(a, b), (a, b)), lambda r, g: (nt(g, r[1]), tn(r[0], g)))
    nt.defvjp(lambda a, b: (nt(a, b), (a, b)), lambda r, g: (nn(g, r[1]), tn(g, r[0])))
    tn.defvjp(lambda a, b: (tn(a, b), (a, b)), lambda r, g: (nt(r[1], g), nn(r[0], g)))
    return nn, nt, tn


NN, NT, TN = _make_mm(None, True)
NNH, NTH, TNH = _make_mm(lax.Precision.HIGH, False)
NNX, _, _ = _make_mm(HI, False)


def _sigmoid(x):
    return 1.0 / (1.0 + jnp.exp(-x))


def _silu(x):
    return x * _sigmoid(x)


def _softplus(x):
    return jnp.maximum(x, 0.0) + jnp.log(1.0 + jnp.exp(-jnp.abs(x)))


def _log_sigmoid(x):
    return -_softplus(-x)


def _rms(x, g):
    return x * lax.rsqrt(jnp.mean(x * x, axis=-1, keepdims=True) + EPS) * g


def _tile(n, target):
    t = target
    while t >= LANES:
        if n % t == 0:
            return t
        t //= 2
    return n


def matmul(a, b, *, mode, name, out_dtypes=(BF16,), epi=None, extras=(), tm=1024, tn=1024, tk=2048, comm=None):
    if mode == "nn":
        (M, K), (K2, N) = a.shape, b.shape
    elif mode == "nt":
        (M, K), (N, K2) = a.shape, b.shape
    else:
        (K, M), (K2, N) = a.shape, b.shape
    assert K == K2, (name, a.shape, b.shape)
    tm, tn, tk = _tile(M, tm), _tile(N, tn), _tile(K, tk)
    ni, nj, nk = M // tm, N // tn, K // tk
    a_spec = (pl.BlockSpec((tk, tm), lambda i, j, k: (k, i)) if mode == "tn"
              else pl.BlockSpec((tm, tk), lambda i, j, k: (i, k)))
    b_spec = (pl.BlockSpec((tn, tk), lambda i, j, k: (j, k)) if mode == "nt"
              else pl.BlockSpec((tk, tn), lambda i, j, k: (k, j)))
    o_spec = pl.BlockSpec((tm, tn), lambda i, j, k: (i, j))
    dims = {"nn": ((1,), (0,)), "nt": ((1,), (1,)), "tn": ((0,), (0,))}[mode]
    ne, no = len(extras), len(out_dtypes)
    nc = len(comm.arrays) if comm is not None else 0
    n_steps = ni * nj * nk

    def body(a_ref, b_ref, *rest):
        ex, c_in = rest[:ne], rest[ne:ne + nc]
        outs, c_out = rest[ne + nc:ne + nc + no], rest[ne + nc + no:ne + nc + no + nc]
        acc = rest[ne + nc + no + nc]
        c_sems = rest[ne + nc + no + nc + 1:]
        k = pl.program_id(2)
        step = (pl.program_id(0) * nj + pl.program_id(1)) * nk + k

        if comm is not None:
            pl.when(step == 0)(lambda: comm.start(c_in, c_out, c_sems))
            if comm.forward is not None and n_steps >= 3:
                pl.when(step == n_steps // 2)(lambda: comm.forward(c_in, c_out, c_sems))

        @pl.when(k == 0)
        def _():
            acc[...] = jnp.zeros_like(acc)

        acc[...] += lax.dot_general(a_ref[...].astype(BF16), b_ref[...].astype(BF16), (dims, ((), ())),
                                    preferred_element_type=F32)

        @pl.when(k == nk - 1)
        def _():
            r = acc[...]
            vals = epi(r, *[e[...] for e in ex]) if epi is not None else (r,)
            for o, v in zip(outs, vals):
                o[...] = v.astype(o.dtype)

        if comm is not None:
            @pl.when(step == n_steps - 1)
            def _():
                if comm.forward is not None and n_steps < 3:
                    comm.forward(c_in, c_out, c_sems)
                comm.finish(c_in, c_out, c_sems)

    any_spec = pl.BlockSpec(memory_space=pl.ANY)
    sem = ("arbitrary",) * 3 if comm is not None else ("parallel", "parallel", "arbitrary")
    out = pl.pallas_call(
        body, name=name, grid=(ni, nj, nk),
        in_specs=[a_spec, b_spec] + [o_spec] * ne + [any_spec] * nc, out_specs=[o_spec] * no + [any_spec] * nc,
        out_shape=[jax.ShapeDtypeStruct((M, N), d) for d in out_dtypes] + (comm.out_shape if comm is not None else []),
        scratch_shapes=[pltpu.VMEM((tm, tn), F32)] + (comm.sems if comm is not None else []),
        compiler_params=_params(sem))(a, b, *extras, *(comm.arrays if comm is not None else []))
    return out[0] if len(out) == 1 else out


def _row_spec(tr, cb, off, moves):
    return pl.BlockSpec((tr, cb), lambda i, j: (i, off + moves * j))


def _whole_spec(p):
    return pl.BlockSpec(p.shape, lambda i, j: (0,) * p.ndim)


def _row_tile(T, rows):
    widest = max(cb for (_, cb, _, _) in rows)
    return min(T, max(512, (1 << 19) // widest))


def rowwise(fn, rows, params, outs, *, name, ncb=1, tr=None):
    T = rows[0][0].shape[0]
    tr = min(tr, T) if tr else _row_tile(T, rows)
    assert T % tr == 0
    nr, npar = len(rows), len(params)

    def body(*refs):
        r, p, o = refs[:nr], refs[nr:nr + npar], refs[nr + npar:]
        vals = fn(*[x[...].astype(F32) for x in r], *[x[...] for x in p])
        for oref, v in zip(o, vals):
            oref[...] = v.astype(oref.dtype)

    res = pl.pallas_call(
        body, name=name, grid=(T // tr, ncb),
        in_specs=[_row_spec(tr, cb, off, mv) for (_, cb, off, mv) in rows] + [_whole_spec(p) for p in params],
        out_specs=[_row_spec(tr, cb, 0, 1) for (_, cb, _) in outs],
        out_shape=[jax.ShapeDtypeStruct((T, cols), d) for (cols, _, d) in outs],
        compiler_params=_params(("parallel", "parallel")))(*[r[0] for r in rows], *params)
    return res


def rowwise_bwd(fn, rows, params, cots, drows, dparams, *, name, ncb=1, tr=None, adds=()):
    T = rows[0][0].shape[0]
    tr = min(tr, T) if tr else _row_tile(T, rows)
    assert T % tr == 0
    nr, npar, nc, ndr, na = len(rows), len(params), len(cots), len(drows), len(adds)

    def body(*refs):
        r, p, c = refs[:nr], refs[nr:nr + npar], refs[nr + npar:nr + npar + nc]
        base = nr + npar + nc + na
        ad, o_r, o_p = refs[base - na:base], refs[base:base + ndr], refs[base + ndr:]
        prim = [x[...].astype(F32) for x in r] + [x[...] for x in p]
        _, vjp = jax.vjp(lambda *a: tuple(fn(*a)), *prim)
        g = vjp(tuple(x[...].astype(F32) for x in c))
        for k, (oref, (idx, _)) in enumerate(zip(o_r, drows)):
            val = g[idx] + ad[k][...].astype(F32) if k < na else g[idx]
            oref[...] = val.astype(oref.dtype)
        first = jnp.logical_and(pl.program_id(0) == 0, pl.program_id(1) == 0)

        @pl.when(first)
        def _():
            for oref in o_p:
                oref[...] = jnp.zeros_like(oref)

        for oref, idx in zip(o_p, dparams):
            oref[...] += g[nr + idx]

    res = pl.pallas_call(
        body, name=name, grid=(T // tr, ncb),
        in_specs=([_row_spec(tr, cb, off, mv) for (_, cb, off, mv) in rows] + [_whole_spec(p) for p in params]
                  + [_row_spec(tr, cb, off, mv) for (_, cb, off, mv) in tuple(cots) + tuple(adds)]),
        out_specs=([_row_spec(tr, rows[idx][1], 0, 1) for (idx, _) in drows]
                   + [_whole_spec(params[idx]) for idx in dparams]),
        out_shape=([jax.ShapeDtypeStruct((T, ncb * rows[idx][1] if rows[idx][3] else rows[idx][1]), d)
                    for (idx, d) in drows]
                   + [jax.ShapeDtypeStruct(params[idx].shape, F32) for idx in dparams]),
        compiler_params=_params(("arbitrary", "arbitrary")))(
            *[r[0] for r in rows], *params, *[c[0] for c in cots], *[a[0] for a in adds])
    return res


def full(a):
    return (a, a.shape[1], 0, 0)


def cumsum_tokens(x, *, reverse, name, tb=256):
    T = x.shape[0]
    tb = min(tb, T)
    nb = T // tb
    idx = (lambda i: (nb - 1 - i, 0)) if reverse else (lambda i: (i, 0))

    def body(x_ref, o_ref, carry):
        @pl.when(pl.program_id(0) == 0)
        def _():
            carry[...] = jnp.zeros_like(carry)

        ii = lax.broadcasted_iota(jnp.int32, (tb, tb), 0)
        jj = lax.broadcasted_iota(jnp.int32, (tb, tb), 1)
        tri = ((ii <= jj) if reverse else (ii >= jj)).astype(F32)
        c = lax.dot_general(tri, x_ref[...], (((1,), (0,)), ((), ())), precision=HI,
                            preferred_element_type=F32) + carry[0:1, :]
        o_ref[...] = c
        carry[0:1, :] = c[0:1, :] if reverse else c[tb - 1:tb, :]

    return pl.pallas_call(
        body, name=name, grid=(nb,), in_specs=[pl.BlockSpec((tb, LANES), idx)],
        out_specs=pl.BlockSpec((tb, LANES), idx), out_shape=jax.ShapeDtypeStruct((T, LANES), F32),
        scratch_shapes=[pltpu.VMEM((8, LANES), F32)], compiler_params=_params(("arbitrary",)))(x)


def _conv_post(y, kind, dk):
    c = _silu(y)
    r = lax.rsqrt(jnp.sum(c * c, axis=-1, keepdims=True) + EPS)
    return jnp.where(kind == 0, c * r * (dk ** -0.5), jnp.where(kind == 1, c * r, c))


def _conv_taps(cur, prev, w, tr):
    ext = jnp.concatenate([prev, cur], axis=0)
    y = w[3:4, :] * cur
    for d in (1, 2, 3):
        y = y + w[3 - d:4 - d, :] * pltpu.roll(ext, d, 0)[HALO:HALO + tr]
    return y


def conv_fwd(big, w, n_qk_heads, *, name, tr=2048):
    T, W = big.shape[0], w.shape[1]
    tr = min(tr, T)
    nh = W // HEAD

    def body(cur_ref, prev_ref, w_ref, o_ref):
        i, j = pl.program_id(0), pl.program_id(1)
        prev = jnp.where(i > 0, prev_ref[...].astype(F32), 0.0)
        y = _conv_taps(cur_ref[...].astype(F32), prev, w_ref[...], tr)
        kind = jnp.where(j < n_qk_heads, 0, jnp.where(j < 2 * n_qk_heads, 1, 2))
        o_ref[...] = _conv_post(y, kind, HEAD)

    return pl.pallas_call(
        body, name=name, grid=(T // tr, nh),
        in_specs=[pl.BlockSpec((tr, HEAD), lambda i, j: (i, j)),
                  pl.BlockSpec((HALO, HEAD), lambda i, j: (jnp.maximum(i * (tr // HALO) - 1, 0), j)),
                  pl.BlockSpec((GDN_CONV, HEAD), lambda i, j: (0, j))],
        out_specs=pl.BlockSpec((tr, HEAD), lambda i, j: (i, j)),
        out_shape=jax.ShapeDtypeStruct((T, W), F32), compiler_params=_params(("parallel", "parallel")))(big, big, w)


def conv_bwd_taps(big, w, dcn, n_qk_heads, *, name, tr=2048):
    T, W = big.shape[0], w.shape[1]
    tr = min(tr, T)
    nh = W // HEAD

    def body(cur_ref, prev_ref, w_ref, g_ref, dy_ref, dw_ref):
        j, i = pl.program_id(0), pl.program_id(1)
        cur = cur_ref[...].astype(F32)
        prev = jnp.where(i > 0, prev_ref[...].astype(F32), 0.0)
        y = _conv_taps(cur, prev, w_ref[...], tr)
        kind = jnp.where(j < n_qk_heads, 0, jnp.where(j < 2 * n_qk_heads, 1, 2))
        _, vjp = jax.vjp(lambda t: _conv_post(t, kind, HEAD), y)
        dy, = vjp(g_ref[...])
        dy_ref[...] = dy
        ext = jnp.concatenate([prev, cur], axis=0)
        rows = [jnp.sum(dy * (cur if d == 0 else pltpu.roll(ext, d, 0)[HALO:HALO + tr]), axis=0, keepdims=True)
                for d in (3, 2, 1, 0)]

        @pl.when(i == 0)
        def _():
            dw_ref[...] = jnp.zeros_like(dw_ref)

        dw_ref[...] += jnp.concatenate(rows, axis=0)

    return pl.pallas_call(
        body, name=name, grid=(nh, T // tr),
        in_specs=[pl.BlockSpec((tr, HEAD), lambda j, i: (i, j)),
                  pl.BlockSpec((HALO, HEAD), lambda j, i: (jnp.maximum(i * (tr // HALO) - 1, 0), j)),
                  pl.BlockSpec((GDN_CONV, HEAD), lambda j, i: (0, j)),
                  pl.BlockSpec((tr, HEAD), lambda j, i: (i, j))],
        out_specs=[pl.BlockSpec((tr, HEAD), lambda j, i: (i, j)), pl.BlockSpec((GDN_CONV, HEAD), lambda j, i: (0, j))],
        out_shape=[jax.ShapeDtypeStruct((T, W), F32), jax.ShapeDtypeStruct((GDN_CONV, W), F32)],
        compiler_params=_params(("parallel", "arbitrary")))(big, big, w, dcn)


def conv_bwd_input(dy, w, *, name, tr=2048):
    T, W = dy.shape
    tr = min(tr, T)
    nrow = T // tr

    def body(cur_ref, nxt_ref, w_ref, o_ref):
        i = pl.program_id(0)
        cur = cur_ref[...]
        nxt = jnp.where(i < nrow - 1, nxt_ref[...], 0.0)
        ext = jnp.concatenate([cur, nxt], axis=0)
        w = w_ref[...]
        dx = w[3:4, :] * cur
        for d in (1, 2, 3):
            dx = dx + w[3 - d:4 - d, :] * pltpu.roll(ext, tr + HALO - d, 0)[0:tr]
        o_ref[...] = dx.astype(o_ref.dtype)

    return pl.pallas_call(
        body, name=name, grid=(nrow, W // HEAD),
        in_specs=[pl.BlockSpec((tr, HEAD), lambda i, j: (i, j)),
                  pl.BlockSpec((HALO, HEAD), lambda i, j: (jnp.minimum((i + 1) * (tr // HALO), T // HALO - 1), j)),
                  pl.BlockSpec((GDN_CONV, HEAD), lambda i, j: (0, j))],
        out_specs=pl.BlockSpec((tr, HEAD), lambda i, j: (i, j)),
        out_shape=jax.ShapeDtypeStruct((T, W), BF16), compiler_params=_params(("parallel", "parallel")))(dy, dy, w)


@jax.custom_vjp
def _unit_lower_inverse(p):
    C = p.shape[-1]
    eye = lax.broadcasted_iota(jnp.int32, (C, C), 0) == lax.broadcasted_iota(jnp.int32, (C, C), 1)
    t = jnp.where(eye[None], 1.0, 0.0) + p
    n = 2
    while n < C:
        p = NNH(p, p)
        t = t + NNH(t, p)
        n *= 2
    return t


def _unit_lower_inverse_fwd(p):
    t = _unit_lower_inverse(p)
    return t, t


def _unit_lower_inverse_bwd(t, g):
    return (NTH(TNH(t, g), t),)


_unit_lower_inverse.defvjp(_unit_lower_inverse_fwd, _unit_lower_inverse_bwd)


def _gdn_chunk(q, k, v, bg, S):
    H, C = q.shape[0], q.shape[1]
    ii = lax.broadcasted_iota(jnp.int32, (C, C), 0)
    jj = lax.broadcasted_iota(jnp.int32, (C, C), 1)
    lincl = (ii >= jj).astype(F32)
    strict, incl, eye = (ii > jj)[None], (ii >= jj)[None], (ii == jj)[None]
    gam2d = lax.dot_general(lincl, bg, (((1,), (0,)), ((), ())), precision=HI, preferred_element_type=F32)
    lane = lax.broadcasted_iota(jnp.int32, (H, 1, LANES), 2)
    hh = lax.broadcasted_iota(jnp.int32, (H, 1, LANES), 0)
    beta = jnp.sum(bg[None] * (lane == hh + LANE_B).astype(F32), axis=2, keepdims=True)
    gam = jnp.sum(gam2d[None] * (lane == hh + LANE_A).astype(F32), axis=2, keepdims=True)
    last = (lax.broadcasted_iota(jnp.int32, (1, C, 1), 1) == C - 1).astype(F32)
    gam_last = jnp.sum(gam * last, axis=1, keepdims=True)
    gam_row = NNX(jnp.ones((H, C, C), F32), jnp.where(eye, gam, 0.0))
    diff = gam - gam_row
    dec_s = jnp.where(strict, jnp.exp(jnp.where(strict, diff, 0.0)), 0.0)
    dec_i = jnp.where(incl, jnp.exp(jnp.where(incl, diff, 0.0)), 0.0)
    t = _unit_lower_inverse(-(beta * NT(k, k) * dec_s))
    eg = jnp.exp(gam)
    w = NNH(t, beta * eg * k)
    u0 = NNH(t, beta * v)
    qk = NT(q, k) * dec_i
    u = u0 - NN(w, S)
    o = NN(q * eg, S) + NN(qk, u)
    S2 = jnp.exp(gam_last) * S + TN(k * jnp.exp(gam_last - gam), u)
    return o, S2


def _heads(x, base, H):
    return jnp.stack([x[:, base + h * HEAD:base + (h + 1) * HEAD] for h in range(H)])


def _unheads(x):
    return jnp.concatenate([x[h] for h in range(x.shape[0])], axis=1)


def gdn_fwd(cn, bgf, H, *, name):
    T, C, W = cn.shape[0], GDN_CHUNK, H * HEAD
    N = T // C

    def body(cn_ref, bg_ref, o_ref, ss_ref, s_scr):
        @pl.when(pl.program_id(0) == 0)
        def _():
            s_scr[...] = jnp.zeros_like(s_scr)

        x, S = cn_ref[...], s_scr[...]
        ss_ref[0] = S
        o, S2 = _gdn_chunk(_heads(x, 0, H), _heads(x, W, H), _heads(x, 2 * W, H), bg_ref[...], S)
        o_ref[...] = _unheads(o)
        s_scr[...] = S2

    return pl.pallas_call(
        body, name=name, grid=(N,),
        in_specs=[pl.BlockSpec((C, 3 * W), lambda n: (n, 0)), pl.BlockSpec((C, LANES), lambda n: (n, 0))],
        out_specs=[pl.BlockSpec((C, W), lambda n: (n, 0)), pl.BlockSpec((1, H, HEAD, HEAD), lambda n: (n, 0, 0, 0))],
        out_shape=[jax.ShapeDtypeStruct((T, W), F32), jax.ShapeDtypeStruct((N, H, HEAD, HEAD), F32)],
        scratch_shapes=[pltpu.VMEM((H, HEAD, HEAD), F32)], compiler_params=_params(("arbitrary",)))(cn, bgf)


def gdn_bwd(cn, bgf, ss, do, H, *, name):
    T, C, W = cn.shape[0], GDN_CHUNK, H * HEAD
    N = T // C

    def body(cn_ref, bg_ref, ss_ref, do_ref, dcn_ref, dbg_ref, ds_scr):
        @pl.when(pl.program_id(0) == 0)
        def _():
            ds_scr[...] = jnp.zeros_like(ds_scr)

        x = cn_ref[...]
        _, vjp = jax.vjp(_gdn_chunk, _heads(x, 0, H), _heads(x, W, H), _heads(x, 2 * W, H), bg_ref[...], ss_ref[0])
        dq, dk, dv, dbg, dS = vjp((_heads(do_ref[...], 0, H), ds_scr[...]))
        dcn_ref[...] = jnp.concatenate([_unheads(dq), _unheads(dk), _unheads(dv)], axis=1)
        dbg_ref[...] = dbg
        ds_scr[...] = dS

    rev = lambda n: (N - 1 - n, 0)
    return pl.pallas_call(
        body, name=name, grid=(N,),
        in_specs=[pl.BlockSpec((C, 3 * W), rev), pl.BlockSpec((C, LANES), rev),
                  pl.BlockSpec((1, H, HEAD, HEAD), lambda n: (N - 1 - n, 0, 0, 0)), pl.BlockSpec((C, W), rev)],
        out_specs=[pl.BlockSpec((C, 3 * W), rev), pl.BlockSpec((C, LANES), rev)],
        out_shape=[jax.ShapeDtypeStruct((T, 3 * W), F32), jax.ShapeDtypeStruct((T, LANES), F32)],
        scratch_shapes=[pltpu.VMEM((H, HEAD, HEAD), F32)], compiler_params=_params(("arbitrary",)))(cn, bgf, ss, do)


AUG = 2 * HEAD


def fox_prep(big, col_off, gain, cum, kind, H, *, name, tr=2048):
    T = big.shape[0]
    tr = min(tr, T)

    def body(x_ref, g_ref, c_ref, o_ref):
        h = pl.program_id(1)
        x = x_ref[...].astype(F32)
        lane = lax.broadcasted_iota(jnp.int32, (tr, HEAD), 1)
        if kind == 2:
            main, aug = x, jnp.ones((tr, HEAD), F32)
        else:
            main = _rms(x, g_ref[...]) * ((HEAD ** -0.5) if kind == 0 else 1.0)
            c = jnp.sum(jnp.where(lane == LANE_F + h, c_ref[...], 0.0), axis=1, keepdims=True)
            hi = c.astype(BF16).astype(F32)
            mid = (c - hi).astype(BF16).astype(F32)
            lo = c - hi - mid
            if kind == 0:
                aug = jnp.where(lane == 0, hi, jnp.where(lane == 1, mid, jnp.where(lane == 2, lo,
                                                                                   jnp.where(lane < 6, 1.0, 0.0))))
            else:
                aug = jnp.where(lane < 3, 1.0, jnp.where(lane == 3, -hi, jnp.where(lane == 4, -mid,
                                                                                   jnp.where(lane == 5, -lo, 0.0))))
        o_ref[...] = jnp.concatenate([main, aug], axis=1).astype(o_ref.dtype)

    return pl.pallas_call(
        body, name=name, grid=(T // tr, H),
        in_specs=[pl.BlockSpec((tr, HEAD), lambda i, h: (i, col_off // HEAD + h)),
                  pl.BlockSpec((1, HEAD), lambda i, h: (0, 0)), pl.BlockSpec((tr, LANES), lambda i, h: (i, 0))],
        out_specs=pl.BlockSpec((tr, AUG), lambda i, h: (i, h)),
        out_shape=jax.ShapeDtypeStruct((T, H * AUG), BF16), compiler_params=_params(("parallel", "parallel")))(big, gain, cum)


def _fox_logits(q_ref, k_ref, h, tq, tk, diagonal):
    ha = slice(h * AUG, (h + 1) * AUG)
    s = lax.dot_general(q_ref[:, ha], k_ref[:, ha], (((1,), (1,)), ((), ())), preferred_element_type=F32)
    keep = None
    if diagonal:
        keep = lax.broadcasted_iota(jnp.int32, (tq, tk), 0) >= lax.broadcasted_iota(jnp.int32, (tq, tk), 1)
    return s, keep


def _both(cond_off, cond_diag, fn):
    pl.when(cond_off)(functools.partial(fn, False))
    pl.when(cond_diag)(functools.partial(fn, True))


def fox_fwd(qa, ka, va, H, *, name, tq=512, tk=512):
    T = qa.shape[0]
    tq, tk = min(tq, T), min(tk, T)
    nq, nk = T // tq, T // tk
    assert tq == tk

    def body(q_ref, k_ref, v_ref, o_ref, lse_ref, acc, m_scr):
        i, j = pl.program_id(0), pl.program_id(1)

        @pl.when(j == 0)
        def _():
            m_scr[...] = jnp.full_like(m_scr, -jnp.inf)
            acc[...] = jnp.zeros_like(acc)

        def tile(diagonal):
            lane = lax.broadcasted_iota(jnp.int32, (tq, LANES), 1)
            m_all = m_scr[...]
            new_m = m_all
            for h in range(H):
                ha = slice(h * AUG, (h + 1) * AUG)
                s, keep = _fox_logits(q_ref, k_ref, h, tq, tk, diagonal)
                if diagonal:
                    s = jnp.where(keep, s, -jnp.inf)
                m_prev = m_all[:, h:h + 1]
                m_new = jnp.maximum(m_prev, jnp.max(s, axis=1, keepdims=True))
                p = jnp.exp(s - m_new)
                p_hi = p.astype(BF16)
                p_lo = (p - p_hi.astype(F32)).astype(BF16)
                pv = lambda t: lax.dot_general(t, v_ref[:, ha], (((1,), (0,)), ((), ())), preferred_element_type=F32)
                acc[:, ha] = jnp.exp(m_prev - m_new) * acc[:, ha] + (pv(p_hi) + pv(p_lo))
                new_m = jnp.where(lane == h, m_new, new_m)
            m_scr[...] = new_m

        _both(j < i, j == i, tile)

        @pl.when(j == nk - 1)
        def _():
            lane = lax.broadcasted_iota(jnp.int32, (tq, LANES), 1)
            m_all = m_scr[...]
            lse = jnp.zeros((tq, LANES), F32)
            for h in range(H):
                den = acc[:, h * AUG + HEAD:(h + 1) * AUG]
                o_ref[:, h * HEAD:(h + 1) * HEAD] = acc[:, h * AUG:h * AUG + HEAD] / den
                lse = jnp.where(lane == h, m_all[:, h:h + 1] + jnp.log(den[:, 0:1]), lse)
            lse_ref[...] = lse

    kv_idx = lambda i, j: (jnp.minimum(j, i), 0)
    q_idx = lambda i, j: (i, 0)
    return pl.pallas_call(
        body, name=name, grid=(nq, nk),
        in_specs=[pl.BlockSpec((tq, H * AUG), q_idx), pl.BlockSpec((tk, H * AUG), kv_idx), pl.BlockSpec((tk, H * AUG), kv_idx)],
        out_specs=[pl.BlockSpec((tq, H * HEAD), q_idx), pl.BlockSpec((tq, LANES), q_idx)],
        out_shape=[jax.ShapeDtypeStruct((T, H * HEAD), F32), jax.ShapeDtypeStruct((T, LANES), F32)],
        scratch_shapes=[pltpu.VMEM((tq, H * AUG), F32), pltpu.VMEM((tq, LANES), F32)],
        compiler_params=_params(("parallel", "arbitrary")))(qa, ka, va)


def _fox_ds(q_ref, k_ref, v_ref, do_ref, o_ref, lse_ref, h, tq, tk, diagonal):
    hs = slice(h * HEAD, (h + 1) * HEAD)
    s, keep = _fox_logits(q_ref, k_ref, h, tq, tk, diagonal)
    p = jnp.exp(s - lse_ref[:, h:h + 1])
    if diagonal:
        p = jnp.where(keep, p, 0.0)
    do = do_ref[:, hs]
    dp = lax.dot_general(do, v_ref[:, h * AUG:h * AUG + HEAD], (((1,), (1,)), ((), ())), preferred_element_type=F32)
    delta = jnp.sum(do.astype(F32) * o_ref[:, hs], axis=1, keepdims=True)
    return p, p * (dp - delta)


def fox_bwd_kv(qa, ka, va, do, o, lse, H, *, name, tq=512, tk=512):
    T, W = do.shape
    tq, tk = min(tq, T), min(tk, T)
    nq, nk = T // tq, T // tk

    def body(q_ref, k_ref, v_ref, do_ref, o_ref, lse_ref, dk_ref, dv_ref, dc_ref, dk_acc, dv_acc, dc_acc):
        j, i = pl.program_id(0), pl.program_id(1)

        @pl.when(i == 0)
        def _():
            dk_acc[...] = jnp.zeros_like(dk_acc)
            dv_acc[...] = jnp.zeros_like(dv_acc)
            dc_acc[...] = jnp.zeros_like(dc_acc)

        def tile(diagonal):
            row = lax.broadcasted_iota(jnp.int32, (8, tk), 0)
            dc = dc_acc[...]
            for h in range(H):
                hs = slice(h * HEAD, (h + 1) * HEAD)
                p, ds = _fox_ds(q_ref, k_ref, v_ref, do_ref, o_ref, lse_ref, h, tq, tk, diagonal)
                dv_acc[:, hs] += lax.dot_general(p.astype(BF16), do_ref[:, hs], (((0,), (0,)), ((), ())),
                                                 preferred_element_type=F32)
                dk_acc[:, hs] += lax.dot_general(ds.astype(BF16), q_ref[:, h * AUG:h * AUG + HEAD],
                                                 (((0,), (0,)), ((), ())), preferred_element_type=F32)
                dc = jnp.where(row == h, dc - jnp.sum(ds, axis=0, keepdims=True), dc)
            dc_acc[...] = dc

        _both(i > j, i == j, tile)

        @pl.when(i == nq - 1)
        def _():
            dk_ref[...] = dk_acc[...].astype(dk_ref.dtype)
            dv_ref[...] = dv_acc[...].astype(dv_ref.dtype)
            dc_ref[...] = dc_acc[...]

    q_idx = lambda j, i: (jnp.maximum(i, j), 0)
    kv_idx = lambda j, i: (j, 0)
    return pl.pallas_call(
        body, name=name, grid=(nk, nq),
        in_specs=[pl.BlockSpec((tq, H * AUG), q_idx), pl.BlockSpec((tk, H * AUG), kv_idx), pl.BlockSpec((tk, H * AUG), kv_idx),
                  pl.BlockSpec((tq, W), q_idx), pl.BlockSpec((tq, W), q_idx), pl.BlockSpec((tq, LANES), q_idx)],
        out_specs=[pl.BlockSpec((tk, W), kv_idx), pl.BlockSpec((tk, W), kv_idx), pl.BlockSpec((8, tk), lambda j, i: (0, j))],
        out_shape=[jax.ShapeDtypeStruct((T, W), BF16), jax.ShapeDtypeStruct((T, W), BF16), jax.ShapeDtypeStruct((8, T), F32)],
        scratch_shapes=[pltpu.VMEM((tk, W), F32), pltpu.VMEM((tk, W), F32), pltpu.VMEM((8, tk), F32)],
        compiler_params=_params(("parallel", "arbitrary")))(qa, ka, va, do, o, lse)


def fox_bwd_q(qa, ka, va, do, o, lse, H, *, name, tq=512, tk=512):
    T, W = do.shape
    tq, tk = min(tq, T), min(tk, T)
    nq, nk = T // tq, T // tk

    def body(q_ref, k_ref, v_ref, do_ref, o_ref, lse_ref, dq_ref, dq_acc):
        i, j = pl.program_id(0), pl.program_id(1)

        @pl.when(j == 0)
        def _():
            dq_acc[...] = jnp.zeros_like(dq_acc)

        def tile(diagonal):
            for h in range(H):
                hs = slice(h * HEAD, (h + 1) * HEAD)
                _, ds = _fox_ds(q_ref, k_ref, v_ref, do_ref, o_ref, lse_ref, h, tq, tk, diagonal)
                dq_acc[:, hs] += lax.dot_general(ds.astype(BF16), k_ref[:, h * AUG:h * AUG + HEAD], (((1,), (0,)), ((), ())),
                                                 preferred_element_type=F32)

        _both(j < i, j == i, tile)

        @pl.when(j == nk - 1)
        def _():
            dq_ref[...] = (dq_acc[...] * (HEAD ** -0.5)).astype(dq_ref.dtype)

    q_idx = lambda i, j: (i, 0)
    kv_idx = lambda i, j: (jnp.minimum(j, i), 0)
    return pl.pallas_call(
        body, name=name, grid=(nq, nk),
        in_specs=[pl.BlockSpec((tq, H * AUG), q_idx), pl.BlockSpec((tk, H * AUG), kv_idx), pl.BlockSpec((tk, H * AUG), kv_idx),
                  pl.BlockSpec((tq, W), q_idx), pl.BlockSpec((tq, W), q_idx), pl.BlockSpec((tq, LANES), q_idx)],
        out_specs=pl.BlockSpec((tq, W), q_idx), out_shape=jax.ShapeDtypeStruct((T, W), BF16),
        scratch_shapes=[pltpu.VMEM((tq, W), F32)],
        compiler_params=_params(("parallel", "arbitrary")))(qa, ka, va, do, o, lse)


def _gates_fn(small, a_log_l, dt_bias_l, b_f_l):
    lane = lax.broadcasted_iota(jnp.int32, small.shape, 1)
    beta = _sigmoid(small)
    g = -jnp.exp(a_log_l) * _softplus(small + dt_bias_l)
    lf = _log_sigmoid(small + b_f_l)
    return (jnp.where(lane < LANE_A, beta, jnp.where(lane < LANE_F, g, jnp.where(lane < LANE_F + 8, lf, 0.0))),)


def _gated_norm_fn(o, z, g):
    return (_rms(o, g) * _silu(z),)


def _merge_fn(ya, yb, ym, ga, gb, gm):
    return (_sigmoid(ga) * ya + _sigmoid(gb) * yb + _sigmoid(gm) * ym,)


def _mem_attn_fn(nh, dh, mq, kn, v, gq):
    outs = []
    for h in range(nh):
        hs = slice(h * dh, (h + 1) * dh)
        qn = _rms(mq[:, hs], gq)
        s = NT(qn, kn[:, hs]) * (dh ** -0.5)
        e = jnp.exp(s - jnp.max(s, axis=1, keepdims=True))
        p = e / jnp.sum(e, axis=1, keepdims=True)
        outs.append(NN(p, v[:, hs]))
    return (jnp.concatenate(outs, axis=1),)


def sum_squares(x, *, name, tr=512):
    T, D = x.shape
    tr = min(tr, T)

    def body(x_ref, o_ref):
        @pl.when(pl.program_id(0) == 0)
        def _():
            o_ref[...] = jnp.zeros_like(o_ref)

        v = x_ref[...]
        o_ref[...] += jnp.sum(jnp.sum(v * v, axis=1, keepdims=True), axis=0, keepdims=True)

    return pl.pallas_call(
        body, name=name, grid=(T // tr,), in_specs=[pl.BlockSpec((tr, D), lambda i: (i, 0))],
        out_specs=pl.BlockSpec((1, LANES), lambda i: (0, 0)), out_shape=jax.ShapeDtypeStruct((1, LANES), F32),
        compiler_params=_params(("arbitrary",)))(x)


class Comm:
    def __init__(self, arrays, out_shape, sems, start, forward, finish):
        self.arrays, self.out_shape, self.sems = list(arrays), list(out_shape), list(sems)
        self.start, self.forward, self.finish = start, forward, finish


def _place():
    x, y, c = lax.axis_index("x"), lax.axis_index("y"), lax.axis_index("c")
    chips = [(1 - x, y), (x, 1 - y), (1 - x, 1 - y)]
    return x, y, c, chips


def comm_gather(arrays):
    n = len(arrays)
    lin = lambda px, py, pc: 4 * px + 2 * py + pc

    def copy(ins, outs, sems, a, k, block, to, src=None):
        slot = outs[a].at[lin(*block)]
        return pltpu.make_async_remote_copy(src_ref=slot if src is None else src, dst_ref=slot, send_sem=sems[0].at[a, k],
                                            recv_sem=sems[1].at[a, k], device_id=to, device_id_type=MESH)

    def local(ins, outs, sems, a):
        x, y, c, _ = _place()
        return pltpu.make_async_copy(ins[a], outs[a].at[lin(x, y, c)], sems[2].at[a])

    def start(ins, outs, sems):
        x, y, c, chips = _place()
        for a in range(n):
            local(ins, outs, sems, a).start()
        for a in range(n):
            for j, chip in enumerate(chips):
                copy(ins, outs, sems, a, 1 + j, (x, y, c), (*chip, c), src=ins[a]).start()
            copy(ins, outs, sems, a, 0, (x, y, c), (x, y, 1 - c), src=ins[a]).start()

    def forward(ins, outs, sems):
        x, y, c, chips = _place()
        for a in range(n):
            for j, chip in enumerate(chips):
                copy(ins, outs, sems, a, 1 + j, (*chip, c), (x, y, c)).wait_recv()
                copy(ins, outs, sems, a, 4 + j, (*chip, c), (x, y, 1 - c)).start()

    def finish(ins, outs, sems):
        x, y, c, chips = _place()
        for a in range(n):
            copy(ins, outs, sems, a, 0, (x, y, 1 - c), (x, y, c)).wait_recv()
            for j, chip in enumerate(chips):
                copy(ins, outs, sems, a, 4 + j, (*chip, 1 - c), (x, y, c)).wait_recv()
        for a in range(n):
            for j, chip in enumerate(chips):
                copy(ins, outs, sems, a, 1 + j, (x, y, c), (*chip, c), src=ins[a]).wait_send()
                copy(ins, outs, sems, a, 4 + j, (*chip, c), (x, y, 1 - c)).wait_send()
            copy(ins, outs, sems, a, 0, (x, y, c), (x, y, 1 - c), src=ins[a]).wait_send()
            local(ins, outs, sems, a).wait()

    return Comm(arrays, [jax.ShapeDtypeStruct((N_DEV,) + a.shape, a.dtype) for a in arrays],
                [pltpu.SemaphoreType.DMA((n, 7)), pltpu.SemaphoreType.DMA((n, 7)), pltpu.SemaphoreType.DMA((n,))],
                start, forward, finish)


def comm_direct(arrays, scatter):
    n = len(arrays)

    def peers():
        x, y, c = lax.axis_index("x"), lax.axis_index("y"), lax.axis_index("c")
        out = []
        for r in range(1, N_DEV):
            px, py, pc = (1 - x if r & 4 else x), (1 - y if r & 2 else y), (1 - c if r & 1 else c)
            out.append((r, (px, py, pc), 4 * px + 2 * py + pc))
        return 4 * x + 2 * y + c, out

    def remote(ins, outs, sems, a, r, dev, src_slot, dst_slot):
        return pltpu.make_async_remote_copy(
            src_ref=ins[a].at[src_slot] if scatter[a] else ins[a], dst_ref=outs[a].at[dst_slot],
            send_sem=sems[0].at[a, r - 1], recv_sem=sems[1].at[a, r - 1], device_id=dev, device_id_type=MESH)

    def local(ins, outs, sems, a, me):
        return pltpu.make_async_copy(ins[a].at[me] if scatter[a] else ins[a], outs[a].at[me], sems[2].at[a])

    def start(ins, outs, sems):
        me, ps = peers()
        for a in range(n):
            local(ins, outs, sems, a, me).start()
        for a in range(n):
            for r, dev, lin in ps:
                remote(ins, outs, sems, a, r, dev, lin, me).start()

    def finish(ins, outs, sems):
        me, ps = peers()
        for a in range(n):
            for r, dev, lin in ps:
                remote(ins, outs, sems, a, r, dev, lin, lin).wait_recv()
        for a in range(n):
            for r, dev, lin in ps:
                remote(ins, outs, sems, a, r, dev, lin, me).wait_send()
            local(ins, outs, sems, a, me).wait()

    return Comm(arrays, [jax.ShapeDtypeStruct(a.shape if sc else (N_DEV,) + a.shape, a.dtype) for a, sc in zip(arrays, scatter)],
                [pltpu.SemaphoreType.DMA((n, N_DEV - 1)), pltpu.SemaphoreType.DMA((n, N_DEV - 1)),
                 pltpu.SemaphoreType.DMA((n,))], start, None, finish)


def run_comm(comm, *, name):
    n = len(comm.arrays)

    def body(*refs):
        ins, outs, sems = refs[:n], refs[n:2 * n], refs[2 * n:]
        comm.start(ins, outs, sems)
        if comm.forward is not None:
            comm.forward(ins, outs, sems)
        comm.finish(ins, outs, sems)

    any_spec = pl.BlockSpec(memory_space=pl.ANY)
    return pl.pallas_call(body, name=name, in_specs=[any_spec] * n, out_specs=[any_spec] * n, out_shape=comm.out_shape,
                          scratch_shapes=comm.sems)(*comm.arrays)


def adamw(parts, w, m, v, *, name, tr=128):
    R, Cc = w.shape
    tr = min(tr, R)
    assert R % tr == 0
    n_parts = parts.shape[0]

    def body(p_ref, w_ref, m_ref, v_ref, g_ref, d_ref, nm_ref, nv_ref):
        g = p_ref[0].astype(F32)
        for s in range(1, n_parts):
            g = g + p_ref[s].astype(F32)
        nm = ADAM_B1 * m_ref[...] + (1.0 - ADAM_B1) * g
        nv = ADAM_B2 * v_ref[...] + (1.0 - ADAM_B2) * (g * g)
        m_hat = nm / (1.0 - ADAM_B1 ** ADAM_STEP)
        v_hat = nv / (1.0 - ADAM_B2 ** ADAM_STEP)
        g_ref[...] = g
        d_ref[...] = -ADAM_LR * (m_hat / (jnp.sqrt(v_hat) + ADAM_EPS) + ADAM_WD * w_ref[...])
        nm_ref[...] = nm
        nv_ref[...] = nv

    spec = pl.BlockSpec((tr, Cc), lambda i: (i, 0))
    return pl.pallas_call(
        body, name=name, grid=(R // tr,),
        in_specs=[pl.BlockSpec((n_parts, tr, Cc), lambda i: (0, i, 0)), spec, spec, spec], out_specs=[spec] * 4,
        out_shape=[jax.ShapeDtypeStruct((R, Cc), F32)] * 4, compiler_params=_params(("parallel",)))(parts, w, m, v)


def _lanes(vec, base):
    return jnp.pad(vec[None].astype(F32), ((0, 0), (base, LANES - base - vec.shape[0])))


def _col_shards(full_w):
    R, Ct = full_w.shape
    return jnp.transpose(full_w.reshape(R, N_DEV, Ct // N_DEV), (1, 0, 2))


def _from_col_shards(g):
    return jnp.transpose(g, (1, 0, 2)).reshape(g.shape[1], -1)


def kernel(x, mem, g_mix, w_in, conv_w, a_log, dt_bias, gdn_norm_g, fox_b_f, fox_q_norm, fox_k_norm, g_mem, w_mem_kv, mem_q_norm, mem_k_norm, w_up_gdn, w_up_fox, w_up_mem, w_out, g_mlp, w_ff1, w_ff2, loss_target, m_g_mix, m_w_in, m_conv_w, m_a_log, m_dt_bias, m_gdn_norm_g, m_fox_b_f, m_fox_q_norm, m_fox_k_norm, m_g_mem, m_w_mem_kv, m_mem_q_norm, m_mem_k_norm, m_w_up_gdn, m_w_up_fox, m_w_up_mem, m_w_out, m_g_mlp, m_w_ff1, m_w_ff2, v_g_mix, v_w_in, v_conv_w, v_a_log, v_dt_bias, v_gdn_norm_g, v_fox_b_f, v_fox_q_norm, v_fox_k_norm, v_g_mem, v_w_mem_kv, v_mem_q_norm, v_mem_k_norm, v_w_up_gdn, v_w_up_fox, v_w_up_mem, v_w_out, v_g_mlp, v_w_ff1, v_w_ff2):
    loc = dict(locals())
    big_names = ["w_in", "conv_w", "w_mem_kv", "w_up_gdn", "w_up_fox", "w_up_mem", "w_out", "w_ff1", "w_ff2"]
    col_sharded = {"w_in", "conv_w", "w_up_gdn", "w_up_fox", "w_up_mem", "w_ff1"}
    small_names = ["g_mix", "a_log", "dt_bias", "gdn_norm_g", "fox_b_f", "fox_q_norm", "fox_k_norm", "g_mem",
                   "mem_q_norm", "mem_k_norm", "g_mlp"]

    xs, tgt, mems = x[0], loss_target[0], mem[0]
    T, D = xs.shape
    HG = a_log.shape[1]
    HF = fox_b_f.shape[1]
    DM = mem_q_norm.shape[1]
    GQK, GV = HG * HEAD, HG * HEAD
    GQKV = 2 * GQK + GV
    FW = HF * HEAD
    MW = w_mem_kv.shape[2] // 2
    HM = MW // DM
    assert HG <= 8 and HF <= 8

    shard = {n: loc[n][0].astype(BF16) for n in big_names}
    first, rest, last = big_names[:2], big_names[2:-1], big_names[-1:]
    W = {}

    def take(names, gathered):
        for n, g in zip(names, gathered):
            W[n] = _from_col_shards(g) if n in col_sharded else g.reshape(-1, g.shape[2])

    take(first, run_comm(comm_gather([shard[n] for n in first]), name="gather_in"))
    widths = [GQKV, GV, HG, HG, FW, FW, FW, HF, MW, 3 * D]
    offs = np.concatenate([[0], np.cumsum(widths)]).tolist()
    seg = [W["w_in"][:, offs[i]:offs[i + 1]] for i in range(len(widths))]
    w_big = jnp.concatenate([seg[0], seg[1], seg[4], seg[5], seg[6], seg[8], seg[9]], axis=1)
    pad8 = lambda s: jnp.pad(s, ((0, 0), (0, 8 - s.shape[1])))
    w_small = jnp.concatenate([pad8(seg[2]), pad8(seg[3]), pad8(seg[7]), jnp.zeros((D, LANES - 24), BF16)], axis=1)
    o_z, o_fq, o_fk, o_fv = GQKV, GQKV + GV, GQKV + GV + FW, GQKV + GV + 2 * FW
    o_mq = o_fv + FW
    o_gt = o_mq + MW
    WB = o_gt + 3 * D
    conv_full = W["conv_w"].astype(F32)

    a_log_l, dt_bias_l, b_f_l = _lanes(a_log[0], LANE_A), _lanes(dt_bias[0], LANE_A), _lanes(fox_b_f[0], LANE_F)
    rms_fn = lambda t, g: (_rms(t, g),)

    h, = rowwise(rms_fn, [full(xs)], [g_mix], [(D, D, BF16)], name="rms_mix")
    big, *gathered = matmul(h, w_big, mode="nn", name="proj_big", comm=comm_gather([shard[n] for n in rest]))
    take(rest, gathered)
    small = matmul(h, w_small, mode="nn", name="proj_small", out_dtypes=(F32,))
    bgf, = rowwise(_gates_fn, [full(small)], [a_log_l, dt_bias_l, b_f_l], [(LANES, LANES, F32)], name="gates")

    cn = conv_fwd(big, conv_full, HG, name="conv")
    o_gdn, ss = gdn_fwd(cn, bgf, HG, name="gdn_fwd")
    oa, = rowwise(_gated_norm_fn, [(o_gdn, HEAD, 0, 1), (big, HEAD, o_z // HEAD, 1)], [gdn_norm_g],
                  [(GV, HEAD, BF16)], name="gated_norm", ncb=HG)

    cum = cumsum_tokens(bgf, reverse=False, name="cumsum")
    fqa = fox_prep(big, o_fq, fox_q_norm, cum, 0, HF, name="fox_prep_q")
    fka = fox_prep(big, o_fk, fox_k_norm, cum, 1, HF, name="fox_prep_k")
    fva = fox_prep(big, o_fv, fox_k_norm, cum, 2, HF, name="fox_prep_v")
    ob, lse = fox_fwd(fqa, fka, fva, HF, name="fox_fwd")

    memn, = rowwise(rms_fn, [full(mems)], [g_mem], [(D, D, BF16)], name="rms_mem")
    kv_m = matmul(memn, W["w_mem_kv"], mode="nn", name="mem_kv", out_dtypes=(F32,))
    kmn, = rowwise(rms_fn, [(kv_m, DM, 0, 1)], [mem_k_norm], [(MW, DM, F32)], name="mem_knorm", ncb=HM)
    vm = kv_m[:, MW:]
    mem_fn = functools.partial(_mem_attn_fn, HM, DM)
    om, = rowwise(mem_fn, [(big, MW, o_mq // MW, 0)], [kmn, vm, mem_q_norm], [(MW, MW, BF16)], name="mem_attn")

    ya = matmul(oa, W["w_up_gdn"], mode="nn", name="up_gdn")
    yb = matmul(ob, W["w_up_fox"], mode="nn", name="up_fox")
    ym = matmul(om, W["w_up_mem"], mode="nn", name="up_mem")
    cbm = min(512, D)
    gate_rows = [(big, cbm, (o_gt + b * D) // cbm, 1) for b in range(3)]
    merge_rows = [(ya, cbm, 0, 1), (yb, cbm, 0, 1), (ym, cbm, 0, 1)] + gate_rows
    y, = rowwise(_merge_fn, merge_rows, [], [(D, cbm, BF16)], name="merge", ncb=D // cbm)
    x1 = matmul(y, W["w_out"], mode="nn", name="out_proj", out_dtypes=(F32,), extras=(xs,),
                epi=lambda r, res: (r + res,))

    h2, = rowwise(rms_fn, [full(x1)], [g_mlp], [(D, D, BF16)], name="rms_mlp")
    u_ff, a_ff, *gathered = matmul(h2, W["w_ff1"], mode="nn", name="ff1", out_dtypes=(BF16, BF16),
                                   epi=lambda r: (r, jnp.square(jnp.maximum(r, 0.0))),
                                   comm=comm_gather([shard[n] for n in last]))
    take(last, gathered)
    d_out = matmul(a_ff, W["w_ff2"], mode="nn", name="ff2_loss", out_dtypes=(F32,), extras=(x1, tgt),
                   epi=lambda r, res, t: ((r + res - t) * (1.0 / D),))
    loss_local = 0.5 * D * sum_squares(d_out, name="loss_sum")[0, 0]
    loss = lax.psum(loss_local, ("x", "y", "c"))

    G = {}
    d_u = matmul(d_out, W["w_ff2"], mode="nt", name="d_ff2_in", extras=(u_ff,),
                 epi=lambda r, u: (r * 2.0 * jnp.maximum(u.astype(F32), 0.0),))
    d_h2 = matmul(d_u, W["w_ff1"], mode="nt", name="d_ff1_in")
    d_x1, G["g_mlp"] = rowwise_bwd(rms_fn, [full(x1)], [g_mlp], [full(d_h2)], [(0, F32)], [0], name="d_rms_mlp",
                                   adds=[full(d_out)])
    d_y = matmul(d_x1, W["w_out"], mode="nt", name="d_out_proj_in")
    G["w_out"] = matmul(y, d_x1, mode="tn", name="d_w_out")
    d_ya, d_yb, d_ym, d_ga, d_gb, d_gm = rowwise_bwd(
        _merge_fn, merge_rows, [], [(d_y, cbm, 0, 1)], [(k, BF16) for k in range(6)], [], name="d_merge", ncb=D // cbm)
    d_oa = matmul(d_ya, W["w_up_gdn"], mode="nt", name="d_up_gdn_in")
    d_ob = matmul(d_yb, W["w_up_fox"], mode="nt", name="d_up_fox_in")
    d_om = matmul(d_ym, W["w_up_mem"], mode="nt", name="d_up_mem_in")
    G["w_up_gdn"] = matmul(oa, d_ya, mode="tn", name="d_w_up_gdn")
    G["w_up_fox"] = matmul(ob, d_yb, mode="tn", name="d_w_up_fox")
    G["w_up_mem"] = matmul(om, d_ym, mode="tn", name="d_w_up_mem")

    d_mq, d_kmn, d_vm, G["mem_q_norm"] = rowwise_bwd(
        mem_fn, [(big, MW, o_mq // MW, 0)], [kmn, vm, mem_q_norm], [full(d_om)], [(0, BF16)], [0, 1, 2], name="d_mem_attn")
    d_km, G["mem_k_norm"] = rowwise_bwd(rms_fn, [(kv_m, DM, 0, 1)], [mem_k_norm], [(d_kmn, DM, 0, 1)], [(0, F32)], [0],
                                         name="d_mem_knorm", ncb=HM)
    d_kv_m = jnp.concatenate([d_km, d_vm], axis=1)
    G["w_mem_kv"] = matmul(memn, d_kv_m, mode="tn", name="d_w_mem_kv")
    d_memn = matmul(d_kv_m, W["w_mem_kv"], mode="nt", name="d_mem_kv_in")
    _, G["g_mem"] = rowwise_bwd(rms_fn, [full(mems)], [g_mem], [full(d_memn)], [(0, BF16)], [0], name="d_rms_mem")

    d_fkn, d_fv, d_cum_t = fox_bwd_kv(fqa, fka, fva, d_ob, ob, lse, HF, name="fox_bwd_kv")
    d_fqn = fox_bwd_q(fqa, fka, fva, d_ob, ob, lse, HF, name="fox_bwd_q")
    d_fq, G["fox_q_norm"] = rowwise_bwd(rms_fn, [(big, HEAD, o_fq // HEAD, 1)], [fox_q_norm], [(d_fqn, HEAD, 0, 1)],
                                         [(0, BF16)], [0], name="d_fox_qnorm", ncb=HF)
    d_fk, G["fox_k_norm"] = rowwise_bwd(rms_fn, [(big, HEAD, o_fk // HEAD, 1)], [fox_k_norm], [(d_fkn, HEAD, 0, 1)],
                                         [(0, BF16)], [0], name="d_fox_knorm", ncb=HF)
    d_cum = jnp.pad(d_cum_t[:HF].T, ((0, 0), (LANE_F, LANES - LANE_F - HF)))
    d_logf = cumsum_tokens(d_cum, reverse=True, name="cumsum_rev")

    d_o_gdn, d_z, G["gdn_norm_g"] = rowwise_bwd(
        _gated_norm_fn, [(o_gdn, HEAD, 0, 1), (big, HEAD, o_z // HEAD, 1)], [gdn_norm_g], [(d_oa, HEAD, 0, 1)],
        [(0, F32), (1, BF16)], [0], name="d_gated_norm", ncb=HG)
    d_cn, d_bg = gdn_bwd(cn, bgf, ss, d_o_gdn, HG, name="gdn_bwd")
    d_conv_y, G["conv_w"] = conv_bwd_taps(big, conv_full, d_cn, HG, name="d_conv_taps")
    d_qkv = conv_bwd_input(d_conv_y, conv_full, name="d_conv_in")
    d_small, d_al, d_dt, d_bf = rowwise_bwd(_gates_fn, [full(small)], [a_log_l, dt_bias_l, b_f_l], [full(d_bg + d_logf)],
                                            [(0, F32)], [0, 1, 2], name="d_gates")
    G["a_log"], G["dt_bias"], G["fox_b_f"] = (d_al[:, LANE_A:LANE_A + HG], d_dt[:, LANE_A:LANE_A + HG],
                                               d_bf[:, LANE_F:LANE_F + HF])

    def parts(n):
        g = G[n].astype(BF16)
        return _col_shards(g) if n in col_sharded else g.reshape(N_DEV, -1, g.shape[1])

    recv = {}

    def carried(names, out):
        for n, r in zip(names, out):
            recv[n] = r

    d_big = jnp.concatenate([d_qkv, d_z, d_fq, d_fk, d_fv, d_mq, d_ga, d_gb, d_gm], axis=1)
    group = ["conv_w", "w_mem_kv", "w_up_gdn", "w_up_fox", "w_up_mem", "w_out"]
    G["w_ff2"], *out = matmul(a_ff, d_out, mode="tn", name="d_w_ff2",
                              comm=comm_direct([parts(n) for n in group], [True] * len(group)))
    carried(group, out)
    G["w_ff1"], *out = matmul(h2, d_u, mode="tn", name="d_w_ff1", comm=comm_direct([parts("w_ff2")], [True]))
    carried(["w_ff2"], out)
    g_big, *out = matmul(h, d_big, mode="tn", name="d_w_big", comm=comm_direct([parts("w_ff1")], [True]))
    carried(["w_ff1"], out)
    g_small = matmul(h, d_small, mode="tn", name="d_w_small", out_dtypes=(F32,))
    cols = lambda a, o, wd: a[:, o:o + wd]
    G["w_in"] = jnp.concatenate([
        cols(g_big, 0, GQKV), cols(g_big, o_z, GV), cols(g_small, LANE_B, HG), cols(g_small, LANE_A, HG),
        cols(g_big, o_fq, FW), cols(g_big, o_fk, FW), cols(g_big, o_fv, FW), cols(g_small, LANE_F, HF),
        cols(g_big, o_mq, MW), cols(g_big, o_gt, 3 * D)], axis=1)
    d_h_s = matmul(d_small, w_small, mode="nt", name="d_proj_small_in", out_dtypes=(F32,))
    d_h, *out = matmul(d_big, w_big, mode="nt", name="d_proj_big_in", extras=(d_h_s,), epi=lambda r, e: (r + e,),
                       comm=comm_direct([parts("w_in")], [True]))
    carried(["w_in"], out)
    grad_x, G["g_mix"] = rowwise_bwd(rms_fn, [full(xs)], [g_mix], [full(d_h)], [(0, F32)], [0], name="d_rms_mix",
                                     adds=[full(d_x1)])
    grad_x = grad_x[None]

    small_sizes = [loc[n].shape[1] for n in small_names]
    pack = lambda d: jnp.concatenate([d[n].reshape(1, -1) for n in small_names], axis=1)
    npad = -sum(small_sizes) % LANES
    padp = lambda a: jnp.pad(a, ((0, 0), (0, npad)))
    recv_small, = run_comm(comm_direct([padp(pack(G))], [False]), name="gather_small_grads")

    res = {}
    for n in big_names:
        res[n] = [t[None] for t in adamw(recv[n], loc[n][0], loc["m_" + n][0], loc["v_" + n][0], name="adamw_" + n)]
    sm = adamw(recv_small, padp(pack({n: loc[n] for n in small_names})), padp(pack({n: loc["m_" + n] for n in small_names})),
               padp(pack({n: loc["v_" + n] for n in small_names})), name="adamw_small")
    so = np.concatenate([[0], np.cumsum(small_sizes)]).tolist()
    for i, n in enumerate(small_names):
        res[n] = [t[:, so[i]:so[i + 1]] for t in sm]

    order = ["g_mix", "w_in", "conv_w", "a_log", "dt_bias", "gdn_norm_g", "fox_b_f", "fox_q_norm", "fox_k_norm", "g_mem",
             "w_mem_kv", "mem_q_norm", "mem_k_norm", "w_up_gdn", "w_up_fox", "w_up_mem", "w_out", "g_mlp", "w_ff1", "w_ff2"]
    return (loss, grad_x, *[res[n][0] for n in order], *[res[n][1] for n in order],
            *[res[n][2] for n in order], *[res[n][3] for n in order])
```

```python
import functools

import jax
import jax.numpy as jnp
import numpy as np
from jax import lax
from jax.experimental import pallas as pl
from jax.experimental.pallas import tpu as pltpu

F32 = jnp.float32
BF16 = jnp.bfloat16
HI = lax.Precision.HIGHEST

EPS = 1e-6
GDN_CHUNK = 64
GDN_CONV = 4
HEAD = 128
LANES = 128
HALO = 16
N_DEV = 8
MESH = pl.DeviceIdType.MESH
VMEM_LIMIT_V7X = 56 * 1024 * 1024

ADAM_LR, ADAM_B1, ADAM_B2, ADAM_EPS, ADAM_WD, ADAM_STEP = 0.001, 0.9, 0.999, 1e-08, 0.01, 10

LANE_B, LANE_A, LANE_F = 0, 8, 16


def _params(sem):
    return pltpu.CompilerParams(dimension_semantics=sem, vmem_limit_bytes=VMEM_LIMIT_V7X)


def _dg(a, b, ca, cb, prec):
    nb = a.ndim - 2
    batch = tuple(range(nb))
    return lax.dot_general(a, b, (((ca + nb,), (cb + nb,)), (batch, batch)), precision=prec,
                           preferred_element_type=F32)


def _make_mm(prec, cast):
    def c(x):
        return x.astype(BF16) if cast else x

    @jax.custom_vjp
    def nn(a, b):
        return _dg(c(a), c(b), 1, 0, prec)

    @jax.custom_vjp
    def nt(a, b):
        return _dg(c(a), c(b), 1, 1, prec)

    @jax.custom_vjp
    def tn(a, b):
        return _dg(c(a), c(b), 0, 0, prec)

    nn.defvjp(lambda a, b: (nn(a, b), (a, b)), lambda r, g: (nt(g, r[1]), tn(r[0], g)))
    nt.defvjp(lambda a, b: (nt(a, b), (a, b)), lambda r, g: (nn(g, r[1]), tn(g, r[0])))
    tn.defvjp(lambda a, b: (tn(a, b), (a, b)), lambda r, g: (nt(r[1], g), nn(r[0], g)))
    return nn, nt, tn


NN, NT, TN = _make_mm(None, True)
NNH, NTH, TNH = _make_mm(lax.Precision.HIGH, False)
NNX, _, _ = _make_mm(HI, False)


def _sigmoid(x):
    return 1.0 / (1.0 + jnp.exp(-x))


def _silu(x):
    return x * _sigmoid(x)


def _softplus(x):
    return jnp.maximum(x, 0.0) + jnp.log(1.0 + jnp.exp(-jnp.abs(x)))


def _log_sigmoid(x):
    return -_softplus(-x)


def _rms(x, g):
    return x * lax.rsqrt(jnp.mean(x * x, axis=-1, keepdims=True) + EPS) * g


def _tile(n, target):
    t = target
    while t >= LANES:
        if n % t == 0:
            return t
        t //= 2
    return n


def matmul(a, b, *, mode, name, out_dtypes=(BF16,), epi=None, extras=(), tm=1024, tn=1024, tk=2048, comm=None,
           b_shards=False, out_shards=False):
    if b_shards:
        b_rows, b_cols = b.shape[1], N_DEV * b.shape[2]
    else:
        b_rows, b_cols = b.shape
    if mode == "nn":
        (M, K), (K2, N) = a.shape, (b_rows, b_cols)
    elif mode == "nt":
        (M, K), (N, K2) = a.shape, (b_rows, b_cols)
    else:
        (K, M), (K2, N) = a.shape, (b_rows, b_cols)
    assert K == K2, (name, a.shape, b.shape)
    tm, tn, tk = _tile(M, tm), _tile(N, tn), _tile(K, tk)
    if b_shards and mode == "nt":
        tk = K // N_DEV
    if (b_shards and mode != "nt") or out_shards:
        tn = N // N_DEV
    ni, nj, nk = M // tm, N // tn, K // tk
    a_spec = (pl.BlockSpec((tk, tm), lambda i, j, k: (k, i)) if mode == "tn"
              else pl.BlockSpec((tm, tk), lambda i, j, k: (i, k)))
    if b_shards:
        b_spec = (pl.BlockSpec((None, tn, tk), lambda i, j, k: (k, j, 0)) if mode == "nt"
                  else pl.BlockSpec((None, tk, tn), lambda i, j, k: (j, k, 0)))
    else:
        b_spec = (pl.BlockSpec((tn, tk), lambda i, j, k: (j, k)) if mode == "nt"
                  else pl.BlockSpec((tk, tn), lambda i, j, k: (k, j)))
    o_spec = pl.BlockSpec((tm, tn), lambda i, j, k: (i, j))
    w_spec = pl.BlockSpec((None, tm, tn), lambda i, j, k: (j, i, 0)) if out_shards else o_spec
    w_shape = (N_DEV, M, tn) if out_shards else (M, N)
    dims = {"nn": ((1,), (0,)), "nt": ((1,), (1,)), "tn": ((0,), (0,))}[mode]
    ne, no = len(extras), len(out_dtypes)
    nc = len(comm.arrays) if comm is not None else 0
    n_steps = ni * nj * nk

    def body(a_ref, b_ref, *rest):
        ex, c_in = rest[:ne], rest[ne:ne + nc]
        outs, c_out = rest[ne + nc:ne + nc + no], rest[ne + nc + no:ne + nc + no + nc]
        acc = rest[ne + nc + no + nc]
        c_sems = rest[ne + nc + no + nc + 1:]
        k = pl.program_id(2)
        step = (pl.program_id(0) * nj + pl.program_id(1)) * nk + k

        if comm is not None:
            pl.when(step == 0)(lambda: comm.start(c_in, c_out, c_sems))
            if comm.forward is not None and n_steps >= 3:
                pl.when(step == (3 * n_steps) // 4)(lambda: comm.forward(c_in, c_out, c_sems))

        @pl.when(k == 0)
        def _():
            acc[...] = jnp.zeros_like(acc)

        acc[...] += lax.dot_general(a_ref[...].astype(BF16), b_ref[...].astype(BF16), (dims, ((), ())),
                                    preferred_element_type=F32)

        @pl.when(k == nk - 1)
        def _():
            r = acc[...]
            vals = epi(r, *[e[...] for e in ex]) if epi is not None else (r,)
            for o, v in zip(outs, vals):
                o[...] = v.astype(o.dtype)

        if comm is not None:
            @pl.when(step == n_steps - 1)
            def _():
                if comm.forward is not None and n_steps < 3:
                    comm.forward(c_in, c_out, c_sems)
                comm.finish(c_in, c_out, c_sems)

    any_spec = pl.BlockSpec(memory_space=pl.ANY)
    sem = ("arbitrary",) * 3 if comm is not None else ("parallel", "parallel", "arbitrary")
    out = pl.pallas_call(
        body, name=name, grid=(ni, nj, nk),
        in_specs=[a_spec, b_spec] + [o_spec] * ne + [any_spec] * nc, out_specs=[w_spec] * no + [any_spec] * nc,
        out_shape=[jax.ShapeDtypeStruct(w_shape, d) for d in out_dtypes] + (comm.out_shape if comm is not None else []),
        scratch_shapes=[pltpu.VMEM((tm, tn), F32)] + (comm.sems if comm is not None else []),
        compiler_params=_params(sem))(a, b, *extras, *(comm.arrays if comm is not None else []))
    return out[0] if len(out) == 1 else out


def _row_spec(tr, cb, off, moves):
    return pl.BlockSpec((tr, cb), lambda i, j: (i, off + moves * j))


def _whole_spec(p):
    return pl.BlockSpec(p.shape, lambda i, j: (0,) * p.ndim)


def _row_tile(T, rows):
    widest = max(cb for (_, cb, _, _) in rows)
    return min(T, max(512, (1 << 19) // widest))


def rowwise(fn, rows, params, outs, *, name, ncb=1, tr=None):
    T = rows[0][0].shape[0]
    tr = min(tr, T) if tr else _row_tile(T, rows)
    assert T % tr == 0
    nr, npar = len(rows), len(params)

    def body(*refs):
        r, p, o = refs[:nr], refs[nr:nr + npar], refs[nr + npar:]
        vals = fn(*[x[...].astype(F32) for x in r], *[x[...] for x in p])
        for oref, v in zip(o, vals):
            oref[...] = v.astype(oref.dtype)

    res = pl.pallas_call(
        body, name=name, grid=(T // tr, ncb),
        in_specs=[_row_spec(tr, cb, off, mv) for (_, cb, off, mv) in rows] + [_whole_spec(p) for p in params],
        out_specs=[_row_spec(tr, cb, 0, 1) for (_, cb, _) in outs],
        out_shape=[jax.ShapeDtypeStruct((T, cols), d) for (cols, _, d) in outs],
        compiler_params=_params(("parallel", "parallel")))(*[r[0] for r in rows], *params)
    return res


def rowwise_bwd(fn, rows, params, cots, drows, dparams, *, name, ncb=1, tr=None, adds=()):
    T = rows[0][0].shape[0]
    tr = min(tr, T) if tr else _row_tile(T, rows)
    assert T % tr == 0
    nr, npar, nc, ndr, na = len(rows), len(params), len(cots), len(drows), len(adds)

    def body(*refs):
        r, p, c = refs[:nr], refs[nr:nr + npar], refs[nr + npar:nr + npar + nc]
        base = nr + npar + nc + na
        ad, o_r, o_p = refs[base - na:base], refs[base:base + ndr], refs[base + ndr:]
        prim = [x[...].astype(F32) for x in r] + [x[...] for x in p]
        _, vjp = jax.vjp(lambda *a: tuple(fn(*a)), *prim)
        g = vjp(tuple(x[...].astype(F32) for x in c))
        for k, (oref, (idx, _)) in enumerate(zip(o_r, drows)):
            val = g[idx] + ad[k][...].astype(F32) if k < na else g[idx]
            oref[...] = val.astype(oref.dtype)
        first = jnp.logical_and(pl.program_id(0) == 0, pl.program_id(1) == 0)

        @pl.when(first)
        def _():
            for oref in o_p:
                oref[...] = jnp.zeros_like(oref)

        for oref, idx in zip(o_p, dparams):
            oref[...] += g[nr + idx]

    res = pl.pallas_call(
        body, name=name, grid=(T // tr, ncb),
        in_specs=([_row_spec(tr, cb, off, mv) for (_, cb, off, mv) in rows] + [_whole_spec(p) for p in params]
                  + [_row_spec(tr, cb, off, mv) for (_, cb, off, mv) in tuple(cots) + tuple(adds)]),
        out_specs=([_row_spec(tr, rows[idx][1], 0, 1) for (idx, _) in drows]
                   + [_whole_spec(params[idx]) for idx in dparams]),
        out_shape=([jax.ShapeDtypeStruct((T, ncb * rows[idx][1] if rows[idx][3] else rows[idx][1]), d)
                    for (idx, d) in drows]
                   + [jax.ShapeDtypeStruct(params[idx].shape, F32) for idx in dparams]),
        compiler_params=_params(("arbitrary", "arbitrary")))(
            *[r[0] for r in rows], *params, *[c[0] for c in cots], *[a[0] for a in adds])
    return res


def full(a):
    return (a, a.shape[1], 0, 0)


def cumsum_tokens(x, *, reverse, name, tb=256):
    T = x.shape[0]
    tb = min(tb, T)
    nb = T // tb
    idx = (lambda i: (nb - 1 - i, 0)) if reverse else (lambda i: (i, 0))

    def body(x_ref, o_ref, carry):
        @pl.when(pl.program_id(0) == 0)
        def _():
            carry[...] = jnp.zeros_like(carry)

        ii = lax.broadcasted_iota(jnp.int32, (tb, tb), 0)
        jj = lax.broadcasted_iota(jnp.int32, (tb, tb), 1)
        tri = ((ii <= jj) if reverse else (ii >= jj)).astype(F32)
        c = lax.dot_general(tri, x_ref[...], (((1,), (0,)), ((), ())), precision=HI,
                            preferred_element_type=F32) + carry[0:1, :]
        o_ref[...] = c
        carry[0:1, :] = c[0:1, :] if reverse else c[tb - 1:tb, :]

    return pl.pallas_call(
        body, name=name, grid=(nb,), in_specs=[pl.BlockSpec((tb, LANES), idx)],
        out_specs=pl.BlockSpec((tb, LANES), idx), out_shape=jax.ShapeDtypeStruct((T, LANES), F32),
        scratch_shapes=[pltpu.VMEM((8, LANES), F32)], compiler_params=_params(("arbitrary",)))(x)


def _conv_post(y, kind, dk):
    c = _silu(y)
    r = lax.rsqrt(jnp.sum(c * c, axis=-1, keepdims=True) + EPS)
    return jnp.where(kind == 0, c * r * (dk ** -0.5), jnp.where(kind == 1, c * r, c))


def _conv_taps(cur, prev, w, tr):
    ext = jnp.concatenate([prev, cur], axis=0)
    y = w[3:4, :] * cur
    for d in (1, 2, 3):
        y = y + w[3 - d:4 - d, :] * pltpu.roll(ext, d, 0)[HALO:HALO + tr]
    return y


def conv_fwd(big, w, n_qk_heads, *, name, tr=2048):
    T, W = big.shape[0], w.shape[1]
    tr = min(tr, T)
    nh = W // HEAD

    def body(cur_ref, prev_ref, w_ref, o_ref):
        i, j = pl.program_id(0), pl.program_id(1)
        prev = jnp.where(i > 0, prev_ref[...].astype(F32), 0.0)
        y = _conv_taps(cur_ref[...].astype(F32), prev, w_ref[...], tr)
        kind = jnp.where(j < n_qk_heads, 0, jnp.where(j < 2 * n_qk_heads, 1, 2))
        o_ref[...] = _conv_post(y, kind, HEAD)

    return pl.pallas_call(
        body, name=name, grid=(T // tr, nh),
        in_specs=[pl.BlockSpec((tr, HEAD), lambda i, j: (i, j)),
                  pl.BlockSpec((HALO, HEAD), lambda i, j: (jnp.maximum(i * (tr // HALO) - 1, 0), j)),
                  pl.BlockSpec((GDN_CONV, HEAD), lambda i, j: (0, j))],
        out_specs=pl.BlockSpec((tr, HEAD), lambda i, j: (i, j)),
        out_shape=jax.ShapeDtypeStruct((T, W), F32), compiler_params=_params(("parallel", "parallel")))(big, big, w)


def conv_bwd_taps(big, w, dcn, n_qk_heads, *, name, tr=2048):
    T, W = big.shape[0], w.shape[1]
    tr = min(tr, T)
    nh = W // HEAD

    def body(cur_ref, prev_ref, w_ref, g_ref, dy_ref, dw_ref):
        j, i = pl.program_id(0), pl.program_id(1)
        cur = cur_ref[...].astype(F32)
        prev = jnp.where(i > 0, prev_ref[...].astype(F32), 0.0)
        y = _conv_taps(cur, prev, w_ref[...], tr)
        kind = jnp.where(j < n_qk_heads, 0, jnp.where(j < 2 * n_qk_heads, 1, 2))
        _, vjp = jax.vjp(lambda t: _conv_post(t, kind, HEAD), y)
        dy, = vjp(g_ref[...])
        dy_ref[...] = dy
        ext = jnp.concatenate([prev, cur], axis=0)
        rows = [jnp.sum(dy * (cur if d == 0 else pltpu.roll(ext, d, 0)[HALO:HALO + tr]), axis=0, keepdims=True)
                for d in (3, 2, 1, 0)]

        @pl.when(i == 0)
        def _():
            dw_ref[...] = jnp.zeros_like(dw_ref)

        dw_ref[...] += jnp.concatenate(rows, axis=0)

    return pl.pallas_call(
        body, name=name, grid=(nh, T // tr),
        in_specs=[pl.BlockSpec((tr, HEAD), lambda j, i: (i, j)),
                  pl.BlockSpec((HALO, HEAD), lambda j, i: (jnp.maximum(i * (tr // HALO) - 1, 0), j)),
                  pl.BlockSpec((GDN_CONV, HEAD), lambda j, i: (0, j)),
                  pl.BlockSpec((tr, HEAD), lambda j, i: (i, j))],
        out_specs=[pl.BlockSpec((tr, HEAD), lambda j, i: (i, j)), pl.BlockSpec((GDN_CONV, HEAD), lambda j, i: (0, j))],
        out_shape=[jax.ShapeDtypeStruct((T, W), F32), jax.ShapeDtypeStruct((GDN_CONV, W), F32)],
        compiler_params=_params(("parallel", "arbitrary")))(big, big, w, dcn)


def conv_bwd_input(dy, w, *, name, tr=2048):
    T, W = dy.shape
    tr = min(tr, T)
    nrow = T // tr

    def body(cur_ref, nxt_ref, w_ref, o_ref):
        i = pl.program_id(0)
        cur = cur_ref[...]
        nxt = jnp.where(i < nrow - 1, nxt_ref[...], 0.0)
        ext = jnp.concatenate([cur, nxt], axis=0)
        w = w_ref[...]
        dx = w[3:4, :] * cur
        for d in (1, 2, 3):
            dx = dx + w[3 - d:4 - d, :] * pltpu.roll(ext, tr + HALO - d, 0)[0:tr]
        o_ref[...] = dx.astype(o_ref.dtype)

    return pl.pallas_call(
        body, name=name, grid=(nrow, W // HEAD),
        in_specs=[pl.BlockSpec((tr, HEAD), lambda i, j: (i, j)),
                  pl.BlockSpec((HALO, HEAD), lambda i, j: (jnp.minimum((i + 1) * (tr // HALO), T // HALO - 1), j)),
                  pl.BlockSpec((GDN_CONV, HEAD), lambda i, j: (0, j))],
        out_specs=pl.BlockSpec((tr, HEAD), lambda i, j: (i, j)),
        out_shape=jax.ShapeDtypeStruct((T, W), BF16), compiler_params=_params(("parallel", "parallel")))(dy, dy, w)


INV_BLOCK = 16


def _unit_lower_inverse_value(p):
    C = p.shape[-1]
    ii = lax.broadcasted_iota(jnp.int32, (C, C), 0)
    jj = lax.broadcasted_iota(jnp.int32, (C, C), 1)
    eye = jnp.where((ii == jj)[None], 1.0, 0.0)
    same = ((ii // INV_BLOCK) == (jj // INV_BLOCK))[None]
    pd = jnp.where(same, p, 0.0)
    d_inv = eye + pd
    n = 2
    while n < INV_BLOCK:
        pd = NNH(pd, pd)
        d_inv = d_inv + NNH(d_inv, pd)
        n *= 2
    nb = NNH(d_inv, jnp.where(same, 0.0, p))
    t = eye + nb
    n = 2
    while n < C // INV_BLOCK:
        nb = NNH(nb, nb)
        t = t + NNH(t, nb)
        n *= 2
    return NNH(t, d_inv)


@jax.custom_vjp
def _unit_lower_inverse(p, t_known):
    return _unit_lower_inverse_value(p) if t_known is None else t_known


def _unit_lower_inverse_fwd(p, t_known):
    t = _unit_lower_inverse(p, t_known)
    return t, (t, t_known is not None)


def _unit_lower_inverse_bwd(res, g):
    t, had = res
    return NTH(TNH(t, g), t), (jnp.zeros_like(t) if had else None)


_unit_lower_inverse.defvjp(_unit_lower_inverse_fwd, _unit_lower_inverse_bwd)


def _gdn_chunk(q, k, v, bg, S, t_known=None):
    H, C = q.shape[0], q.shape[1]
    ii = lax.broadcasted_iota(jnp.int32, (C, C), 0)
    jj = lax.broadcasted_iota(jnp.int32, (C, C), 1)
    lincl = (ii >= jj).astype(F32)
    strict, incl, eye = (ii > jj)[None], (ii >= jj)[None], (ii == jj)[None]
    gam2d = lax.dot_general(lincl, bg, (((1,), (0,)), ((), ())), precision=HI, preferred_element_type=F32)
    lane = lax.broadcasted_iota(jnp.int32, (H, 1, LANES), 2)
    hh = lax.broadcasted_iota(jnp.int32, (H, 1, LANES), 0)
    beta = jnp.sum(bg[None] * (lane == hh + LANE_B).astype(F32), axis=2, keepdims=True)
    gam = jnp.sum(gam2d[None] * (lane == hh + LANE_A).astype(F32), axis=2, keepdims=True)
    last = (lax.broadcasted_iota(jnp.int32, (1, C, 1), 1) == C - 1).astype(F32)
    gam_last = jnp.sum(gam * last, axis=1, keepdims=True)
    gam_row = NNX(jnp.ones((H, C, C), F32), jnp.where(eye, gam, 0.0))
    diff = gam - gam_row
    dec_s = jnp.where(strict, jnp.exp(jnp.where(strict, diff, 0.0)), 0.0)
    dec_i = jnp.where(incl, jnp.exp(jnp.where(incl, diff, 0.0)), 0.0)
    t = _unit_lower_inverse(-(beta * NT(k, k) * dec_s), t_known)
    eg = jnp.exp(gam)
    w = NNH(t, beta * eg * k)
    u0 = NNH(t, beta * v)
    qk = NT(q, k) * dec_i
    u = u0 - NN(w, S)
    o = NN(q * eg, S) + NN(qk, u)
    S2 = jnp.exp(gam_last) * S + TN(k * jnp.exp(gam_last - gam), u)
    return o, S2, t


def _heads(x, base, H):
    return jnp.stack([x[:, base + h * HEAD:base + (h + 1) * HEAD] for h in range(H)])


def _unheads(x):
    return jnp.concatenate([x[h] for h in range(x.shape[0])], axis=1)


def gdn_fwd(cn, bgf, H, *, name):
    T, C, W = cn.shape[0], GDN_CHUNK, H * HEAD
    N = T // C

    def body(cn_ref, bg_ref, o_ref, ss_ref, t_ref, s_scr):
        @pl.when(pl.program_id(0) == 0)
        def _():
            s_scr[...] = jnp.zeros_like(s_scr)

        x, S = cn_ref[...], s_scr[...]
        ss_ref[0] = S
        o, S2, t = _gdn_chunk(_heads(x, 0, H), _heads(x, W, H), _heads(x, 2 * W, H), bg_ref[...], S)
        o_ref[...] = _unheads(o)
        t_ref[0] = t
        s_scr[...] = S2

    return pl.pallas_call(
        body, name=name, grid=(N,),
        in_specs=[pl.BlockSpec((C, 3 * W), lambda n: (n, 0)), pl.BlockSpec((C, LANES), lambda n: (n, 0))],
        out_specs=[pl.BlockSpec((C, W), lambda n: (n, 0)), pl.BlockSpec((1, H, HEAD, HEAD), lambda n: (n, 0, 0, 0)),
                   pl.BlockSpec((1, H, C, C), lambda n: (n, 0, 0, 0))],
        out_shape=[jax.ShapeDtypeStruct((T, W), F32), jax.ShapeDtypeStruct((N, H, HEAD, HEAD), F32),
                   jax.ShapeDtypeStruct((N, H, C, C), F32)],
        scratch_shapes=[pltpu.VMEM((H, HEAD, HEAD), F32)], compiler_params=_params(("arbitrary",)))(cn, bgf)


def gdn_bwd(cn, bgf, ss, tinv, do, H, *, name):
    T, C, W = cn.shape[0], GDN_CHUNK, H * HEAD
    N = T // C

    def body(cn_ref, bg_ref, ss_ref, t_ref, do_ref, dcn_ref, dbg_ref, ds_scr):
        @pl.when(pl.program_id(0) == 0)
        def _():
            ds_scr[...] = jnp.zeros_like(ds_scr)

        x = cn_ref[...]
        t_known = t_ref[0]
        _, vjp = jax.vjp(lambda *a: _gdn_chunk(*a, t_known)[:2],
                         _heads(x, 0, H), _heads(x, W, H), _heads(x, 2 * W, H), bg_ref[...], ss_ref[0])
        dq, dk, dv, dbg, dS = vjp((_heads(do_ref[...], 0, H), ds_scr[...]))
        dcn_ref[...] = jnp.concatenate([_unheads(dq), _unheads(dk), _unheads(dv)], axis=1)
        dbg_ref[...] = dbg
        ds_scr[...] = dS

    rev = lambda n: (N - 1 - n, 0)
    return pl.pallas_call(
        body, name=name, grid=(N,),
        in_specs=[pl.BlockSpec((C, 3 * W), rev), pl.BlockSpec((C, LANES), rev),
                  pl.BlockSpec((1, H, HEAD, HEAD), lambda n: (N - 1 - n, 0, 0, 0)),
                  pl.BlockSpec((1, H, C, C), lambda n: (N - 1 - n, 0, 0, 0)), pl.BlockSpec((C, W), rev)],
        out_specs=[pl.BlockSpec((C, 3 * W), rev), pl.BlockSpec((C, LANES), rev)],
        out_shape=[jax.ShapeDtypeStruct((T, 3 * W), F32), jax.ShapeDtypeStruct((T, LANES), F32)],
        scratch_shapes=[pltpu.VMEM((H, HEAD, HEAD), F32)], compiler_params=_params(("arbitrary",)))(cn, bgf, ss, tinv, do)


AUG = 2 * HEAD


def fox_prep(big, col_off, gain, cum, kind, H, *, name, tr=2048):
    T = big.shape[0]
    tr = min(tr, T)

    def body(x_ref, g_ref, c_ref, o_ref):
        h = pl.program_id(1)
        x = x_ref[...].astype(F32)
        lane = lax.broadcasted_iota(jnp.int32, (tr, HEAD), 1)
        if kind == 2:
            main, aug = x, jnp.ones((tr, HEAD), F32)
        else:
            main = _rms(x, g_ref[...]) * ((HEAD ** -0.5) if kind == 0 else 1.0)
            c = jnp.sum(jnp.where(lane == LANE_F + h, c_ref[...], 0.0), axis=1, keepdims=True)
            hi = c.astype(BF16).astype(F32)
            mid = (c - hi).astype(BF16).astype(F32)
            lo = c - hi - mid
            if kind == 0:
                aug = jnp.where(lane == 0, hi, jnp.where(lane == 1, mid, jnp.where(lane == 2, lo,
                                                                                   jnp.where(lane < 6, 1.0, 0.0))))
            else:
                aug = jnp.where(lane < 3, 1.0, jnp.where(lane == 3, -hi, jnp.where(lane == 4, -mid,
                                                                                   jnp.where(lane == 5, -lo, 0.0))))
        o_ref[...] = jnp.concatenate([main, aug], axis=1).astype(o_ref.dtype)

    return pl.pallas_call(
        body, name=name, grid=(T // tr, H),
        in_specs=[pl.BlockSpec((tr, HEAD), lambda i, h: (i, col_off // HEAD + h)),
                  pl.BlockSpec((1, HEAD), lambda i, h: (0, 0)), pl.BlockSpec((tr, LANES), lambda i, h: (i, 0))],
        out_specs=pl.BlockSpec((tr, AUG), lambda i, h: (i, h)),
        out_shape=jax.ShapeDtypeStruct((T, H * AUG), BF16), compiler_params=_params(("parallel", "parallel")))(big, gain, cum)


def _fox_logits(q_ref, k_ref, h, tq, tk, diagonal):
    ha = slice(h * AUG, (h + 1) * AUG)
    s = lax.dot_general(q_ref[:, ha], k_ref[:, ha], (((1,), (1,)), ((), ())), preferred_element_type=F32)
    keep = None
    if diagonal:
        keep = lax.broadcasted_iota(jnp.int32, (tq, tk), 0) >= lax.broadcasted_iota(jnp.int32, (tq, tk), 1)
    return s, keep


def _both(cond_off, cond_diag, fn):
    pl.when(cond_off)(functools.partial(fn, False))
    pl.when(cond_diag)(functools.partial(fn, True))


def fox_fwd(qa, ka, va, H, *, name, tq=512, tk=512):
    T = qa.shape[0]
    tq, tk = min(tq, T), min(tk, T)
    nq, nk = T // tq, T // tk
    assert tq == tk

    def body(q_ref, k_ref, v_ref, o_ref, lse_ref, acc, m_scr):
        i, j = pl.program_id(0), pl.program_id(1)

        @pl.when(j == 0)
        def _():
            m_scr[...] = jnp.full_like(m_scr, -jnp.inf)
            acc[...] = jnp.zeros_like(acc)

        def tile(diagonal):
            lane = lax.broadcasted_iota(jnp.int32, (tq, LANES), 1)
            m_all = m_scr[...]
            new_m = m_all
            for h in range(H):
                ha = slice(h * AUG, (h + 1) * AUG)
                s, keep = _fox_logits(q_ref, k_ref, h, tq, tk, diagonal)
                if diagonal:
                    s = jnp.where(keep, s, -jnp.inf)
                m_prev = m_all[:, h:h + 1]
                m_new = jnp.maximum(m_prev, jnp.max(s, axis=1, keepdims=True))
                p = jnp.exp(s - m_new)
                p_hi = p.astype(BF16)
                p_lo = (p - p_hi.astype(F32)).astype(BF16)
                pv = lambda t: lax.dot_general(t, v_ref[:, ha], (((1,), (0,)), ((), ())), preferred_element_type=F32)
                acc[:, ha] = jnp.exp(m_prev - m_new) * acc[:, ha] + (pv(p_hi) + pv(p_lo))
                new_m = jnp.where(lane == h, m_new, new_m)
            m_scr[...] = new_m

        _both(j < i, j == i, tile)

        @pl.when(j == nk - 1)
        def _():
            lane = lax.broadcasted_iota(jnp.int32, (tq, LANES), 1)
            m_all = m_scr[...]
            lse = jnp.zeros((tq, LANES), F32)
            for h in range(H):
                den = acc[:, h * AUG + HEAD:(h + 1) * AUG]
                o_ref[:, h * HEAD:(h + 1) * HEAD] = acc[:, h * AUG:h * AUG + HEAD] / den
                lse = jnp.where(lane == h, m_all[:, h:h + 1] + jnp.log(den[:, 0:1]), lse)
            lse_ref[...] = lse

    kv_idx = lambda i, j: (jnp.minimum(j, i), 0)
    q_idx = lambda i, j: (i, 0)
    return pl.pallas_call(
        body, name=name, grid=(nq, nk),
        in_specs=[pl.BlockSpec((tq, H * AUG), q_idx), pl.BlockSpec((tk, H * AUG), kv_idx), pl.BlockSpec((tk, H * AUG), kv_idx)],
        out_specs=[pl.BlockSpec((tq, H * HEAD), q_idx), pl.BlockSpec((tq, LANES), q_idx)],
        out_shape=[jax.ShapeDtypeStruct((T, H * HEAD), F32), jax.ShapeDtypeStruct((T, LANES), F32)],
        scratch_shapes=[pltpu.VMEM((tq, H * AUG), F32), pltpu.VMEM((tq, LANES), F32)],
        compiler_params=_params(("parallel", "arbitrary")))(qa, ka, va)


def _fox_ds(q_ref, k_ref, v_ref, do_ref, o_ref, lse_ref, h, tq, tk, diagonal):
    hs = slice(h * HEAD, (h + 1) * HEAD)
    s, keep = _fox_logits(q_ref, k_ref, h, tq, tk, diagonal)
    p = jnp.exp(s - lse_ref[:, h:h + 1])
    if diagonal:
        p = jnp.where(keep, p, 0.0)
    do = do_ref[:, hs]
    dp = lax.dot_general(do, v_ref[:, h * AUG:h * AUG + HEAD], (((1,), (1,)), ((), ())), preferred_element_type=F32)
    delta = jnp.sum(do.astype(F32) * o_ref[:, hs], axis=1, keepdims=True)
    return p, p * (dp - delta)


def fox_bwd_kv(qa, ka, va, do, o, lse, H, *, name, tq=512, tk=512):
    T, W = do.shape
    tq, tk = min(tq, T), min(tk, T)
    nq, nk = T // tq, T // tk

    def body(q_ref, k_ref, v_ref, do_ref, o_ref, lse_ref, dk_ref, dv_ref, dc_ref, dk_acc, dv_acc, dc_acc):
        j, i = pl.program_id(0), pl.program_id(1)

        @pl.when(i == 0)
        def _():
            dk_acc[...] = jnp.zeros_like(dk_acc)
            dv_acc[...] = jnp.zeros_like(dv_acc)
            dc_acc[...] = jnp.zeros_like(dc_acc)

        def tile(diagonal):
            row = lax.broadcasted_iota(jnp.int32, (8, tk), 0)
            dc = dc_acc[...]
            for h in range(H):
                hs = slice(h * HEAD, (h + 1) * HEAD)
                p, ds = _fox_ds(q_ref, k_ref, v_ref, do_ref, o_ref, lse_ref, h, tq, tk, diagonal)
                dv_acc[:, hs] += lax.dot_general(p.astype(BF16), do_ref[:, hs], (((0,), (0,)), ((), ())),
                                                 preferred_element_type=F32)
                dk_acc[:, hs] += lax.dot_general(ds.astype(BF16), q_ref[:, h * AUG:h * AUG + HEAD],
                                                 (((0,), (0,)), ((), ())), preferred_element_type=F32)
                dc = jnp.where(row == h, dc - jnp.sum(ds, axis=0, keepdims=True), dc)
            dc_acc[...] = dc

        _both(i > j, i == j, tile)

        @pl.when(i == nq - 1)
        def _():
            dk_ref[...] = dk_acc[...].astype(dk_ref.dtype)
            dv_ref[...] = dv_acc[...].astype(dv_ref.dtype)
            dc_ref[...] = dc_acc[...]

    q_idx = lambda j, i: (jnp.maximum(i, j), 0)
    kv_idx = lambda j, i: (j, 0)
    return pl.pallas_call(
        body, name=name, grid=(nk, nq),
        in_specs=[pl.BlockSpec((tq, H * AUG), q_idx), pl.BlockSpec((tk, H * AUG), kv_idx), pl.BlockSpec((tk, H * AUG), kv_idx),
                  pl.BlockSpec((tq, W), q_idx), pl.BlockSpec((tq, W), q_idx), pl.BlockSpec((tq, LANES), q_idx)],
        out_specs=[pl.BlockSpec((tk, W), kv_idx), pl.BlockSpec((tk, W), kv_idx), pl.BlockSpec((8, tk), lambda j, i: (0, j))],
        out_shape=[jax.ShapeDtypeStruct((T, W), BF16), jax.ShapeDtypeStruct((T, W), BF16), jax.ShapeDtypeStruct((8, T), F32)],
        scratch_shapes=[pltpu.VMEM((tk, W), F32), pltpu.VMEM((tk, W), F32), pltpu.VMEM((8, tk), F32)],
        compiler_params=_params(("parallel", "arbitrary")))(qa, ka, va, do, o, lse)


def fox_bwd_q(qa, ka, va, do, o, lse, H, *, name, tq=512, tk=512):
    T, W = do.shape
    tq, tk = min(tq, T), min(tk, T)
    nq, nk = T // tq, T // tk

    def body(q_ref, k_ref, v_ref, do_ref, o_ref, lse_ref, dq_ref, dq_acc):
        i, j = pl.program_id(0), pl.program_id(1)

        @pl.when(j == 0)
        def _():
            dq_acc[...] = jnp.zeros_like(dq_acc)

        def tile(diagonal):
            for h in range(H):
                hs = slice(h * HEAD, (h + 1) * HEAD)
                _, ds = _fox_ds(q_ref, k_ref, v_ref, do_ref, o_ref, lse_ref, h, tq, tk, diagonal)
                dq_acc[:, hs] += lax.dot_general(ds.astype(BF16), k_ref[:, h * AUG:h * AUG + HEAD], (((1,), (0,)), ((), ())),
                                                 preferred_element_type=F32)

        _both(j < i, j == i, tile)

        @pl.when(j == nk - 1)
        def _():
            dq_ref[...] = (dq_acc[...] * (HEAD ** -0.5)).astype(dq_ref.dtype)

    q_idx = lambda i, j: (i, 0)
    kv_idx = lambda i, j: (jnp.minimum(j, i), 0)
    return pl.pallas_call(
        body, name=name, grid=(nq, nk),
        in_specs=[pl.BlockSpec((tq, H * AUG), q_idx), pl.BlockSpec((tk, H * AUG), kv_idx), pl.BlockSpec((tk, H * AUG), kv_idx),
                  pl.BlockSpec((tq, W), q_idx), pl.BlockSpec((tq, W), q_idx), pl.BlockSpec((tq, LANES), q_idx)],
        out_specs=pl.BlockSpec((tq, W), q_idx), out_shape=jax.ShapeDtypeStruct((T, W), BF16),
        scratch_shapes=[pltpu.VMEM((tq, W), F32)],
        compiler_params=_params(("parallel", "arbitrary")))(qa, ka, va, do, o, lse)


def _gates_fn(small, a_log_l, dt_bias_l, b_f_l):
    lane = lax.broadcasted_iota(jnp.int32, small.shape, 1)
    beta = _sigmoid(small)
    g = -jnp.exp(a_log_l) * _softplus(small + dt_bias_l)
    lf = _log_sigmoid(small + b_f_l)
    return (jnp.where(lane < LANE_A, beta, jnp.where(lane < LANE_F, g, jnp.where(lane < LANE_F + 8, lf, 0.0))),)


def _gated_norm_fn(o, z, g):
    return (_rms(o, g) * _silu(z),)


def _merge_fn(ya, yb, ym, ga, gb, gm):
    return (_sigmoid(ga) * ya + _sigmoid(gb) * yb + _sigmoid(gm) * ym,)


def _mem_attn_fn(nh, dh, mq, kn, v, gq):
    outs = []
    for h in range(nh):
        hs = slice(h * dh, (h + 1) * dh)
        qn = _rms(mq[:, hs], gq)
        s = NT(qn, kn[:, hs]) * (dh ** -0.5)
        e = jnp.exp(s - jnp.max(s, axis=1, keepdims=True))
        p = e / jnp.sum(e, axis=1, keepdims=True)
        outs.append(NN(p, v[:, hs]))
    return (jnp.concatenate(outs, axis=1),)


def sum_squares(x, *, name, tr=512):
    T, D = x.shape
    tr = min(tr, T)

    def body(x_ref, o_ref):
        @pl.when(pl.program_id(0) == 0)
        def _():
            o_ref[...] = jnp.zeros_like(o_ref)

        v = x_ref[...]
        o_ref[...] += jnp.sum(jnp.sum(v * v, axis=1, keepdims=True), axis=0, keepdims=True)

    return pl.pallas_call(
        body, name=name, grid=(T // tr,), in_specs=[pl.BlockSpec((tr, D), lambda i: (i, 0))],
        out_specs=pl.BlockSpec((1, LANES), lambda i: (0, 0)), out_shape=jax.ShapeDtypeStruct((1, LANES), F32),
        compiler_params=_params(("arbitrary",)))(x)


class Comm:
    def __init__(self, arrays, out_shape, sems, start, forward, finish):
        self.arrays, self.out_shape, self.sems = list(arrays), list(out_shape), list(sems)
        self.start, self.forward, self.finish = start, forward, finish


def _place():
    x, y, c = lax.axis_index("x"), lax.axis_index("y"), lax.axis_index("c")
    chips = [(1 - x, y), (x, 1 - y), (1 - x, 1 - y)]
    return x, y, c, chips


def comm_gather(arrays):
    n = len(arrays)
    lin = lambda px, py, pc: 4 * px + 2 * py + pc

    def copy(ins, outs, sems, a, k, block, to, src=None):
        slot = outs[a].at[lin(*block)]
        return pltpu.make_async_remote_copy(src_ref=slot if src is None else src, dst_ref=slot, send_sem=sems[0].at[a, k],
                                            recv_sem=sems[1].at[a, k], device_id=to, device_id_type=MESH)

    def local(ins, outs, sems, a):
        x, y, c, _ = _place()
        return pltpu.make_async_copy(ins[a], outs[a].at[lin(x, y, c)], sems[2].at[a])

    def start(ins, outs, sems):
        x, y, c, chips = _place()
        for a in range(n):
            local(ins, outs, sems, a).start()
        for a in range(n):
            for j, chip in enumerate(chips):
                copy(ins, outs, sems, a, 1 + j, (x, y, c), (*chip, c), src=ins[a]).start()
            copy(ins, outs, sems, a, 0, (x, y, c), (x, y, 1 - c), src=ins[a]).start()

    def forward(ins, outs, sems):
        x, y, c, chips = _place()
        for a in range(n):
            for j, chip in enumerate(chips):
                copy(ins, outs, sems, a, 1 + j, (*chip, c), (x, y, c)).wait_recv()
                copy(ins, outs, sems, a, 4 + j, (*chip, c), (x, y, 1 - c)).start()

    def finish(ins, outs, sems):
        x, y, c, chips = _place()
        for a in range(n):
            copy(ins, outs, sems, a, 0, (x, y, 1 - c), (x, y, c)).wait_recv()
            for j, chip in enumerate(chips):
                copy(ins, outs, sems, a, 4 + j, (*chip, 1 - c), (x, y, c)).wait_recv()
        for a in range(n):
            for j, chip in enumerate(chips):
                copy(ins, outs, sems, a, 1 + j, (x, y, c), (*chip, c), src=ins[a]).wait_send()
                copy(ins, outs, sems, a, 4 + j, (*chip, c), (x, y, 1 - c)).wait_send()
            copy(ins, outs, sems, a, 0, (x, y, c), (x, y, 1 - c), src=ins[a]).wait_send()
            local(ins, outs, sems, a).wait()

    return Comm(arrays, [jax.ShapeDtypeStruct((N_DEV,) + a.shape, a.dtype) for a in arrays],
                [pltpu.SemaphoreType.DMA((n, 7)), pltpu.SemaphoreType.DMA((n, 7)), pltpu.SemaphoreType.DMA((n,))],
                start, forward, finish)


def comm_direct(arrays, scatter):
    n = len(arrays)

    def peers():
        x, y, c = lax.axis_index("x"), lax.axis_index("y"), lax.axis_index("c")
        out = []
        for r in range(1, N_DEV):
            px, py, pc = (1 - x if r & 4 else x), (1 - y if r & 2 else y), (1 - c if r & 1 else c)
            out.append((r, (px, py, pc), 4 * px + 2 * py + pc))
        return 4 * x + 2 * y + c, out

    def remote(ins, outs, sems, a, r, dev, src_slot, dst_slot):
        return pltpu.make_async_remote_copy(
            src_ref=ins[a].at[src_slot] if scatter[a] else ins[a], dst_ref=outs[a].at[dst_slot],
            send_sem=sems[0].at[a, r - 1], recv_sem=sems[1].at[a, r - 1], device_id=dev, device_id_type=MESH)

    def local(ins, outs, sems, a, me):
        return pltpu.make_async_copy(ins[a].at[me] if scatter[a] else ins[a], outs[a].at[me], sems[2].at[a])

    def start(ins, outs, sems):
        me, ps = peers()
        for a in range(n):
            local(ins, outs, sems, a, me).start()
        for a in range(n):
            for r, dev, lin in ps:
                remote(ins, outs, sems, a, r, dev, lin, me).start()

    def finish(ins, outs, sems):
        me, ps = peers()
        for a in range(n):
            for r, dev, lin in ps:
                remote(ins, outs, sems, a, r, dev, lin, lin).wait_recv()
        for a in range(n):
            for r, dev, lin in ps:
                remote(ins, outs, sems, a, r, dev, lin, me).wait_send()
            local(ins, outs, sems, a, me).wait()

    return Comm(arrays, [jax.ShapeDtypeStruct(a.shape if sc else (N_DEV,) + a.shape, a.dtype) for a, sc in zip(arrays, scatter)],
                [pltpu.SemaphoreType.DMA((n, N_DEV - 1)), pltpu.SemaphoreType.DMA((n, N_DEV - 1)),
                 pltpu.SemaphoreType.DMA((n,))], start, None, finish)


def run_comm(comm, *, name):
    n = len(comm.arrays)

    def body(*refs):
        ins, outs, sems = refs[:n], refs[n:2 * n], refs[2 * n:]
        comm.start(ins, outs, sems)
        if comm.forward is not None:
            comm.forward(ins, outs, sems)
        comm.finish(ins, outs, sems)

    any_spec = pl.BlockSpec(memory_space=pl.ANY)
    return pl.pallas_call(body, name=name, in_specs=[any_spec] * n, out_specs=[any_spec] * n, out_shape=comm.out_shape,
                          scratch_shapes=comm.sems)(*comm.arrays)


def adamw(parts, w, m, v, *, name, tr=128):
    R, Cc = w.shape
    tr = min(tr, R)
    assert R % tr == 0
    n_parts = parts.shape[0]

    def body(p_ref, w_ref, m_ref, v_ref, g_ref, d_ref, nm_ref, nv_ref):
        g = p_ref[0].astype(F32)
        for s in range(1, n_parts):
            g = g + p_ref[s].astype(F32)
        nm = ADAM_B1 * m_ref[...] + (1.0 - ADAM_B1) * g
        nv = ADAM_B2 * v_ref[...] + (1.0 - ADAM_B2) * (g * g)
        m_hat = nm / (1.0 - ADAM_B1 ** ADAM_STEP)
        v_hat = nv / (1.0 - ADAM_B2 ** ADAM_STEP)
        g_ref[...] = g
        d_ref[...] = -ADAM_LR * (m_hat / (jnp.sqrt(v_hat) + ADAM_EPS) + ADAM_WD * w_ref[...])
        nm_ref[...] = nm
        nv_ref[...] = nv

    spec = pl.BlockSpec((tr, Cc), lambda i: (i, 0))
    return pl.pallas_call(
        body, name=name, grid=(R // tr,),
        in_specs=[pl.BlockSpec((n_parts, tr, Cc), lambda i: (0, i, 0)), spec, spec, spec], out_specs=[spec] * 4,
        out_shape=[jax.ShapeDtypeStruct((R, Cc), F32)] * 4, compiler_params=_params(("parallel",)))(parts, w, m, v)


def _lanes(vec, base):
    return jnp.pad(vec[None].astype(F32), ((0, 0), (base, LANES - base - vec.shape[0])))


def _col_shards(full_w):
    R, Ct = full_w.shape
    return jnp.transpose(full_w.reshape(R, N_DEV, Ct // N_DEV), (1, 0, 2))


def _from_col_shards(g):
    return jnp.transpose(g, (1, 0, 2)).reshape(g.shape[1], -1)


def kernel(x, mem, g_mix, w_in, conv_w, a_log, dt_bias, gdn_norm_g, fox_b_f, fox_q_norm, fox_k_norm, g_mem, w_mem_kv, mem_q_norm, mem_k_norm, w_up_gdn, w_up_fox, w_up_mem, w_out, g_mlp, w_ff1, w_ff2, loss_target, m_g_mix, m_w_in, m_conv_w, m_a_log, m_dt_bias, m_gdn_norm_g, m_fox_b_f, m_fox_q_norm, m_fox_k_norm, m_g_mem, m_w_mem_kv, m_mem_q_norm, m_mem_k_norm, m_w_up_gdn, m_w_up_fox, m_w_up_mem, m_w_out, m_g_mlp, m_w_ff1, m_w_ff2, v_g_mix, v_w_in, v_conv_w, v_a_log, v_dt_bias, v_gdn_norm_g, v_fox_b_f, v_fox_q_norm, v_fox_k_norm, v_g_mem, v_w_mem_kv, v_mem_q_norm, v_mem_k_norm, v_w_up_gdn, v_w_up_fox, v_w_up_mem, v_w_out, v_g_mlp, v_w_ff1, v_w_ff2):
    loc = dict(locals())
    big_names = ["w_in", "conv_w", "w_mem_kv", "w_up_gdn", "w_up_fox", "w_up_mem", "w_out", "w_ff1", "w_ff2"]
    col_sharded = {"w_in", "conv_w", "w_up_gdn", "w_up_fox", "w_up_mem", "w_ff1"}
    small_names = ["g_mix", "a_log", "dt_bias", "gdn_norm_g", "fox_b_f", "fox_q_norm", "fox_k_norm", "g_mem",
                   "mem_q_norm", "mem_k_norm", "g_mlp"]

    xs, tgt, mems = x[0], loss_target[0], mem[0]
    T, D = xs.shape
    HG = a_log.shape[1]
    HF = fox_b_f.shape[1]
    DM = mem_q_norm.shape[1]
    GQK, GV = HG * HEAD, HG * HEAD
    GQKV = 2 * GQK + GV
    FW = HF * HEAD
    MW = w_mem_kv.shape[2] // 2
    HM = MW // DM
    assert HG <= 8 and HF <= 8

    shard = {n: loc[n][0].astype(BF16) for n in big_names}
    first, rest, last = big_names[:2], big_names[2:-1], big_names[-1:]
    W = {}

    def take(names, gathered):
        for n, g in zip(names, gathered):
            if n == "w_ff1":
                W[n] = g
            else:
                W[n] = _from_col_shards(g) if n in col_sharded else g.reshape(-1, g.shape[2])

    take(first, run_comm(comm_gather([shard[n] for n in first]), name="gather_in"))
    widths = [GQKV, GV, HG, HG, FW, FW, FW, HF, MW, 3 * D]
    offs = np.concatenate([[0], np.cumsum(widths)]).tolist()
    seg = [W["w_in"][:, offs[i]:offs[i + 1]] for i in range(len(widths))]
    w_big = jnp.concatenate([seg[0], seg[1], seg[4], seg[5], seg[6], seg[8], seg[9]], axis=1)
    pad8 = lambda s: jnp.pad(s, ((0, 0), (0, 8 - s.shape[1])))
    w_small = jnp.concatenate([pad8(seg[2]), pad8(seg[3]), pad8(seg[7]), jnp.zeros((D, LANES - 24), BF16)], axis=1)
    o_z, o_fq, o_fk, o_fv = GQKV, GQKV + GV, GQKV + GV + FW, GQKV + GV + 2 * FW
    o_mq = o_fv + FW
    o_gt = o_mq + MW
    WB = o_gt + 3 * D
    conv_full = W["conv_w"].astype(F32)

    a_log_l, dt_bias_l, b_f_l = _lanes(a_log[0], LANE_A), _lanes(dt_bias[0], LANE_A), _lanes(fox_b_f[0], LANE_F)
    rms_fn = lambda t, g: (_rms(t, g),)

    h, = rowwise(rms_fn, [full(xs)], [g_mix], [(D, D, BF16)], name="rms_mix")
    big, *gathered = matmul(h, w_big, mode="nn", name="proj_big", comm=comm_gather([shard[n] for n in rest]))
    take(rest, gathered)
    small = matmul(h, w_small, mode="nn", name="proj_small", out_dtypes=(F32,))
    bgf, = rowwise(_gates_fn, [full(small)], [a_log_l, dt_bias_l, b_f_l], [(LANES, LANES, F32)], name="gates")

    cn = conv_fwd(big, conv_full, HG, name="conv")
    o_gdn, ss, tinv = gdn_fwd(cn, bgf, HG, name="gdn_fwd")
    oa, = rowwise(_gated_norm_fn, [(o_gdn, HEAD, 0, 1), (big, HEAD, o_z // HEAD, 1)], [gdn_norm_g],
                  [(GV, HEAD, BF16)], name="gated_norm", ncb=HG)

    cum = cumsum_tokens(bgf, reverse=False, name="cumsum")
    fqa = fox_prep(big, o_fq, fox_q_norm, cum, 0, HF, name="fox_prep_q")
    fka = fox_prep(big, o_fk, fox_k_norm, cum, 1, HF, name="fox_prep_k")
    fva = fox_prep(big, o_fv, fox_k_norm, cum, 2, HF, name="fox_prep_v")
    ob, lse = fox_fwd(fqa, fka, fva, HF, name="fox_fwd")

    memn, = rowwise(rms_fn, [full(mems)], [g_mem], [(D, D, BF16)], name="rms_mem")
    kv_m = matmul(memn, W["w_mem_kv"], mode="nn", name="mem_kv", out_dtypes=(F32,))
    kmn, = rowwise(rms_fn, [(kv_m, DM, 0, 1)], [mem_k_norm], [(MW, DM, F32)], name="mem_knorm", ncb=HM)
    vm = kv_m[:, MW:]
    mem_fn = functools.partial(_mem_attn_fn, HM, DM)
    om, = rowwise(mem_fn, [(big, MW, o_mq // MW, 0)], [kmn, vm, mem_q_norm], [(MW, MW, BF16)], name="mem_attn")

    ya = matmul(oa, W["w_up_gdn"], mode="nn", name="up_gdn")
    yb = matmul(ob, W["w_up_fox"], mode="nn", name="up_fox")
    ym = matmul(om, W["w_up_mem"], mode="nn", name="up_mem")
    cbm = min(512, D)
    gate_rows = [(big, cbm, (o_gt + b * D) // cbm, 1) for b in range(3)]
    merge_rows = [(ya, cbm, 0, 1), (yb, cbm, 0, 1), (ym, cbm, 0, 1)] + gate_rows
    y, = rowwise(_merge_fn, merge_rows, [], [(D, cbm, BF16)], name="merge", ncb=D // cbm)
    x1 = matmul(y, W["w_out"], mode="nn", name="out_proj", out_dtypes=(F32,), extras=(xs,),
                epi=lambda r, res: (r + res,))

    h2, = rowwise(rms_fn, [full(x1)], [g_mlp], [(D, D, BF16)], name="rms_mlp")
    u_ff, a_ff, *gathered = matmul(h2, W["w_ff1"], mode="nn", name="ff1", out_dtypes=(BF16, BF16), b_shards=True,
                                   epi=lambda r: (r, jnp.square(jnp.maximum(r, 0.0))),
                                   comm=comm_gather([shard[n] for n in last]))
    take(last, gathered)
    d_out = matmul(a_ff, W["w_ff2"], mode="nn", name="ff2_loss", out_dtypes=(F32,), extras=(x1, tgt),
                   epi=lambda r, res, t: ((r + res - t) * (1.0 / D),))
    loss_local = 0.5 * D * sum_squares(d_out, name="loss_sum")[0, 0]
    loss = lax.psum(loss_local, ("x", "y", "c"))

    G = {}
    d_u = matmul(d_out, W["w_ff2"], mode="nt", name="d_ff2_in", extras=(u_ff,),
                 epi=lambda r, u: (r * 2.0 * jnp.maximum(u.astype(F32), 0.0),))
    d_h2 = matmul(d_u, W["w_ff1"], mode="nt", name="d_ff1_in", b_shards=True)
    d_x1, G["g_mlp"] = rowwise_bwd(rms_fn, [full(x1)], [g_mlp], [full(d_h2)], [(0, F32)], [0], name="d_rms_mlp",
                                   adds=[full(d_out)])
    d_y = matmul(d_x1, W["w_out"], mode="nt", name="d_out_proj_in")
    G["w_out"] = matmul(y, d_x1, mode="tn", name="d_w_out")
    d_ya, d_yb, d_ym, d_ga, d_gb, d_gm = rowwise_bwd(
        _merge_fn, merge_rows, [], [(d_y, cbm, 0, 1)], [(k, BF16) for k in range(6)], [], name="d_merge", ncb=D // cbm)
    d_oa = matmul(d_ya, W["w_up_gdn"], mode="nt", name="d_up_gdn_in")
    d_ob = matmul(d_yb, W["w_up_fox"], mode="nt", name="d_up_fox_in")
    d_om = matmul(d_ym, W["w_up_mem"], mode="nt", name="d_up_mem_in")
    G["w_up_gdn"] = matmul(oa, d_ya, mode="tn", name="d_w_up_gdn")
    G["w_up_fox"] = matmul(ob, d_yb, mode="tn", name="d_w_up_fox")
    G["w_up_mem"] = matmul(om, d_ym, mode="tn", name="d_w_up_mem")

    d_mq, d_kmn, d_vm, G["mem_q_norm"] = rowwise_bwd(
        mem_fn, [(big, MW, o_mq // MW, 0)], [kmn, vm, mem_q_norm], [full(d_om)], [(0, BF16)], [0, 1, 2], name="d_mem_attn")
    d_km, G["mem_k_norm"] = rowwise_bwd(rms_fn, [(kv_m, DM, 0, 1)], [mem_k_norm], [(d_kmn, DM, 0, 1)], [(0, F32)], [0],
                                         name="d_mem_knorm", ncb=HM)
    d_kv_m = jnp.concatenate([d_km, d_vm], axis=1)
    G["w_mem_kv"] = matmul(memn, d_kv_m, mode="tn", name="d_w_mem_kv")
    d_memn = matmul(d_kv_m, W["w_mem_kv"], mode="nt", name="d_mem_kv_in")
    _, G["g_mem"] = rowwise_bwd(rms_fn, [full(mems)], [g_mem], [full(d_memn)], [(0, BF16)], [0], name="d_rms_mem")

    d_fkn, d_fv, d_cum_t = fox_bwd_kv(fqa, fka, fva, d_ob, ob, lse, HF, name="fox_bwd_kv")
    d_fqn = fox_bwd_q(fqa, fka, fva, d_ob, ob, lse, HF, name="fox_bwd_q")
    d_fq, G["fox_q_norm"] = rowwise_bwd(rms_fn, [(big, HEAD, o_fq // HEAD, 1)], [fox_q_norm], [(d_fqn, HEAD, 0, 1)],
                                         [(0, BF16)], [0], name="d_fox_qnorm", ncb=HF)
    d_fk, G["fox_k_norm"] = rowwise_bwd(rms_fn, [(big, HEAD, o_fk // HEAD, 1)], [fox_k_norm], [(d_fkn, HEAD, 0, 1)],
                                         [(0, BF16)], [0], name="d_fox_knorm", ncb=HF)
    d_cum = jnp.pad(d_cum_t[:HF].T, ((0, 0), (LANE_F, LANES - LANE_F - HF)))
    d_logf = cumsum_tokens(d_cum, reverse=True, name="cumsum_rev")

    d_o_gdn, d_z, G["gdn_norm_g"] = rowwise_bwd(
        _gated_norm_fn, [(o_gdn, HEAD, 0, 1), (big, HEAD, o_z // HEAD, 1)], [gdn_norm_g], [(d_oa, HEAD, 0, 1)],
        [(0, F32), (1, BF16)], [0], name="d_gated_norm", ncb=HG)
    d_cn, d_bg = gdn_bwd(cn, bgf, ss, tinv, d_o_gdn, HG, name="gdn_bwd")
    d_conv_y, G["conv_w"] = conv_bwd_taps(big, conv_full, d_cn, HG, name="d_conv_taps")
    d_qkv = conv_bwd_input(d_conv_y, conv_full, name="d_conv_in")
    d_small, d_al, d_dt, d_bf = rowwise_bwd(_gates_fn, [full(small)], [a_log_l, dt_bias_l, b_f_l], [full(d_bg + d_logf)],
                                            [(0, F32)], [0, 1, 2], name="d_gates")
    G["a_log"], G["dt_bias"], G["fox_b_f"] = (d_al[:, LANE_A:LANE_A + HG], d_dt[:, LANE_A:LANE_A + HG],
                                               d_bf[:, LANE_F:LANE_F + HF])

    def parts(n):
        g = G[n].astype(BF16)
        if g.ndim == 3:
            return g
        return _col_shards(g) if n in col_sharded else g.reshape(N_DEV, -1, g.shape[1])

    recv = {}

    def carried(names, out):
        for n, r in zip(names, out):
            recv[n] = r

    d_big = jnp.concatenate([d_qkv, d_z, d_fq, d_fk, d_fv, d_mq, d_ga, d_gb, d_gm], axis=1)
    group = ["conv_w", "w_mem_kv", "w_up_gdn", "w_up_fox", "w_up_mem", "w_out"]
    G["w_ff2"], *out = matmul(a_ff, d_out, mode="tn", name="d_w_ff2",
                              comm=comm_direct([parts(n) for n in group], [True] * len(group)))
    carried(group, out)
    G["w_ff1"], *out = matmul(h2, d_u, mode="tn", name="d_w_ff1", out_shards=True,
                              comm=comm_direct([parts("w_ff2")], [True]))
    carried(["w_ff2"], out)
    g_big, *out = matmul(h, d_big, mode="tn", name="d_w_big", comm=comm_direct([parts("w_ff1")], [True]))
    carried(["w_ff1"], out)
    g_small = matmul(h, d_small, mode="tn", name="d_w_small", out_dtypes=(F32,))
    cols = lambda a, o, wd: a[:, o:o + wd]
    G["w_in"] = jnp.concatenate([
        cols(g_big, 0, GQKV), cols(g_big, o_z, GV), cols(g_small, LANE_B, HG), cols(g_small, LANE_A, HG),
        cols(g_big, o_fq, FW), cols(g_big, o_fk, FW), cols(g_big, o_fv, FW), cols(g_small, LANE_F, HF),
        cols(g_big, o_mq, MW), cols(g_big, o_gt, 3 * D)], axis=1)
    d_h_s = matmul(d_small, w_small, mode="nt", name="d_proj_small_in", out_dtypes=(F32,))
    d_h, *out = matmul(d_big, w_big, mode="nt", name="d_proj_big_in", extras=(d_h_s,), epi=lambda r, e: (r + e,),
                       comm=comm_direct([parts("w_in")], [True]))
    carried(["w_in"], out)
    grad_x, G["g_mix"] = rowwise_bwd(rms_fn, [full(xs)], [g_mix], [full(d_h)], [(0, F32)], [0], name="d_rms_mix",
                                     adds=[full(d_x1)])
    grad_x = grad_x[None]

    small_sizes = [loc[n].shape[1] for n in small_names]
    pack = lambda d: jnp.concatenate([d[n].reshape(1, -1) for n in small_names], axis=1)
    npad = -sum(small_sizes) % LANES
    padp = lambda a: jnp.pad(a, ((0, 0), (0, npad)))
    recv_small, = run_comm(comm_direct([padp(pack(G))], [False]), name="gather_small_grads")

    res = {}
    for n in big_names:
        res[n] = [t[None] for t in adamw(recv[n], loc[n][0], loc["m_" + n][0], loc["v_" + n][0], name="adamw_" + n)]
    sm = adamw(recv_small, padp(pack({n: loc[n] for n in small_names})), padp(pack({n: loc["m_" + n] for n in small_names})),
               padp(pack({n: loc["v_" + n] for n in small_names})), name="adamw_small")
    so = np.concatenate([[0], np.cumsum(small_sizes)]).tolist()
    for i, n in enumerate(small_names):
        res[n] = [t[:, so[i]:so[i + 1]] for t in sm]

    order = ["g_mix", "w_in", "conv_w", "a_log", "dt_bias", "gdn_norm_g", "fox_b_f", "fox_q_norm", "fox_k_norm", "g_mem",
             "w_mem_kv", "mem_q_norm", "mem_k_norm", "w_up_gdn", "w_up_fox", "w_up_mem", "w_out", "g_mlp", "w_ff1", "w_ff2"]
    return (loss, grad_x, *[res[n][0] for n in order], *[res[n][1] for n in order],
            *[res[n][2] for n in order], *[res[n][3] for n in order])
```

```python
import functools

import jax
import jax.numpy as jnp
import numpy as np
from jax import lax
from jax.experimental import pallas as pl
from jax.experimental.pallas import tpu as pltpu

F32 = jnp.float32
BF16 = jnp.bfloat16
HI = lax.Precision.HIGHEST

EPS = 1e-6
GDN_CHUNK = 64
GDN_CONV = 4
HEAD = 128
LANES = 128
HALO = 16
N_DEV = 8
MESH = pl.DeviceIdType.MESH
VMEM_LIMIT_V7X = 56 * 1024 * 1024

ADAM_LR, ADAM_B1, ADAM_B2, ADAM_EPS, ADAM_WD, ADAM_STEP = 0.001, 0.9, 0.999, 1e-08, 0.01, 10

LANE_B, LANE_A, LANE_F = 0, 8, 16


def _params(sem):
    return pltpu.CompilerParams(dimension_semantics=sem, vmem_limit_bytes=VMEM_LIMIT_V7X)


def _dg(a, b, ca, cb, prec):
    nb = a.ndim - 2
    batch = tuple(range(nb))
    return lax.dot_general(a, b, (((ca + nb,), (cb + nb,)), (batch, batch)), precision=prec,
                           preferred_element_type=F32)


def _make_mm(prec, cast):
    def c(x):
        return x.astype(BF16) if cast else x

    @jax.custom_vjp
    def nn(a, b):
        return _dg(c(a), c(b), 1, 0, prec)

    @jax.custom_vjp
    def nt(a, b):
        return _dg(c(a), c(b), 1, 1, prec)

    @jax.custom_vjp
    def tn(a, b):
        return _dg(c(a), c(b), 0, 0, prec)

    nn.defvjp(lambda a, b: (nn(a, b), (a, b)), lambda r, g: (nt(g, r[1]), tn(r[0], g)))
    nt.defvjp(lambda a, b: (nt(a, b), (a, b)), lambda r, g: (nn(g, r[1]), tn(g, r[0])))
    tn.defvjp(lambda a, b: (tn(a, b), (a, b)), lambda r, g: (nt(r[1], g), nn(r[0], g)))
    return nn, nt, tn


NN, NT, TN = _make_mm(None, True)
NNH, NTH, TNH = _make_mm(lax.Precision.HIGH, False)
NNX, _, _ = _make_mm(HI, False)


def _sigmoid(x):
    return 1.0 / (1.0 + jnp.exp(-x))


def _silu(x):
    return x * _sigmoid(x)


def _softplus(x):
    return jnp.maximum(x, 0.0) + jnp.log(1.0 + jnp.exp(-jnp.abs(x)))


def _log_sigmoid(x):
    return -_softplus(-x)


def _rms(x, g):
    return x * lax.rsqrt(jnp.mean(x * x, axis=-1, keepdims=True) + EPS) * g


def _tile(n, target):
    t = target
    while t >= LANES:
        if n % t == 0:
            return t
        t //= 2
    return n


def matmul(a, b, *, mode, name, out_dtypes=(BF16,), epi=None, extras=(), tm=1024, tn=1024, tk=2048, comm=None,
           b_shards=False, out_shards=False):
    if b_shards:
        b_rows, b_cols = b.shape[1], N_DEV * b.shape[2]
    else:
        b_rows, b_cols = b.shape
    if mode == "nn":
        (M, K), (K2, N) = a.shape, (b_rows, b_cols)
    elif mode == "nt":
        (M, K), (N, K2) = a.shape, (b_rows, b_cols)
    else:
        (K, M), (K2, N) = a.shape, (b_rows, b_cols)
    assert K == K2, (name, a.shape, b.shape)
    tm, tn, tk = _tile(M, tm), _tile(N, tn), _tile(K, tk)
    if b_shards and mode == "nt":
        tk = K // N_DEV
    if (b_shards and mode != "nt") or out_shards:
        tn = N // N_DEV
    ni, nj, nk = M // tm, N // tn, K // tk
    a_spec = (pl.BlockSpec((tk, tm), lambda i, j, k: (k, i)) if mode == "tn"
              else pl.BlockSpec((tm, tk), lambda i, j, k: (i, k)))
    if b_shards:
        b_spec = (pl.BlockSpec((None, tn, tk), lambda i, j, k: (k, j, 0)) if mode == "nt"
                  else pl.BlockSpec((None, tk, tn), lambda i, j, k: (j, k, 0)))
    else:
        b_spec = (pl.BlockSpec((tn, tk), lambda i, j, k: (j, k)) if mode == "nt"
                  else pl.BlockSpec((tk, tn), lambda i, j, k: (k, j)))
    o_spec = pl.BlockSpec((tm, tn), lambda i, j, k: (i, j))
    w_spec = pl.BlockSpec((None, tm, tn), lambda i, j, k: (j, i, 0)) if out_shards else o_spec
    w_shape = (N_DEV, M, tn) if out_shards else (M, N)
    dims = {"nn": ((1,), (0,)), "nt": ((1,), (1,)), "tn": ((0,), (0,))}[mode]
    ne, no = len(extras), len(out_dtypes)
    nc = len(comm.arrays) if comm is not None else 0
    n_steps = ni * nj * nk

    def body(a_ref, b_ref, *rest):
        ex, c_in = rest[:ne], rest[ne:ne + nc]
        outs, c_out = rest[ne + nc:ne + nc + no], rest[ne + nc + no:ne + nc + no + nc]
        acc = rest[ne + nc + no + nc]
        c_sems = rest[ne + nc + no + nc + 1:]
        k = pl.program_id(2)
        step = (pl.program_id(0) * nj + pl.program_id(1)) * nk + k

        if comm is not None:
            pl.when(step == 0)(lambda: comm.start(c_in, c_out, c_sems))
            if comm.forward is not None and n_steps >= 3:
                pl.when(step == (3 * n_steps) // 4)(lambda: comm.forward(c_in, c_out, c_sems))

        @pl.when(k == 0)
        def _():
            acc[...] = jnp.zeros_like(acc)

        acc[...] += lax.dot_general(a_ref[...].astype(BF16), b_ref[...].astype(BF16), (dims, ((), ())),
                                    preferred_element_type=F32)

        @pl.when(k == nk - 1)
        def _():
            r = acc[...]
            vals = epi(r, *[e[...] for e in ex]) if epi is not None else (r,)
            for o, v in zip(outs, vals):
                o[...] = v.astype(o.dtype)

        if comm is not None:
            @pl.when(step == n_steps - 1)
            def _():
                if comm.forward is not None and n_steps < 3:
                    comm.forward(c_in, c_out, c_sems)
                comm.finish(c_in, c_out, c_sems)

    any_spec = pl.BlockSpec(memory_space=pl.ANY)
    sem = ("arbitrary",) * 3 if comm is not None else ("parallel", "parallel", "arbitrary")
    out = pl.pallas_call(
        body, name=name, grid=(ni, nj, nk),
        in_specs=[a_spec, b_spec] + [o_spec] * ne + [any_spec] * nc, out_specs=[w_spec] * no + [any_spec] * nc,
        out_shape=[jax.ShapeDtypeStruct(w_shape, d) for d in out_dtypes] + (comm.out_shape if comm is not None else []),
        scratch_shapes=[pltpu.VMEM((tm, tn), F32)] + (comm.sems if comm is not None else []),
        compiler_params=_params(sem))(a, b, *extras, *(comm.arrays if comm is not None else []))
    return out[0] if len(out) == 1 else out


def _row_spec(tr, cb, off, moves):
    return pl.BlockSpec((tr, cb), lambda i, j: (i, off + moves * j))


def _whole_spec(p):
    return pl.BlockSpec(p.shape, lambda i, j: (0,) * p.ndim)


def _row_tile(T, rows):
    widest = max(cb for (_, cb, _, _) in rows)
    return min(T, max(512, (1 << 19) // widest))


def rowwise(fn, rows, params, outs, *, name, ncb=1, tr=None):
    T = rows[0][0].shape[0]
    tr = min(tr, T) if tr else _row_tile(T, rows)
    assert T % tr == 0
    nr, npar = len(rows), len(params)

    def body(*refs):
        r, p, o = refs[:nr], refs[nr:nr + npar], refs[nr + npar:]
        vals = fn(*[x[...].astype(F32) for x in r], *[x[...] for x in p])
        for oref, v in zip(o, vals):
            oref[...] = v.astype(oref.dtype)

    res = pl.pallas_call(
        body, name=name, grid=(T // tr, ncb),
        in_specs=[_row_spec(tr, cb, off, mv) for (_, cb, off, mv) in rows] + [_whole_spec(p) for p in params],
        out_specs=[_row_spec(tr, cb, 0, 1) for (_, cb, _) in outs],
        out_shape=[jax.ShapeDtypeStruct((T, cols), d) for (cols, _, d) in outs],
        compiler_params=_params(("parallel", "parallel")))(*[r[0] for r in rows], *params)
    return res


def rowwise_bwd(fn, rows, params, cots, drows, dparams, *, name, ncb=1, tr=None, adds=()):
    T = rows[0][0].shape[0]
    tr = min(tr, T) if tr else _row_tile(T, rows)
    assert T % tr == 0
    nr, npar, nc, ndr, na = len(rows), len(params), len(cots), len(drows), len(adds)

    def body(*refs):
        r, p, c = refs[:nr], refs[nr:nr + npar], refs[nr + npar:nr + npar + nc]
        base = nr + npar + nc + na
        ad, o_r, o_p = refs[base - na:base], refs[base:base + ndr], refs[base + ndr:]
        prim = [x[...].astype(F32) for x in r] + [x[...] for x in p]
        _, vjp = jax.vjp(lambda *a: tuple(fn(*a)), *prim)
        g = vjp(tuple(x[...].astype(F32) for x in c))
        for k, (oref, (idx, _)) in enumerate(zip(o_r, drows)):
            val = g[idx] + ad[k][...].astype(F32) if k < na else g[idx]
            oref[...] = val.astype(oref.dtype)
        first = jnp.logical_and(pl.program_id(0) == 0, pl.program_id(1) == 0)

        @pl.when(first)
        def _():
            for oref in o_p:
                oref[...] = jnp.zeros_like(oref)

        for oref, idx in zip(o_p, dparams):
            oref[...] += g[nr + idx]

    res = pl.pallas_call(
        body, name=name, grid=(T // tr, ncb),
        in_specs=([_row_spec(tr, cb, off, mv) for (_, cb, off, mv) in rows] + [_whole_spec(p) for p in params]
                  + [_row_spec(tr, cb, off, mv) for (_, cb, off, mv) in tuple(cots) + tuple(adds)]),
        out_specs=([_row_spec(tr, rows[idx][1], 0, 1) for (idx, _) in drows]
                   + [_whole_spec(params[idx]) for idx in dparams]),
        out_shape=([jax.ShapeDtypeStruct((T, ncb * rows[idx][1] if rows[idx][3] else rows[idx][1]), d)
                    for (idx, d) in drows]
                   + [jax.ShapeDtypeStruct(params[idx].shape, F32) for idx in dparams]),
        compiler_params=_params(("arbitrary", "arbitrary")))(
            *[r[0] for r in rows], *params, *[c[0] for c in cots], *[a[0] for a in adds])
    return res


def full(a):
    return (a, a.shape[1], 0, 0)


def cumsum_tokens(x, *, reverse, name, tb=256):
    T = x.shape[0]
    tb = min(tb, T)
    nb = T // tb
    idx = (lambda i: (nb - 1 - i, 0)) if reverse else (lambda i: (i, 0))

    def body(x_ref, o_ref, carry):
        @pl.when(pl.program_id(0) == 0)
        def _():
            carry[...] = jnp.zeros_like(carry)

        ii = lax.broadcasted_iota(jnp.int32, (tb, tb), 0)
        jj = lax.broadcasted_iota(jnp.int32, (tb, tb), 1)
        tri = ((ii <= jj) if reverse else (ii >= jj)).astype(F32)
        c = lax.dot_general(tri, x_ref[...], (((1,), (0,)), ((), ())), precision=HI,
                            preferred_element_type=F32) + carry[0:1, :]
        o_ref[...] = c
        carry[0:1, :] = c[0:1, :] if reverse else c[tb - 1:tb, :]

    return pl.pallas_call(
        body, name=name, grid=(nb,), in_specs=[pl.BlockSpec((tb, LANES), idx)],
        out_specs=pl.BlockSpec((tb, LANES), idx), out_shape=jax.ShapeDtypeStruct((T, LANES), F32),
        scratch_shapes=[pltpu.VMEM((8, LANES), F32)], compiler_params=_params(("arbitrary",)))(x)


def _conv_post(y, kind, dk):
    c = _silu(y)
    r = lax.rsqrt(jnp.sum(c * c, axis=-1, keepdims=True) + EPS)
    return jnp.where(kind == 0, c * r * (dk ** -0.5), jnp.where(kind == 1, c * r, c))


def _conv_taps(cur, prev, w, tr):
    ext = jnp.concatenate([prev, cur], axis=0)
    y = w[3:4, :] * cur
    for d in (1, 2, 3):
        y = y + w[3 - d:4 - d, :] * pltpu.roll(ext, d, 0)[HALO:HALO + tr]
    return y


def conv_fwd(big, w, n_qk_heads, *, name, tr=2048):
    T, W = big.shape[0], w.shape[1]
    tr = min(tr, T)
    nh = W // HEAD

    def body(cur_ref, prev_ref, w_ref, o_ref):
        i, j = pl.program_id(0), pl.program_id(1)
        prev = jnp.where(i > 0, prev_ref[...].astype(F32), 0.0)
        y = _conv_taps(cur_ref[...].astype(F32), prev, w_ref[...], tr)
        kind = jnp.where(j < n_qk_heads, 0, jnp.where(j < 2 * n_qk_heads, 1, 2))
        o_ref[...] = _conv_post(y, kind, HEAD)

    return pl.pallas_call(
        body, name=name, grid=(T // tr, nh),
        in_specs=[pl.BlockSpec((tr, HEAD), lambda i, j: (i, j)),
                  pl.BlockSpec((HALO, HEAD), lambda i, j: (jnp.maximum(i * (tr // HALO) - 1, 0), j)),
                  pl.BlockSpec((GDN_CONV, HEAD), lambda i, j: (0, j))],
        out_specs=pl.BlockSpec((tr, HEAD), lambda i, j: (i, j)),
        out_shape=jax.ShapeDtypeStruct((T, W), F32), compiler_params=_params(("parallel", "parallel")))(big, big, w)


def conv_bwd_taps(big, w, dcn, n_qk_heads, *, name, tr=2048):
    T, W = big.shape[0], w.shape[1]
    tr = min(tr, T)
    nh = W // HEAD

    def body(cur_ref, prev_ref, w_ref, g_ref, dy_ref, dw_ref):
        j, i = pl.program_id(0), pl.program_id(1)
        cur = cur_ref[...].astype(F32)
        prev = jnp.where(i > 0, prev_ref[...].astype(F32), 0.0)
        y = _conv_taps(cur, prev, w_ref[...], tr)
        kind = jnp.where(j < n_qk_heads, 0, jnp.where(j < 2 * n_qk_heads, 1, 2))
        _, vjp = jax.vjp(lambda t: _conv_post(t, kind, HEAD), y)
        dy, = vjp(g_ref[...])
        dy_ref[...] = dy
        ext = jnp.concatenate([prev, cur], axis=0)
        rows = [jnp.sum(dy * (cur if d == 0 else pltpu.roll(ext, d, 0)[HALO:HALO + tr]), axis=0, keepdims=True)
                for d in (3, 2, 1, 0)]

        @pl.when(i == 0)
        def _():
            dw_ref[...] = jnp.zeros_like(dw_ref)

        dw_ref[...] += jnp.concatenate(rows, axis=0)

    return pl.pallas_call(
        body, name=name, grid=(nh, T // tr),
        in_specs=[pl.BlockSpec((tr, HEAD), lambda j, i: (i, j)),
                  pl.BlockSpec((HALO, HEAD), lambda j, i: (jnp.maximum(i * (tr // HALO) - 1, 0), j)),
                  pl.BlockSpec((GDN_CONV, HEAD), lambda j, i: (0, j)),
                  pl.BlockSpec((tr, HEAD), lambda j, i: (i, j))],
        out_specs=[pl.BlockSpec((tr, HEAD), lambda j, i: (i, j)), pl.BlockSpec((GDN_CONV, HEAD), lambda j, i: (0, j))],
        out_shape=[jax.ShapeDtypeStruct((T, W), F32), jax.ShapeDtypeStruct((GDN_CONV, W), F32)],
        compiler_params=_params(("parallel", "arbitrary")))(big, big, w, dcn)


def conv_bwd_input(dy, w, *, name, tr=2048):
    T, W = dy.shape
    tr = min(tr, T)
    nrow = T // tr

    def body(cur_ref, nxt_ref, w_ref, o_ref):
        i = pl.program_id(0)
        cur = cur_ref[...]
        nxt = jnp.where(i < nrow - 1, nxt_ref[...], 0.0)
        ext = jnp.concatenate([cur, nxt], axis=0)
        w = w_ref[...]
        dx = w[3:4, :] * cur
        for d in (1, 2, 3):
            dx = dx + w[3 - d:4 - d, :] * pltpu.roll(ext, tr + HALO - d, 0)[0:tr]
        o_ref[...] = dx.astype(o_ref.dtype)

    return pl.pallas_call(
        body, name=name, grid=(nrow, W // HEAD),
        in_specs=[pl.BlockSpec((tr, HEAD), lambda i, j: (i, j)),
                  pl.BlockSpec((HALO, HEAD), lambda i, j: (jnp.minimum((i + 1) * (tr // HALO), T // HALO - 1), j)),
                  pl.BlockSpec((GDN_CONV, HEAD), lambda i, j: (0, j))],
        out_specs=pl.BlockSpec((tr, HEAD), lambda i, j: (i, j)),
        out_shape=jax.ShapeDtypeStruct((T, W), BF16), compiler_params=_params(("parallel", "parallel")))(dy, dy, w)


INV_BLOCK = 16


def _unit_lower_inverse_value(p):
    C = p.shape[-1]
    ii = lax.broadcasted_iota(jnp.int32, (C, C), 0)
    jj = lax.broadcasted_iota(jnp.int32, (C, C), 1)
    eye = jnp.where((ii == jj)[None], 1.0, 0.0)
    same = ((ii // INV_BLOCK) == (jj // INV_BLOCK))[None]
    pd = jnp.where(same, p, 0.0)
    d_inv = eye + pd
    n = 2
    while n < INV_BLOCK:
        pd = NNH(pd, pd)
        d_inv = d_inv + NNH(d_inv, pd)
        n *= 2
    nb = NNH(d_inv, jnp.where(same, 0.0, p))
    t = eye + nb
    n = 2
    while n < C // INV_BLOCK:
        nb = NNH(nb, nb)
        t = t + NNH(t, nb)
        n *= 2
    return NNH(t, d_inv)


@jax.custom_vjp
def _unit_lower_inverse(p, t_known):
    return _unit_lower_inverse_value(p) if t_known is None else t_known


def _unit_lower_inverse_fwd(p, t_known):
    t = _unit_lower_inverse(p, t_known)
    return t, (t, t_known is not None)


def _unit_lower_inverse_bwd(res, g):
    t, had = res
    return NTH(TNH(t, g), t), (jnp.zeros_like(t) if had else None)


_unit_lower_inverse.defvjp(_unit_lower_inverse_fwd, _unit_lower_inverse_bwd)


def _gdn_chunk(q, k, v, bg, S, t_known=None):
    H, C = q.shape[0], q.shape[1]
    ii = lax.broadcasted_iota(jnp.int32, (C, C), 0)
    jj = lax.broadcasted_iota(jnp.int32, (C, C), 1)
    lincl = (ii >= jj).astype(F32)
    strict, incl, eye = (ii > jj)[None], (ii >= jj)[None], (ii == jj)[None]
    gam2d = lax.dot_general(lincl, bg, (((1,), (0,)), ((), ())), precision=HI, preferred_element_type=F32)
    lane = lax.broadcasted_iota(jnp.int32, (H, 1, LANES), 2)
    hh = lax.broadcasted_iota(jnp.int32, (H, 1, LANES), 0)
    beta = jnp.sum(bg[None] * (lane == hh + LANE_B).astype(F32), axis=2, keepdims=True)
    gam = jnp.sum(gam2d[None] * (lane == hh + LANE_A).astype(F32), axis=2, keepdims=True)
    last = (lax.broadcasted_iota(jnp.int32, (1, C, 1), 1) == C - 1).astype(F32)
    gam_last = jnp.sum(gam * last, axis=1, keepdims=True)
    gam_row = NNX(jnp.ones((H, C, C), F32), jnp.where(eye, gam, 0.0))
    diff = gam - gam_row
    dec_s = jnp.where(strict, jnp.exp(jnp.where(strict, diff, 0.0)), 0.0)
    dec_i = jnp.where(incl, jnp.exp(jnp.where(incl, diff, 0.0)), 0.0)
    t = _unit_lower_inverse(-(beta * NT(k, k) * dec_s), t_known)
    eg = jnp.exp(gam)
    w = NNH(t, beta * eg * k)
    u0 = NNH(t, beta * v)
    qk = NT(q, k) * dec_i
    u = u0 - NN(w, S)
    o = NN(q * eg, S) + NN(qk, u)
    S2 = jnp.exp(gam_last) * S + TN(k * jnp.exp(gam_last - gam), u)
    return o, S2, t


def _heads(x, base, H):
    return jnp.stack([x[:, base + h * HEAD:base + (h + 1) * HEAD] for h in range(H)])


def _unheads(x):
    return jnp.concatenate([x[h] for h in range(x.shape[0])], axis=1)


def gdn_fwd(cn, bgf, H, *, name):
    T, C, W = cn.shape[0], GDN_CHUNK, H * HEAD
    N = T // C

    def body(cn_ref, bg_ref, o_ref, ss_ref, t_ref, s_scr):
        @pl.when(pl.program_id(0) == 0)
        def _():
            s_scr[...] = jnp.zeros_like(s_scr)

        x, S = cn_ref[...], s_scr[...]
        ss_ref[0] = S
        o, S2, t = _gdn_chunk(_heads(x, 0, H), _heads(x, W, H), _heads(x, 2 * W, H), bg_ref[...], S)
        o_ref[...] = _unheads(o)
        t_ref[0] = t
        s_scr[...] = S2

    return pl.pallas_call(
        body, name=name, grid=(N,),
        in_specs=[pl.BlockSpec((C, 3 * W), lambda n: (n, 0)), pl.BlockSpec((C, LANES), lambda n: (n, 0))],
        out_specs=[pl.BlockSpec((C, W), lambda n: (n, 0)), pl.BlockSpec((1, H, HEAD, HEAD), lambda n: (n, 0, 0, 0)),
                   pl.BlockSpec((1, H, C, C), lambda n: (n, 0, 0, 0))],
        out_shape=[jax.ShapeDtypeStruct((T, W), F32), jax.ShapeDtypeStruct((N, H, HEAD, HEAD), F32),
                   jax.ShapeDtypeStruct((N, H, C, C), F32)],
        scratch_shapes=[pltpu.VMEM((H, HEAD, HEAD), F32)], compiler_params=_params(("arbitrary",)))(cn, bgf)


def gdn_bwd(cn, bgf, ss, tinv, do, H, *, name):
    T, C, W = cn.shape[0], GDN_CHUNK, H * HEAD
    N = T // C

    def body(cn_ref, bg_ref, ss_ref, t_ref, do_ref, dcn_ref, dbg_ref, ds_scr):
        @pl.when(pl.program_id(0) == 0)
        def _():
            ds_scr[...] = jnp.zeros_like(ds_scr)

        x = cn_ref[...]
        t_known = t_ref[0]
        _, vjp = jax.vjp(lambda *a: _gdn_chunk(*a, t_known)[:2],
                         _heads(x, 0, H), _heads(x, W, H), _heads(x, 2 * W, H), bg_ref[...], ss_ref[0])
        dq, dk, dv, dbg, dS = vjp((_heads(do_ref[...], 0, H), ds_scr[...]))
        dcn_ref[...] = jnp.concatenate([_unheads(dq), _unheads(dk), _unheads(dv)], axis=1)
        dbg_ref[...] = dbg
        ds_scr[...] = dS

    rev = lambda n: (N - 1 - n, 0)
    return pl.pallas_call(
        body, name=name, grid=(N,),
        in_specs=[pl.BlockSpec((C, 3 * W), rev), pl.BlockSpec((C, LANES), rev),
                  pl.BlockSpec((1, H, HEAD, HEAD), lambda n: (N - 1 - n, 0, 0, 0)),
                  pl.BlockSpec((1, H, C, C), lambda n: (N - 1 - n, 0, 0, 0)), pl.BlockSpec((C, W), rev)],
        out_specs=[pl.BlockSpec((C, 3 * W), rev), pl.BlockSpec((C, LANES), rev)],
        out_shape=[jax.ShapeDtypeStruct((T, 3 * W), F32), jax.ShapeDtypeStruct((T, LANES), F32)],
        scratch_shapes=[pltpu.VMEM((H, HEAD, HEAD), F32)], compiler_params=_params(("arbitrary",)))(cn, bgf, ss, tinv, do)


AUG = 2 * HEAD


def fox_prep(big, col_off, gain, cum, kind, H, *, name, tr=2048):
    T = big.shape[0]
    tr = min(tr, T)

    def body(x_ref, g_ref, c_ref, o_ref):
        h = pl.program_id(1)
        x = x_ref[...].astype(F32)
        lane = lax.broadcasted_iota(jnp.int32, (tr, HEAD), 1)
        if kind == 2:
            main, aug = x, jnp.ones((tr, HEAD), F32)
        else:
            main = _rms(x, g_ref[...]) * ((HEAD ** -0.5) if kind == 0 else 1.0)
            c = jnp.sum(jnp.where(lane == LANE_F + h, c_ref[...], 0.0), axis=1, keepdims=True)
            hi = c.astype(BF16).astype(F32)
            mid = (c - hi).astype(BF16).astype(F32)
            lo = c - hi - mid
            if kind == 0:
                aug = jnp.where(lane == 0, hi, jnp.where(lane == 1, mid, jnp.where(lane == 2, lo,
                                                                                   jnp.where(lane < 6, 1.0, 0.0))))
            else:
                aug = jnp.where(lane < 3, 1.0, jnp.where(lane == 3, -hi, jnp.where(lane == 4, -mid,
                                                                                   jnp.where(lane == 5, -lo, 0.0))))
        o_ref[...] = jnp.concatenate([main, aug], axis=1).astype(o_ref.dtype)

    return pl.pallas_call(
        body, name=name, grid=(T // tr, H),
        in_specs=[pl.BlockSpec((tr, HEAD), lambda i, h: (i, col_off // HEAD + h)),
                  pl.BlockSpec((1, HEAD), lambda i, h: (0, 0)), pl.BlockSpec((tr, LANES), lambda i, h: (i, 0))],
        out_specs=pl.BlockSpec((tr, AUG), lambda i, h: (i, h)),
        out_shape=jax.ShapeDtypeStruct((T, H * AUG), BF16), compiler_params=_params(("parallel", "parallel")))(big, gain, cum)


def _fox_logits(q_ref, k_ref, h, tq, tk, diagonal):
    ha = slice(h * AUG, (h + 1) * AUG)
    s = lax.dot_general(q_ref[:, ha], k_ref[:, ha], (((1,), (1,)), ((), ())), preferred_element_type=F32)
    keep = None
    if diagonal:
        keep = lax.broadcasted_iota(jnp.int32, (tq, tk), 0) >= lax.broadcasted_iota(jnp.int32, (tq, tk), 1)
    return s, keep


def _dispatch(live_ref, H, i, j, below, on_diagonal, head_fn, after=None):
    def straight(diagonal):
        for h in range(H):
            head_fn(h, diagonal)
        if after is not None:
            after()

    def by_head():
        for h in range(H):
            pl.when(live_ref[h, i, j] != 0)(functools.partial(head_fn, h, False))
        if after is not None:
            after()

    pl.when(jnp.logical_and(below, live_ref[H, i, j] == 2))(functools.partial(straight, False))
    pl.when(jnp.logical_and(below, live_ref[H, i, j] == 1))(by_head)
    pl.when(on_diagonal)(functools.partial(straight, True))


FOX_TILE = 512
EXP_UNDERFLOW = -100.0


def fox_live_tiles(cum, q_gain, k_gain, H, T):
    t = min(FOX_TILE, T)
    n = T // t
    c = cum[:, LANE_F:LANE_F + H]
    bias = c[0::t][:, None, :] - c[t - 1::t][None, :, :]
    bound = 1.02 * (HEAD ** 0.5) * jnp.max(jnp.abs(q_gain)) * jnp.max(jnp.abs(k_gain))
    causal = (jnp.arange(n)[:, None] >= jnp.arange(n)[None, :])[:, :, None]
    live = jnp.logical_and(causal, 2.0 * bound + bias >= EXP_UNDERFLOW)
    status = jnp.where(jnp.all(live, axis=2), 2, jnp.where(jnp.any(live, axis=2), 1, 0))
    return jnp.concatenate([jnp.transpose(live, (2, 0, 1)).astype(jnp.int32), status[None].astype(jnp.int32)], axis=0)


def fox_fwd(qa, ka, va, live, H, *, name):
    T = qa.shape[0]
    tq = tk = min(FOX_TILE, T)
    nq, nk = T // tq, T // tk

    def body(live_ref, q_ref, k_ref, v_ref, o_ref, lse_ref, acc, m_scr):
        i, j = pl.program_id(0), pl.program_id(1)

        @pl.when(j == 0)
        def _():
            m_scr[...] = jnp.full_like(m_scr, -jnp.inf)
            acc[...] = jnp.zeros_like(acc)

        def head(h, diagonal):
            ha = slice(h * AUG, (h + 1) * AUG)
            s, keep = _fox_logits(q_ref, k_ref, h, tq, tk, diagonal)
            if diagonal:
                s = jnp.where(keep, s, -jnp.inf)
            m_prev = m_scr[h]
            m_new = jnp.maximum(m_prev, jnp.max(s, axis=1, keepdims=True))
            p = jnp.exp(s - m_new[:, 0:1])
            p_hi = p.astype(BF16)
            p_lo = (p - p_hi.astype(F32)).astype(BF16)
            pv = lambda t: lax.dot_general(t, v_ref[:, ha], (((1,), (0,)), ((), ())), preferred_element_type=F32)
            acc[:, ha] = jnp.exp(m_prev[:, 0:1] - m_new[:, 0:1]) * acc[:, ha] + (pv(p_hi) + pv(p_lo))
            m_scr[h] = m_new

        _dispatch(live_ref, H, i, j, j < i, j == i, head)

        @pl.when(j == nk - 1)
        def _():
            lane = lax.broadcasted_iota(jnp.int32, (tq, LANES), 1)
            lse = jnp.zeros((tq, LANES), F32)
            for h in range(H):
                den = acc[:, h * AUG + HEAD:(h + 1) * AUG]
                o_ref[:, h * HEAD:(h + 1) * HEAD] = acc[:, h * AUG:h * AUG + HEAD] / den
                lse = jnp.where(lane == h, m_scr[h] + jnp.log(den), lse)
            lse_ref[...] = lse

    kv_idx = lambda i, j, f: (jnp.minimum(j, i), 0)
    q_idx = lambda i, j, f: (i, 0)
    return pl.pallas_call(
        body, name=name,
        grid_spec=pltpu.PrefetchScalarGridSpec(
            num_scalar_prefetch=1, grid=(nq, nk),
            in_specs=[pl.BlockSpec((tq, H * AUG), q_idx), pl.BlockSpec((tk, H * AUG), kv_idx), pl.BlockSpec((tk, H * AUG), kv_idx)],
            out_specs=[pl.BlockSpec((tq, H * HEAD), q_idx), pl.BlockSpec((tq, LANES), q_idx)],
            scratch_shapes=[pltpu.VMEM((tq, H * AUG), F32), pltpu.VMEM((H, tq, LANES), F32)]),
        out_shape=[jax.ShapeDtypeStruct((T, H * HEAD), F32), jax.ShapeDtypeStruct((T, LANES), F32)],
        compiler_params=_params(("parallel", "arbitrary")))(live, qa, ka, va)


def _fox_ds(q_ref, k_ref, v_ref, do_ref, o_ref, lse_ref, h, tq, tk, diagonal):
    hs = slice(h * HEAD, (h + 1) * HEAD)
    s, keep = _fox_logits(q_ref, k_ref, h, tq, tk, diagonal)
    p = jnp.exp(s - lse_ref[:, h:h + 1])
    if diagonal:
        p = jnp.where(keep, p, 0.0)
    do = do_ref[:, hs]
    dp = lax.dot_general(do, v_ref[:, h * AUG:h * AUG + HEAD], (((1,), (1,)), ((), ())), preferred_element_type=F32)
    delta = jnp.sum(do.astype(F32) * o_ref[:, hs], axis=1, keepdims=True)
    return p, p * (dp - delta)


def fox_bwd_kv(qa, ka, va, do, o, lse, live, H, *, name):
    T, W = do.shape
    tq = tk = min(FOX_TILE, T)
    nq, nk = T // tq, T // tk

    def body(live_ref, q_ref, k_ref, v_ref, do_ref, o_ref, lse_ref, dk_ref, dv_ref, dc_ref, dk_acc, dv_acc, dc_acc):
        j, i = pl.program_id(0), pl.program_id(1)

        @pl.when(i == 0)
        def _():
            dk_acc[...] = jnp.zeros_like(dk_acc)
            dv_acc[...] = jnp.zeros_like(dv_acc)
            dc_acc[...] = jnp.zeros_like(dc_acc)

        def head(h, diagonal):
            hs = slice(h * HEAD, (h + 1) * HEAD)
            p, ds = _fox_ds(q_ref, k_ref, v_ref, do_ref, o_ref, lse_ref, h, tq, tk, diagonal)
            dv_acc[:, hs] += lax.dot_general(p.astype(BF16), do_ref[:, hs], (((0,), (0,)), ((), ())),
                                             preferred_element_type=F32)
            dk_acc[:, hs] += lax.dot_general(ds.astype(BF16), q_ref[:, h * AUG:h * AUG + HEAD],
                                             (((0,), (0,)), ((), ())), preferred_element_type=F32)
            dc_acc[h] -= jnp.broadcast_to(jnp.sum(ds, axis=0, keepdims=True), (8, tk))

        _dispatch(live_ref, H, i, j, i > j, i == j, head)

        @pl.when(i == nq - 1)
        def _():
            dk_ref[...] = dk_acc[...].astype(dk_ref.dtype)
            dv_ref[...] = dv_acc[...].astype(dv_ref.dtype)
            row = lax.broadcasted_iota(jnp.int32, (8, tk), 0)
            dc = jnp.zeros((8, tk), F32)
            for h in range(H):
                dc = jnp.where(row == h, dc_acc[h], dc)
            dc_ref[...] = dc

    q_idx = lambda j, i, f: (jnp.maximum(i, j), 0)
    kv_idx = lambda j, i, f: (j, 0)
    return pl.pallas_call(
        body, name=name,
        grid_spec=pltpu.PrefetchScalarGridSpec(
            num_scalar_prefetch=1, grid=(nk, nq),
            in_specs=[pl.BlockSpec((tq, H * AUG), q_idx), pl.BlockSpec((tk, H * AUG), kv_idx), pl.BlockSpec((tk, H * AUG), kv_idx),
                      pl.BlockSpec((tq, W), q_idx), pl.BlockSpec((tq, W), q_idx), pl.BlockSpec((tq, LANES), q_idx)],
            out_specs=[pl.BlockSpec((tk, W), kv_idx), pl.BlockSpec((tk, W), kv_idx),
                       pl.BlockSpec((8, tk), lambda j, i, f: (0, j))],
            scratch_shapes=[pltpu.VMEM((tk, W), F32), pltpu.VMEM((tk, W), F32), pltpu.VMEM((H, 8, tk), F32)]),
        out_shape=[jax.ShapeDtypeStruct((T, W), BF16), jax.ShapeDtypeStruct((T, W), BF16), jax.ShapeDtypeStruct((8, T), F32)],
        compiler_params=_params(("parallel", "arbitrary")))(live, qa, ka, va, do, o, lse)


def fox_bwd_q(qa, ka, va, do, o, lse, live, H, *, name):
    T, W = do.shape
    tq = tk = min(FOX_TILE, T)
    nq, nk = T // tq, T // tk

    def body(live_ref, q_ref, k_ref, v_ref, do_ref, o_ref, lse_ref, dq_ref, dq_acc):
        i, j = pl.program_id(0), pl.program_id(1)

        @pl.when(j == 0)
        def _():
            dq_acc[...] = jnp.zeros_like(dq_acc)

        def head(h, diagonal):
            hs = slice(h * HEAD, (h + 1) * HEAD)
            _, ds = _fox_ds(q_ref, k_ref, v_ref, do_ref, o_ref, lse_ref, h, tq, tk, diagonal)
            dq_acc[:, hs] += lax.dot_general(ds.astype(BF16), k_ref[:, h * AUG:h * AUG + HEAD],
                                             (((1,), (0,)), ((), ())), preferred_element_type=F32)

        _dispatch(live_ref, H, i, j, j < i, j == i, head)

        @pl.when(j == nk - 1)
        def _():
            dq_ref[...] = (dq_acc[...] * (HEAD ** -0.5)).astype(dq_ref.dtype)

    q_idx = lambda i, j, f: (i, 0)
    kv_idx = lambda i, j, f: (jnp.minimum(j, i), 0)
    return pl.pallas_call(
        body, name=name,
        grid_spec=pltpu.PrefetchScalarGridSpec(
            num_scalar_prefetch=1, grid=(nq, nk),
            in_specs=[pl.BlockSpec((tq, H * AUG), q_idx), pl.BlockSpec((tk, H * AUG), kv_idx), pl.BlockSpec((tk, H * AUG), kv_idx),
                      pl.BlockSpec((tq, W), q_idx), pl.BlockSpec((tq, W), q_idx), pl.BlockSpec((tq, LANES), q_idx)],
            out_specs=pl.BlockSpec((tq, W), q_idx),
            scratch_shapes=[pltpu.VMEM((tq, W), F32)]),
        out_shape=jax.ShapeDtypeStruct((T, W), BF16),
        compiler_params=_params(("parallel", "arbitrary")))(live, qa, ka, va, do, o, lse)


def _gates_fn(small, a_log_l, dt_bias_l, b_f_l):
    lane = lax.broadcasted_iota(jnp.int32, small.shape, 1)
    beta = _sigmoid(small)
    g = -jnp.exp(a_log_l) * _softplus(small + dt_bias_l)
    lf = _log_sigmoid(small + b_f_l)
    return (jnp.where(lane < LANE_A, beta, jnp.where(lane < LANE_F, g, jnp.where(lane < LANE_F + 8, lf, 0.0))),)


def _gated_norm_fn(o, z, g):
    return (_rms(o, g) * _silu(z),)


def _merge_fn(ya, yb, ym, ga, gb, gm):
    return (_sigmoid(ga) * ya + _sigmoid(gb) * yb + _sigmoid(gm) * ym,)


def _mem_attn_fn(nh, dh, mq, kn, v, gq):
    outs = []
    for h in range(nh):
        hs = slice(h * dh, (h + 1) * dh)
        qn = _rms(mq[:, hs], gq)
        s = NT(qn, kn[:, hs]) * (dh ** -0.5)
        e = jnp.exp(s - jnp.max(s, axis=1, keepdims=True))
        p = e / jnp.sum(e, axis=1, keepdims=True)
        outs.append(NN(p, v[:, hs]))
    return (jnp.concatenate(outs, axis=1),)


def sum_squares(x, *, name, tr=512):
    T, D = x.shape
    tr = min(tr, T)

    def body(x_ref, o_ref):
        @pl.when(pl.program_id(0) == 0)
        def _():
            o_ref[...] = jnp.zeros_like(o_ref)

        v = x_ref[...]
        o_ref[...] += jnp.sum(jnp.sum(v * v, axis=1, keepdims=True), axis=0, keepdims=True)

    return pl.pallas_call(
        body, name=name, grid=(T // tr,), in_specs=[pl.BlockSpec((tr, D), lambda i: (i, 0))],
        out_specs=pl.BlockSpec((1, LANES), lambda i: (0, 0)), out_shape=jax.ShapeDtypeStruct((1, LANES), F32),
        compiler_params=_params(("arbitrary",)))(x)


class Comm:
    def __init__(self, arrays, out_shape, sems, start, forward, finish):
        self.arrays, self.out_shape, self.sems = list(arrays), list(out_shape), list(sems)
        self.start, self.forward, self.finish = start, forward, finish


def _place():
    x, y, c = lax.axis_index("x"), lax.axis_index("y"), lax.axis_index("c")
    chips = [(1 - x, y), (x, 1 - y), (1 - x, 1 - y)]
    return x, y, c, chips


def comm_gather(arrays):
    n = len(arrays)
    lin = lambda px, py, pc: 4 * px + 2 * py + pc

    def copy(ins, outs, sems, a, k, block, to, src=None):
        slot = outs[a].at[lin(*block)]
        return pltpu.make_async_remote_copy(src_ref=slot if src is None else src, dst_ref=slot, send_sem=sems[0].at[a, k],
                                            recv_sem=sems[1].at[a, k], device_id=to, device_id_type=MESH)

    def local(ins, outs, sems, a):
        x, y, c, _ = _place()
        return pltpu.make_async_copy(ins[a], outs[a].at[lin(x, y, c)], sems[2].at[a])

    def start(ins, outs, sems):
        x, y, c, chips = _place()
        for a in range(n):
            local(ins, outs, sems, a).start()
        for a in range(n):
            for j, chip in enumerate(chips):
                copy(ins, outs, sems, a, 1 + j, (x, y, c), (*chip, c), src=ins[a]).start()
            copy(ins, outs, sems, a, 0, (x, y, c), (x, y, 1 - c), src=ins[a]).start()

    def forward(ins, outs, sems):
        x, y, c, chips = _place()
        for a in range(n):
            for j, chip in enumerate(chips):
                copy(ins, outs, sems, a, 1 + j, (*chip, c), (x, y, c)).wait_recv()
                copy(ins, outs, sems, a, 4 + j, (*chip, c), (x, y, 1 - c)).start()

    def finish(ins, outs, sems):
        x, y, c, chips = _place()
        for a in range(n):
            copy(ins, outs, sems, a, 0, (x, y, 1 - c), (x, y, c)).wait_recv()
            for j, chip in enumerate(chips):
                copy(ins, outs, sems, a, 4 + j, (*chip, 1 - c), (x, y, c)).wait_recv()
        for a in range(n):
            for j, chip in enumerate(chips):
                copy(ins, outs, sems, a, 1 + j, (x, y, c), (*chip, c), src=ins[a]).wait_send()
                copy(ins, outs, sems, a, 4 + j, (*chip, c), (x, y, 1 - c)).wait_send()
            copy(ins, outs, sems, a, 0, (x, y, c), (x, y, 1 - c), src=ins[a]).wait_send()
            local(ins, outs, sems, a).wait()

    return Comm(arrays, [jax.ShapeDtypeStruct((N_DEV,) + a.shape, a.dtype) for a in arrays],
                [pltpu.SemaphoreType.DMA((n, 7)), pltpu.SemaphoreType.DMA((n, 7)), pltpu.SemaphoreType.DMA((n,))],
                start, forward, finish)


def comm_direct(arrays, scatter):
    n = len(arrays)

    def peers():
        x, y, c = lax.axis_index("x"), lax.axis_index("y"), lax.axis_index("c")
        out = []
        for r in range(1, N_DEV):
            px, py, pc = (1 - x if r & 4 else x), (1 - y if r & 2 else y), (1 - c if r & 1 else c)
            out.append((r, (px, py, pc), 4 * px + 2 * py + pc))
        return 4 * x + 2 * y + c, out

    def remote(ins, outs, sems, a, r, dev, src_slot, dst_slot):
        return pltpu.make_async_remote_copy(
            src_ref=ins[a].at[src_slot] if scatter[a] else ins[a], dst_ref=outs[a].at[dst_slot],
            send_sem=sems[0].at[a, r - 1], recv_sem=sems[1].at[a, r - 1], device_id=dev, device_id_type=MESH)

    def local(ins, outs, sems, a, me):
        return pltpu.make_async_copy(ins[a].at[me] if scatter[a] else ins[a], outs[a].at[me], sems[2].at[a])

    def start(ins, outs, sems):
        me, ps = peers()
        for a in range(n):
            local(ins, outs, sems, a, me).start()
        for a in range(n):
            for r, dev, lin in ps:
                remote(ins, outs, sems, a, r, dev, lin, me).start()

    def finish(ins, outs, sems):
        me, ps = peers()
        for a in range(n):
            for r, dev, lin in ps:
                remote(ins, outs, sems, a, r, dev, lin, lin).wait_recv()
        for a in range(n):
            for r, dev, lin in ps:
                remote(ins, outs, sems, a, r, dev, lin, me).wait_send()
            local(ins, outs, sems, a, me).wait()

    return Comm(arrays, [jax.ShapeDtypeStruct(a.shape if sc else (N_DEV,) + a.shape, a.dtype) for a, sc in zip(arrays, scatter)],
                [pltpu.SemaphoreType.DMA((n, N_DEV - 1)), pltpu.SemaphoreType.DMA((n, N_DEV - 1)),
                 pltpu.SemaphoreType.DMA((n,))], start, None, finish)


def run_comm(comm, *, name):
    n = len(comm.arrays)

    def body(*refs):
        ins, outs, sems = refs[:n], refs[n:2 * n], refs[2 * n:]
        comm.start(ins, outs, sems)
        if comm.forward is not None:
            comm.forward(ins, outs, sems)
        comm.finish(ins, outs, sems)

    any_spec = pl.BlockSpec(memory_space=pl.ANY)
    return pl.pallas_call(body, name=name, in_specs=[any_spec] * n, out_specs=[any_spec] * n, out_shape=comm.out_shape,
                          scratch_shapes=comm.sems)(*comm.arrays)


def adamw(parts, w, m, v, *, name, tr=128):
    R, Cc = w.shape
    tr = min(tr, R)
    assert R % tr == 0
    n_parts = parts.shape[0]

    def body(p_ref, w_ref, m_ref, v_ref, g_ref, d_ref, nm_ref, nv_ref):
        g = p_ref[0].astype(F32)
        for s in range(1, n_parts):
            g = g + p_ref[s].astype(F32)
        nm = ADAM_B1 * m_ref[...] + (1.0 - ADAM_B1) * g
        nv = ADAM_B2 * v_ref[...] + (1.0 - ADAM_B2) * (g * g)
        m_hat = nm / (1.0 - ADAM_B1 ** ADAM_STEP)
        v_hat = nv / (1.0 - ADAM_B2 ** ADAM_STEP)
        g_ref[...] = g
        d_ref[...] = -ADAM_LR * (m_hat / (jnp.sqrt(v_hat) + ADAM_EPS) + ADAM_WD * w_ref[...])
        nm_ref[...] = nm
        nv_ref[...] = nv

    spec = pl.BlockSpec((tr, Cc), lambda i: (i, 0))
    return pl.pallas_call(
        body, name=name, grid=(R // tr,),
        in_specs=[pl.BlockSpec((n_parts, tr, Cc), lambda i: (0, i, 0)), spec, spec, spec], out_specs=[spec] * 4,
        out_shape=[jax.ShapeDtypeStruct((R, Cc), F32)] * 4, compiler_params=_params(("parallel",)))(parts, w, m, v)


def _lanes(vec, base):
    return jnp.pad(vec[None].astype(F32), ((0, 0), (base, LANES - base - vec.shape[0])))


def _col_shards(full_w):
    R, Ct = full_w.shape
    return jnp.transpose(full_w.reshape(R, N_DEV, Ct // N_DEV), (1, 0, 2))


def _from_col_shards(g):
    return jnp.transpose(g, (1, 0, 2)).reshape(g.shape[1], -1)


def kernel(x, mem, g_mix, w_in, conv_w, a_log, dt_bias, gdn_norm_g, fox_b_f, fox_q_norm, fox_k_norm, g_mem, w_mem_kv, mem_q_norm, mem_k_norm, w_up_gdn, w_up_fox, w_up_mem, w_out, g_mlp, w_ff1, w_ff2, loss_target, m_g_mix, m_w_in, m_conv_w, m_a_log, m_dt_bias, m_gdn_norm_g, m_fox_b_f, m_fox_q_norm, m_fox_k_norm, m_g_mem, m_w_mem_kv, m_mem_q_norm, m_mem_k_norm, m_w_up_gdn, m_w_up_fox, m_w_up_mem, m_w_out, m_g_mlp, m_w_ff1, m_w_ff2, v_g_mix, v_w_in, v_conv_w, v_a_log, v_dt_bias, v_gdn_norm_g, v_fox_b_f, v_fox_q_norm, v_fox_k_norm, v_g_mem, v_w_mem_kv, v_mem_q_norm, v_mem_k_norm, v_w_up_gdn, v_w_up_fox, v_w_up_mem, v_w_out, v_g_mlp, v_w_ff1, v_w_ff2):
    loc = dict(locals())
    big_names = ["w_in", "conv_w", "w_mem_kv", "w_up_gdn", "w_up_fox", "w_up_mem", "w_out", "w_ff1", "w_ff2"]
    col_sharded = {"w_in", "conv_w", "w_up_gdn", "w_up_fox", "w_up_mem", "w_ff1"}
    small_names = ["g_mix", "a_log", "dt_bias", "gdn_norm_g", "fox_b_f", "fox_q_norm", "fox_k_norm", "g_mem",
                   "mem_q_norm", "mem_k_norm", "g_mlp"]

    xs, tgt, mems = x[0], loss_target[0], mem[0]
    T, D = xs.shape
    HG = a_log.shape[1]
    HF = fox_b_f.shape[1]
    DM = mem_q_norm.shape[1]
    GQK, GV = HG * HEAD, HG * HEAD
    GQKV = 2 * GQK + GV
    FW = HF * HEAD
    MW = w_mem_kv.shape[2] // 2
    HM = MW // DM
    assert HG <= 8 and HF <= 8

    shard = {n: loc[n][0].astype(BF16) for n in big_names}
    first, rest, last = big_names[:2], big_names[2:-1], big_names[-1:]
    W = {}

    def take(names, gathered):
        for n, g in zip(names, gathered):
            if n == "w_ff1":
                W[n] = g
            else:
                W[n] = _from_col_shards(g) if n in col_sharded else g.reshape(-1, g.shape[2])

    take(first, run_comm(comm_gather([shard[n] for n in first]), name="gather_in"))
    widths = [GQKV, GV, HG, HG, FW, FW, FW, HF, MW, 3 * D]
    offs = np.concatenate([[0], np.cumsum(widths)]).tolist()
    seg = [W["w_in"][:, offs[i]:offs[i + 1]] for i in range(len(widths))]
    w_big = jnp.concatenate([seg[0], seg[1], seg[4], seg[5], seg[6], seg[8], seg[9]], axis=1)
    pad8 = lambda s: jnp.pad(s, ((0, 0), (0, 8 - s.shape[1])))
    w_small = jnp.concatenate([pad8(seg[2]), pad8(seg[3]), pad8(seg[7]), jnp.zeros((D, LANES - 24), BF16)], axis=1)
    o_z, o_fq, o_fk, o_fv = GQKV, GQKV + GV, GQKV + GV + FW, GQKV + GV + 2 * FW
    o_mq = o_fv + FW
    o_gt = o_mq + MW
    WB = o_gt + 3 * D
    conv_full = W["conv_w"].astype(F32)

    a_log_l, dt_bias_l, b_f_l = _lanes(a_log[0], LANE_A), _lanes(dt_bias[0], LANE_A), _lanes(fox_b_f[0], LANE_F)
    rms_fn = lambda t, g: (_rms(t, g),)

    h, = rowwise(rms_fn, [full(xs)], [g_mix], [(D, D, BF16)], name="rms_mix")
    big, *gathered = matmul(h, w_big, mode="nn", name="proj_big", comm=comm_gather([shard[n] for n in rest]))
    take(rest, gathered)
    small = matmul(h, w_small, mode="nn", name="proj_small", out_dtypes=(F32,))
    bgf, = rowwise(_gates_fn, [full(small)], [a_log_l, dt_bias_l, b_f_l], [(LANES, LANES, F32)], name="gates")

    cn = conv_fwd(big, conv_full, HG, name="conv")
    o_gdn, ss, tinv = gdn_fwd(cn, bgf, HG, name="gdn_fwd")
    oa, = rowwise(_gated_norm_fn, [(o_gdn, HEAD, 0, 1), (big, HEAD, o_z // HEAD, 1)], [gdn_norm_g],
                  [(GV, HEAD, BF16)], name="gated_norm", ncb=HG)

    cum = cumsum_tokens(bgf, reverse=False, name="cumsum")
    fqa = fox_prep(big, o_fq, fox_q_norm, cum, 0, HF, name="fox_prep_q")
    fka = fox_prep(big, o_fk, fox_k_norm, cum, 1, HF, name="fox_prep_k")
    fva = fox_prep(big, o_fv, fox_k_norm, cum, 2, HF, name="fox_prep_v")
    live = fox_live_tiles(cum, fox_q_norm, fox_k_norm, HF, T)
    ob, lse = fox_fwd(fqa, fka, fva, live, HF, name="fox_fwd")

    memn, = rowwise(rms_fn, [full(mems)], [g_mem], [(D, D, BF16)], name="rms_mem")
    kv_m = matmul(memn, W["w_mem_kv"], mode="nn", name="mem_kv", out_dtypes=(F32,))
    kmn, = rowwise(rms_fn, [(kv_m, DM, 0, 1)], [mem_k_norm], [(MW, DM, F32)], name="mem_knorm", ncb=HM)
    vm = kv_m[:, MW:]
    mem_fn = functools.partial(_mem_attn_fn, HM, DM)
    om, = rowwise(mem_fn, [(big, MW, o_mq // MW, 0)], [kmn, vm, mem_q_norm], [(MW, MW, BF16)], name="mem_attn")

    ya = matmul(oa, W["w_up_gdn"], mode="nn", name="up_gdn")
    yb = matmul(ob, W["w_up_fox"], mode="nn", name="up_fox")
    ym = matmul(om, W["w_up_mem"], mode="nn", name="up_mem")
    cbm = min(512, D)
    gate_rows = [(big, cbm, (o_gt + b * D) // cbm, 1) for b in range(3)]
    merge_rows = [(ya, cbm, 0, 1), (yb, cbm, 0, 1), (ym, cbm, 0, 1)] + gate_rows
    y, = rowwise(_merge_fn, merge_rows, [], [(D, cbm, BF16)], name="merge", ncb=D // cbm)
    x1 = matmul(y, W["w_out"], mode="nn", name="out_proj", out_dtypes=(F32,), extras=(xs,),
                epi=lambda r, res: (r + res,))

    h2, = rowwise(rms_fn, [full(x1)], [g_mlp], [(D, D, BF16)], name="rms_mlp")
    u_ff, a_ff, *gathered = matmul(h2, W["w_ff1"], mode="nn", name="ff1", out_dtypes=(BF16, BF16), b_shards=True,
                                   epi=lambda r: (r, jnp.square(jnp.maximum(r, 0.0))),
                                   comm=comm_gather([shard[n] for n in last]))
    take(last, gathered)
    d_out = matmul(a_ff, W["w_ff2"], mode="nn", name="ff2_loss", out_dtypes=(F32,), extras=(x1, tgt),
                   epi=lambda r, res, t: ((r + res - t) * (1.0 / D),))
    loss_local = 0.5 * D * sum_squares(d_out, name="loss_sum")[0, 0]
    loss = lax.psum(loss_local, ("x", "y", "c"))

    G = {}
    d_u = matmul(d_out, W["w_ff2"], mode="nt", name="d_ff2_in", extras=(u_ff,),
                 epi=lambda r, u: (r * 2.0 * jnp.maximum(u.astype(F32), 0.0),))
    d_h2 = matmul(d_u, W["w_ff1"], mode="nt", name="d_ff1_in", b_shards=True)
    d_x1, G["g_mlp"] = rowwise_bwd(rms_fn, [full(x1)], [g_mlp], [full(d_h2)], [(0, F32)], [0], name="d_rms_mlp",
                                   adds=[full(d_out)])
    d_y = matmul(d_x1, W["w_out"], mode="nt", name="d_out_proj_in")
    G["w_out"] = matmul(y, d_x1, mode="tn", name="d_w_out")
    d_ya, d_yb, d_ym, d_ga, d_gb, d_gm = rowwise_bwd(
        _merge_fn, merge_rows, [], [(d_y, cbm, 0, 1)], [(k, BF16) for k in range(6)], [], name="d_merge", ncb=D // cbm)
    d_oa = matmul(d_ya, W["w_up_gdn"], mode="nt", name="d_up_gdn_in")
    d_ob = matmul(d_yb, W["w_up_fox"], mode="nt", name="d_up_fox_in")
    d_om = matmul(d_ym, W["w_up_mem"], mode="nt", name="d_up_mem_in")
    G["w_up_gdn"] = matmul(oa, d_ya, mode="tn", name="d_w_up_gdn")
    G["w_up_fox"] = matmul(ob, d_yb, mode="tn", name="d_w_up_fox")
    G["w_up_mem"] = matmul(om, d_ym, mode="tn", name="d_w_up_mem")

    d_mq, d_kmn, d_vm, G["mem_q_norm"] = rowwise_bwd(
        mem_fn, [(big, MW, o_mq // MW, 0)], [kmn, vm, mem_q_norm], [full(d_om)], [(0, BF16)], [0, 1, 2], name="d_mem_attn")
    d_km, G["mem_k_norm"] = rowwise_bwd(rms_fn, [(kv_m, DM, 0, 1)], [mem_k_norm], [(d_kmn, DM, 0, 1)], [(0, F32)], [0],
                                         name="d_mem_knorm", ncb=HM)
    d_kv_m = jnp.concatenate([d_km, d_vm], axis=1)
    G["w_mem_kv"] = matmul(memn, d_kv_m, mode="tn", name="d_w_mem_kv")
    d_memn = matmul(d_kv_m, W["w_mem_kv"], mode="nt", name="d_mem_kv_in")
    _, G["g_mem"] = rowwise_bwd(rms_fn, [full(mems)], [g_mem], [full(d_memn)], [(0, BF16)], [0], name="d_rms_mem")

    d_fkn, d_fv, d_cum_t = fox_bwd_kv(fqa, fka, fva, d_ob, ob, lse, live, HF, name="fox_bwd_kv")
    d_fqn = fox_bwd_q(fqa, fka, fva, d_ob, ob, lse, live, HF, name="fox_bwd_q")
    d_fq, G["fox_q_norm"] = rowwise_bwd(rms_fn, [(big, HEAD, o_fq // HEAD, 1)], [fox_q_norm], [(d_fqn, HEAD, 0, 1)],
                                         [(0, BF16)], [0], name="d_fox_qnorm", ncb=HF)
    d_fk, G["fox_k_norm"] = rowwise_bwd(rms_fn, [(big, HEAD, o_fk // HEAD, 1)], [fox_k_norm], [(d_fkn, HEAD, 0, 1)],
                                         [(0, BF16)], [0], name="d_fox_knorm", ncb=HF)
    d_cum = jnp.pad(d_cum_t[:HF].T, ((0, 0), (LANE_F, LANES - LANE_F - HF)))
    d_logf = cumsum_tokens(d_cum, reverse=True, name="cumsum_rev")

    d_o_gdn, d_z, G["gdn_norm_g"] = rowwise_bwd(
        _gated_norm_fn, [(o_gdn, HEAD, 0, 1), (big, HEAD, o_z // HEAD, 1)], [gdn_norm_g], [(d_oa, HEAD, 0, 1)],
        [(0, F32), (1, BF16)], [0], name="d_gated_norm", ncb=HG)
    d_cn, d_bg = gdn_bwd(cn, bgf, ss, tinv, d_o_gdn, HG, name="gdn_bwd")
    d_conv_y, G["conv_w"] = conv_bwd_taps(big, conv_full, d_cn, HG, name="d_conv_taps")
    d_qkv = conv_bwd_input(d_conv_y, conv_full, name="d_conv_in")
    d_small, d_al, d_dt, d_bf = rowwise_bwd(_gates_fn, [full(small)], [a_log_l, dt_bias_l, b_f_l], [full(d_bg + d_logf)],
                                            [(0, F32)], [0, 1, 2], name="d_gates")
    G["a_log"], G["dt_bias"], G["fox_b_f"] = (d_al[:, LANE_A:LANE_A + HG], d_dt[:, LANE_A:LANE_A + HG],
                                               d_bf[:, LANE_F:LANE_F + HF])

    def parts(n):
        g = G[n].astype(BF16)
        if g.ndim == 3:
            return g
        return _col_shards(g) if n in col_sharded else g.reshape(N_DEV, -1, g.shape[1])

    recv = {}

    def carried(names, out):
        for n, r in zip(names, out):
            recv[n] = r

    d_big = jnp.concatenate([d_qkv, d_z, d_fq, d_fk, d_fv, d_mq, d_ga, d_gb, d_gm], axis=1)
    group = ["conv_w", "w_mem_kv", "w_up_gdn", "w_up_fox", "w_up_mem", "w_out"]
    G["w_ff2"], *out = matmul(a_ff, d_out, mode="tn", name="d_w_ff2",
                              comm=comm_direct([parts(n) for n in group], [True] * len(group)))
    carried(group, out)
    G["w_ff1"], *out = matmul(h2, d_u, mode="tn", name="d_w_ff1", out_shards=True,
                              comm=comm_direct([parts("w_ff2")], [True]))
    carried(["w_ff2"], out)
    g_big, *out = matmul(h, d_big, mode="tn", name="d_w_big", comm=comm_direct([parts("w_ff1")], [True]))
    carried(["w_ff1"], out)
    g_small = matmul(h, d_small, mode="tn", name="d_w_small", out_dtypes=(F32,))
    cols = lambda a, o, wd: a[:, o:o + wd]
    G["w_in"] = jnp.concatenate([
        cols(g_big, 0, GQKV), cols(g_big, o_z, GV), cols(g_small, LANE_B, HG), cols(g_small, LANE_A, HG),
        cols(g_big, o_fq, FW), cols(g_big, o_fk, FW), cols(g_big, o_fv, FW), cols(g_small, LANE_F, HF),
        cols(g_big, o_mq, MW), cols(g_big, o_gt, 3 * D)], axis=1)
    d_h_s = matmul(d_small, w_small, mode="nt", name="d_proj_small_in", out_dtypes=(F32,))
    d_h, *out = matmul(d_big, w_big, mode="nt", name="d_proj_big_in", extras=(d_h_s,), epi=lambda r, e: (r + e,),
                       comm=comm_direct([parts("w_in")], [True]))
    carried(["w_in"], out)
    grad_x, G["g_mix"] = rowwise_bwd(rms_fn, [full(xs)], [g_mix], [full(d_h)], [(0, F32)], [0], name="d_rms_mix",
                                     adds=[full(d_x1)])
    grad_x = grad_x[None]

    small_sizes = [loc[n].shape[1] for n in small_names]
    pack = lambda d: jnp.concatenate([d[n].reshape(1, -1) for n in small_names], axis=1)
    npad = -sum(small_sizes) % LANES
    padp = lambda a: jnp.pad(a, ((0, 0), (0, npad)))
    recv_small, = run_comm(comm_direct([padp(pack(G))], [False]), name="gather_small_grads")

    res = {}
    for n in big_names:
        res[n] = [t[None] for t in adamw(recv[n], loc[n][0], loc["m_" + n][0], loc["v_" + n][0], name="adamw_" + n)]
    sm = adamw(recv_small, padp(pack({n: loc[n] for n in small_names})), padp(pack({n: loc["m_" + n] for n in small_names})),
               padp(pack({n: loc["v_" + n] for n in small_names})), name="adamw_small")
    so = np.concatenate([[0], np.cumsum(small_sizes)]).tolist()
    for i, n in enumerate(small_names):
        res[n] = [t[:, so[i]:so[i + 1]] for t in sm]

    order = ["g_mix", "w_in", "conv_w", "a_log", "dt_bias", "gdn_norm_g", "fox_b_f", "fox_q_norm", "fox_k_norm", "g_mem",
             "w_mem_kv", "mem_q_norm", "mem_k_norm", "w_up_gdn", "w_up_fox", "w_up_mem", "w_out", "g_mlp", "w_ff1", "w_ff2"]
    return (loss, grad_x, *[res[n][0] for n in order], *[res[n][1] for n in order],
            *[res[n][2] for n in order], *[res[n][3] for n in order])
```

```python
import functools

import jax
import jax.numpy as jnp
import numpy as np
from jax import lax
from jax.experimental import pallas as pl
from jax.experimental.pallas import tpu as pltpu

F32 = jnp.float32
BF16 = jnp.bfloat16
HI = lax.Precision.HIGHEST

EPS = 1e-6
GDN_CHUNK = 64
GDN_CONV = 4
HEAD = 128
LANES = 128
HALO = 16
N_DEV = 8
MESH = pl.DeviceIdType.MESH
VMEM_LIMIT_V7X = 56 * 1024 * 1024

ADAM_LR, ADAM_B1, ADAM_B2, ADAM_EPS, ADAM_WD, ADAM_STEP = 0.001, 0.9, 0.999, 1e-08, 0.01, 10

LANE_B, LANE_A, LANE_F = 0, 8, 16


def _params(sem):
    return pltpu.CompilerParams(dimension_semantics=sem, vmem_limit_bytes=VMEM_LIMIT_V7X)


def _dg(a, b, ca, cb, prec):
    nb = a.ndim - 2
    batch = tuple(range(nb))
    return lax.dot_general(a, b, (((ca + nb,), (cb + nb,)), (batch, batch)), precision=prec,
                           preferred_element_type=F32)


def _make_mm(prec, cast):
    def c(x):
        return x.astype(BF16) if cast else x

    @jax.custom_vjp
    def nn(a, b):
        return _dg(c(a), c(b), 1, 0, prec)

    @jax.custom_vjp
    def nt(a, b):
        return _dg(c(a), c(b), 1, 1, prec)

    @jax.custom_vjp
    def tn(a, b):
        return _dg(c(a), c(b), 0, 0, prec)

    nn.defvjp(lambda a, b: (nn(a, b), (a, b)), lambda r, g: (nt(g, r[1]), tn(r[0], g)))
    nt.defvjp(lambda a, b: (nt(a, b), (a, b)), lambda r, g: (nn(g, r[1]), tn(g, r[0])))
    tn.defvjp(lambda a, b: (tn(a, b), (a, b)), lambda r, g: (nt(r[1], g), nn(r[0], g)))
    return nn, nt, tn


NN, NT, TN = _make_mm(None, True)
NNH, NTH, TNH = _make_mm(lax.Precision.HIGH, False)
NNX, _, _ = _make_mm(HI, False)


def _sigmoid(x):
    return 1.0 / (1.0 + jnp.exp(-x))


def _silu(x):
    return x * _sigmoid(x)


def _softplus(x):
    return jnp.maximum(x, 0.0) + jnp.log(1.0 + jnp.exp(-jnp.abs(x)))


def _log_sigmoid(x):
    return -_softplus(-x)


def _rms(x, g):
    return x * lax.rsqrt(jnp.mean(x * x, axis=-1, keepdims=True) + EPS) * g


def _tile(n, target):
    t = target
    while t >= LANES:
        if n % t == 0:
            return t
        t //= 2
    return n


def matmul(a, b, *, mode, name, out_dtypes=(BF16,), epi=None, extras=(), tm=1024, tn=1024, tk=2048, comm=None,
           b_shards=False, out_shards=False):
    if b_shards:
        b_rows, b_cols = b.shape[1], N_DEV * b.shape[2]
    else:
        b_rows, b_cols = b.shape
    if mode == "nn":
        (M, K), (K2, N) = a.shape, (b_rows, b_cols)
    elif mode == "nt":
        (M, K), (N, K2) = a.shape, (b_rows, b_cols)
    else:
        (K, M), (K2, N) = a.shape, (b_rows, b_cols)
    assert K == K2, (name, a.shape, b.shape)
    tm, tn, tk = _tile(M, tm), _tile(N, tn), _tile(K, tk)
    if b_shards and mode == "nt":
        tk = K // N_DEV
    if (b_shards and mode != "nt") or out_shards:
        tn = N // N_DEV
    ni, nj, nk = M // tm, N // tn, K // tk
    a_spec = (pl.BlockSpec((tk, tm), lambda i, j, k: (k, i)) if mode == "tn"
              else pl.BlockSpec((tm, tk), lambda i, j, k: (i, k)))
    if b_shards:
        b_spec = (pl.BlockSpec((None, tn, tk), lambda i, j, k: (k, j, 0)) if mode == "nt"
                  else pl.BlockSpec((None, tk, tn), lambda i, j, k: (j, k, 0)))
    else:
        b_spec = (pl.BlockSpec((tn, tk), lambda i, j, k: (j, k)) if mode == "nt"
                  else pl.BlockSpec((tk, tn), lambda i, j, k: (k, j)))
    o_spec = pl.BlockSpec((tm, tn), lambda i, j, k: (i, j))
    w_spec = pl.BlockSpec((None, tm, tn), lambda i, j, k: (j, i, 0)) if out_shards else o_spec
    w_shape = (N_DEV, M, tn) if out_shards else (M, N)
    dims = {"nn": ((1,), (0,)), "nt": ((1,), (1,)), "tn": ((0,), (0,))}[mode]
    ne, no = len(extras), len(out_dtypes)
    nc = len(comm.arrays) if comm is not None else 0
    n_steps = ni * nj * nk

    def body(a_ref, b_ref, *rest):
        ex, c_in = rest[:ne], rest[ne:ne + nc]
        outs, c_out = rest[ne + nc:ne + nc + no], rest[ne + nc + no:ne + nc + no + nc]
        acc = rest[ne + nc + no + nc]
        c_sems = rest[ne + nc + no + nc + 1:]
        k = pl.program_id(2)
        step = (pl.program_id(0) * nj + pl.program_id(1)) * nk + k

        if comm is not None:
            pl.when(step == 0)(lambda: comm.start(c_in, c_out, c_sems))
            if comm.forward is not None and n_steps >= 3:
                pl.when(step == (3 * n_steps) // 4)(lambda: comm.forward(c_in, c_out, c_sems))

        @pl.when(k == 0)
        def _():
            acc[...] = jnp.zeros_like(acc)

        acc[...] += lax.dot_general(a_ref[...].astype(BF16), b_ref[...].astype(BF16), (dims, ((), ())),
                                    preferred_element_type=F32)

        @pl.when(k == nk - 1)
        def _():
            r = acc[...]
            vals = epi(r, *[e[...] for e in ex]) if epi is not None else (r,)
            for o, v in zip(outs, vals):
                o[...] = v.astype(o.dtype)

        if comm is not None:
            @pl.when(step == n_steps - 1)
            def _():
                if comm.forward is not None and n_steps < 3:
                    comm.forward(c_in, c_out, c_sems)
                comm.finish(c_in, c_out, c_sems)

    any_spec = pl.BlockSpec(memory_space=pl.ANY)
    sem = ("arbitrary",) * 3 if comm is not None else ("parallel", "parallel", "arbitrary")
    out = pl.pallas_call(
        body, name=name, grid=(ni, nj, nk),
        in_specs=[a_spec, b_spec] + [o_spec] * ne + [any_spec] * nc, out_specs=[w_spec] * no + [any_spec] * nc,
        out_shape=[jax.ShapeDtypeStruct(w_shape, d) for d in out_dtypes] + (comm.out_shape if comm is not None else []),
        scratch_shapes=[pltpu.VMEM((tm, tn), F32)] + (comm.sems if comm is not None else []),
        compiler_params=_params(sem))(a, b, *extras, *(comm.arrays if comm is not None else []))
    return out[0] if len(out) == 1 else out


def _row_spec(tr, cb, off, moves):
    return pl.BlockSpec((tr, cb), lambda i, j: (i, off + moves * j))


def _whole_spec(p):
    return pl.BlockSpec(p.shape, lambda i, j: (0,) * p.ndim)


def _row_tile(T, rows):
    widest = max(cb for (_, cb, _, _) in rows)
    return min(T, max(512, (1 << 19) // widest))


def rowwise(fn, rows, params, outs, *, name, ncb=1, tr=None):
    T = rows[0][0].shape[0]
    tr = min(tr, T) if tr else _row_tile(T, rows)
    assert T % tr == 0
    nr, npar = len(rows), len(params)

    def body(*refs):
        r, p, o = refs[:nr], refs[nr:nr + npar], refs[nr + npar:]
        vals = fn(*[x[...].astype(F32) for x in r], *[x[...] for x in p])
        for oref, v in zip(o, vals):
            oref[...] = v.astype(oref.dtype)

    res = pl.pallas_call(
        body, name=name, grid=(T // tr, ncb),
        in_specs=[_row_spec(tr, cb, off, mv) for (_, cb, off, mv) in rows] + [_whole_spec(p) for p in params],
        out_specs=[_row_spec(tr, cb, 0, 1) for (_, cb, _) in outs],
        out_shape=[jax.ShapeDtypeStruct((T, cols), d) for (cols, _, d) in outs],
        compiler_params=_params(("parallel", "parallel")))(*[r[0] for r in rows], *params)
    return res


def rowwise_bwd(fn, rows, params, cots, drows, dparams, *, name, ncb=1, tr=None, adds=()):
    T = rows[0][0].shape[0]
    tr = min(tr, T) if tr else _row_tile(T, rows)
    assert T % tr == 0
    nr, npar, nc, ndr, na = len(rows), len(params), len(cots), len(drows), len(adds)

    def body(*refs):
        r, p, c = refs[:nr], refs[nr:nr + npar], refs[nr + npar:nr + npar + nc]
        base = nr + npar + nc + na
        ad, o_r, o_p = refs[base - na:base], refs[base:base + ndr], refs[base + ndr:]
        prim = [x[...].astype(F32) for x in r] + [x[...] for x in p]
        _, vjp = jax.vjp(lambda *a: tuple(fn(*a)), *prim)
        g = vjp(tuple(x[...].astype(F32) for x in c))
        for k, (oref, (idx, _)) in enumerate(zip(o_r, drows)):
            val = g[idx] + ad[k][...].astype(F32) if k < na else g[idx]
            oref[...] = val.astype(oref.dtype)
        first = jnp.logical_and(pl.program_id(0) == 0, pl.program_id(1) == 0)

        @pl.when(first)
        def _():
            for oref in o_p:
                oref[...] = jnp.zeros_like(oref)

        for oref, idx in zip(o_p, dparams):
            oref[...] += g[nr + idx]

    res = pl.pallas_call(
        body, name=name, grid=(T // tr, ncb),
        in_specs=([_row_spec(tr, cb, off, mv) for (_, cb, off, mv) in rows] + [_whole_spec(p) for p in params]
                  + [_row_spec(tr, cb, off, mv) for (_, cb, off, mv) in tuple(cots) + tuple(adds)]),
        out_specs=([_row_spec(tr, rows[idx][1], 0, 1) for (idx, _) in drows]
                   + [_whole_spec(params[idx]) for idx in dparams]),
        out_shape=([jax.ShapeDtypeStruct((T, ncb * rows[idx][1] if rows[idx][3] else rows[idx][1]), d)
                    for (idx, d) in drows]
                   + [jax.ShapeDtypeStruct(params[idx].shape, F32) for idx in dparams]),
        compiler_params=_params(("arbitrary", "arbitrary")))(
            *[r[0] for r in rows], *params, *[c[0] for c in cots], *[a[0] for a in adds])
    return res


def full(a):
    return (a, a.shape[1], 0, 0)


def cumsum_tokens(x, *, reverse, name, tb=256):
    T = x.shape[0]
    tb = min(tb, T)
    nb = T // tb
    idx = (lambda i: (nb - 1 - i, 0)) if reverse else (lambda i: (i, 0))

    def body(x_ref, o_ref, carry):
        @pl.when(pl.program_id(0) == 0)
        def _():
            carry[...] = jnp.zeros_like(carry)

        ii = lax.broadcasted_iota(jnp.int32, (tb, tb), 0)
        jj = lax.broadcasted_iota(jnp.int32, (tb, tb), 1)
        tri = ((ii <= jj) if reverse else (ii >= jj)).astype(F32)
        c = lax.dot_general(tri, x_ref[...], (((1,), (0,)), ((), ())), precision=HI,
                            preferred_element_type=F32) + carry[0:1, :]
        o_ref[...] = c
        carry[0:1, :] = c[0:1, :] if reverse else c[tb - 1:tb, :]

    return pl.pallas_call(
        body, name=name, grid=(nb,), in_specs=[pl.BlockSpec((tb, LANES), idx)],
        out_specs=pl.BlockSpec((tb, LANES), idx), out_shape=jax.ShapeDtypeStruct((T, LANES), F32),
        scratch_shapes=[pltpu.VMEM((8, LANES), F32)], compiler_params=_params(("arbitrary",)))(x)


def _conv_post(y, kind, dk):
    c = _silu(y)
    if kind == 2:
        return c
    r = lax.rsqrt(jnp.sum(c * c, axis=-1, keepdims=True) + EPS)
    return c * r * (dk ** -0.5) if kind == 0 else c * r


def _by_kind(j, n_qk_heads, fn):
    pl.when(j < n_qk_heads)(functools.partial(fn, 0))
    pl.when(jnp.logical_and(j >= n_qk_heads, j < 2 * n_qk_heads))(functools.partial(fn, 1))
    pl.when(j >= 2 * n_qk_heads)(functools.partial(fn, 2))


def _conv_taps(cur, prev, w, tr):
    ext = jnp.concatenate([prev, cur], axis=0)
    y = w[3:4, :] * cur
    for d in (1, 2, 3):
        y = y + w[3 - d:4 - d, :] * pltpu.roll(ext, d, 0)[HALO:HALO + tr]
    return y


def conv_fwd(big, w, n_qk_heads, *, name, tr=2048):
    T, W = big.shape[0], w.shape[1]
    tr = min(tr, T)
    nh = W // HEAD

    def body(cur_ref, prev_ref, w_ref, o_ref):
        i, j = pl.program_id(0), pl.program_id(1)
        prev = jnp.where(i > 0, prev_ref[...].astype(F32), 0.0)
        y = _conv_taps(cur_ref[...].astype(F32), prev, w_ref[...], tr)

        def post(kind):
            o_ref[...] = _conv_post(y, kind, HEAD)

        _by_kind(j, n_qk_heads, post)

    return pl.pallas_call(
        body, name=name, grid=(T // tr, nh),
        in_specs=[pl.BlockSpec((tr, HEAD), lambda i, j: (i, j)),
                  pl.BlockSpec((HALO, HEAD), lambda i, j: (jnp.maximum(i * (tr // HALO) - 1, 0), j)),
                  pl.BlockSpec((GDN_CONV, HEAD), lambda i, j: (0, j))],
        out_specs=pl.BlockSpec((tr, HEAD), lambda i, j: (i, j)),
        out_shape=jax.ShapeDtypeStruct((T, W), F32), compiler_params=_params(("parallel", "parallel")))(big, big, w)


def conv_bwd_taps(big, w, dcn, n_qk_heads, *, name, tr=2048):
    T, W = big.shape[0], w.shape[1]
    tr = min(tr, T)
    nh = W // HEAD

    def body(cur_ref, prev_ref, w_ref, g_ref, dy_ref, dw_ref):
        j, i = pl.program_id(0), pl.program_id(1)
        cur = cur_ref[...].astype(F32)
        prev = jnp.where(i > 0, prev_ref[...].astype(F32), 0.0)
        y = _conv_taps(cur, prev, w_ref[...], tr)

        @pl.when(i == 0)
        def _():
            dw_ref[...] = jnp.zeros_like(dw_ref)

        def back(kind):
            _, vjp = jax.vjp(lambda t: _conv_post(t, kind, HEAD), y)
            dy, = vjp(g_ref[...])
            dy_ref[...] = dy
            ext = jnp.concatenate([prev, cur], axis=0)
            rows = [jnp.sum(dy * (cur if d == 0 else pltpu.roll(ext, d, 0)[HALO:HALO + tr]), axis=0, keepdims=True)
                    for d in (3, 2, 1, 0)]
            dw_ref[...] += jnp.concatenate(rows, axis=0)

        _by_kind(j, n_qk_heads, back)

    return pl.pallas_call(
        body, name=name, grid=(nh, T // tr),
        in_specs=[pl.BlockSpec((tr, HEAD), lambda j, i: (i, j)),
                  pl.BlockSpec((HALO, HEAD), lambda j, i: (jnp.maximum(i * (tr // HALO) - 1, 0), j)),
                  pl.BlockSpec((GDN_CONV, HEAD), lambda j, i: (0, j)),
                  pl.BlockSpec((tr, HEAD), lambda j, i: (i, j))],
        out_specs=[pl.BlockSpec((tr, HEAD), lambda j, i: (i, j)), pl.BlockSpec((GDN_CONV, HEAD), lambda j, i: (0, j))],
        out_shape=[jax.ShapeDtypeStruct((T, W), F32), jax.ShapeDtypeStruct((GDN_CONV, W), F32)],
        compiler_params=_params(("parallel", "arbitrary")))(big, big, w, dcn)


def conv_bwd_input(dy, w, *, name, tr=2048):
    T, W = dy.shape
    tr = min(tr, T)
    nrow = T // tr

    def body(cur_ref, nxt_ref, w_ref, o_ref):
        i = pl.program_id(0)
        cur = cur_ref[...]
        nxt = jnp.where(i < nrow - 1, nxt_ref[...], 0.0)
        ext = jnp.concatenate([cur, nxt], axis=0)
        w = w_ref[...]
        dx = w[3:4, :] * cur
        for d in (1, 2, 3):
            dx = dx + w[3 - d:4 - d, :] * pltpu.roll(ext, tr + HALO - d, 0)[0:tr]
        o_ref[...] = dx.astype(o_ref.dtype)

    return pl.pallas_call(
        body, name=name, grid=(nrow, W // HEAD),
        in_specs=[pl.BlockSpec((tr, HEAD), lambda i, j: (i, j)),
                  pl.BlockSpec((HALO, HEAD), lambda i, j: (jnp.minimum((i + 1) * (tr // HALO), T // HALO - 1), j)),
                  pl.BlockSpec((GDN_CONV, HEAD), lambda i, j: (0, j))],
        out_specs=pl.BlockSpec((tr, HEAD), lambda i, j: (i, j)),
        out_shape=jax.ShapeDtypeStruct((T, W), BF16), compiler_params=_params(("parallel", "parallel")))(dy, dy, w)


INV_BLOCK = 16


def _unit_lower_inverse_value(p):
    C = p.shape[-1]
    ii = lax.broadcasted_iota(jnp.int32, (C, C), 0)
    jj = lax.broadcasted_iota(jnp.int32, (C, C), 1)
    eye = jnp.where((ii == jj)[None], 1.0, 0.0)
    same = ((ii // INV_BLOCK) == (jj // INV_BLOCK))[None]
    pd = jnp.where(same, p, 0.0)
    d_inv = eye + pd
    n = 2
    while n < INV_BLOCK:
        pd = NNH(pd, pd)
        d_inv = d_inv + NNH(d_inv, pd)
        n *= 2
    nb = NNH(d_inv, jnp.where(same, 0.0, p))
    t = eye + nb
    n = 2
    while n < C // INV_BLOCK:
        nb = NNH(nb, nb)
        t = t + NNH(t, nb)
        n *= 2
    return NNH(t, d_inv)


@jax.custom_vjp
def _unit_lower_inverse(p, t_known):
    return _unit_lower_inverse_value(p) if t_known is None else t_known


def _unit_lower_inverse_fwd(p, t_known):
    t = _unit_lower_inverse(p, t_known)
    return t, (t, t_known is not None)


def _unit_lower_inverse_bwd(res, g):
    t, had = res
    return NTH(TNH(t, g), t), (jnp.zeros_like(t) if had else None)


_unit_lower_inverse.defvjp(_unit_lower_inverse_fwd, _unit_lower_inverse_bwd)


def _gdn_chunk(q, k, v, bg, S, t_known=None):
    H, C = q.shape[0], q.shape[1]
    ii = lax.broadcasted_iota(jnp.int32, (C, C), 0)
    jj = lax.broadcasted_iota(jnp.int32, (C, C), 1)
    lincl = (ii >= jj).astype(F32)
    strict, incl, eye = (ii > jj)[None], (ii >= jj)[None], (ii == jj)[None]
    gam2d = lax.dot_general(lincl, bg, (((1,), (0,)), ((), ())), precision=HI, preferred_element_type=F32)
    lane = lax.broadcasted_iota(jnp.int32, (H, 1, LANES), 2)
    hh = lax.broadcasted_iota(jnp.int32, (H, 1, LANES), 0)
    beta = jnp.sum(bg[None] * (lane == hh + LANE_B).astype(F32), axis=2, keepdims=True)
    gam = jnp.sum(gam2d[None] * (lane == hh + LANE_A).astype(F32), axis=2, keepdims=True)
    last = (lax.broadcasted_iota(jnp.int32, (1, C, 1), 1) == C - 1).astype(F32)
    gam_last = jnp.sum(gam * last, axis=1, keepdims=True)
    gam_row = NNX(jnp.ones((H, C, C), F32), jnp.where(eye, gam, 0.0))
    diff = gam - gam_row
    dec_s = jnp.where(strict, jnp.exp(jnp.where(strict, diff, 0.0)), 0.0)
    dec_i = jnp.where(incl, jnp.exp(jnp.where(incl, diff, 0.0)), 0.0)
    t = _unit_lower_inverse(-(beta * NT(k, k) * dec_s), t_known)
    eg = jnp.exp(gam)
    wu = NNH(t, jnp.concatenate([beta * eg * k, beta * v], axis=-1))
    w, u0 = wu[..., :HEAD], wu[..., HEAD:]
    qk = NT(q, k) * dec_i
    u = u0 - NN(w, S)
    o = NN(jnp.concatenate([q * eg, qk], axis=-1), jnp.concatenate([S, u], axis=-2))
    S2 = jnp.exp(gam_last) * S + TN(k * jnp.exp(gam_last - gam), u)
    return o, S2, t


def _heads(x, base, H):
    return jnp.stack([x[:, base + h * HEAD:base + (h + 1) * HEAD] for h in range(H)])


def _unheads(x):
    return jnp.concatenate([x[h] for h in range(x.shape[0])], axis=1)


def gdn_fwd(cn, bgf, H, *, name):
    T, C, W = cn.shape[0], GDN_CHUNK, H * HEAD
    N = T // C

    def body(cn_ref, bg_ref, o_ref, ss_ref, t_ref, s_scr):
        @pl.when(pl.program_id(0) == 0)
        def _():
            s_scr[...] = jnp.zeros_like(s_scr)

        x, S = cn_ref[...], s_scr[...]
        ss_ref[0] = S
        o, S2, t = _gdn_chunk(_heads(x, 0, H), _heads(x, W, H), _heads(x, 2 * W, H), bg_ref[...], S)
        o_ref[...] = _unheads(o)
        t_ref[0] = t
        s_scr[...] = S2

    return pl.pallas_call(
        body, name=name, grid=(N,),
        in_specs=[pl.BlockSpec((C, 3 * W), lambda n: (n, 0)), pl.BlockSpec((C, LANES), lambda n: (n, 0))],
        out_specs=[pl.BlockSpec((C, W), lambda n: (n, 0)), pl.BlockSpec((1, H, HEAD, HEAD), lambda n: (n, 0, 0, 0)),
                   pl.BlockSpec((1, H, C, C), lambda n: (n, 0, 0, 0))],
        out_shape=[jax.ShapeDtypeStruct((T, W), F32), jax.ShapeDtypeStruct((N, H, HEAD, HEAD), F32),
                   jax.ShapeDtypeStruct((N, H, C, C), F32)],
        scratch_shapes=[pltpu.VMEM((H, HEAD, HEAD), F32)], compiler_params=_params(("arbitrary",)))(cn, bgf)


def gdn_bwd(cn, bgf, ss, tinv, do, H, *, name):
    T, C, W = cn.shape[0], GDN_CHUNK, H * HEAD
    N = T // C

    def body(cn_ref, bg_ref, ss_ref, t_ref, do_ref, dcn_ref, dbg_ref, ds_scr):
        @pl.when(pl.program_id(0) == 0)
        def _():
            ds_scr[...] = jnp.zeros_like(ds_scr)

        x = cn_ref[...]
        t_known = t_ref[0]
        _, vjp = jax.vjp(lambda *a: _gdn_chunk(*a, t_known)[:2],
                         _heads(x, 0, H), _heads(x, W, H), _heads(x, 2 * W, H), bg_ref[...], ss_ref[0])
        dq, dk, dv, dbg, dS = vjp((_heads(do_ref[...], 0, H), ds_scr[...]))
        dcn_ref[...] = jnp.concatenate([_unheads(dq), _unheads(dk), _unheads(dv)], axis=1)
        dbg_ref[...] = dbg
        ds_scr[...] = dS

    rev = lambda n: (N - 1 - n, 0)
    return pl.pallas_call(
        body, name=name, grid=(N,),
        in_specs=[pl.BlockSpec((C, 3 * W), rev), pl.BlockSpec((C, LANES), rev),
                  pl.BlockSpec((1, H, HEAD, HEAD), lambda n: (N - 1 - n, 0, 0, 0)),
                  pl.BlockSpec((1, H, C, C), lambda n: (N - 1 - n, 0, 0, 0)), pl.BlockSpec((C, W), rev)],
        out_specs=[pl.BlockSpec((C, 3 * W), rev), pl.BlockSpec((C, LANES), rev)],
        out_shape=[jax.ShapeDtypeStruct((T, 3 * W), F32), jax.ShapeDtypeStruct((T, LANES), F32)],
        scratch_shapes=[pltpu.VMEM((H, HEAD, HEAD), F32)], compiler_params=_params(("arbitrary",)))(cn, bgf, ss, tinv, do)


AUG = 2 * HEAD


def fox_prep(big, col_off, gain, cum, kind, H, *, name, tr=2048):
    T = big.shape[0]
    tr = min(tr, T)

    def body(x_ref, g_ref, c_ref, o_ref):
        h = pl.program_id(1)
        x = x_ref[...].astype(F32)
        lane = lax.broadcasted_iota(jnp.int32, (tr, HEAD), 1)
        if kind == 2:
            main, aug = x, jnp.ones((tr, HEAD), F32)
        else:
            main = _rms(x, g_ref[...]) * ((HEAD ** -0.5) if kind == 0 else 1.0)
            c = jnp.sum(jnp.where(lane == LANE_F + h, c_ref[...], 0.0), axis=1, keepdims=True)
            hi = c.astype(BF16).astype(F32)
            mid = (c - hi).astype(BF16).astype(F32)
            lo = c - hi - mid
            if kind == 0:
                aug = jnp.where(lane == 0, hi, jnp.where(lane == 1, mid, jnp.where(lane == 2, lo,
                                                                                   jnp.where(lane < 6, 1.0, 0.0))))
            else:
                aug = jnp.where(lane < 3, 1.0, jnp.where(lane == 3, -hi, jnp.where(lane == 4, -mid,
                                                                                   jnp.where(lane == 5, -lo, 0.0))))
        o_ref[...] = jnp.concatenate([main, aug], axis=1).astype(o_ref.dtype)

    return pl.pallas_call(
        body, name=name, grid=(T // tr, H),
        in_specs=[pl.BlockSpec((tr, HEAD), lambda i, h: (i, col_off // HEAD + h)),
                  pl.BlockSpec((1, HEAD), lambda i, h: (0, 0)), pl.BlockSpec((tr, LANES), lambda i, h: (i, 0))],
        out_specs=pl.BlockSpec((tr, AUG), lambda i, h: (i, h)),
        out_shape=jax.ShapeDtypeStruct((T, H * AUG), BF16), compiler_params=_params(("parallel", "parallel")))(big, gain, cum)


def _fox_logits(q_ref, k_ref, h, tq, tk, diagonal):
    ha = slice(h * AUG, (h + 1) * AUG)
    s = lax.dot_general(q_ref[:, ha], k_ref[:, ha], (((1,), (1,)), ((), ())), preferred_element_type=F32)
    keep = None
    if diagonal:
        keep = lax.broadcasted_iota(jnp.int32, (tq, tk), 0) >= lax.broadcasted_iota(jnp.int32, (tq, tk), 1)
    return s, keep


def _dispatch(live_ref, H, i, j, below, on_diagonal, head_fn, after=None):
    def straight(diagonal):
        for h in range(H):
            head_fn(h, diagonal)
        if after is not None:
            after()

    def by_head():
        for h in range(H):
            pl.when(live_ref[h, i, j] != 0)(functools.partial(head_fn, h, False))
        if after is not None:
            after()

    pl.when(jnp.logical_and(below, live_ref[H, i, j] == 2))(functools.partial(straight, False))
    pl.when(jnp.logical_and(below, live_ref[H, i, j] == 1))(by_head)
    pl.when(on_diagonal)(functools.partial(straight, True))


FOX_TILE = 512
EXP_UNDERFLOW = -100.0


def fox_live_tiles(cum, q_gain, k_gain, H, T):
    t = min(FOX_TILE, T)
    n = T // t
    c = cum[:, LANE_F:LANE_F + H]
    bias = c[0::t][:, None, :] - c[t - 1::t][None, :, :]
    bound = 1.02 * (HEAD ** 0.5) * jnp.max(jnp.abs(q_gain)) * jnp.max(jnp.abs(k_gain))
    causal = (jnp.arange(n)[:, None] >= jnp.arange(n)[None, :])[:, :, None]
    live = jnp.logical_and(causal, 2.0 * bound + bias >= EXP_UNDERFLOW)
    status = jnp.where(jnp.all(live, axis=2), 2, jnp.where(jnp.any(live, axis=2), 1, 0))
    return jnp.concatenate([jnp.transpose(live, (2, 0, 1)).astype(jnp.int32), status[None].astype(jnp.int32)], axis=0)


def fox_fwd(qa, ka, va, live, H, *, name):
    T = qa.shape[0]
    tq = tk = min(FOX_TILE, T)
    nq, nk = T // tq, T // tk

    def body(live_ref, q_ref, k_ref, v_ref, o_ref, lse_ref, acc, m_scr):
        i, j = pl.program_id(0), pl.program_id(1)

        @pl.when(j == 0)
        def _():
            m_scr[...] = jnp.full_like(m_scr, -jnp.inf)
            acc[...] = jnp.zeros_like(acc)

        def head(h, diagonal):
            ha = slice(h * AUG, (h + 1) * AUG)
            s, keep = _fox_logits(q_ref, k_ref, h, tq, tk, diagonal)
            if diagonal:
                s = jnp.where(keep, s, -jnp.inf)
            m_prev = m_scr[h]
            m_new = jnp.maximum(m_prev, jnp.max(s, axis=1, keepdims=True))
            p = jnp.exp(s - m_new[:, 0:1])
            p_hi = p.astype(BF16)
            p_lo = (p - p_hi.astype(F32)).astype(BF16)
            pv = lambda t: lax.dot_general(t, v_ref[:, ha], (((1,), (0,)), ((), ())), preferred_element_type=F32)
            acc[:, ha] = jnp.exp(m_prev[:, 0:1] - m_new[:, 0:1]) * acc[:, ha] + (pv(p_hi) + pv(p_lo))
            m_scr[h] = m_new

        _dispatch(live_ref, H, i, j, j < i, j == i, head)

        @pl.when(j == nk - 1)
        def _():
            lane = lax.broadcasted_iota(jnp.int32, (tq, LANES), 1)
            lse = jnp.zeros((tq, LANES), F32)
            for h in range(H):
                den = acc[:, h * AUG + HEAD:(h + 1) * AUG]
                o_ref[:, h * HEAD:(h + 1) * HEAD] = acc[:, h * AUG:h * AUG + HEAD] / den
                lse = jnp.where(lane == h, m_scr[h] + jnp.log(den), lse)
            lse_ref[...] = lse

    kv_idx = lambda i, j, f: (jnp.minimum(j, i), 0)
    q_idx = lambda i, j, f: (i, 0)
    return pl.pallas_call(
        body, name=name,
        grid_spec=pltpu.PrefetchScalarGridSpec(
            num_scalar_prefetch=1, grid=(nq, nk),
            in_specs=[pl.BlockSpec((tq, H * AUG), q_idx), pl.BlockSpec((tk, H * AUG), kv_idx), pl.BlockSpec((tk, H * AUG), kv_idx)],
            out_specs=[pl.BlockSpec((tq, H * HEAD), q_idx), pl.BlockSpec((tq, LANES), q_idx)],
            scratch_shapes=[pltpu.VMEM((tq, H * AUG), F32), pltpu.VMEM((H, tq, LANES), F32)]),
        out_shape=[jax.ShapeDtypeStruct((T, H * HEAD), F32), jax.ShapeDtypeStruct((T, LANES), F32)],
        compiler_params=_params(("parallel", "arbitrary")))(live, qa, ka, va)


def _fox_ds(q_ref, k_ref, v_ref, do_ref, o_ref, lse_ref, h, tq, tk, diagonal):
    hs = slice(h * HEAD, (h + 1) * HEAD)
    s, keep = _fox_logits(q_ref, k_ref, h, tq, tk, diagonal)
    p = jnp.exp(s - lse_ref[:, h:h + 1])
    if diagonal:
        p = jnp.where(keep, p, 0.0)
    do = do_ref[:, hs]
    dp = lax.dot_general(do, v_ref[:, h * AUG:h * AUG + HEAD], (((1,), (1,)), ((), ())), preferred_element_type=F32)
    delta = jnp.sum(do.astype(F32) * o_ref[:, hs], axis=1, keepdims=True)
    return p, p * (dp - delta)


def fox_bwd_kv(qa, ka, va, do, o, lse, live, H, *, name):
    T, W = do.shape
    tq = tk = min(FOX_TILE, T)
    nq, nk = T // tq, T // tk

    def body(live_ref, q_ref, k_ref, v_ref, do_ref, o_ref, lse_ref, dk_ref, dv_ref, dc_ref, dk_acc, dv_acc, dc_acc):
        j, i = pl.program_id(0), pl.program_id(1)

        @pl.when(i == 0)
        def _():
            dk_acc[...] = jnp.zeros_like(dk_acc)
            dv_acc[...] = jnp.zeros_like(dv_acc)
            dc_acc[...] = jnp.zeros_like(dc_acc)

        def head(h, diagonal):
            hs = slice(h * HEAD, (h + 1) * HEAD)
            p, ds = _fox_ds(q_ref, k_ref, v_ref, do_ref, o_ref, lse_ref, h, tq, tk, diagonal)
            dv_acc[:, hs] += lax.dot_general(p.astype(BF16), do_ref[:, hs], (((0,), (0,)), ((), ())),
                                             preferred_element_type=F32)
            dk_acc[:, hs] += lax.dot_general(ds.astype(BF16), q_ref[:, h * AUG:h * AUG + HEAD],
                                             (((0,), (0,)), ((), ())), preferred_element_type=F32)
            dc_acc[h] -= jnp.broadcast_to(jnp.sum(ds, axis=0, keepdims=True), (8, tk))

        _dispatch(live_ref, H, i, j, i > j, i == j, head)

        @pl.when(i == nq - 1)
        def _():
            dk_ref[...] = dk_acc[...].astype(dk_ref.dtype)
            dv_ref[...] = dv_acc[...].astype(dv_ref.dtype)
            row = lax.broadcasted_iota(jnp.int32, (8, tk), 0)
            dc = jnp.zeros((8, tk), F32)
            for h in range(H):
                dc = jnp.where(row == h, dc_acc[h], dc)
            dc_ref[...] = dc

    q_idx = lambda j, i, f: (jnp.maximum(i, j), 0)
    kv_idx = lambda j, i, f: (j, 0)
    return pl.pallas_call(
        body, name=name,
        grid_spec=pltpu.PrefetchScalarGridSpec(
            num_scalar_prefetch=1, grid=(nk, nq),
            in_specs=[pl.BlockSpec((tq, H * AUG), q_idx), pl.BlockSpec((tk, H * AUG), kv_idx), pl.BlockSpec((tk, H * AUG), kv_idx),
                      pl.BlockSpec((tq, W), q_idx), pl.BlockSpec((tq, W), q_idx), pl.BlockSpec((tq, LANES), q_idx)],
            out_specs=[pl.BlockSpec((tk, W), kv_idx), pl.BlockSpec((tk, W), kv_idx),
                       pl.BlockSpec((8, tk), lambda j, i, f: (0, j))],
            scratch_shapes=[pltpu.VMEM((tk, W), F32), pltpu.VMEM((tk, W), F32), pltpu.VMEM((H, 8, tk), F32)]),
        out_shape=[jax.ShapeDtypeStruct((T, W), BF16), jax.ShapeDtypeStruct((T, W), BF16), jax.ShapeDtypeStruct((8, T), F32)],
        compiler_params=_params(("parallel", "arbitrary")))(live, qa, ka, va, do, o, lse)


def fox_bwd_q(qa, ka, va, do, o, lse, live, H, *, name):
    T, W = do.shape
    tq = tk = min(FOX_TILE, T)
    nq, nk = T // tq, T // tk

    def body(live_ref, q_ref, k_ref, v_ref, do_ref, o_ref, lse_ref, dq_ref, dq_acc):
        i, j = pl.program_id(0), pl.program_id(1)

        @pl.when(j == 0)
        def _():
            dq_acc[...] = jnp.zeros_like(dq_acc)

        def head(h, diagonal):
            hs = slice(h * HEAD, (h + 1) * HEAD)
            _, ds = _fox_ds(q_ref, k_ref, v_ref, do_ref, o_ref, lse_ref, h, tq, tk, diagonal)
            dq_acc[:, hs] += lax.dot_general(ds.astype(BF16), k_ref[:, h * AUG:h * AUG + HEAD],
                                             (((1,), (0,)), ((), ())), preferred_element_type=F32)

        _dispatch(live_ref, H, i, j, j < i, j == i, head)

        @pl.when(j == nk - 1)
        def _():
            dq_ref[...] = (dq_acc[...] * (HEAD ** -0.5)).astype(dq_ref.dtype)

    q_idx = lambda i, j, f: (i, 0)
    kv_idx = lambda i, j, f: (jnp.minimum(j, i), 0)
    return pl.pallas_call(
        body, name=name,
        grid_spec=pltpu.PrefetchScalarGridSpec(
            num_scalar_prefetch=1, grid=(nq, nk),
            in_specs=[pl.BlockSpec((tq, H * AUG), q_idx), pl.BlockSpec((tk, H * AUG), kv_idx), pl.BlockSpec((tk, H * AUG), kv_idx),
                      pl.BlockSpec((tq, W), q_idx), pl.BlockSpec((tq, W), q_idx), pl.BlockSpec((tq, LANES), q_idx)],
            out_specs=pl.BlockSpec((tq, W), q_idx),
            scratch_shapes=[pltpu.VMEM((tq, W), F32)]),
        out_shape=jax.ShapeDtypeStruct((T, W), BF16),
        compiler_params=_params(("parallel", "arbitrary")))(live, qa, ka, va, do, o, lse)


def _gates_fn(small, a_log_l, dt_bias_l, b_f_l):
    lane = lax.broadcasted_iota(jnp.int32, small.shape, 1)
    beta = _sigmoid(small)
    g = -jnp.exp(a_log_l) * _softplus(small + dt_bias_l)
    lf = _log_sigmoid(small + b_f_l)
    return (jnp.where(lane < LANE_A, beta, jnp.where(lane < LANE_F, g, jnp.where(lane < LANE_F + 8, lf, 0.0))),)


def _gated_norm_fn(o, z, g):
    return (_rms(o, g) * _silu(z),)


def _merge_fn(ya, yb, ym, ga, gb, gm):
    return (_sigmoid(ga) * ya + _sigmoid(gb) * yb + _sigmoid(gm) * ym,)


def _mem_attn_fn(nh, dh, mq, kn, v, gq):
    outs = []
    for h in range(nh):
        hs = slice(h * dh, (h + 1) * dh)
        qn = _rms(mq[:, hs], gq)
        s = NT(qn, kn[:, hs]) * (dh ** -0.5)
        e = jnp.exp(s - jnp.max(s, axis=1, keepdims=True))
        p = e / jnp.sum(e, axis=1, keepdims=True)
        outs.append(NN(p, v[:, hs]))
    return (jnp.concatenate(outs, axis=1),)


def sum_squares(x, *, name, tr=512):
    T, D = x.shape
    tr = min(tr, T)

    def body(x_ref, o_ref):
        @pl.when(pl.program_id(0) == 0)
        def _():
            o_ref[...] = jnp.zeros_like(o_ref)

        v = x_ref[...]
        o_ref[...] += jnp.sum(jnp.sum(v * v, axis=1, keepdims=True), axis=0, keepdims=True)

    return pl.pallas_call(
        body, name=name, grid=(T // tr,), in_specs=[pl.BlockSpec((tr, D), lambda i: (i, 0))],
        out_specs=pl.BlockSpec((1, LANES), lambda i: (0, 0)), out_shape=jax.ShapeDtypeStruct((1, LANES), F32),
        compiler_params=_params(("arbitrary",)))(x)


class Comm:
    def __init__(self, arrays, out_shape, sems, start, forward, finish):
        self.arrays, self.out_shape, self.sems = list(arrays), list(out_shape), list(sems)
        self.start, self.forward, self.finish = start, forward, finish


def _place():
    x, y, c = lax.axis_index("x"), lax.axis_index("y"), lax.axis_index("c")
    chips = [(1 - x, y), (x, 1 - y), (1 - x, 1 - y)]
    return x, y, c, chips


def comm_gather(arrays):
    n = len(arrays)
    lin = lambda px, py, pc: 4 * px + 2 * py + pc

    def copy(ins, outs, sems, a, k, block, to, src=None):
        slot = outs[a].at[lin(*block)]
        return pltpu.make_async_remote_copy(src_ref=slot if src is None else src, dst_ref=slot, send_sem=sems[0].at[a, k],
                                            recv_sem=sems[1].at[a, k], device_id=to, device_id_type=MESH)

    def local(ins, outs, sems, a):
        x, y, c, _ = _place()
        return pltpu.make_async_copy(ins[a], outs[a].at[lin(x, y, c)], sems[2].at[a])

    def start(ins, outs, sems):
        x, y, c, chips = _place()
        for a in range(n):
            local(ins, outs, sems, a).start()
        for a in range(n):
            for j, chip in enumerate(chips):
                copy(ins, outs, sems, a, 1 + j, (x, y, c), (*chip, c), src=ins[a]).start()
            copy(ins, outs, sems, a, 0, (x, y, c), (x, y, 1 - c), src=ins[a]).start()

    def forward(ins, outs, sems):
        x, y, c, chips = _place()
        for a in range(n):
            for j, chip in enumerate(chips):
                copy(ins, outs, sems, a, 1 + j, (*chip, c), (x, y, c)).wait_recv()
                copy(ins, outs, sems, a, 4 + j, (*chip, c), (x, y, 1 - c)).start()

    def finish(ins, outs, sems):
        x, y, c, chips = _place()
        for a in range(n):
            copy(ins, outs, sems, a, 0, (x, y, 1 - c), (x, y, c)).wait_recv()
            for j, chip in enumerate(chips):
                copy(ins, outs, sems, a, 4 + j, (*chip, 1 - c), (x, y, c)).wait_recv()
        for a in range(n):
            for j, chip in enumerate(chips):
                copy(ins, outs, sems, a, 1 + j, (x, y, c), (*chip, c), src=ins[a]).wait_send()
                copy(ins, outs, sems, a, 4 + j, (*chip, c), (x, y, 1 - c)).wait_send()
            copy(ins, outs, sems, a, 0, (x, y, c), (x, y, 1 - c), src=ins[a]).wait_send()
            local(ins, outs, sems, a).wait()

    return Comm(arrays, [jax.ShapeDtypeStruct((N_DEV,) + a.shape, a.dtype) for a in arrays],
                [pltpu.SemaphoreType.DMA((n, 7)), pltpu.SemaphoreType.DMA((n, 7)), pltpu.SemaphoreType.DMA((n,))],
                start, forward, finish)


def comm_direct(arrays, scatter):
    n = len(arrays)

    def peers():
        x, y, c = lax.axis_index("x"), lax.axis_index("y"), lax.axis_index("c")
        out = []
        for r in range(1, N_DEV):
            px, py, pc = (1 - x if r & 4 else x), (1 - y if r & 2 else y), (1 - c if r & 1 else c)
            out.append((r, (px, py, pc), 4 * px + 2 * py + pc))
        return 4 * x + 2 * y + c, out

    def remote(ins, outs, sems, a, r, dev, src_slot, dst_slot):
        return pltpu.make_async_remote_copy(
            src_ref=ins[a].at[src_slot] if scatter[a] else ins[a], dst_ref=outs[a].at[dst_slot],
            send_sem=sems[0].at[a, r - 1], recv_sem=sems[1].at[a, r - 1], device_id=dev, device_id_type=MESH)

    def local(ins, outs, sems, a, me):
        return pltpu.make_async_copy(ins[a].at[me] if scatter[a] else ins[a], outs[a].at[me], sems[2].at[a])

    def start(ins, outs, sems):
        me, ps = peers()
        for a in range(n):
            local(ins, outs, sems, a, me).start()
        for a in range(n):
            for r, dev, lin in ps:
                remote(ins, outs, sems, a, r, dev, lin, me).start()

    def finish(ins, outs, sems):
        me, ps = peers()
        for a in range(n):
            for r, dev, lin in ps:
                remote(ins, outs, sems, a, r, dev, lin, lin).wait_recv()
        for a in range(n):
            for r, dev, lin in ps:
                remote(ins, outs, sems, a, r, dev, lin, me).wait_send()
            local(ins, outs, sems, a, me).wait()

    return Comm(arrays, [jax.ShapeDtypeStruct(a.shape if sc else (N_DEV,) + a.shape, a.dtype) for a, sc in zip(arrays, scatter)],
                [pltpu.SemaphoreType.DMA((n, N_DEV - 1)), pltpu.SemaphoreType.DMA((n, N_DEV - 1)),
                 pltpu.SemaphoreType.DMA((n,))], start, None, finish)


def run_comm(comm, *, name):
    n = len(comm.arrays)

    def body(*refs):
        ins, outs, sems = refs[:n], refs[n:2 * n], refs[2 * n:]
        comm.start(ins, outs, sems)
        if comm.forward is not None:
            comm.forward(ins, outs, sems)
        comm.finish(ins, outs, sems)

    any_spec = pl.BlockSpec(memory_space=pl.ANY)
    return pl.pallas_call(body, name=name, in_specs=[any_spec] * n, out_specs=[any_spec] * n, out_shape=comm.out_shape,
                          scratch_shapes=comm.sems)(*comm.arrays)


def adamw(parts, w, m, v, *, name, tr=128):
    R, Cc = w.shape
    tr = min(tr, R)
    assert R % tr == 0
    n_parts = parts.shape[0]

    def body(p_ref, w_ref, m_ref, v_ref, g_ref, d_ref, nm_ref, nv_ref):
        g = p_ref[0].astype(F32)
        for s in range(1, n_parts):
            g = g + p_ref[s].astype(F32)
        nm = ADAM_B1 * m_ref[...] + (1.0 - ADAM_B1) * g
        nv = ADAM_B2 * v_ref[...] + (1.0 - ADAM_B2) * (g * g)
        m_hat = nm / (1.0 - ADAM_B1 ** ADAM_STEP)
        v_hat = nv / (1.0 - ADAM_B2 ** ADAM_STEP)
        g_ref[...] = g
        d_ref[...] = -ADAM_LR * (m_hat / (jnp.sqrt(v_hat) + ADAM_EPS) + ADAM_WD * w_ref[...])
        nm_ref[...] = nm
        nv_ref[...] = nv

    spec = pl.BlockSpec((tr, Cc), lambda i: (i, 0))
    return pl.pallas_call(
        body, name=name, grid=(R // tr,),
        in_specs=[pl.BlockSpec((n_parts, tr, Cc), lambda i: (0, i, 0)), spec, spec, spec], out_specs=[spec] * 4,
        out_shape=[jax.ShapeDtypeStruct((R, Cc), F32)] * 4, compiler_params=_params(("parallel",)))(parts, w, m, v)


def _lanes(vec, base):
    return jnp.pad(vec[None].astype(F32), ((0, 0), (base, LANES - base - vec.shape[0])))


def _col_shards(full_w):
    R, Ct = full_w.shape
    return jnp.transpose(full_w.reshape(R, N_DEV, Ct // N_DEV), (1, 0, 2))


def _from_col_shards(g):
    return jnp.transpose(g, (1, 0, 2)).reshape(g.shape[1], -1)


def kernel(x, mem, g_mix, w_in, conv_w, a_log, dt_bias, gdn_norm_g, fox_b_f, fox_q_norm, fox_k_norm, g_mem, w_mem_kv, mem_q_norm, mem_k_norm, w_up_gdn, w_up_fox, w_up_mem, w_out, g_mlp, w_ff1, w_ff2, loss_target, m_g_mix, m_w_in, m_conv_w, m_a_log, m_dt_bias, m_gdn_norm_g, m_fox_b_f, m_fox_q_norm, m_fox_k_norm, m_g_mem, m_w_mem_kv, m_mem_q_norm, m_mem_k_norm, m_w_up_gdn, m_w_up_fox, m_w_up_mem, m_w_out, m_g_mlp, m_w_ff1, m_w_ff2, v_g_mix, v_w_in, v_conv_w, v_a_log, v_dt_bias, v_gdn_norm_g, v_fox_b_f, v_fox_q_norm, v_fox_k_norm, v_g_mem, v_w_mem_kv, v_mem_q_norm, v_mem_k_norm, v_w_up_gdn, v_w_up_fox, v_w_up_mem, v_w_out, v_g_mlp, v_w_ff1, v_w_ff2):
    loc = dict(locals())
    big_names = ["w_in", "conv_w", "w_mem_kv", "w_up_gdn", "w_up_fox", "w_up_mem", "w_out", "w_ff1", "w_ff2"]
    col_sharded = {"w_in", "conv_w", "w_up_gdn", "w_up_fox", "w_up_mem", "w_ff1"}
    small_names = ["g_mix", "a_log", "dt_bias", "gdn_norm_g", "fox_b_f", "fox_q_norm", "fox_k_norm", "g_mem",
                   "mem_q_norm", "mem_k_norm", "g_mlp"]

    xs, tgt, mems = x[0], loss_target[0], mem[0]
    T, D = xs.shape
    HG = a_log.shape[1]
    HF = fox_b_f.shape[1]
    DM = mem_q_norm.shape[1]
    GQK, GV = HG * HEAD, HG * HEAD
    GQKV = 2 * GQK + GV
    FW = HF * HEAD
    MW = w_mem_kv.shape[2] // 2
    HM = MW // DM
    assert HG <= 8 and HF <= 8

    shard = {n: loc[n][0].astype(BF16) for n in big_names}
    first, rest, last = big_names[:2], big_names[2:-1], big_names[-1:]
    W = {}

    def take(names, gathered):
        for n, g in zip(names, gathered):
            if n == "w_ff1":
                W[n] = g
            else:
                W[n] = _from_col_shards(g) if n in col_sharded else g.reshape(-1, g.shape[2])

    take(first, run_comm(comm_gather([shard[n] for n in first]), name="gather_in"))
    widths = [GQKV, GV, HG, HG, FW, FW, FW, HF, MW, 3 * D]
    offs = np.concatenate([[0], np.cumsum(widths)]).tolist()
    seg = [W["w_in"][:, offs[i]:offs[i + 1]] for i in range(len(widths))]
    w_big = jnp.concatenate([seg[0], seg[1], seg[4], seg[5], seg[6], seg[8], seg[9]], axis=1)
    pad8 = lambda s: jnp.pad(s, ((0, 0), (0, 8 - s.shape[1])))
    w_small = jnp.concatenate([pad8(seg[2]), pad8(seg[3]), pad8(seg[7]), jnp.zeros((D, LANES - 24), BF16)], axis=1)
    o_z, o_fq, o_fk, o_fv = GQKV, GQKV + GV, GQKV + GV + FW, GQKV + GV + 2 * FW
    o_mq = o_fv + FW
    o_gt = o_mq + MW
    WB = o_gt + 3 * D
    conv_full = W["conv_w"].astype(F32)

    a_log_l, dt_bias_l, b_f_l = _lanes(a_log[0], LANE_A), _lanes(dt_bias[0], LANE_A), _lanes(fox_b_f[0], LANE_F)
    rms_fn = lambda t, g: (_rms(t, g),)

    h, = rowwise(rms_fn, [full(xs)], [g_mix], [(D, D, BF16)], name="rms_mix")
    big, *gathered = matmul(h, w_big, mode="nn", name="proj_big", comm=comm_gather([shard[n] for n in rest]))
    take(rest, gathered)
    small = matmul(h, w_small, mode="nn", name="proj_small", out_dtypes=(F32,))
    bgf, = rowwise(_gates_fn, [full(small)], [a_log_l, dt_bias_l, b_f_l], [(LANES, LANES, F32)], name="gates")

    cn = conv_fwd(big, conv_full, HG, name="conv")
    o_gdn, ss, tinv = gdn_fwd(cn, bgf, HG, name="gdn_fwd")
    oa, = rowwise(_gated_norm_fn, [(o_gdn, HEAD, 0, 1), (big, HEAD, o_z // HEAD, 1)], [gdn_norm_g],
                  [(GV, HEAD, BF16)], name="gated_norm", ncb=HG)

    cum = cumsum_tokens(bgf, reverse=False, name="cumsum")
    fqa = fox_prep(big, o_fq, fox_q_norm, cum, 0, HF, name="fox_prep_q")
    fka = fox_prep(big, o_fk, fox_k_norm, cum, 1, HF, name="fox_prep_k")
    fva = fox_prep(big, o_fv, fox_k_norm, cum, 2, HF, name="fox_prep_v")
    live = fox_live_tiles(cum, fox_q_norm, fox_k_norm, HF, T)
    ob, lse = fox_fwd(fqa, fka, fva, live, HF, name="fox_fwd")

    memn, = rowwise(rms_fn, [full(mems)], [g_mem], [(D, D, BF16)], name="rms_mem")
    kv_m = matmul(memn, W["w_mem_kv"], mode="nn", name="mem_kv", out_dtypes=(F32,))
    kmn, = rowwise(rms_fn, [(kv_m, DM, 0, 1)], [mem_k_norm], [(MW, DM, F32)], name="mem_knorm", ncb=HM)
    vm = kv_m[:, MW:]
    mem_fn = functools.partial(_mem_attn_fn, HM, DM)
    om, = rowwise(mem_fn, [(big, MW, o_mq // MW, 0)], [kmn, vm, mem_q_norm], [(MW, MW, BF16)], name="mem_attn")

    ya = matmul(oa, W["w_up_gdn"], mode="nn", name="up_gdn")
    yb = matmul(ob, W["w_up_fox"], mode="nn", name="up_fox")
    ym = matmul(om, W["w_up_mem"], mode="nn", name="up_mem")
    cbm = min(512, D)
    gate_rows = [(big, cbm, (o_gt + b * D) // cbm, 1) for b in range(3)]
    merge_rows = [(ya, cbm, 0, 1), (yb, cbm, 0, 1), (ym, cbm, 0, 1)] + gate_rows
    y, = rowwise(_merge_fn, merge_rows, [], [(D, cbm, BF16)], name="merge", ncb=D // cbm)
    x1 = matmul(y, W["w_out"], mode="nn", name="out_proj", out_dtypes=(F32,), extras=(xs,),
                epi=lambda r, res: (r + res,))

    h2, = rowwise(rms_fn, [full(x1)], [g_mlp], [(D, D, BF16)], name="rms_mlp")
    u_ff, a_ff, *gathered = matmul(h2, W["w_ff1"], mode="nn", name="ff1", out_dtypes=(BF16, BF16), b_shards=True,
                                   epi=lambda r: (r, jnp.square(jnp.maximum(r, 0.0))),
                                   comm=comm_gather([shard[n] for n in last]))
    take(last, gathered)
    d_out = matmul(a_ff, W["w_ff2"], mode="nn", name="ff2_loss", out_dtypes=(F32,), extras=(x1, tgt),
                   epi=lambda r, res, t: ((r + res - t) * (1.0 / D),))
    loss_local = 0.5 * D * sum_squares(d_out, name="loss_sum")[0, 0]
    loss = lax.psum(loss_local, ("x", "y", "c"))

    G = {}
    d_u = matmul(d_out, W["w_ff2"], mode="nt", name="d_ff2_in", extras=(u_ff,),
                 epi=lambda r, u: (r * 2.0 * jnp.maximum(u.astype(F32), 0.0),))
    d_h2 = matmul(d_u, W["w_ff1"], mode="nt", name="d_ff1_in", b_shards=True)
    d_x1, G["g_mlp"] = rowwise_bwd(rms_fn, [full(x1)], [g_mlp], [full(d_h2)], [(0, F32)], [0], name="d_rms_mlp",
                                   adds=[full(d_out)])
    d_y = matmul(d_x1, W["w_out"], mode="nt", name="d_out_proj_in")
    G["w_out"] = matmul(y, d_x1, mode="tn", name="d_w_out")
    d_ya, d_yb, d_ym, d_ga, d_gb, d_gm = rowwise_bwd(
        _merge_fn, merge_rows, [], [(d_y, cbm, 0, 1)], [(k, BF16) for k in range(6)], [], name="d_merge", ncb=D // cbm)
    d_oa = matmul(d_ya, W["w_up_gdn"], mode="nt", name="d_up_gdn_in")
    d_ob = matmul(d_yb, W["w_up_fox"], mode="nt", name="d_up_fox_in")
    d_om = matmul(d_ym, W["w_up_mem"], mode="nt", name="d_up_mem_in")
    G["w_up_gdn"] = matmul(oa, d_ya, mode="tn", name="d_w_up_gdn")
    G["w_up_fox"] = matmul(ob, d_yb, mode="tn", name="d_w_up_fox")
    G["w_up_mem"] = matmul(om, d_ym, mode="tn", name="d_w_up_mem")

    d_mq, d_kmn, d_vm, G["mem_q_norm"] = rowwise_bwd(
        mem_fn, [(big, MW, o_mq // MW, 0)], [kmn, vm, mem_q_norm], [full(d_om)], [(0, BF16)], [0, 1, 2], name="d_mem_attn")
    d_km, G["mem_k_norm"] = rowwise_bwd(rms_fn, [(kv_m, DM, 0, 1)], [mem_k_norm], [(d_kmn, DM, 0, 1)], [(0, F32)], [0],
                                         name="d_mem_knorm", ncb=HM)
    d_kv_m = jnp.concatenate([d_km, d_vm], axis=1)
    G["w_mem_kv"] = matmul(memn, d_kv_m, mode="tn", name="d_w_mem_kv")
    d_memn = matmul(d_kv_m, W["w_mem_kv"], mode="nt", name="d_mem_kv_in")
    _, G["g_mem"] = rowwise_bwd(rms_fn, [full(mems)], [g_mem], [full(d_memn)], [(0, BF16)], [0], name="d_rms_mem")

    d_fkn, d_fv, d_cum_t = fox_bwd_kv(fqa, fka, fva, d_ob, ob, lse, live, HF, name="fox_bwd_kv")
    d_fqn = fox_bwd_q(fqa, fka, fva, d_ob, ob, lse, live, HF, name="fox_bwd_q")
    d_fq, G["fox_q_norm"] = rowwise_bwd(rms_fn, [(big, HEAD, o_fq // HEAD, 1)], [fox_q_norm], [(d_fqn, HEAD, 0, 1)],
                                         [(0, BF16)], [0], name="d_fox_qnorm", ncb=HF)
    d_fk, G["fox_k_norm"] = rowwise_bwd(rms_fn, [(big, HEAD, o_fk // HEAD, 1)], [fox_k_norm], [(d_fkn, HEAD, 0, 1)],
                                         [(0, BF16)], [0], name="d_fox_knorm", ncb=HF)
    d_cum = jnp.pad(d_cum_t[:HF].T, ((0, 0), (LANE_F, LANES - LANE_F - HF)))
    d_logf = cumsum_tokens(d_cum, reverse=True, name="cumsum_rev")

    d_o_gdn, d_z, G["gdn_norm_g"] = rowwise_bwd(
        _gated_norm_fn, [(o_gdn, HEAD, 0, 1), (big, HEAD, o_z // HEAD, 1)], [gdn_norm_g], [(d_oa, HEAD, 0, 1)],
        [(0, F32), (1, BF16)], [0], name="d_gated_norm", ncb=HG)
    d_cn, d_bg = gdn_bwd(cn, bgf, ss, tinv, d_o_gdn, HG, name="gdn_bwd")
    d_conv_y, G["conv_w"] = conv_bwd_taps(big, conv_full, d_cn, HG, name="d_conv_taps")
    d_qkv = conv_bwd_input(d_conv_y, conv_full, name="d_conv_in")
    d_small, d_al, d_dt, d_bf = rowwise_bwd(_gates_fn, [full(small)], [a_log_l, dt_bias_l, b_f_l], [full(d_bg + d_logf)],
                                            [(0, F32)], [0, 1, 2], name="d_gates")
    G["a_log"], G["dt_bias"], G["fox_b_f"] = (d_al[:, LANE_A:LANE_A + HG], d_dt[:, LANE_A:LANE_A + HG],
                                               d_bf[:, LANE_F:LANE_F + HF])

    def parts(n):
        g = G[n].astype(BF16)
        if g.ndim == 3:
            return g
        return _col_shards(g) if n in col_sharded else g.reshape(N_DEV, -1, g.shape[1])

    recv = {}

    def carried(names, out):
        for n, r in zip(names, out):
            recv[n] = r

    d_big = jnp.concatenate([d_qkv, d_z, d_fq, d_fk, d_fv, d_mq, d_ga, d_gb, d_gm], axis=1)
    group = ["conv_w", "w_mem_kv", "w_up_gdn", "w_up_fox", "w_up_mem", "w_out"]
    G["w_ff2"], *out = matmul(a_ff, d_out, mode="tn", name="d_w_ff2",
                              comm=comm_direct([parts(n) for n in group], [True] * len(group)))
    carried(group, out)
    G["w_ff1"], *out = matmul(h2, d_u, mode="tn", name="d_w_ff1", out_shards=True,
                              comm=comm_direct([parts("w_ff2")], [True]))
    carried(["w_ff2"], out)
    g_big, *out = matmul(h, d_big, mode="tn", name="d_w_big", comm=comm_direct([parts("w_ff1")], [True]))
    carried(["w_ff1"], out)
    g_small = matmul(h, d_small, mode="tn", name="d_w_small", out_dtypes=(F32,))
    cols = lambda a, o, wd: a[:, o:o + wd]
    G["w_in"] = jnp.concatenate([
        cols(g_big, 0, GQKV), cols(g_big, o_z, GV), cols(g_small, LANE_B, HG), cols(g_small, LANE_A, HG),
        cols(g_big, o_fq, FW), cols(g_big, o_fk, FW), cols(g_big, o_fv, FW), cols(g_small, LANE_F, HF),
        cols(g_big, o_mq, MW), cols(g_big, o_gt, 3 * D)], axis=1)
    d_h_s = matmul(d_small, w_small, mode="nt", name="d_proj_small_in", out_dtypes=(F32,))
    d_h, *out = matmul(d_big, w_big, mode="nt", name="d_proj_big_in", extras=(d_h_s,), epi=lambda r, e: (r + e,),
                       comm=comm_direct([parts("w_in")], [True]))
    carried(["w_in"], out)
    grad_x, G["g_mix"] = rowwise_bwd(rms_fn, [full(xs)], [g_mix], [full(d_h)], [(0, F32)], [0], name="d_rms_mix",
                                     adds=[full(d_x1)])
    grad_x = grad_x[None]

    small_sizes = [loc[n].shape[1] for n in small_names]
    pack = lambda d: jnp.concatenate([d[n].reshape(1, -1) for n in small_names], axis=1)
    npad = -sum(small_sizes) % LANES
    padp = lambda a: jnp.pad(a, ((0, 0), (0, npad)))
    recv_small, = run_comm(comm_direct([padp(pack(G))], [False]), name="gather_small_grads")

    res = {}
    for n in big_names:
        res[n] = [t[None] for t in adamw(recv[n], loc[n][0], loc["m_" + n][0], loc["v_" + n][0], name="adamw_" + n)]
    sm = adamw(recv_small, padp(pack({n: loc[n] for n in small_names})), padp(pack({n: loc["m_" + n] for n in small_names})),
               padp(pack({n: loc["v_" + n] for n in small_names})), name="adamw_small")
    so = np.concatenate([[0], np.cumsum(small_sizes)]).tolist()
    for i, n in enumerate(small_names):
        res[n] = [t[:, so[i]:so[i + 1]] for t in sm]

    order = ["g_mix", "w_in", "conv_w", "a_log", "dt_bias", "gdn_norm_g", "fox_b_f", "fox_q_norm", "fox_k_norm", "g_mem",
             "w_mem_kv", "mem_q_norm", "mem_k_norm", "w_up_gdn", "w_up_fox", "w_up_mem", "w_out", "g_mlp", "w_ff1", "w_ff2"]
    return (loss, grad_x, *[res[n][0] for n in order], *[res[n][1] for n in order],
            *[res[n][2] for n in order], *[res[n][3] for n in order])
```

```python
import functools

import jax
import jax.numpy as jnp
import numpy as np
from jax import lax
from jax.experimental import pallas as pl
from jax.experimental.pallas import tpu as pltpu

F32 = jnp.float32
BF16 = jnp.bfloat16
HI = lax.Precision.HIGHEST

EPS = 1e-6
GDN_CHUNK = 64
GDN_CONV = 4
HEAD = 128
LANES = 128
HALO = 16
N_DEV = 8
MESH = pl.DeviceIdType.MESH
VMEM_LIMIT_V7X = 56 * 1024 * 1024

ADAM_LR, ADAM_B1, ADAM_B2, ADAM_EPS, ADAM_WD, ADAM_STEP = 0.001, 0.9, 0.999, 1e-08, 0.01, 10

LANE_B, LANE_A, LANE_F = 0, 8, 16


def _params(sem):
    return pltpu.CompilerParams(dimension_semantics=sem, vmem_limit_bytes=VMEM_LIMIT_V7X)


def _dg(a, b, ca, cb, prec):
    nb = a.ndim - 2
    batch = tuple(range(nb))
    return lax.dot_general(a, b, (((ca + nb,), (cb + nb,)), (batch, batch)), precision=prec,
                           preferred_element_type=F32)


def _make_mm(prec, cast):
    def c(x):
        return x.astype(BF16) if cast else x

    @jax.custom_vjp
    def nn(a, b):
        return _dg(c(a), c(b), 1, 0, prec)

    @jax.custom_vjp
    def nt(a, b):
        return _dg(c(a), c(b), 1, 1, prec)

    @jax.custom_vjp
    def tn(a, b):
        return _dg(c(a), c(b), 0, 0, prec)

    nn.defvjp(lambda a, b: (nn(a, b), (a, b)), lambda r, g: (nt(g, r[1]), tn(r[0], g)))
    nt.defvjp(lambda a, b: (nt(a, b), (a, b)), lambda r, g: (nn(g, r[1]), tn(g, r[0])))
    tn.defvjp(lambda a, b: (tn(a, b), (a, b)), lambda r, g: (nt(r[1], g), nn(r[0], g)))
    return nn, nt, tn


NN, NT, TN = _make_mm(None, True)
NNH, NTH, TNH = _make_mm(lax.Precision.HIGH, False)
NNX, _, _ = _make_mm(HI, False)


def _sigmoid(x):
    return 1.0 / (1.0 + jnp.exp(-x))


def _silu(x):
    return x * _sigmoid(x)


def _softplus(x):
    return jnp.maximum(x, 0.0) + jnp.log(1.0 + jnp.exp(-jnp.abs(x)))


def _log_sigmoid(x):
    return -_softplus(-x)


def _rms(x, g):
    return x * lax.rsqrt(jnp.mean(x * x, axis=-1, keepdims=True) + EPS) * g


def _tile(n, target):
    t = target
    while t >= LANES:
        if n % t == 0:
            return t
        t //= 2
    return n


def matmul(a, b, *, mode, name, out_dtypes=(BF16,), epi=None, extras=(), tm=1024, tn=1024, tk=2048, comm=None,
           b_shards=False, out_shards=False):
    if b_shards:
        b_rows, b_cols = b.shape[1], N_DEV * b.shape[2]
    else:
        b_rows, b_cols = b.shape
    if mode == "nn":
        (M, K), (K2, N) = a.shape, (b_rows, b_cols)
    elif mode == "nt":
        (M, K), (N, K2) = a.shape, (b_rows, b_cols)
    else:
        (K, M), (K2, N) = a.shape, (b_rows, b_cols)
    assert K == K2, (name, a.shape, b.shape)
    tm, tn, tk = _tile(M, tm), _tile(N, tn), _tile(K, tk)
    if b_shards and mode == "nt":
        tk = K // N_DEV
    if (b_shards and mode != "nt") or out_shards:
        tn = N // N_DEV
    ni, nj, nk = M // tm, N // tn, K // tk
    a_spec = (pl.BlockSpec((tk, tm), lambda i, j, k: (k, i)) if mode == "tn"
              else pl.BlockSpec((tm, tk), lambda i, j, k: (i, k)))
    if b_shards:
        b_spec = (pl.BlockSpec((None, tn, tk), lambda i, j, k: (k, j, 0)) if mode == "nt"
                  else pl.BlockSpec((None, tk, tn), lambda i, j, k: (j, k, 0)))
    else:
        b_spec = (pl.BlockSpec((tn, tk), lambda i, j, k: (j, k)) if mode == "nt"
                  else pl.BlockSpec((tk, tn), lambda i, j, k: (k, j)))
    o_spec = pl.BlockSpec((tm, tn), lambda i, j, k: (i, j))
    w_spec = pl.BlockSpec((None, tm, tn), lambda i, j, k: (j, i, 0)) if out_shards else o_spec
    w_shape = (N_DEV, M, tn) if out_shards else (M, N)
    dims = {"nn": ((1,), (0,)), "nt": ((1,), (1,)), "tn": ((0,), (0,))}[mode]
    ne, no = len(extras), len(out_dtypes)
    nc = len(comm.arrays) if comm is not None else 0
    n_steps = ni * nj * nk

    def body(a_ref, b_ref, *rest):
        ex, c_in = rest[:ne], rest[ne:ne + nc]
        outs, c_out = rest[ne + nc:ne + nc + no], rest[ne + nc + no:ne + nc + no + nc]
        acc = rest[ne + nc + no + nc]
        c_sems = rest[ne + nc + no + nc + 1:]
        k = pl.program_id(2)
        step = (pl.program_id(0) * nj + pl.program_id(1)) * nk + k

        if comm is not None:
            pl.when(step == 0)(lambda: comm.start(c_in, c_out, c_sems))
            if comm.forward is not None and n_steps >= 3:
                pl.when(step == (3 * n_steps) // 4)(lambda: comm.forward(c_in, c_out, c_sems))

        @pl.when(k == 0)
        def _():
            acc[...] = jnp.zeros_like(acc)

        acc[...] += lax.dot_general(a_ref[...].astype(BF16), b_ref[...].astype(BF16), (dims, ((), ())),
                                    preferred_element_type=F32)

        @pl.when(k == nk - 1)
        def _():
            r = acc[...]
            vals = epi(r, *[e[...] for e in ex]) if epi is not None else (r,)
            for o, v in zip(outs, vals):
                o[...] = v.astype(o.dtype)

        if comm is not None:
            @pl.when(step == n_steps - 1)
            def _():
                if comm.forward is not None and n_steps < 3:
                    comm.forward(c_in, c_out, c_sems)
                comm.finish(c_in, c_out, c_sems)

    any_spec = pl.BlockSpec(memory_space=pl.ANY)
    sem = ("arbitrary",) * 3 if comm is not None else ("parallel", "parallel", "arbitrary")
    out = pl.pallas_call(
        body, name=name, grid=(ni, nj, nk),
        in_specs=[a_spec, b_spec] + [o_spec] * ne + [any_spec] * nc, out_specs=[w_spec] * no + [any_spec] * nc,
        out_shape=[jax.ShapeDtypeStruct(w_shape, d) for d in out_dtypes] + (comm.out_shape if comm is not None else []),
        scratch_shapes=[pltpu.VMEM((tm, tn), F32)] + (comm.sems if comm is not None else []),
        compiler_params=_params(sem))(a, b, *extras, *(comm.arrays if comm is not None else []))
    return out[0] if len(out) == 1 else out


def _row_spec(tr, cb, off, moves):
    return pl.BlockSpec((tr, cb), lambda i, j: (i, off + moves * j))


def _whole_spec(p):
    return pl.BlockSpec(p.shape, lambda i, j: (0,) * p.ndim)


def _row_tile(T, rows):
    widest = max(cb for (_, cb, _, _) in rows)
    return min(T, max(512, (1 << 19) // widest))


def rowwise(fn, rows, params, outs, *, name, ncb=1, tr=None):
    T = rows[0][0].shape[0]
    tr = min(tr, T) if tr else _row_tile(T, rows)
    assert T % tr == 0
    nr, npar = len(rows), len(params)

    def body(*refs):
        r, p, o = refs[:nr], refs[nr:nr + npar], refs[nr + npar:]
        vals = fn(*[x[...].astype(F32) for x in r], *[x[...] for x in p])
        for oref, v in zip(o, vals):
            oref[...] = v.astype(oref.dtype)

    res = pl.pallas_call(
        body, name=name, grid=(T // tr, ncb),
        in_specs=[_row_spec(tr, cb, off, mv) for (_, cb, off, mv) in rows] + [_whole_spec(p) for p in params],
        out_specs=[_row_spec(tr, cb, 0, 1) for (_, cb, _) in outs],
        out_shape=[jax.ShapeDtypeStruct((T, cols), d) for (cols, _, d) in outs],
        compiler_params=_params(("parallel", "parallel")))(*[r[0] for r in rows], *params)
    return res


def rowwise_bwd(fn, rows, params, cots, drows, dparams, *, name, ncb=1, tr=None, adds=()):
    T = rows[0][0].shape[0]
    tr = min(tr, T) if tr else _row_tile(T, rows)
    assert T % tr == 0
    nr, npar, nc, ndr, na = len(rows), len(params), len(cots), len(drows), len(adds)

    def body(*refs):
        r, p, c = refs[:nr], refs[nr:nr + npar], refs[nr + npar:nr + npar + nc]
        base = nr + npar + nc + na
        ad, o_r, o_p = refs[base - na:base], refs[base:base + ndr], refs[base + ndr:]
        prim = [x[...].astype(F32) for x in r] + [x[...] for x in p]
        _, vjp = jax.vjp(lambda *a: tuple(fn(*a)), *prim)
        g = vjp(tuple(x[...].astype(F32) for x in c))
        for k, (oref, (idx, _)) in enumerate(zip(o_r, drows)):
            val = g[idx] + ad[k][...].astype(F32) if k < na else g[idx]
            oref[...] = val.astype(oref.dtype)
        first = jnp.logical_and(pl.program_id(0) == 0, pl.program_id(1) == 0)

        @pl.when(first)
        def _():
            for oref in o_p:
                oref[...] = jnp.zeros_like(oref)

        for oref, idx in zip(o_p, dparams):
            oref[...] += g[nr + idx]

    res = pl.pallas_call(
        body, name=name, grid=(T // tr, ncb),
        in_specs=([_row_spec(tr, cb, off, mv) for (_, cb, off, mv) in rows] + [_whole_spec(p) for p in params]
                  + [_row_spec(tr, cb, off, mv) for (_, cb, off, mv) in tuple(cots) + tuple(adds)]),
        out_specs=([_row_spec(tr, rows[idx][1], 0, 1) for (idx, _) in drows]
                   + [_whole_spec(params[idx]) for idx in dparams]),
        out_shape=([jax.ShapeDtypeStruct((T, ncb * rows[idx][1] if rows[idx][3] else rows[idx][1]), d)
                    for (idx, d) in drows]
                   + [jax.ShapeDtypeStruct(params[idx].shape, F32) for idx in dparams]),
        compiler_params=_params(("arbitrary", "arbitrary")))(
            *[r[0] for r in rows], *params, *[c[0] for c in cots], *[a[0] for a in adds])
    return res


def full(a):
    return (a, a.shape[1], 0, 0)


def cumsum_tokens(x, *, reverse, name, tb=256):
    T = x.shape[0]
    tb = min(tb, T)
    nb = T // tb
    idx = (lambda i: (nb - 1 - i, 0)) if reverse else (lambda i: (i, 0))

    def body(x_ref, o_ref, carry):
        @pl.when(pl.program_id(0) == 0)
        def _():
            carry[...] = jnp.zeros_like(carry)

        ii = lax.broadcasted_iota(jnp.int32, (tb, tb), 0)
        jj = lax.broadcasted_iota(jnp.int32, (tb, tb), 1)
        tri = ((ii <= jj) if reverse else (ii >= jj)).astype(F32)
        c = lax.dot_general(tri, x_ref[...], (((1,), (0,)), ((), ())), precision=HI,
                            preferred_element_type=F32) + carry[0:1, :]
        o_ref[...] = c
        carry[0:1, :] = c[0:1, :] if reverse else c[tb - 1:tb, :]

    return pl.pallas_call(
        body, name=name, grid=(nb,), in_specs=[pl.BlockSpec((tb, LANES), idx)],
        out_specs=pl.BlockSpec((tb, LANES), idx), out_shape=jax.ShapeDtypeStruct((T, LANES), F32),
        scratch_shapes=[pltpu.VMEM((8, LANES), F32)], compiler_params=_params(("arbitrary",)))(x)


def _conv_post(y, kind, dk):
    c = _silu(y)
    r = lax.rsqrt(jnp.sum(c * c, axis=-1, keepdims=True) + EPS)
    return jnp.where(kind == 0, c * r * (dk ** -0.5), jnp.where(kind == 1, c * r, c))


def _conv_taps(cur, prev, w, tr):
    ext = jnp.concatenate([prev, cur], axis=0)
    y = w[3:4, :] * cur
    for d in (1, 2, 3):
        y = y + w[3 - d:4 - d, :] * pltpu.roll(ext, d, 0)[HALO:HALO + tr]
    return y


def conv_fwd(big, w, n_qk_heads, *, name, tr=2048):
    T, W = big.shape[0], w.shape[1]
    tr = min(tr, T)
    nh = W // HEAD

    def body(cur_ref, prev_ref, w_ref, o_ref):
        i, j = pl.program_id(0), pl.program_id(1)
        prev = jnp.where(i > 0, prev_ref[...].astype(F32), 0.0)
        y = _conv_taps(cur_ref[...].astype(F32), prev, w_ref[...], tr)
        kind = jnp.where(j < n_qk_heads, 0, jnp.where(j < 2 * n_qk_heads, 1, 2))
        o_ref[...] = _conv_post(y, kind, HEAD)

    return pl.pallas_call(
        body, name=name, grid=(T // tr, nh),
        in_specs=[pl.BlockSpec((tr, HEAD), lambda i, j: (i, j)),
                  pl.BlockSpec((HALO, HEAD), lambda i, j: (jnp.maximum(i * (tr // HALO) - 1, 0), j)),
                  pl.BlockSpec((GDN_CONV, HEAD), lambda i, j: (0, j))],
        out_specs=pl.BlockSpec((tr, HEAD), lambda i, j: (i, j)),
        out_shape=jax.ShapeDtypeStruct((T, W), F32), compiler_params=_params(("parallel", "parallel")))(big, big, w)


def conv_bwd_taps(big, w, dcn, n_qk_heads, *, name, tr=2048):
    T, W = big.shape[0], w.shape[1]
    tr = min(tr, T)
    nh = W // HEAD

    def body(cur_ref, prev_ref, w_ref, g_ref, dy_ref, dw_ref):
        j, i = pl.program_id(0), pl.program_id(1)
        cur = cur_ref[...].astype(F32)
        prev = jnp.where(i > 0, prev_ref[...].astype(F32), 0.0)
        y = _conv_taps(cur, prev, w_ref[...], tr)
        kind = jnp.where(j < n_qk_heads, 0, jnp.where(j < 2 * n_qk_heads, 1, 2))
        _, vjp = jax.vjp(lambda t: _conv_post(t, kind, HEAD), y)
        dy, = vjp(g_ref[...])
        dy_ref[...] = dy
        ext = jnp.concatenate([prev, cur], axis=0)
        rows = [jnp.sum(dy * (cur if d == 0 else pltpu.roll(ext, d, 0)[HALO:HALO + tr]), axis=0, keepdims=True)
                for d in (3, 2, 1, 0)]

        @pl.when(i == 0)
        def _():
            dw_ref[...] = jnp.zeros_like(dw_ref)

        dw_ref[...] += jnp.concatenate(rows, axis=0)

    return pl.pallas_call(
        body, name=name, grid=(nh, T // tr),
        in_specs=[pl.BlockSpec((tr, HEAD), lambda j, i: (i, j)),
                  pl.BlockSpec((HALO, HEAD), lambda j, i: (jnp.maximum(i * (tr // HALO) - 1, 0), j)),
                  pl.BlockSpec((GDN_CONV, HEAD), lambda j, i: (0, j)),
                  pl.BlockSpec((tr, HEAD), lambda j, i: (i, j))],
        out_specs=[pl.BlockSpec((tr, HEAD), lambda j, i: (i, j)), pl.BlockSpec((GDN_CONV, HEAD), lambda j, i: (0, j))],
        out_shape=[jax.ShapeDtypeStruct((T, W), F32), jax.ShapeDtypeStruct((GDN_CONV, W), F32)],
        compiler_params=_params(("parallel", "arbitrary")))(big, big, w, dcn)


def conv_bwd_input(dy, w, *, name, tr=2048):
    T, W = dy.shape
    tr = min(tr, T)
    nrow = T // tr

    def body(cur_ref, nxt_ref, w_ref, o_ref):
        i = pl.program_id(0)
        cur = cur_ref[...]
        nxt = jnp.where(i < nrow - 1, nxt_ref[...], 0.0)
        ext = jnp.concatenate([cur, nxt], axis=0)
        w = w_ref[...]
        dx = w[3:4, :] * cur
        for d in (1, 2, 3):
            dx = dx + w[3 - d:4 - d, :] * pltpu.roll(ext, tr + HALO - d, 0)[0:tr]
        o_ref[...] = dx.astype(o_ref.dtype)

    return pl.pallas_call(
        body, name=name, grid=(nrow, W // HEAD),
        in_specs=[pl.BlockSpec((tr, HEAD), lambda i, j: (i, j)),
                  pl.BlockSpec((HALO, HEAD), lambda i, j: (jnp.minimum((i + 1) * (tr // HALO), T // HALO - 1), j)),
                  pl.BlockSpec((GDN_CONV, HEAD), lambda i, j: (0, j))],
        out_specs=pl.BlockSpec((tr, HEAD), lambda i, j: (i, j)),
        out_shape=jax.ShapeDtypeStruct((T, W), BF16), compiler_params=_params(("parallel", "parallel")))(dy, dy, w)


INV_BLOCK = 16


def _unit_lower_inverse_value(p):
    C = p.shape[-1]
    ii = lax.broadcasted_iota(jnp.int32, (C, C), 0)
    jj = lax.broadcasted_iota(jnp.int32, (C, C), 1)
    eye = jnp.where((ii == jj)[None], 1.0, 0.0)
    same = ((ii // INV_BLOCK) == (jj // INV_BLOCK))[None]
    pd = jnp.where(same, p, 0.0)
    d_inv = eye + pd
    n = 2
    while n < INV_BLOCK:
        pd = NNH(pd, pd)
        d_inv = d_inv + NNH(d_inv, pd)
        n *= 2
    nb = NNH(d_inv, jnp.where(same, 0.0, p))
    t = eye + nb
    n = 2
    while n < C // INV_BLOCK:
        nb = NNH(nb, nb)
        t = t + NNH(t, nb)
        n *= 2
    return NNH(t, d_inv)


@jax.custom_vjp
def _unit_lower_inverse(p, t_known):
    return _unit_lower_inverse_value(p) if t_known is None else t_known


def _unit_lower_inverse_fwd(p, t_known):
    t = _unit_lower_inverse(p, t_known)
    return t, (t, t_known is not None)


def _unit_lower_inverse_bwd(res, g):
    t, had = res
    return NTH(TNH(t, g), t), (jnp.zeros_like(t) if had else None)


_unit_lower_inverse.defvjp(_unit_lower_inverse_fwd, _unit_lower_inverse_bwd)


def _gdn_chunk(q, k, v, bg, S, t_known=None):
    H, C = q.shape[0], q.shape[1]
    ii = lax.broadcasted_iota(jnp.int32, (C, C), 0)
    jj = lax.broadcasted_iota(jnp.int32, (C, C), 1)
    lincl = (ii >= jj).astype(F32)
    strict, incl, eye = (ii > jj)[None], (ii >= jj)[None], (ii == jj)[None]
    gam2d = lax.dot_general(lincl, bg, (((1,), (0,)), ((), ())), precision=HI, preferred_element_type=F32)
    lane = lax.broadcasted_iota(jnp.int32, (H, 1, LANES), 2)
    hh = lax.broadcasted_iota(jnp.int32, (H, 1, LANES), 0)
    beta = jnp.sum(bg[None] * (lane == hh + LANE_B).astype(F32), axis=2, keepdims=True)
    gam = jnp.sum(gam2d[None] * (lane == hh + LANE_A).astype(F32), axis=2, keepdims=True)
    last = (lax.broadcasted_iota(jnp.int32, (1, C, 1), 1) == C - 1).astype(F32)
    gam_last = jnp.sum(gam * last, axis=1, keepdims=True)
    gam_row = NNX(jnp.ones((H, C, C), F32), jnp.where(eye, gam, 0.0))
    diff = gam - gam_row
    dec_s = jnp.where(strict, jnp.exp(jnp.where(strict, diff, 0.0)), 0.0)
    dec_i = jnp.where(incl, jnp.exp(jnp.where(incl, diff, 0.0)), 0.0)
    t = _unit_lower_inverse(-(beta * NT(k, k) * dec_s), t_known)
    eg = jnp.exp(gam)
    wu = NNH(t, jnp.concatenate([beta * eg * k, beta * v], axis=-1))
    w, u0 = wu[..., :HEAD], wu[..., HEAD:]
    qk = NT(q, k) * dec_i
    u = u0 - NN(w, S)
    o = NN(jnp.concatenate([q * eg, qk], axis=-1), jnp.concatenate([S, u], axis=-2))
    S2 = jnp.exp(gam_last) * S + TN(k * jnp.exp(gam_last - gam), u)
    return o, S2, t


def _heads(x, base, H):
    return jnp.stack([x[:, base + h * HEAD:base + (h + 1) * HEAD] for h in range(H)])


def _unheads(x):
    return jnp.concatenate([x[h] for h in range(x.shape[0])], axis=1)


def gdn_fwd(cn, bgf, H, *, name):
    T, C, W = cn.shape[0], GDN_CHUNK, H * HEAD
    N = T // C

    def body(cn_ref, bg_ref, o_ref, ss_ref, t_ref, s_scr):
        @pl.when(pl.program_id(0) == 0)
        def _():
            s_scr[...] = jnp.zeros_like(s_scr)

        x, S = cn_ref[...], s_scr[...]
        ss_ref[0] = S
        o, S2, t = _gdn_chunk(_heads(x, 0, H), _heads(x, W, H), _heads(x, 2 * W, H), bg_ref[...], S)
        o_ref[...] = _unheads(o)
        t_ref[0] = t
        s_scr[...] = S2

    return pl.pallas_call(
        body, name=name, grid=(N,),
        in_specs=[pl.BlockSpec((C, 3 * W), lambda n: (n, 0)), pl.BlockSpec((C, LANES), lambda n: (n, 0))],
        out_specs=[pl.BlockSpec((C, W), lambda n: (n, 0)), pl.BlockSpec((1, H, HEAD, HEAD), lambda n: (n, 0, 0, 0)),
                   pl.BlockSpec((1, H, C, C), lambda n: (n, 0, 0, 0))],
        out_shape=[jax.ShapeDtypeStruct((T, W), F32), jax.ShapeDtypeStruct((N, H, HEAD, HEAD), F32),
                   jax.ShapeDtypeStruct((N, H, C, C), F32)],
        scratch_shapes=[pltpu.VMEM((H, HEAD, HEAD), F32)], compiler_params=_params(("arbitrary",)))(cn, bgf)


def gdn_bwd(cn, bgf, ss, tinv, do, H, *, name):
    T, C, W = cn.shape[0], GDN_CHUNK, H * HEAD
    N = T // C

    def body(cn_ref, bg_ref, ss_ref, t_ref, do_ref, dcn_ref, dbg_ref, ds_scr):
        @pl.when(pl.program_id(0) == 0)
        def _():
            ds_scr[...] = jnp.zeros_like(ds_scr)

        x = cn_ref[...]
        t_known = t_ref[0]
        _, vjp = jax.vjp(lambda *a: _gdn_chunk(*a, t_known)[:2],
                         _heads(x, 0, H), _heads(x, W, H), _heads(x, 2 * W, H), bg_ref[...], ss_ref[0])
        dq, dk, dv, dbg, dS = vjp((_heads(do_ref[...], 0, H), ds_scr[...]))
        dcn_ref[...] = jnp.concatenate([_unheads(dq), _unheads(dk), _unheads(dv)], axis=1)
        dbg_ref[...] = dbg
        ds_scr[...] = dS

    rev = lambda n: (N - 1 - n, 0)
    return pl.pallas_call(
        body, name=name, grid=(N,),
        in_specs=[pl.BlockSpec((C, 3 * W), rev), pl.BlockSpec((C, LANES), rev),
                  pl.BlockSpec((1, H, HEAD, HEAD), lambda n: (N - 1 - n, 0, 0, 0)),
                  pl.BlockSpec((1, H, C, C), lambda n: (N - 1 - n, 0, 0, 0)), pl.BlockSpec((C, W), rev)],
        out_specs=[pl.BlockSpec((C, 3 * W), rev), pl.BlockSpec((C, LANES), rev)],
        out_shape=[jax.ShapeDtypeStruct((T, 3 * W), F32), jax.ShapeDtypeStruct((T, LANES), F32)],
        scratch_shapes=[pltpu.VMEM((H, HEAD, HEAD), F32)], compiler_params=_params(("arbitrary",)))(cn, bgf, ss, tinv, do)


AUG = 2 * HEAD


def fox_prep(big, col_off, gain, cum, kind, H, *, name, tr=2048):
    T = big.shape[0]
    tr = min(tr, T)

    def body(x_ref, g_ref, c_ref, o_ref):
        h = pl.program_id(1)
        x = x_ref[...].astype(F32)
        lane = lax.broadcasted_iota(jnp.int32, (tr, HEAD), 1)
        if kind == 2:
            main, aug = x, jnp.ones((tr, HEAD), F32)
        else:
            main = _rms(x, g_ref[...]) * ((HEAD ** -0.5) if kind == 0 else 1.0)
            c = jnp.sum(jnp.where(lane == LANE_F + h, c_ref[...], 0.0), axis=1, keepdims=True)
            hi = c.astype(BF16).astype(F32)
            mid = (c - hi).astype(BF16).astype(F32)
            lo = c - hi - mid
            if kind == 0:
                aug = jnp.where(lane == 0, hi, jnp.where(lane == 1, mid, jnp.where(lane == 2, lo,
                                                                                   jnp.where(lane < 6, 1.0, 0.0))))
            else:
                aug = jnp.where(lane < 3, 1.0, jnp.where(lane == 3, -hi, jnp.where(lane == 4, -mid,
                                                                                   jnp.where(lane == 5, -lo, 0.0))))
        o_ref[...] = jnp.concatenate([main, aug], axis=1).astype(o_ref.dtype)

    return pl.pallas_call(
        body, name=name, grid=(T // tr, H),
        in_specs=[pl.BlockSpec((tr, HEAD), lambda i, h: (i, col_off // HEAD + h)),
                  pl.BlockSpec((1, HEAD), lambda i, h: (0, 0)), pl.BlockSpec((tr, LANES), lambda i, h: (i, 0))],
        out_specs=pl.BlockSpec((tr, AUG), lambda i, h: (i, h)),
        out_shape=jax.ShapeDtypeStruct((T, H * AUG), BF16), compiler_params=_params(("parallel", "parallel")))(big, gain, cum)


def _fox_logits(q_ref, k_ref, h, tq, tk, diagonal):
    ha = slice(h * AUG, (h + 1) * AUG)
    s = lax.dot_general(q_ref[:, ha], k_ref[:, ha], (((1,), (1,)), ((), ())), preferred_element_type=F32)
    keep = None
    if diagonal:
        keep = lax.broadcasted_iota(jnp.int32, (tq, tk), 0) >= lax.broadcasted_iota(jnp.int32, (tq, tk), 1)
    return s, keep


def _dispatch(live_ref, H, i, j, below, on_diagonal, head_fn, after=None):
    def straight(diagonal):
        for h in range(H):
            head_fn(h, diagonal)
        if after is not None:
            after()

    def by_head():
        for h in range(H):
            pl.when(live_ref[h, i, j] != 0)(functools.partial(head_fn, h, False))
        if after is not None:
            after()

    pl.when(jnp.logical_and(below, live_ref[H, i, j] == 2))(functools.partial(straight, False))
    pl.when(jnp.logical_and(below, live_ref[H, i, j] == 1))(by_head)
    pl.when(on_diagonal)(functools.partial(straight, True))


FOX_TILE = 512
EXP_UNDERFLOW = -100.0


def fox_live_tiles(cum, q_gain, k_gain, H, T):
    t = min(FOX_TILE, T)
    n = T // t
    c = cum[:, LANE_F:LANE_F + H]
    bias = c[0::t][:, None, :] - c[t - 1::t][None, :, :]
    bound = 1.02 * (HEAD ** 0.5) * jnp.max(jnp.abs(q_gain)) * jnp.max(jnp.abs(k_gain))
    causal = (jnp.arange(n)[:, None] >= jnp.arange(n)[None, :])[:, :, None]
    live = jnp.logical_and(causal, 2.0 * bound + bias >= EXP_UNDERFLOW)
    status = jnp.where(jnp.all(live, axis=2), 2, jnp.where(jnp.any(live, axis=2), 1, 0))
    return jnp.concatenate([jnp.transpose(live, (2, 0, 1)).astype(jnp.int32), status[None].astype(jnp.int32)], axis=0)


def fox_fwd(qa, ka, va, live, H, *, name):
    T = qa.shape[0]
    tq = tk = min(FOX_TILE, T)
    nq, nk = T // tq, T // tk

    def body(live_ref, q_ref, k_ref, v_ref, o_ref, lse_ref, acc, m_scr):
        i, j = pl.program_id(0), pl.program_id(1)

        @pl.when(j == 0)
        def _():
            m_scr[...] = jnp.full_like(m_scr, -jnp.inf)
            acc[...] = jnp.zeros_like(acc)

        def head(h, diagonal):
            ha = slice(h * AUG, (h + 1) * AUG)
            s, keep = _fox_logits(q_ref, k_ref, h, tq, tk, diagonal)
            if diagonal:
                s = jnp.where(keep, s, -jnp.inf)
            m_prev = m_scr[h]
            m_new = jnp.maximum(m_prev, jnp.max(s, axis=1, keepdims=True))
            p = jnp.exp(s - m_new[:, 0:1])
            p_hi = p.astype(BF16)
            p_lo = (p - p_hi.astype(F32)).astype(BF16)
            pv = lambda t: lax.dot_general(t, v_ref[:, ha], (((1,), (0,)), ((), ())), preferred_element_type=F32)
            acc[:, ha] = jnp.exp(m_prev[:, 0:1] - m_new[:, 0:1]) * acc[:, ha] + (pv(p_hi) + pv(p_lo))
            m_scr[h] = m_new

        _dispatch(live_ref, H, i, j, j < i, j == i, head)

        @pl.when(j == nk - 1)
        def _():
            lane = lax.broadcasted_iota(jnp.int32, (tq, LANES), 1)
            lse = jnp.zeros((tq, LANES), F32)
            for h in range(H):
                den = acc[:, h * AUG + HEAD:(h + 1) * AUG]
                o_ref[:, h * HEAD:(h + 1) * HEAD] = acc[:, h * AUG:h * AUG + HEAD] / den
                lse = jnp.where(lane == h, m_scr[h] + jnp.log(den), lse)
            lse_ref[...] = lse

    kv_idx = lambda i, j, f: (jnp.minimum(j, i), 0)
    q_idx = lambda i, j, f: (i, 0)
    return pl.pallas_call(
        body, name=name,
        grid_spec=pltpu.PrefetchScalarGridSpec(
            num_scalar_prefetch=1, grid=(nq, nk),
            in_specs=[pl.BlockSpec((tq, H * AUG), q_idx), pl.BlockSpec((tk, H * AUG), kv_idx), pl.BlockSpec((tk, H * AUG), kv_idx)],
            out_specs=[pl.BlockSpec((tq, H * HEAD), q_idx), pl.BlockSpec((tq, LANES), q_idx)],
            scratch_shapes=[pltpu.VMEM((tq, H * AUG), F32), pltpu.VMEM((H, tq, LANES), F32)]),
        out_shape=[jax.ShapeDtypeStruct((T, H * HEAD), F32), jax.ShapeDtypeStruct((T, LANES), F32)],
        compiler_params=_params(("parallel", "arbitrary")))(live, qa, ka, va)


def _fox_ds(q_ref, k_ref, v_ref, do_ref, o_ref, lse_ref, h, tq, tk, diagonal):
    hs = slice(h * HEAD, (h + 1) * HEAD)
    s, keep = _fox_logits(q_ref, k_ref, h, tq, tk, diagonal)
    p = jnp.exp(s - lse_ref[:, h:h + 1])
    if diagonal:
        p = jnp.where(keep, p, 0.0)
    do = do_ref[:, hs]
    dp = lax.dot_general(do, v_ref[:, h * AUG:h * AUG + HEAD], (((1,), (1,)), ((), ())), preferred_element_type=F32)
    delta = jnp.sum(do.astype(F32) * o_ref[:, hs], axis=1, keepdims=True)
    return p, p * (dp - delta)


def fox_bwd_kv(qa, ka, va, do, o, lse, live, H, *, name):
    T, W = do.shape
    tq = tk = min(FOX_TILE, T)
    nq, nk = T // tq, T // tk

    def body(live_ref, q_ref, k_ref, v_ref, do_ref, o_ref, lse_ref, dk_ref, dv_ref, dc_ref, dk_acc, dv_acc, dc_acc):
        j, i = pl.program_id(0), pl.program_id(1)

        @pl.when(i == 0)
        def _():
            dk_acc[...] = jnp.zeros_like(dk_acc)
            dv_acc[...] = jnp.zeros_like(dv_acc)
            dc_acc[...] = jnp.zeros_like(dc_acc)

        def head(h, diagonal):
            hs = slice(h * HEAD, (h + 1) * HEAD)
            p, ds = _fox_ds(q_ref, k_ref, v_ref, do_ref, o_ref, lse_ref, h, tq, tk, diagonal)
            dv_acc[:, hs] += lax.dot_general(p.astype(BF16), do_ref[:, hs], (((0,), (0,)), ((), ())),
                                             preferred_element_type=F32)
            dk_acc[:, hs] += lax.dot_general(ds.astype(BF16), q_ref[:, h * AUG:h * AUG + HEAD],
                                             (((0,), (0,)), ((), ())), preferred_element_type=F32)
            dc_acc[h] -= jnp.broadcast_to(jnp.sum(ds, axis=0, keepdims=True), (8, tk))

        _dispatch(live_ref, H, i, j, i > j, i == j, head)

        @pl.when(i == nq - 1)
        def _():
            dk_ref[...] = dk_acc[...].astype(dk_ref.dtype)
            dv_ref[...] = dv_acc[...].astype(dv_ref.dtype)
            row = lax.broadcasted_iota(jnp.int32, (8, tk), 0)
            dc = jnp.zeros((8, tk), F32)
            for h in range(H):
                dc = jnp.where(row == h, dc_acc[h], dc)
            dc_ref[...] = dc

    q_idx = lambda j, i, f: (jnp.maximum(i, j), 0)
    kv_idx = lambda j, i, f: (j, 0)
    return pl.pallas_call(
        body, name=name,
        grid_spec=pltpu.PrefetchScalarGridSpec(
            num_scalar_prefetch=1, grid=(nk, nq),
            in_specs=[pl.BlockSpec((tq, H * AUG), q_idx), pl.BlockSpec((tk, H * AUG), kv_idx), pl.BlockSpec((tk, H * AUG), kv_idx),
                      pl.BlockSpec((tq, W), q_idx), pl.BlockSpec((tq, W), q_idx), pl.BlockSpec((tq, LANES), q_idx)],
            out_specs=[pl.BlockSpec((tk, W), kv_idx), pl.BlockSpec((tk, W), kv_idx),
                       pl.BlockSpec((8, tk), lambda j, i, f: (0, j))],
            scratch_shapes=[pltpu.VMEM((tk, W), F32), pltpu.VMEM((tk, W), F32), pltpu.VMEM((H, 8, tk), F32)]),
        out_shape=[jax.ShapeDtypeStruct((T, W), BF16), jax.ShapeDtypeStruct((T, W), BF16), jax.ShapeDtypeStruct((8, T), F32)],
        compiler_params=_params(("parallel", "arbitrary")))(live, qa, ka, va, do, o, lse)


def fox_bwd_q(qa, ka, va, do, o, lse, live, H, *, name):
    T, W = do.shape
    tq = tk = min(FOX_TILE, T)
    nq, nk = T // tq, T // tk

    def body(live_ref, q_ref, k_ref, v_ref, do_ref, o_ref, lse_ref, dq_ref, dq_acc):
        i, j = pl.program_id(0), pl.program_id(1)

        @pl.when(j == 0)
        def _():
            dq_acc[...] = jnp.zeros_like(dq_acc)

        def head(h, diagonal):
            hs = slice(h * HEAD, (h + 1) * HEAD)
            _, ds = _fox_ds(q_ref, k_ref, v_ref, do_ref, o_ref, lse_ref, h, tq, tk, diagonal)
            dq_acc[:, hs] += lax.dot_general(ds.astype(BF16), k_ref[:, h * AUG:h * AUG + HEAD],
                                             (((1,), (0,)), ((), ())), preferred_element_type=F32)

        _dispatch(live_ref, H, i, j, j < i, j == i, head)

        @pl.when(j == nk - 1)
        def _():
            dq_ref[...] = (dq_acc[...] * (HEAD ** -0.5)).astype(dq_ref.dtype)

    q_idx = lambda i, j, f: (i, 0)
    kv_idx = lambda i, j, f: (jnp.minimum(j, i), 0)
    return pl.pallas_call(
        body, name=name,
        grid_spec=pltpu.PrefetchScalarGridSpec(
            num_scalar_prefetch=1, grid=(nq, nk),
            in_specs=[pl.BlockSpec((tq, H * AUG), q_idx), pl.BlockSpec((tk, H * AUG), kv_idx), pl.BlockSpec((tk, H * AUG), kv_idx),
                      pl.BlockSpec((tq, W), q_idx), pl.BlockSpec((tq, W), q_idx), pl.BlockSpec((tq, LANES), q_idx)],
            out_specs=pl.BlockSpec((tq, W), q_idx),
            scratch_shapes=[pltpu.VMEM((tq, W), F32)]),
        out_shape=jax.ShapeDtypeStruct((T, W), BF16),
        compiler_params=_params(("parallel", "arbitrary")))(live, qa, ka, va, do, o, lse)


def _gates_fn(small, a_log_l, dt_bias_l, b_f_l):
    lane = lax.broadcasted_iota(jnp.int32, small.shape, 1)
    beta = _sigmoid(small)
    g = -jnp.exp(a_log_l) * _softplus(small + dt_bias_l)
    lf = _log_sigmoid(small + b_f_l)
    return (jnp.where(lane < LANE_A, beta, jnp.where(lane < LANE_F, g, jnp.where(lane < LANE_F + 8, lf, 0.0))),)


def _gated_norm_fn(o, z, g):
    return (_rms(o, g) * _silu(z),)


def _merge_fn(ya, yb, ym, ga, gb, gm):
    return (_sigmoid(ga) * ya + _sigmoid(gb) * yb + _sigmoid(gm) * ym,)


def _mem_attn_fn(nh, dh, mq, kn, v, gq):
    outs = []
    for h in range(nh):
        hs = slice(h * dh, (h + 1) * dh)
        qn = _rms(mq[:, hs], gq)
        s = NT(qn, kn[:, hs]) * (dh ** -0.5)
        e = jnp.exp(s - jnp.max(s, axis=1, keepdims=True))
        p = e / jnp.sum(e, axis=1, keepdims=True)
        outs.append(NN(p, v[:, hs]))
    return (jnp.concatenate(outs, axis=1),)


def sum_squares(x, *, name, tr=512):
    T, D = x.shape
    tr = min(tr, T)

    def body(x_ref, o_ref):
        @pl.when(pl.program_id(0) == 0)
        def _():
            o_ref[...] = jnp.zeros_like(o_ref)

        v = x_ref[...]
        o_ref[...] += jnp.sum(jnp.sum(v * v, axis=1, keepdims=True), axis=0, keepdims=True)

    return pl.pallas_call(
        body, name=name, grid=(T // tr,), in_specs=[pl.BlockSpec((tr, D), lambda i: (i, 0))],
        out_specs=pl.BlockSpec((1, LANES), lambda i: (0, 0)), out_shape=jax.ShapeDtypeStruct((1, LANES), F32),
        compiler_params=_params(("arbitrary",)))(x)


class Comm:
    def __init__(self, arrays, out_shape, sems, start, forward, finish):
        self.arrays, self.out_shape, self.sems = list(arrays), list(out_shape), list(sems)
        self.start, self.forward, self.finish = start, forward, finish


def _place():
    x, y, c = lax.axis_index("x"), lax.axis_index("y"), lax.axis_index("c")
    chips = [(1 - x, y), (x, 1 - y), (1 - x, 1 - y)]
    return x, y, c, chips


def comm_gather(arrays):
    n = len(arrays)
    lin = lambda px, py, pc: 4 * px + 2 * py + pc

    def copy(ins, outs, sems, a, k, block, to, src=None):
        slot = outs[a].at[lin(*block)]
        return pltpu.make_async_remote_copy(src_ref=slot if src is None else src, dst_ref=slot, send_sem=sems[0].at[a, k],
                                            recv_sem=sems[1].at[a, k], device_id=to, device_id_type=MESH)

    def local(ins, outs, sems, a):
        x, y, c, _ = _place()
        return pltpu.make_async_copy(ins[a], outs[a].at[lin(x, y, c)], sems[2].at[a])

    def start(ins, outs, sems):
        x, y, c, chips = _place()
        for a in range(n):
            local(ins, outs, sems, a).start()
        for a in range(n):
            for j, chip in enumerate(chips):
                copy(ins, outs, sems, a, 1 + j, (x, y, c), (*chip, c), src=ins[a]).start()
            copy(ins, outs, sems, a, 0, (x, y, c), (x, y, 1 - c), src=ins[a]).start()

    def forward(ins, outs, sems):
        x, y, c, chips = _place()
        for a in range(n):
            for j, chip in enumerate(chips):
                copy(ins, outs, sems, a, 1 + j, (*chip, c), (x, y, c)).wait_recv()
                copy(ins, outs, sems, a, 4 + j, (*chip, c), (x, y, 1 - c)).start()

    def finish(ins, outs, sems):
        x, y, c, chips = _place()
        for a in range(n):
            copy(ins, outs, sems, a, 0, (x, y, 1 - c), (x, y, c)).wait_recv()
            for j, chip in enumerate(chips):
                copy(ins, outs, sems, a, 4 + j, (*chip, 1 - c), (x, y, c)).wait_recv()
        for a in range(n):
            for j, chip in enumerate(chips):
                copy(ins, outs, sems, a, 1 + j, (x, y, c), (*chip, c), src=ins[a]).wait_send()
                copy(ins, outs, sems, a, 4 + j, (*chip, c), (x, y, 1 - c)).wait_send()
            copy(ins, outs, sems, a, 0, (x, y, c), (x, y, 1 - c), src=ins[a]).wait_send()
            local(ins, outs, sems, a).wait()

    return Comm(arrays, [jax.ShapeDtypeStruct((N_DEV,) + a.shape, a.dtype) for a in arrays],
                [pltpu.SemaphoreType.DMA((n, 7)), pltpu.SemaphoreType.DMA((n, 7)), pltpu.SemaphoreType.DMA((n,))],
                start, forward, finish)


def comm_direct(arrays, scatter):
    n = len(arrays)

    def peers():
        x, y, c = lax.axis_index("x"), lax.axis_index("y"), lax.axis_index("c")
        out = []
        for r in range(1, N_DEV):
            px, py, pc = (1 - x if r & 4 else x), (1 - y if r & 2 else y), (1 - c if r & 1 else c)
            out.append((r, (px, py, pc), 4 * px + 2 * py + pc))
        return 4 * x + 2 * y + c, out

    def remote(ins, outs, sems, a, r, dev, src_slot, dst_slot):
        return pltpu.make_async_remote_copy(
            src_ref=ins[a].at[src_slot] if scatter[a] else ins[a], dst_ref=outs[a].at[dst_slot],
            send_sem=sems[0].at[a, r - 1], recv_sem=sems[1].at[a, r - 1], device_id=dev, device_id_type=MESH)

    def local(ins, outs, sems, a, me):
        return pltpu.make_async_copy(ins[a].at[me] if scatter[a] else ins[a], outs[a].at[me], sems[2].at[a])

    def start(ins, outs, sems):
        me, ps = peers()
        for a in range(n):
            local(ins, outs, sems, a, me).start()
        for a in range(n):
            for r, dev, lin in ps:
                remote(ins, outs, sems, a, r, dev, lin, me).start()

    def finish(ins, outs, sems):
        me, ps = peers()
        for a in range(n):
            for r, dev, lin in ps:
                remote(ins, outs, sems, a, r, dev, lin, lin).wait_recv()
        for a in range(n):
            for r, dev, lin in ps:
                remote(ins, outs, sems, a, r, dev, lin, me).wait_send()
            local(ins, outs, sems, a, me).wait()

    return Comm(arrays, [jax.ShapeDtypeStruct(a.shape if sc else (N_DEV,) + a.shape, a.dtype) for a, sc in zip(arrays, scatter)],
                [pltpu.SemaphoreType.DMA((n, N_DEV - 1)), pltpu.SemaphoreType.DMA((n, N_DEV - 1)),
                 pltpu.SemaphoreType.DMA((n,))], start, None, finish)


def comm_to_sibling(parts):
    n = len(parts)

    def copy(ins, outs, sems, a, k):
        x, y, c, _ = _place()
        return pltpu.make_async_remote_copy(src_ref=ins[a].at[2 * k + (1 - c)], dst_ref=outs[a].at[k], send_sem=sems[0].at[a, k],
                                            recv_sem=sems[1].at[a, k], device_id=(x, y, 1 - c), device_id_type=MESH)

    def start(ins, outs, sems):
        for a in range(n):
            for k in range(4):
                copy(ins, outs, sems, a, k).start()

    def finish(ins, outs, sems):
        for a in range(n):
            for k in range(4):
                copy(ins, outs, sems, a, k).wait()

    return Comm(parts, [jax.ShapeDtypeStruct((4,) + p.shape[1:], p.dtype) for p in parts],
                [pltpu.SemaphoreType.DMA((n, 4)), pltpu.SemaphoreType.DMA((n, 4))], start, None, finish)


def comm_to_owner_chip(sums):
    n = len(sums)

    def remote(ins, outs, sems, a, j, src_k, dst_k, chip):
        _, _, c, _ = _place()
        return pltpu.make_async_remote_copy(src_ref=ins[a].at[src_k], dst_ref=outs[a].at[dst_k], send_sem=sems[0].at[a, j],
                                            recv_sem=sems[1].at[a, j], device_id=(*chip, c), device_id_type=MESH)

    def local(ins, outs, sems, a):
        x, y, _, _ = _place()
        return pltpu.make_async_copy(ins[a].at[2 * x + y], outs[a].at[2 * x + y], sems[2].at[a])

    def start(ins, outs, sems):
        x, y, _, chips = _place()
        for a in range(n):
            local(ins, outs, sems, a).start()
            for j, (px, py) in enumerate(chips):
                remote(ins, outs, sems, a, j, 2 * px + py, 2 * x + y, (px, py)).start()

    def finish(ins, outs, sems):
        x, y, _, chips = _place()
        for a in range(n):
            for j, (px, py) in enumerate(chips):
                remote(ins, outs, sems, a, j, 2 * x + y, 2 * px + py, (px, py)).wait_recv()
        for a in range(n):
            for j, (px, py) in enumerate(chips):
                remote(ins, outs, sems, a, j, 2 * px + py, 2 * x + y, (px, py)).wait_send()
            local(ins, outs, sems, a).wait()

    return Comm(sums, [jax.ShapeDtypeStruct(s.shape, s.dtype) for s in sums],
                [pltpu.SemaphoreType.DMA((n, 3)), pltpu.SemaphoreType.DMA((n, 3)), pltpu.SemaphoreType.DMA((n,))],
                start, None, finish)


def pair_sum(parts, sib, *, name, tr=128):
    _, R, Cc = parts.shape
    tr = min(tr, R)
    assert R % tr == 0
    core = lax.axis_index("c").astype(jnp.int32).reshape(1)

    def body(c_ref, p_ref, s_ref, o_ref):
        o_ref[0] = (p_ref[0, 0].astype(F32) + s_ref[0].astype(F32)).astype(o_ref.dtype)

    return pl.pallas_call(
        body, name=name,
        grid_spec=pltpu.PrefetchScalarGridSpec(
            num_scalar_prefetch=1, grid=(4, R // tr),
            in_specs=[pl.BlockSpec((1, 1, tr, Cc), lambda k, i, c: (k, c[0], i, 0)),
                      pl.BlockSpec((1, tr, Cc), lambda k, i, c: (k, i, 0))],
            out_specs=pl.BlockSpec((1, tr, Cc), lambda k, i, c: (k, i, 0))),
        out_shape=jax.ShapeDtypeStruct((4, R, Cc), BF16),
        compiler_params=_params(("parallel", "parallel")))(core, parts.reshape(4, 2, R, Cc), sib)


def run_comm(comm, *, name):
    n = len(comm.arrays)

    def body(*refs):
        ins, outs, sems = refs[:n], refs[n:2 * n], refs[2 * n:]
        comm.start(ins, outs, sems)
        if comm.forward is not None:
            comm.forward(ins, outs, sems)
        comm.finish(ins, outs, sems)

    any_spec = pl.BlockSpec(memory_space=pl.ANY)
    return pl.pallas_call(body, name=name, in_specs=[any_spec] * n, out_specs=[any_spec] * n, out_shape=comm.out_shape,
                          scratch_shapes=comm.sems)(*comm.arrays)


def adamw(parts, w, m, v, *, name, tr=128):
    R, Cc = w.shape
    tr = min(tr, R)
    assert R % tr == 0
    n_parts = parts.shape[0]

    def body(p_ref, w_ref, m_ref, v_ref, g_ref, d_ref, nm_ref, nv_ref):
        g = p_ref[0].astype(F32)
        for s in range(1, n_parts):
            g = g + p_ref[s].astype(F32)
        nm = ADAM_B1 * m_ref[...] + (1.0 - ADAM_B1) * g
        nv = ADAM_B2 * v_ref[...] + (1.0 - ADAM_B2) * (g * g)
        m_hat = nm / (1.0 - ADAM_B1 ** ADAM_STEP)
        v_hat = nv / (1.0 - ADAM_B2 ** ADAM_STEP)
        g_ref[...] = g
        d_ref[...] = -ADAM_LR * (m_hat / (jnp.sqrt(v_hat) + ADAM_EPS) + ADAM_WD * w_ref[...])
        nm_ref[...] = nm
        nv_ref[...] = nv

    spec = pl.BlockSpec((tr, Cc), lambda i: (i, 0))
    return pl.pallas_call(
        body, name=name, grid=(R // tr,),
        in_specs=[pl.BlockSpec((n_parts, tr, Cc), lambda i: (0, i, 0)), spec, spec, spec], out_specs=[spec] * 4,
        out_shape=[jax.ShapeDtypeStruct((R, Cc), F32)] * 4, compiler_params=_params(("parallel",)))(parts, w, m, v)


def _lanes(vec, base):
    return jnp.pad(vec[None].astype(F32), ((0, 0), (base, LANES - base - vec.shape[0])))


def _col_shards(full_w):
    R, Ct = full_w.shape
    return jnp.transpose(full_w.reshape(R, N_DEV, Ct // N_DEV), (1, 0, 2))


def _from_col_shards(g):
    return jnp.transpose(g, (1, 0, 2)).reshape(g.shape[1], -1)


def kernel(x, mem, g_mix, w_in, conv_w, a_log, dt_bias, gdn_norm_g, fox_b_f, fox_q_norm, fox_k_norm, g_mem, w_mem_kv, mem_q_norm, mem_k_norm, w_up_gdn, w_up_fox, w_up_mem, w_out, g_mlp, w_ff1, w_ff2, loss_target, m_g_mix, m_w_in, m_conv_w, m_a_log, m_dt_bias, m_gdn_norm_g, m_fox_b_f, m_fox_q_norm, m_fox_k_norm, m_g_mem, m_w_mem_kv, m_mem_q_norm, m_mem_k_norm, m_w_up_gdn, m_w_up_fox, m_w_up_mem, m_w_out, m_g_mlp, m_w_ff1, m_w_ff2, v_g_mix, v_w_in, v_conv_w, v_a_log, v_dt_bias, v_gdn_norm_g, v_fox_b_f, v_fox_q_norm, v_fox_k_norm, v_g_mem, v_w_mem_kv, v_mem_q_norm, v_mem_k_norm, v_w_up_gdn, v_w_up_fox, v_w_up_mem, v_w_out, v_g_mlp, v_w_ff1, v_w_ff2):
    loc = dict(locals())
    big_names = ["w_in", "conv_w", "w_mem_kv", "w_up_gdn", "w_up_fox", "w_up_mem", "w_out", "w_ff1", "w_ff2"]
    col_sharded = {"w_in", "conv_w", "w_up_gdn", "w_up_fox", "w_up_mem", "w_ff1"}
    small_names = ["g_mix", "a_log", "dt_bias", "gdn_norm_g", "fox_b_f", "fox_q_norm", "fox_k_norm", "g_mem",
                   "mem_q_norm", "mem_k_norm", "g_mlp"]

    xs, tgt, mems = x[0], loss_target[0], mem[0]
    T, D = xs.shape
    HG = a_log.shape[1]
    HF = fox_b_f.shape[1]
    DM = mem_q_norm.shape[1]
    GQK, GV = HG * HEAD, HG * HEAD
    GQKV = 2 * GQK + GV
    FW = HF * HEAD
    MW = w_mem_kv.shape[2] // 2
    HM = MW // DM
    assert HG <= 8 and HF <= 8

    shard = {n: loc[n][0].astype(BF16) for n in big_names}
    first, rest, last = big_names[:2], big_names[2:-1], big_names[-1:]
    W = {}

    def take(names, gathered):
        for n, g in zip(names, gathered):
            if n == "w_ff1":
                W[n] = g
            else:
                W[n] = _from_col_shards(g) if n in col_sharded else g.reshape(-1, g.shape[2])

    take(first, run_comm(comm_gather([shard[n] for n in first]), name="gather_in"))
    widths = [GQKV, GV, HG, HG, FW, FW, FW, HF, MW, 3 * D]
    offs = np.concatenate([[0], np.cumsum(widths)]).tolist()
    seg = [W["w_in"][:, offs[i]:offs[i + 1]] for i in range(len(widths))]
    w_big = jnp.concatenate([seg[0], seg[1], seg[4], seg[5], seg[6], seg[8], seg[9]], axis=1)
    pad8 = lambda s: jnp.pad(s, ((0, 0), (0, 8 - s.shape[1])))
    w_small = jnp.concatenate([pad8(seg[2]), pad8(seg[3]), pad8(seg[7]), jnp.zeros((D, LANES - 24), BF16)], axis=1)
    o_z, o_fq, o_fk, o_fv = GQKV, GQKV + GV, GQKV + GV + FW, GQKV + GV + 2 * FW
    o_mq = o_fv + FW
    o_gt = o_mq + MW
    WB = o_gt + 3 * D
    conv_full = W["conv_w"].astype(F32)

    a_log_l, dt_bias_l, b_f_l = _lanes(a_log[0], LANE_A), _lanes(dt_bias[0], LANE_A), _lanes(fox_b_f[0], LANE_F)
    rms_fn = lambda t, g: (_rms(t, g),)

    h, = rowwise(rms_fn, [full(xs)], [g_mix], [(D, D, BF16)], name="rms_mix")
    big, *gathered = matmul(h, w_big, mode="nn", name="proj_big", comm=comm_gather([shard[n] for n in rest]))
    take(rest, gathered)
    small = matmul(h, w_small, mode="nn", name="proj_small", out_dtypes=(F32,))
    bgf, = rowwise(_gates_fn, [full(small)], [a_log_l, dt_bias_l, b_f_l], [(LANES, LANES, F32)], name="gates")

    cn = conv_fwd(big, conv_full, HG, name="conv")
    o_gdn, ss, tinv = gdn_fwd(cn, bgf, HG, name="gdn_fwd")
    oa, = rowwise(_gated_norm_fn, [(o_gdn, HEAD, 0, 1), (big, HEAD, o_z // HEAD, 1)], [gdn_norm_g],
                  [(GV, HEAD, BF16)], name="gated_norm", ncb=HG)

    cum = cumsum_tokens(bgf, reverse=False, name="cumsum")
    fqa = fox_prep(big, o_fq, fox_q_norm, cum, 0, HF, name="fox_prep_q")
    fka = fox_prep(big, o_fk, fox_k_norm, cum, 1, HF, name="fox_prep_k")
    fva = fox_prep(big, o_fv, fox_k_norm, cum, 2, HF, name="fox_prep_v")
    live = fox_live_tiles(cum, fox_q_norm, fox_k_norm, HF, T)
    ob, lse = fox_fwd(fqa, fka, fva, live, HF, name="fox_fwd")

    memn, = rowwise(rms_fn, [full(mems)], [g_mem], [(D, D, BF16)], name="rms_mem")
    kv_m = matmul(memn, W["w_mem_kv"], mode="nn", name="mem_kv", out_dtypes=(F32,))
    kmn, = rowwise(rms_fn, [(kv_m, DM, 0, 1)], [mem_k_norm], [(MW, DM, F32)], name="mem_knorm", ncb=HM)
    vm = kv_m[:, MW:]
    mem_fn = functools.partial(_mem_attn_fn, HM, DM)
    om, = rowwise(mem_fn, [(big, MW, o_mq // MW, 0)], [kmn, vm, mem_q_norm], [(MW, MW, BF16)], name="mem_attn")

    ya = matmul(oa, W["w_up_gdn"], mode="nn", name="up_gdn")
    yb = matmul(ob, W["w_up_fox"], mode="nn", name="up_fox")
    ym = matmul(om, W["w_up_mem"], mode="nn", name="up_mem")
    cbm = min(512, D)
    gate_rows = [(big, cbm, (o_gt + b * D) // cbm, 1) for b in range(3)]
    merge_rows = [(ya, cbm, 0, 1), (yb, cbm, 0, 1), (ym, cbm, 0, 1)] + gate_rows
    y, = rowwise(_merge_fn, merge_rows, [], [(D, cbm, BF16)], name="merge", ncb=D // cbm)
    x1 = matmul(y, W["w_out"], mode="nn", name="out_proj", out_dtypes=(F32,), extras=(xs,),
                epi=lambda r, res: (r + res,))

    h2, = rowwise(rms_fn, [full(x1)], [g_mlp], [(D, D, BF16)], name="rms_mlp")
    u_ff, a_ff, *gathered = matmul(h2, W["w_ff1"], mode="nn", name="ff1", out_dtypes=(BF16, BF16), b_shards=True,
                                   epi=lambda r: (r, jnp.square(jnp.maximum(r, 0.0))),
                                   comm=comm_gather([shard[n] for n in last]))
    take(last, gathered)
    d_out = matmul(a_ff, W["w_ff2"], mode="nn", name="ff2_loss", out_dtypes=(F32,), extras=(x1, tgt),
                   epi=lambda r, res, t: ((r + res - t) * (1.0 / D),))
    loss_local = 0.5 * D * sum_squares(d_out, name="loss_sum")[0, 0]
    loss = lax.psum(loss_local, ("x", "y", "c"))

    G = {}
    d_u = matmul(d_out, W["w_ff2"], mode="nt", name="d_ff2_in", extras=(u_ff,),
                 epi=lambda r, u: (r * 2.0 * jnp.maximum(u.astype(F32), 0.0),))
    d_h2 = matmul(d_u, W["w_ff1"], mode="nt", name="d_ff1_in", b_shards=True)
    d_x1, G["g_mlp"] = rowwise_bwd(rms_fn, [full(x1)], [g_mlp], [full(d_h2)], [(0, F32)], [0], name="d_rms_mlp",
                                   adds=[full(d_out)])
    d_y = matmul(d_x1, W["w_out"], mode="nt", name="d_out_proj_in")
    G["w_out"] = matmul(y, d_x1, mode="tn", name="d_w_out")
    d_ya, d_yb, d_ym, d_ga, d_gb, d_gm = rowwise_bwd(
        _merge_fn, merge_rows, [], [(d_y, cbm, 0, 1)], [(k, BF16) for k in range(6)], [], name="d_merge", ncb=D // cbm)
    d_oa = matmul(d_ya, W["w_up_gdn"], mode="nt", name="d_up_gdn_in")
    d_ob = matmul(d_yb, W["w_up_fox"], mode="nt", name="d_up_fox_in")
    d_om = matmul(d_ym, W["w_up_mem"], mode="nt", name="d_up_mem_in")
    G["w_up_gdn"] = matmul(oa, d_ya, mode="tn", name="d_w_up_gdn")
    G["w_up_fox"] = matmul(ob, d_yb, mode="tn", name="d_w_up_fox")
    G["w_up_mem"] = matmul(om, d_ym, mode="tn", name="d_w_up_mem")

    d_mq, d_kmn, d_vm, G["mem_q_norm"] = rowwise_bwd(
        mem_fn, [(big, MW, o_mq // MW, 0)], [kmn, vm, mem_q_norm], [full(d_om)], [(0, BF16)], [0, 1, 2], name="d_mem_attn")
    d_km, G["mem_k_norm"] = rowwise_bwd(rms_fn, [(kv_m, DM, 0, 1)], [mem_k_norm], [(d_kmn, DM, 0, 1)], [(0, F32)], [0],
                                         name="d_mem_knorm", ncb=HM)
    d_kv_m = jnp.concatenate([d_km, d_vm], axis=1)
    G["w_mem_kv"] = matmul(memn, d_kv_m, mode="tn", name="d_w_mem_kv")
    d_memn = matmul(d_kv_m, W["w_mem_kv"], mode="nt", name="d_mem_kv_in")
    _, G["g_mem"] = rowwise_bwd(rms_fn, [full(mems)], [g_mem], [full(d_memn)], [(0, BF16)], [0], name="d_rms_mem")

    d_fkn, d_fv, d_cum_t = fox_bwd_kv(fqa, fka, fva, d_ob, ob, lse, live, HF, name="fox_bwd_kv")
    d_fqn = fox_bwd_q(fqa, fka, fva, d_ob, ob, lse, live, HF, name="fox_bwd_q")
    d_fq, G["fox_q_norm"] = rowwise_bwd(rms_fn, [(big, HEAD, o_fq // HEAD, 1)], [fox_q_norm], [(d_fqn, HEAD, 0, 1)],
                                         [(0, BF16)], [0], name="d_fox_qnorm", ncb=HF)
    d_fk, G["fox_k_norm"] = rowwise_bwd(rms_fn, [(big, HEAD, o_fk // HEAD, 1)], [fox_k_norm], [(d_fkn, HEAD, 0, 1)],
                                         [(0, BF16)], [0], name="d_fox_knorm", ncb=HF)
    d_cum = jnp.pad(d_cum_t[:HF].T, ((0, 0), (LANE_F, LANES - LANE_F - HF)))
    d_logf = cumsum_tokens(d_cum, reverse=True, name="cumsum_rev")

    d_o_gdn, d_z, G["gdn_norm_g"] = rowwise_bwd(
        _gated_norm_fn, [(o_gdn, HEAD, 0, 1), (big, HEAD, o_z // HEAD, 1)], [gdn_norm_g], [(d_oa, HEAD, 0, 1)],
        [(0, F32), (1, BF16)], [0], name="d_gated_norm", ncb=HG)
    d_cn, d_bg = gdn_bwd(cn, bgf, ss, tinv, d_o_gdn, HG, name="gdn_bwd")
    d_conv_y, G["conv_w"] = conv_bwd_taps(big, conv_full, d_cn, HG, name="d_conv_taps")
    d_qkv = conv_bwd_input(d_conv_y, conv_full, name="d_conv_in")
    d_small, d_al, d_dt, d_bf = rowwise_bwd(_gates_fn, [full(small)], [a_log_l, dt_bias_l, b_f_l], [full(d_bg + d_logf)],
                                            [(0, F32)], [0, 1, 2], name="d_gates")
    G["a_log"], G["dt_bias"], G["fox_b_f"] = (d_al[:, LANE_A:LANE_A + HG], d_dt[:, LANE_A:LANE_A + HG],
                                               d_bf[:, LANE_F:LANE_F + HF])

    def parts(n):
        g = G[n].astype(BF16)
        if g.ndim == 3:
            return g
        return _col_shards(g) if n in col_sharded else g.reshape(N_DEV, -1, g.shape[1])

    recv = {}

    def carried(names, out):
        for n, r in zip(names, out):
            recv[n] = r

    d_big = jnp.concatenate([d_qkv, d_z, d_fq, d_fk, d_fv, d_mq, d_ga, d_gb, d_gm], axis=1)
    group = ["conv_w", "w_mem_kv", "w_up_gdn", "w_up_fox", "w_up_mem", "w_out"]
    G["w_ff2"], *out = matmul(a_ff, d_out, mode="tn", name="d_w_ff2",
                              comm=comm_direct([parts(n) for n in group], [True] * len(group)))
    carried(group, out)
    G["w_ff1"], *out = matmul(h2, d_u, mode="tn", name="d_w_ff1", out_shards=True,
                              comm=comm_direct([parts("w_ff2")], [True]))
    carried(["w_ff2"], out)
    g_big, *out = matmul(h, d_big, mode="tn", name="d_w_big", comm=comm_direct([parts("w_ff1")], [True]))
    carried(["w_ff1"], out)
    g_small = matmul(h, d_small, mode="tn", name="d_w_small", out_dtypes=(F32,))
    cols = lambda a, o, wd: a[:, o:o + wd]
    G["w_in"] = jnp.concatenate([
        cols(g_big, 0, GQKV), cols(g_big, o_z, GV), cols(g_small, LANE_B, HG), cols(g_small, LANE_A, HG),
        cols(g_big, o_fq, FW), cols(g_big, o_fk, FW), cols(g_big, o_fv, FW), cols(g_small, LANE_F, HF),
        cols(g_big, o_mq, MW), cols(g_big, o_gt, 3 * D)], axis=1)
    p_in = parts("w_in")
    d_h_s, from_sibling = matmul(d_small, w_small, mode="nt", name="d_proj_small_in", out_dtypes=(F32,),
                                 comm=comm_to_sibling([p_in]))
    chip_sums = pair_sum(p_in, from_sibling, name="w_in_pair_sum")
    d_h, *out = matmul(d_big, w_big, mode="nt", name="d_proj_big_in", extras=(d_h_s,), epi=lambda r, e: (r + e,),
                       comm=comm_to_owner_chip([chip_sums]))
    carried(["w_in"], out)
    grad_x, G["g_mix"] = rowwise_bwd(rms_fn, [full(xs)], [g_mix], [full(d_h)], [(0, F32)], [0], name="d_rms_mix",
                                     adds=[full(d_x1)])
    grad_x = grad_x[None]

    small_sizes = [loc[n].shape[1] for n in small_names]
    pack = lambda d: jnp.concatenate([d[n].reshape(1, -1) for n in small_names], axis=1)
    npad = -sum(small_sizes) % LANES
    padp = lambda a: jnp.pad(a, ((0, 0), (0, npad)))
    recv_small, = run_comm(comm_direct([padp(pack(G))], [False]), name="gather_small_grads")

    res = {}
    for n in big_names:
        res[n] = [t[None] for t in adamw(recv[n], loc[n][0], loc["m_" + n][0], loc["v_" + n][0], name="adamw_" + n)]
    sm = adamw(recv_small, padp(pack({n: loc[n] for n in small_names})), padp(pack({n: loc["m_" + n] for n in small_names})),
               padp(pack({n: loc["v_" + n] for n in small_names})), name="adamw_small")
    so = np.concatenate([[0], np.cumsum(small_sizes)]).tolist()
    for i, n in enumerate(small_names):
        res[n] = [t[:, so[i]:so[i + 1]] for t in sm]

    order = ["g_mix", "w_in", "conv_w", "a_log", "dt_bias", "gdn_norm_g", "fox_b_f", "fox_q_norm", "fox_k_norm", "g_mem",
             "w_mem_kv", "mem_q_norm", "mem_k_norm", "w_up_gdn", "w_up_fox", "w_up_mem", "w_out", "g_mlp", "w_ff1", "w_ff2"]
    return (loss, grad_x, *[res[n][0] for n in order], *[res[n][1] for n in order],
            *[res[n][2] for n in order], *[res[n][3] for n in order])
```

```python
import functools

import jax
import jax.numpy as jnp
import numpy as np
from jax import lax
from jax.experimental import pallas as pl
from jax.experimental.pallas import tpu as pltpu

F32 = jnp.float32
BF16 = jnp.bfloat16
HI = lax.Precision.HIGHEST

EPS = 1e-6
GDN_CHUNK = 64
GDN_CONV = 4
HEAD = 128
LANES = 128
HALO = 16
N_DEV = 8
MESH = pl.DeviceIdType.MESH
VMEM_LIMIT_V7X = 56 * 1024 * 1024

ADAM_LR, ADAM_B1, ADAM_B2, ADAM_EPS, ADAM_WD, ADAM_STEP = 0.001, 0.9, 0.999, 1e-08, 0.01, 10

LANE_B, LANE_A, LANE_F = 0, 8, 16


def _params(sem):
    return pltpu.CompilerParams(dimension_semantics=sem, vmem_limit_bytes=VMEM_LIMIT_V7X)


def _dg(a, b, ca, cb, prec):
    nb = a.ndim - 2
    batch = tuple(range(nb))
    return lax.dot_general(a, b, (((ca + nb,), (cb + nb,)), (batch, batch)), precision=prec,
                           preferred_element_type=F32)


def _make_mm(prec, cast):
    def c(x):
        return x.astype(BF16) if cast else x

    @jax.custom_vjp
    def nn(a, b):
        return _dg(c(a), c(b), 1, 0, prec)

    @jax.custom_vjp
    def nt(a, b):
        return _dg(c(a), c(b), 1, 1, prec)

    @jax.custom_vjp
    def tn(a, b):
        return _dg(c(a), c(b), 0, 0, prec)

    nn.defvjp(lambda a, b: (nn(a, b), (a, b)), lambda r, g: (nt(g, r[1]), tn(r[0], g)))
    nt.defvjp(lambda a, b: (nt(a, b), (a, b)), lambda r, g: (nn(g, r[1]), tn(g, r[0])))
    tn.defvjp(lambda a, b: (tn(a, b), (a, b)), lambda r, g: (nt(r[1], g), nn(r[0], g)))
    return nn, nt, tn


NN, NT, TN = _make_mm(None, True)
NNH, NTH, TNH = _make_mm(lax.Precision.HIGH, False)
NNX, _, _ = _make_mm(HI, False)


def _sigmoid(x):
    return 1.0 / (1.0 + jnp.exp(-x))


def _silu(x):
    return x * _sigmoid(x)


def _softplus(x):
    return jnp.maximum(x, 0.0) + jnp.log(1.0 + jnp.exp(-jnp.abs(x)))


def _log_sigmoid(x):
    return -_softplus(-x)


def _rms(x, g):
    return x * lax.rsqrt(jnp.mean(x * x, axis=-1, keepdims=True) + EPS) * g


def _tile(n, target):
    t = target
    while t >= LANES:
        if n % t == 0:
            return t
        t //= 2
    return n


def matmul(a, b, *, mode, name, out_dtypes=(BF16,), epi=None, extras=(), tm=1024, tn=1024, tk=2048, comm=None,
           b_shards=False, out_shards=False):
    if b_shards:
        b_rows, b_cols = b.shape[1], N_DEV * b.shape[2]
    else:
        b_rows, b_cols = b.shape
    if mode == "nn":
        (M, K), (K2, N) = a.shape, (b_rows, b_cols)
    elif mode == "nt":
        (M, K), (N, K2) = a.shape, (b_rows, b_cols)
    else:
        (K, M), (K2, N) = a.shape, (b_rows, b_cols)
    assert K == K2, (name, a.shape, b.shape)
    tm, tn, tk = _tile(M, tm), _tile(N, tn), _tile(K, tk)
    if b_shards and mode == "nt":
        tk = K // N_DEV
    if (b_shards and mode != "nt") or out_shards:
        tn = N // N_DEV
    ni, nj, nk = M // tm, N // tn, K // tk
    a_spec = (pl.BlockSpec((tk, tm), lambda i, j, k: (k, i)) if mode == "tn"
              else pl.BlockSpec((tm, tk), lambda i, j, k: (i, k)))
    if b_shards:
        b_spec = (pl.BlockSpec((None, tn, tk), lambda i, j, k: (k, j, 0)) if mode == "nt"
                  else pl.BlockSpec((None, tk, tn), lambda i, j, k: (j, k, 0)))
    else:
        b_spec = (pl.BlockSpec((tn, tk), lambda i, j, k: (j, k)) if mode == "nt"
                  else pl.BlockSpec((tk, tn), lambda i, j, k: (k, j)))
    o_spec = pl.BlockSpec((tm, tn), lambda i, j, k: (i, j))
    w_spec = pl.BlockSpec((None, tm, tn), lambda i, j, k: (j, i, 0)) if out_shards else o_spec
    w_shape = (N_DEV, M, tn) if out_shards else (M, N)
    dims = {"nn": ((1,), (0,)), "nt": ((1,), (1,)), "tn": ((0,), (0,))}[mode]
    ne, no = len(extras), len(out_dtypes)
    nc = len(comm.arrays) if comm is not None else 0
    n_steps = ni * nj * nk

    def body(a_ref, b_ref, *rest):
        ex, c_in = rest[:ne], rest[ne:ne + nc]
        outs, c_out = rest[ne + nc:ne + nc + no], rest[ne + nc + no:ne + nc + no + nc]
        acc = rest[ne + nc + no + nc]
        c_sems = rest[ne + nc + no + nc + 1:]
        k = pl.program_id(2)
        step = (pl.program_id(0) * nj + pl.program_id(1)) * nk + k

        if comm is not None:
            pl.when(step == 0)(lambda: comm.start(c_in, c_out, c_sems))
            if comm.forward is not None and n_steps >= 3:
                pl.when(step == (3 * n_steps) // 4)(lambda: comm.forward(c_in, c_out, c_sems))

        @pl.when(k == 0)
        def _():
            acc[...] = jnp.zeros_like(acc)

        acc[...] += lax.dot_general(a_ref[...].astype(BF16), b_ref[...].astype(BF16), (dims, ((), ())),
                                    preferred_element_type=F32)

        @pl.when(k == nk - 1)
        def _():
            r = acc[...]
            vals = epi(r, *[e[...] for e in ex]) if epi is not None else (r,)
            for o, v in zip(outs, vals):
                o[...] = v.astype(o.dtype)

        if comm is not None:
            @pl.when(step == n_steps - 1)
            def _():
                if comm.forward is not None and n_steps < 3:
                    comm.forward(c_in, c_out, c_sems)
                comm.finish(c_in, c_out, c_sems)

    any_spec = pl.BlockSpec(memory_space=pl.ANY)
    sem = ("arbitrary",) * 3 if comm is not None else ("parallel", "parallel", "arbitrary")
    out = pl.pallas_call(
        body, name=name, grid=(ni, nj, nk),
        in_specs=[a_spec, b_spec] + [o_spec] * ne + [any_spec] * nc, out_specs=[w_spec] * no + [any_spec] * nc,
        out_shape=[jax.ShapeDtypeStruct(w_shape, d) for d in out_dtypes] + (comm.out_shape if comm is not None else []),
        scratch_shapes=[pltpu.VMEM((tm, tn), F32)] + (comm.sems if comm is not None else []),
        compiler_params=_params(sem))(a, b, *extras, *(comm.arrays if comm is not None else []))
    return out[0] if len(out) == 1 else out


def _row_spec(tr, cb, off, moves):
    return pl.BlockSpec((tr, cb), lambda i, j: (i, off + moves * j))


def _whole_spec(p):
    return pl.BlockSpec(p.shape, lambda i, j: (0,) * p.ndim)


def _row_tile(T, rows):
    widest = max(cb for (_, cb, _, _) in rows)
    return min(T, max(512, (1 << 19) // widest))


def rowwise(fn, rows, params, outs, *, name, ncb=1, tr=None, comm=None):
    T = rows[0][0].shape[0]
    tr = min(tr, T) if tr else _row_tile(T, rows)
    assert T % tr == 0
    nr, npar, no = len(rows), len(params), len(outs)
    nc = len(comm.arrays) if comm is not None else 0
    n_steps = (T // tr) * ncb

    def body(*refs):
        r, p, c_in = refs[:nr], refs[nr:nr + npar], refs[nr + npar:nr + npar + nc]
        o, c_out, c_sems = refs[nr + npar + nc:nr + npar + nc + no], refs[nr + npar + nc + no:nr + npar + 2 * nc + no], \
            refs[nr + npar + 2 * nc + no:]
        step = pl.program_id(0) * ncb + pl.program_id(1)
        if comm is not None:
            pl.when(step == 0)(lambda: comm.start(c_in, c_out, c_sems))
            if comm.forward is not None and n_steps >= 3:
                pl.when(step == (3 * n_steps) // 4)(lambda: comm.forward(c_in, c_out, c_sems))
        vals = fn(*[x[...].astype(F32) for x in r], *[x[...] for x in p])
        for oref, v in zip(o, vals):
            oref[...] = v.astype(oref.dtype)
        if comm is not None:
            @pl.when(step == n_steps - 1)
            def _():
                if comm.forward is not None and n_steps < 3:
                    comm.forward(c_in, c_out, c_sems)
                comm.finish(c_in, c_out, c_sems)

    any_spec = pl.BlockSpec(memory_space=pl.ANY)
    res = pl.pallas_call(
        body, name=name, grid=(T // tr, ncb),
        in_specs=([_row_spec(tr, cb, off, mv) for (_, cb, off, mv) in rows] + [_whole_spec(p) for p in params]
                  + [any_spec] * nc),
        out_specs=[_row_spec(tr, cb, 0, 1) for (_, cb, _) in outs] + [any_spec] * nc,
        out_shape=[jax.ShapeDtypeStruct((T, cols), d) for (cols, _, d) in outs] + (comm.out_shape if comm is not None else []),
        scratch_shapes=comm.sems if comm is not None else [],
        compiler_params=_params(("arbitrary", "arbitrary") if comm is not None else ("parallel", "parallel")))(
            *[r[0] for r in rows], *params, *(comm.arrays if comm is not None else []))
    return res


def rowwise_bwd(fn, rows, params, cots, drows, dparams, *, name, ncb=1, tr=None, adds=()):
    T = rows[0][0].shape[0]
    tr = min(tr, T) if tr else _row_tile(T, rows)
    assert T % tr == 0
    nr, npar, nc, ndr, na = len(rows), len(params), len(cots), len(drows), len(adds)

    def body(*refs):
        r, p, c = refs[:nr], refs[nr:nr + npar], refs[nr + npar:nr + npar + nc]
        base = nr + npar + nc + na
        ad, o_r, o_p = refs[base - na:base], refs[base:base + ndr], refs[base + ndr:]
        prim = [x[...].astype(F32) for x in r] + [x[...] for x in p]
        _, vjp = jax.vjp(lambda *a: tuple(fn(*a)), *prim)
        g = vjp(tuple(x[...].astype(F32) for x in c))
        for k, (oref, (idx, _)) in enumerate(zip(o_r, drows)):
            val = g[idx] + ad[k][...].astype(F32) if k < na else g[idx]
            oref[...] = val.astype(oref.dtype)
        first = jnp.logical_and(pl.program_id(0) == 0, pl.program_id(1) == 0)

        @pl.when(first)
        def _():
            for oref in o_p:
                oref[...] = jnp.zeros_like(oref)

        for oref, idx in zip(o_p, dparams):
            oref[...] += g[nr + idx]

    res = pl.pallas_call(
        body, name=name, grid=(T // tr, ncb),
        in_specs=([_row_spec(tr, cb, off, mv) for (_, cb, off, mv) in rows] + [_whole_spec(p) for p in params]
                  + [_row_spec(tr, cb, off, mv) for (_, cb, off, mv) in tuple(cots) + tuple(adds)]),
        out_specs=([_row_spec(tr, rows[idx][1], 0, 1) for (idx, _) in drows]
                   + [_whole_spec(params[idx]) for idx in dparams]),
        out_shape=([jax.ShapeDtypeStruct((T, ncb * rows[idx][1] if rows[idx][3] else rows[idx][1]), d)
                    for (idx, d) in drows]
                   + [jax.ShapeDtypeStruct(params[idx].shape, F32) for idx in dparams]),
        compiler_params=_params(("arbitrary", "arbitrary")))(
            *[r[0] for r in rows], *params, *[c[0] for c in cots], *[a[0] for a in adds])
    return res


def full(a):
    return (a, a.shape[1], 0, 0)


def cumsum_tokens(x, *, reverse, name, tb=256):
    T = x.shape[0]
    tb = min(tb, T)
    nb = T // tb
    idx = (lambda i: (nb - 1 - i, 0)) if reverse else (lambda i: (i, 0))

    def body(x_ref, o_ref, carry):
        @pl.when(pl.program_id(0) == 0)
        def _():
            carry[...] = jnp.zeros_like(carry)

        ii = lax.broadcasted_iota(jnp.int32, (tb, tb), 0)
        jj = lax.broadcasted_iota(jnp.int32, (tb, tb), 1)
        tri = ((ii <= jj) if reverse else (ii >= jj)).astype(F32)
        c = lax.dot_general(tri, x_ref[...], (((1,), (0,)), ((), ())), precision=HI,
                            preferred_element_type=F32) + carry[0:1, :]
        o_ref[...] = c
        carry[0:1, :] = c[0:1, :] if reverse else c[tb - 1:tb, :]

    return pl.pallas_call(
        body, name=name, grid=(nb,), in_specs=[pl.BlockSpec((tb, LANES), idx)],
        out_specs=pl.BlockSpec((tb, LANES), idx), out_shape=jax.ShapeDtypeStruct((T, LANES), F32),
        scratch_shapes=[pltpu.VMEM((8, LANES), F32)], compiler_params=_params(("arbitrary",)))(x)


def _conv_post(y, kind, dk):
    c = _silu(y)
    r = lax.rsqrt(jnp.sum(c * c, axis=-1, keepdims=True) + EPS)
    return jnp.where(kind == 0, c * r * (dk ** -0.5), jnp.where(kind == 1, c * r, c))


def _conv_taps(cur, prev, w, tr):
    ext = jnp.concatenate([prev, cur], axis=0)
    y = w[3:4, :] * cur
    for d in (1, 2, 3):
        y = y + w[3 - d:4 - d, :] * pltpu.roll(ext, d, 0)[HALO:HALO + tr]
    return y


def conv_fwd(big, w, n_qk_heads, *, name, tr=2048):
    T, W = big.shape[0], w.shape[1]
    tr = min(tr, T)
    nh = W // HEAD

    def body(cur_ref, prev_ref, w_ref, o_ref):
        i, j = pl.program_id(0), pl.program_id(1)
        prev = jnp.where(i > 0, prev_ref[...].astype(F32), 0.0)
        y = _conv_taps(cur_ref[...].astype(F32), prev, w_ref[...], tr)
        kind = jnp.where(j < n_qk_heads, 0, jnp.where(j < 2 * n_qk_heads, 1, 2))
        o_ref[...] = _conv_post(y, kind, HEAD)

    return pl.pallas_call(
        body, name=name, grid=(T // tr, nh),
        in_specs=[pl.BlockSpec((tr, HEAD), lambda i, j: (i, j)),
                  pl.BlockSpec((HALO, HEAD), lambda i, j: (jnp.maximum(i * (tr // HALO) - 1, 0), j)),
                  pl.BlockSpec((GDN_CONV, HEAD), lambda i, j: (0, j))],
        out_specs=pl.BlockSpec((tr, HEAD), lambda i, j: (i, j)),
        out_shape=jax.ShapeDtypeStruct((T, W), F32), compiler_params=_params(("parallel", "parallel")))(big, big, w)


def conv_bwd_taps(big, w, dcn, n_qk_heads, *, name, tr=2048):
    T, W = big.shape[0], w.shape[1]
    tr = min(tr, T)
    nh = W // HEAD

    def body(cur_ref, prev_ref, w_ref, g_ref, dy_ref, dw_ref):
        j, i = pl.program_id(0), pl.program_id(1)
        cur = cur_ref[...].astype(F32)
        prev = jnp.where(i > 0, prev_ref[...].astype(F32), 0.0)
        y = _conv_taps(cur, prev, w_ref[...], tr)
        kind = jnp.where(j < n_qk_heads, 0, jnp.where(j < 2 * n_qk_heads, 1, 2))
        _, vjp = jax.vjp(lambda t: _conv_post(t, kind, HEAD), y)
        dy, = vjp(g_ref[...])
        dy_ref[...] = dy
        ext = jnp.concatenate([prev, cur], axis=0)
        rows = [jnp.sum(dy * (cur if d == 0 else pltpu.roll(ext, d, 0)[HALO:HALO + tr]), axis=0, keepdims=True)
                for d in (3, 2, 1, 0)]

        @pl.when(i == 0)
        def _():
            dw_ref[...] = jnp.zeros_like(dw_ref)

        dw_ref[...] += jnp.concatenate(rows, axis=0)

    return pl.pallas_call(
        body, name=name, grid=(nh, T // tr),
        in_specs=[pl.BlockSpec((tr, HEAD), lambda j, i: (i, j)),
                  pl.BlockSpec((HALO, HEAD), lambda j, i: (jnp.maximum(i * (tr // HALO) - 1, 0), j)),
                  pl.BlockSpec((GDN_CONV, HEAD), lambda j, i: (0, j)),
                  pl.BlockSpec((tr, HEAD), lambda j, i: (i, j))],
        out_specs=[pl.BlockSpec((tr, HEAD), lambda j, i: (i, j)), pl.BlockSpec((GDN_CONV, HEAD), lambda j, i: (0, j))],
        out_shape=[jax.ShapeDtypeStruct((T, W), F32), jax.ShapeDtypeStruct((GDN_CONV, W), F32)],
        compiler_params=_params(("parallel", "arbitrary")))(big, big, w, dcn)


def conv_bwd_input(dy, w, *, name, tr=2048):
    T, W = dy.shape
    tr = min(tr, T)
    nrow = T // tr

    def body(cur_ref, nxt_ref, w_ref, o_ref):
        i = pl.program_id(0)
        cur = cur_ref[...]
        nxt = jnp.where(i < nrow - 1, nxt_ref[...], 0.0)
        ext = jnp.concatenate([cur, nxt], axis=0)
        w = w_ref[...]
        dx = w[3:4, :] * cur
        for d in (1, 2, 3):
            dx = dx + w[3 - d:4 - d, :] * pltpu.roll(ext, tr + HALO - d, 0)[0:tr]
        o_ref[...] = dx.astype(o_ref.dtype)

    return pl.pallas_call(
        body, name=name, grid=(nrow, W // HEAD),
        in_specs=[pl.BlockSpec((tr, HEAD), lambda i, j: (i, j)),
                  pl.BlockSpec((HALO, HEAD), lambda i, j: (jnp.minimum((i + 1) * (tr // HALO), T // HALO - 1), j)),
                  pl.BlockSpec((GDN_CONV, HEAD), lambda i, j: (0, j))],
        out_specs=pl.BlockSpec((tr, HEAD), lambda i, j: (i, j)),
        out_shape=jax.ShapeDtypeStruct((T, W), BF16), compiler_params=_params(("parallel", "parallel")))(dy, dy, w)


INV_BLOCK = 16


def _unit_lower_inverse_value(p):
    C = p.shape[-1]
    ii = lax.broadcasted_iota(jnp.int32, (C, C), 0)
    jj = lax.broadcasted_iota(jnp.int32, (C, C), 1)
    eye = jnp.where((ii == jj)[None], 1.0, 0.0)
    same = ((ii // INV_BLOCK) == (jj // INV_BLOCK))[None]
    pd = jnp.where(same, p, 0.0)
    d_inv = eye + pd
    n = 2
    while n < INV_BLOCK:
        pd = NNH(pd, pd)
        d_inv = d_inv + NNH(d_inv, pd)
        n *= 2
    nb = NNH(d_inv, jnp.where(same, 0.0, p))
    t = eye + nb
    n = 2
    while n < C // INV_BLOCK:
        nb = NNH(nb, nb)
        t = t + NNH(t, nb)
        n *= 2
    return NNH(t, d_inv)


@jax.custom_vjp
def _unit_lower_inverse(p, t_known):
    return _unit_lower_inverse_value(p) if t_known is None else t_known


def _unit_lower_inverse_fwd(p, t_known):
    t = _unit_lower_inverse(p, t_known)
    return t, (t, t_known is not None)


def _unit_lower_inverse_bwd(res, g):
    t, had = res
    return NTH(TNH(t, g), t), (jnp.zeros_like(t) if had else None)


_unit_lower_inverse.defvjp(_unit_lower_inverse_fwd, _unit_lower_inverse_bwd)


def _gdn_chunk(q, k, v, bg, S, t_known=None):
    H, C = q.shape[0], q.shape[1]
    ii = lax.broadcasted_iota(jnp.int32, (C, C), 0)
    jj = lax.broadcasted_iota(jnp.int32, (C, C), 1)
    lincl = (ii >= jj).astype(F32)
    strict, incl, eye = (ii > jj)[None], (ii >= jj)[None], (ii == jj)[None]
    gam2d = lax.dot_general(lincl, bg, (((1,), (0,)), ((), ())), precision=HI, preferred_element_type=F32)
    lane = lax.broadcasted_iota(jnp.int32, (H, 1, LANES), 2)
    hh = lax.broadcasted_iota(jnp.int32, (H, 1, LANES), 0)
    beta = jnp.sum(bg[None] * (lane == hh + LANE_B).astype(F32), axis=2, keepdims=True)
    gam = jnp.sum(gam2d[None] * (lane == hh + LANE_A).astype(F32), axis=2, keepdims=True)
    last = (lax.broadcasted_iota(jnp.int32, (1, C, 1), 1) == C - 1).astype(F32)
    gam_last = jnp.sum(gam * last, axis=1, keepdims=True)
    gam_row = NNX(jnp.ones((H, C, C), F32), jnp.where(eye, gam, 0.0))
    diff = gam - gam_row
    dec_s = jnp.where(strict, jnp.exp(jnp.where(strict, diff, 0.0)), 0.0)
    dec_i = jnp.where(incl, jnp.exp(jnp.where(incl, diff, 0.0)), 0.0)
    t = _unit_lower_inverse(-(beta * NT(k, k) * dec_s), t_known)
    eg = jnp.exp(gam)
    wu = NNH(t, jnp.concatenate([beta * eg * k, beta * v], axis=-1))
    w, u0 = wu[..., :HEAD], wu[..., HEAD:]
    qk = NT(q, k) * dec_i
    u = u0 - NN(w, S)
    o = NN(jnp.concatenate([q * eg, qk], axis=-1), jnp.concatenate([S, u], axis=-2))
    S2 = jnp.exp(gam_last) * S + TN(k * jnp.exp(gam_last - gam), u)
    return o, S2, t


def _heads(x, base, H):
    return jnp.stack([x[:, base + h * HEAD:base + (h + 1) * HEAD] for h in range(H)])


def _unheads(x):
    return jnp.concatenate([x[h] for h in range(x.shape[0])], axis=1)


def gdn_fwd(cn, bgf, H, *, name):
    T, C, W = cn.shape[0], GDN_CHUNK, H * HEAD
    N = T // C

    def body(cn_ref, bg_ref, o_ref, ss_ref, t_ref, s_scr):
        @pl.when(pl.program_id(0) == 0)
        def _():
            s_scr[...] = jnp.zeros_like(s_scr)

        x, S = cn_ref[...], s_scr[...]
        ss_ref[0] = S
        o, S2, t = _gdn_chunk(_heads(x, 0, H), _heads(x, W, H), _heads(x, 2 * W, H), bg_ref[...], S)
        o_ref[...] = _unheads(o)
        t_ref[0] = t
        s_scr[...] = S2

    return pl.pallas_call(
        body, name=name, grid=(N,),
        in_specs=[pl.BlockSpec((C, 3 * W), lambda n: (n, 0)), pl.BlockSpec((C, LANES), lambda n: (n, 0))],
        out_specs=[pl.BlockSpec((C, W), lambda n: (n, 0)), pl.BlockSpec((1, H, HEAD, HEAD), lambda n: (n, 0, 0, 0)),
                   pl.BlockSpec((1, H, C, C), lambda n: (n, 0, 0, 0))],
        out_shape=[jax.ShapeDtypeStruct((T, W), F32), jax.ShapeDtypeStruct((N, H, HEAD, HEAD), F32),
                   jax.ShapeDtypeStruct((N, H, C, C), F32)],
        scratch_shapes=[pltpu.VMEM((H, HEAD, HEAD), F32)], compiler_params=_params(("arbitrary",)))(cn, bgf)


def gdn_bwd(cn, bgf, ss, tinv, do, H, *, name):
    T, C, W = cn.shape[0], GDN_CHUNK, H * HEAD
    N = T // C

    def body(cn_ref, bg_ref, ss_ref, t_ref, do_ref, dcn_ref, dbg_ref, ds_scr):
        @pl.when(pl.program_id(0) == 0)
        def _():
            ds_scr[...] = jnp.zeros_like(ds_scr)

        x = cn_ref[...]
        t_known = t_ref[0]
        _, vjp = jax.vjp(lambda *a: _gdn_chunk(*a, t_known)[:2],
                         _heads(x, 0, H), _heads(x, W, H), _heads(x, 2 * W, H), bg_ref[...], ss_ref[0])
        dq, dk, dv, dbg, dS = vjp((_heads(do_ref[...], 0, H), ds_scr[...]))
        dcn_ref[...] = jnp.concatenate([_unheads(dq), _unheads(dk), _unheads(dv)], axis=1)
        dbg_ref[...] = dbg
        ds_scr[...] = dS

    rev = lambda n: (N - 1 - n, 0)
    return pl.pallas_call(
        body, name=name, grid=(N,),
        in_specs=[pl.BlockSpec((C, 3 * W), rev), pl.BlockSpec((C, LANES), rev),
                  pl.BlockSpec((1, H, HEAD, HEAD), lambda n: (N - 1 - n, 0, 0, 0)),
                  pl.BlockSpec((1, H, C, C), lambda n: (N - 1 - n, 0, 0, 0)), pl.BlockSpec((C, W), rev)],
        out_specs=[pl.BlockSpec((C, 3 * W), rev), pl.BlockSpec((C, LANES), rev)],
        out_shape=[jax.ShapeDtypeStruct((T, 3 * W), F32), jax.ShapeDtypeStruct((T, LANES), F32)],
        scratch_shapes=[pltpu.VMEM((H, HEAD, HEAD), F32)], compiler_params=_params(("arbitrary",)))(cn, bgf, ss, tinv, do)


AUG = 2 * HEAD


def fox_prep(big, col_off, gain, cum, kind, H, *, name, tr=2048):
    T = big.shape[0]
    tr = min(tr, T)

    def body(x_ref, g_ref, c_ref, o_ref):
        h = pl.program_id(1)
        x = x_ref[...].astype(F32)
        lane = lax.broadcasted_iota(jnp.int32, (tr, HEAD), 1)
        if kind == 2:
            main, aug = x, jnp.ones((tr, HEAD), F32)
        else:
            main = _rms(x, g_ref[...]) * ((HEAD ** -0.5) if kind == 0 else 1.0)
            c = jnp.sum(jnp.where(lane == LANE_F + h, c_ref[...], 0.0), axis=1, keepdims=True)
            hi = c.astype(BF16).astype(F32)
            mid = (c - hi).astype(BF16).astype(F32)
            lo = c - hi - mid
            if kind == 0:
                aug = jnp.where(lane == 0, hi, jnp.where(lane == 1, mid, jnp.where(lane == 2, lo,
                                                                                   jnp.where(lane < 6, 1.0, 0.0))))
            else:
                aug = jnp.where(lane < 3, 1.0, jnp.where(lane == 3, -hi, jnp.where(lane == 4, -mid,
                                                                                   jnp.where(lane == 5, -lo, 0.0))))
        o_ref[...] = jnp.concatenate([main, aug], axis=1).astype(o_ref.dtype)

    return pl.pallas_call(
        body, name=name, grid=(T // tr, H),
        in_specs=[pl.BlockSpec((tr, HEAD), lambda i, h: (i, col_off // HEAD + h)),
                  pl.BlockSpec((1, HEAD), lambda i, h: (0, 0)), pl.BlockSpec((tr, LANES), lambda i, h: (i, 0))],
        out_specs=pl.BlockSpec((tr, AUG), lambda i, h: (i, h)),
        out_shape=jax.ShapeDtypeStruct((T, H * AUG), BF16), compiler_params=_params(("parallel", "parallel")))(big, gain, cum)


def _fox_logits(q_ref, k_ref, h, tq, tk, diagonal):
    ha = slice(h * AUG, (h + 1) * AUG)
    s = lax.dot_general(q_ref[:, ha], k_ref[:, ha], (((1,), (1,)), ((), ())), preferred_element_type=F32)
    keep = None
    if diagonal:
        keep = lax.broadcasted_iota(jnp.int32, (tq, tk), 0) >= lax.broadcasted_iota(jnp.int32, (tq, tk), 1)
    return s, keep


def _dispatch(live_ref, H, i, j, below, on_diagonal, head_fn, after=None, straight_from=None):
    straight_from = H if straight_from is None else straight_from
    def straight(diagonal):
        for h in range(H):
            head_fn(h, diagonal)
        if after is not None:
            after()

    def by_head():
        for h in range(H):
            pl.when(live_ref[h, i, j] != 0)(functools.partial(head_fn, h, False))
        if after is not None:
            after()

    n_live = live_ref[H, i, j]
    pl.when(jnp.logical_and(below, n_live >= straight_from))(functools.partial(straight, False))
    pl.when(jnp.logical_and(below, jnp.logical_and(n_live > 0, n_live < straight_from)))(by_head)
    pl.when(on_diagonal)(functools.partial(straight, True))


FOX_TILE = 512
EXP_UNDERFLOW = -100.0


def fox_live_tiles(cum, q_gain, k_gain, H, T):
    t = min(FOX_TILE, T)
    n = T // t
    c = cum[:, LANE_F:LANE_F + H]
    bias = c[0::t][:, None, :] - c[t - 1::t][None, :, :]
    bound = 1.02 * (HEAD ** 0.5) * jnp.max(jnp.abs(q_gain)) * jnp.max(jnp.abs(k_gain))
    causal = (jnp.arange(n)[:, None] >= jnp.arange(n)[None, :])[:, :, None]
    live = jnp.logical_and(causal, 2.0 * bound + bias >= EXP_UNDERFLOW)
    live = live.astype(jnp.int32)
    return jnp.concatenate([jnp.transpose(live, (2, 0, 1)), jnp.sum(live, axis=2)[None]], axis=0)


def fox_fwd(qa, ka, va, live, H, *, name):
    T = qa.shape[0]
    tq = tk = min(FOX_TILE, T)
    nq, nk = T // tq, T // tk

    def body(live_ref, q_ref, k_ref, v_ref, o_ref, lse_ref, acc, m_scr):
        i, j = pl.program_id(0), pl.program_id(1)

        @pl.when(j == 0)
        def _():
            m_scr[...] = jnp.full_like(m_scr, -jnp.inf)
            acc[...] = jnp.zeros_like(acc)

        def head(h, diagonal):
            ha = slice(h * AUG, (h + 1) * AUG)
            s, keep = _fox_logits(q_ref, k_ref, h, tq, tk, diagonal)
            if diagonal:
                s = jnp.where(keep, s, -jnp.inf)
            m_prev = m_scr[h]
            m_new = jnp.maximum(m_prev, jnp.max(s, axis=1, keepdims=True))
            p = jnp.exp(s - m_new[:, 0:1])
            p_hi = p.astype(BF16)
            p_lo = (p - p_hi.astype(F32)).astype(BF16)
            pv = lambda t: lax.dot_general(t, v_ref[:, ha], (((1,), (0,)), ((), ())), preferred_element_type=F32)
            acc[:, ha] = jnp.exp(m_prev[:, 0:1] - m_new[:, 0:1]) * acc[:, ha] + (pv(p_hi) + pv(p_lo))
            m_scr[h] = m_new

        _dispatch(live_ref, H, i, j, j < i, j == i, head, straight_from=(5 * H + 7) // 8)

        @pl.when(j == nk - 1)
        def _():
            lane = lax.broadcasted_iota(jnp.int32, (tq, LANES), 1)
            lse = jnp.zeros((tq, LANES), F32)
            for h in range(H):
                den = acc[:, h * AUG + HEAD:(h + 1) * AUG]
                o_ref[:, h * HEAD:(h + 1) * HEAD] = acc[:, h * AUG:h * AUG + HEAD] / den
                lse = jnp.where(lane == h, m_scr[h] + jnp.log(den), lse)
            lse_ref[...] = lse

    kv_idx = lambda i, j, f: (jnp.minimum(j, i), 0)
    q_idx = lambda i, j, f: (i, 0)
    return pl.pallas_call(
        body, name=name,
        grid_spec=pltpu.PrefetchScalarGridSpec(
            num_scalar_prefetch=1, grid=(nq, nk),
            in_specs=[pl.BlockSpec((tq, H * AUG), q_idx), pl.BlockSpec((tk, H * AUG), kv_idx), pl.BlockSpec((tk, H * AUG), kv_idx)],
            out_specs=[pl.BlockSpec((tq, H * HEAD), q_idx), pl.BlockSpec((tq, LANES), q_idx)],
            scratch_shapes=[pltpu.VMEM((tq, H * AUG), F32), pltpu.VMEM((H, tq, LANES), F32)]),
        out_shape=[jax.ShapeDtypeStruct((T, H * HEAD), F32), jax.ShapeDtypeStruct((T, LANES), F32)],
        compiler_params=_params(("parallel", "arbitrary")))(live, qa, ka, va)


def _fox_ds(q_ref, k_ref, v_ref, do_ref, o_ref, lse_ref, h, tq, tk, diagonal):
    hs = slice(h * HEAD, (h + 1) * HEAD)
    s, keep = _fox_logits(q_ref, k_ref, h, tq, tk, diagonal)
    p = jnp.exp(s - lse_ref[:, h:h + 1])
    if diagonal:
        p = jnp.where(keep, p, 0.0)
    do = do_ref[:, hs]
    dp = lax.dot_general(do, v_ref[:, h * AUG:h * AUG + HEAD], (((1,), (1,)), ((), ())), preferred_element_type=F32)
    delta = jnp.sum(do.astype(F32) * o_ref[:, hs], axis=1, keepdims=True)
    return p, p * (dp - delta)


def fox_bwd_kv(qa, ka, va, do, o, lse, live, H, *, name):
    T, W = do.shape
    tq = tk = min(FOX_TILE, T)
    nq, nk = T // tq, T // tk

    def body(live_ref, q_ref, k_ref, v_ref, do_ref, o_ref, lse_ref, dk_ref, dv_ref, dc_ref, dk_acc, dv_acc, dc_acc):
        j, i = pl.program_id(0), pl.program_id(1)

        @pl.when(i == 0)
        def _():
            dk_acc[...] = jnp.zeros_like(dk_acc)
            dv_acc[...] = jnp.zeros_like(dv_acc)
            dc_acc[...] = jnp.zeros_like(dc_acc)

        def head(h, diagonal):
            hs = slice(h * HEAD, (h + 1) * HEAD)
            p, ds = _fox_ds(q_ref, k_ref, v_ref, do_ref, o_ref, lse_ref, h, tq, tk, diagonal)
            dv_acc[:, hs] += lax.dot_general(p.astype(BF16), do_ref[:, hs], (((0,), (0,)), ((), ())),
                                             preferred_element_type=F32)
            dk_acc[:, hs] += lax.dot_general(ds.astype(BF16), q_ref[:, h * AUG:h * AUG + HEAD],
                                             (((0,), (0,)), ((), ())), preferred_element_type=F32)
            dc_acc[h] -= jnp.broadcast_to(jnp.sum(ds, axis=0, keepdims=True), (8, tk))

        _dispatch(live_ref, H, i, j, i > j, i == j, head)

        @pl.when(i == nq - 1)
        def _():
            dk_ref[...] = dk_acc[...].astype(dk_ref.dtype)
            dv_ref[...] = dv_acc[...].astype(dv_ref.dtype)
            row = lax.broadcasted_iota(jnp.int32, (8, tk), 0)
            dc = jnp.zeros((8, tk), F32)
            for h in range(H):
                dc = jnp.where(row == h, dc_acc[h], dc)
            dc_ref[...] = dc

    q_idx = lambda j, i, f: (jnp.maximum(i, j), 0)
    kv_idx = lambda j, i, f: (j, 0)
    return pl.pallas_call(
        body, name=name,
        grid_spec=pltpu.PrefetchScalarGridSpec(
            num_scalar_prefetch=1, grid=(nk, nq),
            in_specs=[pl.BlockSpec((tq, H * AUG), q_idx), pl.BlockSpec((tk, H * AUG), kv_idx), pl.BlockSpec((tk, H * AUG), kv_idx),
                      pl.BlockSpec((tq, W), q_idx), pl.BlockSpec((tq, W), q_idx), pl.BlockSpec((tq, LANES), q_idx)],
            out_specs=[pl.BlockSpec((tk, W), kv_idx), pl.BlockSpec((tk, W), kv_idx),
                       pl.BlockSpec((8, tk), lambda j, i, f: (0, j))],
            scratch_shapes=[pltpu.VMEM((tk, W), F32), pltpu.VMEM((tk, W), F32), pltpu.VMEM((H, 8, tk), F32)]),
        out_shape=[jax.ShapeDtypeStruct((T, W), BF16), jax.ShapeDtypeStruct((T, W), BF16), jax.ShapeDtypeStruct((8, T), F32)],
        compiler_params=_params(("parallel", "arbitrary")))(live, qa, ka, va, do, o, lse)


def fox_bwd_q(qa, ka, va, do, o, lse, live, H, *, name):
    T, W = do.shape
    tq = tk = min(FOX_TILE, T)
    nq, nk = T // tq, T // tk

    def body(live_ref, q_ref, k_ref, v_ref, do_ref, o_ref, lse_ref, dq_ref, dq_acc):
        i, j = pl.program_id(0), pl.program_id(1)

        @pl.when(j == 0)
        def _():
            dq_acc[...] = jnp.zeros_like(dq_acc)

        def head(h, diagonal):
            hs = slice(h * HEAD, (h + 1) * HEAD)
            _, ds = _fox_ds(q_ref, k_ref, v_ref, do_ref, o_ref, lse_ref, h, tq, tk, diagonal)
            dq_acc[:, hs] += lax.dot_general(ds.astype(BF16), k_ref[:, h * AUG:h * AUG + HEAD],
                                             (((1,), (0,)), ((), ())), preferred_element_type=F32)

        _dispatch(live_ref, H, i, j, j < i, j == i, head)

        @pl.when(j == nk - 1)
        def _():
            dq_ref[...] = (dq_acc[...] * (HEAD ** -0.5)).astype(dq_ref.dtype)

    q_idx = lambda i, j, f: (i, 0)
    kv_idx = lambda i, j, f: (jnp.minimum(j, i), 0)
    return pl.pallas_call(
        body, name=name,
        grid_spec=pltpu.PrefetchScalarGridSpec(
            num_scalar_prefetch=1, grid=(nq, nk),
            in_specs=[pl.BlockSpec((tq, H * AUG), q_idx), pl.BlockSpec((tk, H * AUG), kv_idx), pl.BlockSpec((tk, H * AUG), kv_idx),
                      pl.BlockSpec((tq, W), q_idx), pl.BlockSpec((tq, W), q_idx), pl.BlockSpec((tq, LANES), q_idx)],
            out_specs=pl.BlockSpec((tq, W), q_idx),
            scratch_shapes=[pltpu.VMEM((tq, W), F32)]),
        out_shape=jax.ShapeDtypeStruct((T, W), BF16),
        compiler_params=_params(("parallel", "arbitrary")))(live, qa, ka, va, do, o, lse)


def _gates_fn(small, a_log_l, dt_bias_l, b_f_l):
    lane = lax.broadcasted_iota(jnp.int32, small.shape, 1)
    beta = _sigmoid(small)
    g = -jnp.exp(a_log_l) * _softplus(small + dt_bias_l)
    lf = _log_sigmoid(small + b_f_l)
    return (jnp.where(lane < LANE_A, beta, jnp.where(lane < LANE_F, g, jnp.where(lane < LANE_F + 8, lf, 0.0))),)


def _gated_norm_fn(o, z, g):
    return (_rms(o, g) * _silu(z),)


def _merge_fn(ya, yb, ym, ga, gb, gm):
    return (_sigmoid(ga) * ya + _sigmoid(gb) * yb + _sigmoid(gm) * ym,)


def _mem_attn_fn(nh, dh, mq, kn, v, gq):
    outs = []
    for h in range(nh):
        hs = slice(h * dh, (h + 1) * dh)
        qn = _rms(mq[:, hs], gq)
        s = NT(qn, kn[:, hs]) * (dh ** -0.5)
        e = jnp.exp(s - jnp.max(s, axis=1, keepdims=True))
        p = e / jnp.sum(e, axis=1, keepdims=True)
        outs.append(NN(p, v[:, hs]))
    return (jnp.concatenate(outs, axis=1),)


def sum_squares(x, *, name, tr=512):
    T, D = x.shape
    tr = min(tr, T)

    def body(x_ref, o_ref):
        @pl.when(pl.program_id(0) == 0)
        def _():
            o_ref[...] = jnp.zeros_like(o_ref)

        v = x_ref[...]
        o_ref[...] += jnp.sum(jnp.sum(v * v, axis=1, keepdims=True), axis=0, keepdims=True)

    return pl.pallas_call(
        body, name=name, grid=(T // tr,), in_specs=[pl.BlockSpec((tr, D), lambda i: (i, 0))],
        out_specs=pl.BlockSpec((1, LANES), lambda i: (0, 0)), out_shape=jax.ShapeDtypeStruct((1, LANES), F32),
        compiler_params=_params(("arbitrary",)))(x)


class Comm:
    def __init__(self, arrays, out_shape, sems, start, forward, finish):
        self.arrays, self.out_shape, self.sems = list(arrays), list(out_shape), list(sems)
        self.start, self.forward, self.finish = start, forward, finish


def _place():
    x, y, c = lax.axis_index("x"), lax.axis_index("y"), lax.axis_index("c")
    chips = [(1 - x, y), (x, 1 - y), (1 - x, 1 - y)]
    return x, y, c, chips


def comm_gather(arrays):
    n = len(arrays)
    lin = lambda px, py, pc: 4 * px + 2 * py + pc

    def copy(ins, outs, sems, a, k, block, to, src=None):
        slot = outs[a].at[lin(*block)]
        return pltpu.make_async_remote_copy(src_ref=slot if src is None else src, dst_ref=slot, send_sem=sems[0].at[a, k],
                                            recv_sem=sems[1].at[a, k], device_id=to, device_id_type=MESH)

    def local(ins, outs, sems, a):
        x, y, c, _ = _place()
        return pltpu.make_async_copy(ins[a], outs[a].at[lin(x, y, c)], sems[2].at[a])

    def start(ins, outs, sems):
        x, y, c, chips = _place()
        for a in range(n):
            local(ins, outs, sems, a).start()
        for a in range(n):
            for j, chip in enumerate(chips):
                copy(ins, outs, sems, a, 1 + j, (x, y, c), (*chip, c), src=ins[a]).start()
            copy(ins, outs, sems, a, 0, (x, y, c), (x, y, 1 - c), src=ins[a]).start()

    def forward(ins, outs, sems):
        x, y, c, chips = _place()
        for a in range(n):
            for j, chip in enumerate(chips):
                copy(ins, outs, sems, a, 1 + j, (*chip, c), (x, y, c)).wait_recv()
                copy(ins, outs, sems, a, 4 + j, (*chip, c), (x, y, 1 - c)).start()

    def finish(ins, outs, sems):
        x, y, c, chips = _place()
        for a in range(n):
            copy(ins, outs, sems, a, 0, (x, y, 1 - c), (x, y, c)).wait_recv()
            for j, chip in enumerate(chips):
                copy(ins, outs, sems, a, 4 + j, (*chip, 1 - c), (x, y, c)).wait_recv()
        for a in range(n):
            for j, chip in enumerate(chips):
                copy(ins, outs, sems, a, 1 + j, (x, y, c), (*chip, c), src=ins[a]).wait_send()
                copy(ins, outs, sems, a, 4 + j, (*chip, c), (x, y, 1 - c)).wait_send()
            copy(ins, outs, sems, a, 0, (x, y, c), (x, y, 1 - c), src=ins[a]).wait_send()
            local(ins, outs, sems, a).wait()

    return Comm(arrays, [jax.ShapeDtypeStruct((N_DEV,) + a.shape, a.dtype) for a in arrays],
                [pltpu.SemaphoreType.DMA((n, 7)), pltpu.SemaphoreType.DMA((n, 7)), pltpu.SemaphoreType.DMA((n,))],
                start, forward, finish)


def comm_direct(arrays, scatter):
    n = len(arrays)

    def peers():
        x, y, c = lax.axis_index("x"), lax.axis_index("y"), lax.axis_index("c")
        out = []
        for r in range(1, N_DEV):
            px, py, pc = (1 - x if r & 4 else x), (1 - y if r & 2 else y), (1 - c if r & 1 else c)
            out.append((r, (px, py, pc), 4 * px + 2 * py + pc))
        return 4 * x + 2 * y + c, out

    def remote(ins, outs, sems, a, r, dev, src_slot, dst_slot):
        return pltpu.make_async_remote_copy(
            src_ref=ins[a].at[src_slot] if scatter[a] else ins[a], dst_ref=outs[a].at[dst_slot],
            send_sem=sems[0].at[a, r - 1], recv_sem=sems[1].at[a, r - 1], device_id=dev, device_id_type=MESH)

    def local(ins, outs, sems, a, me):
        return pltpu.make_async_copy(ins[a].at[me] if scatter[a] else ins[a], outs[a].at[me], sems[2].at[a])

    def start(ins, outs, sems):
        me, ps = peers()
        for a in range(n):
            local(ins, outs, sems, a, me).start()
        for a in range(n):
            for r, dev, lin in ps:
                remote(ins, outs, sems, a, r, dev, lin, me).start()

    def finish(ins, outs, sems):
        me, ps = peers()
        for a in range(n):
            for r, dev, lin in ps:
                remote(ins, outs, sems, a, r, dev, lin, lin).wait_recv()
        for a in range(n):
            for r, dev, lin in ps:
                remote(ins, outs, sems, a, r, dev, lin, me).wait_send()
            local(ins, outs, sems, a, me).wait()

    return Comm(arrays, [jax.ShapeDtypeStruct(a.shape if sc else (N_DEV,) + a.shape, a.dtype) for a, sc in zip(arrays, scatter)],
                [pltpu.SemaphoreType.DMA((n, N_DEV - 1)), pltpu.SemaphoreType.DMA((n, N_DEV - 1)),
                 pltpu.SemaphoreType.DMA((n,))], start, None, finish)


def comm_to_sibling(parts):
    n = len(parts)

    def copy(ins, outs, sems, a, k):
        x, y, c, _ = _place()
        return pltpu.make_async_remote_copy(src_ref=ins[a].at[2 * k + (1 - c)], dst_ref=outs[a].at[k], send_sem=sems[0].at[a, k],
                                            recv_sem=sems[1].at[a, k], device_id=(x, y, 1 - c), device_id_type=MESH)

    def start(ins, outs, sems):
        for a in range(n):
            for k in range(4):
                copy(ins, outs, sems, a, k).start()

    def finish(ins, outs, sems):
        for a in range(n):
            for k in range(4):
                copy(ins, outs, sems, a, k).wait()

    return Comm(parts, [jax.ShapeDtypeStruct((4,) + p.shape[1:], p.dtype) for p in parts],
                [pltpu.SemaphoreType.DMA((n, 4)), pltpu.SemaphoreType.DMA((n, 4))], start, None, finish)


def comm_to_owner_chip(sums):
    n = len(sums)

    def remote(ins, outs, sems, a, j, src_k, dst_k, chip):
        _, _, c, _ = _place()
        return pltpu.make_async_remote_copy(src_ref=ins[a].at[src_k], dst_ref=outs[a].at[dst_k], send_sem=sems[0].at[a, j],
                                            recv_sem=sems[1].at[a, j], device_id=(*chip, c), device_id_type=MESH)

    def local(ins, outs, sems, a):
        x, y, _, _ = _place()
        return pltpu.make_async_copy(ins[a].at[2 * x + y], outs[a].at[2 * x + y], sems[2].at[a])

    def start(ins, outs, sems):
        x, y, _, chips = _place()
        for a in range(n):
            local(ins, outs, sems, a).start()
            for j, (px, py) in enumerate(chips):
                remote(ins, outs, sems, a, j, 2 * px + py, 2 * x + y, (px, py)).start()

    def finish(ins, outs, sems):
        x, y, _, chips = _place()
        for a in range(n):
            for j, (px, py) in enumerate(chips):
                remote(ins, outs, sems, a, j, 2 * x + y, 2 * px + py, (px, py)).wait_recv()
        for a in range(n):
            for j, (px, py) in enumerate(chips):
                remote(ins, outs, sems, a, j, 2 * px + py, 2 * x + y, (px, py)).wait_send()
            local(ins, outs, sems, a).wait()

    return Comm(sums, [jax.ShapeDtypeStruct(s.shape, s.dtype) for s in sums],
                [pltpu.SemaphoreType.DMA((n, 3)), pltpu.SemaphoreType.DMA((n, 3)), pltpu.SemaphoreType.DMA((n,))],
                start, None, finish)


def pair_sum(parts, sib, *, name, tr=512):
    _, R, Cc = parts.shape
    tr = min(tr, R)
    assert R % tr == 0
    core = lax.axis_index("c").astype(jnp.int32).reshape(1)

    def body(c_ref, p_ref, s_ref, o_ref):
        o_ref[0] = (p_ref[0, 0].astype(F32) + s_ref[0].astype(F32)).astype(o_ref.dtype)

    return pl.pallas_call(
        body, name=name,
        grid_spec=pltpu.PrefetchScalarGridSpec(
            num_scalar_prefetch=1, grid=(4, R // tr),
            in_specs=[pl.BlockSpec((1, 1, tr, Cc), lambda k, i, c: (k, c[0], i, 0)),
                      pl.BlockSpec((1, tr, Cc), lambda k, i, c: (k, i, 0))],
            out_specs=pl.BlockSpec((1, tr, Cc), lambda k, i, c: (k, i, 0))),
        out_shape=jax.ShapeDtypeStruct((4, R, Cc), BF16),
        compiler_params=_params(("parallel", "parallel")))(core, parts.reshape(4, 2, R, Cc), sib)


def run_comm(comm, *, name):
    n = len(comm.arrays)

    def body(*refs):
        ins, outs, sems = refs[:n], refs[n:2 * n], refs[2 * n:]
        comm.start(ins, outs, sems)
        if comm.forward is not None:
            comm.forward(ins, outs, sems)
        comm.finish(ins, outs, sems)

    any_spec = pl.BlockSpec(memory_space=pl.ANY)
    return pl.pallas_call(body, name=name, in_specs=[any_spec] * n, out_specs=[any_spec] * n, out_shape=comm.out_shape,
                          scratch_shapes=comm.sems)(*comm.arrays)


def adamw(parts, w, m, v, *, name, tr=128):
    R, Cc = w.shape
    tr = min(tr, R)
    assert R % tr == 0
    n_parts = parts.shape[0]

    def body(p_ref, w_ref, m_ref, v_ref, g_ref, d_ref, nm_ref, nv_ref):
        g = p_ref[0].astype(F32)
        for s in range(1, n_parts):
            g = g + p_ref[s].astype(F32)
        nm = ADAM_B1 * m_ref[...] + (1.0 - ADAM_B1) * g
        nv = ADAM_B2 * v_ref[...] + (1.0 - ADAM_B2) * (g * g)
        m_hat = nm / (1.0 - ADAM_B1 ** ADAM_STEP)
        v_hat = nv / (1.0 - ADAM_B2 ** ADAM_STEP)
        g_ref[...] = g
        d_ref[...] = -ADAM_LR * (m_hat / (jnp.sqrt(v_hat) + ADAM_EPS) + ADAM_WD * w_ref[...])
        nm_ref[...] = nm
        nv_ref[...] = nv

    spec = pl.BlockSpec((tr, Cc), lambda i: (i, 0))
    return pl.pallas_call(
        body, name=name, grid=(R // tr,),
        in_specs=[pl.BlockSpec((n_parts, tr, Cc), lambda i: (0, i, 0)), spec, spec, spec], out_specs=[spec] * 4,
        out_shape=[jax.ShapeDtypeStruct((R, Cc), F32)] * 4, compiler_params=_params(("parallel",)))(parts, w, m, v)


def _lanes(vec, base):
    return jnp.pad(vec[None].astype(F32), ((0, 0), (base, LANES - base - vec.shape[0])))


def _col_shards(full_w):
    R, Ct = full_w.shape
    return jnp.transpose(full_w.reshape(R, N_DEV, Ct // N_DEV), (1, 0, 2))


def _from_col_shards(g):
    return jnp.transpose(g, (1, 0, 2)).reshape(g.shape[1], -1)


def kernel(x, mem, g_mix, w_in, conv_w, a_log, dt_bias, gdn_norm_g, fox_b_f, fox_q_norm, fox_k_norm, g_mem, w_mem_kv, mem_q_norm, mem_k_norm, w_up_gdn, w_up_fox, w_up_mem, w_out, g_mlp, w_ff1, w_ff2, loss_target, m_g_mix, m_w_in, m_conv_w, m_a_log, m_dt_bias, m_gdn_norm_g, m_fox_b_f, m_fox_q_norm, m_fox_k_norm, m_g_mem, m_w_mem_kv, m_mem_q_norm, m_mem_k_norm, m_w_up_gdn, m_w_up_fox, m_w_up_mem, m_w_out, m_g_mlp, m_w_ff1, m_w_ff2, v_g_mix, v_w_in, v_conv_w, v_a_log, v_dt_bias, v_gdn_norm_g, v_fox_b_f, v_fox_q_norm, v_fox_k_norm, v_g_mem, v_w_mem_kv, v_mem_q_norm, v_mem_k_norm, v_w_up_gdn, v_w_up_fox, v_w_up_mem, v_w_out, v_g_mlp, v_w_ff1, v_w_ff2):
    loc = dict(locals())
    big_names = ["w_in", "conv_w", "w_mem_kv", "w_up_gdn", "w_up_fox", "w_up_mem", "w_out", "w_ff1", "w_ff2"]
    col_sharded = {"w_in", "conv_w", "w_up_gdn", "w_up_fox", "w_up_mem", "w_ff1"}
    small_names = ["g_mix", "a_log", "dt_bias", "gdn_norm_g", "fox_b_f", "fox_q_norm", "fox_k_norm", "g_mem",
                   "mem_q_norm", "mem_k_norm", "g_mlp"]

    xs, tgt, mems = x[0], loss_target[0], mem[0]
    T, D = xs.shape
    HG = a_log.shape[1]
    HF = fox_b_f.shape[1]
    DM = mem_q_norm.shape[1]
    GQK, GV = HG * HEAD, HG * HEAD
    GQKV = 2 * GQK + GV
    FW = HF * HEAD
    MW = w_mem_kv.shape[2] // 2
    HM = MW // DM
    assert HG <= 8 and HF <= 8

    shard = {n: loc[n][0].astype(BF16) for n in big_names}
    first, rest, last = big_names[:2], big_names[2:-1], big_names[-1:]
    W = {}

    def take(names, gathered):
        for n, g in zip(names, gathered):
            if n == "w_ff1":
                W[n] = g
            else:
                W[n] = _from_col_shards(g) if n in col_sharded else g.reshape(-1, g.shape[2])

    rms_fn = lambda t, g: (_rms(t, g),)
    h, *gathered = rowwise(rms_fn, [full(xs)], [g_mix], [(D, D, BF16)], name="rms_mix", comm=comm_gather([shard[n] for n in first]))
    take(first, gathered)
    widths = [GQKV, GV, HG, HG, FW, FW, FW, HF, MW, 3 * D]
    offs = np.concatenate([[0], np.cumsum(widths)]).tolist()
    seg = [W["w_in"][:, offs[i]:offs[i + 1]] for i in range(len(widths))]
    w_big = jnp.concatenate([seg[0], seg[1], seg[4], seg[5], seg[6], seg[8], seg[9]], axis=1)
    pad8 = lambda s: jnp.pad(s, ((0, 0), (0, 8 - s.shape[1])))
    w_small = jnp.concatenate([pad8(seg[2]), pad8(seg[3]), pad8(seg[7]), jnp.zeros((D, LANES - 24), BF16)], axis=1)
    o_z, o_fq, o_fk, o_fv = GQKV, GQKV + GV, GQKV + GV + FW, GQKV + GV + 2 * FW
    o_mq = o_fv + FW
    o_gt = o_mq + MW
    WB = o_gt + 3 * D
    conv_full = W["conv_w"].astype(F32)

    a_log_l, dt_bias_l, b_f_l = _lanes(a_log[0], LANE_A), _lanes(dt_bias[0], LANE_A), _lanes(fox_b_f[0], LANE_F)

    big, *gathered = matmul(h, w_big, mode="nn", name="proj_big", comm=comm_gather([shard[n] for n in rest]))
    take(rest, gathered)
    small = matmul(h, w_small, mode="nn", name="proj_small", out_dtypes=(F32,))
    bgf, = rowwise(_gates_fn, [full(small)], [a_log_l, dt_bias_l, b_f_l], [(LANES, LANES, F32)], name="gates")

    cn = conv_fwd(big, conv_full, HG, name="conv")
    o_gdn, ss, tinv = gdn_fwd(cn, bgf, HG, name="gdn_fwd")
    oa, = rowwise(_gated_norm_fn, [(o_gdn, HEAD, 0, 1), (big, HEAD, o_z // HEAD, 1)], [gdn_norm_g],
                  [(GV, HEAD, BF16)], name="gated_norm", ncb=HG)

    cum = cumsum_tokens(bgf, reverse=False, name="cumsum")
    fqa = fox_prep(big, o_fq, fox_q_norm, cum, 0, HF, name="fox_prep_q")
    fka = fox_prep(big, o_fk, fox_k_norm, cum, 1, HF, name="fox_prep_k")
    fva = fox_prep(big, o_fv, fox_k_norm, cum, 2, HF, name="fox_prep_v")
    live = fox_live_tiles(cum, fox_q_norm, fox_k_norm, HF, T)
    ob, lse = fox_fwd(fqa, fka, fva, live, HF, name="fox_fwd")

    memn, = rowwise(rms_fn, [full(mems)], [g_mem], [(D, D, BF16)], name="rms_mem")
    kv_m = matmul(memn, W["w_mem_kv"], mode="nn", name="mem_kv", out_dtypes=(F32,))
    kmn, = rowwise(rms_fn, [(kv_m, DM, 0, 1)], [mem_k_norm], [(MW, DM, F32)], name="mem_knorm", ncb=HM)
    vm = kv_m[:, MW:]
    mem_fn = functools.partial(_mem_attn_fn, HM, DM)
    om, = rowwise(mem_fn, [(big, MW, o_mq // MW, 0)], [kmn, vm, mem_q_norm], [(MW, MW, BF16)], name="mem_attn")

    ya = matmul(oa, W["w_up_gdn"], mode="nn", name="up_gdn")
    yb = matmul(ob, W["w_up_fox"], mode="nn", name="up_fox")
    ym = matmul(om, W["w_up_mem"], mode="nn", name="up_mem")
    cbm = min(512, D)
    gate_rows = [(big, cbm, (o_gt + b * D) // cbm, 1) for b in range(3)]
    merge_rows = [(ya, cbm, 0, 1), (yb, cbm, 0, 1), (ym, cbm, 0, 1)] + gate_rows
    y, = rowwise(_merge_fn, merge_rows, [], [(D, cbm, BF16)], name="merge", ncb=D // cbm)
    x1 = matmul(y, W["w_out"], mode="nn", name="out_proj", out_dtypes=(F32,), extras=(xs,),
                epi=lambda r, res: (r + res,))

    h2, = rowwise(rms_fn, [full(x1)], [g_mlp], [(D, D, BF16)], name="rms_mlp")
    u_ff, a_ff, *gathered = matmul(h2, W["w_ff1"], mode="nn", name="ff1", out_dtypes=(BF16, BF16), b_shards=True,
                                   epi=lambda r: (r, jnp.square(jnp.maximum(r, 0.0))),
                                   comm=comm_gather([shard[n] for n in last]))
    take(last, gathered)
    d_out = matmul(a_ff, W["w_ff2"], mode="nn", name="ff2_loss", out_dtypes=(F32,), extras=(x1, tgt),
                   epi=lambda r, res, t: ((r + res - t) * (1.0 / D),))
    loss_local = 0.5 * D * sum_squares(d_out, name="loss_sum")[0, 0]
    loss = lax.psum(loss_local, ("x", "y", "c"))

    G = {}
    d_u = matmul(d_out, W["w_ff2"], mode="nt", name="d_ff2_in", extras=(u_ff,),
                 epi=lambda r, u: (r * 2.0 * jnp.maximum(u.astype(F32), 0.0),))
    d_h2 = matmul(d_u, W["w_ff1"], mode="nt", name="d_ff1_in", b_shards=True)
    d_x1, G["g_mlp"] = rowwise_bwd(rms_fn, [full(x1)], [g_mlp], [full(d_h2)], [(0, F32)], [0], name="d_rms_mlp",
                                   adds=[full(d_out)])
    d_y = matmul(d_x1, W["w_out"], mode="nt", name="d_out_proj_in")
    G["w_out"] = matmul(y, d_x1, mode="tn", name="d_w_out")
    d_ya, d_yb, d_ym, d_ga, d_gb, d_gm = rowwise_bwd(
        _merge_fn, merge_rows, [], [(d_y, cbm, 0, 1)], [(k, BF16) for k in range(6)], [], name="d_merge", ncb=D // cbm)
    d_oa = matmul(d_ya, W["w_up_gdn"], mode="nt", name="d_up_gdn_in")
    d_ob = matmul(d_yb, W["w_up_fox"], mode="nt", name="d_up_fox_in")
    d_om = matmul(d_ym, W["w_up_mem"], mode="nt", name="d_up_mem_in")
    G["w_up_gdn"] = matmul(oa, d_ya, mode="tn", name="d_w_up_gdn")
    G["w_up_fox"] = matmul(ob, d_yb, mode="tn", name="d_w_up_fox")
    G["w_up_mem"] = matmul(om, d_ym, mode="tn", name="d_w_up_mem")

    d_mq, d_kmn, d_vm, G["mem_q_norm"] = rowwise_bwd(
        mem_fn, [(big, MW, o_mq // MW, 0)], [kmn, vm, mem_q_norm], [full(d_om)], [(0, BF16)], [0, 1, 2], name="d_mem_attn")
    d_km, G["mem_k_norm"] = rowwise_bwd(rms_fn, [(kv_m, DM, 0, 1)], [mem_k_norm], [(d_kmn, DM, 0, 1)], [(0, F32)], [0],
                                         name="d_mem_knorm", ncb=HM)
    d_kv_m = jnp.concatenate([d_km, d_vm], axis=1)
    G["w_mem_kv"] = matmul(memn, d_kv_m, mode="tn", name="d_w_mem_kv")
    d_memn = matmul(d_kv_m, W["w_mem_kv"], mode="nt", name="d_mem_kv_in")
    _, G["g_mem"] = rowwise_bwd(rms_fn, [full(mems)], [g_mem], [full(d_memn)], [(0, BF16)], [0], name="d_rms_mem")

    d_fkn, d_fv, d_cum_t = fox_bwd_kv(fqa, fka, fva, d_ob, ob, lse, live, HF, name="fox_bwd_kv")
    d_fqn = fox_bwd_q(fqa, fka, fva, d_ob, ob, lse, live, HF, name="fox_bwd_q")
    d_fq, G["fox_q_norm"] = rowwise_bwd(rms_fn, [(big, HEAD, o_fq // HEAD, 1)], [fox_q_norm], [(d_fqn, HEAD, 0, 1)],
                                         [(0, BF16)], [0], name="d_fox_qnorm", ncb=HF)
    d_fk, G["fox_k_norm"] = rowwise_bwd(rms_fn, [(big, HEAD, o_fk // HEAD, 1)], [fox_k_norm], [(d_fkn, HEAD, 0, 1)],
                                         [(0, BF16)], [0], name="d_fox_knorm", ncb=HF)
    d_cum = jnp.pad(d_cum_t[:HF].T, ((0, 0), (LANE_F, LANES - LANE_F - HF)))
    d_logf = cumsum_tokens(d_cum, reverse=True, name="cumsum_rev")

    d_o_gdn, d_z, G["gdn_norm_g"] = rowwise_bwd(
        _gated_norm_fn, [(o_gdn, HEAD, 0, 1), (big, HEAD, o_z // HEAD, 1)], [gdn_norm_g], [(d_oa, HEAD, 0, 1)],
        [(0, F32), (1, BF16)], [0], name="d_gated_norm", ncb=HG)
    d_cn, d_bg = gdn_bwd(cn, bgf, ss, tinv, d_o_gdn, HG, name="gdn_bwd")
    d_conv_y, G["conv_w"] = conv_bwd_taps(big, conv_full, d_cn, HG, name="d_conv_taps")
    d_qkv = conv_bwd_input(d_conv_y, conv_full, name="d_conv_in")
    d_small, d_al, d_dt, d_bf = rowwise_bwd(_gates_fn, [full(small)], [a_log_l, dt_bias_l, b_f_l], [full(d_bg + d_logf)],
                                            [(0, F32)], [0, 1, 2], name="d_gates")
    G["a_log"], G["dt_bias"], G["fox_b_f"] = (d_al[:, LANE_A:LANE_A + HG], d_dt[:, LANE_A:LANE_A + HG],
                                               d_bf[:, LANE_F:LANE_F + HF])

    def parts(n):
        g = G[n].astype(BF16)
        if g.ndim == 3:
            return g
        return _col_shards(g) if n in col_sharded else g.reshape(N_DEV, -1, g.shape[1])

    recv = {}

    def carried(names, out):
        for n, r in zip(names, out):
            recv[n] = r

    d_big = jnp.concatenate([d_qkv, d_z, d_fq, d_fk, d_fv, d_mq, d_ga, d_gb, d_gm], axis=1)
    group = ["conv_w", "w_mem_kv", "w_up_gdn", "w_up_fox", "w_up_mem", "w_out"]
    G["w_ff2"], *out = matmul(a_ff, d_out, mode="tn", name="d_w_ff2",
                              comm=comm_direct([parts(n) for n in group], [True] * len(group)))
    carried(group, out)
    G["w_ff1"], *out = matmul(h2, d_u, mode="tn", name="d_w_ff1", out_shards=True,
                              comm=comm_direct([parts("w_ff2")], [True]))
    carried(["w_ff2"], out)
    g_big, *out = matmul(h, d_big, mode="tn", name="d_w_big", comm=comm_direct([parts("w_ff1")], [True]))
    carried(["w_ff1"], out)
    g_small = matmul(h, d_small, mode="tn", name="d_w_small", out_dtypes=(F32,))
    cols = lambda a, o, wd: a[:, o:o + wd]
    G["w_in"] = jnp.concatenate([
        cols(g_big, 0, GQKV), cols(g_big, o_z, GV), cols(g_small, LANE_B, HG), cols(g_small, LANE_A, HG),
        cols(g_big, o_fq, FW), cols(g_big, o_fk, FW), cols(g_big, o_fv, FW), cols(g_small, LANE_F, HF),
        cols(g_big, o_mq, MW), cols(g_big, o_gt, 3 * D)], axis=1)
    p_in = parts("w_in")
    d_h_s, from_sibling = matmul(d_small, w_small, mode="nt", name="d_proj_small_in", out_dtypes=(F32,),
                                 comm=comm_to_sibling([p_in]))
    chip_sums = pair_sum(p_in, from_sibling, name="w_in_pair_sum")
    d_h, *out = matmul(d_big, w_big, mode="nt", name="d_proj_big_in", extras=(d_h_s,), epi=lambda r, e: (r + e,),
                       comm=comm_to_owner_chip([chip_sums]))
    carried(["w_in"], out)
    grad_x, G["g_mix"] = rowwise_bwd(rms_fn, [full(xs)], [g_mix], [full(d_h)], [(0, F32)], [0], name="d_rms_mix",
                                     adds=[full(d_x1)])
    grad_x = grad_x[None]

    small_sizes = [loc[n].shape[1] for n in small_names]
    pack = lambda d: jnp.concatenate([d[n].reshape(1, -1) for n in small_names], axis=1)
    npad = -sum(small_sizes) % LANES
    padp = lambda a: jnp.pad(a, ((0, 0), (0, npad)))
    recv_small, = run_comm(comm_direct([padp(pack(G))], [False]), name="gather_small_grads")

    res = {}
    for n in big_names:
        res[n] = [t[None] for t in adamw(recv[n], loc[n][0], loc["m_" + n][0], loc["v_" + n][0], name="adamw_" + n)]
    sm = adamw(recv_small, padp(pack({n: loc[n] for n in small_names})), padp(pack({n: loc["m_" + n] for n in small_names})),
               padp(pack({n: loc["v_" + n] for n in small_names})), name="adamw_small")
    so = np.concatenate([[0], np.cumsum(small_sizes)]).tolist()
    for i, n in enumerate(small_names):
        res[n] = [t[:, so[i]:so[i + 1]] for t in sm]

    order = ["g_mix", "w_in", "conv_w", "a_log", "dt_bias", "gdn_norm_g", "fox_b_f", "fox_q_norm", "fox_k_norm", "g_mem",
             "w_mem_kv", "mem_q_norm", "mem_k_norm", "w_up_gdn", "w_up_fox", "w_up_mem", "w_out", "g_mlp", "w_ff1", "w_ff2"]
    return (loss, grad_x, *[res[n][0] for n in order], *[res[n][1] for n in order],
            *[res[n][2] for n in order], *[res[n][3] for n in order])
```

```python
import functools

import jax
import jax.numpy as jnp
import numpy as np
from jax import lax
from jax.experimental import pallas as pl
from jax.experimental.pallas import tpu as pltpu

F32 = jnp.float32
BF16 = jnp.bfloat16
HI = lax.Precision.HIGHEST

EPS = 1e-6
GDN_CHUNK = 64
GDN_CONV = 4
HEAD = 128
LANES = 128
HALO = 16
N_DEV = 8
MESH = pl.DeviceIdType.MESH
VMEM_LIMIT_V7X = 56 * 1024 * 1024

ADAM_LR, ADAM_B1, ADAM_B2, ADAM_EPS, ADAM_WD, ADAM_STEP = 0.001, 0.9, 0.999, 1e-08, 0.01, 10

LANE_B, LANE_A, LANE_F = 0, 8, 16


def _params(sem):
    return pltpu.CompilerParams(dimension_semantics=sem, vmem_limit_bytes=VMEM_LIMIT_V7X)


def _dg(a, b, ca, cb, prec):
    nb = a.ndim - 2
    batch = tuple(range(nb))
    return lax.dot_general(a, b, (((ca + nb,), (cb + nb,)), (batch, batch)), precision=prec,
                           preferred_element_type=F32)


def _make_mm(prec, cast):
    def c(x):
        return x.astype(BF16) if cast else x

    @jax.custom_vjp
    def nn(a, b):
        return _dg(c(a), c(b), 1, 0, prec)

    @jax.custom_vjp
    def nt(a, b):
        return _dg(c(a), c(b), 1, 1, prec)

    @jax.custom_vjp
    def tn(a, b):
        return _dg(c(a), c(b), 0, 0, prec)

    nn.defvjp(lambda a, b: (nn(a, b), (a, b)), lambda r, g: (nt(g, r[1]), tn(r[0], g)))
    nt.defvjp(lambda a, b: (nt(a, b), (a, b)), lambda r, g: (nn(g, r[1]), tn(g, r[0])))
    tn.defvjp(lambda a, b: (tn(a, b), (a, b)), lambda r, g: (nt(r[1], g), nn(r[0], g)))
    return nn, nt, tn


NN, NT, TN = _make_mm(None, True)
NNH, NTH, TNH = _make_mm(lax.Precision.HIGH, False)
NNX, _, _ = _make_mm(HI, False)


def _sigmoid(x):
    return 1.0 / (1.0 + jnp.exp(-x))


def _silu(x):
    return x * _sigmoid(x)


def _softplus(x):
    return jnp.maximum(x, 0.0) + jnp.log(1.0 + jnp.exp(-jnp.abs(x)))


def _log_sigmoid(x):
    return -_softplus(-x)


def _rms(x, g):
    return x * lax.rsqrt(jnp.mean(x * x, axis=-1, keepdims=True) + EPS) * g


def _tile(n, target):
    t = target
    while t >= LANES:
        if n % t == 0:
            return t
        t //= 2
    return n


def matmul(a, b, *, mode, name, out_dtypes=(BF16,), epi=None, extras=(), tm=1024, tn=1024, tk=2048, comm=None,
           b_shards=False, out_shards=False):
    if b_shards:
        b_rows, b_cols = b.shape[1], N_DEV * b.shape[2]
    else:
        b_rows, b_cols = b.shape
    if mode == "nn":
        (M, K), (K2, N) = a.shape, (b_rows, b_cols)
    elif mode == "nt":
        (M, K), (N, K2) = a.shape, (b_rows, b_cols)
    else:
        (K, M), (K2, N) = a.shape, (b_rows, b_cols)
    assert K == K2, (name, a.shape, b.shape)
    tm, tn, tk = _tile(M, tm), _tile(N, tn), _tile(K, tk)
    if b_shards and mode == "nt":
        tk = K // N_DEV
    if (b_shards and mode != "nt") or out_shards:
        tn = N // N_DEV
    ni, nj, nk = M // tm, N // tn, K // tk
    a_spec = (pl.BlockSpec((tk, tm), lambda i, j, k: (k, i)) if mode == "tn"
              else pl.BlockSpec((tm, tk), lambda i, j, k: (i, k)))
    if b_shards:
        b_spec = (pl.BlockSpec((None, tn, tk), lambda i, j, k: (k, j, 0)) if mode == "nt"
                  else pl.BlockSpec((None, tk, tn), lambda i, j, k: (j, k, 0)))
    else:
        b_spec = (pl.BlockSpec((tn, tk), lambda i, j, k: (j, k)) if mode == "nt"
                  else pl.BlockSpec((tk, tn), lambda i, j, k: (k, j)))
    o_spec = pl.BlockSpec((tm, tn), lambda i, j, k: (i, j))
    w_spec = pl.BlockSpec((None, tm, tn), lambda i, j, k: (j, i, 0)) if out_shards else o_spec
    w_shape = (N_DEV, M, tn) if out_shards else (M, N)
    dims = {"nn": ((1,), (0,)), "nt": ((1,), (1,)), "tn": ((0,), (0,))}[mode]
    ne, no = len(extras), len(out_dtypes)
    nc = len(comm.arrays) if comm is not None else 0
    n_steps = ni * nj * nk

    def body(a_ref, b_ref, *rest):
        ex, c_in = rest[:ne], rest[ne:ne + nc]
        outs, c_out = rest[ne + nc:ne + nc + no], rest[ne + nc + no:ne + nc + no + nc]
        acc = rest[ne + nc + no + nc]
        c_sems = rest[ne + nc + no + nc + 1:]
        k = pl.program_id(2)
        step = (pl.program_id(0) * nj + pl.program_id(1)) * nk + k

        if comm is not None:
            pl.when(step == 0)(lambda: comm.start(c_in, c_out, c_sems))
            if comm.forward is not None and n_steps >= 3:
                pl.when(step == (3 * n_steps) // 4)(lambda: comm.forward(c_in, c_out, c_sems))

        @pl.when(k == 0)
        def _():
            acc[...] = jnp.zeros_like(acc)

        acc[...] += lax.dot_general(a_ref[...].astype(BF16), b_ref[...].astype(BF16), (dims, ((), ())),
                                    preferred_element_type=F32)

        @pl.when(k == nk - 1)
        def _():
            r = acc[...]
            vals = epi(r, *[e[...] for e in ex]) if epi is not None else (r,)
            for o, v in zip(outs, vals):
                o[...] = v.astype(o.dtype)

        if comm is not None:
            @pl.when(step == n_steps - 1)
            def _():
                if comm.forward is not None and n_steps < 3:
                    comm.forward(c_in, c_out, c_sems)
                comm.finish(c_in, c_out, c_sems)

    any_spec = pl.BlockSpec(memory_space=pl.ANY)
    sem = ("arbitrary",) * 3 if comm is not None else ("parallel", "parallel", "arbitrary")
    out = pl.pallas_call(
        body, name=name, grid=(ni, nj, nk),
        in_specs=[a_spec, b_spec] + [o_spec] * ne + [any_spec] * nc, out_specs=[w_spec] * no + [any_spec] * nc,
        out_shape=[jax.ShapeDtypeStruct(w_shape, d) for d in out_dtypes] + (comm.out_shape if comm is not None else []),
        scratch_shapes=[pltpu.VMEM((tm, tn), F32)] + (comm.sems if comm is not None else []),
        compiler_params=_params(sem))(a, b, *extras, *(comm.arrays if comm is not None else []))
    return out[0] if len(out) == 1 else out


def _row_spec(tr, cb, off, moves):
    return pl.BlockSpec((tr, cb), lambda i, j: (i, off + moves * j))


def _whole_spec(p):
    return pl.BlockSpec(p.shape, lambda i, j: (0,) * p.ndim)


def _row_tile(T, rows):
    widest = max(cb for (_, cb, _, _) in rows)
    return min(T, max(512, (1 << 19) // widest))


def rowwise(fn, rows, params, outs, *, name, ncb=1, tr=None, comm=None):
    T = rows[0][0].shape[0]
    tr = min(tr, T) if tr else _row_tile(T, rows)
    assert T % tr == 0
    nr, npar, no = len(rows), len(params), len(outs)
    nc = len(comm.arrays) if comm is not None else 0
    n_steps = (T // tr) * ncb

    def body(*refs):
        r, p, c_in = refs[:nr], refs[nr:nr + npar], refs[nr + npar:nr + npar + nc]
        o, c_out, c_sems = refs[nr + npar + nc:nr + npar + nc + no], refs[nr + npar + nc + no:nr + npar + 2 * nc + no], \
            refs[nr + npar + 2 * nc + no:]
        step = pl.program_id(0) * ncb + pl.program_id(1)
        if comm is not None:
            pl.when(step == 0)(lambda: comm.start(c_in, c_out, c_sems))
            if comm.forward is not None and n_steps >= 3:
                pl.when(step == (3 * n_steps) // 4)(lambda: comm.forward(c_in, c_out, c_sems))
        vals = fn(*[x[...].astype(F32) for x in r], *[x[...] for x in p])
        for oref, v in zip(o, vals):
            oref[...] = v.astype(oref.dtype)
        if comm is not None:
            @pl.when(step == n_steps - 1)
            def _():
                if comm.forward is not None and n_steps < 3:
                    comm.forward(c_in, c_out, c_sems)
                comm.finish(c_in, c_out, c_sems)

    any_spec = pl.BlockSpec(memory_space=pl.ANY)
    res = pl.pallas_call(
        body, name=name, grid=(T // tr, ncb),
        in_specs=([_row_spec(tr, cb, off, mv) for (_, cb, off, mv) in rows] + [_whole_spec(p) for p in params]
                  + [any_spec] * nc),
        out_specs=[_row_spec(tr, cb, 0, 1) for (_, cb, _) in outs] + [any_spec] * nc,
        out_shape=[jax.ShapeDtypeStruct((T, cols), d) for (cols, _, d) in outs] + (comm.out_shape if comm is not None else []),
        scratch_shapes=comm.sems if comm is not None else [],
        compiler_params=_params(("arbitrary", "arbitrary") if comm is not None else ("parallel", "parallel")))(
            *[r[0] for r in rows], *params, *(comm.arrays if comm is not None else []))
    return res


def rowwise_bwd(fn, rows, params, cots, drows, dparams, *, name, ncb=1, tr=None, adds=()):
    T = rows[0][0].shape[0]
    tr = min(tr, T) if tr else _row_tile(T, rows)
    assert T % tr == 0
    nr, npar, nc, ndr, na = len(rows), len(params), len(cots), len(drows), len(adds)

    def body(*refs):
        r, p, c = refs[:nr], refs[nr:nr + npar], refs[nr + npar:nr + npar + nc]
        base = nr + npar + nc + na
        ad, o_r, o_p = refs[base - na:base], refs[base:base + ndr], refs[base + ndr:]
        prim = [x[...].astype(F32) for x in r] + [x[...] for x in p]
        _, vjp = jax.vjp(lambda *a: tuple(fn(*a)), *prim)
        g = vjp(tuple(x[...].astype(F32) for x in c))
        for k, (oref, (idx, _)) in enumerate(zip(o_r, drows)):
            val = g[idx] + ad[k][...].astype(F32) if k < na else g[idx]
            oref[...] = val.astype(oref.dtype)
        first = jnp.logical_and(pl.program_id(0) == 0, pl.program_id(1) == 0)

        @pl.when(first)
        def _():
            for oref in o_p:
                oref[...] = jnp.zeros_like(oref)

        for oref, idx in zip(o_p, dparams):
            oref[...] += g[nr + idx]

    res = pl.pallas_call(
        body, name=name, grid=(T // tr, ncb),
        in_specs=([_row_spec(tr, cb, off, mv) for (_, cb, off, mv) in rows] + [_whole_spec(p) for p in params]
                  + [_row_spec(tr, cb, off, mv) for (_, cb, off, mv) in tuple(cots) + tuple(adds)]),
        out_specs=([_row_spec(tr, rows[idx][1], 0, 1) for (idx, _) in drows]
                   + [_whole_spec(params[idx]) for idx in dparams]),
        out_shape=([jax.ShapeDtypeStruct((T, ncb * rows[idx][1] if rows[idx][3] else rows[idx][1]), d)
                    for (idx, d) in drows]
                   + [jax.ShapeDtypeStruct(params[idx].shape, F32) for idx in dparams]),
        compiler_params=_params(("arbitrary", "arbitrary")))(
            *[r[0] for r in rows], *params, *[c[0] for c in cots], *[a[0] for a in adds])
    return res


def full(a):
    return (a, a.shape[1], 0, 0)


def cumsum_tokens(x, *, reverse, name, tb=256):
    T = x.shape[0]
    tb = min(tb, T)
    nb = T // tb
    idx = (lambda i: (nb - 1 - i, 0)) if reverse else (lambda i: (i, 0))

    def body(x_ref, o_ref, carry):
        @pl.when(pl.program_id(0) == 0)
        def _():
            carry[...] = jnp.zeros_like(carry)

        ii = lax.broadcasted_iota(jnp.int32, (tb, tb), 0)
        jj = lax.broadcasted_iota(jnp.int32, (tb, tb), 1)
        tri = ((ii <= jj) if reverse else (ii >= jj)).astype(F32)
        c = lax.dot_general(tri, x_ref[...], (((1,), (0,)), ((), ())), precision=HI,
                            preferred_element_type=F32) + carry[0:1, :]
        o_ref[...] = c
        carry[0:1, :] = c[0:1, :] if reverse else c[tb - 1:tb, :]

    return pl.pallas_call(
        body, name=name, grid=(nb,), in_specs=[pl.BlockSpec((tb, LANES), idx)],
        out_specs=pl.BlockSpec((tb, LANES), idx), out_shape=jax.ShapeDtypeStruct((T, LANES), F32),
        scratch_shapes=[pltpu.VMEM((8, LANES), F32)], compiler_params=_params(("arbitrary",)))(x)


def _conv_post(y, kind, dk):
    c = _silu(y)
    r = lax.rsqrt(jnp.sum(c * c, axis=-1, keepdims=True) + EPS)
    return jnp.where(kind == 0, c * r * (dk ** -0.5), jnp.where(kind == 1, c * r, c))


def _conv_taps(cur, prev, w, tr):
    ext = jnp.concatenate([prev, cur], axis=0)
    y = w[3:4, :] * cur
    for d in (1, 2, 3):
        y = y + w[3 - d:4 - d, :] * pltpu.roll(ext, d, 0)[HALO:HALO + tr]
    return y


def conv_fwd(big, w, n_qk_heads, *, name, tr=2048):
    T, W = big.shape[0], w.shape[1]
    tr = min(tr, T)
    nh = W // HEAD

    def body(cur_ref, prev_ref, w_ref, o_ref):
        i, j = pl.program_id(0), pl.program_id(1)
        prev = jnp.where(i > 0, prev_ref[...].astype(F32), 0.0)
        y = _conv_taps(cur_ref[...].astype(F32), prev, w_ref[...], tr)
        kind = jnp.where(j < n_qk_heads, 0, jnp.where(j < 2 * n_qk_heads, 1, 2))
        o_ref[...] = _conv_post(y, kind, HEAD)

    return pl.pallas_call(
        body, name=name, grid=(T // tr, nh),
        in_specs=[pl.BlockSpec((tr, HEAD), lambda i, j: (i, j)),
                  pl.BlockSpec((HALO, HEAD), lambda i, j: (jnp.maximum(i * (tr // HALO) - 1, 0), j)),
                  pl.BlockSpec((GDN_CONV, HEAD), lambda i, j: (0, j))],
        out_specs=pl.BlockSpec((tr, HEAD), lambda i, j: (i, j)),
        out_shape=jax.ShapeDtypeStruct((T, W), F32), compiler_params=_params(("parallel", "parallel")))(big, big, w)


def conv_bwd_taps(big, w, dcn, n_qk_heads, *, name, tr=2048):
    T, W = big.shape[0], w.shape[1]
    tr = min(tr, T)
    nh = W // HEAD

    def body(cur_ref, prev_ref, w_ref, g_ref, dy_ref, dw_ref):
        j, i = pl.program_id(0), pl.program_id(1)
        cur = cur_ref[...].astype(F32)
        prev = jnp.where(i > 0, prev_ref[...].astype(F32), 0.0)
        y = _conv_taps(cur, prev, w_ref[...], tr)
        kind = jnp.where(j < n_qk_heads, 0, jnp.where(j < 2 * n_qk_heads, 1, 2))
        _, vjp = jax.vjp(lambda t: _conv_post(t, kind, HEAD), y)
        dy, = vjp(g_ref[...])
        dy_ref[...] = dy
        ext = jnp.concatenate([prev, cur], axis=0)
        rows = [jnp.sum(dy * (cur if d == 0 else pltpu.roll(ext, d, 0)[HALO:HALO + tr]), axis=0, keepdims=True)
                for d in (3, 2, 1, 0)]

        @pl.when(i == 0)
        def _():
            dw_ref[...] = jnp.zeros_like(dw_ref)

        dw_ref[...] += jnp.concatenate(rows, axis=0)

    return pl.pallas_call(
        body, name=name, grid=(nh, T // tr),
        in_specs=[pl.BlockSpec((tr, HEAD), lambda j, i: (i, j)),
                  pl.BlockSpec((HALO, HEAD), lambda j, i: (jnp.maximum(i * (tr // HALO) - 1, 0), j)),
                  pl.BlockSpec((GDN_CONV, HEAD), lambda j, i: (0, j)),
                  pl.BlockSpec((tr, HEAD), lambda j, i: (i, j))],
        out_specs=[pl.BlockSpec((tr, HEAD), lambda j, i: (i, j)), pl.BlockSpec((GDN_CONV, HEAD), lambda j, i: (0, j))],
        out_shape=[jax.ShapeDtypeStruct((T, W), F32), jax.ShapeDtypeStruct((GDN_CONV, W), F32)],
        compiler_params=_params(("parallel", "arbitrary")))(big, big, w, dcn)


def conv_bwd_input(dy, w, *, name, tr=2048):
    T, W = dy.shape
    tr = min(tr, T)
    nrow = T // tr

    def body(cur_ref, nxt_ref, w_ref, o_ref):
        i = pl.program_id(0)
        cur = cur_ref[...]
        nxt = jnp.where(i < nrow - 1, nxt_ref[...], 0.0)
        ext = jnp.concatenate([cur, nxt], axis=0)
        w = w_ref[...]
        dx = w[3:4, :] * cur
        for d in (1, 2, 3):
            dx = dx + w[3 - d:4 - d, :] * pltpu.roll(ext, tr + HALO - d, 0)[0:tr]
        o_ref[...] = dx.astype(o_ref.dtype)

    return pl.pallas_call(
        body, name=name, grid=(nrow, W // HEAD),
        in_specs=[pl.BlockSpec((tr, HEAD), lambda i, j: (i, j)),
                  pl.BlockSpec((HALO, HEAD), lambda i, j: (jnp.minimum((i + 1) * (tr // HALO), T // HALO - 1), j)),
                  pl.BlockSpec((GDN_CONV, HEAD), lambda i, j: (0, j))],
        out_specs=pl.BlockSpec((tr, HEAD), lambda i, j: (i, j)),
        out_shape=jax.ShapeDtypeStruct((T, W), BF16), compiler_params=_params(("parallel", "parallel")))(dy, dy, w)


INV_BLOCK = 16


def _unit_lower_inverse_value(p):
    C = p.shape[-1]
    ii = lax.broadcasted_iota(jnp.int32, (C, C), 0)
    jj = lax.broadcasted_iota(jnp.int32, (C, C), 1)
    eye = jnp.where((ii == jj)[None], 1.0, 0.0)
    same = ((ii // INV_BLOCK) == (jj // INV_BLOCK))[None]
    pd = jnp.where(same, p, 0.0)
    d_inv = eye + pd
    n = 2
    while n < INV_BLOCK:
        pd = NNH(pd, pd)
        d_inv = d_inv + NNH(d_inv, pd)
        n *= 2
    nb = NNH(d_inv, jnp.where(same, 0.0, p))
    t = eye + nb
    n = 2
    while n < C // INV_BLOCK:
        nb = NNH(nb, nb)
        t = t + NNH(t, nb)
        n *= 2
    return NNH(t, d_inv)


@jax.custom_vjp
def _unit_lower_inverse(p, t_known):
    return _unit_lower_inverse_value(p) if t_known is None else t_known


def _unit_lower_inverse_fwd(p, t_known):
    t = _unit_lower_inverse(p, t_known)
    return t, (t, t_known is not None)


def _unit_lower_inverse_bwd(res, g):
    t, had = res
    return NTH(TNH(t, g), t), (jnp.zeros_like(t) if had else None)


_unit_lower_inverse.defvjp(_unit_lower_inverse_fwd, _unit_lower_inverse_bwd)


def _gdn_chunk(q, k, v, bg, S, t_known=None):
    H, C = q.shape[0], q.shape[1]
    ii = lax.broadcasted_iota(jnp.int32, (C, C), 0)
    jj = lax.broadcasted_iota(jnp.int32, (C, C), 1)
    lincl = (ii >= jj).astype(F32)
    strict, incl, eye = (ii > jj)[None], (ii >= jj)[None], (ii == jj)[None]
    gam2d = lax.dot_general(lincl, bg, (((1,), (0,)), ((), ())), precision=HI, preferred_element_type=F32)
    lane = lax.broadcasted_iota(jnp.int32, (H, 1, LANES), 2)
    hh = lax.broadcasted_iota(jnp.int32, (H, 1, LANES), 0)
    beta = jnp.sum(bg[None] * (lane == hh + LANE_B).astype(F32), axis=2, keepdims=True)
    gam = jnp.sum(gam2d[None] * (lane == hh + LANE_A).astype(F32), axis=2, keepdims=True)
    last = (lax.broadcasted_iota(jnp.int32, (1, C, 1), 1) == C - 1).astype(F32)
    gam_last = jnp.sum(gam * last, axis=1, keepdims=True)
    gam_row = NNX(jnp.ones((H, C, C), F32), jnp.where(eye, gam, 0.0))
    diff = gam - gam_row
    dec_s = jnp.where(strict, jnp.exp(jnp.where(strict, diff, 0.0)), 0.0)
    dec_i = jnp.where(incl, jnp.exp(jnp.where(incl, diff, 0.0)), 0.0)
    t = _unit_lower_inverse(-(beta * NT(k, k) * dec_s), t_known)
    eg = jnp.exp(gam)
    wu = NNH(t, jnp.concatenate([beta * eg * k, beta * v], axis=-1))
    w, u0 = wu[..., :HEAD], wu[..., HEAD:]
    qk = NT(q, k) * dec_i
    u = u0 - NN(w, S)
    o = NN(jnp.concatenate([q * eg, qk], axis=-1), jnp.concatenate([S, u], axis=-2))
    S2 = jnp.exp(gam_last) * S + TN(k * jnp.exp(gam_last - gam), u)
    return o, S2, t


def _heads(x, base, H):
    return jnp.stack([x[:, base + h * HEAD:base + (h + 1) * HEAD] for h in range(H)])


def _unheads(x):
    return jnp.concatenate([x[h] for h in range(x.shape[0])], axis=1)


GDN_CHUNKS_PER_STEP = 8


def gdn_fwd(cn, bgf, H, *, name):
    T, C, W = cn.shape[0], GDN_CHUNK, H * HEAD
    N = T // C
    K = GDN_CHUNKS_PER_STEP if N % GDN_CHUNKS_PER_STEP == 0 else 1

    def body(cn_ref, bg_ref, o_ref, ss_ref, t_ref, s_scr):
        @pl.when(pl.program_id(0) == 0)
        def _():
            s_scr[...] = jnp.zeros_like(s_scr)

        S = s_scr[...]
        for c in range(K):
            rows = slice(c * C, (c + 1) * C)
            x = cn_ref[rows, :]
            ss_ref[c] = S
            o, S, t = _gdn_chunk(_heads(x, 0, H), _heads(x, W, H), _heads(x, 2 * W, H), bg_ref[rows, :], S)
            o_ref[rows, :] = _unheads(o)
            t_ref[c] = t
        s_scr[...] = S

    return pl.pallas_call(
        body, name=name, grid=(N // K,),
        in_specs=[pl.BlockSpec((K * C, 3 * W), lambda n: (n, 0)), pl.BlockSpec((K * C, LANES), lambda n: (n, 0))],
        out_specs=[pl.BlockSpec((K * C, W), lambda n: (n, 0)), pl.BlockSpec((K, H, HEAD, HEAD), lambda n: (n, 0, 0, 0)),
                   pl.BlockSpec((K, H, C, C), lambda n: (n, 0, 0, 0))],
        out_shape=[jax.ShapeDtypeStruct((T, W), F32), jax.ShapeDtypeStruct((N, H, HEAD, HEAD), F32),
                   jax.ShapeDtypeStruct((N, H, C, C), F32)],
        scratch_shapes=[pltpu.VMEM((H, HEAD, HEAD), F32)], compiler_params=_params(("arbitrary",)))(cn, bgf)


def gdn_bwd(cn, bgf, ss, tinv, do, H, *, name):
    T, C, W = cn.shape[0], GDN_CHUNK, H * HEAD
    N = T // C
    K = GDN_CHUNKS_PER_STEP if N % GDN_CHUNKS_PER_STEP == 0 else 1
    NS = N // K

    def body(cn_ref, bg_ref, ss_ref, t_ref, do_ref, dcn_ref, dbg_ref, ds_scr):
        @pl.when(pl.program_id(0) == 0)
        def _():
            ds_scr[...] = jnp.zeros_like(ds_scr)

        dS = ds_scr[...]
        for c in reversed(range(K)):
            rows = slice(c * C, (c + 1) * C)
            x = cn_ref[rows, :]
            t_known = t_ref[c]
            _, vjp = jax.vjp(lambda *a: _gdn_chunk(*a, t_known)[:2],
                             _heads(x, 0, H), _heads(x, W, H), _heads(x, 2 * W, H), bg_ref[rows, :], ss_ref[c])
            dq, dk, dv, dbg, dS = vjp((_heads(do_ref[rows, :], 0, H), dS))
            dcn_ref[rows, :] = jnp.concatenate([_unheads(dq), _unheads(dk), _unheads(dv)], axis=1)
            dbg_ref[rows, :] = dbg
        ds_scr[...] = dS

    rev = lambda n: (NS - 1 - n, 0)
    rev4 = lambda n: (NS - 1 - n, 0, 0, 0)
    return pl.pallas_call(
        body, name=name, grid=(NS,),
        in_specs=[pl.BlockSpec((K * C, 3 * W), rev), pl.BlockSpec((K * C, LANES), rev),
                  pl.BlockSpec((K, H, HEAD, HEAD), rev4), pl.BlockSpec((K, H, C, C), rev4), pl.BlockSpec((K * C, W), rev)],
        out_specs=[pl.BlockSpec((K * C, 3 * W), rev), pl.BlockSpec((K * C, LANES), rev)],
        out_shape=[jax.ShapeDtypeStruct((T, 3 * W), F32), jax.ShapeDtypeStruct((T, LANES), F32)],
        scratch_shapes=[pltpu.VMEM((H, HEAD, HEAD), F32)], compiler_params=_params(("arbitrary",)))(cn, bgf, ss, tinv, do)


AUG = 2 * HEAD


def fox_prep(big, col_off, gain, cum, kind, H, *, name, tr=2048):
    T = big.shape[0]
    tr = min(tr, T)

    def body(x_ref, g_ref, c_ref, o_ref):
        h = pl.program_id(1)
        x = x_ref[...].astype(F32)
        lane = lax.broadcasted_iota(jnp.int32, (tr, HEAD), 1)
        if kind == 2:
            main, aug = x, jnp.ones((tr, HEAD), F32)
        else:
            main = _rms(x, g_ref[...]) * ((HEAD ** -0.5) if kind == 0 else 1.0)
            c = jnp.sum(jnp.where(lane == LANE_F + h, c_ref[...], 0.0), axis=1, keepdims=True)
            hi = c.astype(BF16).astype(F32)
            mid = (c - hi).astype(BF16).astype(F32)
            lo = c - hi - mid
            if kind == 0:
                aug = jnp.where(lane == 0, hi, jnp.where(lane == 1, mid, jnp.where(lane == 2, lo,
                                                                                   jnp.where(lane < 6, 1.0, 0.0))))
            else:
                aug = jnp.where(lane < 3, 1.0, jnp.where(lane == 3, -hi, jnp.where(lane == 4, -mid,
                                                                                   jnp.where(lane == 5, -lo, 0.0))))
        o_ref[...] = jnp.concatenate([main, aug], axis=1).astype(o_ref.dtype)

    return pl.pallas_call(
        body, name=name, grid=(T // tr, H),
        in_specs=[pl.BlockSpec((tr, HEAD), lambda i, h: (i, col_off // HEAD + h)),
                  pl.BlockSpec((1, HEAD), lambda i, h: (0, 0)), pl.BlockSpec((tr, LANES), lambda i, h: (i, 0))],
        out_specs=pl.BlockSpec((tr, AUG), lambda i, h: (i, h)),
        out_shape=jax.ShapeDtypeStruct((T, H * AUG), BF16), compiler_params=_params(("parallel", "parallel")))(big, gain, cum)


def _fox_logits(q_ref, k_ref, h, tq, tk, diagonal):
    ha = slice(h * AUG, (h + 1) * AUG)
    s = lax.dot_general(q_ref[:, ha], k_ref[:, ha], (((1,), (1,)), ((), ())), preferred_element_type=F32)
    keep = None
    if diagonal:
        keep = lax.broadcasted_iota(jnp.int32, (tq, tk), 0) >= lax.broadcasted_iota(jnp.int32, (tq, tk), 1)
    return s, keep


def _dispatch(live_ref, H, i, j, below, on_diagonal, head_fn, after=None, straight_from=None):
    straight_from = H if straight_from is None else straight_from
    def straight(diagonal):
        for h in range(H):
            head_fn(h, diagonal)
        if after is not None:
            after()

    def by_head():
        for h in range(H):
            pl.when(live_ref[h, i, j] != 0)(functools.partial(head_fn, h, False))
        if after is not None:
            after()

    n_live = live_ref[H, i, j]
    pl.when(jnp.logical_and(below, n_live >= straight_from))(functools.partial(straight, False))
    pl.when(jnp.logical_and(below, jnp.logical_and(n_live > 0, n_live < straight_from)))(by_head)
    pl.when(on_diagonal)(functools.partial(straight, True))


FOX_TILE = 512
EXP_UNDERFLOW = -100.0


def fox_live_tiles(cum, q_gain, k_gain, H, T):
    t = min(FOX_TILE, T)
    n = T // t
    c = cum[:, LANE_F:LANE_F + H]
    bias = c[0::t][:, None, :] - c[t - 1::t][None, :, :]
    bound = 1.02 * (HEAD ** 0.5) * jnp.max(jnp.abs(q_gain)) * jnp.max(jnp.abs(k_gain))
    causal = (jnp.arange(n)[:, None] >= jnp.arange(n)[None, :])[:, :, None]
    live = jnp.logical_and(causal, 2.0 * bound + bias >= EXP_UNDERFLOW)
    live = live.astype(jnp.int32)
    return jnp.concatenate([jnp.transpose(live, (2, 0, 1)), jnp.sum(live, axis=2)[None]], axis=0)


def fox_fwd(qa, ka, va, live, H, *, name):
    T = qa.shape[0]
    tq = tk = min(FOX_TILE, T)
    nq, nk = T // tq, T // tk

    def body(live_ref, q_ref, k_ref, v_ref, o_ref, lse_ref, acc, m_scr):
        i, j = pl.program_id(0), pl.program_id(1)

        @pl.when(j == 0)
        def _():
            m_scr[...] = jnp.full_like(m_scr, -jnp.inf)
            acc[...] = jnp.zeros_like(acc)

        def head(h, diagonal):
            ha = slice(h * AUG, (h + 1) * AUG)
            s, keep = _fox_logits(q_ref, k_ref, h, tq, tk, diagonal)
            if diagonal:
                s = jnp.where(keep, s, -jnp.inf)
            m_prev = m_scr[h]
            m_new = jnp.maximum(m_prev, jnp.max(s, axis=1, keepdims=True))
            p = jnp.exp(s - m_new[:, 0:1])
            p_hi = p.astype(BF16)
            p_lo = (p - p_hi.astype(F32)).astype(BF16)
            pv = lambda t: lax.dot_general(t, v_ref[:, ha], (((1,), (0,)), ((), ())), preferred_element_type=F32)
            acc[:, ha] = jnp.exp(m_prev[:, 0:1] - m_new[:, 0:1]) * acc[:, ha] + (pv(p_hi) + pv(p_lo))
            m_scr[h] = m_new

        _dispatch(live_ref, H, i, j, j < i, j == i, head, straight_from=(5 * H + 7) // 8)

        @pl.when(j == nk - 1)
        def _():
            lane = lax.broadcasted_iota(jnp.int32, (tq, LANES), 1)
            lse = jnp.zeros((tq, LANES), F32)
            for h in range(H):
                den = acc[:, h * AUG + HEAD:(h + 1) * AUG]
                o_ref[:, h * HEAD:(h + 1) * HEAD] = acc[:, h * AUG:h * AUG + HEAD] / den
                lse = jnp.where(lane == h, m_scr[h] + jnp.log(den), lse)
            lse_ref[...] = lse

    kv_idx = lambda i, j, f: (jnp.minimum(j, i), 0)
    q_idx = lambda i, j, f: (i, 0)
    return pl.pallas_call(
        body, name=name,
        grid_spec=pltpu.PrefetchScalarGridSpec(
            num_scalar_prefetch=1, grid=(nq, nk),
            in_specs=[pl.BlockSpec((tq, H * AUG), q_idx), pl.BlockSpec((tk, H * AUG), kv_idx), pl.BlockSpec((tk, H * AUG), kv_idx)],
            out_specs=[pl.BlockSpec((tq, H * HEAD), q_idx), pl.BlockSpec((tq, LANES), q_idx)],
            scratch_shapes=[pltpu.VMEM((tq, H * AUG), F32), pltpu.VMEM((H, tq, LANES), F32)]),
        out_shape=[jax.ShapeDtypeStruct((T, H * HEAD), F32), jax.ShapeDtypeStruct((T, LANES), F32)],
        compiler_params=_params(("parallel", "arbitrary")))(live, qa, ka, va)


def _fox_ds(q_ref, k_ref, v_ref, do_ref, o_ref, lse_ref, h, tq, tk, diagonal):
    hs = slice(h * HEAD, (h + 1) * HEAD)
    s, keep = _fox_logits(q_ref, k_ref, h, tq, tk, diagonal)
    p = jnp.exp(s - lse_ref[:, h:h + 1])
    if diagonal:
        p = jnp.where(keep, p, 0.0)
    do = do_ref[:, hs]
    dp = lax.dot_general(do, v_ref[:, h * AUG:h * AUG + HEAD], (((1,), (1,)), ((), ())), preferred_element_type=F32)
    delta = jnp.sum(do.astype(F32) * o_ref[:, hs], axis=1, keepdims=True)
    return p, p * (dp - delta)


def fox_bwd_kv(qa, ka, va, do, o, lse, live, H, *, name):
    T, W = do.shape
    tq = tk = min(FOX_TILE, T)
    nq, nk = T // tq, T // tk

    def body(live_ref, q_ref, k_ref, v_ref, do_ref, o_ref, lse_ref, dk_ref, dv_ref, dc_ref, dk_acc, dv_acc, dc_acc):
        j, i = pl.program_id(0), pl.program_id(1)

        @pl.when(i == 0)
        def _():
            dk_acc[...] = jnp.zeros_like(dk_acc)
            dv_acc[...] = jnp.zeros_like(dv_acc)
            dc_acc[...] = jnp.zeros_like(dc_acc)

        def head(h, diagonal):
            hs = slice(h * HEAD, (h + 1) * HEAD)
            p, ds = _fox_ds(q_ref, k_ref, v_ref, do_ref, o_ref, lse_ref, h, tq, tk, diagonal)
            dv_acc[:, hs] += lax.dot_general(p.astype(BF16), do_ref[:, hs], (((0,), (0,)), ((), ())),
                                             preferred_element_type=F32)
            dk_acc[:, hs] += lax.dot_general(ds.astype(BF16), q_ref[:, h * AUG:h * AUG + HEAD],
                                             (((0,), (0,)), ((), ())), preferred_element_type=F32)
            dc_acc[h] -= jnp.broadcast_to(jnp.sum(ds, axis=0, keepdims=True), (8, tk))

        _dispatch(live_ref, H, i, j, i > j, i == j, head)

        @pl.when(i == nq - 1)
        def _():
            dk_ref[...] = dk_acc[...].astype(dk_ref.dtype)
            dv_ref[...] = dv_acc[...].astype(dv_ref.dtype)
            row = lax.broadcasted_iota(jnp.int32, (8, tk), 0)
            dc = jnp.zeros((8, tk), F32)
            for h in range(H):
                dc = jnp.where(row == h, dc_acc[h], dc)
            dc_ref[...] = dc

    q_idx = lambda j, i, f: (jnp.maximum(i, j), 0)
    kv_idx = lambda j, i, f: (j, 0)
    return pl.pallas_call(
        body, name=name,
        grid_spec=pltpu.PrefetchScalarGridSpec(
            num_scalar_prefetch=1, grid=(nk, nq),
            in_specs=[pl.BlockSpec((tq, H * AUG), q_idx), pl.BlockSpec((tk, H * AUG), kv_idx), pl.BlockSpec((tk, H * AUG), kv_idx),
                      pl.BlockSpec((tq, W), q_idx), pl.BlockSpec((tq, W), q_idx), pl.BlockSpec((tq, LANES), q_idx)],
            out_specs=[pl.BlockSpec((tk, W), kv_idx), pl.BlockSpec((tk, W), kv_idx),
                       pl.BlockSpec((8, tk), lambda j, i, f: (0, j))],
            scratch_shapes=[pltpu.VMEM((tk, W), F32), pltpu.VMEM((tk, W), F32), pltpu.VMEM((H, 8, tk), F32)]),
        out_shape=[jax.ShapeDtypeStruct((T, W), BF16), jax.ShapeDtypeStruct((T, W), BF16), jax.ShapeDtypeStruct((8, T), F32)],
        compiler_params=_params(("parallel", "arbitrary")))(live, qa, ka, va, do, o, lse)


def fox_bwd_q(qa, ka, va, do, o, lse, live, H, *, name):
    T, W = do.shape
    tq = tk = min(FOX_TILE, T)
    nq, nk = T // tq, T // tk

    def body(live_ref, q_ref, k_ref, v_ref, do_ref, o_ref, lse_ref, dq_ref, dq_acc):
        i, j = pl.program_id(0), pl.program_id(1)

        @pl.when(j == 0)
        def _():
            dq_acc[...] = jnp.zeros_like(dq_acc)

        def head(h, diagonal):
            hs = slice(h * HEAD, (h + 1) * HEAD)
            _, ds = _fox_ds(q_ref, k_ref, v_ref, do_ref, o_ref, lse_ref, h, tq, tk, diagonal)
            dq_acc[:, hs] += lax.dot_general(ds.astype(BF16), k_ref[:, h * AUG:h * AUG + HEAD],
                                             (((1,), (0,)), ((), ())), preferred_element_type=F32)

        _dispatch(live_ref, H, i, j, j < i, j == i, head)

        @pl.when(j == nk - 1)
        def _():
            dq_ref[...] = (dq_acc[...] * (HEAD ** -0.5)).astype(dq_ref.dtype)

    q_idx = lambda i, j, f: (i, 0)
    kv_idx = lambda i, j, f: (jnp.minimum(j, i), 0)
    return pl.pallas_call(
        body, name=name,
        grid_spec=pltpu.PrefetchScalarGridSpec(
            num_scalar_prefetch=1, grid=(nq, nk),
            in_specs=[pl.BlockSpec((tq, H * AUG), q_idx), pl.BlockSpec((tk, H * AUG), kv_idx), pl.BlockSpec((tk, H * AUG), kv_idx),
                      pl.BlockSpec((tq, W), q_idx), pl.BlockSpec((tq, W), q_idx), pl.BlockSpec((tq, LANES), q_idx)],
            out_specs=pl.BlockSpec((tq, W), q_idx),
            scratch_shapes=[pltpu.VMEM((tq, W), F32)]),
        out_shape=jax.ShapeDtypeStruct((T, W), BF16),
        compiler_params=_params(("parallel", "arbitrary")))(live, qa, ka, va, do, o, lse)


def _gates_fn(small, a_log_l, dt_bias_l, b_f_l):
    lane = lax.broadcasted_iota(jnp.int32, small.shape, 1)
    beta = _sigmoid(small)
    g = -jnp.exp(a_log_l) * _softplus(small + dt_bias_l)
    lf = _log_sigmoid(small + b_f_l)
    return (jnp.where(lane < LANE_A, beta, jnp.where(lane < LANE_F, g, jnp.where(lane < LANE_F + 8, lf, 0.0))),)


def _gated_norm_fn(o, z, g):
    return (_rms(o, g) * _silu(z),)


def _merge_fn(ya, yb, ym, ga, gb, gm):
    return (_sigmoid(ga) * ya + _sigmoid(gb) * yb + _sigmoid(gm) * ym,)


def _mem_attn_fn(nh, dh, mq, kn, v, gq):
    outs = []
    for h in range(nh):
        hs = slice(h * dh, (h + 1) * dh)
        qn = _rms(mq[:, hs], gq)
        s = NT(qn, kn[:, hs]) * (dh ** -0.5)
        e = jnp.exp(s - jnp.max(s, axis=1, keepdims=True))
        p = e / jnp.sum(e, axis=1, keepdims=True)
        outs.append(NN(p, v[:, hs]))
    return (jnp.concatenate(outs, axis=1),)


def sum_squares(x, *, name, tr=512):
    T, D = x.shape
    tr = min(tr, T)

    def body(x_ref, o_ref):
        @pl.when(pl.program_id(0) == 0)
        def _():
            o_ref[...] = jnp.zeros_like(o_ref)

        v = x_ref[...]
        o_ref[...] += jnp.sum(jnp.sum(v * v, axis=1, keepdims=True), axis=0, keepdims=True)

    return pl.pallas_call(
        body, name=name, grid=(T // tr,), in_specs=[pl.BlockSpec((tr, D), lambda i: (i, 0))],
        out_specs=pl.BlockSpec((1, LANES), lambda i: (0, 0)), out_shape=jax.ShapeDtypeStruct((1, LANES), F32),
        compiler_params=_params(("arbitrary",)))(x)


class Comm:
    def __init__(self, arrays, out_shape, sems, start, forward, finish):
        self.arrays, self.out_shape, self.sems = list(arrays), list(out_shape), list(sems)
        self.start, self.forward, self.finish = start, forward, finish


def _place():
    x, y, c = lax.axis_index("x"), lax.axis_index("y"), lax.axis_index("c")
    chips = [(1 - x, y), (x, 1 - y), (1 - x, 1 - y)]
    return x, y, c, chips


def comm_gather(arrays):
    n = len(arrays)
    lin = lambda px, py, pc: 4 * px + 2 * py + pc

    def copy(ins, outs, sems, a, k, block, to, src=None):
        slot = outs[a].at[lin(*block)]
        return pltpu.make_async_remote_copy(src_ref=slot if src is None else src, dst_ref=slot, send_sem=sems[0].at[a, k],
                                            recv_sem=sems[1].at[a, k], device_id=to, device_id_type=MESH)

    def local(ins, outs, sems, a):
        x, y, c, _ = _place()
        return pltpu.make_async_copy(ins[a], outs[a].at[lin(x, y, c)], sems[2].at[a])

    def start(ins, outs, sems):
        x, y, c, chips = _place()
        for a in range(n):
            local(ins, outs, sems, a).start()
        for a in range(n):
            for j, chip in enumerate(chips):
                copy(ins, outs, sems, a, 1 + j, (x, y, c), (*chip, c), src=ins[a]).start()
            copy(ins, outs, sems, a, 0, (x, y, c), (x, y, 1 - c), src=ins[a]).start()

    def forward(ins, outs, sems):
        x, y, c, chips = _place()
        for a in range(n):
            for j, chip in enumerate(chips):
                copy(ins, outs, sems, a, 1 + j, (*chip, c), (x, y, c)).wait_recv()
                copy(ins, outs, sems, a, 4 + j, (*chip, c), (x, y, 1 - c)).start()

    def finish(ins, outs, sems):
        x, y, c, chips = _place()
        for a in range(n):
            copy(ins, outs, sems, a, 0, (x, y, 1 - c), (x, y, c)).wait_recv()
            for j, chip in enumerate(chips):
                copy(ins, outs, sems, a, 4 + j, (*chip, 1 - c), (x, y, c)).wait_recv()
        for a in range(n):
            for j, chip in enumerate(chips):
                copy(ins, outs, sems, a, 1 + j, (x, y, c), (*chip, c), src=ins[a]).wait_send()
                copy(ins, outs, sems, a, 4 + j, (*chip, c), (x, y, 1 - c)).wait_send()
            copy(ins, outs, sems, a, 0, (x, y, c), (x, y, 1 - c), src=ins[a]).wait_send()
            local(ins, outs, sems, a).wait()

    return Comm(arrays, [jax.ShapeDtypeStruct((N_DEV,) + a.shape, a.dtype) for a in arrays],
                [pltpu.SemaphoreType.DMA((n, 7)), pltpu.SemaphoreType.DMA((n, 7)), pltpu.SemaphoreType.DMA((n,))],
                start, forward, finish)


def comm_direct(arrays, scatter):
    n = len(arrays)

    def peers():
        x, y, c = lax.axis_index("x"), lax.axis_index("y"), lax.axis_index("c")
        out = []
        for r in range(1, N_DEV):
            px, py, pc = (1 - x if r & 4 else x), (1 - y if r & 2 else y), (1 - c if r & 1 else c)
            out.append((r, (px, py, pc), 4 * px + 2 * py + pc))
        return 4 * x + 2 * y + c, out

    def remote(ins, outs, sems, a, r, dev, src_slot, dst_slot):
        return pltpu.make_async_remote_copy(
            src_ref=ins[a].at[src_slot] if scatter[a] else ins[a], dst_ref=outs[a].at[dst_slot],
            send_sem=sems[0].at[a, r - 1], recv_sem=sems[1].at[a, r - 1], device_id=dev, device_id_type=MESH)

    def local(ins, outs, sems, a, me):
        return pltpu.make_async_copy(ins[a].at[me] if scatter[a] else ins[a], outs[a].at[me], sems[2].at[a])

    def start(ins, outs, sems):
        me, ps = peers()
        for a in range(n):
            local(ins, outs, sems, a, me).start()
        for a in range(n):
            for r, dev, lin in ps:
                remote(ins, outs, sems, a, r, dev, lin, me).start()

    def finish(ins, outs, sems):
        me, ps = peers()
        for a in range(n):
            for r, dev, lin in ps:
                remote(ins, outs, sems, a, r, dev, lin, lin).wait_recv()
        for a in range(n):
            for r, dev, lin in ps:
                remote(ins, outs, sems, a, r, dev, lin, me).wait_send()
            local(ins, outs, sems, a, me).wait()

    return Comm(arrays, [jax.ShapeDtypeStruct(a.shape if sc else (N_DEV,) + a.shape, a.dtype) for a, sc in zip(arrays, scatter)],
                [pltpu.SemaphoreType.DMA((n, N_DEV - 1)), pltpu.SemaphoreType.DMA((n, N_DEV - 1)),
                 pltpu.SemaphoreType.DMA((n,))], start, None, finish)


def comm_to_sibling(parts):
    n = len(parts)

    def copy(ins, outs, sems, a, k):
        x, y, c, _ = _place()
        return pltpu.make_async_remote_copy(src_ref=ins[a].at[2 * k + (1 - c)], dst_ref=outs[a].at[k], send_sem=sems[0].at[a, k],
                                            recv_sem=sems[1].at[a, k], device_id=(x, y, 1 - c), device_id_type=MESH)

    def start(ins, outs, sems):
        for a in range(n):
            for k in range(4):
                copy(ins, outs, sems, a, k).start()

    def finish(ins, outs, sems):
        for a in range(n):
            for k in range(4):
                copy(ins, outs, sems, a, k).wait()

    return Comm(parts, [jax.ShapeDtypeStruct((4,) + p.shape[1:], p.dtype) for p in parts],
                [pltpu.SemaphoreType.DMA((n, 4)), pltpu.SemaphoreType.DMA((n, 4))], start, None, finish)


def comm_to_owner_chip(sums):
    n = len(sums)

    def remote(ins, outs, sems, a, j, src_k, dst_k, chip):
        _, _, c, _ = _place()
        return pltpu.make_async_remote_copy(src_ref=ins[a].at[src_k], dst_ref=outs[a].at[dst_k], send_sem=sems[0].at[a, j],
                                            recv_sem=sems[1].at[a, j], device_id=(*chip, c), device_id_type=MESH)

    def local(ins, outs, sems, a):
        x, y, _, _ = _place()
        return pltpu.make_async_copy(ins[a].at[2 * x + y], outs[a].at[2 * x + y], sems[2].at[a])

    def start(ins, outs, sems):
        x, y, _, chips = _place()
        for a in range(n):
            local(ins, outs, sems, a).start()
            for j, (px, py) in enumerate(chips):
                remote(ins, outs, sems, a, j, 2 * px + py, 2 * x + y, (px, py)).start()

    def finish(ins, outs, sems):
        x, y, _, chips = _place()
        for a in range(n):
            for j, (px, py) in enumerate(chips):
                remote(ins, outs, sems, a, j, 2 * x + y, 2 * px + py, (px, py)).wait_recv()
        for a in range(n):
            for j, (px, py) in enumerate(chips):
                remote(ins, outs, sems, a, j, 2 * px + py, 2 * x + y, (px, py)).wait_send()
            local(ins, outs, sems, a).wait()

    return Comm(sums, [jax.ShapeDtypeStruct(s.shape, s.dtype) for s in sums],
                [pltpu.SemaphoreType.DMA((n, 3)), pltpu.SemaphoreType.DMA((n, 3)), pltpu.SemaphoreType.DMA((n,))],
                start, None, finish)


def pair_sum(parts, sib, *, name, tr=512):
    _, R, Cc = parts.shape
    tr = min(tr, R)
    assert R % tr == 0
    core = lax.axis_index("c").astype(jnp.int32).reshape(1)

    def body(c_ref, p_ref, s_ref, o_ref):
        o_ref[0] = (p_ref[0, 0].astype(F32) + s_ref[0].astype(F32)).astype(o_ref.dtype)

    return pl.pallas_call(
        body, name=name,
        grid_spec=pltpu.PrefetchScalarGridSpec(
            num_scalar_prefetch=1, grid=(4, R // tr),
            in_specs=[pl.BlockSpec((1, 1, tr, Cc), lambda k, i, c: (k, c[0], i, 0)),
                      pl.BlockSpec((1, tr, Cc), lambda k, i, c: (k, i, 0))],
            out_specs=pl.BlockSpec((1, tr, Cc), lambda k, i, c: (k, i, 0))),
        out_shape=jax.ShapeDtypeStruct((4, R, Cc), BF16),
        compiler_params=_params(("parallel", "parallel")))(core, parts.reshape(4, 2, R, Cc), sib)


def run_comm(comm, *, name):
    n = len(comm.arrays)

    def body(*refs):
        ins, outs, sems = refs[:n], refs[n:2 * n], refs[2 * n:]
        comm.start(ins, outs, sems)
        if comm.forward is not None:
            comm.forward(ins, outs, sems)
        comm.finish(ins, outs, sems)

    any_spec = pl.BlockSpec(memory_space=pl.ANY)
    return pl.pallas_call(body, name=name, in_specs=[any_spec] * n, out_specs=[any_spec] * n, out_shape=comm.out_shape,
                          scratch_shapes=comm.sems)(*comm.arrays)


def adamw(parts, w, m, v, *, name, tr=128):
    R, Cc = w.shape
    tr = min(tr, R)
    assert R % tr == 0
    n_parts = parts.shape[0]

    def body(p_ref, w_ref, m_ref, v_ref, g_ref, d_ref, nm_ref, nv_ref):
        g = p_ref[0].astype(F32)
        for s in range(1, n_parts):
            g = g + p_ref[s].astype(F32)
        nm = ADAM_B1 * m_ref[...] + (1.0 - ADAM_B1) * g
        nv = ADAM_B2 * v_ref[...] + (1.0 - ADAM_B2) * (g * g)
        m_hat = nm / (1.0 - ADAM_B1 ** ADAM_STEP)
        v_hat = nv / (1.0 - ADAM_B2 ** ADAM_STEP)
        g_ref[...] = g
        d_ref[...] = -ADAM_LR * (m_hat / (jnp.sqrt(v_hat) + ADAM_EPS) + ADAM_WD * w_ref[...])
        nm_ref[...] = nm
        nv_ref[...] = nv

    spec = pl.BlockSpec((tr, Cc), lambda i: (i, 0))
    return pl.pallas_call(
        body, name=name, grid=(R // tr,),
        in_specs=[pl.BlockSpec((n_parts, tr, Cc), lambda i: (0, i, 0)), spec, spec, spec], out_specs=[spec] * 4,
        out_shape=[jax.ShapeDtypeStruct((R, Cc), F32)] * 4, compiler_params=_params(("parallel",)))(parts, w, m, v)


def _lanes(vec, base):
    return jnp.pad(vec[None].astype(F32), ((0, 0), (base, LANES - base - vec.shape[0])))


def _col_shards(full_w):
    R, Ct = full_w.shape
    return jnp.transpose(full_w.reshape(R, N_DEV, Ct // N_DEV), (1, 0, 2))


def _from_col_shards(g):
    return jnp.transpose(g, (1, 0, 2)).reshape(g.shape[1], -1)


def kernel(x, mem, g_mix, w_in, conv_w, a_log, dt_bias, gdn_norm_g, fox_b_f, fox_q_norm, fox_k_norm, g_mem, w_mem_kv, mem_q_norm, mem_k_norm, w_up_gdn, w_up_fox, w_up_mem, w_out, g_mlp, w_ff1, w_ff2, loss_target, m_g_mix, m_w_in, m_conv_w, m_a_log, m_dt_bias, m_gdn_norm_g, m_fox_b_f, m_fox_q_norm, m_fox_k_norm, m_g_mem, m_w_mem_kv, m_mem_q_norm, m_mem_k_norm, m_w_up_gdn, m_w_up_fox, m_w_up_mem, m_w_out, m_g_mlp, m_w_ff1, m_w_ff2, v_g_mix, v_w_in, v_conv_w, v_a_log, v_dt_bias, v_gdn_norm_g, v_fox_b_f, v_fox_q_norm, v_fox_k_norm, v_g_mem, v_w_mem_kv, v_mem_q_norm, v_mem_k_norm, v_w_up_gdn, v_w_up_fox, v_w_up_mem, v_w_out, v_g_mlp, v_w_ff1, v_w_ff2):
    loc = dict(locals())
    big_names = ["w_in", "conv_w", "w_mem_kv", "w_up_gdn", "w_up_fox", "w_up_mem", "w_out", "w_ff1", "w_ff2"]
    col_sharded = {"w_in", "conv_w", "w_up_gdn", "w_up_fox", "w_up_mem", "w_ff1"}
    small_names = ["g_mix", "a_log", "dt_bias", "gdn_norm_g", "fox_b_f", "fox_q_norm", "fox_k_norm", "g_mem",
                   "mem_q_norm", "mem_k_norm", "g_mlp"]

    xs, tgt, mems = x[0], loss_target[0], mem[0]
    T, D = xs.shape
    HG = a_log.shape[1]
    HF = fox_b_f.shape[1]
    DM = mem_q_norm.shape[1]
    GQK, GV = HG * HEAD, HG * HEAD
    GQKV = 2 * GQK + GV
    FW = HF * HEAD
    MW = w_mem_kv.shape[2] // 2
    HM = MW // DM
    assert HG <= 8 and HF <= 8

    shard = {n: loc[n][0].astype(BF16) for n in big_names}
    first, rest, last = big_names[:2], big_names[2:-1], big_names[-1:]
    W = {}

    def take(names, gathered):
        for n, g in zip(names, gathered):
            if n == "w_ff1":
                W[n] = g
            else:
                W[n] = _from_col_shards(g) if n in col_sharded else g.reshape(-1, g.shape[2])

    rms_fn = lambda t, g: (_rms(t, g),)
    h, *gathered = rowwise(rms_fn, [full(xs)], [g_mix], [(D, D, BF16)], name="rms_mix", comm=comm_gather([shard[n] for n in first]))
    take(first, gathered)
    widths = [GQKV, GV, HG, HG, FW, FW, FW, HF, MW, 3 * D]
    offs = np.concatenate([[0], np.cumsum(widths)]).tolist()
    seg = [W["w_in"][:, offs[i]:offs[i + 1]] for i in range(len(widths))]
    w_big = jnp.concatenate([seg[0], seg[1], seg[4], seg[5], seg[6], seg[8], seg[9]], axis=1)
    pad8 = lambda s: jnp.pad(s, ((0, 0), (0, 8 - s.shape[1])))
    w_small = jnp.concatenate([pad8(seg[2]), pad8(seg[3]), pad8(seg[7]), jnp.zeros((D, LANES - 24), BF16)], axis=1)
    o_z, o_fq, o_fk, o_fv = GQKV, GQKV + GV, GQKV + GV + FW, GQKV + GV + 2 * FW
    o_mq = o_fv + FW
    o_gt = o_mq + MW
    WB = o_gt + 3 * D
    conv_full = W["conv_w"].astype(F32)

    a_log_l, dt_bias_l, b_f_l = _lanes(a_log[0], LANE_A), _lanes(dt_bias[0], LANE_A), _lanes(fox_b_f[0], LANE_F)

    big, *gathered = matmul(h, w_big, mode="nn", name="proj_big", comm=comm_gather([shard[n] for n in rest]))
    take(rest, gathered)
    small = matmul(h, w_small, mode="nn", name="proj_small", out_dtypes=(F32,))
    bgf, = rowwise(_gates_fn, [full(small)], [a_log_l, dt_bias_l, b_f_l], [(LANES, LANES, F32)], name="gates")

    cn = conv_fwd(big, conv_full, HG, name="conv")
    o_gdn, ss, tinv = gdn_fwd(cn, bgf, HG, name="gdn_fwd")
    oa, = rowwise(_gated_norm_fn, [(o_gdn, HEAD, 0, 1), (big, HEAD, o_z // HEAD, 1)], [gdn_norm_g],
                  [(GV, HEAD, BF16)], name="gated_norm", ncb=HG)

    cum = cumsum_tokens(bgf, reverse=False, name="cumsum")
    fqa = fox_prep(big, o_fq, fox_q_norm, cum, 0, HF, name="fox_prep_q")
    fka = fox_prep(big, o_fk, fox_k_norm, cum, 1, HF, name="fox_prep_k")
    fva = fox_prep(big, o_fv, fox_k_norm, cum, 2, HF, name="fox_prep_v")
    live = fox_live_tiles(cum, fox_q_norm, fox_k_norm, HF, T)
    ob, lse = fox_fwd(fqa, fka, fva, live, HF, name="fox_fwd")

    memn, = rowwise(rms_fn, [full(mems)], [g_mem], [(D, D, BF16)], name="rms_mem")
    kv_m = matmul(memn, W["w_mem_kv"], mode="nn", name="mem_kv", out_dtypes=(F32,))
    kmn, = rowwise(rms_fn, [(kv_m, DM, 0, 1)], [mem_k_norm], [(MW, DM, F32)], name="mem_knorm", ncb=HM)
    vm = kv_m[:, MW:]
    mem_fn = functools.partial(_mem_attn_fn, HM, DM)
    om, = rowwise(mem_fn, [(big, MW, o_mq // MW, 0)], [kmn, vm, mem_q_norm], [(MW, MW, BF16)], name="mem_attn")

    ya = matmul(oa, W["w_up_gdn"], mode="nn", name="up_gdn")
    yb = matmul(ob, W["w_up_fox"], mode="nn", name="up_fox")
    ym = matmul(om, W["w_up_mem"], mode="nn", name="up_mem")
    cbm = min(512, D)
    gate_rows = [(big, cbm, (o_gt + b * D) // cbm, 1) for b in range(3)]
    merge_rows = [(ya, cbm, 0, 1), (yb, cbm, 0, 1), (ym, cbm, 0, 1)] + gate_rows
    y, = rowwise(_merge_fn, merge_rows, [], [(D, cbm, BF16)], name="merge", ncb=D // cbm)
    x1 = matmul(y, W["w_out"], mode="nn", name="out_proj", out_dtypes=(F32,), extras=(xs,),
                epi=lambda r, res: (r + res,))

    h2, = rowwise(rms_fn, [full(x1)], [g_mlp], [(D, D, BF16)], name="rms_mlp")
    u_ff, a_ff, *gathered = matmul(h2, W["w_ff1"], mode="nn", name="ff1", out_dtypes=(BF16, BF16), b_shards=True,
                                   epi=lambda r: (r, jnp.square(jnp.maximum(r, 0.0))),
                                   comm=comm_gather([shard[n] for n in last]))
    take(last, gathered)
    d_out = matmul(a_ff, W["w_ff2"], mode="nn", name="ff2_loss", out_dtypes=(F32,), extras=(x1, tgt),
                   epi=lambda r, res, t: ((r + res - t) * (1.0 / D),))
    loss_local = 0.5 * D * sum_squares(d_out, name="loss_sum")[0, 0]
    loss = lax.psum(loss_local, ("x", "y", "c"))

    G = {}
    d_u = matmul(d_out, W["w_ff2"], mode="nt", name="d_ff2_in", extras=(u_ff,),
                 epi=lambda r, u: (r * 2.0 * jnp.maximum(u.astype(F32), 0.0),))
    d_h2 = matmul(d_u, W["w_ff1"], mode="nt", name="d_ff1_in", b_shards=True)
    d_x1, G["g_mlp"] = rowwise_bwd(rms_fn, [full(x1)], [g_mlp], [full(d_h2)], [(0, F32)], [0], name="d_rms_mlp",
                                   adds=[full(d_out)])
    d_y = matmul(d_x1, W["w_out"], mode="nt", name="d_out_proj_in")
    G["w_out"] = matmul(y, d_x1, mode="tn", name="d_w_out")
    d_ya, d_yb, d_ym, d_ga, d_gb, d_gm = rowwise_bwd(
        _merge_fn, merge_rows, [], [(d_y, cbm, 0, 1)], [(k, BF16) for k in range(6)], [], name="d_merge", ncb=D // cbm)
    d_oa = matmul(d_ya, W["w_up_gdn"], mode="nt", name="d_up_gdn_in")
    d_ob = matmul(d_yb, W["w_up_fox"], mode="nt", name="d_up_fox_in")
    d_om = matmul(d_ym, W["w_up_mem"], mode="nt", name="d_up_mem_in")
    G["w_up_gdn"] = matmul(oa, d_ya, mode="tn", name="d_w_up_gdn")
    G["w_up_fox"] = matmul(ob, d_yb, mode="tn", name="d_w_up_fox")
    G["w_up_mem"] = matmul(om, d_ym, mode="tn", name="d_w_up_mem")

    d_mq, d_kmn, d_vm, G["mem_q_norm"] = rowwise_bwd(
        mem_fn, [(big, MW, o_mq // MW, 0)], [kmn, vm, mem_q_norm], [full(d_om)], [(0, BF16)], [0, 1, 2], name="d_mem_attn")
    d_km, G["mem_k_norm"] = rowwise_bwd(rms_fn, [(kv_m, DM, 0, 1)], [mem_k_norm], [(d_kmn, DM, 0, 1)], [(0, F32)], [0],
                                         name="d_mem_knorm", ncb=HM)
    d_kv_m = jnp.concatenate([d_km, d_vm], axis=1)
    G["w_mem_kv"] = matmul(memn, d_kv_m, mode="tn", name="d_w_mem_kv")
    d_memn = matmul(d_kv_m, W["w_mem_kv"], mode="nt", name="d_mem_kv_in")
    _, G["g_mem"] = rowwise_bwd(rms_fn, [full(mems)], [g_mem], [full(d_memn)], [(0, BF16)], [0], name="d_rms_mem")

    d_fkn, d_fv, d_cum_t = fox_bwd_kv(fqa, fka, fva, d_ob, ob, lse, live, HF, name="fox_bwd_kv")
    d_fqn = fox_bwd_q(fqa, fka, fva, d_ob, ob, lse, live, HF, name="fox_bwd_q")
    d_fq, G["fox_q_norm"] = rowwise_bwd(rms_fn, [(big, HEAD, o_fq // HEAD, 1)], [fox_q_norm], [(d_fqn, HEAD, 0, 1)],
                                         [(0, BF16)], [0], name="d_fox_qnorm", ncb=HF)
    d_fk, G["fox_k_norm"] = rowwise_bwd(rms_fn, [(big, HEAD, o_fk // HEAD, 1)], [fox_k_norm], [(d_fkn, HEAD, 0, 1)],
                                         [(0, BF16)], [0], name="d_fox_knorm", ncb=HF)
    d_cum = jnp.pad(d_cum_t[:HF].T, ((0, 0), (LANE_F, LANES - LANE_F - HF)))
    d_logf = cumsum_tokens(d_cum, reverse=True, name="cumsum_rev")

    d_o_gdn, d_z, G["gdn_norm_g"] = rowwise_bwd(
        _gated_norm_fn, [(o_gdn, HEAD, 0, 1), (big, HEAD, o_z // HEAD, 1)], [gdn_norm_g], [(d_oa, HEAD, 0, 1)],
        [(0, F32), (1, BF16)], [0], name="d_gated_norm", ncb=HG)
    d_cn, d_bg = gdn_bwd(cn, bgf, ss, tinv, d_o_gdn, HG, name="gdn_bwd")
    d_conv_y, G["conv_w"] = conv_bwd_taps(big, conv_full, d_cn, HG, name="d_conv_taps")
    d_qkv = conv_bwd_input(d_conv_y, conv_full, name="d_conv_in")
    d_small, d_al, d_dt, d_bf = rowwise_bwd(_gates_fn, [full(small)], [a_log_l, dt_bias_l, b_f_l], [full(d_bg + d_logf)],
                                            [(0, F32)], [0, 1, 2], name="d_gates")
    G["a_log"], G["dt_bias"], G["fox_b_f"] = (d_al[:, LANE_A:LANE_A + HG], d_dt[:, LANE_A:LANE_A + HG],
                                               d_bf[:, LANE_F:LANE_F + HF])

    def parts(n):
        g = G[n].astype(BF16)
        if g.ndim == 3:
            return g
        return _col_shards(g) if n in col_sharded else g.reshape(N_DEV, -1, g.shape[1])

    recv = {}

    def carried(names, out):
        for n, r in zip(names, out):
            recv[n] = r

    d_big = jnp.concatenate([d_qkv, d_z, d_fq, d_fk, d_fv, d_mq, d_ga, d_gb, d_gm], axis=1)
    group = ["conv_w", "w_mem_kv", "w_up_gdn", "w_up_fox", "w_up_mem", "w_out"]
    G["w_ff2"], *out = matmul(a_ff, d_out, mode="tn", name="d_w_ff2",
                              comm=comm_direct([parts(n) for n in group], [True] * len(group)))
    carried(group, out)
    G["w_ff1"], *out = matmul(h2, d_u, mode="tn", name="d_w_ff1", out_shards=True,
                              comm=comm_direct([parts("w_ff2")], [True]))
    carried(["w_ff2"], out)
    g_big, *out = matmul(h, d_big, mode="tn", name="d_w_big", comm=comm_direct([parts("w_ff1")], [True]))
    carried(["w_ff1"], out)
    g_small = matmul(h, d_small, mode="tn", name="d_w_small", out_dtypes=(F32,))
    cols = lambda a, o, wd: a[:, o:o + wd]
    G["w_in"] = jnp.concatenate([
        cols(g_big, 0, GQKV), cols(g_big, o_z, GV), cols(g_small, LANE_B, HG), cols(g_small, LANE_A, HG),
        cols(g_big, o_fq, FW), cols(g_big, o_fk, FW), cols(g_big, o_fv, FW), cols(g_small, LANE_F, HF),
        cols(g_big, o_mq, MW), cols(g_big, o_gt, 3 * D)], axis=1)
    p_in = parts("w_in")
    d_h_s, from_sibling = matmul(d_small, w_small, mode="nt", name="d_proj_small_in", out_dtypes=(F32,),
                                 comm=comm_to_sibling([p_in]))
    chip_sums = pair_sum(p_in, from_sibling, name="w_in_pair_sum")
    d_h, *out = matmul(d_big, w_big, mode="nt", name="d_proj_big_in", extras=(d_h_s,), epi=lambda r, e: (r + e,),
                       comm=comm_to_owner_chip([chip_sums]))
    carried(["w_in"], out)
    grad_x, G["g_mix"] = rowwise_bwd(rms_fn, [full(xs)], [g_mix], [full(d_h)], [(0, F32)], [0], name="d_rms_mix",
                                     adds=[full(d_x1)])
    grad_x = grad_x[None]

    small_sizes = [loc[n].shape[1] for n in small_names]
    pack = lambda d: jnp.concatenate([d[n].reshape(1, -1) for n in small_names], axis=1)
    npad = -sum(small_sizes) % LANES
    padp = lambda a: jnp.pad(a, ((0, 0), (0, npad)))
    recv_small, = run_comm(comm_direct([padp(pack(G))], [False]), name="gather_small_grads")

    res = {}
    for n in big_names:
        res[n] = [t[None] for t in adamw(recv[n], loc[n][0], loc["m_" + n][0], loc["v_" + n][0], name="adamw_" + n)]
    sm = adamw(recv_small, padp(pack({n: loc[n] for n in small_names})), padp(pack({n: loc["m_" + n] for n in small_names})),
               padp(pack({n: loc["v_" + n] for n in small_names})), name="adamw_small")
    so = np.concatenate([[0], np.cumsum(small_sizes)]).tolist()
    for i, n in enumerate(small_names):
        res[n] = [t[:, so[i]:so[i + 1]] for t in sm]

    order = ["g_mix", "w_in", "conv_w", "a_log", "dt_bias", "gdn_norm_g", "fox_b_f", "fox_q_norm", "fox_k_norm", "g_mem",
             "w_mem_kv", "mem_q_norm", "mem_k_norm", "w_up_gdn", "w_up_fox", "w_up_mem", "w_out", "g_mlp", "w_ff1", "w_ff2"]
    return (loss, grad_x, *[res[n][0] for n in order], *[res[n][1] for n in order],
            *[res[n][2] for n in order], *[res[n][3] for n in order])
```

```python
import functools

import jax
import jax.numpy as jnp
import numpy as np
from jax import lax
from jax.experimental import pallas as pl
from jax.experimental.pallas import tpu as pltpu

F32 = jnp.float32
BF16 = jnp.bfloat16
HI = lax.Precision.HIGHEST

EPS = 1e-6
GDN_CHUNK = 64
GDN_CONV = 4
HEAD = 128
LANES = 128
HALO = 16
N_DEV = 8
MESH = pl.DeviceIdType.MESH
VMEM_LIMIT_V7X = 56 * 1024 * 1024

ADAM_LR, ADAM_B1, ADAM_B2, ADAM_EPS, ADAM_WD, ADAM_STEP = 0.001, 0.9, 0.999, 1e-08, 0.01, 10

LANE_B, LANE_A, LANE_F = 0, 8, 16


def _params(sem):
    return pltpu.CompilerParams(dimension_semantics=sem, vmem_limit_bytes=VMEM_LIMIT_V7X)


def _dg(a, b, ca, cb, prec):
    nb = a.ndim - 2
    batch = tuple(range(nb))
    return lax.dot_general(a, b, (((ca + nb,), (cb + nb,)), (batch, batch)), precision=prec,
                           preferred_element_type=F32)


def _make_mm(prec, cast):
    def c(x):
        return x.astype(BF16) if cast else x

    @jax.custom_vjp
    def nn(a, b):
        return _dg(c(a), c(b), 1, 0, prec)

    @jax.custom_vjp
    def nt(a, b):
        return _dg(c(a), c(b), 1, 1, prec)

    @jax.custom_vjp
    def tn(a, b):
        return _dg(c(a), c(b), 0, 0, prec)

    nn.defvjp(lambda a, b: (nn(a, b), (a, b)), lambda r, g: (nt(g, r[1]), tn(r[0], g)))
    nt.defvjp(lambda a, b: (nt(a, b), (a, b)), lambda r, g: (nn(g, r[1]), tn(g, r[0])))
    tn.defvjp(lambda a, b: (tn(a, b), (a, b)), lambda r, g: (nt(r[1], g), nn(r[0], g)))
    return nn, nt, tn


NN, NT, TN = _make_mm(None, True)
NNH, NTH, TNH = _make_mm(lax.Precision.HIGH, False)
NNX, _, _ = _make_mm(HI, False)


def _sigmoid(x):
    return 1.0 / (1.0 + jnp.exp(-x))


def _silu(x):
    return x * _sigmoid(x)


def _softplus(x):
    return jnp.maximum(x, 0.0) + jnp.log(1.0 + jnp.exp(-jnp.abs(x)))


def _log_sigmoid(x):
    return -_softplus(-x)


def _rms(x, g):
    return x * lax.rsqrt(jnp.mean(x * x, axis=-1, keepdims=True) + EPS) * g


def _tile(n, target):
    t = target
    while t >= LANES:
        if n % t == 0:
            return t
        t //= 2
    return n


def matmul(a, b, *, mode, name, out_dtypes=(BF16,), epi=None, extras=(), tm=1024, tn=1024, tk=2048, comm=None,
           b_shards=False, out_shards=False):
    if b_shards:
        b_rows, b_cols = b.shape[1], N_DEV * b.shape[2]
    else:
        b_rows, b_cols = b.shape
    if mode == "nn":
        (M, K), (K2, N) = a.shape, (b_rows, b_cols)
    elif mode == "nt":
        (M, K), (N, K2) = a.shape, (b_rows, b_cols)
    else:
        (K, M), (K2, N) = a.shape, (b_rows, b_cols)
    assert K == K2, (name, a.shape, b.shape)
    tm, tn, tk = _tile(M, tm), _tile(N, tn), _tile(K, tk)
    if b_shards and mode == "nt":
        tk = K // N_DEV
    if (b_shards and mode != "nt") or out_shards:
        tn = N // N_DEV
    ni, nj, nk = M // tm, N // tn, K // tk
    a_spec = (pl.BlockSpec((tk, tm), lambda i, j, k: (k, i)) if mode == "tn"
              else pl.BlockSpec((tm, tk), lambda i, j, k: (i, k)))
    if b_shards:
        b_spec = (pl.BlockSpec((None, tn, tk), lambda i, j, k: (k, j, 0)) if mode == "nt"
                  else pl.BlockSpec((None, tk, tn), lambda i, j, k: (j, k, 0)))
    else:
        b_spec = (pl.BlockSpec((tn, tk), lambda i, j, k: (j, k)) if mode == "nt"
                  else pl.BlockSpec((tk, tn), lambda i, j, k: (k, j)))
    o_spec = pl.BlockSpec((tm, tn), lambda i, j, k: (i, j))
    w_spec = pl.BlockSpec((None, tm, tn), lambda i, j, k: (j, i, 0)) if out_shards else o_spec
    w_shape = (N_DEV, M, tn) if out_shards else (M, N)
    dims = {"nn": ((1,), (0,)), "nt": ((1,), (1,)), "tn": ((0,), (0,))}[mode]
    ne, no = len(extras), len(out_dtypes)
    nc = len(comm.arrays) if comm is not None else 0
    n_steps = ni * nj * nk

    def body(a_ref, b_ref, *rest):
        ex, c_in = rest[:ne], rest[ne:ne + nc]
        outs, c_out = rest[ne + nc:ne + nc + no], rest[ne + nc + no:ne + nc + no + nc]
        acc = rest[ne + nc + no + nc]
        c_sems = rest[ne + nc + no + nc + 1:]
        k = pl.program_id(2)
        step = (pl.program_id(0) * nj + pl.program_id(1)) * nk + k

        if comm is not None:
            pl.when(step == 0)(lambda: comm.start(c_in, c_out, c_sems))
            if comm.forward is not None and n_steps >= 3:
                pl.when(step == (3 * n_steps) // 4)(lambda: comm.forward(c_in, c_out, c_sems))

        @pl.when(k == 0)
        def _():
            acc[...] = jnp.zeros_like(acc)

        acc[...] += lax.dot_general(a_ref[...].astype(BF16), b_ref[...].astype(BF16), (dims, ((), ())),
                                    preferred_element_type=F32)

        @pl.when(k == nk - 1)
        def _():
            r = acc[...]
            vals = epi(r, *[e[...] for e in ex]) if epi is not None else (r,)
            for o, v in zip(outs, vals):
                o[...] = v.astype(o.dtype)

        if comm is not None:
            @pl.when(step == n_steps - 1)
            def _():
                if comm.forward is not None and n_steps < 3:
                    comm.forward(c_in, c_out, c_sems)
                comm.finish(c_in, c_out, c_sems)

    any_spec = pl.BlockSpec(memory_space=pl.ANY)
    sem = ("arbitrary",) * 3 if comm is not None else ("parallel", "parallel", "arbitrary")
    out = pl.pallas_call(
        body, name=name, grid=(ni, nj, nk),
        in_specs=[a_spec, b_spec] + [o_spec] * ne + [any_spec] * nc, out_specs=[w_spec] * no + [any_spec] * nc,
        out_shape=[jax.ShapeDtypeStruct(w_shape, d) for d in out_dtypes] + (comm.out_shape if comm is not None else []),
        scratch_shapes=[pltpu.VMEM((tm, tn), F32)] + (comm.sems if comm is not None else []),
        compiler_params=_params(sem))(a, b, *extras, *(comm.arrays if comm is not None else []))
    return out[0] if len(out) == 1 else out


def _row_spec(tr, cb, off, moves):
    return pl.BlockSpec((tr, cb), lambda i, j: (i, off + moves * j))


def _whole_spec(p):
    return pl.BlockSpec(p.shape, lambda i, j: (0,) * p.ndim)


def _row_tile(T, rows):
    widest = max(cb for (_, cb, _, _) in rows)
    return min(T, max(512, (1 << 19) // widest))


def rowwise(fn, rows, params, outs, *, name, ncb=1, tr=None, comm=None):
    T = rows[0][0].shape[0]
    tr = min(tr, T) if tr else _row_tile(T, rows)
    assert T % tr == 0
    nr, npar, no = len(rows), len(params), len(outs)
    nc = len(comm.arrays) if comm is not None else 0
    n_steps = (T // tr) * ncb

    def body(*refs):
        r, p, c_in = refs[:nr], refs[nr:nr + npar], refs[nr + npar:nr + npar + nc]
        o, c_out, c_sems = refs[nr + npar + nc:nr + npar + nc + no], refs[nr + npar + nc + no:nr + npar + 2 * nc + no], \
            refs[nr + npar + 2 * nc + no:]
        step = pl.program_id(0) * ncb + pl.program_id(1)
        if comm is not None:
            pl.when(step == 0)(lambda: comm.start(c_in, c_out, c_sems))
            if comm.forward is not None and n_steps >= 3:
                pl.when(step == (3 * n_steps) // 4)(lambda: comm.forward(c_in, c_out, c_sems))
        vals = fn(*[x[...].astype(F32) for x in r], *[x[...] for x in p])
        for oref, v in zip(o, vals):
            oref[...] = v.astype(oref.dtype)
        if comm is not None:
            @pl.when(step == n_steps - 1)
            def _():
                if comm.forward is not None and n_steps < 3:
                    comm.forward(c_in, c_out, c_sems)
                comm.finish(c_in, c_out, c_sems)

    any_spec = pl.BlockSpec(memory_space=pl.ANY)
    res = pl.pallas_call(
        body, name=name, grid=(T // tr, ncb),
        in_specs=([_row_spec(tr, cb, off, mv) for (_, cb, off, mv) in rows] + [_whole_spec(p) for p in params]
                  + [any_spec] * nc),
        out_specs=[_row_spec(tr, cb, 0, 1) for (_, cb, _) in outs] + [any_spec] * nc,
        out_shape=[jax.ShapeDtypeStruct((T, cols), d) for (cols, _, d) in outs] + (comm.out_shape if comm is not None else []),
        scratch_shapes=comm.sems if comm is not None else [],
        compiler_params=_params(("arbitrary", "arbitrary") if comm is not None else ("parallel", "parallel")))(
            *[r[0] for r in rows], *params, *(comm.arrays if comm is not None else []))
    return res


def rowwise_bwd(fn, rows, params, cots, drows, dparams, *, name, ncb=1, tr=None, adds=()):
    T = rows[0][0].shape[0]
    tr = min(tr, T) if tr else _row_tile(T, rows)
    assert T % tr == 0
    nr, npar, nc, ndr, na = len(rows), len(params), len(cots), len(drows), len(adds)

    def body(*refs):
        r, p, c = refs[:nr], refs[nr:nr + npar], refs[nr + npar:nr + npar + nc]
        base = nr + npar + nc + na
        ad, o_r, o_p = refs[base - na:base], refs[base:base + ndr], refs[base + ndr:]
        prim = [x[...].astype(F32) for x in r] + [x[...] for x in p]
        _, vjp = jax.vjp(lambda *a: tuple(fn(*a)), *prim)
        g = vjp(tuple(x[...].astype(F32) for x in c))
        for k, (oref, (idx, _)) in enumerate(zip(o_r, drows)):
            val = g[idx] + ad[k][...].astype(F32) if k < na else g[idx]
            oref[...] = val.astype(oref.dtype)
        first = jnp.logical_and(pl.program_id(0) == 0, pl.program_id(1) == 0)

        @pl.when(first)
        def _():
            for oref in o_p:
                oref[...] = jnp.zeros_like(oref)

        for oref, idx in zip(o_p, dparams):
            oref[...] += g[nr + idx]

    res = pl.pallas_call(
        body, name=name, grid=(T // tr, ncb),
        in_specs=([_row_spec(tr, cb, off, mv) for (_, cb, off, mv) in rows] + [_whole_spec(p) for p in params]
                  + [_row_spec(tr, cb, off, mv) for (_, cb, off, mv) in tuple(cots) + tuple(adds)]),
        out_specs=([_row_spec(tr, rows[idx][1], 0, 1) for (idx, _) in drows]
                   + [_whole_spec(params[idx]) for idx in dparams]),
        out_shape=([jax.ShapeDtypeStruct((T, ncb * rows[idx][1] if rows[idx][3] else rows[idx][1]), d)
                    for (idx, d) in drows]
                   + [jax.ShapeDtypeStruct(params[idx].shape, F32) for idx in dparams]),
        compiler_params=_params(("arbitrary", "arbitrary")))(
            *[r[0] for r in rows], *params, *[c[0] for c in cots], *[a[0] for a in adds])
    return res


def full(a):
    return (a, a.shape[1], 0, 0)


def cumsum_tokens(x, *, reverse, name, tb=256):
    T = x.shape[0]
    tb = min(tb, T)
    nb = T // tb
    idx = (lambda i: (nb - 1 - i, 0)) if reverse else (lambda i: (i, 0))

    def body(x_ref, o_ref, carry):
        @pl.when(pl.program_id(0) == 0)
        def _():
            carry[...] = jnp.zeros_like(carry)

        ii = lax.broadcasted_iota(jnp.int32, (tb, tb), 0)
        jj = lax.broadcasted_iota(jnp.int32, (tb, tb), 1)
        tri = ((ii <= jj) if reverse else (ii >= jj)).astype(F32)
        c = lax.dot_general(tri, x_ref[...], (((1,), (0,)), ((), ())), precision=HI,
                            preferred_element_type=F32) + carry[0:1, :]
        o_ref[...] = c
        carry[0:1, :] = c[0:1, :] if reverse else c[tb - 1:tb, :]

    return pl.pallas_call(
        body, name=name, grid=(nb,), in_specs=[pl.BlockSpec((tb, LANES), idx)],
        out_specs=pl.BlockSpec((tb, LANES), idx), out_shape=jax.ShapeDtypeStruct((T, LANES), F32),
        scratch_shapes=[pltpu.VMEM((8, LANES), F32)], compiler_params=_params(("arbitrary",)))(x)


def _conv_post(y, kind, dk):
    c = _silu(y)
    r = lax.rsqrt(jnp.sum(c * c, axis=-1, keepdims=True) + EPS)
    return jnp.where(kind == 0, c * r * (dk ** -0.5), jnp.where(kind == 1, c * r, c))


def _conv_taps(cur, prev, w, tr):
    ext = jnp.concatenate([prev, cur], axis=0)
    y = w[3:4, :] * cur
    for d in (1, 2, 3):
        y = y + w[3 - d:4 - d, :] * pltpu.roll(ext, d, 0)[HALO:HALO + tr]
    return y


def conv_fwd(big, w, n_qk_heads, *, name, tr=2048):
    T, W = big.shape[0], w.shape[1]
    tr = min(tr, T)
    nh = W // HEAD

    def body(cur_ref, prev_ref, w_ref, o_ref):
        i, j = pl.program_id(0), pl.program_id(1)
        prev = jnp.where(i > 0, prev_ref[...].astype(F32), 0.0)
        y = _conv_taps(cur_ref[...].astype(F32), prev, w_ref[...], tr)
        kind = jnp.where(j < n_qk_heads, 0, jnp.where(j < 2 * n_qk_heads, 1, 2))
        o_ref[...] = _conv_post(y, kind, HEAD)

    return pl.pallas_call(
        body, name=name, grid=(T // tr, nh),
        in_specs=[pl.BlockSpec((tr, HEAD), lambda i, j: (i, j)),
                  pl.BlockSpec((HALO, HEAD), lambda i, j: (jnp.maximum(i * (tr // HALO) - 1, 0), j)),
                  pl.BlockSpec((GDN_CONV, HEAD), lambda i, j: (0, j))],
        out_specs=pl.BlockSpec((tr, HEAD), lambda i, j: (i, j)),
        out_shape=jax.ShapeDtypeStruct((T, W), F32), compiler_params=_params(("parallel", "parallel")))(big, big, w)


def conv_bwd_taps(big, w, dcn, n_qk_heads, *, name, tr=2048):
    T, W = big.shape[0], w.shape[1]
    tr = min(tr, T)
    nh = W // HEAD

    def body(cur_ref, prev_ref, w_ref, g_ref, dy_ref, dw_ref):
        j, i = pl.program_id(0), pl.program_id(1)
        cur = cur_ref[...].astype(F32)
        prev = jnp.where(i > 0, prev_ref[...].astype(F32), 0.0)
        y = _conv_taps(cur, prev, w_ref[...], tr)
        kind = jnp.where(j < n_qk_heads, 0, jnp.where(j < 2 * n_qk_heads, 1, 2))
        _, vjp = jax.vjp(lambda t: _conv_post(t, kind, HEAD), y)
        dy, = vjp(g_ref[...])
        dy_ref[...] = dy
        ext = jnp.concatenate([prev, cur], axis=0)
        rows = [jnp.sum(dy * (cur if d == 0 else pltpu.roll(ext, d, 0)[HALO:HALO + tr]), axis=0, keepdims=True)
                for d in (3, 2, 1, 0)]

        @pl.when(i == 0)
        def _():
            dw_ref[...] = jnp.zeros_like(dw_ref)

        dw_ref[...] += jnp.concatenate(rows, axis=0)

    return pl.pallas_call(
        body, name=name, grid=(nh, T // tr),
        in_specs=[pl.BlockSpec((tr, HEAD), lambda j, i: (i, j)),
                  pl.BlockSpec((HALO, HEAD), lambda j, i: (jnp.maximum(i * (tr // HALO) - 1, 0), j)),
                  pl.BlockSpec((GDN_CONV, HEAD), lambda j, i: (0, j)),
                  pl.BlockSpec((tr, HEAD), lambda j, i: (i, j))],
        out_specs=[pl.BlockSpec((tr, HEAD), lambda j, i: (i, j)), pl.BlockSpec((GDN_CONV, HEAD), lambda j, i: (0, j))],
        out_shape=[jax.ShapeDtypeStruct((T, W), F32), jax.ShapeDtypeStruct((GDN_CONV, W), F32)],
        compiler_params=_params(("parallel", "arbitrary")))(big, big, w, dcn)


def conv_bwd_input(dy, w, *, name, tr=2048):
    T, W = dy.shape
    tr = min(tr, T)
    nrow = T // tr

    def body(cur_ref, nxt_ref, w_ref, o_ref):
        i = pl.program_id(0)
        cur = cur_ref[...]
        nxt = jnp.where(i < nrow - 1, nxt_ref[...], 0.0)
        ext = jnp.concatenate([cur, nxt], axis=0)
        w = w_ref[...]
        dx = w[3:4, :] * cur
        for d in (1, 2, 3):
            dx = dx + w[3 - d:4 - d, :] * pltpu.roll(ext, tr + HALO - d, 0)[0:tr]
        o_ref[...] = dx.astype(o_ref.dtype)

    return pl.pallas_call(
        body, name=name, grid=(nrow, W // HEAD),
        in_specs=[pl.BlockSpec((tr, HEAD), lambda i, j: (i, j)),
                  pl.BlockSpec((HALO, HEAD), lambda i, j: (jnp.minimum((i + 1) * (tr // HALO), T // HALO - 1), j)),
                  pl.BlockSpec((GDN_CONV, HEAD), lambda i, j: (0, j))],
        out_specs=pl.BlockSpec((tr, HEAD), lambda i, j: (i, j)),
        out_shape=jax.ShapeDtypeStruct((T, W), BF16), compiler_params=_params(("parallel", "parallel")))(dy, dy, w)


INV_BLOCK = 16


def _unit_lower_inverse_value(p):
    C = p.shape[-1]
    ii = lax.broadcasted_iota(jnp.int32, (C, C), 0)
    jj = lax.broadcasted_iota(jnp.int32, (C, C), 1)
    eye = jnp.where((ii == jj)[None], 1.0, 0.0)
    same = ((ii // INV_BLOCK) == (jj // INV_BLOCK))[None]
    pd = jnp.where(same, p, 0.0)
    d_inv = eye + pd
    n = 2
    while n < INV_BLOCK:
        pd = NNH(pd, pd)
        d_inv = d_inv + NNH(d_inv, pd)
        n *= 2
    nb = NNH(d_inv, jnp.where(same, 0.0, p))
    t = eye + nb
    n = 2
    while n < C // INV_BLOCK:
        nb = NNH(nb, nb)
        t = t + NNH(t, nb)
        n *= 2
    return NNH(t, d_inv)


@jax.custom_vjp
def _unit_lower_inverse(p, t_known):
    return _unit_lower_inverse_value(p) if t_known is None else t_known


def _unit_lower_inverse_fwd(p, t_known):
    t = _unit_lower_inverse(p, t_known)
    return t, (t, t_known is not None)


def _unit_lower_inverse_bwd(res, g):
    t, had = res
    return NTH(TNH(t, g), t), (jnp.zeros_like(t) if had else None)


_unit_lower_inverse.defvjp(_unit_lower_inverse_fwd, _unit_lower_inverse_bwd)


def _gdn_chunk(q, k, v, bg, S, t_known=None):
    H, C = q.shape[0], q.shape[1]
    ii = lax.broadcasted_iota(jnp.int32, (C, C), 0)
    jj = lax.broadcasted_iota(jnp.int32, (C, C), 1)
    lincl = (ii >= jj).astype(F32)
    strict, incl, eye = (ii > jj)[None], (ii >= jj)[None], (ii == jj)[None]
    gam2d = lax.dot_general(lincl, bg, (((1,), (0,)), ((), ())), precision=HI, preferred_element_type=F32)
    lane = lax.broadcasted_iota(jnp.int32, (H, 1, LANES), 2)
    hh = lax.broadcasted_iota(jnp.int32, (H, 1, LANES), 0)
    beta = jnp.sum(bg[None] * (lane == hh + LANE_B).astype(F32), axis=2, keepdims=True)
    gam = jnp.sum(gam2d[None] * (lane == hh + LANE_A).astype(F32), axis=2, keepdims=True)
    last = (lax.broadcasted_iota(jnp.int32, (1, C, 1), 1) == C - 1).astype(F32)
    gam_last = jnp.sum(gam * last, axis=1, keepdims=True)
    gam_row = NNX(jnp.ones((H, C, C), F32), jnp.where(eye, gam, 0.0))
    diff = gam - gam_row
    dec_s = jnp.where(strict, jnp.exp(jnp.where(strict, diff, 0.0)), 0.0)
    dec_i = jnp.where(incl, jnp.exp(jnp.where(incl, diff, 0.0)), 0.0)
    t = _unit_lower_inverse(-(beta * NT(k, k) * dec_s), t_known)
    eg = jnp.exp(gam)
    wu = NNH(t, jnp.concatenate([beta * eg * k, beta * v], axis=-1))
    w, u0 = wu[..., :HEAD], wu[..., HEAD:]
    qk = NT(q, k) * dec_i
    u = u0 - NN(w, S)
    o = NN(jnp.concatenate([q * eg, qk], axis=-1), jnp.concatenate([S, u], axis=-2))
    S2 = jnp.exp(gam_last) * S + TN(k * jnp.exp(gam_last - gam), u)
    return o, S2, t


def _heads(x, base, H):
    return jnp.stack([x[:, base + h * HEAD:base + (h + 1) * HEAD] for h in range(H)])


def _unheads(x):
    return jnp.concatenate([x[h] for h in range(x.shape[0])], axis=1)


GDN_CHUNKS_PER_STEP = 8


def gdn_fwd(cn, bgf, H, *, name):
    T, C, W = cn.shape[0], GDN_CHUNK, H * HEAD
    N = T // C
    K = GDN_CHUNKS_PER_STEP if N % GDN_CHUNKS_PER_STEP == 0 else 1

    def body(cn_ref, bg_ref, o_ref, ss_ref, t_ref, s_scr):
        @pl.when(pl.program_id(0) == 0)
        def _():
            s_scr[...] = jnp.zeros_like(s_scr)

        S = s_scr[...]
        for c in range(K):
            rows = slice(c * C, (c + 1) * C)
            x = cn_ref[rows, :]
            ss_ref[c] = S
            o, S, t = _gdn_chunk(_heads(x, 0, H), _heads(x, W, H), _heads(x, 2 * W, H), bg_ref[rows, :], S)
            o_ref[rows, :] = _unheads(o)
            t_ref[c] = t
        s_scr[...] = S

    return pl.pallas_call(
        body, name=name, grid=(N // K,),
        in_specs=[pl.BlockSpec((K * C, 3 * W), lambda n: (n, 0)), pl.BlockSpec((K * C, LANES), lambda n: (n, 0))],
        out_specs=[pl.BlockSpec((K * C, W), lambda n: (n, 0)), pl.BlockSpec((K, H, HEAD, HEAD), lambda n: (n, 0, 0, 0)),
                   pl.BlockSpec((K, H, C, C), lambda n: (n, 0, 0, 0))],
        out_shape=[jax.ShapeDtypeStruct((T, W), F32), jax.ShapeDtypeStruct((N, H, HEAD, HEAD), F32),
                   jax.ShapeDtypeStruct((N, H, C, C), F32)],
        scratch_shapes=[pltpu.VMEM((H, HEAD, HEAD), F32)], compiler_params=_params(("arbitrary",)))(cn, bgf)


def gdn_bwd(cn, bgf, ss, tinv, do, H, *, name):
    T, C, W = cn.shape[0], GDN_CHUNK, H * HEAD
    N = T // C
    K = GDN_CHUNKS_PER_STEP if N % GDN_CHUNKS_PER_STEP == 0 else 1
    NS = N // K

    def body(cn_ref, bg_ref, ss_ref, t_ref, do_ref, dcn_ref, dbg_ref, ds_scr):
        @pl.when(pl.program_id(0) == 0)
        def _():
            ds_scr[...] = jnp.zeros_like(ds_scr)

        dS = ds_scr[...]
        for c in reversed(range(K)):
            rows = slice(c * C, (c + 1) * C)
            x = cn_ref[rows, :]
            t_known = t_ref[c]
            _, vjp = jax.vjp(lambda *a: _gdn_chunk(*a, t_known)[:2],
                             _heads(x, 0, H), _heads(x, W, H), _heads(x, 2 * W, H), bg_ref[rows, :], ss_ref[c])
            dq, dk, dv, dbg, dS = vjp((_heads(do_ref[rows, :], 0, H), dS))
            dcn_ref[rows, :] = jnp.concatenate([_unheads(dq), _unheads(dk), _unheads(dv)], axis=1)
            dbg_ref[rows, :] = dbg
        ds_scr[...] = dS

    rev = lambda n: (NS - 1 - n, 0)
    rev4 = lambda n: (NS - 1 - n, 0, 0, 0)
    return pl.pallas_call(
        body, name=name, grid=(NS,),
        in_specs=[pl.BlockSpec((K * C, 3 * W), rev), pl.BlockSpec((K * C, LANES), rev),
                  pl.BlockSpec((K, H, HEAD, HEAD), rev4), pl.BlockSpec((K, H, C, C), rev4), pl.BlockSpec((K * C, W), rev)],
        out_specs=[pl.BlockSpec((K * C, 3 * W), rev), pl.BlockSpec((K * C, LANES), rev)],
        out_shape=[jax.ShapeDtypeStruct((T, 3 * W), F32), jax.ShapeDtypeStruct((T, LANES), F32)],
        scratch_shapes=[pltpu.VMEM((H, HEAD, HEAD), F32)], compiler_params=_params(("arbitrary",)))(cn, bgf, ss, tinv, do)


AUG = 2 * HEAD


def fox_prep(big, col_off, gain, cum, kind, H, *, name, tr=2048):
    T = big.shape[0]
    tr = min(tr, T)

    def body(x_ref, g_ref, c_ref, o_ref):
        h = pl.program_id(1)
        x = x_ref[...].astype(F32)
        lane = lax.broadcasted_iota(jnp.int32, (tr, HEAD), 1)
        if kind == 2:
            main, aug = x, jnp.ones((tr, HEAD), F32)
        else:
            main = _rms(x, g_ref[...]) * ((HEAD ** -0.5) if kind == 0 else 1.0)
            c = jnp.sum(jnp.where(lane == LANE_F + h, c_ref[...], 0.0), axis=1, keepdims=True)
            hi = c.astype(BF16).astype(F32)
            mid = (c - hi).astype(BF16).astype(F32)
            lo = c - hi - mid
            if kind == 0:
                aug = jnp.where(lane == 0, hi, jnp.where(lane == 1, mid, jnp.where(lane == 2, lo,
                                                                                   jnp.where(lane < 6, 1.0, 0.0))))
            else:
                aug = jnp.where(lane < 3, 1.0, jnp.where(lane == 3, -hi, jnp.where(lane == 4, -mid,
                                                                                   jnp.where(lane == 5, -lo, 0.0))))
        o_ref[...] = jnp.concatenate([main, aug], axis=1).astype(o_ref.dtype)

    return pl.pallas_call(
        body, name=name, grid=(T // tr, H),
        in_specs=[pl.BlockSpec((tr, HEAD), lambda i, h: (i, col_off // HEAD + h)),
                  pl.BlockSpec((1, HEAD), lambda i, h: (0, 0)), pl.BlockSpec((tr, LANES), lambda i, h: (i, 0))],
        out_specs=pl.BlockSpec((tr, AUG), lambda i, h: (i, h)),
        out_shape=jax.ShapeDtypeStruct((T, H * AUG), BF16), compiler_params=_params(("parallel", "parallel")))(big, gain, cum)


def _fox_logits(q_ref, k_ref, h, tq, tk, diagonal):
    ha = slice(h * AUG, (h + 1) * AUG)
    s = lax.dot_general(q_ref[:, ha], k_ref[:, ha], (((1,), (1,)), ((), ())), preferred_element_type=F32)
    keep = None
    if diagonal:
        keep = lax.broadcasted_iota(jnp.int32, (tq, tk), 0) >= lax.broadcasted_iota(jnp.int32, (tq, tk), 1)
    return s, keep


def _dispatch(live_ref, H, i, j, below, on_diagonal, head_fn, after=None, straight_from=None):
    straight_from = H if straight_from is None else straight_from
    def straight(diagonal):
        for h in range(H):
            head_fn(h, diagonal)
        if after is not None:
            after()

    def by_head():
        for h in range(H):
            pl.when(live_ref[h, i, j] != 0)(functools.partial(head_fn, h, False))
        if after is not None:
            after()

    n_live = live_ref[H, i, j]
    pl.when(jnp.logical_and(below, n_live >= straight_from))(functools.partial(straight, False))
    pl.when(jnp.logical_and(below, jnp.logical_and(n_live > 0, n_live < straight_from)))(by_head)
    pl.when(on_diagonal)(functools.partial(straight, True))


FOX_TILE = 512
EXP_UNDERFLOW = -100.0


def fox_live_tiles(cum, q_gain, k_gain, H, T):
    t = min(FOX_TILE, T)
    n = T // t
    c = cum[:, LANE_F:LANE_F + H]
    bias = c[0::t][:, None, :] - c[t - 1::t][None, :, :]
    bound = 1.02 * (HEAD ** 0.5) * jnp.max(jnp.abs(q_gain)) * jnp.max(jnp.abs(k_gain))
    causal = (jnp.arange(n)[:, None] >= jnp.arange(n)[None, :])[:, :, None]
    live = jnp.logical_and(causal, 2.0 * bound + bias >= EXP_UNDERFLOW)
    live = live.astype(jnp.int32)
    return jnp.concatenate([jnp.transpose(live, (2, 0, 1)), jnp.sum(live, axis=2)[None]], axis=0)


def fox_fwd(qa, ka, va, live, H, *, name):
    T = qa.shape[0]
    tq = tk = min(FOX_TILE, T)
    nq, nk = T // tq, T // tk

    def body(live_ref, q_ref, k_ref, v_ref, o_ref, lse_ref, acc, m_scr):
        i, j = pl.program_id(0), pl.program_id(1)

        @pl.when(j == 0)
        def _():
            m_scr[...] = jnp.full_like(m_scr, -jnp.inf)
            acc[...] = jnp.zeros_like(acc)

        def head(h, diagonal):
            ha = slice(h * AUG, (h + 1) * AUG)
            s, keep = _fox_logits(q_ref, k_ref, h, tq, tk, diagonal)
            if diagonal:
                s = jnp.where(keep, s, -jnp.inf)
            m_prev = m_scr[h]
            m_new = jnp.maximum(m_prev, jnp.max(s, axis=1, keepdims=True))
            p = jnp.exp(s - m_new[:, 0:1])
            p_hi = p.astype(BF16)
            p_lo = (p - p_hi.astype(F32)).astype(BF16)
            pv = lambda t: lax.dot_general(t, v_ref[:, ha], (((1,), (0,)), ((), ())), preferred_element_type=F32)
            acc[:, ha] = jnp.exp(m_prev[:, 0:1] - m_new[:, 0:1]) * acc[:, ha] + (pv(p_hi) + pv(p_lo))
            m_scr[h] = m_new

        _dispatch(live_ref, H, i, j, j < i, j == i, head, straight_from=(5 * H + 7) // 8)

        @pl.when(j == nk - 1)
        def _():
            lane = lax.broadcasted_iota(jnp.int32, (tq, LANES), 1)
            lse = jnp.zeros((tq, LANES), F32)
            for h in range(H):
                den = acc[:, h * AUG + HEAD:(h + 1) * AUG]
                o_ref[:, h * HEAD:(h + 1) * HEAD] = acc[:, h * AUG:h * AUG + HEAD] / den
                lse = jnp.where(lane == h, m_scr[h] + jnp.log(den), lse)
            lse_ref[...] = lse

    kv_idx = lambda i, j, f: (jnp.minimum(j, i), 0)
    q_idx = lambda i, j, f: (i, 0)
    return pl.pallas_call(
        body, name=name,
        grid_spec=pltpu.PrefetchScalarGridSpec(
            num_scalar_prefetch=1, grid=(nq, nk),
            in_specs=[pl.BlockSpec((tq, H * AUG), q_idx), pl.BlockSpec((tk, H * AUG), kv_idx), pl.BlockSpec((tk, H * AUG), kv_idx)],
            out_specs=[pl.BlockSpec((tq, H * HEAD), q_idx), pl.BlockSpec((tq, LANES), q_idx)],
            scratch_shapes=[pltpu.VMEM((tq, H * AUG), F32), pltpu.VMEM((H, tq, LANES), F32)]),
        out_shape=[jax.ShapeDtypeStruct((T, H * HEAD), F32), jax.ShapeDtypeStruct((T, LANES), F32)],
        compiler_params=_params(("parallel", "arbitrary")))(live, qa, ka, va)


def _fox_ds(q_ref, k_ref, v_ref, do_ref, o_ref, lse_ref, h, tq, tk, diagonal):
    hs = slice(h * HEAD, (h + 1) * HEAD)
    s, keep = _fox_logits(q_ref, k_ref, h, tq, tk, diagonal)
    p = jnp.exp(s - lse_ref[:, h:h + 1])
    if diagonal:
        p = jnp.where(keep, p, 0.0)
    do = do_ref[:, hs]
    dp = lax.dot_general(do, v_ref[:, h * AUG:h * AUG + HEAD], (((1,), (1,)), ((), ())), preferred_element_type=F32)
    delta = jnp.sum(do.astype(F32) * o_ref[:, hs], axis=1, keepdims=True)
    return p, p * (dp - delta)


def fox_bwd_kv(qa, ka, va, do, o, lse, live, H, *, name):
    T, W = do.shape
    tq = tk = min(FOX_TILE, T)
    nq, nk = T // tq, T // tk

    def body(live_ref, q_ref, k_ref, v_ref, do_ref, o_ref, lse_ref, dk_ref, dv_ref, dc_ref, dk_acc, dv_acc, dc_acc):
        j, i = pl.program_id(0), pl.program_id(1)

        @pl.when(i == 0)
        def _():
            dk_acc[...] = jnp.zeros_like(dk_acc)
            dv_acc[...] = jnp.zeros_like(dv_acc)
            dc_acc[...] = jnp.zeros_like(dc_acc)

        def head(h, diagonal):
            hs = slice(h * HEAD, (h + 1) * HEAD)
            p, ds = _fox_ds(q_ref, k_ref, v_ref, do_ref, o_ref, lse_ref, h, tq, tk, diagonal)
            dv_acc[:, hs] += lax.dot_general(p.astype(BF16), do_ref[:, hs], (((0,), (0,)), ((), ())),
                                             preferred_element_type=F32)
            dk_acc[:, hs] += lax.dot_general(ds.astype(BF16), q_ref[:, h * AUG:h * AUG + HEAD],
                                             (((0,), (0,)), ((), ())), preferred_element_type=F32)
            dc_acc[h] -= jnp.broadcast_to(jnp.sum(ds, axis=0, keepdims=True), (8, tk))

        _dispatch(live_ref, H, i, j, i > j, i == j, head)

        @pl.when(i == nq - 1)
        def _():
            dk_ref[...] = dk_acc[...].astype(dk_ref.dtype)
            dv_ref[...] = dv_acc[...].astype(dv_ref.dtype)
            row = lax.broadcasted_iota(jnp.int32, (8, tk), 0)
            dc = jnp.zeros((8, tk), F32)
            for h in range(H):
                dc = jnp.where(row == h, dc_acc[h], dc)
            dc_ref[...] = dc

    q_idx = lambda j, i, f: (jnp.maximum(i, j), 0)
    kv_idx = lambda j, i, f: (j, 0)
    return pl.pallas_call(
        body, name=name,
        grid_spec=pltpu.PrefetchScalarGridSpec(
            num_scalar_prefetch=1, grid=(nk, nq),
            in_specs=[pl.BlockSpec((tq, H * AUG), q_idx), pl.BlockSpec((tk, H * AUG), kv_idx), pl.BlockSpec((tk, H * AUG), kv_idx),
                      pl.BlockSpec((tq, W), q_idx), pl.BlockSpec((tq, W), q_idx), pl.BlockSpec((tq, LANES), q_idx)],
            out_specs=[pl.BlockSpec((tk, W), kv_idx), pl.BlockSpec((tk, W), kv_idx),
                       pl.BlockSpec((8, tk), lambda j, i, f: (0, j))],
            scratch_shapes=[pltpu.VMEM((tk, W), F32), pltpu.VMEM((tk, W), F32), pltpu.VMEM((H, 8, tk), F32)]),
        out_shape=[jax.ShapeDtypeStruct((T, W), BF16), jax.ShapeDtypeStruct((T, W), BF16), jax.ShapeDtypeStruct((8, T), F32)],
        compiler_params=_params(("parallel", "arbitrary")))(live, qa, ka, va, do, o, lse)


def fox_bwd_q(qa, ka, va, do, o, lse, live, H, *, name):
    T, W = do.shape
    tq = tk = min(FOX_TILE, T)
    nq, nk = T // tq, T // tk

    def body(live_ref, q_ref, k_ref, v_ref, do_ref, o_ref, lse_ref, dq_ref, dq_acc):
        i, j = pl.program_id(0), pl.program_id(1)

        @pl.when(j == 0)
        def _():
            dq_acc[...] = jnp.zeros_like(dq_acc)

        def head(h, diagonal):
            hs = slice(h * HEAD, (h + 1) * HEAD)
            _, ds = _fox_ds(q_ref, k_ref, v_ref, do_ref, o_ref, lse_ref, h, tq, tk, diagonal)
            dq_acc[:, hs] += lax.dot_general(ds.astype(BF16), k_ref[:, h * AUG:h * AUG + HEAD],
                                             (((1,), (0,)), ((), ())), preferred_element_type=F32)

        _dispatch(live_ref, H, i, j, j < i, j == i, head)

        @pl.when(j == nk - 1)
        def _():
            dq_ref[...] = (dq_acc[...] * (HEAD ** -0.5)).astype(dq_ref.dtype)

    q_idx = lambda i, j, f: (i, 0)
    kv_idx = lambda i, j, f: (jnp.minimum(j, i), 0)
    return pl.pallas_call(
        body, name=name,
        grid_spec=pltpu.PrefetchScalarGridSpec(
            num_scalar_prefetch=1, grid=(nq, nk),
            in_specs=[pl.BlockSpec((tq, H * AUG), q_idx), pl.BlockSpec((tk, H * AUG), kv_idx), pl.BlockSpec((tk, H * AUG), kv_idx),
                      pl.BlockSpec((tq, W), q_idx), pl.BlockSpec((tq, W), q_idx), pl.BlockSpec((tq, LANES), q_idx)],
            out_specs=pl.BlockSpec((tq, W), q_idx),
            scratch_shapes=[pltpu.VMEM((tq, W), F32)]),
        out_shape=jax.ShapeDtypeStruct((T, W), BF16),
        compiler_params=_params(("parallel", "arbitrary")))(live, qa, ka, va, do, o, lse)


def _gates_fn(small, a_log_l, dt_bias_l, b_f_l):
    lane = lax.broadcasted_iota(jnp.int32, small.shape, 1)
    beta = _sigmoid(small)
    g = -jnp.exp(a_log_l) * _softplus(small + dt_bias_l)
    lf = _log_sigmoid(small + b_f_l)
    return (jnp.where(lane < LANE_A, beta, jnp.where(lane < LANE_F, g, jnp.where(lane < LANE_F + 8, lf, 0.0))),)


def _gated_norm_fn(o, z, g):
    return (_rms(o, g) * _silu(z),)


def _merge_fn(ya, yb, ym, ga, gb, gm):
    return (_sigmoid(ga) * ya + _sigmoid(gb) * yb + _sigmoid(gm) * ym,)


def _mem_attn_fn(nh, dh, mq, kn, v, gq):
    outs = []
    for h in range(nh):
        hs = slice(h * dh, (h + 1) * dh)
        qn = _rms(mq[:, hs], gq)
        s = NT(qn, kn[:, hs]) * (dh ** -0.5)
        e = jnp.exp(s - jnp.max(s, axis=1, keepdims=True))
        p = e / jnp.sum(e, axis=1, keepdims=True)
        outs.append(NN(p, v[:, hs]))
    return (jnp.concatenate(outs, axis=1),)


def sum_squares(x, *, name, tr=512):
    T, D = x.shape
    tr = min(tr, T)

    def body(x_ref, o_ref):
        @pl.when(pl.program_id(0) == 0)
        def _():
            o_ref[...] = jnp.zeros_like(o_ref)

        v = x_ref[...]
        o_ref[...] += jnp.sum(jnp.sum(v * v, axis=1, keepdims=True), axis=0, keepdims=True)

    return pl.pallas_call(
        body, name=name, grid=(T // tr,), in_specs=[pl.BlockSpec((tr, D), lambda i: (i, 0))],
        out_specs=pl.BlockSpec((1, LANES), lambda i: (0, 0)), out_shape=jax.ShapeDtypeStruct((1, LANES), F32),
        compiler_params=_params(("arbitrary",)))(x)


class Comm:
    def __init__(self, arrays, out_shape, sems, start, forward, finish):
        self.arrays, self.out_shape, self.sems = list(arrays), list(out_shape), list(sems)
        self.start, self.forward, self.finish = start, forward, finish


def _place():
    x, y, c = lax.axis_index("x"), lax.axis_index("y"), lax.axis_index("c")
    chips = [(1 - x, y), (x, 1 - y), (1 - x, 1 - y)]
    return x, y, c, chips


def comm_gather(arrays):
    n = len(arrays)
    lin = lambda px, py, pc: 4 * px + 2 * py + pc

    def copy(ins, outs, sems, a, k, block, to, src=None):
        slot = outs[a].at[lin(*block)]
        return pltpu.make_async_remote_copy(src_ref=slot if src is None else src, dst_ref=slot, send_sem=sems[0].at[a, k],
                                            recv_sem=sems[1].at[a, k], device_id=to, device_id_type=MESH)

    def local(ins, outs, sems, a):
        x, y, c, _ = _place()
        return pltpu.make_async_copy(ins[a], outs[a].at[lin(x, y, c)], sems[2].at[a])

    def start(ins, outs, sems):
        x, y, c, chips = _place()
        for a in range(n):
            local(ins, outs, sems, a).start()
        for a in range(n):
            for j, chip in enumerate(chips):
                copy(ins, outs, sems, a, 1 + j, (x, y, c), (*chip, c), src=ins[a]).start()
            copy(ins, outs, sems, a, 0, (x, y, c), (x, y, 1 - c), src=ins[a]).start()

    def forward(ins, outs, sems):
        x, y, c, chips = _place()
        for a in range(n):
            for j, chip in enumerate(chips):
                copy(ins, outs, sems, a, 1 + j, (*chip, c), (x, y, c)).wait_recv()
                copy(ins, outs, sems, a, 4 + j, (*chip, c), (x, y, 1 - c)).start()

    def finish(ins, outs, sems):
        x, y, c, chips = _place()
        for a in range(n):
            copy(ins, outs, sems, a, 0, (x, y, 1 - c), (x, y, c)).wait_recv()
            for j, chip in enumerate(chips):
                copy(ins, outs, sems, a, 4 + j, (*chip, 1 - c), (x, y, c)).wait_recv()
        for a in range(n):
            for j, chip in enumerate(chips):
                copy(ins, outs, sems, a, 1 + j, (x, y, c), (*chip, c), src=ins[a]).wait_send()
                copy(ins, outs, sems, a, 4 + j, (*chip, c), (x, y, 1 - c)).wait_send()
            copy(ins, outs, sems, a, 0, (x, y, c), (x, y, 1 - c), src=ins[a]).wait_send()
            local(ins, outs, sems, a).wait()

    return Comm(arrays, [jax.ShapeDtypeStruct((N_DEV,) + a.shape, a.dtype) for a in arrays],
                [pltpu.SemaphoreType.DMA((n, 7)), pltpu.SemaphoreType.DMA((n, 7)), pltpu.SemaphoreType.DMA((n,))],
                start, forward, finish)


def comm_direct(arrays, scatter):
    n = len(arrays)

    def peers():
        x, y, c = lax.axis_index("x"), lax.axis_index("y"), lax.axis_index("c")
        out = []
        for r in range(1, N_DEV):
            px, py, pc = (1 - x if r & 4 else x), (1 - y if r & 2 else y), (1 - c if r & 1 else c)
            out.append((r, (px, py, pc), 4 * px + 2 * py + pc))
        return 4 * x + 2 * y + c, out

    def remote(ins, outs, sems, a, r, dev, src_slot, dst_slot):
        return pltpu.make_async_remote_copy(
            src_ref=ins[a].at[src_slot] if scatter[a] else ins[a], dst_ref=outs[a].at[dst_slot],
            send_sem=sems[0].at[a, r - 1], recv_sem=sems[1].at[a, r - 1], device_id=dev, device_id_type=MESH)

    def local(ins, outs, sems, a, me):
        return pltpu.make_async_copy(ins[a].at[me] if scatter[a] else ins[a], outs[a].at[me], sems[2].at[a])

    def start(ins, outs, sems):
        me, ps = peers()
        for a in range(n):
            local(ins, outs, sems, a, me).start()
        for a in range(n):
            for r, dev, lin in ps:
                remote(ins, outs, sems, a, r, dev, lin, me).start()

    def finish(ins, outs, sems):
        me, ps = peers()
        for a in range(n):
            for r, dev, lin in ps:
                remote(ins, outs, sems, a, r, dev, lin, lin).wait_recv()
        for a in range(n):
            for r, dev, lin in ps:
                remote(ins, outs, sems, a, r, dev, lin, me).wait_send()
            local(ins, outs, sems, a, me).wait()

    return Comm(arrays, [jax.ShapeDtypeStruct(a.shape if sc else (N_DEV,) + a.shape, a.dtype) for a, sc in zip(arrays, scatter)],
                [pltpu.SemaphoreType.DMA((n, N_DEV - 1)), pltpu.SemaphoreType.DMA((n, N_DEV - 1)),
                 pltpu.SemaphoreType.DMA((n,))], start, None, finish)


def comm_to_sibling(parts):
    n = len(parts)

    def copy(ins, outs, sems, a, k):
        x, y, c, _ = _place()
        return pltpu.make_async_remote_copy(src_ref=ins[a].at[2 * k + (1 - c)], dst_ref=outs[a].at[k], send_sem=sems[0].at[a, k],
                                            recv_sem=sems[1].at[a, k], device_id=(x, y, 1 - c), device_id_type=MESH)

    def start(ins, outs, sems):
        for a in range(n):
            for k in range(4):
                copy(ins, outs, sems, a, k).start()

    def finish(ins, outs, sems):
        for a in range(n):
            for k in range(4):
                copy(ins, outs, sems, a, k).wait()

    return Comm(parts, [jax.ShapeDtypeStruct((4,) + p.shape[1:], p.dtype) for p in parts],
                [pltpu.SemaphoreType.DMA((n, 4)), pltpu.SemaphoreType.DMA((n, 4))], start, None, finish)


def comm_to_owner_chip(sums):
    n = len(sums)

    def remote(ins, outs, sems, a, j, src_k, dst_k, chip):
        _, _, c, _ = _place()
        return pltpu.make_async_remote_copy(src_ref=ins[a].at[src_k], dst_ref=outs[a].at[dst_k], send_sem=sems[0].at[a, j],
                                            recv_sem=sems[1].at[a, j], device_id=(*chip, c), device_id_type=MESH)

    def local(ins, outs, sems, a):
        x, y, _, _ = _place()
        return pltpu.make_async_copy(ins[a].at[2 * x + y], outs[a].at[2 * x + y], sems[2].at[a])

    def start(ins, outs, sems):
        x, y, _, chips = _place()
        for a in range(n):
            local(ins, outs, sems, a).start()
            for j, (px, py) in enumerate(chips):
                remote(ins, outs, sems, a, j, 2 * px + py, 2 * x + y, (px, py)).start()

    def finish(ins, outs, sems):
        x, y, _, chips = _place()
        for a in range(n):
            for j, (px, py) in enumerate(chips):
                remote(ins, outs, sems, a, j, 2 * x + y, 2 * px + py, (px, py)).wait_recv()
        for a in range(n):
            for j, (px, py) in enumerate(chips):
                remote(ins, outs, sems, a, j, 2 * px + py, 2 * x + y, (px, py)).wait_send()
            local(ins, outs, sems, a).wait()

    return Comm(sums, [jax.ShapeDtypeStruct(s.shape, s.dtype) for s in sums],
                [pltpu.SemaphoreType.DMA((n, 3)), pltpu.SemaphoreType.DMA((n, 3)), pltpu.SemaphoreType.DMA((n,))],
                start, None, finish)


def pair_sum(parts, sib, *, name, tr=512):
    _, R, Cc = parts.shape
    tr = min(tr, R)
    assert R % tr == 0
    core = lax.axis_index("c").astype(jnp.int32).reshape(1)

    def body(c_ref, p_ref, s_ref, o_ref):
        o_ref[0] = (p_ref[0, 0].astype(F32) + s_ref[0].astype(F32)).astype(o_ref.dtype)

    return pl.pallas_call(
        body, name=name,
        grid_spec=pltpu.PrefetchScalarGridSpec(
            num_scalar_prefetch=1, grid=(4, R // tr),
            in_specs=[pl.BlockSpec((1, 1, tr, Cc), lambda k, i, c: (k, c[0], i, 0)),
                      pl.BlockSpec((1, tr, Cc), lambda k, i, c: (k, i, 0))],
            out_specs=pl.BlockSpec((1, tr, Cc), lambda k, i, c: (k, i, 0))),
        out_shape=jax.ShapeDtypeStruct((4, R, Cc), BF16),
        compiler_params=_params(("parallel", "parallel")))(core, parts.reshape(4, 2, R, Cc), sib)


def run_comm(comm, *, name):
    n = len(comm.arrays)

    def body(*refs):
        ins, outs, sems = refs[:n], refs[n:2 * n], refs[2 * n:]
        comm.start(ins, outs, sems)
        if comm.forward is not None:
            comm.forward(ins, outs, sems)
        comm.finish(ins, outs, sems)

    any_spec = pl.BlockSpec(memory_space=pl.ANY)
    return pl.pallas_call(body, name=name, in_specs=[any_spec] * n, out_specs=[any_spec] * n, out_shape=comm.out_shape,
                          scratch_shapes=comm.sems)(*comm.arrays)


def adamw(parts, w, m, v, *, name, tr=128):
    R, Cc = w.shape
    tr = min(tr, R)
    assert R % tr == 0
    n_parts = parts.shape[0]

    def body(p_ref, w_ref, m_ref, v_ref, g_ref, d_ref, nm_ref, nv_ref):
        g = p_ref[0].astype(F32)
        for s in range(1, n_parts):
            g = g + p_ref[s].astype(F32)
        nm = ADAM_B1 * m_ref[...] + (1.0 - ADAM_B1) * g
        nv = ADAM_B2 * v_ref[...] + (1.0 - ADAM_B2) * (g * g)
        m_hat = nm / (1.0 - ADAM_B1 ** ADAM_STEP)
        v_hat = nv / (1.0 - ADAM_B2 ** ADAM_STEP)
        g_ref[...] = g
        d_ref[...] = -ADAM_LR * (m_hat / (jnp.sqrt(v_hat) + ADAM_EPS) + ADAM_WD * w_ref[...])
        nm_ref[...] = nm
        nv_ref[...] = nv

    spec = pl.BlockSpec((tr, Cc), lambda i: (i, 0))
    return pl.pallas_call(
        body, name=name, grid=(R // tr,),
        in_specs=[pl.BlockSpec((n_parts, tr, Cc), lambda i: (0, i, 0)), spec, spec, spec], out_specs=[spec] * 4,
        out_shape=[jax.ShapeDtypeStruct((R, Cc), F32)] * 4, compiler_params=_params(("parallel",)))(parts, w, m, v)


def _lanes(vec, base):
    return jnp.pad(vec[None].astype(F32), ((0, 0), (base, LANES - base - vec.shape[0])))


def _col_shards(full_w):
    R, Ct = full_w.shape
    return jnp.transpose(full_w.reshape(R, N_DEV, Ct // N_DEV), (1, 0, 2))


def _regroup(sources, pieces_of):
    return jnp.concatenate([sources[k][:, a:a + w] for (k, a, w) in pieces_of], axis=1)


def _pieces(spans, lo, hi):
    out, pos = [], 0
    for (k, a, w) in spans:
        s, e = max(lo, pos), min(hi, pos + w)
        if s < e:
            out.append((k, a + s - pos, e - s))
        pos += w
    return out


def _from_col_shards(g):
    return jnp.transpose(g, (1, 0, 2)).reshape(g.shape[1], -1)


def kernel(x, mem, g_mix, w_in, conv_w, a_log, dt_bias, gdn_norm_g, fox_b_f, fox_q_norm, fox_k_norm, g_mem, w_mem_kv, mem_q_norm, mem_k_norm, w_up_gdn, w_up_fox, w_up_mem, w_out, g_mlp, w_ff1, w_ff2, loss_target, m_g_mix, m_w_in, m_conv_w, m_a_log, m_dt_bias, m_gdn_norm_g, m_fox_b_f, m_fox_q_norm, m_fox_k_norm, m_g_mem, m_w_mem_kv, m_mem_q_norm, m_mem_k_norm, m_w_up_gdn, m_w_up_fox, m_w_up_mem, m_w_out, m_g_mlp, m_w_ff1, m_w_ff2, v_g_mix, v_w_in, v_conv_w, v_a_log, v_dt_bias, v_gdn_norm_g, v_fox_b_f, v_fox_q_norm, v_fox_k_norm, v_g_mem, v_w_mem_kv, v_mem_q_norm, v_mem_k_norm, v_w_up_gdn, v_w_up_fox, v_w_up_mem, v_w_out, v_g_mlp, v_w_ff1, v_w_ff2):
    loc = dict(locals())
    big_names = ["w_in", "conv_w", "w_mem_kv", "w_up_gdn", "w_up_fox", "w_up_mem", "w_out", "w_ff1", "w_ff2"]
    col_sharded = {"w_in", "conv_w", "w_up_gdn", "w_up_fox", "w_up_mem", "w_ff1"}
    small_names = ["g_mix", "a_log", "dt_bias", "gdn_norm_g", "fox_b_f", "fox_q_norm", "fox_k_norm", "g_mem",
                   "mem_q_norm", "mem_k_norm", "g_mlp"]

    xs, tgt, mems = x[0], loss_target[0], mem[0]
    T, D = xs.shape
    HG = a_log.shape[1]
    HF = fox_b_f.shape[1]
    DM = mem_q_norm.shape[1]
    GQK, GV = HG * HEAD, HG * HEAD
    GQKV = 2 * GQK + GV
    FW = HF * HEAD
    MW = w_mem_kv.shape[2] // 2
    HM = MW // DM
    assert HG <= 8 and HF <= 8

    shard = {n: loc[n][0].astype(BF16) for n in big_names}
    first, rest, last = big_names[:2], big_names[2:-1], big_names[-1:]
    W = {}

    def take(names, gathered):
        for n, g in zip(names, gathered):
            if n in ("w_ff1", "w_in"):
                W[n] = g
            else:
                W[n] = _from_col_shards(g) if n in col_sharded else g.reshape(-1, g.shape[2])

    rms_fn = lambda t, g: (_rms(t, g),)
    h, *gathered = rowwise(rms_fn, [full(xs)], [g_mix], [(D, D, BF16)], name="rms_mix", comm=comm_gather([shard[n] for n in first]))
    take(first, gathered)
    widths = [GQKV, GV, HG, HG, FW, FW, FW, HF, MW, 3 * D]
    offs = np.concatenate([[0], np.cumsum(widths)]).tolist()
    cs = W["w_in"].shape[2]
    shard_spans = [(d, 0, cs) for d in range(N_DEV)]
    in_shards = [W["w_in"][d] for d in range(N_DEV)]
    seg = lambda i: _regroup(in_shards, _pieces(shard_spans, offs[i], offs[i + 1]))
    w_big = jnp.concatenate([seg(0), seg(1), seg(4), seg(5), seg(6), seg(8), seg(9)], axis=1)
    pad8 = lambda s: jnp.pad(s, ((0, 0), (0, 8 - s.shape[1])))
    w_small = jnp.concatenate([pad8(seg(2)), pad8(seg(3)), pad8(seg(7)), jnp.zeros((D, LANES - 24), BF16)], axis=1)
    o_z, o_fq, o_fk, o_fv = GQKV, GQKV + GV, GQKV + GV + FW, GQKV + GV + 2 * FW
    o_mq = o_fv + FW
    o_gt = o_mq + MW
    WB = o_gt + 3 * D
    conv_full = W["conv_w"].astype(F32)

    a_log_l, dt_bias_l, b_f_l = _lanes(a_log[0], LANE_A), _lanes(dt_bias[0], LANE_A), _lanes(fox_b_f[0], LANE_F)

    big, *gathered = matmul(h, w_big, mode="nn", name="proj_big", comm=comm_gather([shard[n] for n in rest]))
    take(rest, gathered)
    small = matmul(h, w_small, mode="nn", name="proj_small", out_dtypes=(F32,))
    bgf, = rowwise(_gates_fn, [full(small)], [a_log_l, dt_bias_l, b_f_l], [(LANES, LANES, F32)], name="gates")

    cn = conv_fwd(big, conv_full, HG, name="conv")
    o_gdn, ss, tinv = gdn_fwd(cn, bgf, HG, name="gdn_fwd")
    oa, = rowwise(_gated_norm_fn, [(o_gdn, HEAD, 0, 1), (big, HEAD, o_z // HEAD, 1)], [gdn_norm_g],
                  [(GV, HEAD, BF16)], name="gated_norm", ncb=HG)

    cum = cumsum_tokens(bgf, reverse=False, name="cumsum")
    fqa = fox_prep(big, o_fq, fox_q_norm, cum, 0, HF, name="fox_prep_q")
    fka = fox_prep(big, o_fk, fox_k_norm, cum, 1, HF, name="fox_prep_k")
    fva = fox_prep(big, o_fv, fox_k_norm, cum, 2, HF, name="fox_prep_v")
    live = fox_live_tiles(cum, fox_q_norm, fox_k_norm, HF, T)
    ob, lse = fox_fwd(fqa, fka, fva, live, HF, name="fox_fwd")

    memn, = rowwise(rms_fn, [full(mems)], [g_mem], [(D, D, BF16)], name="rms_mem")
    kv_m = matmul(memn, W["w_mem_kv"], mode="nn", name="mem_kv", out_dtypes=(F32,))
    kmn, = rowwise(rms_fn, [(kv_m, DM, 0, 1)], [mem_k_norm], [(MW, DM, F32)], name="mem_knorm", ncb=HM)
    vm = kv_m[:, MW:]
    mem_fn = functools.partial(_mem_attn_fn, HM, DM)
    om, = rowwise(mem_fn, [(big, MW, o_mq // MW, 0)], [kmn, vm, mem_q_norm], [(MW, MW, BF16)], name="mem_attn")

    ya = matmul(oa, W["w_up_gdn"], mode="nn", name="up_gdn")
    yb = matmul(ob, W["w_up_fox"], mode="nn", name="up_fox")
    ym = matmul(om, W["w_up_mem"], mode="nn", name="up_mem")
    cbm = min(512, D)
    gate_rows = [(big, cbm, (o_gt + b * D) // cbm, 1) for b in range(3)]
    merge_rows = [(ya, cbm, 0, 1), (yb, cbm, 0, 1), (ym, cbm, 0, 1)] + gate_rows
    y, = rowwise(_merge_fn, merge_rows, [], [(D, cbm, BF16)], name="merge", ncb=D // cbm)
    x1 = matmul(y, W["w_out"], mode="nn", name="out_proj", out_dtypes=(F32,), extras=(xs,),
                epi=lambda r, res: (r + res,))

    h2, = rowwise(rms_fn, [full(x1)], [g_mlp], [(D, D, BF16)], name="rms_mlp")
    u_ff, a_ff, *gathered = matmul(h2, W["w_ff1"], mode="nn", name="ff1", out_dtypes=(BF16, BF16), b_shards=True,
                                   epi=lambda r: (r, jnp.square(jnp.maximum(r, 0.0))),
                                   comm=comm_gather([shard[n] for n in last]))
    take(last, gathered)
    d_out = matmul(a_ff, W["w_ff2"], mode="nn", name="ff2_loss", out_dtypes=(F32,), extras=(x1, tgt),
                   epi=lambda r, res, t: ((r + res - t) * (1.0 / D),))
    loss_local = 0.5 * D * sum_squares(d_out, name="loss_sum")[0, 0]
    loss = lax.psum(loss_local, ("x", "y", "c"))

    G = {}
    d_u = matmul(d_out, W["w_ff2"], mode="nt", name="d_ff2_in", extras=(u_ff,),
                 epi=lambda r, u: (r * 2.0 * jnp.maximum(u.astype(F32), 0.0),))
    d_h2 = matmul(d_u, W["w_ff1"], mode="nt", name="d_ff1_in", b_shards=True)
    d_x1, G["g_mlp"] = rowwise_bwd(rms_fn, [full(x1)], [g_mlp], [full(d_h2)], [(0, F32)], [0], name="d_rms_mlp",
                                   adds=[full(d_out)])
    d_y = matmul(d_x1, W["w_out"], mode="nt", name="d_out_proj_in")
    G["w_out"] = matmul(y, d_x1, mode="tn", name="d_w_out")
    d_ya, d_yb, d_ym, d_ga, d_gb, d_gm = rowwise_bwd(
        _merge_fn, merge_rows, [], [(d_y, cbm, 0, 1)], [(k, BF16) for k in range(6)], [], name="d_merge", ncb=D // cbm)
    d_oa = matmul(d_ya, W["w_up_gdn"], mode="nt", name="d_up_gdn_in")
    d_ob = matmul(d_yb, W["w_up_fox"], mode="nt", name="d_up_fox_in")
    d_om = matmul(d_ym, W["w_up_mem"], mode="nt", name="d_up_mem_in")
    G["w_up_gdn"] = matmul(oa, d_ya, mode="tn", name="d_w_up_gdn")
    G["w_up_fox"] = matmul(ob, d_yb, mode="tn", name="d_w_up_fox")
    G["w_up_mem"] = matmul(om, d_ym, mode="tn", name="d_w_up_mem")

    d_mq, d_kmn, d_vm, G["mem_q_norm"] = rowwise_bwd(
        mem_fn, [(big, MW, o_mq // MW, 0)], [kmn, vm, mem_q_norm], [full(d_om)], [(0, BF16)], [0, 1, 2], name="d_mem_attn")
    d_km, G["mem_k_norm"] = rowwise_bwd(rms_fn, [(kv_m, DM, 0, 1)], [mem_k_norm], [(d_kmn, DM, 0, 1)], [(0, F32)], [0],
                                         name="d_mem_knorm", ncb=HM)
    d_kv_m = jnp.concatenate([d_km, d_vm], axis=1)
    G["w_mem_kv"] = matmul(memn, d_kv_m, mode="tn", name="d_w_mem_kv")
    d_memn = matmul(d_kv_m, W["w_mem_kv"], mode="nt", name="d_mem_kv_in")
    _, G["g_mem"] = rowwise_bwd(rms_fn, [full(mems)], [g_mem], [full(d_memn)], [(0, BF16)], [0], name="d_rms_mem")

    d_fkn, d_fv, d_cum_t = fox_bwd_kv(fqa, fka, fva, d_ob, ob, lse, live, HF, name="fox_bwd_kv")
    d_fqn = fox_bwd_q(fqa, fka, fva, d_ob, ob, lse, live, HF, name="fox_bwd_q")
    d_fq, G["fox_q_norm"] = rowwise_bwd(rms_fn, [(big, HEAD, o_fq // HEAD, 1)], [fox_q_norm], [(d_fqn, HEAD, 0, 1)],
                                         [(0, BF16)], [0], name="d_fox_qnorm", ncb=HF)
    d_fk, G["fox_k_norm"] = rowwise_bwd(rms_fn, [(big, HEAD, o_fk // HEAD, 1)], [fox_k_norm], [(d_fkn, HEAD, 0, 1)],
                                         [(0, BF16)], [0], name="d_fox_knorm", ncb=HF)
    d_cum = jnp.pad(d_cum_t[:HF].T, ((0, 0), (LANE_F, LANES - LANE_F - HF)))
    d_logf = cumsum_tokens(d_cum, reverse=True, name="cumsum_rev")

    d_o_gdn, d_z, G["gdn_norm_g"] = rowwise_bwd(
        _gated_norm_fn, [(o_gdn, HEAD, 0, 1), (big, HEAD, o_z // HEAD, 1)], [gdn_norm_g], [(d_oa, HEAD, 0, 1)],
        [(0, F32), (1, BF16)], [0], name="d_gated_norm", ncb=HG)
    d_cn, d_bg = gdn_bwd(cn, bgf, ss, tinv, d_o_gdn, HG, name="gdn_bwd")
    d_conv_y, G["conv_w"] = conv_bwd_taps(big, conv_full, d_cn, HG, name="d_conv_taps")
    d_qkv = conv_bwd_input(d_conv_y, conv_full, name="d_conv_in")
    d_small, d_al, d_dt, d_bf = rowwise_bwd(_gates_fn, [full(small)], [a_log_l, dt_bias_l, b_f_l], [full(d_bg + d_logf)],
                                            [(0, F32)], [0, 1, 2], name="d_gates")
    G["a_log"], G["dt_bias"], G["fox_b_f"] = (d_al[:, LANE_A:LANE_A + HG], d_dt[:, LANE_A:LANE_A + HG],
                                               d_bf[:, LANE_F:LANE_F + HF])

    def parts(n):
        g = G[n].astype(BF16)
        if g.ndim == 3:
            return g
        return _col_shards(g) if n in col_sharded else g.reshape(N_DEV, -1, g.shape[1])

    recv = {}

    def carried(names, out):
        for n, r in zip(names, out):
            recv[n] = r

    d_big = jnp.concatenate([d_qkv, d_z, d_fq, d_fk, d_fv, d_mq, d_ga, d_gb, d_gm], axis=1)
    group = ["conv_w", "w_mem_kv", "w_up_gdn", "w_up_fox", "w_up_mem", "w_out"]
    G["w_ff2"], *out = matmul(a_ff, d_out, mode="tn", name="d_w_ff2",
                              comm=comm_direct([parts(n) for n in group], [True] * len(group)))
    carried(group, out)
    G["w_ff1"], *out = matmul(h2, d_u, mode="tn", name="d_w_ff1", out_shards=True,
                              comm=comm_direct([parts("w_ff2")], [True]))
    carried(["w_ff2"], out)
    g_big, *out = matmul(h, d_big, mode="tn", name="d_w_big", comm=comm_direct([parts("w_ff1")], [True]))
    carried(["w_ff1"], out)
    g_small = matmul(h, d_small, mode="tn", name="d_w_small", out_dtypes=(F32,))
    grad_spans = [(0, 0, GQKV), (0, o_z, GV), (1, LANE_B, HG), (1, LANE_A, HG), (0, o_fq, FW), (0, o_fk, FW), (0, o_fv, FW),
                  (1, LANE_F, HF), (0, o_mq, MW), (0, o_gt, 3 * D)]
    grad_src = [g_big, g_small.astype(BF16)]
    G["w_in"] = jnp.stack([_regroup(grad_src, _pieces(grad_spans, d * cs, (d + 1) * cs)) for d in range(N_DEV)])
    p_in = parts("w_in")
    d_h_s, from_sibling = matmul(d_small, w_small, mode="nt", name="d_proj_small_in", out_dtypes=(F32,),
                                 comm=comm_to_sibling([p_in]))
    chip_sums = pair_sum(p_in, from_sibling, name="w_in_pair_sum")
    d_h, *out = matmul(d_big, w_big, mode="nt", name="d_proj_big_in", extras=(d_h_s,), epi=lambda r, e: (r + e,),
                       comm=comm_to_owner_chip([chip_sums]))
    carried(["w_in"], out)
    grad_x, G["g_mix"] = rowwise_bwd(rms_fn, [full(xs)], [g_mix], [full(d_h)], [(0, F32)], [0], name="d_rms_mix",
                                     adds=[full(d_x1)])
    grad_x = grad_x[None]

    small_sizes = [loc[n].shape[1] for n in small_names]
    pack = lambda d: jnp.concatenate([d[n].reshape(1, -1) for n in small_names], axis=1)
    npad = -sum(small_sizes) % LANES
    padp = lambda a: jnp.pad(a, ((0, 0), (0, npad)))
    recv_small, = run_comm(comm_direct([padp(pack(G))], [False]), name="gather_small_grads")

    res = {}
    for n in big_names:
        res[n] = [t[None] for t in adamw(recv[n], loc[n][0], loc["m_" + n][0], loc["v_" + n][0], name="adamw_" + n)]
    sm = adamw(recv_small, padp(pack({n: loc[n] for n in small_names})), padp(pack({n: loc["m_" + n] for n in small_names})),
               padp(pack({n: loc["v_" + n] for n in small_names})), name="adamw_small")
    so = np.concatenate([[0], np.cumsum(small_sizes)]).tolist()
    for i, n in enumerate(small_names):
        res[n] = [t[:, so[i]:so[i + 1]] for t in sm]

    order = ["g_mix", "w_in", "conv_w", "a_log", "dt_bias", "gdn_norm_g", "fox_b_f", "fox_q_norm", "fox_k_norm", "g_mem",
             "w_mem_kv", "mem_q_norm", "mem_k_norm", "w_up_gdn", "w_up_fox", "w_up_mem", "w_out", "g_mlp", "w_ff1", "w_ff2"]
    return (loss, grad_x, *[res[n][0] for n in order], *[res[n][1] for n in order],
            *[res[n][2] for n in order], *[res[n][3] for n in order])
```

```python
import functools

import jax
import jax.numpy as jnp
import numpy as np
from jax import lax
from jax.experimental import pallas as pl
from jax.experimental.pallas import tpu as pltpu

F32 = jnp.float32
BF16 = jnp.bfloat16
HI = lax.Precision.HIGHEST

EPS = 1e-6
GDN_CHUNK = 64
GDN_CONV = 4
HEAD = 128
LANES = 128
HALO = 16
N_DEV = 8
MESH = pl.DeviceIdType.MESH
VMEM_LIMIT_V7X = 56 * 1024 * 1024

ADAM_LR, ADAM_B1, ADAM_B2, ADAM_EPS, ADAM_WD, ADAM_STEP = 0.001, 0.9, 0.999, 1e-08, 0.01, 10

LANE_B, LANE_A, LANE_F = 0, 8, 16


def _params(sem):
    return pltpu.CompilerParams(dimension_semantics=sem, vmem_limit_bytes=VMEM_LIMIT_V7X)


def _dg(a, b, ca, cb, prec):
    nb = a.ndim - 2
    batch = tuple(range(nb))
    return lax.dot_general(a, b, (((ca + nb,), (cb + nb,)), (batch, batch)), precision=prec,
                           preferred_element_type=F32)


def _make_mm(prec, cast):
    def c(x):
        return x.astype(BF16) if cast else x

    @jax.custom_vjp
    def nn(a, b):
        return _dg(c(a), c(b), 1, 0, prec)

    @jax.custom_vjp
    def nt(a, b):
        return _dg(c(a), c(b), 1, 1, prec)

    @jax.custom_vjp
    def tn(a, b):
        return _dg(c(a), c(b), 0, 0, prec)

    nn.defvjp(lambda a, b: (nn(a, b), (a, b)), lambda r, g: (nt(g, r[1]), tn(r[0], g)))
    nt.defvjp(lambda a, b: (nt(a, b), (a, b)), lambda r, g: (nn(g, r[1]), tn(g, r[0])))
    tn.defvjp(lambda a, b: (tn(a, b), (a, b)), lambda r, g: (nt(r[1], g), nn(r[0], g)))
    return nn, nt, tn


NN, NT, TN = _make_mm(None, True)
NNH, NTH, TNH = _make_mm(lax.Precision.HIGH, False)
NNX, _, _ = _make_mm(HI, False)


def _sigmoid(x):
    return 1.0 / (1.0 + jnp.exp(-x))


def _silu(x):
    return x * _sigmoid(x)


def _softplus(x):
    return jnp.maximum(x, 0.0) + jnp.log(1.0 + jnp.exp(-jnp.abs(x)))


def _log_sigmoid(x):
    return -_softplus(-x)


def _rms(x, g):
    return x * lax.rsqrt(jnp.mean(x * x, axis=-1, keepdims=True) + EPS) * g


def _tile(n, target):
    t = target
    while t >= LANES:
        if n % t == 0:
            return t
        t //= 2
    return n


def matmul(a, b, *, mode, name, out_dtypes=(BF16,), epi=None, extras=(), tm=1024, tn=1024, tk=2048, comm=None,
           b_shards=False, out_shards=False):
    if b_shards:
        b_rows, b_cols = b.shape[1], N_DEV * b.shape[2]
    else:
        b_rows, b_cols = b.shape
    if mode == "nn":
        (M, K), (K2, N) = a.shape, (b_rows, b_cols)
    elif mode == "nt":
        (M, K), (N, K2) = a.shape, (b_rows, b_cols)
    else:
        (K, M), (K2, N) = a.shape, (b_rows, b_cols)
    assert K == K2, (name, a.shape, b.shape)
    tm, tn, tk = _tile(M, tm), _tile(N, tn), _tile(K, tk)
    if b_shards and mode == "nt":
        tk = K // N_DEV
    if (b_shards and mode != "nt") or out_shards:
        tn = N // N_DEV
    ni, nj, nk = M // tm, N // tn, K // tk
    a_spec = (pl.BlockSpec((tk, tm), lambda i, j, k: (k, i)) if mode == "tn"
              else pl.BlockSpec((tm, tk), lambda i, j, k: (i, k)))
    if b_shards:
        b_spec = (pl.BlockSpec((None, tn, tk), lambda i, j, k: (k, j, 0)) if mode == "nt"
                  else pl.BlockSpec((None, tk, tn), lambda i, j, k: (j, k, 0)))
    else:
        b_spec = (pl.BlockSpec((tn, tk), lambda i, j, k: (j, k)) if mode == "nt"
                  else pl.BlockSpec((tk, tn), lambda i, j, k: (k, j)))
    o_spec = pl.BlockSpec((tm, tn), lambda i, j, k: (i, j))
    w_spec = pl.BlockSpec((None, tm, tn), lambda i, j, k: (j, i, 0)) if out_shards else o_spec
    w_shape = (N_DEV, M, tn) if out_shards else (M, N)
    dims = {"nn": ((1,), (0,)), "nt": ((1,), (1,)), "tn": ((0,), (0,))}[mode]
    ne, no = len(extras), len(out_dtypes)
    nc = len(comm.arrays) if comm is not None else 0
    n_steps = ni * nj * nk

    def body(a_ref, b_ref, *rest):
        ex, c_in = rest[:ne], rest[ne:ne + nc]
        outs, c_out = rest[ne + nc:ne + nc + no], rest[ne + nc + no:ne + nc + no + nc]
        acc = rest[ne + nc + no + nc]
        c_sems = rest[ne + nc + no + nc + 1:]
        k = pl.program_id(2)
        step = (pl.program_id(0) * nj + pl.program_id(1)) * nk + k

        if comm is not None:
            pl.when(step == 0)(lambda: comm.start(c_in, c_out, c_sems))
            if comm.forward is not None and n_steps >= 3:
                pl.when(step == (3 * n_steps) // 4)(lambda: comm.forward(c_in, c_out, c_sems))

        @pl.when(k == 0)
        def _():
            acc[...] = jnp.zeros_like(acc)

        acc[...] += lax.dot_general(a_ref[...].astype(BF16), b_ref[...].astype(BF16), (dims, ((), ())),
                                    preferred_element_type=F32)

        @pl.when(k == nk - 1)
        def _():
            r = acc[...]
            vals = epi(r, *[e[...] for e in ex]) if epi is not None else (r,)
            for o, v in zip(outs, vals):
                o[...] = v.astype(o.dtype)

        if comm is not None:
            @pl.when(step == n_steps - 1)
            def _():
                if comm.forward is not None and n_steps < 3:
                    comm.forward(c_in, c_out, c_sems)
                comm.finish(c_in, c_out, c_sems)

    any_spec = pl.BlockSpec(memory_space=pl.ANY)
    sem = ("arbitrary",) * 3 if comm is not None else ("parallel", "parallel", "arbitrary")
    out = pl.pallas_call(
        body, name=name, grid=(ni, nj, nk),
        in_specs=[a_spec, b_spec] + [o_spec] * ne + [any_spec] * nc, out_specs=[w_spec] * no + [any_spec] * nc,
        out_shape=[jax.ShapeDtypeStruct(w_shape, d) for d in out_dtypes] + (comm.out_shape if comm is not None else []),
        scratch_shapes=[pltpu.VMEM((tm, tn), F32)] + (comm.sems if comm is not None else []),
        compiler_params=_params(sem))(a, b, *extras, *(comm.arrays if comm is not None else []))
    return out[0] if len(out) == 1 else out


def _row_spec(tr, cb, off, moves):
    return pl.BlockSpec((tr, cb), lambda i, j: (i, off + moves * j))


def _whole_spec(p):
    return pl.BlockSpec(p.shape, lambda i, j: (0,) * p.ndim)


def _row_tile(T, rows):
    widest = max(cb for (_, cb, _, _) in rows)
    return min(T, max(512, (1 << 19) // widest))


def rowwise(fn, rows, params, outs, *, name, ncb=1, tr=None, comm=None):
    T = rows[0][0].shape[0]
    tr = min(tr, T) if tr else _row_tile(T, rows)
    assert T % tr == 0
    nr, npar, no = len(rows), len(params), len(outs)
    nc = len(comm.arrays) if comm is not None else 0
    n_steps = (T // tr) * ncb

    def body(*refs):
        r, p, c_in = refs[:nr], refs[nr:nr + npar], refs[nr + npar:nr + npar + nc]
        o, c_out, c_sems = refs[nr + npar + nc:nr + npar + nc + no], refs[nr + npar + nc + no:nr + npar + 2 * nc + no], \
            refs[nr + npar + 2 * nc + no:]
        step = pl.program_id(0) * ncb + pl.program_id(1)
        if comm is not None:
            pl.when(step == 0)(lambda: comm.start(c_in, c_out, c_sems))
            if comm.forward is not None and n_steps >= 3:
                pl.when(step == (3 * n_steps) // 4)(lambda: comm.forward(c_in, c_out, c_sems))
        vals = fn(*[x[...].astype(F32) for x in r], *[x[...] for x in p])
        for oref, v in zip(o, vals):
            oref[...] = v.astype(oref.dtype)
        if comm is not None:
            @pl.when(step == n_steps - 1)
            def _():
                if comm.forward is not None and n_steps < 3:
                    comm.forward(c_in, c_out, c_sems)
                comm.finish(c_in, c_out, c_sems)

    any_spec = pl.BlockSpec(memory_space=pl.ANY)
    res = pl.pallas_call(
        body, name=name, grid=(T // tr, ncb),
        in_specs=([_row_spec(tr, cb, off, mv) for (_, cb, off, mv) in rows] + [_whole_spec(p) for p in params]
                  + [any_spec] * nc),
        out_specs=[_row_spec(tr, cb, 0, 1) for (_, cb, _) in outs] + [any_spec] * nc,
        out_shape=[jax.ShapeDtypeStruct((T, cols), d) for (cols, _, d) in outs] + (comm.out_shape if comm is not None else []),
        scratch_shapes=comm.sems if comm is not None else [],
        compiler_params=_params(("arbitrary", "arbitrary") if comm is not None else ("parallel", "parallel")))(
            *[r[0] for r in rows], *params, *(comm.arrays if comm is not None else []))
    return res


def rowwise_bwd(fn, rows, params, cots, drows, dparams, *, name, ncb=1, tr=None, adds=()):
    T = rows[0][0].shape[0]
    tr = min(tr, T) if tr else _row_tile(T, rows)
    assert T % tr == 0
    nr, npar, nc, ndr, na = len(rows), len(params), len(cots), len(drows), len(adds)

    def body(*refs):
        r, p, c = refs[:nr], refs[nr:nr + npar], refs[nr + npar:nr + npar + nc]
        base = nr + npar + nc + na
        ad, o_r, o_p = refs[base - na:base], refs[base:base + ndr], refs[base + ndr:]
        prim = [x[...].astype(F32) for x in r] + [x[...] for x in p]
        _, vjp = jax.vjp(lambda *a: tuple(fn(*a)), *prim)
        g = vjp(tuple(x[...].astype(F32) for x in c))
        for k, (oref, (idx, _)) in enumerate(zip(o_r, drows)):
            val = g[idx] + ad[k][...].astype(F32) if k < na else g[idx]
            oref[...] = val.astype(oref.dtype)
        first = jnp.logical_and(pl.program_id(0) == 0, pl.program_id(1) == 0)

        @pl.when(first)
        def _():
            for oref in o_p:
                oref[...] = jnp.zeros_like(oref)

        for oref, idx in zip(o_p, dparams):
            oref[...] += g[nr + idx]

    res = pl.pallas_call(
        body, name=name, grid=(T // tr, ncb),
        in_specs=([_row_spec(tr, cb, off, mv) for (_, cb, off, mv) in rows] + [_whole_spec(p) for p in params]
                  + [_row_spec(tr, cb, off, mv) for (_, cb, off, mv) in tuple(cots) + tuple(adds)]),
        out_specs=([_row_spec(tr, rows[idx][1], 0, 1) for (idx, _) in drows]
                   + [_whole_spec(params[idx]) for idx in dparams]),
        out_shape=([jax.ShapeDtypeStruct((T, ncb * rows[idx][1] if rows[idx][3] else rows[idx][1]), d)
                    for (idx, d) in drows]
                   + [jax.ShapeDtypeStruct(params[idx].shape, F32) for idx in dparams]),
        compiler_params=_params(("arbitrary", "arbitrary")))(
            *[r[0] for r in rows], *params, *[c[0] for c in cots], *[a[0] for a in adds])
    return res


def full(a):
    return (a, a.shape[1], 0, 0)


def cumsum_tokens(x, *, reverse, name, tb=256):
    T = x.shape[0]
    tb = min(tb, T)
    nb = T // tb
    idx = (lambda i: (nb - 1 - i, 0)) if reverse else (lambda i: (i, 0))

    def body(x_ref, o_ref, carry):
        @pl.when(pl.program_id(0) == 0)
        def _():
            carry[...] = jnp.zeros_like(carry)

        ii = lax.broadcasted_iota(jnp.int32, (tb, tb), 0)
        jj = lax.broadcasted_iota(jnp.int32, (tb, tb), 1)
        tri = ((ii <= jj) if reverse else (ii >= jj)).astype(F32)
        c = lax.dot_general(tri, x_ref[...], (((1,), (0,)), ((), ())), precision=HI,
                            preferred_element_type=F32) + carry[0:1, :]
        o_ref[...] = c
        carry[0:1, :] = c[0:1, :] if reverse else c[tb - 1:tb, :]

    return pl.pallas_call(
        body, name=name, grid=(nb,), in_specs=[pl.BlockSpec((tb, LANES), idx)],
        out_specs=pl.BlockSpec((tb, LANES), idx), out_shape=jax.ShapeDtypeStruct((T, LANES), F32),
        scratch_shapes=[pltpu.VMEM((8, LANES), F32)], compiler_params=_params(("arbitrary",)))(x)


def _conv_post(y, kind, dk):
    c = _silu(y)
    r = lax.rsqrt(jnp.sum(c * c, axis=-1, keepdims=True) + EPS)
    return jnp.where(kind == 0, c * r * (dk ** -0.5), jnp.where(kind == 1, c * r, c))


def _conv_taps(cur, prev, w, tr):
    ext = jnp.concatenate([prev, cur], axis=0)
    y = w[3:4, :] * cur
    for d in (1, 2, 3):
        y = y + w[3 - d:4 - d, :] * pltpu.roll(ext, d, 0)[HALO:HALO + tr]
    return y


def conv_fwd(big, w, n_qk_heads, *, name, tr=2048):
    T, W = big.shape[0], w.shape[1]
    tr = min(tr, T)
    nh = W // HEAD

    def body(cur_ref, prev_ref, w_ref, o_ref):
        i, j = pl.program_id(0), pl.program_id(1)
        prev = jnp.where(i > 0, prev_ref[...].astype(F32), 0.0)
        y = _conv_taps(cur_ref[...].astype(F32), prev, w_ref[...], tr)
        kind = jnp.where(j < n_qk_heads, 0, jnp.where(j < 2 * n_qk_heads, 1, 2))
        o_ref[...] = _conv_post(y, kind, HEAD)

    return pl.pallas_call(
        body, name=name, grid=(T // tr, nh),
        in_specs=[pl.BlockSpec((tr, HEAD), lambda i, j: (i, j)),
                  pl.BlockSpec((HALO, HEAD), lambda i, j: (jnp.maximum(i * (tr // HALO) - 1, 0), j)),
                  pl.BlockSpec((GDN_CONV, HEAD), lambda i, j: (0, j))],
        out_specs=pl.BlockSpec((tr, HEAD), lambda i, j: (i, j)),
        out_shape=jax.ShapeDtypeStruct((T, W), F32), compiler_params=_params(("parallel", "parallel")))(big, big, w)


def conv_bwd_taps(big, w, dcn, n_qk_heads, *, name, tr=2048):
    T, W = big.shape[0], w.shape[1]
    tr = min(tr, T)
    nh = W // HEAD

    def body(cur_ref, prev_ref, w_ref, g_ref, dy_ref, dw_ref):
        j, i = pl.program_id(0), pl.program_id(1)
        cur = cur_ref[...].astype(F32)
        prev = jnp.where(i > 0, prev_ref[...].astype(F32), 0.0)
        y = _conv_taps(cur, prev, w_ref[...], tr)
        kind = jnp.where(j < n_qk_heads, 0, jnp.where(j < 2 * n_qk_heads, 1, 2))
        _, vjp = jax.vjp(lambda t: _conv_post(t, kind, HEAD), y)
        dy, = vjp(g_ref[...])
        dy_ref[...] = dy
        ext = jnp.concatenate([prev, cur], axis=0)
        rows = [jnp.sum(dy * (cur if d == 0 else pltpu.roll(ext, d, 0)[HALO:HALO + tr]), axis=0, keepdims=True)
                for d in (3, 2, 1, 0)]

        @pl.when(i == 0)
        def _():
            dw_ref[...] = jnp.zeros_like(dw_ref)

        dw_ref[...] += jnp.concatenate(rows, axis=0)

    return pl.pallas_call(
        body, name=name, grid=(nh, T // tr),
        in_specs=[pl.BlockSpec((tr, HEAD), lambda j, i: (i, j)),
                  pl.BlockSpec((HALO, HEAD), lambda j, i: (jnp.maximum(i * (tr // HALO) - 1, 0), j)),
                  pl.BlockSpec((GDN_CONV, HEAD), lambda j, i: (0, j)),
                  pl.BlockSpec((tr, HEAD), lambda j, i: (i, j))],
        out_specs=[pl.BlockSpec((tr, HEAD), lambda j, i: (i, j)), pl.BlockSpec((GDN_CONV, HEAD), lambda j, i: (0, j))],
        out_shape=[jax.ShapeDtypeStruct((T, W), F32), jax.ShapeDtypeStruct((GDN_CONV, W), F32)],
        compiler_params=_params(("parallel", "arbitrary")))(big, big, w, dcn)


def conv_bwd_input(dy, w, *, name, tr=2048):
    T, W = dy.shape
    tr = min(tr, T)
    nrow = T // tr

    def body(cur_ref, nxt_ref, w_ref, o_ref):
        i = pl.program_id(0)
        cur = cur_ref[...]
        nxt = jnp.where(i < nrow - 1, nxt_ref[...], 0.0)
        ext = jnp.concatenate([cur, nxt], axis=0)
        w = w_ref[...]
        dx = w[3:4, :] * cur
        for d in (1, 2, 3):
            dx = dx + w[3 - d:4 - d, :] * pltpu.roll(ext, tr + HALO - d, 0)[0:tr]
        o_ref[...] = dx.astype(o_ref.dtype)

    return pl.pallas_call(
        body, name=name, grid=(nrow, W // HEAD),
        in_specs=[pl.BlockSpec((tr, HEAD), lambda i, j: (i, j)),
                  pl.BlockSpec((HALO, HEAD), lambda i, j: (jnp.minimum((i + 1) * (tr // HALO), T // HALO - 1), j)),
                  pl.BlockSpec((GDN_CONV, HEAD), lambda i, j: (0, j))],
        out_specs=pl.BlockSpec((tr, HEAD), lambda i, j: (i, j)),
        out_shape=jax.ShapeDtypeStruct((T, W), BF16), compiler_params=_params(("parallel", "parallel")))(dy, dy, w)


INV_BLOCK = 16


def _unit_lower_inverse_value(p):
    C = p.shape[-1]
    ii = lax.broadcasted_iota(jnp.int32, (C, C), 0)
    jj = lax.broadcasted_iota(jnp.int32, (C, C), 1)
    eye = jnp.where((ii == jj)[None], 1.0, 0.0)
    same = ((ii // INV_BLOCK) == (jj // INV_BLOCK))[None]
    pd = jnp.where(same, p, 0.0)
    d_inv = eye + pd
    n = 2
    while n < INV_BLOCK:
        pd = NNH(pd, pd)
        d_inv = d_inv + NNH(d_inv, pd)
        n *= 2
    nb = NNH(d_inv, jnp.where(same, 0.0, p))
    t = eye + nb
    n = 2
    while n < C // INV_BLOCK:
        nb = NNH(nb, nb)
        t = t + NNH(t, nb)
        n *= 2
    return NNH(t, d_inv)


@jax.custom_vjp
def _unit_lower_inverse(p, t_known):
    return _unit_lower_inverse_value(p) if t_known is None else t_known


def _unit_lower_inverse_fwd(p, t_known):
    t = _unit_lower_inverse(p, t_known)
    return t, (t, t_known is not None)


def _unit_lower_inverse_bwd(res, g):
    t, had = res
    return NTH(TNH(t, g), t), (jnp.zeros_like(t) if had else None)


_unit_lower_inverse.defvjp(_unit_lower_inverse_fwd, _unit_lower_inverse_bwd)


def _gdn_chunk(q, k, v, bg, S, t_known=None):
    H, C = q.shape[0], q.shape[1]
    ii = lax.broadcasted_iota(jnp.int32, (C, C), 0)
    jj = lax.broadcasted_iota(jnp.int32, (C, C), 1)
    lincl = (ii >= jj).astype(F32)
    strict, incl, eye = (ii > jj)[None], (ii >= jj)[None], (ii == jj)[None]
    gam2d = lax.dot_general(lincl, bg, (((1,), (0,)), ((), ())), precision=HI, preferred_element_type=F32)
    lane = lax.broadcasted_iota(jnp.int32, (H, 1, LANES), 2)
    hh = lax.broadcasted_iota(jnp.int32, (H, 1, LANES), 0)
    beta = jnp.sum(bg[None] * (lane == hh + LANE_B).astype(F32), axis=2, keepdims=True)
    gam = jnp.sum(gam2d[None] * (lane == hh + LANE_A).astype(F32), axis=2, keepdims=True)
    last = (lax.broadcasted_iota(jnp.int32, (1, C, 1), 1) == C - 1).astype(F32)
    gam_last = jnp.sum(gam * last, axis=1, keepdims=True)
    gam_row = NNX(jnp.ones((H, C, C), F32), jnp.where(eye, gam, 0.0))
    diff = gam - gam_row
    dec_s = jnp.where(strict, jnp.exp(jnp.where(strict, diff, 0.0)), 0.0)
    dec_i = jnp.where(incl, jnp.exp(jnp.where(incl, diff, 0.0)), 0.0)
    t = _unit_lower_inverse(-(beta * NT(k, k) * dec_s), t_known)
    eg = jnp.exp(gam)
    wu = NNH(t, jnp.concatenate([beta * eg * k, beta * v], axis=-1))
    w, u0 = wu[..., :HEAD], wu[..., HEAD:]
    qk = NT(q, k) * dec_i
    u = u0 - NN(w, S)
    o = NN(jnp.concatenate([q * eg, qk], axis=-1), jnp.concatenate([S, u], axis=-2))
    S2 = jnp.exp(gam_last) * S + TN(k * jnp.exp(gam_last - gam), u)
    return o, S2, t


def _heads(x, base, H):
    return jnp.stack([x[:, base + h * HEAD:base + (h + 1) * HEAD] for h in range(H)])


def _unheads(x):
    return jnp.concatenate([x[h] for h in range(x.shape[0])], axis=1)


GDN_CHUNKS_PER_STEP = 8


def gdn_fwd(cn, bgf, H, *, name):
    T, C, W = cn.shape[0], GDN_CHUNK, H * HEAD
    N = T // C
    K = GDN_CHUNKS_PER_STEP if N % GDN_CHUNKS_PER_STEP == 0 else 1

    def body(cn_ref, bg_ref, o_ref, ss_ref, t_ref, s_scr):
        @pl.when(pl.program_id(0) == 0)
        def _():
            s_scr[...] = jnp.zeros_like(s_scr)

        S = s_scr[...]
        for c in range(K):
            rows = slice(c * C, (c + 1) * C)
            x = cn_ref[rows, :]
            ss_ref[c] = S
            o, S, t = _gdn_chunk(_heads(x, 0, H), _heads(x, W, H), _heads(x, 2 * W, H), bg_ref[rows, :], S)
            o_ref[rows, :] = _unheads(o)
            t_ref[c] = t
        s_scr[...] = S

    return pl.pallas_call(
        body, name=name, grid=(N // K,),
        in_specs=[pl.BlockSpec((K * C, 3 * W), lambda n: (n, 0)), pl.BlockSpec((K * C, LANES), lambda n: (n, 0))],
        out_specs=[pl.BlockSpec((K * C, W), lambda n: (n, 0)), pl.BlockSpec((K, H, HEAD, HEAD), lambda n: (n, 0, 0, 0)),
                   pl.BlockSpec((K, H, C, C), lambda n: (n, 0, 0, 0))],
        out_shape=[jax.ShapeDtypeStruct((T, W), F32), jax.ShapeDtypeStruct((N, H, HEAD, HEAD), F32),
                   jax.ShapeDtypeStruct((N, H, C, C), F32)],
        scratch_shapes=[pltpu.VMEM((H, HEAD, HEAD), F32)], compiler_params=_params(("arbitrary",)))(cn, bgf)


def gdn_bwd(cn, bgf, ss, tinv, do, H, *, name):
    T, C, W = cn.shape[0], GDN_CHUNK, H * HEAD
    N = T // C
    K = GDN_CHUNKS_PER_STEP if N % GDN_CHUNKS_PER_STEP == 0 else 1
    NS = N // K

    def body(cn_ref, bg_ref, ss_ref, t_ref, do_ref, dcn_ref, dbg_ref, ds_scr):
        @pl.when(pl.program_id(0) == 0)
        def _():
            ds_scr[...] = jnp.zeros_like(ds_scr)

        dS = ds_scr[...]
        for c in reversed(range(K)):
            rows = slice(c * C, (c + 1) * C)
            x = cn_ref[rows, :]
            t_known = t_ref[c]
            _, vjp = jax.vjp(lambda *a: _gdn_chunk(*a, t_known)[:2],
                             _heads(x, 0, H), _heads(x, W, H), _heads(x, 2 * W, H), bg_ref[rows, :], ss_ref[c])
            dq, dk, dv, dbg, dS = vjp((_heads(do_ref[rows, :], 0, H), dS))
            dcn_ref[rows, :] = jnp.concatenate([_unheads(dq), _unheads(dk), _unheads(dv)], axis=1)
            dbg_ref[rows, :] = dbg
        ds_scr[...] = dS

    rev = lambda n: (NS - 1 - n, 0)
    rev4 = lambda n: (NS - 1 - n, 0, 0, 0)
    return pl.pallas_call(
        body, name=name, grid=(NS,),
        in_specs=[pl.BlockSpec((K * C, 3 * W), rev), pl.BlockSpec((K * C, LANES), rev),
                  pl.BlockSpec((K, H, HEAD, HEAD), rev4), pl.BlockSpec((K, H, C, C), rev4), pl.BlockSpec((K * C, W), rev)],
        out_specs=[pl.BlockSpec((K * C, 3 * W), rev), pl.BlockSpec((K * C, LANES), rev)],
        out_shape=[jax.ShapeDtypeStruct((T, 3 * W), F32), jax.ShapeDtypeStruct((T, LANES), F32)],
        scratch_shapes=[pltpu.VMEM((H, HEAD, HEAD), F32)], compiler_params=_params(("arbitrary",)))(cn, bgf, ss, tinv, do)


AUG = 2 * HEAD


def fox_prep(big, col_off, gain, cum, kind, H, *, name, tr=2048):
    T = big.shape[0]
    tr = min(tr, T)

    def body(x_ref, g_ref, c_ref, o_ref):
        h = pl.program_id(1)
        x = x_ref[...].astype(F32)
        lane = lax.broadcasted_iota(jnp.int32, (tr, HEAD), 1)
        if kind == 2:
            main, aug = x, jnp.ones((tr, HEAD), F32)
        else:
            main = _rms(x, g_ref[...]) * ((HEAD ** -0.5) if kind == 0 else 1.0)
            c = jnp.sum(jnp.where(lane == LANE_F + h, c_ref[...], 0.0), axis=1, keepdims=True)
            hi = c.astype(BF16).astype(F32)
            mid = (c - hi).astype(BF16).astype(F32)
            lo = c - hi - mid
            if kind == 0:
                aug = jnp.where(lane == 0, hi, jnp.where(lane == 1, mid, jnp.where(lane == 2, lo,
                                                                                   jnp.where(lane < 6, 1.0, 0.0))))
            else:
                aug = jnp.where(lane < 3, 1.0, jnp.where(lane == 3, -hi, jnp.where(lane == 4, -mid,
                                                                                   jnp.where(lane == 5, -lo, 0.0))))
        o_ref[...] = jnp.concatenate([main, aug], axis=1).astype(o_ref.dtype)

    return pl.pallas_call(
        body, name=name, grid=(T // tr, H),
        in_specs=[pl.BlockSpec((tr, HEAD), lambda i, h: (i, col_off // HEAD + h)),
                  pl.BlockSpec((1, HEAD), lambda i, h: (0, 0)), pl.BlockSpec((tr, LANES), lambda i, h: (i, 0))],
        out_specs=pl.BlockSpec((tr, AUG), lambda i, h: (i, h)),
        out_shape=jax.ShapeDtypeStruct((T, H * AUG), BF16), compiler_params=_params(("parallel", "parallel")))(big, gain, cum)


def _fox_logits(q_ref, k_ref, h, tq, tk, diagonal):
    ha = slice(h * AUG, (h + 1) * AUG)
    s = lax.dot_general(q_ref[:, ha], k_ref[:, ha], (((1,), (1,)), ((), ())), preferred_element_type=F32)
    keep = None
    if diagonal:
        keep = lax.broadcasted_iota(jnp.int32, (tq, tk), 0) >= lax.broadcasted_iota(jnp.int32, (tq, tk), 1)
    return s, keep


def _dispatch(live_ref, H, i, j, below, on_diagonal, head_fn, after=None, straight_from=None):
    straight_from = H if straight_from is None else straight_from
    def straight(diagonal):
        for h in range(H):
            head_fn(h, diagonal)
        if after is not None:
            after()

    def by_head():
        for h in range(H):
            pl.when(live_ref[h, i, j] != 0)(functools.partial(head_fn, h, False))
        if after is not None:
            after()

    n_live = live_ref[H, i, j]
    pl.when(jnp.logical_and(below, n_live >= straight_from))(functools.partial(straight, False))
    pl.when(jnp.logical_and(below, jnp.logical_and(n_live > 0, n_live < straight_from)))(by_head)
    pl.when(on_diagonal)(functools.partial(straight, True))


FOX_TILE = 512
EXP_UNDERFLOW = -100.0


def fox_live_tiles(cum, q_gain, k_gain, H, T):
    t = min(FOX_TILE, T)
    n = T // t
    c = cum[:, LANE_F:LANE_F + H]
    bias = c[0::t][:, None, :] - c[t - 1::t][None, :, :]
    bound = 1.02 * (HEAD ** 0.5) * jnp.max(jnp.abs(q_gain)) * jnp.max(jnp.abs(k_gain))
    causal = (jnp.arange(n)[:, None] >= jnp.arange(n)[None, :])[:, :, None]
    live = jnp.logical_and(causal, 2.0 * bound + bias >= EXP_UNDERFLOW)
    live = live.astype(jnp.int32)
    return jnp.concatenate([jnp.transpose(live, (2, 0, 1)), jnp.sum(live, axis=2)[None]], axis=0)


def fox_fwd(qa, ka, va, live, H, *, name):
    T = qa.shape[0]
    tq = tk = min(FOX_TILE, T)
    nq, nk = T // tq, T // tk

    def body(live_ref, q_ref, k_ref, v_ref, o_ref, lse_ref, acc, m_scr):
        i, j = pl.program_id(0), pl.program_id(1)

        @pl.when(j == 0)
        def _():
            m_scr[...] = jnp.full_like(m_scr, -jnp.inf)
            acc[...] = jnp.zeros_like(acc)

        def head(h, diagonal):
            ha = slice(h * AUG, (h + 1) * AUG)
            s, keep = _fox_logits(q_ref, k_ref, h, tq, tk, diagonal)
            if diagonal:
                s = jnp.where(keep, s, -jnp.inf)
            m_prev = m_scr[h]
            m_new = jnp.maximum(m_prev, jnp.max(s, axis=1, keepdims=True))
            p = jnp.exp(s - m_new[:, 0:1])
            p_hi = p.astype(BF16)
            p_lo = (p - p_hi.astype(F32)).astype(BF16)
            pv = lambda t: lax.dot_general(t, v_ref[:, ha], (((1,), (0,)), ((), ())), preferred_element_type=F32)
            acc[:, ha] = jnp.exp(m_prev[:, 0:1] - m_new[:, 0:1]) * acc[:, ha] + (pv(p_hi) + pv(p_lo))
            m_scr[h] = m_new

        _dispatch(live_ref, H, i, j, j < i, j == i, head, straight_from=(5 * H + 7) // 8)

        @pl.when(j == nk - 1)
        def _():
            lane = lax.broadcasted_iota(jnp.int32, (tq, LANES), 1)
            lse = jnp.zeros((tq, LANES), F32)
            for h in range(H):
                den = acc[:, h * AUG + HEAD:(h + 1) * AUG]
                o_ref[:, h * HEAD:(h + 1) * HEAD] = acc[:, h * AUG:h * AUG + HEAD] / den
                lse = jnp.where(lane == h, m_scr[h] + jnp.log(den), lse)
            lse_ref[...] = lse

    kv_idx = lambda i, j, f: (jnp.minimum(j, i), 0)
    q_idx = lambda i, j, f: (i, 0)
    return pl.pallas_call(
        body, name=name,
        grid_spec=pltpu.PrefetchScalarGridSpec(
            num_scalar_prefetch=1, grid=(nq, nk),
            in_specs=[pl.BlockSpec((tq, H * AUG), q_idx), pl.BlockSpec((tk, H * AUG), kv_idx), pl.BlockSpec((tk, H * AUG), kv_idx)],
            out_specs=[pl.BlockSpec((tq, H * HEAD), q_idx), pl.BlockSpec((tq, LANES), q_idx)],
            scratch_shapes=[pltpu.VMEM((tq, H * AUG), F32), pltpu.VMEM((H, tq, LANES), F32)]),
        out_shape=[jax.ShapeDtypeStruct((T, H * HEAD), F32), jax.ShapeDtypeStruct((T, LANES), F32)],
        compiler_params=_params(("parallel", "arbitrary")))(live, qa, ka, va)


def _fox_ds(q_ref, k_ref, v_ref, do_ref, o_ref, lse_ref, h, tq, tk, diagonal):
    hs = slice(h * HEAD, (h + 1) * HEAD)
    s, keep = _fox_logits(q_ref, k_ref, h, tq, tk, diagonal)
    p = jnp.exp(s - lse_ref[:, h:h + 1])
    if diagonal:
        p = jnp.where(keep, p, 0.0)
    do = do_ref[:, hs]
    dp = lax.dot_general(do, v_ref[:, h * AUG:h * AUG + HEAD], (((1,), (1,)), ((), ())), preferred_element_type=F32)
    delta = jnp.sum(do.astype(F32) * o_ref[:, hs], axis=1, keepdims=True)
    return p, p * (dp - delta)


def fox_bwd_kv(qa, ka, va, do, o, lse, live, H, *, name):
    T, W = do.shape
    tq = tk = min(FOX_TILE, T)
    nq, nk = T // tq, T // tk

    def body(live_ref, q_ref, k_ref, v_ref, do_ref, o_ref, lse_ref, dk_ref, dv_ref, dc_ref, dk_acc, dv_acc, dc_acc):
        j, i = pl.program_id(0), pl.program_id(1)

        @pl.when(i == 0)
        def _():
            dk_acc[...] = jnp.zeros_like(dk_acc)
            dv_acc[...] = jnp.zeros_like(dv_acc)
            dc_acc[...] = jnp.zeros_like(dc_acc)

        def head(h, diagonal):
            hs = slice(h * HEAD, (h + 1) * HEAD)
            p, ds = _fox_ds(q_ref, k_ref, v_ref, do_ref, o_ref, lse_ref, h, tq, tk, diagonal)
            dv_acc[:, hs] += lax.dot_general(p.astype(BF16), do_ref[:, hs], (((0,), (0,)), ((), ())),
                                             preferred_element_type=F32)
            dk_acc[:, hs] += lax.dot_general(ds.astype(BF16), q_ref[:, h * AUG:h * AUG + HEAD],
                                             (((0,), (0,)), ((), ())), preferred_element_type=F32)
            dc_acc[h] -= jnp.broadcast_to(jnp.sum(ds, axis=0, keepdims=True), (8, tk))

        _dispatch(live_ref, H, i, j, i > j, i == j, head)

        @pl.when(i == nq - 1)
        def _():
            dk_ref[...] = dk_acc[...].astype(dk_ref.dtype)
            dv_ref[...] = dv_acc[...].astype(dv_ref.dtype)
            row = lax.broadcasted_iota(jnp.int32, (8, tk), 0)
            dc = jnp.zeros((8, tk), F32)
            for h in range(H):
                dc = jnp.where(row == h, dc_acc[h], dc)
            dc_ref[...] = dc

    q_idx = lambda j, i, f: (jnp.maximum(i, j), 0)
    kv_idx = lambda j, i, f: (j, 0)
    return pl.pallas_call(
        body, name=name,
        grid_spec=pltpu.PrefetchScalarGridSpec(
            num_scalar_prefetch=1, grid=(nk, nq),
            in_specs=[pl.BlockSpec((tq, H * AUG), q_idx), pl.BlockSpec((tk, H * AUG), kv_idx), pl.BlockSpec((tk, H * AUG), kv_idx),
                      pl.BlockSpec((tq, W), q_idx), pl.BlockSpec((tq, W), q_idx), pl.BlockSpec((tq, LANES), q_idx)],
            out_specs=[pl.BlockSpec((tk, W), kv_idx), pl.BlockSpec((tk, W), kv_idx),
                       pl.BlockSpec((8, tk), lambda j, i, f: (0, j))],
            scratch_shapes=[pltpu.VMEM((tk, W), F32), pltpu.VMEM((tk, W), F32), pltpu.VMEM((H, 8, tk), F32)]),
        out_shape=[jax.ShapeDtypeStruct((T, W), BF16), jax.ShapeDtypeStruct((T, W), BF16), jax.ShapeDtypeStruct((8, T), F32)],
        compiler_params=_params(("parallel", "arbitrary")))(live, qa, ka, va, do, o, lse)


def fox_bwd_q(qa, ka, va, do, o, lse, live, H, *, name):
    T, W = do.shape
    tq = tk = min(FOX_TILE, T)
    nq, nk = T // tq, T // tk

    def body(live_ref, q_ref, k_ref, v_ref, do_ref, o_ref, lse_ref, dq_ref, dq_acc):
        i, j = pl.program_id(0), pl.program_id(1)

        @pl.when(j == 0)
        def _():
            dq_acc[...] = jnp.zeros_like(dq_acc)

        def head(h, diagonal):
            hs = slice(h * HEAD, (h + 1) * HEAD)
            _, ds = _fox_ds(q_ref, k_ref, v_ref, do_ref, o_ref, lse_ref, h, tq, tk, diagonal)
            dq_acc[:, hs] += lax.dot_general(ds.astype(BF16), k_ref[:, h * AUG:h * AUG + HEAD],
                                             (((1,), (0,)), ((), ())), preferred_element_type=F32)

        _dispatch(live_ref, H, i, j, j < i, j == i, head)

        @pl.when(j == nk - 1)
        def _():
            dq_ref[...] = (dq_acc[...] * (HEAD ** -0.5)).astype(dq_ref.dtype)

    q_idx = lambda i, j, f: (i, 0)
    kv_idx = lambda i, j, f: (jnp.minimum(j, i), 0)
    return pl.pallas_call(
        body, name=name,
        grid_spec=pltpu.PrefetchScalarGridSpec(
            num_scalar_prefetch=1, grid=(nq, nk),
            in_specs=[pl.BlockSpec((tq, H * AUG), q_idx), pl.BlockSpec((tk, H * AUG), kv_idx), pl.BlockSpec((tk, H * AUG), kv_idx),
                      pl.BlockSpec((tq, W), q_idx), pl.BlockSpec((tq, W), q_idx), pl.BlockSpec((tq, LANES), q_idx)],
            out_specs=pl.BlockSpec((tq, W), q_idx),
            scratch_shapes=[pltpu.VMEM((tq, W), F32)]),
        out_shape=jax.ShapeDtypeStruct((T, W), BF16),
        compiler_params=_params(("parallel", "arbitrary")))(live, qa, ka, va, do, o, lse)


def _gates_fn(small, a_log_l, dt_bias_l, b_f_l):
    lane = lax.broadcasted_iota(jnp.int32, small.shape, 1)
    beta = _sigmoid(small)
    g = -jnp.exp(a_log_l) * _softplus(small + dt_bias_l)
    lf = _log_sigmoid(small + b_f_l)
    return (jnp.where(lane < LANE_A, beta, jnp.where(lane < LANE_F, g, jnp.where(lane < LANE_F + 8, lf, 0.0))),)


def _gated_norm_fn(o, z, g):
    return (_rms(o, g) * _silu(z),)


def _merge_fn(ya, yb, ym, ga, gb, gm):
    return (_sigmoid(ga) * ya + _sigmoid(gb) * yb + _sigmoid(gm) * ym,)


def _mem_attn_fn(nh, dh, mq, kn, v, gq):
    outs = []
    for h in range(nh):
        hs = slice(h * dh, (h + 1) * dh)
        qn = _rms(mq[:, hs], gq)
        s = NT(qn, kn[:, hs]) * (dh ** -0.5)
        e = jnp.exp(s - jnp.max(s, axis=1, keepdims=True))
        p = e / jnp.sum(e, axis=1, keepdims=True)
        outs.append(NN(p, v[:, hs]))
    return (jnp.concatenate(outs, axis=1),)


def sum_squares(x, *, name, tr=512):
    T, D = x.shape
    tr = min(tr, T)

    def body(x_ref, o_ref):
        @pl.when(pl.program_id(0) == 0)
        def _():
            o_ref[...] = jnp.zeros_like(o_ref)

        v = x_ref[...]
        o_ref[...] += jnp.sum(jnp.sum(v * v, axis=1, keepdims=True), axis=0, keepdims=True)

    return pl.pallas_call(
        body, name=name, grid=(T // tr,), in_specs=[pl.BlockSpec((tr, D), lambda i: (i, 0))],
        out_specs=pl.BlockSpec((1, LANES), lambda i: (0, 0)), out_shape=jax.ShapeDtypeStruct((1, LANES), F32),
        compiler_params=_params(("arbitrary",)))(x)


class Comm:
    def __init__(self, arrays, out_shape, sems, start, forward, finish):
        self.arrays, self.out_shape, self.sems = list(arrays), list(out_shape), list(sems)
        self.start, self.forward, self.finish = start, forward, finish


def _place():
    x, y, c = lax.axis_index("x"), lax.axis_index("y"), lax.axis_index("c")
    chips = [(1 - x, y), (x, 1 - y), (1 - x, 1 - y)]
    return x, y, c, chips


def comm_gather(arrays):
    n = len(arrays)
    lin = lambda px, py, pc: 4 * px + 2 * py + pc

    def copy(ins, outs, sems, a, k, block, to, src=None):
        slot = outs[a].at[lin(*block)]
        return pltpu.make_async_remote_copy(src_ref=slot if src is None else src, dst_ref=slot, send_sem=sems[0].at[a, k],
                                            recv_sem=sems[1].at[a, k], device_id=to, device_id_type=MESH)

    def local(ins, outs, sems, a):
        x, y, c, _ = _place()
        return pltpu.make_async_copy(ins[a], outs[a].at[lin(x, y, c)], sems[2].at[a])

    def start(ins, outs, sems):
        x, y, c, chips = _place()
        for a in range(n):
            local(ins, outs, sems, a).start()
        for a in range(n):
            for j, chip in enumerate(chips):
                copy(ins, outs, sems, a, 1 + j, (x, y, c), (*chip, c), src=ins[a]).start()
            copy(ins, outs, sems, a, 0, (x, y, c), (x, y, 1 - c), src=ins[a]).start()

    def forward(ins, outs, sems):
        x, y, c, chips = _place()
        for a in range(n):
            for j, chip in enumerate(chips):
                copy(ins, outs, sems, a, 1 + j, (*chip, c), (x, y, c)).wait_recv()
                copy(ins, outs, sems, a, 4 + j, (*chip, c), (x, y, 1 - c)).start()

    def finish(ins, outs, sems):
        x, y, c, chips = _place()
        for a in range(n):
            copy(ins, outs, sems, a, 0, (x, y, 1 - c), (x, y, c)).wait_recv()
            for j, chip in enumerate(chips):
                copy(ins, outs, sems, a, 4 + j, (*chip, 1 - c), (x, y, c)).wait_recv()
        for a in range(n):
            for j, chip in enumerate(chips):
                copy(ins, outs, sems, a, 1 + j, (x, y, c), (*chip, c), src=ins[a]).wait_send()
                copy(ins, outs, sems, a, 4 + j, (*chip, c), (x, y, 1 - c)).wait_send()
            copy(ins, outs, sems, a, 0, (x, y, c), (x, y, 1 - c), src=ins[a]).wait_send()
            local(ins, outs, sems, a).wait()

    return Comm(arrays, [jax.ShapeDtypeStruct((N_DEV,) + a.shape, a.dtype) for a in arrays],
                [pltpu.SemaphoreType.DMA((n, 7)), pltpu.SemaphoreType.DMA((n, 7)), pltpu.SemaphoreType.DMA((n,))],
                start, forward, finish)


def comm_direct(arrays, scatter):
    n = len(arrays)

    def peers():
        x, y, c = lax.axis_index("x"), lax.axis_index("y"), lax.axis_index("c")
        out = []
        for r in range(1, N_DEV):
            px, py, pc = (1 - x if r & 4 else x), (1 - y if r & 2 else y), (1 - c if r & 1 else c)
            out.append((r, (px, py, pc), 4 * px + 2 * py + pc))
        return 4 * x + 2 * y + c, out

    def remote(ins, outs, sems, a, r, dev, src_slot, dst_slot):
        return pltpu.make_async_remote_copy(
            src_ref=ins[a].at[src_slot] if scatter[a] else ins[a], dst_ref=outs[a].at[dst_slot],
            send_sem=sems[0].at[a, r - 1], recv_sem=sems[1].at[a, r - 1], device_id=dev, device_id_type=MESH)

    def local(ins, outs, sems, a, me):
        return pltpu.make_async_copy(ins[a].at[me] if scatter[a] else ins[a], outs[a].at[me], sems[2].at[a])

    def start(ins, outs, sems):
        me, ps = peers()
        for a in range(n):
            local(ins, outs, sems, a, me).start()
        for a in range(n):
            for r, dev, lin in ps:
                remote(ins, outs, sems, a, r, dev, lin, me).start()

    def finish(ins, outs, sems):
        me, ps = peers()
        for a in range(n):
            for r, dev, lin in ps:
                remote(ins, outs, sems, a, r, dev, lin, lin).wait_recv()
        for a in range(n):
            for r, dev, lin in ps:
                remote(ins, outs, sems, a, r, dev, lin, me).wait_send()
            local(ins, outs, sems, a, me).wait()

    return Comm(arrays, [jax.ShapeDtypeStruct(a.shape if sc else (N_DEV,) + a.shape, a.dtype) for a, sc in zip(arrays, scatter)],
                [pltpu.SemaphoreType.DMA((n, N_DEV - 1)), pltpu.SemaphoreType.DMA((n, N_DEV - 1)),
                 pltpu.SemaphoreType.DMA((n,))], start, None, finish)


def comm_to_sibling(parts):
    n = len(parts)

    def copy(ins, outs, sems, a, k):
        x, y, c, _ = _place()
        return pltpu.make_async_remote_copy(src_ref=ins[a].at[2 * k + (1 - c)], dst_ref=outs[a].at[k], send_sem=sems[0].at[a, k],
                                            recv_sem=sems[1].at[a, k], device_id=(x, y, 1 - c), device_id_type=MESH)

    def start(ins, outs, sems):
        for a in range(n):
            for k in range(4):
                copy(ins, outs, sems, a, k).start()

    def finish(ins, outs, sems):
        for a in range(n):
            for k in range(4):
                copy(ins, outs, sems, a, k).wait()

    return Comm(parts, [jax.ShapeDtypeStruct((4,) + p.shape[1:], p.dtype) for p in parts],
                [pltpu.SemaphoreType.DMA((n, 4)), pltpu.SemaphoreType.DMA((n, 4))], start, None, finish)


def comm_to_owner_chip(sums):
    n = len(sums)

    def remote(ins, outs, sems, a, j, src_k, dst_k, chip):
        _, _, c, _ = _place()
        return pltpu.make_async_remote_copy(src_ref=ins[a].at[src_k], dst_ref=outs[a].at[dst_k], send_sem=sems[0].at[a, j],
                                            recv_sem=sems[1].at[a, j], device_id=(*chip, c), device_id_type=MESH)

    def local(ins, outs, sems, a):
        x, y, _, _ = _place()
        return pltpu.make_async_copy(ins[a].at[2 * x + y], outs[a].at[2 * x + y], sems[2].at[a])

    def start(ins, outs, sems):
        x, y, _, chips = _place()
        for a in range(n):
            local(ins, outs, sems, a).start()
            for j, (px, py) in enumerate(chips):
                remote(ins, outs, sems, a, j, 2 * px + py, 2 * x + y, (px, py)).start()

    def finish(ins, outs, sems):
        x, y, _, chips = _place()
        for a in range(n):
            for j, (px, py) in enumerate(chips):
                remote(ins, outs, sems, a, j, 2 * x + y, 2 * px + py, (px, py)).wait_recv()
        for a in range(n):
            for j, (px, py) in enumerate(chips):
                remote(ins, outs, sems, a, j, 2 * px + py, 2 * x + y, (px, py)).wait_send()
            local(ins, outs, sems, a).wait()

    return Comm(sums, [jax.ShapeDtypeStruct(s.shape, s.dtype) for s in sums],
                [pltpu.SemaphoreType.DMA((n, 3)), pltpu.SemaphoreType.DMA((n, 3)), pltpu.SemaphoreType.DMA((n,))],
                start, None, finish)


def pair_sum(parts, sib, *, name, tr=512):
    _, R, Cc = parts.shape
    tr = min(tr, R)
    assert R % tr == 0
    core = lax.axis_index("c").astype(jnp.int32).reshape(1)

    def body(c_ref, p_ref, s_ref, o_ref):
        o_ref[0] = (p_ref[0, 0].astype(F32) + s_ref[0].astype(F32)).astype(o_ref.dtype)

    return pl.pallas_call(
        body, name=name,
        grid_spec=pltpu.PrefetchScalarGridSpec(
            num_scalar_prefetch=1, grid=(4, R // tr),
            in_specs=[pl.BlockSpec((1, 1, tr, Cc), lambda k, i, c: (k, c[0], i, 0)),
                      pl.BlockSpec((1, tr, Cc), lambda k, i, c: (k, i, 0))],
            out_specs=pl.BlockSpec((1, tr, Cc), lambda k, i, c: (k, i, 0))),
        out_shape=jax.ShapeDtypeStruct((4, R, Cc), BF16),
        compiler_params=_params(("parallel", "parallel")))(core, parts.reshape(4, 2, R, Cc), sib)


def run_comm(comm, *, name):
    n = len(comm.arrays)

    def body(*refs):
        ins, outs, sems = refs[:n], refs[n:2 * n], refs[2 * n:]
        comm.start(ins, outs, sems)
        if comm.forward is not None:
            comm.forward(ins, outs, sems)
        comm.finish(ins, outs, sems)

    any_spec = pl.BlockSpec(memory_space=pl.ANY)
    return pl.pallas_call(body, name=name, in_specs=[any_spec] * n, out_specs=[any_spec] * n, out_shape=comm.out_shape,
                          scratch_shapes=comm.sems)(*comm.arrays)


def adamw(parts, w, m, v, *, name, tr=128):
    _, R, Cc = w.shape
    tr = min(tr, R)
    assert R % tr == 0
    n_parts = parts.shape[0]

    def body(p_ref, w_ref, m_ref, v_ref, g_ref, d_ref, nm_ref, nv_ref):
        g = p_ref[0].astype(F32)
        for s in range(1, n_parts):
            g = g + p_ref[s].astype(F32)
        nm = ADAM_B1 * m_ref[...] + (1.0 - ADAM_B1) * g
        nv = ADAM_B2 * v_ref[...] + (1.0 - ADAM_B2) * (g * g)
        m_hat = nm / (1.0 - ADAM_B1 ** ADAM_STEP)
        v_hat = nv / (1.0 - ADAM_B2 ** ADAM_STEP)
        g_ref[...] = g
        d_ref[...] = -ADAM_LR * (m_hat / (jnp.sqrt(v_hat) + ADAM_EPS) + ADAM_WD * w_ref[...])
        nm_ref[...] = nm
        nv_ref[...] = nv

    spec = pl.BlockSpec((None, tr, Cc), lambda i: (0, i, 0))
    return pl.pallas_call(
        body, name=name, grid=(R // tr,),
        in_specs=[pl.BlockSpec((n_parts, tr, Cc), lambda i: (0, i, 0)), spec, spec, spec], out_specs=[spec] * 4,
        out_shape=[jax.ShapeDtypeStruct((1, R, Cc), F32)] * 4, compiler_params=_params(("parallel",)))(parts, w, m, v)


def _lanes(vec, base):
    return jnp.pad(vec[None].astype(F32), ((0, 0), (base, LANES - base - vec.shape[0])))


def _col_shards(full_w):
    R, Ct = full_w.shape
    return jnp.transpose(full_w.reshape(R, N_DEV, Ct // N_DEV), (1, 0, 2))


def _regroup(sources, pieces_of):
    return jnp.concatenate([sources[k][:, a:a + w] for (k, a, w) in pieces_of], axis=1)


def _pieces(spans, lo, hi):
    out, pos = [], 0
    for (k, a, w) in spans:
        s, e = max(lo, pos), min(hi, pos + w)
        if s < e:
            out.append((k, a + s - pos, e - s))
        pos += w
    return out


def _from_col_shards(g):
    return jnp.transpose(g, (1, 0, 2)).reshape(g.shape[1], -1)


def kernel(x, mem, g_mix, w_in, conv_w, a_log, dt_bias, gdn_norm_g, fox_b_f, fox_q_norm, fox_k_norm, g_mem, w_mem_kv, mem_q_norm, mem_k_norm, w_up_gdn, w_up_fox, w_up_mem, w_out, g_mlp, w_ff1, w_ff2, loss_target, m_g_mix, m_w_in, m_conv_w, m_a_log, m_dt_bias, m_gdn_norm_g, m_fox_b_f, m_fox_q_norm, m_fox_k_norm, m_g_mem, m_w_mem_kv, m_mem_q_norm, m_mem_k_norm, m_w_up_gdn, m_w_up_fox, m_w_up_mem, m_w_out, m_g_mlp, m_w_ff1, m_w_ff2, v_g_mix, v_w_in, v_conv_w, v_a_log, v_dt_bias, v_gdn_norm_g, v_fox_b_f, v_fox_q_norm, v_fox_k_norm, v_g_mem, v_w_mem_kv, v_mem_q_norm, v_mem_k_norm, v_w_up_gdn, v_w_up_fox, v_w_up_mem, v_w_out, v_g_mlp, v_w_ff1, v_w_ff2):
    loc = dict(locals())
    big_names = ["w_in", "conv_w", "w_mem_kv", "w_up_gdn", "w_up_fox", "w_up_mem", "w_out", "w_ff1", "w_ff2"]
    col_sharded = {"w_in", "conv_w", "w_up_gdn", "w_up_fox", "w_up_mem", "w_ff1"}
    small_names = ["g_mix", "a_log", "dt_bias", "gdn_norm_g", "fox_b_f", "fox_q_norm", "fox_k_norm", "g_mem",
                   "mem_q_norm", "mem_k_norm", "g_mlp"]

    xs, tgt, mems = x[0], loss_target[0], mem[0]
    T, D = xs.shape
    HG = a_log.shape[1]
    HF = fox_b_f.shape[1]
    DM = mem_q_norm.shape[1]
    GQK, GV = HG * HEAD, HG * HEAD
    GQKV = 2 * GQK + GV
    FW = HF * HEAD
    MW = w_mem_kv.shape[2] // 2
    HM = MW // DM
    assert HG <= 8 and HF <= 8

    shard = {n: loc[n][0].astype(BF16) for n in big_names}
    first, rest, last = big_names[:2], big_names[2:-1], big_names[-1:]
    W = {}

    def take(names, gathered):
        for n, g in zip(names, gathered):
            if n in ("w_ff1", "w_in"):
                W[n] = g
            else:
                W[n] = _from_col_shards(g) if n in col_sharded else g.reshape(-1, g.shape[2])

    rms_fn = lambda t, g: (_rms(t, g),)
    h, *gathered = rowwise(rms_fn, [full(xs)], [g_mix], [(D, D, BF16)], name="rms_mix", comm=comm_gather([shard[n] for n in first]))
    take(first, gathered)
    widths = [GQKV, GV, HG, HG, FW, FW, FW, HF, MW, 3 * D]
    offs = np.concatenate([[0], np.cumsum(widths)]).tolist()
    cs = W["w_in"].shape[2]
    shard_spans = [(d, 0, cs) for d in range(N_DEV)]
    in_shards = [W["w_in"][d] for d in range(N_DEV)]
    seg = lambda i: _regroup(in_shards, _pieces(shard_spans, offs[i], offs[i + 1]))
    w_big = jnp.concatenate([seg(0), seg(1), seg(4), seg(5), seg(6), seg(8), seg(9)], axis=1)
    pad8 = lambda s: jnp.pad(s, ((0, 0), (0, 8 - s.shape[1])))
    w_small = jnp.concatenate([pad8(seg(2)), pad8(seg(3)), pad8(seg(7)), jnp.zeros((D, LANES - 24), BF16)], axis=1)
    o_z, o_fq, o_fk, o_fv = GQKV, GQKV + GV, GQKV + GV + FW, GQKV + GV + 2 * FW
    o_mq = o_fv + FW
    o_gt = o_mq + MW
    WB = o_gt + 3 * D
    conv_full = W["conv_w"].astype(F32)

    a_log_l, dt_bias_l, b_f_l = _lanes(a_log[0], LANE_A), _lanes(dt_bias[0], LANE_A), _lanes(fox_b_f[0], LANE_F)

    big, *gathered = matmul(h, w_big, mode="nn", name="proj_big", comm=comm_gather([shard[n] for n in rest]))
    take(rest, gathered)
    small = matmul(h, w_small, mode="nn", name="proj_small", out_dtypes=(F32,))
    bgf, = rowwise(_gates_fn, [full(small)], [a_log_l, dt_bias_l, b_f_l], [(LANES, LANES, F32)], name="gates")

    cn = conv_fwd(big, conv_full, HG, name="conv")
    o_gdn, ss, tinv = gdn_fwd(cn, bgf, HG, name="gdn_fwd")
    oa, = rowwise(_gated_norm_fn, [(o_gdn, HEAD, 0, 1), (big, HEAD, o_z // HEAD, 1)], [gdn_norm_g],
                  [(GV, HEAD, BF16)], name="gated_norm", ncb=HG)

    cum = cumsum_tokens(bgf, reverse=False, name="cumsum")
    fqa = fox_prep(big, o_fq, fox_q_norm, cum, 0, HF, name="fox_prep_q")
    fka = fox_prep(big, o_fk, fox_k_norm, cum, 1, HF, name="fox_prep_k")
    fva = fox_prep(big, o_fv, fox_k_norm, cum, 2, HF, name="fox_prep_v")
    live = fox_live_tiles(cum, fox_q_norm, fox_k_norm, HF, T)
    ob, lse = fox_fwd(fqa, fka, fva, live, HF, name="fox_fwd")

    memn, = rowwise(rms_fn, [full(mems)], [g_mem], [(D, D, BF16)], name="rms_mem")
    kv_m = matmul(memn, W["w_mem_kv"], mode="nn", name="mem_kv", out_dtypes=(F32,))
    kmn, = rowwise(rms_fn, [(kv_m, DM, 0, 1)], [mem_k_norm], [(MW, DM, F32)], name="mem_knorm", ncb=HM)
    vm = kv_m[:, MW:]
    mem_fn = functools.partial(_mem_attn_fn, HM, DM)
    om, = rowwise(mem_fn, [(big, MW, o_mq // MW, 0)], [kmn, vm, mem_q_norm], [(MW, MW, BF16)], name="mem_attn")

    ya = matmul(oa, W["w_up_gdn"], mode="nn", name="up_gdn")
    yb = matmul(ob, W["w_up_fox"], mode="nn", name="up_fox")
    ym = matmul(om, W["w_up_mem"], mode="nn", name="up_mem")
    cbm = min(512, D)
    gate_rows = [(big, cbm, (o_gt + b * D) // cbm, 1) for b in range(3)]
    merge_rows = [(ya, cbm, 0, 1), (yb, cbm, 0, 1), (ym, cbm, 0, 1)] + gate_rows
    y, = rowwise(_merge_fn, merge_rows, [], [(D, cbm, BF16)], name="merge", ncb=D // cbm)
    x1 = matmul(y, W["w_out"], mode="nn", name="out_proj", out_dtypes=(F32,), extras=(xs,),
                epi=lambda r, res: (r + res,))

    h2, = rowwise(rms_fn, [full(x1)], [g_mlp], [(D, D, BF16)], name="rms_mlp")
    u_ff, a_ff, *gathered = matmul(h2, W["w_ff1"], mode="nn", name="ff1", out_dtypes=(BF16, BF16), b_shards=True,
                                   epi=lambda r: (r, jnp.square(jnp.maximum(r, 0.0))),
                                   comm=comm_gather([shard[n] for n in last]))
    take(last, gathered)
    d_out = matmul(a_ff, W["w_ff2"], mode="nn", name="ff2_loss", out_dtypes=(F32,), extras=(x1, tgt),
                   epi=lambda r, res, t: ((r + res - t) * (1.0 / D),))
    loss_local = 0.5 * D * sum_squares(d_out, name="loss_sum")[0, 0]
    loss = lax.psum(loss_local, ("x", "y", "c"))

    G = {}
    d_u = matmul(d_out, W["w_ff2"], mode="nt", name="d_ff2_in", extras=(u_ff,),
                 epi=lambda r, u: (r * 2.0 * jnp.maximum(u.astype(F32), 0.0),))
    d_h2 = matmul(d_u, W["w_ff1"], mode="nt", name="d_ff1_in", b_shards=True)
    d_x1, G["g_mlp"] = rowwise_bwd(rms_fn, [full(x1)], [g_mlp], [full(d_h2)], [(0, F32)], [0], name="d_rms_mlp",
                                   adds=[full(d_out)])
    d_y = matmul(d_x1, W["w_out"], mode="nt", name="d_out_proj_in")
    G["w_out"] = matmul(y, d_x1, mode="tn", name="d_w_out")
    d_ya, d_yb, d_ym, d_ga, d_gb, d_gm = rowwise_bwd(
        _merge_fn, merge_rows, [], [(d_y, cbm, 0, 1)], [(k, BF16) for k in range(6)], [], name="d_merge", ncb=D // cbm)
    d_oa = matmul(d_ya, W["w_up_gdn"], mode="nt", name="d_up_gdn_in")
    d_ob = matmul(d_yb, W["w_up_fox"], mode="nt", name="d_up_fox_in")
    d_om = matmul(d_ym, W["w_up_mem"], mode="nt", name="d_up_mem_in")
    G["w_up_gdn"] = matmul(oa, d_ya, mode="tn", name="d_w_up_gdn")
    G["w_up_fox"] = matmul(ob, d_yb, mode="tn", name="d_w_up_fox")
    G["w_up_mem"] = matmul(om, d_ym, mode="tn", name="d_w_up_mem")

    d_mq, d_kmn, d_vm, G["mem_q_norm"] = rowwise_bwd(
        mem_fn, [(big, MW, o_mq // MW, 0)], [kmn, vm, mem_q_norm], [full(d_om)], [(0, BF16)], [0, 1, 2], name="d_mem_attn")
    d_km, G["mem_k_norm"] = rowwise_bwd(rms_fn, [(kv_m, DM, 0, 1)], [mem_k_norm], [(d_kmn, DM, 0, 1)], [(0, F32)], [0],
                                         name="d_mem_knorm", ncb=HM)
    d_kv_m = jnp.concatenate([d_km, d_vm], axis=1)
    G["w_mem_kv"] = matmul(memn, d_kv_m, mode="tn", name="d_w_mem_kv")
    d_memn = matmul(d_kv_m, W["w_mem_kv"], mode="nt", name="d_mem_kv_in")
    _, G["g_mem"] = rowwise_bwd(rms_fn, [full(mems)], [g_mem], [full(d_memn)], [(0, BF16)], [0], name="d_rms_mem")

    d_fkn, d_fv, d_cum_t = fox_bwd_kv(fqa, fka, fva, d_ob, ob, lse, live, HF, name="fox_bwd_kv")
    d_fqn = fox_bwd_q(fqa, fka, fva, d_ob, ob, lse, live, HF, name="fox_bwd_q")
    d_fq, G["fox_q_norm"] = rowwise_bwd(rms_fn, [(big, HEAD, o_fq // HEAD, 1)], [fox_q_norm], [(d_fqn, HEAD, 0, 1)],
                                         [(0, BF16)], [0], name="d_fox_qnorm", ncb=HF)
    d_fk, G["fox_k_norm"] = rowwise_bwd(rms_fn, [(big, HEAD, o_fk // HEAD, 1)], [fox_k_norm], [(d_fkn, HEAD, 0, 1)],
                                         [(0, BF16)], [0], name="d_fox_knorm", ncb=HF)
    d_cum = jnp.pad(d_cum_t[:HF].T, ((0, 0), (LANE_F, LANES - LANE_F - HF)))
    d_logf = cumsum_tokens(d_cum, reverse=True, name="cumsum_rev")

    d_o_gdn, d_z, G["gdn_norm_g"] = rowwise_bwd(
        _gated_norm_fn, [(o_gdn, HEAD, 0, 1), (big, HEAD, o_z // HEAD, 1)], [gdn_norm_g], [(d_oa, HEAD, 0, 1)],
        [(0, F32), (1, BF16)], [0], name="d_gated_norm", ncb=HG)
    d_cn, d_bg = gdn_bwd(cn, bgf, ss, tinv, d_o_gdn, HG, name="gdn_bwd")
    d_conv_y, G["conv_w"] = conv_bwd_taps(big, conv_full, d_cn, HG, name="d_conv_taps")
    d_qkv = conv_bwd_input(d_conv_y, conv_full, name="d_conv_in")
    d_small, d_al, d_dt, d_bf = rowwise_bwd(_gates_fn, [full(small)], [a_log_l, dt_bias_l, b_f_l], [full(d_bg + d_logf)],
                                            [(0, F32)], [0, 1, 2], name="d_gates")
    G["a_log"], G["dt_bias"], G["fox_b_f"] = (d_al[:, LANE_A:LANE_A + HG], d_dt[:, LANE_A:LANE_A + HG],
                                               d_bf[:, LANE_F:LANE_F + HF])

    def parts(n):
        g = G[n].astype(BF16)
        if g.ndim == 3:
            return g
        return _col_shards(g) if n in col_sharded else g.reshape(N_DEV, -1, g.shape[1])

    recv = {}

    def carried(names, out):
        for n, r in zip(names, out):
            recv[n] = r

    d_big = jnp.concatenate([d_qkv, d_z, d_fq, d_fk, d_fv, d_mq, d_ga, d_gb, d_gm], axis=1)
    group = ["conv_w", "w_mem_kv", "w_up_gdn", "w_up_fox", "w_up_mem", "w_out"]
    G["w_ff2"], *out = matmul(a_ff, d_out, mode="tn", name="d_w_ff2",
                              comm=comm_direct([parts(n) for n in group], [True] * len(group)))
    carried(group, out)
    G["w_ff1"], *out = matmul(h2, d_u, mode="tn", name="d_w_ff1", out_shards=True,
                              comm=comm_direct([parts("w_ff2")], [True]))
    carried(["w_ff2"], out)
    g_big, *out = matmul(h, d_big, mode="tn", name="d_w_big", comm=comm_direct([parts("w_ff1")], [True]))
    carried(["w_ff1"], out)
    g_small = matmul(h, d_small, mode="tn", name="d_w_small", out_dtypes=(F32,))
    grad_spans = [(0, 0, GQKV), (0, o_z, GV), (1, LANE_B, HG), (1, LANE_A, HG), (0, o_fq, FW), (0, o_fk, FW), (0, o_fv, FW),
                  (1, LANE_F, HF), (0, o_mq, MW), (0, o_gt, 3 * D)]
    grad_src = [g_big, g_small.astype(BF16)]
    G["w_in"] = jnp.stack([_regroup(grad_src, _pieces(grad_spans, d * cs, (d + 1) * cs)) for d in range(N_DEV)])
    p_in = parts("w_in")
    d_h_s, from_sibling = matmul(d_small, w_small, mode="nt", name="d_proj_small_in", out_dtypes=(F32,),
                                 comm=comm_to_sibling([p_in]))
    chip_sums = pair_sum(p_in, from_sibling, name="w_in_pair_sum")
    d_h, *out = matmul(d_big, w_big, mode="nt", name="d_proj_big_in", extras=(d_h_s,), epi=lambda r, e: (r + e,),
                       comm=comm_to_owner_chip([chip_sums]))
    carried(["w_in"], out)
    grad_x, G["g_mix"] = rowwise_bwd(rms_fn, [full(xs)], [g_mix], [full(d_h)], [(0, F32)], [0], name="d_rms_mix",
                                     adds=[full(d_x1)])
    grad_x = grad_x[None]

    small_sizes = [loc[n].shape[1] for n in small_names]
    pack = lambda d: jnp.concatenate([d[n].reshape(1, -1) for n in small_names], axis=1)
    npad = -sum(small_sizes) % LANES
    padp = lambda a: jnp.pad(a, ((0, 0), (0, npad)))
    recv_small, = run_comm(comm_direct([padp(pack(G))], [False]), name="gather_small_grads")

    res = {}
    for n in big_names:
        res[n] = adamw(recv[n], loc[n], loc["m_" + n], loc["v_" + n], name="adamw_" + n)
    packed = lambda prefix: padp(pack({n: loc[prefix + n] for n in small_names}))[None]
    sm = adamw(recv_small, packed(""), packed("m_"), packed("v_"), name="adamw_small")
    so = np.concatenate([[0], np.cumsum(small_sizes)]).tolist()
    for i, n in enumerate(small_names):
        res[n] = [t[0, :, so[i]:so[i + 1]] for t in sm]

    order = ["g_mix", "w_in", "conv_w", "a_log", "dt_bias", "gdn_norm_g", "fox_b_f", "fox_q_norm", "fox_k_norm", "g_mem",
             "w_mem_kv", "mem_q_norm", "mem_k_norm", "w_up_gdn", "w_up_fox", "w_up_mem", "w_out", "g_mlp", "w_ff1", "w_ff2"]
    return (loss, grad_x, *[res[n][0] for n in order], *[res[n][1] for n in order],
            *[res[n][2] for n in order], *[res[n][3] for n in order])
```

```python
import functools

import jax
import jax.numpy as jnp
import numpy as np
from jax import lax
from jax.experimental import pallas as pl
from jax.experimental.pallas import tpu as pltpu

F32 = jnp.float32
BF16 = jnp.bfloat16
HI = lax.Precision.HIGHEST

EPS = 1e-6
GDN_CHUNK = 64
GDN_CONV = 4
HEAD = 128
LANES = 128
HALO = 16
N_DEV = 8
MESH = pl.DeviceIdType.MESH
VMEM_LIMIT_V7X = 56 * 1024 * 1024

ADAM_LR, ADAM_B1, ADAM_B2, ADAM_EPS, ADAM_WD, ADAM_STEP = 0.001, 0.9, 0.999, 1e-08, 0.01, 10

LANE_B, LANE_A, LANE_F = 0, 8, 16


def _params(sem):
    return pltpu.CompilerParams(dimension_semantics=sem, vmem_limit_bytes=VMEM_LIMIT_V7X)


def _dg(a, b, ca, cb, prec):
    nb = a.ndim - 2
    batch = tuple(range(nb))
    return lax.dot_general(a, b, (((ca + nb,), (cb + nb,)), (batch, batch)), precision=prec,
                           preferred_element_type=F32)


def _make_mm(prec, cast):
    def c(x):
        return x.astype(BF16) if cast else x

    @jax.custom_vjp
    def nn(a, b):
        return _dg(c(a), c(b), 1, 0, prec)

    @jax.custom_vjp
    def nt(a, b):
        return _dg(c(a), c(b), 1, 1, prec)

    @jax.custom_vjp
    def tn(a, b):
        return _dg(c(a), c(b), 0, 0, prec)

    nn.defvjp(lambda a, b: (nn(a, b), (a, b)), lambda r, g: (nt(g, r[1]), tn(r[0], g)))
    nt.defvjp(lambda a, b: (nt(a, b), (a, b)), lambda r, g: (nn(g, r[1]), tn(g, r[0])))
    tn.defvjp(lambda a, b: (tn(a, b), (a, b)), lambda r, g: (nt(r[1], g), nn(r[0], g)))
    return nn, nt, tn


NN, NT, TN = _make_mm(None, True)
NNH, NTH, TNH = _make_mm(lax.Precision.HIGH, False)
NNX, _, _ = _make_mm(HI, False)


def _sigmoid(x):
    return 1.0 / (1.0 + jnp.exp(-x))


def _silu(x):
    return x * _sigmoid(x)


def _softplus(x):
    return jnp.maximum(x, 0.0) + jnp.log(1.0 + jnp.exp(-jnp.abs(x)))


def _log_sigmoid(x):
    return -_softplus(-x)


def _rms(x, g):
    return x * lax.rsqrt(jnp.mean(x * x, axis=-1, keepdims=True) + EPS) * g


def _tile(n, target):
    t = target
    while t >= LANES:
        if n % t == 0:
            return t
        t //= 2
    return n


def matmul(a, b, *, mode, name, out_dtypes=(BF16,), epi=None, extras=(), tm=1024, tn=1024, tk=2048, comm=None,
           b_shards=False, out_shards=False):
    if b_shards:
        b_rows, b_cols = b.shape[1], N_DEV * b.shape[2]
    else:
        b_rows, b_cols = b.shape
    if mode == "nn":
        (M, K), (K2, N) = a.shape, (b_rows, b_cols)
    elif mode == "nt":
        (M, K), (N, K2) = a.shape, (b_rows, b_cols)
    else:
        (K, M), (K2, N) = a.shape, (b_rows, b_cols)
    assert K == K2, (name, a.shape, b.shape)
    tm, tn, tk = _tile(M, tm), _tile(N, tn), _tile(K, tk)
    if b_shards and mode == "nt":
        tk = K // N_DEV
    if (b_shards and mode != "nt") or out_shards:
        tn = N // N_DEV
    ni, nj, nk = M // tm, N // tn, K // tk
    a_spec = (pl.BlockSpec((tk, tm), lambda i, j, k: (k, i)) if mode == "tn"
              else pl.BlockSpec((tm, tk), lambda i, j, k: (i, k)))
    if b_shards:
        b_spec = (pl.BlockSpec((None, tn, tk), lambda i, j, k: (k, j, 0)) if mode == "nt"
                  else pl.BlockSpec((None, tk, tn), lambda i, j, k: (j, k, 0)))
    else:
        b_spec = (pl.BlockSpec((tn, tk), lambda i, j, k: (j, k)) if mode == "nt"
                  else pl.BlockSpec((tk, tn), lambda i, j, k: (k, j)))
    o_spec = pl.BlockSpec((tm, tn), lambda i, j, k: (i, j))
    w_spec = pl.BlockSpec((None, tm, tn), lambda i, j, k: (j, i, 0)) if out_shards else o_spec
    w_shape = (N_DEV, M, tn) if out_shards else (M, N)
    dims = {"nn": ((1,), (0,)), "nt": ((1,), (1,)), "tn": ((0,), (0,))}[mode]
    ne, no = len(extras), len(out_dtypes)
    nc = len(comm.arrays) if comm is not None else 0
    n_steps = ni * nj * nk

    def body(a_ref, b_ref, *rest):
        ex, c_in = rest[:ne], rest[ne:ne + nc]
        outs, c_out = rest[ne + nc:ne + nc + no], rest[ne + nc + no:ne + nc + no + nc]
        acc = rest[ne + nc + no + nc]
        c_sems = rest[ne + nc + no + nc + 1:]
        k = pl.program_id(2)
        step = (pl.program_id(0) * nj + pl.program_id(1)) * nk + k

        if comm is not None:
            pl.when(step == 0)(lambda: comm.start(c_in, c_out, c_sems))
            if comm.forward is not None and n_steps >= 3:
                pl.when(step == (3 * n_steps) // 4)(lambda: comm.forward(c_in, c_out, c_sems))

        @pl.when(k == 0)
        def _():
            acc[...] = jnp.zeros_like(acc)

        acc[...] += lax.dot_general(a_ref[...].astype(BF16), b_ref[...].astype(BF16), (dims, ((), ())),
                                    preferred_element_type=F32)

        @pl.when(k == nk - 1)
        def _():
            r = acc[...]
            vals = epi(r, *[e[...] for e in ex]) if epi is not None else (r,)
            for o, v in zip(outs, vals):
                o[...] = v.astype(o.dtype)

        if comm is not None:
            @pl.when(step == n_steps - 1)
            def _():
                if comm.forward is not None and n_steps < 3:
                    comm.forward(c_in, c_out, c_sems)
                comm.finish(c_in, c_out, c_sems)

    any_spec = pl.BlockSpec(memory_space=pl.ANY)
    sem = ("arbitrary",) * 3 if comm is not None else ("parallel", "parallel", "arbitrary")
    out = pl.pallas_call(
        body, name=name, grid=(ni, nj, nk),
        in_specs=[a_spec, b_spec] + [o_spec] * ne + [any_spec] * nc, out_specs=[w_spec] * no + [any_spec] * nc,
        out_shape=[jax.ShapeDtypeStruct(w_shape, d) for d in out_dtypes] + (comm.out_shape if comm is not None else []),
        scratch_shapes=[pltpu.VMEM((tm, tn), F32)] + (comm.sems if comm is not None else []),
        compiler_params=_params(sem))(a, b, *extras, *(comm.arrays if comm is not None else []))
    return out[0] if len(out) == 1 else out


def _row_spec(tr, cb, off, moves):
    return pl.BlockSpec((tr, cb), lambda i, j: (i, off + moves * j))


def _whole_spec(p):
    return pl.BlockSpec(p.shape, lambda i, j: (0,) * p.ndim)


def _row_tile(T, rows):
    widest = max(cb for (_, cb, _, _) in rows)
    return min(T, max(512, (1 << 19) // widest))


def rowwise(fn, rows, params, outs, *, name, ncb=1, tr=None, comm=None):
    T = rows[0][0].shape[0]
    tr = min(tr, T) if tr else _row_tile(T, rows)
    assert T % tr == 0
    nr, npar, no = len(rows), len(params), len(outs)
    nc = len(comm.arrays) if comm is not None else 0
    n_steps = (T // tr) * ncb

    def body(*refs):
        r, p, c_in = refs[:nr], refs[nr:nr + npar], refs[nr + npar:nr + npar + nc]
        o, c_out, c_sems = refs[nr + npar + nc:nr + npar + nc + no], refs[nr + npar + nc + no:nr + npar + 2 * nc + no], \
            refs[nr + npar + 2 * nc + no:]
        step = pl.program_id(0) * ncb + pl.program_id(1)
        if comm is not None:
            pl.when(step == 0)(lambda: comm.start(c_in, c_out, c_sems))
            if comm.forward is not None and n_steps >= 3:
                pl.when(step == (3 * n_steps) // 4)(lambda: comm.forward(c_in, c_out, c_sems))
        vals = fn(*[x[...].astype(F32) for x in r], *[x[...] for x in p])
        for oref, v in zip(o, vals):
            oref[...] = v.astype(oref.dtype)
        if comm is not None:
            @pl.when(step == n_steps - 1)
            def _():
                if comm.forward is not None and n_steps < 3:
                    comm.forward(c_in, c_out, c_sems)
                comm.finish(c_in, c_out, c_sems)

    any_spec = pl.BlockSpec(memory_space=pl.ANY)
    res = pl.pallas_call(
        body, name=name, grid=(T // tr, ncb),
        in_specs=([_row_spec(tr, cb, off, mv) for (_, cb, off, mv) in rows] + [_whole_spec(p) for p in params]
                  + [any_spec] * nc),
        out_specs=[_row_spec(tr, cb, 0, 1) for (_, cb, _) in outs] + [any_spec] * nc,
        out_shape=[jax.ShapeDtypeStruct((T, cols), d) for (cols, _, d) in outs] + (comm.out_shape if comm is not None else []),
        scratch_shapes=comm.sems if comm is not None else [],
        compiler_params=_params(("arbitrary", "arbitrary") if comm is not None else ("parallel", "parallel")))(
            *[r[0] for r in rows], *params, *(comm.arrays if comm is not None else []))
    return res


def rowwise_bwd(fn, rows, params, cots, drows, dparams, *, name, ncb=1, tr=None, adds=()):
    T = rows[0][0].shape[0]
    tr = min(tr, T) if tr else _row_tile(T, rows)
    assert T % tr == 0
    nr, npar, nc, ndr, na = len(rows), len(params), len(cots), len(drows), len(adds)

    def body(*refs):
        r, p, c = refs[:nr], refs[nr:nr + npar], refs[nr + npar:nr + npar + nc]
        base = nr + npar + nc + na
        ad, o_r, o_p = refs[base - na:base], refs[base:base + ndr], refs[base + ndr:]
        prim = [x[...].astype(F32) for x in r] + [x[...] for x in p]
        _, vjp = jax.vjp(lambda *a: tuple(fn(*a)), *prim)
        g = vjp(tuple(x[...].astype(F32) for x in c))
        for k, (oref, (idx, _)) in enumerate(zip(o_r, drows)):
            val = g[idx] + ad[k][...].astype(F32) if k < na else g[idx]
            oref[...] = val.astype(oref.dtype)
        first = jnp.logical_and(pl.program_id(0) == 0, pl.program_id(1) == 0)

        @pl.when(first)
        def _():
            for oref in o_p:
                oref[...] = jnp.zeros_like(oref)

        for oref, idx in zip(o_p, dparams):
            oref[...] += g[nr + idx]

    res = pl.pallas_call(
        body, name=name, grid=(T // tr, ncb),
        in_specs=([_row_spec(tr, cb, off, mv) for (_, cb, off, mv) in rows] + [_whole_spec(p) for p in params]
                  + [_row_spec(tr, cb, off, mv) for (_, cb, off, mv) in tuple(cots) + tuple(adds)]),
        out_specs=([_row_spec(tr, rows[idx][1], 0, 1) for (idx, _) in drows]
                   + [_whole_spec(params[idx]) for idx in dparams]),
        out_shape=([jax.ShapeDtypeStruct((T, ncb * rows[idx][1] if rows[idx][3] else rows[idx][1]), d)
                    for (idx, d) in drows]
                   + [jax.ShapeDtypeStruct(params[idx].shape, F32) for idx in dparams]),
        compiler_params=_params(("arbitrary", "arbitrary")))(
            *[r[0] for r in rows], *params, *[c[0] for c in cots], *[a[0] for a in adds])
    return res


def full(a):
    return (a, a.shape[1], 0, 0)


def cumsum_tokens(x, *, reverse, name, tb=256):
    T = x.shape[0]
    tb = min(tb, T)
    nb = T // tb
    idx = (lambda i: (nb - 1 - i, 0)) if reverse else (lambda i: (i, 0))

    def body(x_ref, o_ref, carry):
        @pl.when(pl.program_id(0) == 0)
        def _():
            carry[...] = jnp.zeros_like(carry)

        ii = lax.broadcasted_iota(jnp.int32, (tb, tb), 0)
        jj = lax.broadcasted_iota(jnp.int32, (tb, tb), 1)
        tri = ((ii <= jj) if reverse else (ii >= jj)).astype(F32)
        c = lax.dot_general(tri, x_ref[...], (((1,), (0,)), ((), ())), precision=HI,
                            preferred_element_type=F32) + carry[0:1, :]
        o_ref[...] = c
        carry[0:1, :] = c[0:1, :] if reverse else c[tb - 1:tb, :]

    return pl.pallas_call(
        body, name=name, grid=(nb,), in_specs=[pl.BlockSpec((tb, LANES), idx)],
        out_specs=pl.BlockSpec((tb, LANES), idx), out_shape=jax.ShapeDtypeStruct((T, LANES), F32),
        scratch_shapes=[pltpu.VMEM((8, LANES), F32)], compiler_params=_params(("arbitrary",)))(x)


def _conv_post(y, kind, dk):
    c = _silu(y)
    r = lax.rsqrt(jnp.sum(c * c, axis=-1, keepdims=True) + EPS)
    return jnp.where(kind == 0, c * r * (dk ** -0.5), jnp.where(kind == 1, c * r, c))


def _conv_taps(cur, prev, w, tr):
    ext = jnp.concatenate([prev, cur], axis=0)
    y = w[3:4, :] * cur
    for d in (1, 2, 3):
        y = y + w[3 - d:4 - d, :] * pltpu.roll(ext, d, 0)[HALO:HALO + tr]
    return y


def conv_fwd(big, w, n_qk_heads, *, name, tr=2048):
    T, W = big.shape[0], w.shape[1]
    tr = min(tr, T)
    nh = W // HEAD

    def body(cur_ref, prev_ref, w_ref, o_ref):
        i, j = pl.program_id(0), pl.program_id(1)
        prev = jnp.where(i > 0, prev_ref[...].astype(F32), 0.0)
        y = _conv_taps(cur_ref[...].astype(F32), prev, w_ref[...], tr)
        kind = jnp.where(j < n_qk_heads, 0, jnp.where(j < 2 * n_qk_heads, 1, 2))
        o_ref[...] = _conv_post(y, kind, HEAD)

    return pl.pallas_call(
        body, name=name, grid=(T // tr, nh),
        in_specs=[pl.BlockSpec((tr, HEAD), lambda i, j: (i, j)),
                  pl.BlockSpec((HALO, HEAD), lambda i, j: (jnp.maximum(i * (tr // HALO) - 1, 0), j)),
                  pl.BlockSpec((GDN_CONV, HEAD), lambda i, j: (0, j))],
        out_specs=pl.BlockSpec((tr, HEAD), lambda i, j: (i, j)),
        out_shape=jax.ShapeDtypeStruct((T, W), F32), compiler_params=_params(("parallel", "parallel")))(big, big, w)


def conv_bwd_taps(big, w, dcn, n_qk_heads, *, name, tr=2048):
    T, W = big.shape[0], w.shape[1]
    tr = min(tr, T)
    nh = W // HEAD

    def body(cur_ref, prev_ref, w_ref, g_ref, dy_ref, dw_ref):
        j, i = pl.program_id(0), pl.program_id(1)
        cur = cur_ref[...].astype(F32)
        prev = jnp.where(i > 0, prev_ref[...].astype(F32), 0.0)
        y = _conv_taps(cur, prev, w_ref[...], tr)
        kind = jnp.where(j < n_qk_heads, 0, jnp.where(j < 2 * n_qk_heads, 1, 2))
        _, vjp = jax.vjp(lambda t: _conv_post(t, kind, HEAD), y)
        dy, = vjp(g_ref[...])
        dy_ref[...] = dy
        ext = jnp.concatenate([prev, cur], axis=0)
        rows = [jnp.sum(dy * (cur if d == 0 else pltpu.roll(ext, d, 0)[HALO:HALO + tr]), axis=0, keepdims=True)
                for d in (3, 2, 1, 0)]

        @pl.when(i == 0)
        def _():
            dw_ref[...] = jnp.zeros_like(dw_ref)

        dw_ref[...] += jnp.concatenate(rows, axis=0)

    return pl.pallas_call(
        body, name=name, grid=(nh, T // tr),
        in_specs=[pl.BlockSpec((tr, HEAD), lambda j, i: (i, j)),
                  pl.BlockSpec((HALO, HEAD), lambda j, i: (jnp.maximum(i * (tr // HALO) - 1, 0), j)),
                  pl.BlockSpec((GDN_CONV, HEAD), lambda j, i: (0, j)),
                  pl.BlockSpec((tr, HEAD), lambda j, i: (i, j))],
        out_specs=[pl.BlockSpec((tr, HEAD), lambda j, i: (i, j)), pl.BlockSpec((GDN_CONV, HEAD), lambda j, i: (0, j))],
        out_shape=[jax.ShapeDtypeStruct((T, W), F32), jax.ShapeDtypeStruct((GDN_CONV, W), F32)],
        compiler_params=_params(("parallel", "arbitrary")))(big, big, w, dcn)


def conv_bwd_input(dy, w, *, name, tr=2048):
    T, W = dy.shape
    tr = min(tr, T)
    nrow = T // tr

    def body(cur_ref, nxt_ref, w_ref, o_ref):
        i = pl.program_id(0)
        cur = cur_ref[...]
        nxt = jnp.where(i < nrow - 1, nxt_ref[...], 0.0)
        ext = jnp.concatenate([cur, nxt], axis=0)
        w = w_ref[...]
        dx = w[3:4, :] * cur
        for d in (1, 2, 3):
            dx = dx + w[3 - d:4 - d, :] * pltpu.roll(ext, tr + HALO - d, 0)[0:tr]
        o_ref[...] = dx.astype(o_ref.dtype)

    return pl.pallas_call(
        body, name=name, grid=(nrow, W // HEAD),
        in_specs=[pl.BlockSpec((tr, HEAD), lambda i, j: (i, j)),
                  pl.BlockSpec((HALO, HEAD), lambda i, j: (jnp.minimum((i + 1) * (tr // HALO), T // HALO - 1), j)),
                  pl.BlockSpec((GDN_CONV, HEAD), lambda i, j: (0, j))],
        out_specs=pl.BlockSpec((tr, HEAD), lambda i, j: (i, j)),
        out_shape=jax.ShapeDtypeStruct((T, W), BF16), compiler_params=_params(("parallel", "parallel")))(dy, dy, w)


INV_BLOCK = 16


def _unit_lower_inverse_value(p):
    C = p.shape[-1]
    ii = lax.broadcasted_iota(jnp.int32, (C, C), 0)
    jj = lax.broadcasted_iota(jnp.int32, (C, C), 1)
    eye = jnp.where((ii == jj)[None], 1.0, 0.0)
    same = ((ii // INV_BLOCK) == (jj // INV_BLOCK))[None]
    pd = jnp.where(same, p, 0.0)
    d_inv = eye + pd
    n = 2
    while n < INV_BLOCK:
        pd = NNH(pd, pd)
        d_inv = d_inv + NNH(d_inv, pd)
        n *= 2
    nb = NNH(d_inv, jnp.where(same, 0.0, p))
    t = eye + nb
    n = 2
    while n < C // INV_BLOCK:
        nb = NNH(nb, nb)
        t = t + NNH(t, nb)
        n *= 2
    return NNH(t, d_inv)


@jax.custom_vjp
def _unit_lower_inverse(p, t_known):
    return _unit_lower_inverse_value(p) if t_known is None else t_known


def _unit_lower_inverse_fwd(p, t_known):
    t = _unit_lower_inverse(p, t_known)
    return t, (t, t_known is not None)


def _unit_lower_inverse_bwd(res, g):
    t, had = res
    return NTH(TNH(t, g), t), (jnp.zeros_like(t) if had else None)


_unit_lower_inverse.defvjp(_unit_lower_inverse_fwd, _unit_lower_inverse_bwd)


def _gdn_chunk(q, k, v, bg, S, t_known=None):
    H, C = q.shape[0], q.shape[1]
    ii = lax.broadcasted_iota(jnp.int32, (C, C), 0)
    jj = lax.broadcasted_iota(jnp.int32, (C, C), 1)
    lincl = (ii >= jj).astype(F32)
    strict, incl, eye = (ii > jj)[None], (ii >= jj)[None], (ii == jj)[None]
    gam2d = lax.dot_general(lincl, bg, (((1,), (0,)), ((), ())), precision=HI, preferred_element_type=F32)
    lane = lax.broadcasted_iota(jnp.int32, (H, 1, LANES), 2)
    hh = lax.broadcasted_iota(jnp.int32, (H, 1, LANES), 0)
    beta = jnp.sum(bg[None] * (lane == hh + LANE_B).astype(F32), axis=2, keepdims=True)
    gam = jnp.sum(gam2d[None] * (lane == hh + LANE_A).astype(F32), axis=2, keepdims=True)
    last = (lax.broadcasted_iota(jnp.int32, (1, C, 1), 1) == C - 1).astype(F32)
    gam_last = jnp.sum(gam * last, axis=1, keepdims=True)
    gam_row = NNX(jnp.ones((H, C, C), F32), jnp.where(eye, gam, 0.0))
    diff = gam - gam_row
    dec_s = jnp.where(strict, jnp.exp(jnp.where(strict, diff, 0.0)), 0.0)
    dec_i = jnp.where(incl, jnp.exp(jnp.where(incl, diff, 0.0)), 0.0)
    t = _unit_lower_inverse(-(beta * NT(k, k) * dec_s), t_known)
    eg = jnp.exp(gam)
    wu = NNH(t, jnp.concatenate([beta * eg * k, beta * v], axis=-1))
    w, u0 = wu[..., :HEAD], wu[..., HEAD:]
    qk = NT(q, k) * dec_i
    u = u0 - NN(w, S)
    o = NN(jnp.concatenate([q * eg, qk], axis=-1), jnp.concatenate([S, u], axis=-2))
    S2 = jnp.exp(gam_last) * S + TN(k * jnp.exp(gam_last - gam), u)
    return o, S2, t


def _heads(x, base, H):
    return jnp.stack([x[:, base + h * HEAD:base + (h + 1) * HEAD] for h in range(H)])


def _unheads(x):
    return jnp.concatenate([x[h] for h in range(x.shape[0])], axis=1)


GDN_CHUNKS_PER_STEP = 8


def gdn_fwd(cn, bgf, H, *, name):
    T, C, W = cn.shape[0], GDN_CHUNK, H * HEAD
    N = T // C
    K = GDN_CHUNKS_PER_STEP if N % GDN_CHUNKS_PER_STEP == 0 else 1

    def body(cn_ref, bg_ref, o_ref, ss_ref, t_ref, s_scr):
        @pl.when(pl.program_id(0) == 0)
        def _():
            s_scr[...] = jnp.zeros_like(s_scr)

        S = s_scr[...]
        for c in range(K):
            rows = slice(c * C, (c + 1) * C)
            x = cn_ref[rows, :]
            ss_ref[c] = S
            o, S, t = _gdn_chunk(_heads(x, 0, H), _heads(x, W, H), _heads(x, 2 * W, H), bg_ref[rows, :], S)
            o_ref[rows, :] = _unheads(o)
            t_ref[c] = t
        s_scr[...] = S

    return pl.pallas_call(
        body, name=name, grid=(N // K,),
        in_specs=[pl.BlockSpec((K * C, 3 * W), lambda n: (n, 0)), pl.BlockSpec((K * C, LANES), lambda n: (n, 0))],
        out_specs=[pl.BlockSpec((K * C, W), lambda n: (n, 0)), pl.BlockSpec((K, H, HEAD, HEAD), lambda n: (n, 0, 0, 0)),
                   pl.BlockSpec((K, H, C, C), lambda n: (n, 0, 0, 0))],
        out_shape=[jax.ShapeDtypeStruct((T, W), F32), jax.ShapeDtypeStruct((N, H, HEAD, HEAD), F32),
                   jax.ShapeDtypeStruct((N, H, C, C), F32)],
        scratch_shapes=[pltpu.VMEM((H, HEAD, HEAD), F32)], compiler_params=_params(("arbitrary",)))(cn, bgf)


def gdn_bwd(cn, bgf, ss, tinv, do, H, *, name):
    T, C, W = cn.shape[0], GDN_CHUNK, H * HEAD
    N = T // C
    K = GDN_CHUNKS_PER_STEP if N % GDN_CHUNKS_PER_STEP == 0 else 1
    NS = N // K

    def body(cn_ref, bg_ref, ss_ref, t_ref, do_ref, dcn_ref, dbg_ref, ds_scr):
        @pl.when(pl.program_id(0) == 0)
        def _():
            ds_scr[...] = jnp.zeros_like(ds_scr)

        dS = ds_scr[...]
        for c in reversed(range(K)):
            rows = slice(c * C, (c + 1) * C)
            x = cn_ref[rows, :]
            t_known = t_ref[c]
            _, vjp = jax.vjp(lambda *a: _gdn_chunk(*a, t_known)[:2],
                             _heads(x, 0, H), _heads(x, W, H), _heads(x, 2 * W, H), bg_ref[rows, :], ss_ref[c])
            dq, dk, dv, dbg, dS = vjp((_heads(do_ref[rows, :], 0, H), dS))
            dcn_ref[rows, :] = jnp.concatenate([_unheads(dq), _unheads(dk), _unheads(dv)], axis=1)
            dbg_ref[rows, :] = dbg
        ds_scr[...] = dS

    rev = lambda n: (NS - 1 - n, 0)
    rev4 = lambda n: (NS - 1 - n, 0, 0, 0)
    return pl.pallas_call(
        body, name=name, grid=(NS,),
        in_specs=[pl.BlockSpec((K * C, 3 * W), rev), pl.BlockSpec((K * C, LANES), rev),
                  pl.BlockSpec((K, H, HEAD, HEAD), rev4), pl.BlockSpec((K, H, C, C), rev4), pl.BlockSpec((K * C, W), rev)],
        out_specs=[pl.BlockSpec((K * C, 3 * W), rev), pl.BlockSpec((K * C, LANES), rev)],
        out_shape=[jax.ShapeDtypeStruct((T, 3 * W), F32), jax.ShapeDtypeStruct((T, LANES), F32)],
        scratch_shapes=[pltpu.VMEM((H, HEAD, HEAD), F32)], compiler_params=_params(("arbitrary",)))(cn, bgf, ss, tinv, do)


AUG = 2 * HEAD


def fox_prep(big, col_off, gain, cum, kind, H, *, name, tr=2048):
    T = big.shape[0]
    tr = min(tr, T)

    def body(x_ref, g_ref, c_ref, o_ref):
        h = pl.program_id(1)
        x = x_ref[...].astype(F32)
        lane = lax.broadcasted_iota(jnp.int32, (tr, HEAD), 1)
        if kind == 2:
            main, aug = x, jnp.ones((tr, HEAD), F32)
        else:
            main = _rms(x, g_ref[...]) * ((HEAD ** -0.5) if kind == 0 else 1.0)
            c = jnp.sum(jnp.where(lane == LANE_F + h, c_ref[...], 0.0), axis=1, keepdims=True)
            hi = c.astype(BF16).astype(F32)
            mid = (c - hi).astype(BF16).astype(F32)
            lo = c - hi - mid
            if kind == 0:
                aug = jnp.where(lane == 0, hi, jnp.where(lane == 1, mid, jnp.where(lane == 2, lo,
                                                                                   jnp.where(lane < 6, 1.0, 0.0))))
            else:
                aug = jnp.where(lane < 3, 1.0, jnp.where(lane == 3, -hi, jnp.where(lane == 4, -mid,
                                                                                   jnp.where(lane == 5, -lo, 0.0))))
        o_ref[...] = jnp.concatenate([main, aug], axis=1).astype(o_ref.dtype)

    return pl.pallas_call(
        body, name=name, grid=(T // tr, H),
        in_specs=[pl.BlockSpec((tr, HEAD), lambda i, h: (i, col_off // HEAD + h)),
                  pl.BlockSpec((1, HEAD), lambda i, h: (0, 0)), pl.BlockSpec((tr, LANES), lambda i, h: (i, 0))],
        out_specs=pl.BlockSpec((tr, AUG), lambda i, h: (i, h)),
        out_shape=jax.ShapeDtypeStruct((T, H * AUG), BF16), compiler_params=_params(("parallel", "parallel")))(big, gain, cum)


def _fox_logits(q_ref, k_ref, h, tq, tk, diagonal):
    ha = slice(h * AUG, (h + 1) * AUG)
    s = lax.dot_general(q_ref[:, ha], k_ref[:, ha], (((1,), (1,)), ((), ())), preferred_element_type=F32)
    keep = None
    if diagonal:
        keep = lax.broadcasted_iota(jnp.int32, (tq, tk), 0) >= lax.broadcasted_iota(jnp.int32, (tq, tk), 1)
    return s, keep


def _dispatch(live_ref, H, i, j, below, on_diagonal, head_fn, after=None, straight_from=None):
    straight_from = H if straight_from is None else straight_from
    def straight(diagonal):
        for h in range(H):
            head_fn(h, diagonal)
        if after is not None:
            after()

    def by_head():
        for h in range(H):
            pl.when(live_ref[h, i, j] != 0)(functools.partial(head_fn, h, False))
        if after is not None:
            after()

    n_live = live_ref[H, i, j]
    pl.when(jnp.logical_and(below, n_live >= straight_from))(functools.partial(straight, False))
    pl.when(jnp.logical_and(below, jnp.logical_and(n_live > 0, n_live < straight_from)))(by_head)
    pl.when(on_diagonal)(functools.partial(straight, True))


FOX_TILE = 512
EXP_UNDERFLOW = -100.0


def fox_live_tiles(cum, q_gain, k_gain, H, T):
    t = min(FOX_TILE, T)
    n = T // t
    c = cum[:, LANE_F:LANE_F + H]
    bias = c[0::t][:, None, :] - c[t - 1::t][None, :, :]
    bound = 1.02 * (HEAD ** 0.5) * jnp.max(jnp.abs(q_gain)) * jnp.max(jnp.abs(k_gain))
    causal = (jnp.arange(n)[:, None] >= jnp.arange(n)[None, :])[:, :, None]
    live = jnp.logical_and(causal, 2.0 * bound + bias >= EXP_UNDERFLOW)
    live = live.astype(jnp.int32)
    return jnp.concatenate([jnp.transpose(live, (2, 0, 1)), jnp.sum(live, axis=2)[None]], axis=0)


def fox_fwd(qa, ka, va, live, H, *, name):
    T = qa.shape[0]
    tq = tk = min(FOX_TILE, T)
    nq, nk = T // tq, T // tk

    def body(live_ref, q_ref, k_ref, v_ref, o_ref, lse_ref, acc, m_scr):
        i, j = pl.program_id(0), pl.program_id(1)

        @pl.when(j == 0)
        def _():
            m_scr[...] = jnp.full_like(m_scr, -jnp.inf)
            acc[...] = jnp.zeros_like(acc)

        def head(h, diagonal):
            ha = slice(h * AUG, (h + 1) * AUG)
            s, keep = _fox_logits(q_ref, k_ref, h, tq, tk, diagonal)
            if diagonal:
                s = jnp.where(keep, s, -jnp.inf)
            m_prev = m_scr[h]
            m_new = jnp.maximum(m_prev, jnp.max(s, axis=1, keepdims=True))
            p = jnp.exp(s - m_new[:, 0:1])
            p_hi = p.astype(BF16)
            p_lo = (p - p_hi.astype(F32)).astype(BF16)
            pv = lambda t: lax.dot_general(t, v_ref[:, ha], (((1,), (0,)), ((), ())), preferred_element_type=F32)
            acc[:, ha] = jnp.exp(m_prev[:, 0:1] - m_new[:, 0:1]) * acc[:, ha] + (pv(p_hi) + pv(p_lo))
            m_scr[h] = m_new

        _dispatch(live_ref, H, i, j, j < i, j == i, head, straight_from=(5 * H + 7) // 8)

        @pl.when(j == nk - 1)
        def _():
            lane = lax.broadcasted_iota(jnp.int32, (tq, LANES), 1)
            lse = jnp.zeros((tq, LANES), F32)
            for h in range(H):
                den = acc[:, h * AUG + HEAD:(h + 1) * AUG]
                o_ref[:, h * HEAD:(h + 1) * HEAD] = acc[:, h * AUG:h * AUG + HEAD] / den
                lse = jnp.where(lane == h, m_scr[h] + jnp.log(den), lse)
            lse_ref[...] = lse

    kv_idx = lambda i, j, f: (jnp.minimum(j, i), 0)
    q_idx = lambda i, j, f: (i, 0)
    return pl.pallas_call(
        body, name=name,
        grid_spec=pltpu.PrefetchScalarGridSpec(
            num_scalar_prefetch=1, grid=(nq, nk),
            in_specs=[pl.BlockSpec((tq, H * AUG), q_idx), pl.BlockSpec((tk, H * AUG), kv_idx), pl.BlockSpec((tk, H * AUG), kv_idx)],
            out_specs=[pl.BlockSpec((tq, H * HEAD), q_idx), pl.BlockSpec((tq, LANES), q_idx)],
            scratch_shapes=[pltpu.VMEM((tq, H * AUG), F32), pltpu.VMEM((H, tq, LANES), F32)]),
        out_shape=[jax.ShapeDtypeStruct((T, H * HEAD), F32), jax.ShapeDtypeStruct((T, LANES), F32)],
        compiler_params=_params(("parallel", "arbitrary")))(live, qa, ka, va)


def _fox_ds(q_ref, k_ref, v_ref, do_ref, o_ref, lse_ref, h, tq, tk, diagonal):
    hs = slice(h * HEAD, (h + 1) * HEAD)
    s, keep = _fox_logits(q_ref, k_ref, h, tq, tk, diagonal)
    p = jnp.exp(s - lse_ref[:, h:h + 1])
    if diagonal:
        p = jnp.where(keep, p, 0.0)
    do = do_ref[:, hs]
    dp = lax.dot_general(do, v_ref[:, h * AUG:h * AUG + HEAD], (((1,), (1,)), ((), ())), preferred_element_type=F32)
    delta = jnp.sum(do.astype(F32) * o_ref[:, hs], axis=1, keepdims=True)
    return p, p * (dp - delta)


def fox_bwd_kv(qa, ka, va, do, o, lse, live, H, *, name):
    T, W = do.shape
    tq = tk = min(FOX_TILE, T)
    nq, nk = T // tq, T // tk

    def body(live_ref, q_ref, k_ref, v_ref, do_ref, o_ref, lse_ref, dk_ref, dv_ref, dc_ref, dk_acc, dv_acc, dc_acc):
        j, i = pl.program_id(0), pl.program_id(1)

        @pl.when(i == 0)
        def _():
            dk_acc[...] = jnp.zeros_like(dk_acc)
            dv_acc[...] = jnp.zeros_like(dv_acc)
            dc_acc[...] = jnp.zeros_like(dc_acc)

        def head(h, diagonal):
            hs = slice(h * HEAD, (h + 1) * HEAD)
            p, ds = _fox_ds(q_ref, k_ref, v_ref, do_ref, o_ref, lse_ref, h, tq, tk, diagonal)
            dv_acc[:, hs] += lax.dot_general(p.astype(BF16), do_ref[:, hs], (((0,), (0,)), ((), ())),
                                             preferred_element_type=F32)
            dk_acc[:, hs] += lax.dot_general(ds.astype(BF16), q_ref[:, h * AUG:h * AUG + HEAD],
                                             (((0,), (0,)), ((), ())), preferred_element_type=F32)
            dc_acc[h] -= jnp.broadcast_to(jnp.sum(ds, axis=0, keepdims=True), (8, tk))

        _dispatch(live_ref, H, i, j, i > j, i == j, head)

        @pl.when(i == nq - 1)
        def _():
            dk_ref[...] = dk_acc[...].astype(dk_ref.dtype)
            dv_ref[...] = dv_acc[...].astype(dv_ref.dtype)
            row = lax.broadcasted_iota(jnp.int32, (8, tk), 0)
            dc = jnp.zeros((8, tk), F32)
            for h in range(H):
                dc = jnp.where(row == h, dc_acc[h], dc)
            dc_ref[...] = dc

    q_idx = lambda j, i, f: (jnp.maximum(i, j), 0)
    kv_idx = lambda j, i, f: (j, 0)
    return pl.pallas_call(
        body, name=name,
        grid_spec=pltpu.PrefetchScalarGridSpec(
            num_scalar_prefetch=1, grid=(nk, nq),
            in_specs=[pl.BlockSpec((tq, H * AUG), q_idx), pl.BlockSpec((tk, H * AUG), kv_idx), pl.BlockSpec((tk, H * AUG), kv_idx),
                      pl.BlockSpec((tq, W), q_idx), pl.BlockSpec((tq, W), q_idx), pl.BlockSpec((tq, LANES), q_idx)],
            out_specs=[pl.BlockSpec((tk, W), kv_idx), pl.BlockSpec((tk, W), kv_idx),
                       pl.BlockSpec((8, tk), lambda j, i, f: (0, j))],
            scratch_shapes=[pltpu.VMEM((tk, W), F32), pltpu.VMEM((tk, W), F32), pltpu.VMEM((H, 8, tk), F32)]),
        out_shape=[jax.ShapeDtypeStruct((T, W), BF16), jax.ShapeDtypeStruct((T, W), BF16), jax.ShapeDtypeStruct((8, T), F32)],
        compiler_params=_params(("parallel", "arbitrary")))(live, qa, ka, va, do, o, lse)


def fox_bwd_q(qa, ka, va, do, o, lse, live, H, *, name):
    T, W = do.shape
    tq = tk = min(FOX_TILE, T)
    nq, nk = T // tq, T // tk

    def body(live_ref, q_ref, k_ref, v_ref, do_ref, o_ref, lse_ref, dq_ref, dq_acc):
        i, j = pl.program_id(0), pl.program_id(1)

        @pl.when(j == 0)
        def _():
            dq_acc[...] = jnp.zeros_like(dq_acc)

        def head(h, diagonal):
            hs = slice(h * HEAD, (h + 1) * HEAD)
            _, ds = _fox_ds(q_ref, k_ref, v_ref, do_ref, o_ref, lse_ref, h, tq, tk, diagonal)
            dq_acc[:, hs] += lax.dot_general(ds.astype(BF16), k_ref[:, h * AUG:h * AUG + HEAD],
                                             (((1,), (0,)), ((), ())), preferred_element_type=F32)

        _dispatch(live_ref, H, i, j, j < i, j == i, head)

        @pl.when(j == nk - 1)
        def _():
            dq_ref[...] = (dq_acc[...] * (HEAD ** -0.5)).astype(dq_ref.dtype)

    q_idx = lambda i, j, f: (i, 0)
    kv_idx = lambda i, j, f: (jnp.minimum(j, i), 0)
    return pl.pallas_call(
        body, name=name,
        grid_spec=pltpu.PrefetchScalarGridSpec(
            num_scalar_prefetch=1, grid=(nq, nk),
            in_specs=[pl.BlockSpec((tq, H * AUG), q_idx), pl.BlockSpec((tk, H * AUG), kv_idx), pl.BlockSpec((tk, H * AUG), kv_idx),
                      pl.BlockSpec((tq, W), q_idx), pl.BlockSpec((tq, W), q_idx), pl.BlockSpec((tq, LANES), q_idx)],
            out_specs=pl.BlockSpec((tq, W), q_idx),
            scratch_shapes=[pltpu.VMEM((tq, W), F32)]),
        out_shape=jax.ShapeDtypeStruct((T, W), BF16),
        compiler_params=_params(("parallel", "arbitrary")))(live, qa, ka, va, do, o, lse)


def _gates_fn(small, a_log_l, dt_bias_l, b_f_l):
    lane = lax.broadcasted_iota(jnp.int32, small.shape, 1)
    beta = _sigmoid(small)
    g = -jnp.exp(a_log_l) * _softplus(small + dt_bias_l)
    lf = _log_sigmoid(small + b_f_l)
    return (jnp.where(lane < LANE_A, beta, jnp.where(lane < LANE_F, g, jnp.where(lane < LANE_F + 8, lf, 0.0))),)


def _gated_norm_fn(o, z, g):
    return (_rms(o, g) * _silu(z),)


def _merge_fn(ya, yb, ym, ga, gb, gm):
    return (_sigmoid(ga) * ya + _sigmoid(gb) * yb + _sigmoid(gm) * ym,)


def _mem_attn_fn(nh, dh, mq, kn, v, gq):
    outs = []
    for h in range(nh):
        hs = slice(h * dh, (h + 1) * dh)
        qn = _rms(mq[:, hs], gq)
        s = NT(qn, kn[:, hs]) * (dh ** -0.5)
        e = jnp.exp(s - jnp.max(s, axis=1, keepdims=True))
        p = e / jnp.sum(e, axis=1, keepdims=True)
        outs.append(NN(p, v[:, hs]))
    return (jnp.concatenate(outs, axis=1),)


def sum_squares(x, *, name, tr=512):
    T, D = x.shape
    tr = min(tr, T)

    def body(x_ref, o_ref):
        @pl.when(pl.program_id(0) == 0)
        def _():
            o_ref[...] = jnp.zeros_like(o_ref)

        v = x_ref[...]
        o_ref[...] += jnp.sum(jnp.sum(v * v, axis=1, keepdims=True), axis=0, keepdims=True)

    return pl.pallas_call(
        body, name=name, grid=(T // tr,), in_specs=[pl.BlockSpec((tr, D), lambda i: (i, 0))],
        out_specs=pl.BlockSpec((1, LANES), lambda i: (0, 0)), out_shape=jax.ShapeDtypeStruct((1, LANES), F32),
        compiler_params=_params(("arbitrary",)))(x)


class Comm:
    def __init__(self, arrays, out_shape, sems, start, forward, finish):
        self.arrays, self.out_shape, self.sems = list(arrays), list(out_shape), list(sems)
        self.start, self.forward, self.finish = start, forward, finish


def _place():
    x, y, c = lax.axis_index("x"), lax.axis_index("y"), lax.axis_index("c")
    chips = [(1 - x, y), (x, 1 - y), (1 - x, 1 - y)]
    return x, y, c, chips


def comm_gather(arrays):
    n = len(arrays)
    lin = lambda px, py, pc: 4 * px + 2 * py + pc

    def copy(ins, outs, sems, a, k, block, to, src=None):
        slot = outs[a].at[lin(*block)]
        return pltpu.make_async_remote_copy(src_ref=slot if src is None else src, dst_ref=slot, send_sem=sems[0].at[a, k],
                                            recv_sem=sems[1].at[a, k], device_id=to, device_id_type=MESH)

    def local(ins, outs, sems, a):
        x, y, c, _ = _place()
        return pltpu.make_async_copy(ins[a], outs[a].at[lin(x, y, c)], sems[2].at[a])

    def start(ins, outs, sems):
        x, y, c, chips = _place()
        for a in range(n):
            local(ins, outs, sems, a).start()
        for a in range(n):
            for j, chip in enumerate(chips):
                copy(ins, outs, sems, a, 1 + j, (x, y, c), (*chip, c), src=ins[a]).start()
            copy(ins, outs, sems, a, 0, (x, y, c), (x, y, 1 - c), src=ins[a]).start()

    def forward(ins, outs, sems):
        x, y, c, chips = _place()
        for a in range(n):
            for j, chip in enumerate(chips):
                copy(ins, outs, sems, a, 1 + j, (*chip, c), (x, y, c)).wait_recv()
                copy(ins, outs, sems, a, 4 + j, (*chip, c), (x, y, 1 - c)).start()

    def finish(ins, outs, sems):
        x, y, c, chips = _place()
        for a in range(n):
            copy(ins, outs, sems, a, 0, (x, y, 1 - c), (x, y, c)).wait_recv()
            for j, chip in enumerate(chips):
                copy(ins, outs, sems, a, 4 + j, (*chip, 1 - c), (x, y, c)).wait_recv()
        for a in range(n):
            for j, chip in enumerate(chips):
                copy(ins, outs, sems, a, 1 + j, (x, y, c), (*chip, c), src=ins[a]).wait_send()
                copy(ins, outs, sems, a, 4 + j, (*chip, c), (x, y, 1 - c)).wait_send()
            copy(ins, outs, sems, a, 0, (x, y, c), (x, y, 1 - c), src=ins[a]).wait_send()
            local(ins, outs, sems, a).wait()

    return Comm(arrays, [jax.ShapeDtypeStruct((N_DEV,) + a.shape, a.dtype) for a in arrays],
                [pltpu.SemaphoreType.DMA((n, 7)), pltpu.SemaphoreType.DMA((n, 7)), pltpu.SemaphoreType.DMA((n,))],
                start, forward, finish)


def comm_direct(arrays, scatter):
    n = len(arrays)

    def peers():
        x, y, c = lax.axis_index("x"), lax.axis_index("y"), lax.axis_index("c")
        out = []
        for r in range(1, N_DEV):
            px, py, pc = (1 - x if r & 4 else x), (1 - y if r & 2 else y), (1 - c if r & 1 else c)
            out.append((r, (px, py, pc), 4 * px + 2 * py + pc))
        return 4 * x + 2 * y + c, out

    def remote(ins, outs, sems, a, r, dev, src_slot, dst_slot):
        return pltpu.make_async_remote_copy(
            src_ref=ins[a].at[src_slot] if scatter[a] else ins[a], dst_ref=outs[a].at[dst_slot],
            send_sem=sems[0].at[a, r - 1], recv_sem=sems[1].at[a, r - 1], device_id=dev, device_id_type=MESH)

    def local(ins, outs, sems, a, me):
        return pltpu.make_async_copy(ins[a].at[me] if scatter[a] else ins[a], outs[a].at[me], sems[2].at[a])

    def start(ins, outs, sems):
        me, ps = peers()
        for a in range(n):
            local(ins, outs, sems, a, me).start()
        for a in range(n):
            for r, dev, lin in ps:
                remote(ins, outs, sems, a, r, dev, lin, me).start()

    def finish(ins, outs, sems):
        me, ps = peers()
        for a in range(n):
            for r, dev, lin in ps:
                remote(ins, outs, sems, a, r, dev, lin, lin).wait_recv()
        for a in range(n):
            for r, dev, lin in ps:
                remote(ins, outs, sems, a, r, dev, lin, me).wait_send()
            local(ins, outs, sems, a, me).wait()

    return Comm(arrays, [jax.ShapeDtypeStruct(a.shape if sc else (N_DEV,) + a.shape, a.dtype) for a, sc in zip(arrays, scatter)],
                [pltpu.SemaphoreType.DMA((n, N_DEV - 1)), pltpu.SemaphoreType.DMA((n, N_DEV - 1)),
                 pltpu.SemaphoreType.DMA((n,))], start, None, finish)


def comm_to_sibling(parts):
    n = len(parts)

    def copy(ins, outs, sems, a, k):
        x, y, c, _ = _place()
        return pltpu.make_async_remote_copy(src_ref=ins[a].at[2 * k + (1 - c)], dst_ref=outs[a].at[k], send_sem=sems[0].at[a, k],
                                            recv_sem=sems[1].at[a, k], device_id=(x, y, 1 - c), device_id_type=MESH)

    def start(ins, outs, sems):
        for a in range(n):
            for k in range(4):
                copy(ins, outs, sems, a, k).start()

    def finish(ins, outs, sems):
        for a in range(n):
            for k in range(4):
                copy(ins, outs, sems, a, k).wait()

    return Comm(parts, [jax.ShapeDtypeStruct((4,) + p.shape[1:], p.dtype) for p in parts],
                [pltpu.SemaphoreType.DMA((n, 4)), pltpu.SemaphoreType.DMA((n, 4))], start, None, finish)


def comm_to_owner_chip(sums):
    n = len(sums)

    def remote(ins, outs, sems, a, j, src_k, dst_k, chip):
        _, _, c, _ = _place()
        return pltpu.make_async_remote_copy(src_ref=ins[a].at[src_k], dst_ref=outs[a].at[dst_k], send_sem=sems[0].at[a, j],
                                            recv_sem=sems[1].at[a, j], device_id=(*chip, c), device_id_type=MESH)

    def local(ins, outs, sems, a):
        x, y, _, _ = _place()
        return pltpu.make_async_copy(ins[a].at[2 * x + y], outs[a].at[2 * x + y], sems[2].at[a])

    def start(ins, outs, sems):
        x, y, _, chips = _place()
        for a in range(n):
            local(ins, outs, sems, a).start()
            for j, (px, py) in enumerate(chips):
                remote(ins, outs, sems, a, j, 2 * px + py, 2 * x + y, (px, py)).start()

    def finish(ins, outs, sems):
        x, y, _, chips = _place()
        for a in range(n):
            for j, (px, py) in enumerate(chips):
                remote(ins, outs, sems, a, j, 2 * x + y, 2 * px + py, (px, py)).wait_recv()
        for a in range(n):
            for j, (px, py) in enumerate(chips):
                remote(ins, outs, sems, a, j, 2 * px + py, 2 * x + y, (px, py)).wait_send()
            local(ins, outs, sems, a).wait()

    return Comm(sums, [jax.ShapeDtypeStruct(s.shape, s.dtype) for s in sums],
                [pltpu.SemaphoreType.DMA((n, 3)), pltpu.SemaphoreType.DMA((n, 3)), pltpu.SemaphoreType.DMA((n,))],
                start, None, finish)


def pair_sum(parts, sib, *, name, tr=512):
    _, R, Cc = parts.shape
    tr = min(tr, R)
    assert R % tr == 0
    core = lax.axis_index("c").astype(jnp.int32).reshape(1)

    def body(c_ref, p_ref, s_ref, o_ref):
        o_ref[0] = (p_ref[0, 0].astype(F32) + s_ref[0].astype(F32)).astype(o_ref.dtype)

    return pl.pallas_call(
        body, name=name,
        grid_spec=pltpu.PrefetchScalarGridSpec(
            num_scalar_prefetch=1, grid=(4, R // tr),
            in_specs=[pl.BlockSpec((1, 1, tr, Cc), lambda k, i, c: (k, c[0], i, 0)),
                      pl.BlockSpec((1, tr, Cc), lambda k, i, c: (k, i, 0))],
            out_specs=pl.BlockSpec((1, tr, Cc), lambda k, i, c: (k, i, 0))),
        out_shape=jax.ShapeDtypeStruct((4, R, Cc), BF16),
        compiler_params=_params(("parallel", "parallel")))(core, parts.reshape(4, 2, R, Cc), sib)


def run_comm(comm, *, name):
    n = len(comm.arrays)

    def body(*refs):
        ins, outs, sems = refs[:n], refs[n:2 * n], refs[2 * n:]
        comm.start(ins, outs, sems)
        if comm.forward is not None:
            comm.forward(ins, outs, sems)
        comm.finish(ins, outs, sems)

    any_spec = pl.BlockSpec(memory_space=pl.ANY)
    return pl.pallas_call(body, name=name, in_specs=[any_spec] * n, out_specs=[any_spec] * n, out_shape=comm.out_shape,
                          scratch_shapes=comm.sems)(*comm.arrays)


def adamw(parts, w, m, v, *, name, tr=128):
    R, Cc = w.shape
    tr = min(tr, R)
    assert R % tr == 0
    n_parts = parts.shape[0]

    def body(p_ref, w_ref, m_ref, v_ref, g_ref, d_ref, nm_ref, nv_ref):
        g = p_ref[0].astype(F32)
        for s in range(1, n_parts):
            g = g + p_ref[s].astype(F32)
        nm = ADAM_B1 * m_ref[...] + (1.0 - ADAM_B1) * g
        nv = ADAM_B2 * v_ref[...] + (1.0 - ADAM_B2) * (g * g)
        m_hat = nm / (1.0 - ADAM_B1 ** ADAM_STEP)
        v_hat = nv / (1.0 - ADAM_B2 ** ADAM_STEP)
        g_ref[...] = g
        d_ref[...] = -ADAM_LR * (m_hat / (jnp.sqrt(v_hat) + ADAM_EPS) + ADAM_WD * w_ref[...])
        nm_ref[...] = nm
        nv_ref[...] = nv

    spec = pl.BlockSpec((tr, Cc), lambda i: (i, 0))
    return pl.pallas_call(
        body, name=name, grid=(R // tr,),
        in_specs=[pl.BlockSpec((n_parts, tr, Cc), lambda i: (0, i, 0)), spec, spec, spec], out_specs=[spec] * 4,
        out_shape=[jax.ShapeDtypeStruct((R, Cc), F32)] * 4, compiler_params=_params(("parallel",)))(parts, w, m, v)


def _lanes(vec, base):
    return jnp.pad(vec[None].astype(F32), ((0, 0), (base, LANES - base - vec.shape[0])))


def _col_shards(full_w):
    R, Ct = full_w.shape
    return jnp.transpose(full_w.reshape(R, N_DEV, Ct // N_DEV), (1, 0, 2))


def _regroup(sources, pieces_of):
    return jnp.concatenate([sources[k][:, a:a + w] for (k, a, w) in pieces_of], axis=1)


def _pieces(spans, lo, hi):
    out, pos = [], 0
    for (k, a, w) in spans:
        s, e = max(lo, pos), min(hi, pos + w)
        if s < e:
            out.append((k, a + s - pos, e - s))
        pos += w
    return out


def _from_col_shards(g):
    return jnp.transpose(g, (1, 0, 2)).reshape(g.shape[1], -1)


def kernel(x, mem, g_mix, w_in, conv_w, a_log, dt_bias, gdn_norm_g, fox_b_f, fox_q_norm, fox_k_norm, g_mem, w_mem_kv, mem_q_norm, mem_k_norm, w_up_gdn, w_up_fox, w_up_mem, w_out, g_mlp, w_ff1, w_ff2, loss_target, m_g_mix, m_w_in, m_conv_w, m_a_log, m_dt_bias, m_gdn_norm_g, m_fox_b_f, m_fox_q_norm, m_fox_k_norm, m_g_mem, m_w_mem_kv, m_mem_q_norm, m_mem_k_norm, m_w_up_gdn, m_w_up_fox, m_w_up_mem, m_w_out, m_g_mlp, m_w_ff1, m_w_ff2, v_g_mix, v_w_in, v_conv_w, v_a_log, v_dt_bias, v_gdn_norm_g, v_fox_b_f, v_fox_q_norm, v_fox_k_norm, v_g_mem, v_w_mem_kv, v_mem_q_norm, v_mem_k_norm, v_w_up_gdn, v_w_up_fox, v_w_up_mem, v_w_out, v_g_mlp, v_w_ff1, v_w_ff2):
    loc = dict(locals())
    big_names = ["w_in", "conv_w", "w_mem_kv", "w_up_gdn", "w_up_fox", "w_up_mem", "w_out", "w_ff1", "w_ff2"]
    col_sharded = {"w_in", "conv_w", "w_up_gdn", "w_up_fox", "w_up_mem", "w_ff1"}
    small_names = ["g_mix", "a_log", "dt_bias", "gdn_norm_g", "fox_b_f", "fox_q_norm", "fox_k_norm", "g_mem",
                   "mem_q_norm", "mem_k_norm", "g_mlp"]

    xs, tgt, mems = x[0], loss_target[0], mem[0]
    T, D = xs.shape
    HG = a_log.shape[1]
    HF = fox_b_f.shape[1]
    DM = mem_q_norm.shape[1]
    GQK, GV = HG * HEAD, HG * HEAD
    GQKV = 2 * GQK + GV
    FW = HF * HEAD
    MW = w_mem_kv.shape[2] // 2
    HM = MW // DM
    assert HG <= 8 and HF <= 8

    shard = {n: loc[n][0].astype(BF16) for n in big_names}
    first, rest, last = big_names[:2], big_names[2:-1], big_names[-1:]
    W = {}

    def take(names, gathered):
        for n, g in zip(names, gathered):
            if n in ("w_ff1", "w_in"):
                W[n] = g
            else:
                W[n] = _from_col_shards(g) if n in col_sharded else g.reshape(-1, g.shape[2])

    rms_fn = lambda t, g: (_rms(t, g),)
    h, *gathered = rowwise(rms_fn, [full(xs)], [g_mix], [(D, D, BF16)], name="rms_mix", comm=comm_gather([shard[n] for n in first]))
    take(first, gathered)
    widths = [GQKV, GV, HG, HG, FW, FW, FW, HF, MW, 3 * D]
    offs = np.concatenate([[0], np.cumsum(widths)]).tolist()
    cs = W["w_in"].shape[2]
    shard_spans = [(d, 0, cs) for d in range(N_DEV)]
    in_shards = [W["w_in"][d] for d in range(N_DEV)]
    seg = lambda i: _regroup(in_shards, _pieces(shard_spans, offs[i], offs[i + 1]))
    w_big = jnp.concatenate([seg(0), seg(1), seg(4), seg(5), seg(6), seg(8), seg(9)], axis=1)
    pad8 = lambda s: jnp.pad(s, ((0, 0), (0, 8 - s.shape[1])))
    w_small = jnp.concatenate([pad8(seg(2)), pad8(seg(3)), pad8(seg(7)), jnp.zeros((D, LANES - 24), BF16)], axis=1)
    o_z, o_fq, o_fk, o_fv = GQKV, GQKV + GV, GQKV + GV + FW, GQKV + GV + 2 * FW
    o_mq = o_fv + FW
    o_gt = o_mq + MW
    WB = o_gt + 3 * D
    conv_full = W["conv_w"].astype(F32)

    a_log_l, dt_bias_l, b_f_l = _lanes(a_log[0], LANE_A), _lanes(dt_bias[0], LANE_A), _lanes(fox_b_f[0], LANE_F)

    big, *gathered = matmul(h, w_big, mode="nn", name="proj_big", tm=2048, comm=comm_gather([shard[n] for n in rest]))
    take(rest, gathered)
    small = matmul(h, w_small, mode="nn", name="proj_small", out_dtypes=(F32,))
    bgf, = rowwise(_gates_fn, [full(small)], [a_log_l, dt_bias_l, b_f_l], [(LANES, LANES, F32)], name="gates")

    cn = conv_fwd(big, conv_full, HG, name="conv")
    o_gdn, ss, tinv = gdn_fwd(cn, bgf, HG, name="gdn_fwd")
    oa, = rowwise(_gated_norm_fn, [(o_gdn, HEAD, 0, 1), (big, HEAD, o_z // HEAD, 1)], [gdn_norm_g],
                  [(GV, HEAD, BF16)], name="gated_norm", ncb=HG)

    cum = cumsum_tokens(bgf, reverse=False, name="cumsum")
    fqa = fox_prep(big, o_fq, fox_q_norm, cum, 0, HF, name="fox_prep_q")
    fka = fox_prep(big, o_fk, fox_k_norm, cum, 1, HF, name="fox_prep_k")
    fva = fox_prep(big, o_fv, fox_k_norm, cum, 2, HF, name="fox_prep_v")
    live = fox_live_tiles(cum, fox_q_norm, fox_k_norm, HF, T)
    ob, lse = fox_fwd(fqa, fka, fva, live, HF, name="fox_fwd")

    memn, = rowwise(rms_fn, [full(mems)], [g_mem], [(D, D, BF16)], name="rms_mem")
    kv_m = matmul(memn, W["w_mem_kv"], mode="nn", name="mem_kv", out_dtypes=(F32,))
    kmn, = rowwise(rms_fn, [(kv_m, DM, 0, 1)], [mem_k_norm], [(MW, DM, F32)], name="mem_knorm", ncb=HM)
    vm = kv_m[:, MW:]
    mem_fn = functools.partial(_mem_attn_fn, HM, DM)
    om, = rowwise(mem_fn, [(big, MW, o_mq // MW, 0)], [kmn, vm, mem_q_norm], [(MW, MW, BF16)], name="mem_attn")

    ya = matmul(oa, W["w_up_gdn"], mode="nn", name="up_gdn")
    yb = matmul(ob, W["w_up_fox"], mode="nn", name="up_fox")
    ym = matmul(om, W["w_up_mem"], mode="nn", name="up_mem")
    cbm = min(512, D)
    gate_rows = [(big, cbm, (o_gt + b * D) // cbm, 1) for b in range(3)]
    merge_rows = [(ya, cbm, 0, 1), (yb, cbm, 0, 1), (ym, cbm, 0, 1)] + gate_rows
    y, = rowwise(_merge_fn, merge_rows, [], [(D, cbm, BF16)], name="merge", ncb=D // cbm)
    x1 = matmul(y, W["w_out"], mode="nn", name="out_proj", out_dtypes=(F32,), extras=(xs,),
                epi=lambda r, res: (r + res,))

    h2, = rowwise(rms_fn, [full(x1)], [g_mlp], [(D, D, BF16)], name="rms_mlp")
    u_ff, a_ff, *gathered = matmul(h2, W["w_ff1"], mode="nn", name="ff1", out_dtypes=(BF16, BF16), b_shards=True,
                                   epi=lambda r: (r, jnp.square(jnp.maximum(r, 0.0))),
                                   comm=comm_gather([shard[n] for n in last]))
    take(last, gathered)
    d_out = matmul(a_ff, W["w_ff2"], mode="nn", name="ff2_loss", out_dtypes=(F32,), extras=(x1, tgt),
                   epi=lambda r, res, t: ((r + res - t) * (1.0 / D),))
    loss_local = 0.5 * D * sum_squares(d_out, name="loss_sum")[0, 0]
    loss = lax.psum(loss_local, ("x", "y", "c"))

    G = {}
    d_u = matmul(d_out, W["w_ff2"], mode="nt", name="d_ff2_in", extras=(u_ff,),
                 epi=lambda r, u: (r * 2.0 * jnp.maximum(u.astype(F32), 0.0),))
    d_h2 = matmul(d_u, W["w_ff1"], mode="nt", name="d_ff1_in", b_shards=True)
    d_x1, G["g_mlp"] = rowwise_bwd(rms_fn, [full(x1)], [g_mlp], [full(d_h2)], [(0, F32)], [0], name="d_rms_mlp",
                                   adds=[full(d_out)])
    d_y = matmul(d_x1, W["w_out"], mode="nt", name="d_out_proj_in")
    G["w_out"] = matmul(y, d_x1, mode="tn", name="d_w_out")
    d_ya, d_yb, d_ym, d_ga, d_gb, d_gm = rowwise_bwd(
        _merge_fn, merge_rows, [], [(d_y, cbm, 0, 1)], [(k, BF16) for k in range(6)], [], name="d_merge", ncb=D // cbm)
    d_oa = matmul(d_ya, W["w_up_gdn"], mode="nt", name="d_up_gdn_in")
    d_ob = matmul(d_yb, W["w_up_fox"], mode="nt", name="d_up_fox_in")
    d_om = matmul(d_ym, W["w_up_mem"], mode="nt", name="d_up_mem_in")
    G["w_up_gdn"] = matmul(oa, d_ya, mode="tn", name="d_w_up_gdn")
    G["w_up_fox"] = matmul(ob, d_yb, mode="tn", name="d_w_up_fox")
    G["w_up_mem"] = matmul(om, d_ym, mode="tn", name="d_w_up_mem")

    d_mq, d_kmn, d_vm, G["mem_q_norm"] = rowwise_bwd(
        mem_fn, [(big, MW, o_mq // MW, 0)], [kmn, vm, mem_q_norm], [full(d_om)], [(0, BF16)], [0, 1, 2], name="d_mem_attn")
    d_km, G["mem_k_norm"] = rowwise_bwd(rms_fn, [(kv_m, DM, 0, 1)], [mem_k_norm], [(d_kmn, DM, 0, 1)], [(0, F32)], [0],
                                         name="d_mem_knorm", ncb=HM)
    d_kv_m = jnp.concatenate([d_km, d_vm], axis=1)
    G["w_mem_kv"] = matmul(memn, d_kv_m, mode="tn", name="d_w_mem_kv")
    d_memn = matmul(d_kv_m, W["w_mem_kv"], mode="nt", name="d_mem_kv_in")
    _, G["g_mem"] = rowwise_bwd(rms_fn, [full(mems)], [g_mem], [full(d_memn)], [(0, BF16)], [0], name="d_rms_mem")

    d_fkn, d_fv, d_cum_t = fox_bwd_kv(fqa, fka, fva, d_ob, ob, lse, live, HF, name="fox_bwd_kv")
    d_fqn = fox_bwd_q(fqa, fka, fva, d_ob, ob, lse, live, HF, name="fox_bwd_q")
    d_fq, G["fox_q_norm"] = rowwise_bwd(rms_fn, [(big, HEAD, o_fq // HEAD, 1)], [fox_q_norm], [(d_fqn, HEAD, 0, 1)],
                                         [(0, BF16)], [0], name="d_fox_qnorm", ncb=HF)
    d_fk, G["fox_k_norm"] = rowwise_bwd(rms_fn, [(big, HEAD, o_fk // HEAD, 1)], [fox_k_norm], [(d_fkn, HEAD, 0, 1)],
                                         [(0, BF16)], [0], name="d_fox_knorm", ncb=HF)
    d_cum = jnp.pad(d_cum_t[:HF].T, ((0, 0), (LANE_F, LANES - LANE_F - HF)))
    d_logf = cumsum_tokens(d_cum, reverse=True, name="cumsum_rev")

    d_o_gdn, d_z, G["gdn_norm_g"] = rowwise_bwd(
        _gated_norm_fn, [(o_gdn, HEAD, 0, 1), (big, HEAD, o_z // HEAD, 1)], [gdn_norm_g], [(d_oa, HEAD, 0, 1)],
        [(0, F32), (1, BF16)], [0], name="d_gated_norm", ncb=HG)
    d_cn, d_bg = gdn_bwd(cn, bgf, ss, tinv, d_o_gdn, HG, name="gdn_bwd")
    d_conv_y, G["conv_w"] = conv_bwd_taps(big, conv_full, d_cn, HG, name="d_conv_taps")
    d_qkv = conv_bwd_input(d_conv_y, conv_full, name="d_conv_in")
    d_small, d_al, d_dt, d_bf = rowwise_bwd(_gates_fn, [full(small)], [a_log_l, dt_bias_l, b_f_l], [full(d_bg + d_logf)],
                                            [(0, F32)], [0, 1, 2], name="d_gates")
    G["a_log"], G["dt_bias"], G["fox_b_f"] = (d_al[:, LANE_A:LANE_A + HG], d_dt[:, LANE_A:LANE_A + HG],
                                               d_bf[:, LANE_F:LANE_F + HF])

    def parts(n):
        g = G[n].astype(BF16)
        if g.ndim == 3:
            return g
        return _col_shards(g) if n in col_sharded else g.reshape(N_DEV, -1, g.shape[1])

    recv = {}

    def carried(names, out):
        for n, r in zip(names, out):
            recv[n] = r

    d_big = jnp.concatenate([d_qkv, d_z, d_fq, d_fk, d_fv, d_mq, d_ga, d_gb, d_gm], axis=1)
    group = ["conv_w", "w_mem_kv", "w_up_gdn", "w_up_fox", "w_up_mem", "w_out"]
    G["w_ff2"], *out = matmul(a_ff, d_out, mode="tn", name="d_w_ff2",
                              comm=comm_direct([parts(n) for n in group], [True] * len(group)))
    carried(group, out)
    G["w_ff1"], *out = matmul(h2, d_u, mode="tn", name="d_w_ff1", out_shards=True,
                              comm=comm_direct([parts("w_ff2")], [True]))
    carried(["w_ff2"], out)
    g_big, *out = matmul(h, d_big, mode="tn", name="d_w_big", tm=2048, comm=comm_direct([parts("w_ff1")], [True]))
    carried(["w_ff1"], out)
    g_small = matmul(h, d_small, mode="tn", name="d_w_small", out_dtypes=(F32,))
    grad_spans = [(0, 0, GQKV), (0, o_z, GV), (1, LANE_B, HG), (1, LANE_A, HG), (0, o_fq, FW), (0, o_fk, FW), (0, o_fv, FW),
                  (1, LANE_F, HF), (0, o_mq, MW), (0, o_gt, 3 * D)]
    grad_src = [g_big, g_small.astype(BF16)]
    G["w_in"] = jnp.stack([_regroup(grad_src, _pieces(grad_spans, d * cs, (d + 1) * cs)) for d in range(N_DEV)])
    p_in = parts("w_in")
    d_h_s, from_sibling = matmul(d_small, w_small, mode="nt", name="d_proj_small_in", out_dtypes=(F32,),
                                 comm=comm_to_sibling([p_in]))
    chip_sums = pair_sum(p_in, from_sibling, name="w_in_pair_sum")
    d_h, *out = matmul(d_big, w_big, mode="nt", name="d_proj_big_in", extras=(d_h_s,), epi=lambda r, e: (r + e,),
                       comm=comm_to_owner_chip([chip_sums]))
    carried(["w_in"], out)
    grad_x, G["g_mix"] = rowwise_bwd(rms_fn, [full(xs)], [g_mix], [full(d_h)], [(0, F32)], [0], name="d_rms_mix",
                                     adds=[full(d_x1)])
    grad_x = grad_x[None]

    small_sizes = [loc[n].shape[1] for n in small_names]
    pack = lambda d: jnp.concatenate([d[n].reshape(1, -1) for n in small_names], axis=1)
    npad = -sum(small_sizes) % LANES
    padp = lambda a: jnp.pad(a, ((0, 0), (0, npad)))
    recv_small, = run_comm(comm_direct([padp(pack(G))], [False]), name="gather_small_grads")

    res = {}
    for n in big_names:
        res[n] = [t[None] for t in adamw(recv[n], loc[n][0], loc["m_" + n][0], loc["v_" + n][0], name="adamw_" + n)]
    sm = adamw(recv_small, padp(pack({n: loc[n] for n in small_names})), padp(pack({n: loc["m_" + n] for n in small_names})),
               padp(pack({n: loc["v_" + n] for n in small_names})), name="adamw_small")
    so = np.concatenate([[0], np.cumsum(small_sizes)]).tolist()
    for i, n in enumerate(small_names):
        res[n] = [t[:, so[i]:so[i + 1]] for t in sm]

    order = ["g_mix", "w_in", "conv_w", "a_log", "dt_bias", "gdn_norm_g", "fox_b_f", "fox_q_norm", "fox_k_norm", "g_mem",
             "w_mem_kv", "mem_q_norm", "mem_k_norm", "w_up_gdn", "w_up_fox", "w_up_mem", "w_out", "g_mlp", "w_ff1", "w_ff2"]
    return (loss, grad_x, *[res[n][0] for n in order], *[res[n][1] for n in order],
            *[res[n][2] for n in order], *[res[n][3] for n in order])
```

```python
import functools

import jax
import jax.numpy as jnp
import numpy as np
from jax import lax
from jax.experimental import pallas as pl
from jax.experimental.pallas import tpu as pltpu

F32 = jnp.float32
BF16 = jnp.bfloat16
HI = lax.Precision.HIGHEST

EPS = 1e-6
GDN_CHUNK = 64
GDN_CONV = 4
HEAD = 128
LANES = 128
HALO = 16
N_DEV = 8
MESH = pl.DeviceIdType.MESH
VMEM_LIMIT_V7X = 56 * 1024 * 1024

ADAM_LR, ADAM_B1, ADAM_B2, ADAM_EPS, ADAM_WD, ADAM_STEP = 0.001, 0.9, 0.999, 1e-08, 0.01, 10

LANE_B, LANE_A, LANE_F = 0, 8, 16


def _params(sem):
    return pltpu.CompilerParams(dimension_semantics=sem, vmem_limit_bytes=VMEM_LIMIT_V7X)


def _dg(a, b, ca, cb, prec):
    nb = a.ndim - 2
    batch = tuple(range(nb))
    return lax.dot_general(a, b, (((ca + nb,), (cb + nb,)), (batch, batch)), precision=prec,
                           preferred_element_type=F32)


def _make_mm(prec, cast):
    def c(x):
        return x.astype(BF16) if cast else x

    @jax.custom_vjp
    def nn(a, b):
        return _dg(c(a), c(b), 1, 0, prec)

    @jax.custom_vjp
    def nt(a, b):
        return _dg(c(a), c(b), 1, 1, prec)

    @jax.custom_vjp
    def tn(a, b):
        return _dg(c(a), c(b), 0, 0, prec)

    nn.defvjp(lambda a, b: (nn(a, b), (a, b)), lambda r, g: (nt(g, r[1]), tn(r[0], g)))
    nt.defvjp(lambda a, b: (nt(a, b), (a, b)), lambda r, g: (nn(g, r[1]), tn(g, r[0])))
    tn.defvjp(lambda a, b: (tn(a, b), (a, b)), lambda r, g: (nt(r[1], g), nn(r[0], g)))
    return nn, nt, tn


NN, NT, TN = _make_mm(None, True)
NNH, NTH, TNH = _make_mm(lax.Precision.HIGH, False)
NNX, _, _ = _make_mm(HI, False)


def _sigmoid(x):
    return 1.0 / (1.0 + jnp.exp(-x))


def _silu(x):
    return x * _sigmoid(x)


def _softplus(x):
    return jnp.maximum(x, 0.0) + jnp.log(1.0 + jnp.exp(-jnp.abs(x)))


def _log_sigmoid(x):
    return -_softplus(-x)


def _rms(x, g):
    return x * lax.rsqrt(jnp.mean(x * x, axis=-1, keepdims=True) + EPS) * g


def _tile(n, target):
    t = target
    while t >= LANES:
        if n % t == 0:
            return t
        t //= 2
    return n


def matmul(a, b, *, mode, name, out_dtypes=(BF16,), epi=None, extras=(), tm=1024, tn=1024, tk=2048, comm=None,
           b_shards=False, out_shards=False):
    if b_shards:
        b_rows, b_cols = b.shape[1], N_DEV * b.shape[2]
    else:
        b_rows, b_cols = b.shape
    if mode == "nn":
        (M, K), (K2, N) = a.shape, (b_rows, b_cols)
    elif mode == "nt":
        (M, K), (N, K2) = a.shape, (b_rows, b_cols)
    else:
        (K, M), (K2, N) = a.shape, (b_rows, b_cols)
    assert K == K2, (name, a.shape, b.shape)
    tm, tn, tk = _tile(M, tm), _tile(N, tn), _tile(K, tk)
    if b_shards and mode == "nt":
        tk = K // N_DEV
    if (b_shards and mode != "nt") or out_shards:
        tn = N // N_DEV
    ni, nj, nk = M // tm, N // tn, K // tk
    a_spec = (pl.BlockSpec((tk, tm), lambda i, j, k: (k, i)) if mode == "tn"
              else pl.BlockSpec((tm, tk), lambda i, j, k: (i, k)))
    if b_shards:
        b_spec = (pl.BlockSpec((None, tn, tk), lambda i, j, k: (k, j, 0)) if mode == "nt"
                  else pl.BlockSpec((None, tk, tn), lambda i, j, k: (j, k, 0)))
    else:
        b_spec = (pl.BlockSpec((tn, tk), lambda i, j, k: (j, k)) if mode == "nt"
                  else pl.BlockSpec((tk, tn), lambda i, j, k: (k, j)))
    o_spec = pl.BlockSpec((tm, tn), lambda i, j, k: (i, j))
    w_spec = pl.BlockSpec((None, tm, tn), lambda i, j, k: (j, i, 0)) if out_shards else o_spec
    w_shape = (N_DEV, M, tn) if out_shards else (M, N)
    dims = {"nn": ((1,), (0,)), "nt": ((1,), (1,)), "tn": ((0,), (0,))}[mode]
    ne, no = len(extras), len(out_dtypes)
    nc = len(comm.arrays) if comm is not None else 0
    n_steps = ni * nj * nk

    def body(a_ref, b_ref, *rest):
        ex, c_in = rest[:ne], rest[ne:ne + nc]
        outs, c_out = rest[ne + nc:ne + nc + no], rest[ne + nc + no:ne + nc + no + nc]
        acc = rest[ne + nc + no + nc]
        c_sems = rest[ne + nc + no + nc + 1:]
        k = pl.program_id(2)
        step = (pl.program_id(0) * nj + pl.program_id(1)) * nk + k

        if comm is not None:
            pl.when(step == 0)(lambda: comm.start(c_in, c_out, c_sems))
            if comm.forward is not None and n_steps >= 3:
                pl.when(step == (3 * n_steps) // 4)(lambda: comm.forward(c_in, c_out, c_sems))

        @pl.when(k == 0)
        def _():
            acc[...] = jnp.zeros_like(acc)

        acc[...] += lax.dot_general(a_ref[...].astype(BF16), b_ref[...].astype(BF16), (dims, ((), ())),
                                    preferred_element_type=F32)

        @pl.when(k == nk - 1)
        def _():
            r = acc[...]
            vals = epi(r, *[e[...] for e in ex]) if epi is not None else (r,)
            for o, v in zip(outs, vals):
                o[...] = v.astype(o.dtype)

        if comm is not None:
            @pl.when(step == n_steps - 1)
            def _():
                if comm.forward is not None and n_steps < 3:
                    comm.forward(c_in, c_out, c_sems)
                comm.finish(c_in, c_out, c_sems)

    any_spec = pl.BlockSpec(memory_space=pl.ANY)
    sem = ("arbitrary",) * 3 if comm is not None else ("parallel", "parallel", "arbitrary")
    out = pl.pallas_call(
        body, name=name, grid=(ni, nj, nk),
        in_specs=[a_spec, b_spec] + [o_spec] * ne + [any_spec] * nc, out_specs=[w_spec] * no + [any_spec] * nc,
        out_shape=[jax.ShapeDtypeStruct(w_shape, d) for d in out_dtypes] + (comm.out_shape if comm is not None else []),
        scratch_shapes=[pltpu.VMEM((tm, tn), F32)] + (comm.sems if comm is not None else []),
        compiler_params=_params(sem))(a, b, *extras, *(comm.arrays if comm is not None else []))
    return out[0] if len(out) == 1 else out


def _row_spec(tr, cb, off, moves):
    return pl.BlockSpec((tr, cb), lambda i, j: (i, off + moves * j))


def _whole_spec(p):
    return pl.BlockSpec(p.shape, lambda i, j: (0,) * p.ndim)


def _row_tile(T, rows):
    widest = max(cb for (_, cb, _, _) in rows)
    return min(T, max(512, (1 << 19) // widest))


def rowwise(fn, rows, params, outs, *, name, ncb=1, tr=None, comm=None):
    T = rows[0][0].shape[0]
    tr = min(tr, T) if tr else _row_tile(T, rows)
    assert T % tr == 0
    nr, npar, no = len(rows), len(params), len(outs)
    nc = len(comm.arrays) if comm is not None else 0
    n_steps = (T // tr) * ncb

    def body(*refs):
        r, p, c_in = refs[:nr], refs[nr:nr + npar], refs[nr + npar:nr + npar + nc]
        o, c_out, c_sems = refs[nr + npar + nc:nr + npar + nc + no], refs[nr + npar + nc + no:nr + npar + 2 * nc + no], \
            refs[nr + npar + 2 * nc + no:]
        step = pl.program_id(0) * ncb + pl.program_id(1)
        if comm is not None:
            pl.when(step == 0)(lambda: comm.start(c_in, c_out, c_sems))
            if comm.forward is not None and n_steps >= 3:
                pl.when(step == (3 * n_steps) // 4)(lambda: comm.forward(c_in, c_out, c_sems))
        vals = fn(*[x[...].astype(F32) for x in r], *[x[...] for x in p])
        for oref, v in zip(o, vals):
            oref[...] = v.astype(oref.dtype)
        if comm is not None:
            @pl.when(step == n_steps - 1)
            def _():
                if comm.forward is not None and n_steps < 3:
                    comm.forward(c_in, c_out, c_sems)
                comm.finish(c_in, c_out, c_sems)

    any_spec = pl.BlockSpec(memory_space=pl.ANY)
    res = pl.pallas_call(
        body, name=name, grid=(T // tr, ncb),
        in_specs=([_row_spec(tr, cb, off, mv) for (_, cb, off, mv) in rows] + [_whole_spec(p) for p in params]
                  + [any_spec] * nc),
        out_specs=[_row_spec(tr, cb, 0, 1) for (_, cb, _) in outs] + [any_spec] * nc,
        out_shape=[jax.ShapeDtypeStruct((T, cols), d) for (cols, _, d) in outs] + (comm.out_shape if comm is not None else []),
        scratch_shapes=comm.sems if comm is not None else [],
        compiler_params=_params(("arbitrary", "arbitrary") if comm is not None else ("parallel", "parallel")))(
            *[r[0] for r in rows], *params, *(comm.arrays if comm is not None else []))
    return res


def rowwise_bwd(fn, rows, params, cots, drows, dparams, *, name, ncb=1, tr=None, adds=()):
    T = rows[0][0].shape[0]
    tr = min(tr, T) if tr else _row_tile(T, rows)
    assert T % tr == 0
    nr, npar, nc, ndr, na = len(rows), len(params), len(cots), len(drows), len(adds)

    def body(*refs):
        r, p, c = refs[:nr], refs[nr:nr + npar], refs[nr + npar:nr + npar + nc]
        base = nr + npar + nc + na
        ad, o_r, o_p = refs[base - na:base], refs[base:base + ndr], refs[base + ndr:]
        prim = [x[...].astype(F32) for x in r] + [x[...] for x in p]
        _, vjp = jax.vjp(lambda *a: tuple(fn(*a)), *prim)
        g = vjp(tuple(x[...].astype(F32) for x in c))
        for k, (oref, (idx, _)) in enumerate(zip(o_r, drows)):
            val = g[idx] + ad[k][...].astype(F32) if k < na else g[idx]
            oref[...] = val.astype(oref.dtype)
        first = jnp.logical_and(pl.program_id(0) == 0, pl.program_id(1) == 0)

        @pl.when(first)
        def _():
            for oref in o_p:
                oref[...] = jnp.zeros_like(oref)

        for oref, idx in zip(o_p, dparams):
            oref[...] += g[nr + idx]

    res = pl.pallas_call(
        body, name=name, grid=(T // tr, ncb),
        in_specs=([_row_spec(tr, cb, off, mv) for (_, cb, off, mv) in rows] + [_whole_spec(p) for p in params]
                  + [_row_spec(tr, cb, off, mv) for (_, cb, off, mv) in tuple(cots) + tuple(adds)]),
        out_specs=([_row_spec(tr, rows[idx][1], 0, 1) for (idx, _) in drows]
                   + [_whole_spec(params[idx]) for idx in dparams]),
        out_shape=([jax.ShapeDtypeStruct((T, ncb * rows[idx][1] if rows[idx][3] else rows[idx][1]), d)
                    for (idx, d) in drows]
                   + [jax.ShapeDtypeStruct(params[idx].shape, F32) for idx in dparams]),
        compiler_params=_params(("arbitrary", "arbitrary")))(
            *[r[0] for r in rows], *params, *[c[0] for c in cots], *[a[0] for a in adds])
    return res


def full(a):
    return (a, a.shape[1], 0, 0)


def cumsum_tokens(x, *, reverse, name, tb=256):
    T = x.shape[0]
    tb = min(tb, T)
    nb = T // tb
    idx = (lambda i: (nb - 1 - i, 0)) if reverse else (lambda i: (i, 0))

    def body(x_ref, o_ref, carry):
        @pl.when(pl.program_id(0) == 0)
        def _():
            carry[...] = jnp.zeros_like(carry)

        ii = lax.broadcasted_iota(jnp.int32, (tb, tb), 0)
        jj = lax.broadcasted_iota(jnp.int32, (tb, tb), 1)
        tri = ((ii <= jj) if reverse else (ii >= jj)).astype(F32)
        c = lax.dot_general(tri, x_ref[...], (((1,), (0,)), ((), ())), precision=HI,
                            preferred_element_type=F32) + carry[0:1, :]
        o_ref[...] = c
        carry[0:1, :] = c[0:1, :] if reverse else c[tb - 1:tb, :]

    return pl.pallas_call(
        body, name=name, grid=(nb,), in_specs=[pl.BlockSpec((tb, LANES), idx)],
        out_specs=pl.BlockSpec((tb, LANES), idx), out_shape=jax.ShapeDtypeStruct((T, LANES), F32),
        scratch_shapes=[pltpu.VMEM((8, LANES), F32)], compiler_params=_params(("arbitrary",)))(x)


def _conv_post(y, kind, dk):
    c = _silu(y)
    r = lax.rsqrt(jnp.sum(c * c, axis=-1, keepdims=True) + EPS)
    return jnp.where(kind == 0, c * r * (dk ** -0.5), jnp.where(kind == 1, c * r, c))


def _conv_taps(cur, prev, w, tr):
    ext = jnp.concatenate([prev, cur], axis=0)
    y = w[3:4, :] * cur
    for d in (1, 2, 3):
        y = y + w[3 - d:4 - d, :] * pltpu.roll(ext, d, 0)[HALO:HALO + tr]
    return y


def conv_fwd(big, w, n_qk_heads, *, name, tr=2048):
    T, W = big.shape[0], w.shape[1]
    tr = min(tr, T)
    nh = W // HEAD

    def body(cur_ref, prev_ref, w_ref, o_ref):
        i, j = pl.program_id(0), pl.program_id(1)
        prev = jnp.where(i > 0, prev_ref[...].astype(F32), 0.0)
        y = _conv_taps(cur_ref[...].astype(F32), prev, w_ref[...], tr)
        kind = jnp.where(j < n_qk_heads, 0, jnp.where(j < 2 * n_qk_heads, 1, 2))
        o_ref[...] = _conv_post(y, kind, HEAD)

    return pl.pallas_call(
        body, name=name, grid=(T // tr, nh),
        in_specs=[pl.BlockSpec((tr, HEAD), lambda i, j: (i, j)),
                  pl.BlockSpec((HALO, HEAD), lambda i, j: (jnp.maximum(i * (tr // HALO) - 1, 0), j)),
                  pl.BlockSpec((GDN_CONV, HEAD), lambda i, j: (0, j))],
        out_specs=pl.BlockSpec((tr, HEAD), lambda i, j: (i, j)),
        out_shape=jax.ShapeDtypeStruct((T, W), F32), compiler_params=_params(("parallel", "parallel")))(big, big, w)


def conv_bwd_taps(big, w, dcn, n_qk_heads, *, name, tr=2048):
    T, W = big.shape[0], w.shape[1]
    tr = min(tr, T)
    nh = W // HEAD

    def body(cur_ref, prev_ref, w_ref, g_ref, dy_ref, dw_ref):
        j, i = pl.program_id(0), pl.program_id(1)
        cur = cur_ref[...].astype(F32)
        prev = jnp.where(i > 0, prev_ref[...].astype(F32), 0.0)
        y = _conv_taps(cur, prev, w_ref[...], tr)
        kind = jnp.where(j < n_qk_heads, 0, jnp.where(j < 2 * n_qk_heads, 1, 2))
        _, vjp = jax.vjp(lambda t: _conv_post(t, kind, HEAD), y)
        dy, = vjp(g_ref[...])
        dy_ref[...] = dy
        ext = jnp.concatenate([prev, cur], axis=0)
        rows = [jnp.sum(dy * (cur if d == 0 else pltpu.roll(ext, d, 0)[HALO:HALO + tr]), axis=0, keepdims=True)
                for d in (3, 2, 1, 0)]

        @pl.when(i == 0)
        def _():
            dw_ref[...] = jnp.zeros_like(dw_ref)

        dw_ref[...] += jnp.concatenate(rows, axis=0)

    return pl.pallas_call(
        body, name=name, grid=(nh, T // tr),
        in_specs=[pl.BlockSpec((tr, HEAD), lambda j, i: (i, j)),
                  pl.BlockSpec((HALO, HEAD), lambda j, i: (jnp.maximum(i * (tr // HALO) - 1, 0), j)),
                  pl.BlockSpec((GDN_CONV, HEAD), lambda j, i: (0, j)),
                  pl.BlockSpec((tr, HEAD), lambda j, i: (i, j))],
        out_specs=[pl.BlockSpec((tr, HEAD), lambda j, i: (i, j)), pl.BlockSpec((GDN_CONV, HEAD), lambda j, i: (0, j))],
        out_shape=[jax.ShapeDtypeStruct((T, W), F32), jax.ShapeDtypeStruct((GDN_CONV, W), F32)],
        compiler_params=_params(("parallel", "arbitrary")))(big, big, w, dcn)


def conv_bwd_input(dy, w, *, name, tr=2048):
    T, W = dy.shape
    tr = min(tr, T)
    nrow = T // tr

    def body(cur_ref, nxt_ref, w_ref, o_ref):
        i = pl.program_id(0)
        cur = cur_ref[...]
        nxt = jnp.where(i < nrow - 1, nxt_ref[...], 0.0)
        ext = jnp.concatenate([cur, nxt], axis=0)
        w = w_ref[...]
        dx = w[3:4, :] * cur
        for d in (1, 2, 3):
            dx = dx + w[3 - d:4 - d, :] * pltpu.roll(ext, tr + HALO - d, 0)[0:tr]
        o_ref[...] = dx.astype(o_ref.dtype)

    return pl.pallas_call(
        body, name=name, grid=(nrow, W // HEAD),
        in_specs=[pl.BlockSpec((tr, HEAD), lambda i, j: (i, j)),
                  pl.BlockSpec((HALO, HEAD), lambda i, j: (jnp.minimum((i + 1) * (tr // HALO), T // HALO - 1), j)),
                  pl.BlockSpec((GDN_CONV, HEAD), lambda i, j: (0, j))],
        out_specs=pl.BlockSpec((tr, HEAD), lambda i, j: (i, j)),
        out_shape=jax.ShapeDtypeStruct((T, W), BF16), compiler_params=_params(("parallel", "parallel")))(dy, dy, w)


INV_BLOCK = 16


def _unit_lower_inverse_value(p):
    C = p.shape[-1]
    ii = lax.broadcasted_iota(jnp.int32, (C, C), 0)
    jj = lax.broadcasted_iota(jnp.int32, (C, C), 1)
    eye = jnp.where((ii == jj)[None], 1.0, 0.0)
    same = ((ii // INV_BLOCK) == (jj // INV_BLOCK))[None]
    pd = jnp.where(same, p, 0.0)
    d_inv = eye + pd
    n = 2
    while n < INV_BLOCK:
        pd = NNH(pd, pd)
        d_inv = d_inv + NNH(d_inv, pd)
        n *= 2
    nb = NNH(d_inv, jnp.where(same, 0.0, p))
    t = eye + nb
    n = 2
    while n < C // INV_BLOCK:
        nb = NNH(nb, nb)
        t = t + NNH(t, nb)
        n *= 2
    return NNH(t, d_inv)


@jax.custom_vjp
def _unit_lower_inverse(p, t_known):
    return _unit_lower_inverse_value(p) if t_known is None else t_known


def _unit_lower_inverse_fwd(p, t_known):
    t = _unit_lower_inverse(p, t_known)
    return t, (t, t_known is not None)


def _unit_lower_inverse_bwd(res, g):
    t, had = res
    return NTH(TNH(t, g), t), (jnp.zeros_like(t) if had else None)


_unit_lower_inverse.defvjp(_unit_lower_inverse_fwd, _unit_lower_inverse_bwd)


def _gdn_chunk(q, k, v, bg, S, t_known=None):
    H, C = q.shape[0], q.shape[1]
    ii = lax.broadcasted_iota(jnp.int32, (C, C), 0)
    jj = lax.broadcasted_iota(jnp.int32, (C, C), 1)
    lincl = (ii >= jj).astype(F32)
    strict, incl, eye = (ii > jj)[None], (ii >= jj)[None], (ii == jj)[None]
    gam2d = lax.dot_general(lincl, bg, (((1,), (0,)), ((), ())), precision=HI, preferred_element_type=F32)
    lane = lax.broadcasted_iota(jnp.int32, (H, 1, LANES), 2)
    hh = lax.broadcasted_iota(jnp.int32, (H, 1, LANES), 0)
    beta = jnp.sum(bg[None] * (lane == hh + LANE_B).astype(F32), axis=2, keepdims=True)
    gam = jnp.sum(gam2d[None] * (lane == hh + LANE_A).astype(F32), axis=2, keepdims=True)
    last = (lax.broadcasted_iota(jnp.int32, (1, C, 1), 1) == C - 1).astype(F32)
    gam_last = jnp.sum(gam * last, axis=1, keepdims=True)
    gam_row = NNX(jnp.ones((H, C, C), F32), jnp.where(eye, gam, 0.0))
    diff = gam - gam_row
    dec_s = jnp.where(strict, jnp.exp(jnp.where(strict, diff, 0.0)), 0.0)
    dec_i = jnp.where(incl, jnp.exp(jnp.where(incl, diff, 0.0)), 0.0)
    t = _unit_lower_inverse(-(beta * NT(k, k) * dec_s), t_known)
    eg = jnp.exp(gam)
    wu = NNH(t, jnp.concatenate([beta * eg * k, beta * v], axis=-1))
    w, u0 = wu[..., :HEAD], wu[..., HEAD:]
    qk = NT(q, k) * dec_i
    u = u0 - NN(w, S)
    o = NN(jnp.concatenate([q * eg, qk], axis=-1), jnp.concatenate([S, u], axis=-2))
    S2 = jnp.exp(gam_last) * S + TN(k * jnp.exp(gam_last - gam), u)
    return o, S2, t


def _heads(x, base, H):
    return jnp.stack([x[:, base + h * HEAD:base + (h + 1) * HEAD] for h in range(H)])


def _unheads(x):
    return jnp.concatenate([x[h] for h in range(x.shape[0])], axis=1)


GDN_CHUNKS_PER_STEP = 8


def gdn_fwd(cn, bgf, H, *, name):
    T, C, W = cn.shape[0], GDN_CHUNK, H * HEAD
    N = T // C
    K = GDN_CHUNKS_PER_STEP if N % GDN_CHUNKS_PER_STEP == 0 else 1

    def body(cn_ref, bg_ref, o_ref, ss_ref, t_ref, s_scr):
        @pl.when(pl.program_id(0) == 0)
        def _():
            s_scr[...] = jnp.zeros_like(s_scr)

        S = s_scr[...]
        for c in range(K):
            rows = slice(c * C, (c + 1) * C)
            x = cn_ref[rows, :]
            ss_ref[c] = S
            o, S, t = _gdn_chunk(_heads(x, 0, H), _heads(x, W, H), _heads(x, 2 * W, H), bg_ref[rows, :], S)
            o_ref[rows, :] = _unheads(o)
            t_ref[c] = t
        s_scr[...] = S

    return pl.pallas_call(
        body, name=name, grid=(N // K,),
        in_specs=[pl.BlockSpec((K * C, 3 * W), lambda n: (n, 0)), pl.BlockSpec((K * C, LANES), lambda n: (n, 0))],
        out_specs=[pl.BlockSpec((K * C, W), lambda n: (n, 0)), pl.BlockSpec((K, H, HEAD, HEAD), lambda n: (n, 0, 0, 0)),
                   pl.BlockSpec((K, H, C, C), lambda n: (n, 0, 0, 0))],
        out_shape=[jax.ShapeDtypeStruct((T, W), F32), jax.ShapeDtypeStruct((N, H, HEAD, HEAD), F32),
                   jax.ShapeDtypeStruct((N, H, C, C), F32)],
        scratch_shapes=[pltpu.VMEM((H, HEAD, HEAD), F32)], compiler_params=_params(("arbitrary",)))(cn, bgf)


def gdn_bwd(cn, bgf, ss, tinv, do, H, *, name):
    T, C, W = cn.shape[0], GDN_CHUNK, H * HEAD
    N = T // C
    K = GDN_CHUNKS_PER_STEP if N % GDN_CHUNKS_PER_STEP == 0 else 1
    NS = N // K

    def body(cn_ref, bg_ref, ss_ref, t_ref, do_ref, dcn_ref, dbg_ref, ds_scr):
        @pl.when(pl.program_id(0) == 0)
        def _():
            ds_scr[...] = jnp.zeros_like(ds_scr)

        dS = ds_scr[...]
        for c in reversed(range(K)):
            rows = slice(c * C, (c + 1) * C)
            x = cn_ref[rows, :]
            t_known = t_ref[c]
            _, vjp = jax.vjp(lambda *a: _gdn_chunk(*a, t_known)[:2],
                             _heads(x, 0, H), _heads(x, W, H), _heads(x, 2 * W, H), bg_ref[rows, :], ss_ref[c])
            dq, dk, dv, dbg, dS = vjp((_heads(do_ref[rows, :], 0, H), dS))
            dcn_ref[rows, :] = jnp.concatenate([_unheads(dq), _unheads(dk), _unheads(dv)], axis=1)
            dbg_ref[rows, :] = dbg
        ds_scr[...] = dS

    rev = lambda n: (NS - 1 - n, 0)
    rev4 = lambda n: (NS - 1 - n, 0, 0, 0)
    return pl.pallas_call(
        body, name=name, grid=(NS,),
        in_specs=[pl.BlockSpec((K * C, 3 * W), rev), pl.BlockSpec((K * C, LANES), rev),
                  pl.BlockSpec((K, H, HEAD, HEAD), rev4), pl.BlockSpec((K, H, C, C), rev4), pl.BlockSpec((K * C, W), rev)],
        out_specs=[pl.BlockSpec((K * C, 3 * W), rev), pl.BlockSpec((K * C, LANES), rev)],
        out_shape=[jax.ShapeDtypeStruct((T, 3 * W), F32), jax.ShapeDtypeStruct((T, LANES), F32)],
        scratch_shapes=[pltpu.VMEM((H, HEAD, HEAD), F32)], compiler_params=_params(("arbitrary",)))(cn, bgf, ss, tinv, do)


AUG = 2 * HEAD


def fox_prep(big, col_off, gain, cum, kind, H, *, name, tr=2048):
    T = big.shape[0]
    tr = min(tr, T)

    def body(x_ref, g_ref, c_ref, o_ref):
        h = pl.program_id(1)
        x = x_ref[...].astype(F32)
        lane = lax.broadcasted_iota(jnp.int32, (tr, HEAD), 1)
        if kind == 2:
            main, aug = x, jnp.ones((tr, HEAD), F32)
        else:
            main = _rms(x, g_ref[...]) * ((HEAD ** -0.5) if kind == 0 else 1.0)
            c = jnp.sum(jnp.where(lane == LANE_F + h, c_ref[...], 0.0), axis=1, keepdims=True)
            hi = c.astype(BF16).astype(F32)
            mid = (c - hi).astype(BF16).astype(F32)
            lo = c - hi - mid
            if kind == 0:
                aug = jnp.where(lane == 0, hi, jnp.where(lane == 1, mid, jnp.where(lane == 2, lo,
                                                                                   jnp.where(lane < 6, 1.0, 0.0))))
            else:
                aug = jnp.where(lane < 3, 1.0, jnp.where(lane == 3, -hi, jnp.where(lane == 4, -mid,
                                                                                   jnp.where(lane == 5, -lo, 0.0))))
        o_ref[...] = jnp.concatenate([main, aug], axis=1).astype(o_ref.dtype)

    return pl.pallas_call(
        body, name=name, grid=(T // tr, H),
        in_specs=[pl.BlockSpec((tr, HEAD), lambda i, h: (i, col_off // HEAD + h)),
                  pl.BlockSpec((1, HEAD), lambda i, h: (0, 0)), pl.BlockSpec((tr, LANES), lambda i, h: (i, 0))],
        out_specs=pl.BlockSpec((tr, AUG), lambda i, h: (i, h)),
        out_shape=jax.ShapeDtypeStruct((T, H * AUG), BF16), compiler_params=_params(("parallel", "parallel")))(big, gain, cum)


def _fox_logits(q_ref, k_ref, h, tq, tk, diagonal):
    ha = slice(h * AUG, (h + 1) * AUG)
    s = lax.dot_general(q_ref[:, ha], k_ref[:, ha], (((1,), (1,)), ((), ())), preferred_element_type=F32)
    keep = None
    if diagonal:
        keep = lax.broadcasted_iota(jnp.int32, (tq, tk), 0) >= lax.broadcasted_iota(jnp.int32, (tq, tk), 1)
    return s, keep


def _dispatch(live_ref, H, i, j, below, on_diagonal, head_fn, after=None, straight_from=None):
    straight_from = H if straight_from is None else straight_from
    def straight(diagonal):
        for h in range(H):
            head_fn(h, diagonal)
        if after is not None:
            after()

    def by_head():
        for h in range(H):
            pl.when(live_ref[h, i, j] != 0)(functools.partial(head_fn, h, False))
        if after is not None:
            after()

    n_live = live_ref[H, i, j]
    pl.when(jnp.logical_and(below, n_live >= straight_from))(functools.partial(straight, False))
    pl.when(jnp.logical_and(below, jnp.logical_and(n_live > 0, n_live < straight_from)))(by_head)
    pl.when(on_diagonal)(functools.partial(straight, True))


FOX_TILE = 512
EXP_UNDERFLOW = -100.0


def fox_live_tiles(cum, q_gain, k_gain, H, T):
    t = min(FOX_TILE, T)
    n = T // t
    c = cum[:, LANE_F:LANE_F + H]
    bias = c[0::t][:, None, :] - c[t - 1::t][None, :, :]
    bound = 1.02 * (HEAD ** 0.5) * jnp.max(jnp.abs(q_gain)) * jnp.max(jnp.abs(k_gain))
    causal = (jnp.arange(n)[:, None] >= jnp.arange(n)[None, :])[:, :, None]
    live = jnp.logical_and(causal, 2.0 * bound + bias >= EXP_UNDERFLOW)
    live = live.astype(jnp.int32)
    return jnp.concatenate([jnp.transpose(live, (2, 0, 1)), jnp.sum(live, axis=2)[None]], axis=0)


def fox_fwd(qa, ka, va, live, H, *, name):
    T = qa.shape[0]
    tq = tk = min(FOX_TILE, T)
    nq, nk = T // tq, T // tk

    def body(live_ref, q_ref, k_ref, v_ref, o_ref, lse_ref, acc, m_scr):
        i, j = pl.program_id(0), pl.program_id(1)

        @pl.when(j == 0)
        def _():
            m_scr[...] = jnp.full_like(m_scr, -jnp.inf)
            acc[...] = jnp.zeros_like(acc)

        def head(h, diagonal):
            ha = slice(h * AUG, (h + 1) * AUG)
            s, keep = _fox_logits(q_ref, k_ref, h, tq, tk, diagonal)
            if diagonal:
                s = jnp.where(keep, s, -jnp.inf)
            m_prev = m_scr[h]
            m_new = jnp.maximum(m_prev, jnp.max(s, axis=1, keepdims=True))
            p = jnp.exp(s - m_new[:, 0:1])
            p_hi = p.astype(BF16)
            p_lo = (p - p_hi.astype(F32)).astype(BF16)
            pv = lambda t: lax.dot_general(t, v_ref[:, ha], (((1,), (0,)), ((), ())), preferred_element_type=F32)
            acc[:, ha] = jnp.exp(m_prev[:, 0:1] - m_new[:, 0:1]) * acc[:, ha] + (pv(p_hi) + pv(p_lo))
            m_scr[h] = m_new

        _dispatch(live_ref, H, i, j, j < i, j == i, head, straight_from=(5 * H + 7) // 8)

        @pl.when(j == nk - 1)
        def _():
            lane = lax.broadcasted_iota(jnp.int32, (tq, LANES), 1)
            lse = jnp.zeros((tq, LANES), F32)
            for h in range(H):
                den = acc[:, h * AUG + HEAD:(h + 1) * AUG]
                o_ref[:, h * HEAD:(h + 1) * HEAD] = acc[:, h * AUG:h * AUG + HEAD] / den
                lse = jnp.where(lane == h, m_scr[h] + jnp.log(den), lse)
            lse_ref[...] = lse

    kv_idx = lambda i, j, f: (jnp.minimum(j, i), 0)
    q_idx = lambda i, j, f: (i, 0)
    return pl.pallas_call(
        body, name=name,
        grid_spec=pltpu.PrefetchScalarGridSpec(
            num_scalar_prefetch=1, grid=(nq, nk),
            in_specs=[pl.BlockSpec((tq, H * AUG), q_idx), pl.BlockSpec((tk, H * AUG), kv_idx), pl.BlockSpec((tk, H * AUG), kv_idx)],
            out_specs=[pl.BlockSpec((tq, H * HEAD), q_idx), pl.BlockSpec((tq, LANES), q_idx)],
            scratch_shapes=[pltpu.VMEM((tq, H * AUG), F32), pltpu.VMEM((H, tq, LANES), F32)]),
        out_shape=[jax.ShapeDtypeStruct((T, H * HEAD), F32), jax.ShapeDtypeStruct((T, LANES), F32)],
        compiler_params=_params(("parallel", "arbitrary")))(live, qa, ka, va)


def _fox_ds(q_ref, k_ref, v_ref, do_ref, o_ref, lse_ref, h, tq, tk, diagonal):
    hs = slice(h * HEAD, (h + 1) * HEAD)
    s, keep = _fox_logits(q_ref, k_ref, h, tq, tk, diagonal)
    p = jnp.exp(s - lse_ref[:, h:h + 1])
    if diagonal:
        p = jnp.where(keep, p, 0.0)
    do = do_ref[:, hs]
    dp = lax.dot_general(do, v_ref[:, h * AUG:h * AUG + HEAD], (((1,), (1,)), ((), ())), preferred_element_type=F32)
    delta = jnp.sum(do.astype(F32) * o_ref[:, hs], axis=1, keepdims=True)
    return p, p * (dp - delta)


def fox_bwd_kv(qa, ka, va, do, o, lse, live, H, *, name):
    T, W = do.shape
    tq = tk = min(FOX_TILE, T)
    nq, nk = T // tq, T // tk

    def body(live_ref, q_ref, k_ref, v_ref, do_ref, o_ref, lse_ref, dk_ref, dv_ref, dc_ref, dk_acc, dv_acc, dc_acc):
        j, i = pl.program_id(0), pl.program_id(1)

        @pl.when(i == 0)
        def _():
            dk_acc[...] = jnp.zeros_like(dk_acc)
            dv_acc[...] = jnp.zeros_like(dv_acc)
            dc_acc[...] = jnp.zeros_like(dc_acc)

        def head(h, diagonal):
            hs = slice(h * HEAD, (h + 1) * HEAD)
            p, ds = _fox_ds(q_ref, k_ref, v_ref, do_ref, o_ref, lse_ref, h, tq, tk, diagonal)
            dv_acc[:, hs] += lax.dot_general(p.astype(BF16), do_ref[:, hs], (((0,), (0,)), ((), ())),
                                             preferred_element_type=F32)
            dk_acc[:, hs] += lax.dot_general(ds.astype(BF16), q_ref[:, h * AUG:h * AUG + HEAD],
                                             (((0,), (0,)), ((), ())), preferred_element_type=F32)
            dc_acc[h] -= jnp.broadcast_to(jnp.sum(ds, axis=0, keepdims=True), (8, tk))

        _dispatch(live_ref, H, i, j, i > j, i == j, head)

        @pl.when(i == nq - 1)
        def _():
            dk_ref[...] = dk_acc[...].astype(dk_ref.dtype)
            dv_ref[...] = dv_acc[...].astype(dv_ref.dtype)
            row = lax.broadcasted_iota(jnp.int32, (8, tk), 0)
            dc = jnp.zeros((8, tk), F32)
            for h in range(H):
                dc = jnp.where(row == h, dc_acc[h], dc)
            dc_ref[...] = dc

    q_idx = lambda j, i, f: (jnp.maximum(i, j), 0)
    kv_idx = lambda j, i, f: (j, 0)
    return pl.pallas_call(
        body, name=name,
        grid_spec=pltpu.PrefetchScalarGridSpec(
            num_scalar_prefetch=1, grid=(nk, nq),
            in_specs=[pl.BlockSpec((tq, H * AUG), q_idx), pl.BlockSpec((tk, H * AUG), kv_idx), pl.BlockSpec((tk, H * AUG), kv_idx),
                      pl.BlockSpec((tq, W), q_idx), pl.BlockSpec((tq, W), q_idx), pl.BlockSpec((tq, LANES), q_idx)],
            out_specs=[pl.BlockSpec((tk, W), kv_idx), pl.BlockSpec((tk, W), kv_idx),
                       pl.BlockSpec((8, tk), lambda j, i, f: (0, j))],
            scratch_shapes=[pltpu.VMEM((tk, W), F32), pltpu.VMEM((tk, W), F32), pltpu.VMEM((H, 8, tk), F32)]),
        out_shape=[jax.ShapeDtypeStruct((T, W), BF16), jax.ShapeDtypeStruct((T, W), BF16), jax.ShapeDtypeStruct((8, T), F32)],
        compiler_params=_params(("parallel", "arbitrary")))(live, qa, ka, va, do, o, lse)


def fox_bwd_q(qa, ka, va, do, o, lse, live, H, *, name):
    T, W = do.shape
    tq = tk = min(FOX_TILE, T)
    nq, nk = T // tq, T // tk

    def body(live_ref, q_ref, k_ref, v_ref, do_ref, o_ref, lse_ref, dq_ref, dq_acc):
        i, j = pl.program_id(0), pl.program_id(1)

        @pl.when(j == 0)
        def _():
            dq_acc[...] = jnp.zeros_like(dq_acc)

        def head(h, diagonal):
            hs = slice(h * HEAD, (h + 1) * HEAD)
            _, ds = _fox_ds(q_ref, k_ref, v_ref, do_ref, o_ref, lse_ref, h, tq, tk, diagonal)
            dq_acc[:, hs] += lax.dot_general(ds.astype(BF16), k_ref[:, h * AUG:h * AUG + HEAD],
                                             (((1,), (0,)), ((), ())), preferred_element_type=F32)

        _dispatch(live_ref, H, i, j, j < i, j == i, head)

        @pl.when(j == nk - 1)
        def _():
            dq_ref[...] = (dq_acc[...] * (HEAD ** -0.5)).astype(dq_ref.dtype)

    q_idx = lambda i, j, f: (i, 0)
    kv_idx = lambda i, j, f: (jnp.minimum(j, i), 0)
    return pl.pallas_call(
        body, name=name,
        grid_spec=pltpu.PrefetchScalarGridSpec(
            num_scalar_prefetch=1, grid=(nq, nk),
            in_specs=[pl.BlockSpec((tq, H * AUG), q_idx), pl.BlockSpec((tk, H * AUG), kv_idx), pl.BlockSpec((tk, H * AUG), kv_idx),
                      pl.BlockSpec((tq, W), q_idx), pl.BlockSpec((tq, W), q_idx), pl.BlockSpec((tq, LANES), q_idx)],
            out_specs=pl.BlockSpec((tq, W), q_idx),
            scratch_shapes=[pltpu.VMEM((tq, W), F32)]),
        out_shape=jax.ShapeDtypeStruct((T, W), BF16),
        compiler_params=_params(("parallel", "arbitrary")))(live, qa, ka, va, do, o, lse)


def _gates_fn(small, a_log_l, dt_bias_l, b_f_l):
    lane = lax.broadcasted_iota(jnp.int32, small.shape, 1)
    beta = _sigmoid(small)
    g = -jnp.exp(a_log_l) * _softplus(small + dt_bias_l)
    lf = _log_sigmoid(small + b_f_l)
    return (jnp.where(lane < LANE_A, beta, jnp.where(lane < LANE_F, g, jnp.where(lane < LANE_F + 8, lf, 0.0))),)


def _gated_norm_fn(o, z, g):
    return (_rms(o, g) * _silu(z),)


def _merge_fn(ya, yb, ym, ga, gb, gm):
    return (_sigmoid(ga) * ya + _sigmoid(gb) * yb + _sigmoid(gm) * ym,)


def _mem_attn_fn(nh, dh, mq, kn, v, gq):
    outs = []
    for h in range(nh):
        hs = slice(h * dh, (h + 1) * dh)
        qn = _rms(mq[:, hs], gq)
        s = NT(qn, kn[:, hs]) * (dh ** -0.5)
        e = jnp.exp(s - jnp.max(s, axis=1, keepdims=True))
        p = e / jnp.sum(e, axis=1, keepdims=True)
        outs.append(NN(p, v[:, hs]))
    return (jnp.concatenate(outs, axis=1),)


def sum_squares(x, *, name, tr=512):
    T, D = x.shape
    tr = min(tr, T)

    def body(x_ref, o_ref):
        @pl.when(pl.program_id(0) == 0)
        def _():
            o_ref[...] = jnp.zeros_like(o_ref)

        v = x_ref[...]
        o_ref[...] += jnp.sum(jnp.sum(v * v, axis=1, keepdims=True), axis=0, keepdims=True)

    return pl.pallas_call(
        body, name=name, grid=(T // tr,), in_specs=[pl.BlockSpec((tr, D), lambda i: (i, 0))],
        out_specs=pl.BlockSpec((1, LANES), lambda i: (0, 0)), out_shape=jax.ShapeDtypeStruct((1, LANES), F32),
        compiler_params=_params(("arbitrary",)))(x)


class Comm:
    def __init__(self, arrays, out_shape, sems, start, forward, finish):
        self.arrays, self.out_shape, self.sems = list(arrays), list(out_shape), list(sems)
        self.start, self.forward, self.finish = start, forward, finish


def _place():
    x, y, c = lax.axis_index("x"), lax.axis_index("y"), lax.axis_index("c")
    chips = [(1 - x, y), (x, 1 - y), (1 - x, 1 - y)]
    return x, y, c, chips


def comm_gather(arrays):
    n = len(arrays)
    lin = lambda px, py, pc: 4 * px + 2 * py + pc

    def copy(ins, outs, sems, a, k, block, to, src=None):
        slot = outs[a].at[lin(*block)]
        return pltpu.make_async_remote_copy(src_ref=slot if src is None else src, dst_ref=slot, send_sem=sems[0].at[a, k],
                                            recv_sem=sems[1].at[a, k], device_id=to, device_id_type=MESH)

    def local(ins, outs, sems, a):
        x, y, c, _ = _place()
        return pltpu.make_async_copy(ins[a], outs[a].at[lin(x, y, c)], sems[2].at[a])

    def start(ins, outs, sems):
        x, y, c, chips = _place()
        for a in range(n):
            local(ins, outs, sems, a).start()
        for a in range(n):
            for j, chip in enumerate(chips):
                copy(ins, outs, sems, a, 1 + j, (x, y, c), (*chip, c), src=ins[a]).start()
            copy(ins, outs, sems, a, 0, (x, y, c), (x, y, 1 - c), src=ins[a]).start()

    def forward(ins, outs, sems):
        x, y, c, chips = _place()
        for a in range(n):
            for j, chip in enumerate(chips):
                copy(ins, outs, sems, a, 1 + j, (*chip, c), (x, y, c)).wait_recv()
                copy(ins, outs, sems, a, 4 + j, (*chip, c), (x, y, 1 - c)).start()

    def finish(ins, outs, sems):
        x, y, c, chips = _place()
        for a in range(n):
            copy(ins, outs, sems, a, 0, (x, y, 1 - c), (x, y, c)).wait_recv()
            for j, chip in enumerate(chips):
                copy(ins, outs, sems, a, 4 + j, (*chip, 1 - c), (x, y, c)).wait_recv()
        for a in range(n):
            for j, chip in enumerate(chips):
                copy(ins, outs, sems, a, 1 + j, (x, y, c), (*chip, c), src=ins[a]).wait_send()
                copy(ins, outs, sems, a, 4 + j, (*chip, c), (x, y, 1 - c)).wait_send()
            copy(ins, outs, sems, a, 0, (x, y, c), (x, y, 1 - c), src=ins[a]).wait_send()
            local(ins, outs, sems, a).wait()

    return Comm(arrays, [jax.ShapeDtypeStruct((N_DEV,) + a.shape, a.dtype) for a in arrays],
                [pltpu.SemaphoreType.DMA((n, 7)), pltpu.SemaphoreType.DMA((n, 7)), pltpu.SemaphoreType.DMA((n,))],
                start, forward, finish)


def comm_direct(arrays, scatter):
    n = len(arrays)

    def peers():
        x, y, c = lax.axis_index("x"), lax.axis_index("y"), lax.axis_index("c")
        out = []
        for r in range(1, N_DEV):
            px, py, pc = (1 - x if r & 4 else x), (1 - y if r & 2 else y), (1 - c if r & 1 else c)
            out.append((r, (px, py, pc), 4 * px + 2 * py + pc))
        return 4 * x + 2 * y + c, out

    def remote(ins, outs, sems, a, r, dev, src_slot, dst_slot):
        return pltpu.make_async_remote_copy(
            src_ref=ins[a].at[src_slot] if scatter[a] else ins[a], dst_ref=outs[a].at[dst_slot],
            send_sem=sems[0].at[a, r - 1], recv_sem=sems[1].at[a, r - 1], device_id=dev, device_id_type=MESH)

    def local(ins, outs, sems, a, me):
        return pltpu.make_async_copy(ins[a].at[me] if scatter[a] else ins[a], outs[a].at[me], sems[2].at[a])

    def start(ins, outs, sems):
        me, ps = peers()
        for a in range(n):
            local(ins, outs, sems, a, me).start()
        for a in range(n):
            for r, dev, lin in ps:
                remote(ins, outs, sems, a, r, dev, lin, me).start()

    def finish(ins, outs, sems):
        me, ps = peers()
        for a in range(n):
            for r, dev, lin in ps:
                remote(ins, outs, sems, a, r, dev, lin, lin).wait_recv()
        for a in range(n):
            for r, dev, lin in ps:
                remote(ins, outs, sems, a, r, dev, lin, me).wait_send()
            local(ins, outs, sems, a, me).wait()

    return Comm(arrays, [jax.ShapeDtypeStruct(a.shape if sc else (N_DEV,) + a.shape, a.dtype) for a, sc in zip(arrays, scatter)],
                [pltpu.SemaphoreType.DMA((n, N_DEV - 1)), pltpu.SemaphoreType.DMA((n, N_DEV - 1)),
                 pltpu.SemaphoreType.DMA((n,))], start, None, finish)


def comm_to_sibling(parts):
    n = len(parts)

    def copy(ins, outs, sems, a, k):
        x, y, c, _ = _place()
        return pltpu.make_async_remote_copy(src_ref=ins[a].at[2 * k + (1 - c)], dst_ref=outs[a].at[k], send_sem=sems[0].at[a, k],
                                            recv_sem=sems[1].at[a, k], device_id=(x, y, 1 - c), device_id_type=MESH)

    def start(ins, outs, sems):
        for a in range(n):
            for k in range(4):
                copy(ins, outs, sems, a, k).start()

    def finish(ins, outs, sems):
        for a in range(n):
            for k in range(4):
                copy(ins, outs, sems, a, k).wait()

    return Comm(parts, [jax.ShapeDtypeStruct((4,) + p.shape[1:], p.dtype) for p in parts],
                [pltpu.SemaphoreType.DMA((n, 4)), pltpu.SemaphoreType.DMA((n, 4))], start, None, finish)


def comm_to_owner_chip(sums):
    n = len(sums)

    def remote(ins, outs, sems, a, j, src_k, dst_k, chip):
        _, _, c, _ = _place()
        return pltpu.make_async_remote_copy(src_ref=ins[a].at[src_k], dst_ref=outs[a].at[dst_k], send_sem=sems[0].at[a, j],
                                            recv_sem=sems[1].at[a, j], device_id=(*chip, c), device_id_type=MESH)

    def local(ins, outs, sems, a):
        x, y, _, _ = _place()
        return pltpu.make_async_copy(ins[a].at[2 * x + y], outs[a].at[2 * x + y], sems[2].at[a])

    def start(ins, outs, sems):
        x, y, _, chips = _place()
        for a in range(n):
            local(ins, outs, sems, a).start()
            for j, (px, py) in enumerate(chips):
                remote(ins, outs, sems, a, j, 2 * px + py, 2 * x + y, (px, py)).start()

    def finish(ins, outs, sems):
        x, y, _, chips = _place()
        for a in range(n):
            for j, (px, py) in enumerate(chips):
                remote(ins, outs, sems, a, j, 2 * x + y, 2 * px + py, (px, py)).wait_recv()
        for a in range(n):
            for j, (px, py) in enumerate(chips):
                remote(ins, outs, sems, a, j, 2 * px + py, 2 * x + y, (px, py)).wait_send()
            local(ins, outs, sems, a).wait()

    return Comm(sums, [jax.ShapeDtypeStruct(s.shape, s.dtype) for s in sums],
                [pltpu.SemaphoreType.DMA((n, 3)), pltpu.SemaphoreType.DMA((n, 3)), pltpu.SemaphoreType.DMA((n,))],
                start, None, finish)


def pair_sum(parts, sib, *, name, tr=512):
    _, R, Cc = parts.shape
    tr = min(tr, R)
    assert R % tr == 0
    core = lax.axis_index("c").astype(jnp.int32).reshape(1)

    def body(c_ref, p_ref, s_ref, o_ref):
        o_ref[0] = (p_ref[0, 0].astype(F32) + s_ref[0].astype(F32)).astype(o_ref.dtype)

    return pl.pallas_call(
        body, name=name,
        grid_spec=pltpu.PrefetchScalarGridSpec(
            num_scalar_prefetch=1, grid=(4, R // tr),
            in_specs=[pl.BlockSpec((1, 1, tr, Cc), lambda k, i, c: (k, c[0], i, 0)),
                      pl.BlockSpec((1, tr, Cc), lambda k, i, c: (k, i, 0))],
            out_specs=pl.BlockSpec((1, tr, Cc), lambda k, i, c: (k, i, 0))),
        out_shape=jax.ShapeDtypeStruct((4, R, Cc), BF16),
        compiler_params=_params(("parallel", "parallel")))(core, parts.reshape(4, 2, R, Cc), sib)


def run_comm(comm, *, name):
    n = len(comm.arrays)

    def body(*refs):
        ins, outs, sems = refs[:n], refs[n:2 * n], refs[2 * n:]
        comm.start(ins, outs, sems)
        if comm.forward is not None:
            comm.forward(ins, outs, sems)
        comm.finish(ins, outs, sems)

    any_spec = pl.BlockSpec(memory_space=pl.ANY)
    return pl.pallas_call(body, name=name, in_specs=[any_spec] * n, out_specs=[any_spec] * n, out_shape=comm.out_shape,
                          scratch_shapes=comm.sems)(*comm.arrays)


def adamw(parts, w, m, v, *, name, tr=128):
    R, Cc = w.shape
    tr = min(tr, R)
    assert R % tr == 0
    n_parts = parts.shape[0]

    def body(p_ref, w_ref, m_ref, v_ref, g_ref, d_ref, nm_ref, nv_ref):
        g = p_ref[0].astype(F32)
        for s in range(1, n_parts):
            g = g + p_ref[s].astype(F32)
        nm = ADAM_B1 * m_ref[...] + (1.0 - ADAM_B1) * g
        nv = ADAM_B2 * v_ref[...] + (1.0 - ADAM_B2) * (g * g)
        m_hat = nm / (1.0 - ADAM_B1 ** ADAM_STEP)
        v_hat = nv / (1.0 - ADAM_B2 ** ADAM_STEP)
        g_ref[...] = g
        d_ref[...] = -ADAM_LR * (m_hat / (jnp.sqrt(v_hat) + ADAM_EPS) + ADAM_WD * w_ref[...])
        nm_ref[...] = nm
        nv_ref[...] = nv

    spec = pl.BlockSpec((tr, Cc), lambda i: (i, 0))
    return pl.pallas_call(
        body, name=name, grid=(R // tr,),
        in_specs=[pl.BlockSpec((n_parts, tr, Cc), lambda i: (0, i, 0)), spec, spec, spec], out_specs=[spec] * 4,
        out_shape=[jax.ShapeDtypeStruct((R, Cc), F32)] * 4, compiler_params=_params(("parallel",)))(parts, w, m, v)


def _lanes(vec, base):
    return jnp.pad(vec[None].astype(F32), ((0, 0), (base, LANES - base - vec.shape[0])))


def _col_shards(full_w):
    R, Ct = full_w.shape
    return jnp.transpose(full_w.reshape(R, N_DEV, Ct // N_DEV), (1, 0, 2))


def _regroup(sources, pieces_of):
    return jnp.concatenate([sources[k][:, a:a + w] for (k, a, w) in pieces_of], axis=1)


def _pieces(spans, lo, hi):
    out, pos = [], 0
    for (k, a, w) in spans:
        s, e = max(lo, pos), min(hi, pos + w)
        if s < e:
            out.append((k, a + s - pos, e - s))
        pos += w
    return out


def _from_col_shards(g):
    return jnp.transpose(g, (1, 0, 2)).reshape(g.shape[1], -1)


def kernel(x, mem, g_mix, w_in, conv_w, a_log, dt_bias, gdn_norm_g, fox_b_f, fox_q_norm, fox_k_norm, g_mem, w_mem_kv, mem_q_norm, mem_k_norm, w_up_gdn, w_up_fox, w_up_mem, w_out, g_mlp, w_ff1, w_ff2, loss_target, m_g_mix, m_w_in, m_conv_w, m_a_log, m_dt_bias, m_gdn_norm_g, m_fox_b_f, m_fox_q_norm, m_fox_k_norm, m_g_mem, m_w_mem_kv, m_mem_q_norm, m_mem_k_norm, m_w_up_gdn, m_w_up_fox, m_w_up_mem, m_w_out, m_g_mlp, m_w_ff1, m_w_ff2, v_g_mix, v_w_in, v_conv_w, v_a_log, v_dt_bias, v_gdn_norm_g, v_fox_b_f, v_fox_q_norm, v_fox_k_norm, v_g_mem, v_w_mem_kv, v_mem_q_norm, v_mem_k_norm, v_w_up_gdn, v_w_up_fox, v_w_up_mem, v_w_out, v_g_mlp, v_w_ff1, v_w_ff2):
    loc = dict(locals())
    big_names = ["w_in", "conv_w", "w_mem_kv", "w_up_gdn", "w_up_fox", "w_up_mem", "w_out", "w_ff1", "w_ff2"]
    col_sharded = {"w_in", "conv_w", "w_up_gdn", "w_up_fox", "w_up_mem", "w_ff1"}
    small_names = ["g_mix", "a_log", "dt_bias", "gdn_norm_g", "fox_b_f", "fox_q_norm", "fox_k_norm", "g_mem",
                   "mem_q_norm", "mem_k_norm", "g_mlp"]

    xs, tgt, mems = x[0], loss_target[0], mem[0]
    T, D = xs.shape
    HG = a_log.shape[1]
    HF = fox_b_f.shape[1]
    DM = mem_q_norm.shape[1]
    GQK, GV = HG * HEAD, HG * HEAD
    GQKV = 2 * GQK + GV
    FW = HF * HEAD
    MW = w_mem_kv.shape[2] // 2
    HM = MW // DM
    assert HG <= 8 and HF <= 8

    shard = {n: loc[n][0].astype(BF16) for n in big_names}
    first, rest, last = big_names[:2], big_names[2:-1], big_names[-1:]
    W = {}

    def take(names, gathered):
        for n, g in zip(names, gathered):
            if n in ("w_ff1", "w_in"):
                W[n] = g
            else:
                W[n] = _from_col_shards(g) if n in col_sharded else g.reshape(-1, g.shape[2])

    rms_fn = lambda t, g: (_rms(t, g),)
    h, *gathered = rowwise(rms_fn, [full(xs)], [g_mix], [(D, D, BF16)], name="rms_mix", comm=comm_gather([shard[n] for n in first]))
    take(first, gathered)
    widths = [GQKV, GV, HG, HG, FW, FW, FW, HF, MW, 3 * D]
    offs = np.concatenate([[0], np.cumsum(widths)]).tolist()
    cs = W["w_in"].shape[2]
    shard_spans = [(d, 0, cs) for d in range(N_DEV)]
    in_shards = [W["w_in"][d] for d in range(N_DEV)]
    seg = lambda i: _regroup(in_shards, _pieces(shard_spans, offs[i], offs[i + 1]))
    w_big = jnp.concatenate([seg(0), seg(1), seg(4), seg(5), seg(6), seg(8), seg(9)], axis=1)
    pad8 = lambda s: jnp.pad(s, ((0, 0), (0, 8 - s.shape[1])))
    w_small = jnp.concatenate([pad8(seg(2)), pad8(seg(3)), pad8(seg(7)), jnp.zeros((D, LANES - 24), BF16)], axis=1)
    o_z, o_fq, o_fk, o_fv = GQKV, GQKV + GV, GQKV + GV + FW, GQKV + GV + 2 * FW
    o_mq = o_fv + FW
    o_gt = o_mq + MW
    WB = o_gt + 3 * D
    conv_full = W["conv_w"].astype(F32)

    a_log_l, dt_bias_l, b_f_l = _lanes(a_log[0], LANE_A), _lanes(dt_bias[0], LANE_A), _lanes(fox_b_f[0], LANE_F)

    big, *gathered = matmul(h, w_big, mode="nn", name="proj_big", tm=2048, comm=comm_gather([shard[n] for n in rest]))
    take(rest, gathered)
    small = matmul(h, w_small, mode="nn", name="proj_small", out_dtypes=(F32,))
    bgf, = rowwise(_gates_fn, [full(small)], [a_log_l, dt_bias_l, b_f_l], [(LANES, LANES, F32)], name="gates")

    cn = conv_fwd(big, conv_full, HG, name="conv")
    o_gdn, ss, tinv = gdn_fwd(cn, bgf, HG, name="gdn_fwd")
    oa, = rowwise(_gated_norm_fn, [(o_gdn, HEAD, 0, 1), (big, HEAD, o_z // HEAD, 1)], [gdn_norm_g],
                  [(GV, HEAD, BF16)], name="gated_norm", ncb=HG)

    cum = cumsum_tokens(bgf, reverse=False, name="cumsum")
    fqa = fox_prep(big, o_fq, fox_q_norm, cum, 0, HF, name="fox_prep_q")
    fka = fox_prep(big, o_fk, fox_k_norm, cum, 1, HF, name="fox_prep_k")
    fva = fox_prep(big, o_fv, fox_k_norm, cum, 2, HF, name="fox_prep_v")
    live = fox_live_tiles(cum, fox_q_norm, fox_k_norm, HF, T)
    ob, lse = fox_fwd(fqa, fka, fva, live, HF, name="fox_fwd")

    memn, = rowwise(rms_fn, [full(mems)], [g_mem], [(D, D, BF16)], name="rms_mem")
    kv_m = matmul(memn, W["w_mem_kv"], mode="nn", name="mem_kv", out_dtypes=(F32,))
    kmn, = rowwise(rms_fn, [(kv_m, DM, 0, 1)], [mem_k_norm], [(MW, DM, F32)], name="mem_knorm", ncb=HM)
    vm = kv_m[:, MW:]
    mem_fn = functools.partial(_mem_attn_fn, HM, DM)
    om, = rowwise(mem_fn, [(big, MW, o_mq // MW, 0)], [kmn, vm, mem_q_norm], [(MW, MW, BF16)], name="mem_attn")

    ya = matmul(oa, W["w_up_gdn"], mode="nn", name="up_gdn")
    yb = matmul(ob, W["w_up_fox"], mode="nn", name="up_fox")
    ym = matmul(om, W["w_up_mem"], mode="nn", name="up_mem")
    cbm = min(512, D)
    gate_rows = [(big, cbm, (o_gt + b * D) // cbm, 1) for b in range(3)]
    merge_rows = [(ya, cbm, 0, 1), (yb, cbm, 0, 1), (ym, cbm, 0, 1)] + gate_rows
    y, = rowwise(_merge_fn, merge_rows, [], [(D, cbm, BF16)], name="merge", ncb=D // cbm)
    x1 = matmul(y, W["w_out"], mode="nn", name="out_proj", out_dtypes=(F32,), extras=(xs,),
                epi=lambda r, res: (r + res,))

    rms_halves = lambda a, b, g: (_rms(jnp.concatenate([a, b], axis=1), g),)
    h2, = rowwise(rms_halves, [(x1, D // 2, 0, 0), (x1, D // 2, 1, 0)], [g_mlp], [(D, D, BF16)], name="rms_mlp")
    u_ff, a_ff, *gathered = matmul(h2, W["w_ff1"], mode="nn", name="ff1", out_dtypes=(BF16, BF16), b_shards=True,
                                   epi=lambda r: (r, jnp.square(jnp.maximum(r, 0.0))),
                                   comm=comm_gather([shard[n] for n in last]))
    take(last, gathered)
    d_out = matmul(a_ff, W["w_ff2"], mode="nn", name="ff2_loss", out_dtypes=(F32,), extras=(x1, tgt),
                   epi=lambda r, res, t: ((r + res - t) * (1.0 / D),))
    loss_local = 0.5 * D * sum_squares(d_out, name="loss_sum")[0, 0]
    loss = lax.psum(loss_local, ("x", "y", "c"))

    G = {}
    d_u = matmul(d_out, W["w_ff2"], mode="nt", name="d_ff2_in", extras=(u_ff,),
                 epi=lambda r, u: (r * 2.0 * jnp.maximum(u.astype(F32), 0.0),))
    d_h2 = matmul(d_u, W["w_ff1"], mode="nt", name="d_ff1_in", b_shards=True)
    d_x1, G["g_mlp"] = rowwise_bwd(rms_fn, [full(x1)], [g_mlp], [full(d_h2)], [(0, F32)], [0], name="d_rms_mlp",
                                   adds=[full(d_out)])
    d_y = matmul(d_x1, W["w_out"], mode="nt", name="d_out_proj_in")
    G["w_out"] = matmul(y, d_x1, mode="tn", name="d_w_out")
    d_ya, d_yb, d_ym, d_ga, d_gb, d_gm = rowwise_bwd(
        _merge_fn, merge_rows, [], [(d_y, cbm, 0, 1)], [(k, BF16) for k in range(6)], [], name="d_merge", ncb=D // cbm)
    d_oa = matmul(d_ya, W["w_up_gdn"], mode="nt", name="d_up_gdn_in")
    d_ob = matmul(d_yb, W["w_up_fox"], mode="nt", name="d_up_fox_in")
    d_om = matmul(d_ym, W["w_up_mem"], mode="nt", name="d_up_mem_in")
    G["w_up_gdn"] = matmul(oa, d_ya, mode="tn", name="d_w_up_gdn")
    G["w_up_fox"] = matmul(ob, d_yb, mode="tn", name="d_w_up_fox")
    G["w_up_mem"] = matmul(om, d_ym, mode="tn", name="d_w_up_mem")

    d_mq, d_kmn, d_vm, G["mem_q_norm"] = rowwise_bwd(
        mem_fn, [(big, MW, o_mq // MW, 0)], [kmn, vm, mem_q_norm], [full(d_om)], [(0, BF16)], [0, 1, 2], name="d_mem_attn")
    d_km, G["mem_k_norm"] = rowwise_bwd(rms_fn, [(kv_m, DM, 0, 1)], [mem_k_norm], [(d_kmn, DM, 0, 1)], [(0, F32)], [0],
                                         name="d_mem_knorm", ncb=HM)
    d_kv_m = jnp.concatenate([d_km, d_vm], axis=1)
    G["w_mem_kv"] = matmul(memn, d_kv_m, mode="tn", name="d_w_mem_kv")
    d_memn = matmul(d_kv_m, W["w_mem_kv"], mode="nt", name="d_mem_kv_in")
    _, G["g_mem"] = rowwise_bwd(rms_fn, [full(mems)], [g_mem], [full(d_memn)], [(0, BF16)], [0], name="d_rms_mem")

    d_fkn, d_fv, d_cum_t = fox_bwd_kv(fqa, fka, fva, d_ob, ob, lse, live, HF, name="fox_bwd_kv")
    d_fqn = fox_bwd_q(fqa, fka, fva, d_ob, ob, lse, live, HF, name="fox_bwd_q")
    d_fq, G["fox_q_norm"] = rowwise_bwd(rms_fn, [(big, HEAD, o_fq // HEAD, 1)], [fox_q_norm], [(d_fqn, HEAD, 0, 1)],
                                         [(0, BF16)], [0], name="d_fox_qnorm", ncb=HF)
    d_fk, G["fox_k_norm"] = rowwise_bwd(rms_fn, [(big, HEAD, o_fk // HEAD, 1)], [fox_k_norm], [(d_fkn, HEAD, 0, 1)],
                                         [(0, BF16)], [0], name="d_fox_knorm", ncb=HF)
    d_cum = jnp.pad(d_cum_t[:HF].T, ((0, 0), (LANE_F, LANES - LANE_F - HF)))
    d_logf = cumsum_tokens(d_cum, reverse=True, name="cumsum_rev")

    d_o_gdn, d_z, G["gdn_norm_g"] = rowwise_bwd(
        _gated_norm_fn, [(o_gdn, HEAD, 0, 1), (big, HEAD, o_z // HEAD, 1)], [gdn_norm_g], [(d_oa, HEAD, 0, 1)],
        [(0, F32), (1, BF16)], [0], name="d_gated_norm", ncb=HG)
    d_cn, d_bg = gdn_bwd(cn, bgf, ss, tinv, d_o_gdn, HG, name="gdn_bwd")
    d_conv_y, G["conv_w"] = conv_bwd_taps(big, conv_full, d_cn, HG, name="d_conv_taps")
    d_qkv = conv_bwd_input(d_conv_y, conv_full, name="d_conv_in")
    d_small, d_al, d_dt, d_bf = rowwise_bwd(_gates_fn, [full(small)], [a_log_l, dt_bias_l, b_f_l], [full(d_bg + d_logf)],
                                            [(0, F32)], [0, 1, 2], name="d_gates")
    G["a_log"], G["dt_bias"], G["fox_b_f"] = (d_al[:, LANE_A:LANE_A + HG], d_dt[:, LANE_A:LANE_A + HG],
                                               d_bf[:, LANE_F:LANE_F + HF])

    def parts(n):
        g = G[n].astype(BF16)
        if g.ndim == 3:
            return g
        return _col_shards(g) if n in col_sharded else g.reshape(N_DEV, -1, g.shape[1])

    recv = {}

    def carried(names, out):
        for n, r in zip(names, out):
            recv[n] = r

    d_big = jnp.concatenate([d_qkv, d_z, d_fq, d_fk, d_fv, d_mq, d_ga, d_gb, d_gm], axis=1)
    group = ["conv_w", "w_mem_kv", "w_up_gdn", "w_up_fox", "w_up_mem", "w_out"]
    G["w_ff2"], *out = matmul(a_ff, d_out, mode="tn", name="d_w_ff2",
                              comm=comm_direct([parts(n) for n in group], [True] * len(group)))
    carried(group, out)
    G["w_ff1"], *out = matmul(h2, d_u, mode="tn", name="d_w_ff1", out_shards=True,
                              comm=comm_direct([parts("w_ff2")], [True]))
    carried(["w_ff2"], out)
    g_big, *out = matmul(h, d_big, mode="tn", name="d_w_big", tm=2048, comm=comm_direct([parts("w_ff1")], [True]))
    carried(["w_ff1"], out)
    g_small = matmul(h, d_small, mode="tn", name="d_w_small", out_dtypes=(F32,))
    grad_spans = [(0, 0, GQKV), (0, o_z, GV), (1, LANE_B, HG), (1, LANE_A, HG), (0, o_fq, FW), (0, o_fk, FW), (0, o_fv, FW),
                  (1, LANE_F, HF), (0, o_mq, MW), (0, o_gt, 3 * D)]
    grad_src = [g_big, g_small.astype(BF16)]
    G["w_in"] = jnp.stack([_regroup(grad_src, _pieces(grad_spans, d * cs, (d + 1) * cs)) for d in range(N_DEV)])
    p_in = parts("w_in")
    d_h_s, from_sibling = matmul(d_small, w_small, mode="nt", name="d_proj_small_in", out_dtypes=(F32,),
                                 comm=comm_to_sibling([p_in]))
    chip_sums = pair_sum(p_in, from_sibling, name="w_in_pair_sum")
    d_h, *out = matmul(d_big, w_big, mode="nt", name="d_proj_big_in", extras=(d_h_s,), epi=lambda r, e: (r + e,),
                       comm=comm_to_owner_chip([chip_sums]))
    carried(["w_in"], out)
    grad_x, G["g_mix"] = rowwise_bwd(rms_fn, [full(xs)], [g_mix], [full(d_h)], [(0, F32)], [0], name="d_rms_mix",
                                     adds=[full(d_x1)])
    grad_x = grad_x[None]

    small_sizes = [loc[n].shape[1] for n in small_names]
    pack = lambda d: jnp.concatenate([d[n].reshape(1, -1) for n in small_names], axis=1)
    npad = -sum(small_sizes) % LANES
    padp = lambda a: jnp.pad(a, ((0, 0), (0, npad)))
    recv_small, = run_comm(comm_direct([padp(pack(G))], [False]), name="gather_small_grads")

    res = {}
    for n in big_names:
        res[n] = [t[None] for t in adamw(recv[n], loc[n][0], loc["m_" + n][0], loc["v_" + n][0], name="adamw_" + n)]
    sm = adamw(recv_small, padp(pack({n: loc[n] for n in small_names})), padp(pack({n: loc["m_" + n] for n in small_names})),
               padp(pack({n: loc["v_" + n] for n in small_names})), name="adamw_small")
    so = np.concatenate([[0], np.cumsum(small_sizes)]).tolist()
    for i, n in enumerate(small_names):
        res[n] = [t[:, so[i]:so[i + 1]] for t in sm]

    order = ["g_mix", "w_in", "conv_w", "a_log", "dt_bias", "gdn_norm_g", "fox_b_f", "fox_q_norm", "fox_k_norm", "g_mem",
             "w_mem_kv", "mem_q_norm", "mem_k_norm", "w_up_gdn", "w_up_fox", "w_up_mem", "w_out", "g_mlp", "w_ff1", "w_ff2"]
    return (loss, grad_x, *[res[n][0] for n in order], *[res[n][1] for n in order],
            *[res[n][2] for n in order], *[res[n][3] for n in order])
```
